```python
import math
import jax
import jax.numpy as jnp
from jax import lax
import numpy as np

D_MODEL = 1024
BATCH = 2
SEQ = 8192
DEPTH = 2
DEC_BATCH = 128
DEC_SEQ = 4
PAST_LEN = 8192
PAGE_SIZE = 128

N_EVEN = (DEPTH + 1) // 2
N_ODD = DEPTH // 2
NORM_EPS = 1e-6

S5_WIDTH = D_MODEL // 2
S5_GROUP = 16
S5_GROUPS = S5_WIDTH // S5_GROUP
S5_STATE = 64
S5_DT_MIN = 1e-3
S5_DT_MAX = 1e-1

GLA_HEADS = 4
GLA_V_WIDTH = D_MODEL // 2
GLA_K_WIDTH = GLA_V_WIDTH // 2
GLA_HEAD_K = GLA_K_WIDTH // GLA_HEADS
GLA_HEAD_V = GLA_V_WIDTH // GLA_HEADS
GLA_GATE_RANK = 16
GLA_GATE_TAU = 16.0
GLA_CHUNK = 64

EVEN_COLS = (S5_WIDTH, S5_WIDTH, GLA_K_WIDTH, GLA_K_WIDTH, GLA_V_WIDTH, GLA_GATE_RANK, GLA_V_WIDTH)
EVEN_IN = sum(EVEN_COLS)
EVEN_SPLIT = tuple(np.cumsum(EVEN_COLS[:-1]).tolist())
EVEN_MIX = S5_WIDTH + GLA_V_WIDTH

SWA_HEADS = 16
SWA_KV_HEADS = 2
SWA_GROUP = SWA_HEADS // SWA_KV_HEADS
SWA_HEAD_DIM = 64
SWA_WINDOW = 128
ROPE_THETA = 10000.0
SWA_Q_WIDTH = SWA_HEADS * SWA_HEAD_DIM
SWA_KV_WIDTH = SWA_KV_HEADS * SWA_HEAD_DIM
ODD_COLS = (SWA_Q_WIDTH, SWA_KV_WIDTH, SWA_KV_WIDTH, SWA_Q_WIDTH)
ODD_IN = sum(ODD_COLS)
ODD_SPLIT = tuple(np.cumsum(ODD_COLS[:-1]).tolist())

kernel_name = 'hybrid_s5_gla_swa_sink_step'


def rms_norm(x, g):
    xf = x.astype(jnp.float32)
    y = xf * lax.rsqrt(jnp.mean(xf * xf, axis=-1, keepdims=True) + NORM_EPS)
    return (y * g.astype(jnp.float32)).astype(x.dtype)


def s5_discretize(lam_re, lam_im, b_re, b_im, log_dt):
    lam = lax.complex(lam_re.astype(jnp.float32), lam_im.astype(jnp.float32))
    dt = jnp.exp(log_dt.astype(jnp.float32))[:, None]
    lam_bar = jnp.exp(lam * dt)
    b = lax.complex(b_re.astype(jnp.float32), b_im.astype(jnp.float32))
    b_bar = ((lam_bar - 1.0) / lam)[..., None] * b
    return lam_bar, b_bar


def _linear_combine(e1, e2):
    a1, b1 = e1
    a2, b2 = e2
    return a2 * a1, a2 * b1 + b2


def s5_scan(u, state0, lam_bar, b_bar, c_re, c_im, d):
    n, l, _ = u.shape
    ug = u.reshape(n, l, S5_GROUPS, S5_GROUP).astype(jnp.complex64)
    bu = jnp.einsum('gpc,nlgc->nlgp', b_bar, ug)
    bu = bu.at[:, 0].add(lam_bar * state0)
    a = jnp.broadcast_to(lam_bar, bu.shape)
    _, xs = lax.associative_scan(_linear_combine, (a, bu), axis=1)
    c = lax.complex(c_re.astype(jnp.float32), c_im.astype(jnp.float32))
    y = jnp.einsum('gcp,nlgp->nlgc', c, xs).real.reshape(n, l, S5_WIDTH) + d.astype(jnp.float32) * u
    return y, xs[:, -1]


def gla_chunked(q, k, v, log_a, s0):
    n, l = q.shape[:2]
    c = math.gcd(l, GLA_CHUNK)
    nc = l // c

    def to_chunks(t):
        return t.reshape(n, nc, c, GLA_HEADS, t.shape[-1]).transpose(1, 0, 3, 2, 4)

    qc, kc, vc, ac = to_chunks(q), to_chunks(k), to_chunks(v), to_chunks(log_a)
    causal = jnp.tril(jnp.ones((c, c), dtype=bool))

    def step(s, inp):
        qi, ki, vi, ai = inp
        b = jnp.cumsum(ai, axis=-2)
        b_last = b[..., -1:, :]
        q_dec = qi * jnp.exp(b)
        k_dec = ki * jnp.exp(-b)
        att = jnp.where(causal, jnp.einsum('nhik,nhjk->nhij', q_dec, k_dec), 0.0)
        o = jnp.einsum('nhij,nhjv->nhiv', att, vi) + jnp.einsum('nhik,nhkv->nhiv', q_dec, s)
        k_tail = ki * jnp.exp(b_last - b)
        s_new = jnp.exp(b_last[..., 0, :])[..., None] * s + jnp.einsum('nhjk,nhjv->nhkv', k_tail, vi)
        return s_new, o

    s_final, oc = lax.scan(step, s0, (qc, kc, vc, ac))
    o = oc.transpose(1, 0, 3, 2, 4).reshape(n, l, GLA_HEADS, GLA_HEAD_V)
    return o, s_final


def even_layer(x, s5_state0, gla_state0, p):
    n, l, _ = x.shape
    proj = (rms_norm(x, p['norm_g']) @ p['w_in']).astype(jnp.float32)
    u, gate_a, q, k, v, a_low, gate_b = jnp.split(proj, EVEN_SPLIT, axis=-1)
    lam_bar, b_bar = s5_discretize(p['lam_re'], p['lam_im'], p['b_re'], p['b_im'], p['log_dt'])
    y, s5_last = s5_scan(u, s5_state0, lam_bar, b_bar, p['c_re'], p['c_im'], p['d'])
    z = jax.nn.gelu(y)
    z = z * jax.nn.sigmoid(z @ p['w_glu'].astype(jnp.float32) + p['b_glu'].astype(jnp.float32))
    out_a = z * jax.nn.silu(gate_a)
    q = q.reshape(n, l, GLA_HEADS, GLA_HEAD_K) * (GLA_HEAD_K ** -0.5)
    k = k.reshape(n, l, GLA_HEADS, GLA_HEAD_K)
    v = v.reshape(n, l, GLA_HEADS, GLA_HEAD_V)
    gate_logit = a_low @ p['w_gate'].astype(jnp.float32) + p['b_gate'].astype(jnp.float32)
    log_a = (jax.nn.log_sigmoid(gate_logit) / GLA_GATE_TAU).reshape(n, l, GLA_HEADS, GLA_HEAD_K)
    o, gla_last = gla_chunked(q, k, v, log_a, gla_state0)
    out_b = rms_norm(o, p['gla_norm_g']).reshape(n, l, GLA_V_WIDTH) * jax.nn.silu(gate_b)
    mix = jnp.concatenate([out_a, out_b], axis=-1).astype(x.dtype) @ p['w_out']
    return x + mix, s5_last, gla_last


def rope(x, pos):
    half = SWA_HEAD_DIM // 2
    inv_freq = ROPE_THETA ** (-jnp.arange(half, dtype=jnp.float32) / half)
    ang = pos.astype(jnp.float32)[:, None] * inv_freq[None, :]
    cos = jnp.cos(ang)[:, None, :]
    sin = jnp.sin(ang)[:, None, :]
    x = x.astype(jnp.float32)
    x1, x2 = x[..., :half], x[..., half:]
    return jnp.concatenate([x1 * cos - x2 * sin, x2 * cos + x1 * sin], axis=-1)


def sink_attention(q, k, v, q_pos, k_pos, sinks):
    s = jnp.einsum('...qhgd,...khd->...hgqk', q, k) * (SWA_HEAD_DIM ** -0.5)
    diff = q_pos[..., :, None] - k_pos[..., None, :]
    allowed = (diff >= 0) & (diff < SWA_WINDOW) & (k_pos[..., None, :] >= 0)
    s = jnp.where(allowed[..., None, None, :, :], s, -jnp.inf)
    sink = sinks.astype(jnp.float32).reshape(SWA_KV_HEADS, SWA_GROUP, 1, 1)
    m = jnp.maximum(jnp.max(s, axis=-1, keepdims=True), sink)
    pr = jnp.exp(s - m)
    denom = jnp.sum(pr, axis=-1, keepdims=True) + jnp.exp(sink - m)
    return jnp.einsum('...hgqk,...khd->...qhgd', pr / denom, v)


def swa_project(x, pos, p):
    n, l, _ = x.shape
    proj = (rms_norm(x, p['norm_g']) @ p['w_in']).astype(jnp.float32)
    q, k, v, gate = jnp.split(proj, ODD_SPLIT, axis=-1)
    q = rope(rms_norm(q.reshape(n, l, SWA_HEADS, SWA_HEAD_DIM), p['q_norm_g']), pos)
    k = rope(rms_norm(k.reshape(n, l, SWA_KV_HEADS, SWA_HEAD_DIM), p['k_norm_g']), pos)
    v = v.reshape(n, l, SWA_KV_HEADS, SWA_HEAD_DIM)
    return q, k, v, gate


def swa_output(x, o, gate, p):
    n, l, _ = x.shape
    o = o.reshape(n, l, SWA_Q_WIDTH) * jax.nn.silu(gate)
    return x + o.astype(x.dtype) @ p['w_out']


def odd_layer_prompt(x, p):
    n, l, _ = x.shape
    pos = jnp.arange(l, dtype=jnp.int32)
    q, k, v, gate = swa_project(x, pos, p)
    nb = l // SWA_WINDOW
    qb = q.reshape(n, nb, SWA_WINDOW, SWA_KV_HEADS, SWA_GROUP, SWA_HEAD_DIM)

    def band(t):
        tb = t.reshape(n, nb, SWA_WINDOW, SWA_KV_HEADS, SWA_HEAD_DIM)
        prev = jnp.pad(tb, ((0, 0), (1, 0), (0, 0), (0, 0), (0, 0)))[:, :-1]
        return jnp.concatenate([prev, tb], axis=2)

    pos_b = pos.reshape(nb, SWA_WINDOW)
    k_pos = jnp.concatenate([pos_b - SWA_WINDOW, pos_b], axis=1)
    o = sink_attention(qb, band(k), band(v), pos_b, k_pos, p['sinks'])
    cache_len = min(SWA_WINDOW, l)
    return swa_output(x, o, gate, p), k[:, -cache_len:], v[:, -cache_len:]


def odd_layer_sample(x, k_cache, v_cache, p):
    n, l, _ = x.shape
    c = k_cache.shape[1]
    pos = PAST_LEN + jnp.arange(l, dtype=jnp.int32)
    q, k, v, gate = swa_project(x, pos, p)
    kk = jnp.concatenate([k_cache.astype(jnp.float32), k], axis=1)
    vv = jnp.concatenate([v_cache.astype(jnp.float32), v], axis=1)
    k_pos = (PAST_LEN - c) + jnp.arange(c + l, dtype=jnp.int32)
    qg = q.reshape(n, l, SWA_KV_HEADS, SWA_GROUP, SWA_HEAD_DIM)
    o = sink_attention(qg, kk, vv, pos, k_pos, p['sinks'])
    return swa_output(x, o, gate, p), kk[:, -c:], vv[:, -c:]


def setup_inputs(seed: int = 0) -> dict:
    key = jax.random.key(seed)
    ks = iter(jax.random.split(key, 40))

    def nrm(shape, scale):
        return scale * jax.random.normal(next(ks), shape, jnp.float32)

    swa_cache = min(SWA_WINDOW, PAST_LEN)
    lam_re = -0.5 + nrm((N_EVEN, S5_GROUPS, S5_STATE), 0.01)
    lam_im = jnp.pi * jnp.arange(S5_STATE, dtype=jnp.float32) + nrm((N_EVEN, S5_GROUPS, S5_STATE), 0.01)
    log_dt = jax.random.uniform(next(ks), (N_EVEN, S5_GROUPS), jnp.float32,
                                minval=math.log(S5_DT_MIN), maxval=math.log(S5_DT_MAX))
    return {
        'x_prompt': nrm((BATCH, SEQ, D_MODEL), 1.0),
        'x_sample': nrm((DEC_BATCH, DEC_SEQ, D_MODEL), 1.0),
        'state_s5_re': nrm((N_EVEN, DEC_BATCH, S5_GROUPS, S5_STATE), 0.1),
        'state_s5_im': nrm((N_EVEN, DEC_BATCH, S5_GROUPS, S5_STATE), 0.1),
        'state_gla': nrm((N_EVEN, DEC_BATCH, GLA_HEADS, GLA_HEAD_K, GLA_HEAD_V), 0.5),
        'cache_swa_k': nrm((N_ODD, DEC_BATCH, swa_cache, SWA_KV_HEADS, SWA_HEAD_DIM), 1.0),
        'cache_swa_v': nrm((N_ODD, DEC_BATCH, swa_cache, SWA_KV_HEADS, SWA_HEAD_DIM), 1.0),
        'even_norm_g': 1.0 + nrm((N_EVEN, D_MODEL), 0.01),
        'even_w_in': nrm((N_EVEN, D_MODEL, EVEN_IN), D_MODEL ** -0.5),
        's5_lambda_re': lam_re,
        's5_lambda_im': lam_im,
        's5_log_dt': log_dt,
        's5_b_re': nrm((N_EVEN, S5_GROUPS, S5_STATE, S5_GROUP), (2 * S5_GROUP) ** -0.5),
        's5_b_im': nrm((N_EVEN, S5_GROUPS, S5_STATE, S5_GROUP), (2 * S5_GROUP) ** -0.5),
        's5_c_re': nrm((N_EVEN, S5_GROUPS, S5_GROUP, S5_STATE), (2 * S5_STATE) ** -0.5),
        's5_c_im': nrm((N_EVEN, S5_GROUPS, S5_GROUP, S5_STATE), (2 * S5_STATE) ** -0.5),
        's5_d': nrm((N_EVEN, S5_WIDTH), 1.0),
        's5_w_glu': nrm((N_EVEN, S5_WIDTH, S5_WIDTH), S5_WIDTH ** -0.5),
        's5_b_glu': nrm((N_EVEN, S5_WIDTH), 0.01),
        'gla_w_gate': nrm((N_EVEN, GLA_GATE_RANK, GLA_K_WIDTH), GLA_GATE_RANK ** -0.5),
        'gla_b_gate': nrm((N_EVEN, GLA_K_WIDTH), 0.01),
        'gla_norm_g': 1.0 + nrm((N_EVEN, GLA_HEAD_V), 0.01),
        'even_w_out': nrm((N_EVEN, EVEN_MIX, D_MODEL), EVEN_MIX ** -0.5),
        'odd_norm_g': 1.0 + nrm((N_ODD, D_MODEL), 0.01),
        'odd_w_in': nrm((N_ODD, D_MODEL, ODD_IN), D_MODEL ** -0.5),
        'swa_q_norm_g': 1.0 + nrm((N_ODD, SWA_HEAD_DIM), 0.01),
        'swa_k_norm_g': 1.0 + nrm((N_ODD, SWA_HEAD_DIM), 0.01),
        'swa_sinks': nrm((N_ODD, SWA_HEADS), 0.5),
        'odd_w_out': nrm((N_ODD, SWA_Q_WIDTH, D_MODEL), SWA_Q_WIDTH ** -0.5),
    }


def reference(x_prompt, x_sample, state_s5_re, state_s5_im, state_gla, cache_swa_k, cache_swa_v,
              even_norm_g, even_w_in, s5_lambda_re, s5_lambda_im, s5_log_dt, s5_b_re, s5_b_im,
              s5_c_re, s5_c_im, s5_d, s5_w_glu, s5_b_glu, gla_w_gate, gla_b_gate, gla_norm_g,
              even_w_out, odd_norm_g, odd_w_in, swa_q_norm_g, swa_k_norm_g, swa_sinks, odd_w_out):
    hp, hs = x_prompt, x_sample
    s5r_p, s5i_p, gla_p, swk_p, swv_p = [], [], [], [], []
    s5r_s, s5i_s, gla_s, swk_s, swv_s = [], [], [], [], []
    for layer in range(DEPTH):
        i = layer // 2
        if layer % 2 == 0:
            p = {'norm_g': even_norm_g[i], 'w_in': even_w_in[i], 'lam_re': s5_lambda_re[i],
                 'lam_im': s5_lambda_im[i], 'log_dt': s5_log_dt[i], 'b_re': s5_b_re[i], 'b_im': s5_b_im[i],
                 'c_re': s5_c_re[i], 'c_im': s5_c_im[i], 'd': s5_d[i], 'w_glu': s5_w_glu[i],
                 'b_glu': s5_b_glu[i], 'w_gate': gla_w_gate[i], 'b_gate': gla_b_gate[i],
                 'gla_norm_g': gla_norm_g[i], 'w_out': even_w_out[i]}
            nb = hp.shape[0]
            s5_zero = jnp.zeros((nb, S5_GROUPS, S5_STATE), jnp.complex64)
            gla_zero = jnp.zeros((nb, GLA_HEADS, GLA_HEAD_K, GLA_HEAD_V), jnp.float32)
            hp, s5_last_p, gla_last_p = even_layer(hp, s5_zero, gla_zero, p)
            s5_init = lax.complex(state_s5_re[i].astype(jnp.float32), state_s5_im[i].astype(jnp.float32))
            hs, s5_last_s, gla_last_s = even_layer(hs, s5_init, state_gla[i].astype(jnp.float32), p)
            s5r_p.append(s5_last_p.real)
            s5i_p.append(s5_last_p.imag)
            gla_p.append(gla_last_p)
            s5r_s.append(s5_last_s.real)
            s5i_s.append(s5_last_s.imag)
            gla_s.append(gla_last_s)
        else:
            p = {'norm_g': odd_norm_g[i], 'w_in': odd_w_in[i], 'q_norm_g': swa_q_norm_g[i],
                 'k_norm_g': swa_k_norm_g[i], 'sinks': swa_sinks[i], 'w_out': odd_w_out[i]}
            hp, k_p, v_p = odd_layer_prompt(hp, p)
            hs, k_s, v_s = odd_layer_sample(hs, cache_swa_k[i], cache_swa_v[i], p)
            swk_p.append(k_p)
            swv_p.append(v_p)
            swk_s.append(k_s)
            swv_s.append(v_s)
    return (hp, hs,
            jnp.stack(s5r_p), jnp.stack(s5i_p), jnp.stack(gla_p), jnp.stack(swk_p), jnp.stack(swv_p),
            jnp.stack(s5r_s), jnp.stack(s5i_s), jnp.stack(gla_s), jnp.stack(swk_s), jnp.stack(swv_s))
```

```python
import functools
import math

import jax
import jax.numpy as jnp
from jax import lax
from jax.experimental import pallas as pl
from jax.experimental.pallas import tpu as pltpu

F32 = jnp.float32
BF16 = jnp.bfloat16

PAST_LEN = 8192
NORM_EPS = 1e-6
S5_GROUP = 16
S5_STATE = 64
S5_CHUNK = 16
GLA_HEADS = 4
GLA_HEAD_K = 64
GLA_HEAD_V = 128
GLA_GATE_RANK = 16
GLA_GATE_TAU = 16.0
GLA_CHUNK = 64
SWA_HEADS = 16
SWA_KV_HEADS = 2
SWA_GROUP = SWA_HEADS // SWA_KV_HEADS
SWA_HEAD_DIM = 64
SWA_WINDOW = 128
ROPE_THETA = 10000.0
LANES = 128
ROW_TILE = 512
VMEM_LIMIT = 48 * 1024 * 1024

HIGHEST = lax.Precision.HIGHEST


def _cparams(sem):
    return pltpu.CompilerParams(dimension_semantics=sem, vmem_limit_bytes=VMEM_LIMIT)


def _full(shape):
    n = len(shape)
    return pl.BlockSpec(shape, lambda *_: (0,) * n)


def _dot(a, b):
    return jnp.dot(a, b, preferred_element_type=F32)


def _dot_nt(a, b):
    return lax.dot_general(a, b, (((1,), (1,)), ((), ())), preferred_element_type=F32)


def _dot_tn(a, b):
    return lax.dot_general(a, b, (((0,), (0,)), ((), ())), preferred_element_type=F32)


def _split_bf16(x):
    hi = x.astype(BF16)
    lo = (x - hi.astype(F32)).astype(BF16)
    return hi, lo


def _rms_rows(x, g):
    return x * lax.rsqrt(jnp.mean(x * x, axis=-1, keepdims=True) + NORM_EPS) * g


def _sigmoid(x):
    return 1.0 / (1.0 + jnp.exp(-x))


def _silu(x):
    return x * _sigmoid(x)


def _row_tile(m):
    return ROW_TILE if m % ROW_TILE == 0 else m


def _even_in_kernel(x_ref, g_ref, wm_ref, wa_ref, wgb_ref, wgate_ref, bgate_ref,
                    u_ref, ga_ref, q_ref, k_ref, v_ref, la_ref, gb_ref):
    xb = _rms_rows(x_ref[...], g_ref[...]).astype(BF16)

    def proj(lo, hi):
        return _dot(xb, wm_ref[:, lo:hi])

    u_ref[...] = proj(0, 512).astype(u_ref.dtype)
    ga_ref[...] = proj(512, 1024).astype(ga_ref.dtype)
    q_ref[...] = (proj(1024, 1280) * (GLA_HEAD_K ** -0.5)).astype(q_ref.dtype)
    k_ref[...] = proj(1280, 1536).astype(k_ref.dtype)
    v_ref[...] = proj(1536, 2048).astype(v_ref.dtype)
    gb_ref[...] = _dot(xb, wgb_ref[...]).astype(gb_ref.dtype)
    a_low = _dot(xb, wa_ref[...])
    logit = _dot(a_low.astype(BF16), wgate_ref[...]) + bgate_ref[...]
    log_sig = jnp.minimum(logit, 0.0) - jnp.log1p(jnp.exp(-jnp.abs(logit)))
    la_ref[...] = log_sig * (1.0 / GLA_GATE_TAU)


def _even_in(x, g, wm, wa, wgb, wgate, bgate, act_dtype):
    m, d = x.shape
    tm = _row_tile(m)
    row = lambda n: pl.BlockSpec((tm, n), lambda i: (i, 0))
    outs = [(512, act_dtype), (512, act_dtype), (256, act_dtype), (256, act_dtype), (512, act_dtype),
            (256, F32), (512, act_dtype)]
    return pl.pallas_call(
        _even_in_kernel,
        grid=(m // tm,),
        in_specs=[row(d), _full(g.shape), _full(wm.shape), _full(wa.shape), _full(wgb.shape),
                  _full(wgate.shape), _full(bgate.shape)],
        out_specs=[row(n) for n, _ in outs],
        out_shape=[jax.ShapeDtypeStruct((m, n), dt) for n, dt in outs],
        compiler_params=_cparams(("parallel",)),
        name="even_in",
    )(x, g, wm, wa, wgb, wgate, bgate)


def _s5_kernel(u_ref, x0_ref, m_ref, bp_ref, cp_ref, a1_ref, a2_ref, y_ref, xf_ref, loc_ref,
               *, nseq, nchunks):
    p = pl.program_id(0)
    g = pl.program_id(1)

    @pl.when(p == 0)
    def _():
        loc_ref[g] = _dot(u_ref[0], bp_ref[0])

    @pl.when((p == 1) & (g == 0))
    def _():
        a1 = a1_ref[...]
        a2 = a2_ref[...]

        def advance(x, loc):
            return a1 * x + a2 * pltpu.roll(x, S5_STATE, 2) + loc

        if nchunks == 1:
            x0 = x0_ref[...]
            xf_ref[...] = advance(x0, loc_ref[...])
            loc_ref[...] = x0
        else:
            def body(j, xs):
                new = []
                for b in range(nseq):
                    r = b * nchunks + j
                    loc = loc_ref[:, pl.ds(r, 1), :]
                    loc_ref[:, pl.ds(r, 1), :] = xs[b]
                    new.append(advance(xs[b], loc))
                return tuple(new)

            xs = lax.fori_loop(0, nchunks, body, tuple(x0_ref[:, b:b + 1, :] for b in range(nseq)))
            for b in range(nseq):
                xf_ref[:, b:b + 1, :] = xs[b]

    @pl.when(p == 1)
    def _():
        y = _dot(u_ref[0], m_ref[0]) + _dot(loc_ref[g].astype(BF16), cp_ref[0])
        y_ref[0] = y.astype(y_ref.dtype)


def _s5(u2, x0, mt, bp, cp, a1, a2, nseq, nchunks, out_dtype):
    ng, r, kk = u2.shape
    per_g = lambda s: pl.BlockSpec((1,) + s, lambda p, g: (g, 0, 0))
    return pl.pallas_call(
        functools.partial(_s5_kernel, nseq=nseq, nchunks=nchunks),
        grid=(2, ng),
        in_specs=[per_g((r, kk)), _full(x0.shape), per_g((kk, kk)), per_g((kk, 2 * S5_STATE)),
                  per_g((2 * S5_STATE, kk)), _full(a1.shape), _full(a2.shape)],
        out_specs=[pl.BlockSpec((1, r, kk), lambda p, g: (g * p, 0, 0)), _full(x0.shape)],
        out_shape=[jax.ShapeDtypeStruct((ng, r, kk), out_dtype), jax.ShapeDtypeStruct(x0.shape, F32)],
        scratch_shapes=[pltpu.VMEM((ng, r, 2 * S5_STATE), F32)],
        compiler_params=_cparams(("arbitrary", "arbitrary")),
        name="s5",
    )(u2, x0, mt, bp, cp, a1, a2)


def _s5_params(lam_re, lam_im, log_dt, b_re, b_im, c_re, c_im):
    t = S5_CHUNK
    dt = jnp.exp(log_dt)[:, None]
    a = lam_re * dt
    b = lam_im * dt
    n = jnp.arange(t + 1, dtype=F32)[None, :, None]
    mag = jnp.exp(n * a[:, None, :])
    pw_re = mag * jnp.cos(n * b[:, None, :])
    pw_im = mag * jnp.sin(n * b[:, None, :])
    em1_re = jnp.expm1(a) * jnp.cos(b) - 2.0 * jnp.sin(0.5 * b) ** 2
    em1_im = jnp.exp(a) * jnp.sin(b)
    den = lam_re * lam_re + lam_im * lam_im
    z_re = (em1_re * lam_re + em1_im * lam_im) / den
    z_im = (em1_im * lam_re - em1_re * lam_im) / den
    bb_re = z_re[..., None] * b_re - z_im[..., None] * b_im
    bb_im = z_re[..., None] * b_im + z_im[..., None] * b_re
    e_re, e_im = pw_re[:, :t, :, None], pw_im[:, :t, :, None]
    w_re = e_re * bb_re[:, None] - e_im * bb_im[:, None]
    w_im = e_re * bb_im[:, None] + e_im * bb_re[:, None]
    kd = (jnp.einsum('gdpc,gkp->gdck', w_re, c_re, precision=HIGHEST)
          - jnp.einsum('gdpc,gkp->gdck', w_im, c_im, precision=HIGHEST))
    lag = jnp.arange(t)[None, :] - jnp.arange(t)[:, None]
    mt = jnp.where((lag >= 0)[None, :, :, None, None], kd[:, jnp.maximum(lag, 0)], 0.0)
    ng = lam_re.shape[0]
    mt = mt.transpose(0, 1, 3, 2, 4).reshape(ng, t * S5_GROUP, t * S5_GROUP)
    bp = jnp.concatenate([w_re[:, ::-1], w_im[:, ::-1]], axis=2)
    bp = bp.transpose(0, 1, 3, 2).reshape(ng, t * S5_GROUP, 2 * S5_STATE)
    o_re, o_im = pw_re[:, 1:, None, :], pw_im[:, 1:, None, :]
    cl_re = c_re[:, None] * o_re - c_im[:, None] * o_im
    cl_im = c_re[:, None] * o_im + c_im[:, None] * o_re
    cp = jnp.concatenate([cl_re, -cl_im], axis=3)
    cp = cp.transpose(0, 3, 1, 2).reshape(ng, 2 * S5_STATE, t * S5_GROUP)
    return mt, bp, cp, pw_re, pw_im


def _s5_branch(u, x0, ops, nseq, seqlen, out_dtype):
    mt, bp, cp, pw_re, pw_im = ops
    ng = mt.shape[0]
    t = math.gcd(seqlen, S5_CHUNK)
    kk = t * S5_GROUP
    nchunks = seqlen // t
    r = nseq * nchunks
    u2 = u.reshape(r, t, ng, S5_GROUP).transpose(2, 0, 1, 3).reshape(ng, r, kk)
    full = S5_CHUNK * S5_GROUP
    a1 = jnp.concatenate([pw_re[:, t], pw_re[:, t]], axis=-1)[:, None, :]
    a2 = jnp.concatenate([-pw_im[:, t], pw_im[:, t]], axis=-1)[:, None, :]
    y2, xf = _s5(u2, x0, mt[:, :kk, :kk].astype(BF16), bp[:, full - kk:].astype(BF16),
                 cp[:, :, :kk].astype(BF16), a1, a2, nseq, nchunks, out_dtype)
    y = y2.reshape(ng, r, t, S5_GROUP).transpose(1, 2, 0, 3).reshape(nseq * seqlen, ng * S5_GROUP)
    return y, xf


def _gla_kernel(q_ref, k_ref, v_ref, la_ref, gb_ref, s0_ref, ng_ref, o_ref, sf_ref, st_ref,
                *, nseq, nchunks, c):
    i = pl.program_id(1)

    @pl.when(i == 0)
    def _():
        st_ref[...] = s0_ref[...]

    ones_c = jnp.ones((c, LANES), BF16)
    if c > 8:
        rr = lax.broadcasted_iota(jnp.int32, (c, c), 0)
        cc = lax.broadcasted_iota(jnp.int32, (c, c), 1)
        causal = rr >= cc
        tril = causal.astype(BF16)
    else:
        rr = lax.broadcasted_iota(jnp.int32, (c, c), 0)
        cc = lax.broadcasted_iota(jnp.int32, (c, c), 1)
        causal = rr >= cc
    ng = ng_ref[...]

    for s in range(nseq):
        for ci in range(nchunks):
            rows = pl.ds((s * nchunks + ci) * c, c)
            q = q_ref[rows, :].astype(F32)
            k = k_ref[rows, :].astype(F32)
            v = v_ref[rows, :].astype(BF16)
            la = la_ref[rows, :]
            la_hi, la_lo = _split_bf16(la)
            if c > 8:
                bcum = _dot(tril, la_hi) + _dot(tril, la_lo)
            else:
                acc = la[0:1]
                parts = [acc]
                for t in range(1, c):
                    acc = acc + la[t:t + 1]
                    parts.append(acc)
                bcum = jnp.concatenate(parts, axis=0)
            b_last = bcum[c - 1:c]
            q_dec = (q * jnp.exp(bcum)).astype(BF16)
            k_dec = (k * jnp.exp(-bcum)).astype(BF16)
            k_tail = (k * jnp.exp(b_last - bcum)).astype(BF16)
            dec = jnp.exp(_dot_tn(la_hi, ones_c) + _dot_tn(la_lo, ones_c))
            for h in range(GLA_HEADS):
                ks = slice(h * GLA_HEAD_K, (h + 1) * GLA_HEAD_K)
                vs = slice(h * GLA_HEAD_V, (h + 1) * GLA_HEAD_V)
                st = st_ref[s, h]
                att = jnp.where(causal, _dot_nt(q_dec[:, ks], k_dec[:, ks]), 0.0)
                o = _dot(att.astype(BF16), v[:, vs]) + _dot(q_dec[:, ks], st.astype(BF16))
                st_ref[s, h] = dec[ks] * st + _dot_tn(k_tail[:, ks], v[:, vs])
                o = _rms_rows(o, ng) * _silu(gb_ref[rows, vs].astype(F32))
                o_ref[rows, vs] = o.astype(o_ref.dtype)

    @pl.when(i == pl.num_programs(1) - 1)
    def _():
        sf_ref[...] = st_ref[...]


def _gla(q, k, v, la, gb, s0, ng, nseq_total, seqlen, out_dtype):
    c = math.gcd(seqlen, GLA_CHUNK)
    if seqlen >= 8 * c:
        nseq, nchunks = 1, 8
    else:
        nseq, nchunks = 8, seqlen // c
    assert nseq_total % nseq == 0 and seqlen % (nchunks * c) == 0
    nblk = seqlen // (nchunks * c)
    tm = nseq * nchunks * c
    row = lambda n: pl.BlockSpec((tm, n), lambda b, i: (b * nblk + i, 0))
    st_spec = pl.BlockSpec((nseq, GLA_HEADS, GLA_HEAD_K, GLA_HEAD_V), lambda b, i: (b, 0, 0, 0))
    m = q.shape[0]
    return pl.pallas_call(
        functools.partial(_gla_kernel, nseq=nseq, nchunks=nchunks, c=c),
        grid=(nseq_total // nseq, nblk),
        in_specs=[row(256), row(256), row(512), row(256), row(512), st_spec, _full(ng.shape)],
        out_specs=[row(512), st_spec],
        out_shape=[jax.ShapeDtypeStruct((m, 512), out_dtype), jax.ShapeDtypeStruct(s0.shape, F32)],
        scratch_shapes=[pltpu.VMEM((nseq, GLA_HEADS, GLA_HEAD_K, GLA_HEAD_V), F32)],
        compiler_params=_cparams(("parallel", "arbitrary")),
        name="gla",
    )(q, k, v, la, gb, s0, ng)


def _gelu_tanh(x):
    return 0.5 * x * (1.0 + jnp.tanh(math.sqrt(2.0 / math.pi) * (x + 0.044715 * (x * x * x))))


def _even_out_kernel(x_ref, y_ref, u_ref, ga_ref, ob_ref, d_ref, wglu_ref, bglu_ref, woa_ref, wob_ref, h_ref):
    y = y_ref[...].astype(F32) + d_ref[...] * u_ref[...].astype(F32)
    z = _gelu_tanh(y)
    z = z * _sigmoid(_dot(z.astype(BF16), wglu_ref[...]) + bglu_ref[...])
    out_a = z * _silu(ga_ref[...].astype(F32))
    mix = _dot(out_a.astype(BF16), woa_ref[...]) + _dot(ob_ref[...].astype(BF16), wob_ref[...])
    h_ref[...] = x_ref[...] + mix


def _even_out(x, y, u, ga, ob, d, wglu, bglu, woa, wob):
    m, dm = x.shape
    tm = _row_tile(m)
    row = lambda n: pl.BlockSpec((tm, n), lambda i: (i, 0))
    return pl.pallas_call(
        _even_out_kernel,
        grid=(m // tm,),
        in_specs=[row(dm), row(512), row(512), row(512), row(512), _full(d.shape), _full(wglu.shape),
                  _full(bglu.shape), _full(woa.shape), _full(wob.shape)],
        out_specs=row(dm),
        out_shape=jax.ShapeDtypeStruct((m, dm), F32),
        compiler_params=_cparams(("parallel",)),
        name="even_out",
    )(x, y, u, ga, ob, d, wglu, bglu, woa, wob)


def _head_ones(n):
    r = lax.broadcasted_iota(jnp.int32, (n, n), 0) // SWA_HEAD_DIM
    c = lax.broadcasted_iota(jnp.int32, (n, n), 1) // SWA_HEAD_DIM
    return (r == c).astype(BF16)


def _rope_block(x, cos_t, sin_t, upper):
    swapped = jnp.where(upper, pltpu.roll(x, 32, 1), pltpu.roll(x, 96, 1))
    return x * cos_t + swapped * sin_t


def _odd_in_kernel(h_ref, g_ref, w_ref, gq_ref, gk_ref, cos_ref, sin_ref, q_ref, k_ref, v_ref, gate_ref):
    xb = _rms_rows(h_ref[...], g_ref[...]).astype(BF16)
    cos_t = cos_ref[...]
    sin_t = sin_ref[...]
    tm = xb.shape[0]
    lane = lax.broadcasted_iota(jnp.int32, (tm, LANES), 1)
    upper = (lane & 32) != 0
    ones2 = _head_ones(LANES)
    inv_d = 1.0 / SWA_HEAD_DIM
    qw = SWA_HEADS * SWA_HEAD_DIM
    kw = SWA_KV_HEADS * SWA_HEAD_DIM
    for j in range(qw // LANES):
        cols = slice(j * LANES, (j + 1) * LANES)
        q = _dot(xb, w_ref[:, cols])
        ss = _dot((q * q).astype(BF16), ones2)
        qn = q * lax.rsqrt(ss * inv_d + NORM_EPS) * gq_ref[...]
        q_ref[:, cols] = (_rope_block(qn, cos_t, sin_t, upper) * (SWA_HEAD_DIM ** -0.5)).astype(q_ref.dtype)
    k = _dot(xb, w_ref[:, qw:qw + kw])
    ss = _dot((k * k).astype(BF16), ones2)
    kn = k * lax.rsqrt(ss * inv_d + NORM_EPS) * gk_ref[...]
    k_ref[...] = _rope_block(kn, cos_t, sin_t, upper)
    v_ref[...] = _dot(xb, w_ref[:, qw + kw:qw + 2 * kw])
    gate_ref[...] = _dot(xb, w_ref[:, qw + 2 * kw:]).astype(gate_ref.dtype)


def _odd_in(h, g, w, gq, gk, cos_t, sin_t, act_dtype):
    m, dm = h.shape
    tm = _row_tile(m)
    row = lambda n: pl.BlockSpec((tm, n), lambda i: (i, 0))
    assert cos_t.shape[0] % tm == 0
    nper = cos_t.shape[0] // tm
    tab = pl.BlockSpec((tm, LANES), lambda i: (i % nper, 0))
    qw = SWA_HEADS * SWA_HEAD_DIM
    kw = SWA_KV_HEADS * SWA_HEAD_DIM
    return pl.pallas_call(
        _odd_in_kernel,
        grid=(m // tm,),
        in_specs=[row(dm), _full(g.shape), _full(w.shape), _full(gq.shape), _full(gk.shape), tab, tab],
        out_specs=[row(qw), row(kw), row(kw), row(qw)],
        out_shape=[jax.ShapeDtypeStruct((m, qw), act_dtype), jax.ShapeDtypeStruct((m, kw), F32),
                   jax.ShapeDtypeStruct((m, kw), F32), jax.ShapeDtypeStruct((m, qw), act_dtype)],
        compiler_params=_cparams(("parallel",)),
        name="odd_in",
    )(h, g, w, gq, gk, cos_t, sin_t)


def _rope_tables(pos):
    half = SWA_HEAD_DIM // 2
    inv_freq = ROPE_THETA ** (-jnp.arange(half, dtype=F32) / half)
    ang = pos.astype(F32)[:, None] * inv_freq[None, :]
    cos, sin = jnp.cos(ang), jnp.sin(ang)
    return jnp.concatenate([cos, cos, cos, cos], axis=1), jnp.concatenate([-sin, sin, -sin, sin], axis=1)


def _attn_prompt_kernel(sink_ref, q_ref, kc_ref, kp_ref, vc_ref, vp_ref, o_ref, *, nqb):
    i = pl.program_id(1)
    w = SWA_WINDOW
    lane = lax.broadcasted_iota(jnp.int32, (2 * w, LANES), 1)
    low = lane < SWA_HEAD_DIM
    rr = lax.broadcasted_iota(jnp.int32, (w, w), 0)
    cc = lax.broadcasted_iota(jnp.int32, (w, w), 1)
    tri = cc <= rr
    for jb in range(nqb):
        rows = slice(jb * w, (jb + 1) * w)
        if jb == 0:
            k_prev, v_prev = kp_ref[...], vp_ref[...]
            has_prev = i > 0
        else:
            prev_rows = slice((jb - 1) * w, jb * w)
            k_prev, v_prev = kc_ref[prev_rows, :], vc_ref[prev_rows, :]
            has_prev = None
        kcat = jnp.concatenate([k_prev, kc_ref[rows, :]], axis=0)
        vcat = jnp.concatenate([v_prev, vc_ref[rows, :]], axis=0)
        for kv in range(SWA_KV_HEADS):
            own = low if kv == 0 else jnp.logical_not(low)
            k_own = jnp.where(own, kcat, 0.0)
            v_own = jnp.where(own, vcat, 0.0)
            k_oth = pltpu.roll(k_own, SWA_HEAD_DIM, 1)
            v_oth = pltpu.roll(v_own, SWA_HEAD_DIM, 1)
            k_lo, k_hi = (k_own, k_oth) if kv == 0 else (k_oth, k_own)
            v_lo, v_hi = (v_own, v_oth) if kv == 0 else (v_oth, v_own)
            k_rhs = jnp.concatenate([k_lo, k_hi], axis=0).astype(BF16)
            v_rhs = jnp.concatenate([v_lo, v_hi], axis=0).astype(BF16)
            for pr in range(SWA_GROUP // 2):
                hp = kv * (SWA_GROUP // 2) + pr
                cols = slice(hp * LANES, (hp + 1) * LANES)
                s_all = _dot_nt(q_ref[rows, cols], k_rhs)
                probs = []
                for e in range(2):
                    s_prev = s_all[:, (2 * e) * w:(2 * e + 1) * w]
                    s_cur = s_all[:, (2 * e + 1) * w:(2 * e + 2) * w]
                    if has_prev is not None:
                        s_prev = jnp.where(has_prev, s_prev, -jnp.inf)
                    sc = jnp.where(tri, s_cur, s_prev)
                    sink = sink_ref[2 * hp + e]
                    mx = jnp.maximum(jnp.max(sc, axis=-1, keepdims=True), sink)
                    pe = jnp.exp(sc - mx)
                    den = jnp.sum(pe, axis=-1, keepdims=True) + jnp.exp(sink - mx)
                    pe = pe * (1.0 / den)
                    probs += [jnp.where(tri, 0.0, pe), jnp.where(tri, pe, 0.0)]
                p_all = jnp.concatenate(probs, axis=1).astype(BF16)
                o = _dot(p_all, v_rhs)
                o_ref[rows, cols] = o.astype(o_ref.dtype)


def _attn_prompt(sinks, q, k, v, nseq, seqlen):
    w = SWA_WINDOW
    nqb = 4
    tm = nqb * w
    assert seqlen % tm == 0
    nblk = seqlen // tm
    qw = q.shape[1]
    row = lambda n: pl.BlockSpec((tm, n), lambda b, i: (b * nblk + i, 0))
    prev = lambda n: pl.BlockSpec((w, n), lambda b, i: (jnp.maximum((b * nblk + i) * nqb - 1, 0), 0))
    smem = pl.BlockSpec(memory_space=pltpu.SMEM)
    return pl.pallas_call(
        functools.partial(_attn_prompt_kernel, nqb=nqb),
        grid=(nseq, nblk),
        in_specs=[smem, row(qw), row(LANES), prev(LANES), row(LANES), prev(LANES)],
        out_specs=row(qw),
        out_shape=jax.ShapeDtypeStruct(q.shape, BF16),
        compiler_params=_cparams(("parallel", "arbitrary")),
        name="attn_prompt",
    )(sinks, q, k, k, v, v)


def _attn_sample_kernel(sink_ref, q_ref, kn_ref, vn_ref, kc_ref, vc_ref, o_ref, *, nseq, seqlen):
    nq = SWA_HEADS * seqlen
    ncache = kc_ref.shape[1]
    t_row = lax.broadcasted_iota(jnp.int32, (nq, ncache), 0) % seqlen
    c_col = lax.broadcasted_iota(jnp.int32, (nq, ncache), 1)
    cache_ok = c_col > t_row - (SWA_WINDOW - ncache)
    nnew = kn_ref.shape[1]
    t_row_n = lax.broadcasted_iota(jnp.int32, (nq, nnew), 0) % seqlen
    n_col = lax.broadcasted_iota(jnp.int32, (nq, nnew), 1)
    new_ok = n_col <= t_row_n
    sink = sink_ref[...]
    for s in range(nseq):
        q = q_ref[s]
        kn = kn_ref[s].astype(BF16)
        vn = vn_ref[s].astype(BF16)
        sc = jnp.where(cache_ok, _dot_nt(q, kc_ref[s].astype(BF16)), -jnp.inf)
        sn = jnp.where(new_ok, _dot_nt(q, kn), -jnp.inf)
        mx = jnp.maximum(jnp.maximum(jnp.max(sc, axis=-1, keepdims=True), jnp.max(sn, axis=-1, keepdims=True)),
                         sink[:, 0:1])
        pc = jnp.exp(sc - mx)
        pn = jnp.exp(sn - mx)
        den = (jnp.sum(pc, axis=-1, keepdims=True) + jnp.sum(pn, axis=-1, keepdims=True)
               + jnp.exp(sink[:, 0:1] - mx))
        inv = 1.0 / den
        o = _dot((pc * inv).astype(BF16), vc_ref[s].astype(BF16)) + _dot((pn * inv).astype(BF16), vn)
        o_ref[s] = o


def _attn_sample(sink_rows, q, kn, vn, kc, vc, seqlen):
    n = q.shape[0]
    nseq = 8
    assert n % nseq == 0
    blk = lambda a: pl.BlockSpec((nseq,) + a.shape[1:], lambda i: (i, 0, 0))
    return pl.pallas_call(
        functools.partial(_attn_sample_kernel, nseq=nseq, seqlen=seqlen),
        grid=(n // nseq,),
        in_specs=[_full(sink_rows.shape), blk(q), blk(kn), blk(vn), blk(kc), blk(vc)],
        out_specs=blk(q),
        out_shape=jax.ShapeDtypeStruct(q.shape, F32),
        compiler_params=_cparams(("parallel",)),
        name="attn_sample",
    )(sink_rows, q, kn, vn, kc, vc)


def _odd_out_kernel(h_ref, o_ref, gate_ref, w_ref, y_ref):
    og = o_ref[...].astype(F32) * _silu(gate_ref[...].astype(F32))
    y_ref[...] = h_ref[...] + _dot(og.astype(BF16), w_ref[...])


def _odd_out(h, o, gate, w):
    m, dm = h.shape
    tm = _row_tile(m)
    row = lambda n: pl.BlockSpec((tm, n), lambda i: (i, 0))
    return pl.pallas_call(
        _odd_out_kernel,
        grid=(m // tm,),
        in_specs=[row(dm), row(o.shape[1]), row(gate.shape[1]), _full(w.shape)],
        out_specs=row(dm),
        out_shape=jax.ShapeDtypeStruct((m, dm), F32),
        compiler_params=_cparams(("parallel",)),
        name="odd_out",
    )(h, o, gate, w)


def _even_layer(x, s5_x0, gla_s0, p, nseq, seqlen, act_dtype):
    u, ga, q, k, v, la, gb = _even_in(x, p['norm_g'], p['wm'], p['wa'], p['wgb'], p['wgate'], p['bgate'], act_dtype)
    y, s5_fin = _s5_branch(u, s5_x0, p['s5_ops'], nseq, seqlen, act_dtype)
    ob, gla_fin = _gla(q, k, v, la, gb, gla_s0, p['gla_norm_g'], nseq, seqlen, act_dtype)
    h = _even_out(x, y, u, ga, ob, p['d'], p['wglu'], p['bglu'], p['woa'], p['wob'])
    return h, s5_fin, gla_fin


def _odd_common(h, pos, p, act_dtype):
    cos_t, sin_t = _rope_tables(pos)
    return _odd_in(h, p['norm_g'], p['w_in'], p['gq'], p['gk'], cos_t, sin_t, act_dtype)


def kernel(x_prompt, x_sample, state_s5_re, state_s5_im, state_gla, cache_swa_k, cache_swa_v,
           even_norm_g, even_w_in, s5_lambda_re, s5_lambda_im, s5_log_dt, s5_b_re, s5_b_im,
           s5_c_re, s5_c_im, s5_d, s5_w_glu, s5_b_glu, gla_w_gate, gla_b_gate, gla_norm_g,
           even_w_out, odd_norm_g, odd_w_in, swa_q_norm_g, swa_k_norm_g, swa_sinks, odd_w_out):
    nb, seq, dm = x_prompt.shape
    ns, dseq, _ = x_sample.shape
    ng = s5_lambda_re.shape[1]
    xp = x_prompt.reshape(nb * seq, dm)
    xs = x_sample.reshape(ns * dseq, dm)

    i = 0
    w_in = even_w_in[i]
    s5w = ng * S5_GROUP
    col_alow = 2 * s5w + 2 * GLA_HEADS * GLA_HEAD_K + GLA_HEADS * GLA_HEAD_V
    col_gb = col_alow + GLA_GATE_RANK
    pad_rank = LANES - GLA_GATE_RANK
    pe = {
        'norm_g': even_norm_g[i][None, :],
        'wm': w_in[:, :col_alow].astype(BF16),
        'wa': jnp.pad(w_in[:, col_alow:col_gb], ((0, 0), (0, pad_rank))).astype(BF16),
        'wgb': w_in[:, col_gb:].astype(BF16),
        'wgate': jnp.pad(gla_w_gate[i], ((0, pad_rank), (0, 0))).astype(BF16),
        'bgate': gla_b_gate[i][None, :],
        's5_ops': _s5_params(s5_lambda_re[i], s5_lambda_im[i], s5_log_dt[i], s5_b_re[i], s5_b_im[i],
                             s5_c_re[i], s5_c_im[i]),
        'gla_norm_g': gla_norm_g[i][None, :],
        'd': s5_d[i][None, :],
        'wglu': s5_w_glu[i].astype(BF16),
        'bglu': s5_b_glu[i][None, :],
        'woa': even_w_out[i][:s5w].astype(BF16),
        'wob': even_w_out[i][s5w:].astype(BF16),
    }
    s5_zero = jnp.zeros((ng, nb, 2 * S5_STATE), F32)
    gla_zero = jnp.zeros((nb, GLA_HEADS, GLA_HEAD_K, GLA_HEAD_V), F32)
    hp, s5_p, gla_p = _even_layer(xp, s5_zero, gla_zero, pe, nb, seq, BF16)
    s5_init = jnp.concatenate([state_s5_re[i], state_s5_im[i]], axis=-1).transpose(1, 0, 2)
    hs, s5_s, gla_s = _even_layer(xs, s5_init, state_gla[i], pe, ns, dseq, F32)

    def s5_out(xf):
        xf = xf.transpose(1, 0, 2)
        return xf[None, :, :, :S5_STATE], xf[None, :, :, S5_STATE:]

    s5r_p, s5i_p = s5_out(s5_p)
    s5r_s, s5i_s = s5_out(s5_s)

    po = {
        'norm_g': odd_norm_g[i][None, :],
        'w_in': odd_w_in[i].astype(BF16),
        'gq': jnp.tile(swa_q_norm_g[i], LANES // SWA_HEAD_DIM)[None, :],
        'gk': jnp.tile(swa_k_norm_g[i], LANES // SWA_HEAD_DIM)[None, :],
    }
    w_out = odd_w_out[i].astype(BF16)
    sinks = swa_sinks[i]
    kvw = SWA_KV_HEADS * SWA_HEAD_DIM

    q, k, v, gate = _odd_common(hp, jnp.arange(seq, dtype=jnp.int32), po, BF16)
    o = _attn_prompt(sinks, q, k, v, nb, seq)
    y_prompt = _odd_out(hp, o, gate, w_out).reshape(nb, seq, dm)
    cache_len = min(SWA_WINDOW, seq)
    swk_p = k.reshape(nb, seq, SWA_KV_HEADS, SWA_HEAD_DIM)[None, :, seq - cache_len:]
    swv_p = v.reshape(nb, seq, SWA_KV_HEADS, SWA_HEAD_DIM)[None, :, seq - cache_len:]

    pos_s = jnp.tile(PAST_LEN + jnp.arange(dseq, dtype=jnp.int32), ns)
    q, k, v, gate = _odd_common(hs, pos_s, po, F32)
    ncache = cache_swa_k.shape[2]
    qh = q.reshape(ns, dseq, SWA_KV_HEADS, SWA_GROUP, SWA_HEAD_DIM).transpose(0, 2, 3, 1, 4)
    zq = jnp.zeros_like(qh[:, 0])
    q2 = jnp.stack([jnp.concatenate([qh[:, 0], zq], axis=-1), jnp.concatenate([zq, qh[:, 1]], axis=-1)], axis=1)
    q2 = q2.reshape(ns, SWA_HEADS * dseq, kvw).astype(BF16)
    sink_rows = jnp.broadcast_to(jnp.repeat(sinks, dseq)[:, None], (SWA_HEADS * dseq, LANES))
    kn = k.reshape(ns, dseq, kvw)
    vn = v.reshape(ns, dseq, kvw)
    npad = 16 - dseq
    kn_pad = jnp.pad(kn, ((0, 0), (0, npad), (0, 0)))
    vn_pad = jnp.pad(vn, ((0, 0), (0, npad), (0, 0)))
    kc = cache_swa_k[i].reshape(ns, ncache, kvw)
    vc = cache_swa_v[i].reshape(ns, ncache, kvw)
    o2 = _attn_sample(sink_rows, q2, kn_pad, vn_pad, kc, vc, dseq)
    o2 = o2.reshape(ns, SWA_KV_HEADS, SWA_GROUP, dseq, SWA_KV_HEADS, SWA_HEAD_DIM)
    o = jnp.stack([o2[:, 0, :, :, 0], o2[:, 1, :, :, 1]], axis=1)
    o = o.transpose(0, 3, 1, 2, 4).reshape(ns * dseq, SWA_HEADS * SWA_HEAD_DIM)
    y_sample = _odd_out(hs, o, gate, w_out).reshape(ns, dseq, dm)
    swk_s = jnp.concatenate([kc, kn], axis=1)[:, dseq:].reshape(1, ns, ncache, SWA_KV_HEADS, SWA_HEAD_DIM)
    swv_s = jnp.concatenate([vc, vn], axis=1)[:, dseq:].reshape(1, ns, ncache, SWA_KV_HEADS, SWA_HEAD_DIM)

    return (y_prompt, y_sample,
            s5r_p, s5i_p, gla_p[None], swk_p, swv_p,
            s5r_s, s5i_s, gla_s[None], swk_s, swv_s)
```

```python
import functools
import math

import jax
import jax.numpy as jnp
from jax import lax
from jax.experimental import pallas as pl
from jax.experimental.pallas import tpu as pltpu

F32 = jnp.float32
BF16 = jnp.bfloat16

PAST_LEN = 8192
NORM_EPS = 1e-6
S5_GROUP = 16
S5_STATE = 64
S5_CHUNK = 16
GLA_HEADS = 4
GLA_HEAD_K = 64
GLA_HEAD_V = 128
GLA_GATE_RANK = 16
GLA_GATE_TAU = 16.0
GLA_CHUNK = 64
SWA_HEADS = 16
SWA_KV_HEADS = 2
SWA_GROUP = SWA_HEADS // SWA_KV_HEADS
SWA_HEAD_DIM = 64
SWA_WINDOW = 128
ROPE_THETA = 10000.0
LANES = 128
MXU_TILE = 256
OCT = LANES // S5_GROUP
ROW_TILE = 512
VMEM_LIMIT = 48 * 1024 * 1024

HIGHEST = lax.Precision.HIGHEST


def _cparams(sem):
    return pltpu.CompilerParams(dimension_semantics=sem, vmem_limit_bytes=VMEM_LIMIT)


def _full(shape):
    n = len(shape)
    return pl.BlockSpec(shape, lambda *_: (0,) * n)


def _dot(a, b):
    return jnp.dot(a, b, preferred_element_type=F32)


def _dot_nt(a, b):
    return lax.dot_general(a, b, (((1,), (1,)), ((), ())), preferred_element_type=F32)


def _dot_tn(a, b):
    return lax.dot_general(a, b, (((0,), (0,)), ((), ())), preferred_element_type=F32)


def _split_bf16(x):
    hi = x.astype(BF16)
    lo = (x - hi.astype(F32)).astype(BF16)
    return hi, lo


def _rms_rows(x, g):
    return x * lax.rsqrt(jnp.mean(x * x, axis=-1, keepdims=True) + NORM_EPS) * g


def _sigmoid(x):
    return 1.0 / (1.0 + jnp.exp(-x))


def _silu(x):
    return x * _sigmoid(x)


def _row_tile(m):
    return ROW_TILE if m % ROW_TILE == 0 else m


def _even_in_kernel(x_ref, g_ref, wm_ref, wa_ref, wgb_ref, wgate_ref, bgate_ref,
                    u_ref, ga_ref, q_ref, k_ref, v_ref, la_ref, gb_ref):
    xb = _rms_rows(x_ref[...], g_ref[...]).astype(BF16)

    def proj(lo, hi):
        return _dot(xb, wm_ref[:, lo:hi])

    u = proj(0, 512)
    for o in range(u_ref.shape[0]):
        u_ref[o] = u[:, o * LANES:(o + 1) * LANES].astype(u_ref.dtype)
    ga_ref[...] = proj(512, 1024).astype(ga_ref.dtype)
    q_ref[...] = (proj(1024, 1280) * (GLA_HEAD_K ** -0.5)).astype(q_ref.dtype)
    k_ref[...] = proj(1280, 1536).astype(k_ref.dtype)
    v_ref[...] = proj(1536, 2048).astype(v_ref.dtype)
    gb_ref[...] = _dot(xb, wgb_ref[...]).astype(gb_ref.dtype)
    a_low = _dot(xb, wa_ref[...])
    logit = _dot(a_low.astype(BF16), wgate_ref[...]) + bgate_ref[...]
    log_sig = jnp.minimum(logit, 0.0) - jnp.log1p(jnp.exp(-jnp.abs(logit)))
    la_ref[...] = log_sig * (1.0 / GLA_GATE_TAU)


def _even_in(x, g, wm, wa, wgb, wgate, bgate, act_dtype):
    m, d = x.shape
    tm = _row_tile(m)
    row = lambda n: pl.BlockSpec((tm, n), lambda i: (i, 0))
    no = 512 // LANES
    slab = pl.BlockSpec((no, tm, LANES), lambda i: (0, i, 0))
    outs = [(512, act_dtype), (256, act_dtype), (256, act_dtype), (512, act_dtype), (256, F32), (512, act_dtype)]
    return pl.pallas_call(
        _even_in_kernel,
        grid=(m // tm,),
        in_specs=[row(d), _full(g.shape), _full(wm.shape), _full(wa.shape), _full(wgb.shape),
                  _full(wgate.shape), _full(bgate.shape)],
        out_specs=[slab] + [row(n) for n, _ in outs],
        out_shape=[jax.ShapeDtypeStruct((no, m, LANES), act_dtype)]
        + [jax.ShapeDtypeStruct((m, n), dt) for n, dt in outs],
        compiler_params=_cparams(("parallel",)),
        name="even_in",
    )(x, g, wm, wa, wgb, wgate, bgate)


def _s5_state_kernel(u_ref, bp_ref, x0_ref, are_ref, aim_ref, xs_ref, xf_ref, loc_ref, *, nseq, nchunks):
    hw = OCT * S5_STATE
    loc_ref[...] = _dot(u_ref[0], bp_ref[0])
    a_re = are_ref[0]
    a_im = aim_ref[0]
    if nchunks == 1:
        x0 = x0_ref[0]
        xr, xi = x0[:, :hw], x0[:, hw:]
        loc = loc_ref[...]
        xf_ref[0, :, :hw] = a_re * xr - a_im * xi + loc[:, :hw]
        xf_ref[0, :, hw:] = a_re * xi + a_im * xr + loc[:, hw:]
        xs_ref[0] = x0.astype(xs_ref.dtype)
    else:
        def body(j, carry):
            new = []
            for b in range(nseq):
                xr, xi = carry[b]
                row = pl.ds(b * nchunks + j, 1)
                lr = loc_ref[row, :hw]
                li = loc_ref[row, hw:]
                loc_ref[row, :hw] = xr
                loc_ref[row, hw:] = xi
                new.append((a_re * xr - a_im * xi + lr, a_re * xi + a_im * xr + li))
            return tuple(new)

        init = tuple((x0_ref[0, b:b + 1, :hw], x0_ref[0, b:b + 1, hw:]) for b in range(nseq))
        fin = lax.fori_loop(0, nchunks, body, init, unroll=4)
        for b in range(nseq):
            xf_ref[0, b:b + 1, :hw] = fin[b][0]
            xf_ref[0, b:b + 1, hw:] = fin[b][1]
        xs_ref[0] = loc_ref[...].astype(xs_ref.dtype)


def _s5_state(u2, bp, x0, are, aim, nseq, nchunks):
    no, r, kk = u2.shape
    sw = bp.shape[2]
    blk = lambda a: pl.BlockSpec((1,) + a.shape[1:], lambda o: (o,) + (0,) * (a.ndim - 1))
    return pl.pallas_call(
        functools.partial(_s5_state_kernel, nseq=nseq, nchunks=nchunks),
        grid=(no,),
        in_specs=[blk(u2), blk(bp), blk(x0), blk(are), blk(aim)],
        out_specs=[pl.BlockSpec((1, r, sw), lambda o: (o, 0, 0)), blk(x0)],
        out_shape=[jax.ShapeDtypeStruct((no, r, sw), BF16), jax.ShapeDtypeStruct(x0.shape, F32)],
        scratch_shapes=[pltpu.VMEM((r, sw), F32)],
        compiler_params=_cparams(("parallel",)),
        name="s5_state",
    )(u2, bp, x0, are, aim)


def _s5_out_kernel(u_ref, xs_ref, wt_ref, cp_ref, y_ref):
    ntile = u_ref.shape[2] // MXU_TILE
    xs = xs_ref[0]
    for n in range(ntile):
        cols = slice(n * MXU_TILE, (n + 1) * MXU_TILE)
        acc = _dot(xs, cp_ref[0, :, cols])
        for k in range(n + 1):
            acc = acc + _dot(u_ref[0, :, k * MXU_TILE:(k + 1) * MXU_TILE], wt_ref[0, n - k])
        y_ref[0, :, cols] = acc.astype(y_ref.dtype)


def _s5_out(u2, xs, wt, cp, out_dtype):
    no, r, kk = u2.shape
    blk = lambda a: pl.BlockSpec((1,) + a.shape[1:], lambda o: (o,) + (0,) * (a.ndim - 1))
    return pl.pallas_call(
        _s5_out_kernel,
        grid=(no,),
        in_specs=[blk(u2), blk(xs), blk(wt), blk(cp)],
        out_specs=blk(u2),
        out_shape=jax.ShapeDtypeStruct(u2.shape, out_dtype),
        compiler_params=_cparams(("parallel",)),
        name="s5_out",
    )(u2, xs, wt, cp)


def _s5_params(lam_re, lam_im, log_dt, b_re, b_im, c_re, c_im):
    t = S5_CHUNK
    ng = lam_re.shape[0]
    no = ng // OCT
    dt = jnp.exp(log_dt)[:, None]
    a = lam_re * dt
    b = lam_im * dt
    n = jnp.arange(t + 1, dtype=F32)[None, :, None]
    mag = jnp.exp(n * a[:, None, :])
    pw_re = mag * jnp.cos(n * b[:, None, :])
    pw_im = mag * jnp.sin(n * b[:, None, :])
    em1_re = jnp.expm1(a) * jnp.cos(b) - 2.0 * jnp.sin(0.5 * b) ** 2
    em1_im = jnp.exp(a) * jnp.sin(b)
    den = lam_re * lam_re + lam_im * lam_im
    z_re = (em1_re * lam_re + em1_im * lam_im) / den
    z_im = (em1_im * lam_re - em1_re * lam_im) / den
    bb_re = z_re[..., None] * b_re - z_im[..., None] * b_im
    bb_im = z_re[..., None] * b_im + z_im[..., None] * b_re
    e_re, e_im = pw_re[:, :t, :, None], pw_im[:, :t, :, None]
    w_re = e_re * bb_re[:, None] - e_im * bb_im[:, None]
    w_im = e_re * bb_im[:, None] + e_im * bb_re[:, None]
    kd = (jnp.einsum('gdpc,gkp->gdck', w_re, c_re, precision=HIGHEST)
          - jnp.einsum('gdpc,gkp->gdck', w_im, c_im, precision=HIGHEST))
    eye = jnp.eye(OCT, dtype=F32)
    kd_pad = jnp.pad(kd, ((0, 0), (1, 0), (0, 0), (0, 0)))
    nl = t // 2
    lag1 = (2 * jnp.arange(nl)[:, None, None] + jnp.arange(2)[None, None, :] - jnp.arange(2)[None, :, None] + 1)
    x = kd_pad[:, lag1].reshape(no, OCT, nl, 2, 2, S5_GROUP, S5_GROUP)
    wt = x[:, :, :, :, :, :, None, :] * eye[None, :, None, None, None, None, :, None]
    wt = wt.transpose(0, 2, 3, 1, 5, 4, 6, 7).reshape(no, nl, MXU_TILE, MXU_TILE)
    bpg = jnp.stack([w_re[:, ::-1], w_im[:, ::-1]], axis=2)
    bpg = bpg.transpose(0, 1, 4, 2, 3).reshape(no, OCT, t, S5_GROUP, 2, S5_STATE)
    bp = bpg[:, :, :, :, :, None, :] * eye[None, :, None, None, None, :, None]
    bp = bp.transpose(0, 2, 1, 3, 4, 5, 6).reshape(no, t * LANES, 2 * OCT * S5_STATE)
    o_re, o_im = pw_re[:, 1:, None, :], pw_im[:, 1:, None, :]
    cl_re = c_re[:, None] * o_re - c_im[:, None] * o_im
    cl_im = c_re[:, None] * o_im + c_im[:, None] * o_re
    cpg = jnp.stack([cl_re, -cl_im], axis=1).reshape(no, OCT, 2, t, S5_GROUP, S5_STATE)
    cp = cpg[:, :, :, :, :, :, None] * eye[None, :, None, None, None, None, :]
    cp = cp.transpose(0, 2, 1, 5, 3, 6, 4).reshape(no, 2 * OCT * S5_STATE, t * LANES)
    return wt.astype(BF16), bp.astype(BF16), cp.astype(BF16), pw_re, pw_im


def _s5_branch(u4, x0, ops, nseq, seqlen, out_dtype):
    wt, bp, cp, pw_re, pw_im = ops
    no = u4.shape[0]
    t = math.gcd(seqlen, S5_CHUNK)
    kk = t * LANES
    nchunks = seqlen // t
    r = nseq * nchunks
    u2 = u4.reshape(no, r, kk)
    are = pw_re[:, t].reshape(no, 1, OCT * S5_STATE)
    aim = pw_im[:, t].reshape(no, 1, OCT * S5_STATE)
    xs, xf = _s5_state(u2, bp[:, S5_CHUNK * LANES - kk:], x0, are, aim, nseq, nchunks)
    y2 = _s5_out(u2, xs, wt[:, :kk // MXU_TILE], cp[:, :, :kk], out_dtype)
    return y2.reshape(no, nseq * seqlen, LANES), xf


def _s5_state_in(re, im, no):
    nseq = re.shape[0]
    f = lambda a: a.reshape(nseq, no, OCT * S5_STATE).transpose(1, 0, 2)
    return jnp.concatenate([f(re), f(im)], axis=-1)


def _s5_state_out(xf):
    no, nseq, _ = xf.shape
    hw = OCT * S5_STATE
    f = lambda a: a.transpose(1, 0, 2).reshape(1, nseq, no * OCT, S5_STATE)
    return f(xf[:, :, :hw]), f(xf[:, :, hw:])


def _gla_kernel(q_ref, k_ref, v_ref, la_ref, gb_ref, s0_ref, ng_ref, o_ref, sf_ref, st_ref,
                *, nseq, nchunks, c):
    i = pl.program_id(1)

    @pl.when(i == 0)
    def _():
        st_ref[...] = s0_ref[...]

    nh, hk, hv = GLA_HEADS, GLA_HEAD_K, GLA_HEAD_V
    nch = nseq * nchunks
    tm = nch * c
    iota = lambda shape, d: lax.broadcasted_iota(jnp.int32, shape, d)
    ng = ng_ref[...]

    la_hi, la_lo = _split_bf16(la_ref[...])
    rt, ct = iota((tm, tm), 0), iota((tm, tm), 1)
    tril = ((rt // c == ct // c) & (rt >= ct)).astype(BF16)
    bcum = _dot(tril, la_hi) + _dot(tril, la_lo)
    seg = (iota((tm, nch * LANES), 0) // c == iota((tm, nch * LANES), 1) // LANES).astype(BF16)
    dec_all = jnp.exp(_dot_tn(la_hi, seg) + _dot_tn(la_lo, seg))
    q_all = q_ref[...].astype(F32)
    k_all = k_ref[...].astype(F32)
    q_dec_all = q_all * jnp.exp(bcum)
    k_dec_all = k_all * jnp.exp(-bcum)

    own_k = iota((nh * c, nh * hk), 0) // c == iota((nh * c, nh * hk), 1) // hk
    own_v = iota((nh * c, nh * hv), 0) // c == iota((nh * c, nh * hv), 1) // hv
    causal = iota((c, nh * c), 1) % c <= iota((c, nh * c), 0)
    zero_v = jnp.zeros((hk, hv), F32)

    sts = None
    for ci in range(nch):
        s, first, last = ci // nchunks, ci % nchunks == 0, ci % nchunks == nchunks - 1
        if first:
            sts = [st_ref[s, h] for h in range(nh)]
        rows = slice(ci * c, (ci + 1) * c)
        q_dec = q_dec_all[rows].astype(BF16)
        k_dec = k_dec_all[rows]
        b_c = bcum[rows]
        k_tail = (k_all[rows] * jnp.exp(b_c[c - 1:c] - b_c)).astype(BF16)
        v = v_ref[rows, :].astype(F32)
        k_bd = jnp.where(own_k, jnp.concatenate([k_dec] * nh, axis=0), 0.0).astype(BF16)
        v_bd = jnp.where(own_v, jnp.concatenate([v] * nh, axis=0), 0.0).astype(BF16)
        st_bd = jnp.concatenate(
            [jnp.concatenate([zero_v] * h + [sts[h]] + [zero_v] * (nh - 1 - h), axis=1) for h in range(nh)],
            axis=0).astype(BF16)
        att = jnp.where(causal, _dot_nt(q_dec, k_bd), 0.0)
        o = _dot(att.astype(BF16), v_bd) + _dot(q_dec, st_bd)
        kv = _dot_tn(k_tail, v.astype(BF16))
        for h in range(nh):
            ks = slice(h * hk, (h + 1) * hk)
            vs = slice(h * hv, (h + 1) * hv)
            sts[h] = dec_all[ks, ci * LANES:(ci + 1) * LANES] * sts[h] + kv[ks, vs]
            oh = _rms_rows(o[:, vs], ng) * _silu(gb_ref[rows, vs].astype(F32))
            o_ref[rows, vs] = oh.astype(o_ref.dtype)
        if last:
            for h in range(nh):
                st_ref[s, h] = sts[h]

    @pl.when(i == pl.num_programs(1) - 1)
    def _():
        sf_ref[...] = st_ref[...]


def _gla(q, k, v, la, gb, s0, ng, nseq_total, seqlen, out_dtype):
    c = math.gcd(seqlen, GLA_CHUNK)
    if seqlen >= 8 * c:
        nseq, nchunks = 1, 8
    else:
        nseq, nchunks = 8, seqlen // c
    assert nseq_total % nseq == 0 and seqlen % (nchunks * c) == 0
    nblk = seqlen // (nchunks * c)
    tm = nseq * nchunks * c
    row = lambda n: pl.BlockSpec((tm, n), lambda b, i: (b * nblk + i, 0))
    st_spec = pl.BlockSpec((nseq, GLA_HEADS, GLA_HEAD_K, GLA_HEAD_V), lambda b, i: (b, 0, 0, 0))
    m = q.shape[0]
    return pl.pallas_call(
        functools.partial(_gla_kernel, nseq=nseq, nchunks=nchunks, c=c),
        grid=(nseq_total // nseq, nblk),
        in_specs=[row(256), row(256), row(512), row(256), row(512), st_spec, _full(ng.shape)],
        out_specs=[row(512), st_spec],
        out_shape=[jax.ShapeDtypeStruct((m, 512), out_dtype), jax.ShapeDtypeStruct(s0.shape, F32)],
        scratch_shapes=[pltpu.VMEM((nseq, GLA_HEADS, GLA_HEAD_K, GLA_HEAD_V), F32)],
        compiler_params=_cparams(("parallel", "arbitrary")),
        name="gla",
    )(q, k, v, la, gb, s0, ng)


def _gelu_tanh(x):
    return 0.5 * x * (1.0 + jnp.tanh(math.sqrt(2.0 / math.pi) * (x + 0.044715 * (x * x * x))))


def _even_out_kernel(x_ref, y_ref, u_ref, ga_ref, ob_ref, d_ref, wglu_ref, bglu_ref, woa_ref, wob_ref, h_ref):
    no = y_ref.shape[0]
    y = jnp.concatenate([y_ref[o].astype(F32) for o in range(no)], axis=1)
    u = jnp.concatenate([u_ref[o].astype(F32) for o in range(no)], axis=1)
    z = _gelu_tanh(y + d_ref[...] * u)
    z = z * _sigmoid(_dot(z.astype(BF16), wglu_ref[...]) + bglu_ref[...])
    out_a = z * _silu(ga_ref[...].astype(F32))
    mix = _dot(out_a.astype(BF16), woa_ref[...]) + _dot(ob_ref[...].astype(BF16), wob_ref[...])
    h_ref[...] = x_ref[...] + mix


def _even_out(x, y, u, ga, ob, d, wglu, bglu, woa, wob):
    m, dm = x.shape
    tm = _row_tile(m)
    row = lambda n: pl.BlockSpec((tm, n), lambda i: (i, 0))
    slab = pl.BlockSpec((y.shape[0], tm, LANES), lambda i: (0, i, 0))
    return pl.pallas_call(
        _even_out_kernel,
        grid=(m // tm,),
        in_specs=[row(dm), slab, slab, row(512), row(512), _full(d.shape), _full(wglu.shape),
                  _full(bglu.shape), _full(woa.shape), _full(wob.shape)],
        out_specs=row(dm),
        out_shape=jax.ShapeDtypeStruct((m, dm), F32),
        compiler_params=_cparams(("parallel",)),
        name="even_out",
    )(x, y, u, ga, ob, d, wglu, bglu, woa, wob)


def _head_ones(n):
    r = lax.broadcasted_iota(jnp.int32, (n, n), 0) // SWA_HEAD_DIM
    c = lax.broadcasted_iota(jnp.int32, (n, n), 1) // SWA_HEAD_DIM
    return (r == c).astype(BF16)


def _rope_block(x, cos_t, sin_t, upper):
    swapped = jnp.where(upper, pltpu.roll(x, 32, 1), pltpu.roll(x, 96, 1))
    return x * cos_t + swapped * sin_t


def _odd_in_kernel(h_ref, g_ref, w_ref, gq_ref, gk_ref, cos_ref, sin_ref, q_ref, k_ref, v_ref, gate_ref):
    xb = _rms_rows(h_ref[...], g_ref[...]).astype(BF16)
    cos_t = cos_ref[...]
    sin_t = sin_ref[...]
    tm = xb.shape[0]
    lane = lax.broadcasted_iota(jnp.int32, (tm, LANES), 1)
    upper = (lane & 32) != 0
    ones4 = _head_ones(MXU_TILE)
    inv_d = 1.0 / SWA_HEAD_DIM
    qw = SWA_HEADS * SWA_HEAD_DIM
    kw = SWA_KV_HEADS * SWA_HEAD_DIM
    for j in range(qw // MXU_TILE):
        q = _dot(xb, w_ref[:, j * MXU_TILE:(j + 1) * MXU_TILE])
        ss = _dot((q * q).astype(BF16), ones4)
        qn = q * lax.rsqrt(ss * inv_d + NORM_EPS) * gq_ref[...]
        for e in range(MXU_TILE // LANES):
            cols = slice(j * MXU_TILE + e * LANES, j * MXU_TILE + (e + 1) * LANES)
            qe = _rope_block(qn[:, e * LANES:(e + 1) * LANES], cos_t, sin_t, upper)
            q_ref[:, cols] = (qe * (SWA_HEAD_DIM ** -0.5)).astype(q_ref.dtype)
    kv = _dot(xb, w_ref[:, qw:qw + 2 * kw])
    k = kv[:, :kw]
    ss = _dot((k * k).astype(BF16), ones4[:kw, :kw])
    kn = k * lax.rsqrt(ss * inv_d + NORM_EPS) * gk_ref[...]
    k_ref[...] = _rope_block(kn, cos_t, sin_t, upper)
    v_ref[...] = kv[:, kw:]
    gate_ref[...] = _dot(xb, w_ref[:, qw + 2 * kw:]).astype(gate_ref.dtype)


def _odd_in(h, g, w, gq, gk, cos_t, sin_t, act_dtype):
    m, dm = h.shape
    tm = _row_tile(m)
    row = lambda n: pl.BlockSpec((tm, n), lambda i: (i, 0))
    assert cos_t.shape[0] % tm == 0
    nper = cos_t.shape[0] // tm
    tab = pl.BlockSpec((tm, LANES), lambda i: (i % nper, 0))
    qw = SWA_HEADS * SWA_HEAD_DIM
    kw = SWA_KV_HEADS * SWA_HEAD_DIM
    return pl.pallas_call(
        _odd_in_kernel,
        grid=(m // tm,),
        in_specs=[row(dm), _full(g.shape), _full(w.shape), _full(gq.shape), _full(gk.shape), tab, tab],
        out_specs=[row(qw), row(kw), row(kw), row(qw)],
        out_shape=[jax.ShapeDtypeStruct((m, qw), act_dtype), jax.ShapeDtypeStruct((m, kw), F32),
                   jax.ShapeDtypeStruct((m, kw), F32), jax.ShapeDtypeStruct((m, qw), act_dtype)],
        compiler_params=_cparams(("parallel",)),
        name="odd_in",
    )(h, g, w, gq, gk, cos_t, sin_t)


def _rope_tables(pos):
    half = SWA_HEAD_DIM // 2
    inv_freq = ROPE_THETA ** (-jnp.arange(half, dtype=F32) / half)
    ang = pos.astype(F32)[:, None] * inv_freq[None, :]
    cos, sin = jnp.cos(ang), jnp.sin(ang)
    return jnp.concatenate([cos, cos, cos, cos], axis=1), jnp.concatenate([-sin, sin, -sin, sin], axis=1)


def _attn_prompt_kernel(sink_ref, q_ref, kc_ref, kp_ref, vc_ref, vp_ref, o_ref, p_ref, *, nqb):
    i = pl.program_id(1)
    w = SWA_WINDOW
    hd = SWA_HEAD_DIM
    npair = SWA_GROUP // 2
    lane2 = lax.broadcasted_iota(jnp.int32, (2 * w, LANES), 1)
    low = lane2 < hd
    rr = lax.broadcasted_iota(jnp.int32, (w, w), 0)
    cc = lax.broadcasted_iota(jnp.int32, (w, w), 1)
    tri = cc <= rr
    low_w = cc < hd
    r4 = lax.broadcasted_iota(jnp.int32, (4 * w, LANES), 0)
    c4 = lax.broadcasted_iota(jnp.int32, (4 * w, LANES), 1)
    den_cols = ((r4 < 2 * w) == (c4 < hd)).astype(BF16)
    unit = 0
    for jb in range(nqb):
        rows = slice(jb * w, (jb + 1) * w)
        if jb == 0:
            k_prev, v_prev = kp_ref[...], vp_ref[...]
            has_prev = i > 0
        else:
            prev_rows = slice((jb - 1) * w, jb * w)
            k_prev, v_prev = kc_ref[prev_rows, :], vc_ref[prev_rows, :]
            has_prev = None
        kcat = jnp.concatenate([k_prev, kc_ref[rows, :]], axis=0)
        vcat = jnp.concatenate([v_prev, vc_ref[rows, :]], axis=0)
        for kv in range(SWA_KV_HEADS):
            own = low if kv == 0 else jnp.logical_not(low)
            k_own = jnp.where(own, kcat, 0.0)
            v_own = jnp.where(own, vcat, 0.0)
            k_oth = pltpu.roll(k_own, hd, 1)
            v_oth = pltpu.roll(v_own, hd, 1)
            k_lo, k_hi = (k_own, k_oth) if kv == 0 else (k_oth, k_own)
            v_lo, v_hi = (v_own, v_oth) if kv == 0 else (v_oth, v_own)
            k_rhs = jnp.concatenate([k_lo, k_hi], axis=0).astype(BF16)
            v_rhs = jnp.concatenate([jnp.concatenate([v_lo, v_hi], axis=0).astype(BF16), den_cols], axis=1)
            qs = jnp.concatenate([q_ref[rows, (kv * npair + pr) * LANES:(kv * npair + pr + 1) * LANES]
                                  for pr in range(npair)], axis=0)
            s_all = _dot_nt(qs, k_rhs)
            buf = unit % 2
            unit += 1
            sink_terms = []
            for pr in range(npair):
                prow = slice(pr * w, (pr + 1) * w)
                pair_terms = []
                for e in range(2):
                    s_prev = s_all[prow, (2 * e) * w:(2 * e + 1) * w]
                    s_cur = s_all[prow, (2 * e + 1) * w:(2 * e + 2) * w]
                    if has_prev is not None:
                        s_prev = jnp.where(has_prev, s_prev, -jnp.inf)
                    sc = jnp.where(tri, s_cur, s_prev)
                    sink = sink_ref[2 * (kv * npair + pr) + e]
                    mx = jnp.maximum(jnp.max(sc, axis=-1, keepdims=True), sink)
                    pe = jnp.exp(sc - mx)
                    p_ref[buf, prow, (2 * e) * w:(2 * e + 1) * w] = jnp.where(tri, 0.0, pe).astype(BF16)
                    p_ref[buf, prow, (2 * e + 1) * w:(2 * e + 2) * w] = jnp.where(tri, pe, 0.0).astype(BF16)
                    pair_terms.append(jnp.exp(sink - mx))
                sink_terms.append(pair_terms)
            o_ext = _dot(p_ref[buf], v_rhs)
            for pr in range(npair):
                prow = slice(pr * w, (pr + 1) * w)
                cols = slice((kv * npair + pr) * LANES, (kv * npair + pr + 1) * LANES)
                st = jnp.where(low_w, sink_terms[pr][0], sink_terms[pr][1])
                o = o_ext[prow, :LANES] / (o_ext[prow, LANES:] + st)
                o_ref[rows, cols] = o.astype(o_ref.dtype)


def _attn_prompt(sinks, q, k, v, nseq, seqlen):
    w = SWA_WINDOW
    nqb = 4
    tm = nqb * w
    assert seqlen % tm == 0
    nblk = seqlen // tm
    qw = q.shape[1]
    row = lambda n: pl.BlockSpec((tm, n), lambda b, i: (b * nblk + i, 0))
    prev = lambda n: pl.BlockSpec((w, n), lambda b, i: (jnp.maximum((b * nblk + i) * nqb - 1, 0), 0))
    smem = pl.BlockSpec(memory_space=pltpu.SMEM)
    return pl.pallas_call(
        functools.partial(_attn_prompt_kernel, nqb=nqb),
        grid=(nseq, nblk),
        in_specs=[smem, row(qw), row(LANES), prev(LANES), row(LANES), prev(LANES)],
        out_specs=row(qw),
        out_shape=jax.ShapeDtypeStruct(q.shape, BF16),
        scratch_shapes=[pltpu.VMEM((2, 4 * w, 4 * w), BF16)],
        compiler_params=_cparams(("parallel", "arbitrary")),
        name="attn_prompt",
    )(sinks, q, k, k, v, v)


def _attn_sample_kernel(sink_ref, q_ref, kn_ref, vn_ref, kc_ref, vc_ref, o_ref, *, nseq, seqlen):
    nq = SWA_HEADS * seqlen
    ncache = kc_ref.shape[1]
    t_row = lax.broadcasted_iota(jnp.int32, (nq, ncache), 0) % seqlen
    c_col = lax.broadcasted_iota(jnp.int32, (nq, ncache), 1)
    cache_ok = c_col > t_row - (SWA_WINDOW - ncache)
    nnew = kn_ref.shape[1]
    t_row_n = lax.broadcasted_iota(jnp.int32, (nq, nnew), 0) % seqlen
    n_col = lax.broadcasted_iota(jnp.int32, (nq, nnew), 1)
    new_ok = n_col <= t_row_n
    sink = sink_ref[...]
    for s in range(nseq):
        q = q_ref[s]
        kn = kn_ref[s].astype(BF16)
        vn = vn_ref[s].astype(BF16)
        sc = jnp.where(cache_ok, _dot_nt(q, kc_ref[s].astype(BF16)), -jnp.inf)
        sn = jnp.where(new_ok, _dot_nt(q, kn), -jnp.inf)
        mx = jnp.maximum(jnp.maximum(jnp.max(sc, axis=-1, keepdims=True), jnp.max(sn, axis=-1, keepdims=True)),
                         sink[:, 0:1])
        pc = jnp.exp(sc - mx)
        pn = jnp.exp(sn - mx)
        den = (jnp.sum(pc, axis=-1, keepdims=True) + jnp.sum(pn, axis=-1, keepdims=True)
               + jnp.exp(sink[:, 0:1] - mx))
        inv = 1.0 / den
        o = _dot((pc * inv).astype(BF16), vc_ref[s].astype(BF16)) + _dot((pn * inv).astype(BF16), vn)
        o_ref[s] = o


def _attn_sample(sink_rows, q, kn, vn, kc, vc, seqlen):
    n = q.shape[0]
    nseq = 8
    assert n % nseq == 0
    blk = lambda a: pl.BlockSpec((nseq,) + a.shape[1:], lambda i: (i, 0, 0))
    return pl.pallas_call(
        functools.partial(_attn_sample_kernel, nseq=nseq, seqlen=seqlen),
        grid=(n // nseq,),
        in_specs=[_full(sink_rows.shape), blk(q), blk(kn), blk(vn), blk(kc), blk(vc)],
        out_specs=blk(q),
        out_shape=jax.ShapeDtypeStruct(q.shape, F32),
        compiler_params=_cparams(("parallel",)),
        name="attn_sample",
    )(sink_rows, q, kn, vn, kc, vc)


def _odd_out_kernel(h_ref, o_ref, gate_ref, w_ref, y_ref):
    og = o_ref[...].astype(F32) * _silu(gate_ref[...].astype(F32))
    y_ref[...] = h_ref[...] + _dot(og.astype(BF16), w_ref[...])


def _odd_out(h, o, gate, w):
    m, dm = h.shape
    tm = _row_tile(m)
    row = lambda n: pl.BlockSpec((tm, n), lambda i: (i, 0))
    return pl.pallas_call(
        _odd_out_kernel,
        grid=(m // tm,),
        in_specs=[row(dm), row(o.shape[1]), row(gate.shape[1]), _full(w.shape)],
        out_specs=row(dm),
        out_shape=jax.ShapeDtypeStruct((m, dm), F32),
        compiler_params=_cparams(("parallel",)),
        name="odd_out",
    )(h, o, gate, w)


def _even_layer(x, s5_x0, gla_s0, p, nseq, seqlen, act_dtype):
    u4, ga, q, k, v, la, gb = _even_in(x, p['norm_g'], p['wm'], p['wa'], p['wgb'], p['wgate'], p['bgate'], act_dtype)
    y4, s5_fin = _s5_branch(u4, s5_x0, p['s5_ops'], nseq, seqlen, act_dtype)
    ob, gla_fin = _gla(q, k, v, la, gb, gla_s0, p['gla_norm_g'], nseq, seqlen, act_dtype)
    h = _even_out(x, y4, u4, ga, ob, p['d'], p['wglu'], p['bglu'], p['woa'], p['wob'])
    return h, s5_fin, gla_fin


def _odd_common(h, pos, p, act_dtype):
    cos_t, sin_t = _rope_tables(pos)
    return _odd_in(h, p['norm_g'], p['w_in'], p['gq'], p['gk'], cos_t, sin_t, act_dtype)


def kernel(x_prompt, x_sample, state_s5_re, state_s5_im, state_gla, cache_swa_k, cache_swa_v,
           even_norm_g, even_w_in, s5_lambda_re, s5_lambda_im, s5_log_dt, s5_b_re, s5_b_im,
           s5_c_re, s5_c_im, s5_d, s5_w_glu, s5_b_glu, gla_w_gate, gla_b_gate, gla_norm_g,
           even_w_out, odd_norm_g, odd_w_in, swa_q_norm_g, swa_k_norm_g, swa_sinks, odd_w_out):
    nb, seq, dm = x_prompt.shape
    ns, dseq, _ = x_sample.shape
    ng = s5_lambda_re.shape[1]
    no = ng // OCT
    xp = x_prompt.reshape(nb * seq, dm)
    xs = x_sample.reshape(ns * dseq, dm)

    i = 0
    w_in = even_w_in[i]
    s5w = ng * S5_GROUP
    col_alow = 2 * s5w + 2 * GLA_HEADS * GLA_HEAD_K + GLA_HEADS * GLA_HEAD_V
    col_gb = col_alow + GLA_GATE_RANK
    pad_rank = LANES - GLA_GATE_RANK
    pe = {
        'norm_g': even_norm_g[i][None, :],
        'wm': w_in[:, :col_alow].astype(BF16),
        'wa': jnp.pad(w_in[:, col_alow:col_gb], ((0, 0), (0, pad_rank))).astype(BF16),
        'wgb': w_in[:, col_gb:].astype(BF16),
        'wgate': jnp.pad(gla_w_gate[i], ((0, pad_rank), (0, 0))).astype(BF16),
        'bgate': gla_b_gate[i][None, :],
        's5_ops': _s5_params(s5_lambda_re[i], s5_lambda_im[i], s5_log_dt[i], s5_b_re[i], s5_b_im[i],
                             s5_c_re[i], s5_c_im[i]),
        'gla_norm_g': gla_norm_g[i][None, :],
        'd': s5_d[i][None, :],
        'wglu': s5_w_glu[i].astype(BF16),
        'bglu': s5_b_glu[i][None, :],
        'woa': even_w_out[i][:s5w].astype(BF16),
        'wob': even_w_out[i][s5w:].astype(BF16),
    }
    s5_zero = jnp.zeros((no, nb, 2 * OCT * S5_STATE), F32)
    gla_zero = jnp.zeros((nb, GLA_HEADS, GLA_HEAD_K, GLA_HEAD_V), F32)
    hp, s5_p, gla_p = _even_layer(xp, s5_zero, gla_zero, pe, nb, seq, BF16)
    s5_init = _s5_state_in(state_s5_re[i], state_s5_im[i], no)
    hs, s5_s, gla_s = _even_layer(xs, s5_init, state_gla[i], pe, ns, dseq, F32)
    s5r_p, s5i_p = _s5_state_out(s5_p)
    s5r_s, s5i_s = _s5_state_out(s5_s)

    po = {
        'norm_g': odd_norm_g[i][None, :],
        'w_in': odd_w_in[i].astype(BF16),
        'gq': jnp.tile(swa_q_norm_g[i], MXU_TILE // SWA_HEAD_DIM)[None, :],
        'gk': jnp.tile(swa_k_norm_g[i], LANES // SWA_HEAD_DIM)[None, :],
    }
    w_out = odd_w_out[i].astype(BF16)
    sinks = swa_sinks[i]
    kvw = SWA_KV_HEADS * SWA_HEAD_DIM

    q, k, v, gate = _odd_common(hp, jnp.arange(seq, dtype=jnp.int32), po, BF16)
    o = _attn_prompt(sinks, q, k, v, nb, seq)
    y_prompt = _odd_out(hp, o, gate, w_out).reshape(nb, seq, dm)
    cache_len = min(SWA_WINDOW, seq)
    swk_p = k.reshape(nb, seq, SWA_KV_HEADS, SWA_HEAD_DIM)[None, :, seq - cache_len:]
    swv_p = v.reshape(nb, seq, SWA_KV_HEADS, SWA_HEAD_DIM)[None, :, seq - cache_len:]

    pos_s = jnp.tile(PAST_LEN + jnp.arange(dseq, dtype=jnp.int32), ns)
    q, k, v, gate = _odd_common(hs, pos_s, po, F32)
    ncache = cache_swa_k.shape[2]
    qh = q.reshape(ns, dseq, SWA_KV_HEADS, SWA_GROUP, SWA_HEAD_DIM).transpose(0, 2, 3, 1, 4)
    zq = jnp.zeros_like(qh[:, 0])
    q2 = jnp.stack([jnp.concatenate([qh[:, 0], zq], axis=-1), jnp.concatenate([zq, qh[:, 1]], axis=-1)], axis=1)
    q2 = q2.reshape(ns, SWA_HEADS * dseq, kvw).astype(BF16)
    sink_rows = jnp.broadcast_to(jnp.repeat(sinks, dseq)[:, None], (SWA_HEADS * dseq, LANES))
    kn = k.reshape(ns, dseq, kvw)
    vn = v.reshape(ns, dseq, kvw)
    npad = 16 - dseq
    kn_pad = jnp.pad(kn, ((0, 0), (0, npad), (0, 0)))
    vn_pad = jnp.pad(vn, ((0, 0), (0, npad), (0, 0)))
    kc = cache_swa_k[i].reshape(ns, ncache, kvw)
    vc = cache_swa_v[i].reshape(ns, ncache, kvw)
    o2 = _attn_sample(sink_rows, q2, kn_pad, vn_pad, kc, vc, dseq)
    o2 = o2.reshape(ns, SWA_KV_HEADS, SWA_GROUP, dseq, SWA_KV_HEADS, SWA_HEAD_DIM)
    o = jnp.stack([o2[:, 0, :, :, 0], o2[:, 1, :, :, 1]], axis=1)
    o = o.transpose(0, 3, 1, 2, 4).reshape(ns * dseq, SWA_HEADS * SWA_HEAD_DIM)
    y_sample = _odd_out(hs, o, gate, w_out).reshape(ns, dseq, dm)
    swk_s = jnp.concatenate([kc, kn], axis=1)[:, dseq:].reshape(1, ns, ncache, SWA_KV_HEADS, SWA_HEAD_DIM)
    swv_s = jnp.concatenate([vc, vn], axis=1)[:, dseq:].reshape(1, ns, ncache, SWA_KV_HEADS, SWA_HEAD_DIM)

    return (y_prompt, y_sample,
            s5r_p, s5i_p, gla_p[None], swk_p, swv_p,
            s5r_s, s5i_s, gla_s[None], swk_s, swv_s)
```

```python
import functools
import math

import jax
import jax.numpy as jnp
from jax import lax
from jax.experimental import pallas as pl
from jax.experimental.pallas import tpu as pltpu

F32 = jnp.float32
BF16 = jnp.bfloat16

PAST_LEN = 8192
NORM_EPS = 1e-6
S5_GROUP = 16
S5_STATE = 64
S5_CHUNK = 16
GLA_HEADS = 4
GLA_HEAD_K = 64
GLA_HEAD_V = 128
GLA_GATE_RANK = 16
GLA_GATE_TAU = 16.0
GLA_CHUNK = 64
SWA_HEADS = 16
SWA_KV_HEADS = 2
SWA_GROUP = SWA_HEADS // SWA_KV_HEADS
SWA_HEAD_DIM = 64
SWA_WINDOW = 128
ROPE_THETA = 10000.0
LANES = 128
MXU_TILE = 256
OCT = LANES // S5_GROUP
ROW_TILE = 512
VMEM_LIMIT = 48 * 1024 * 1024

HIGHEST = lax.Precision.HIGHEST


def _cparams(sem):
    return pltpu.CompilerParams(dimension_semantics=sem, vmem_limit_bytes=VMEM_LIMIT)


def _full(shape):
    n = len(shape)
    return pl.BlockSpec(shape, lambda *_: (0,) * n)


def _dot(a, b):
    return jnp.dot(a, b, preferred_element_type=F32)


def _dot_nt(a, b):
    return lax.dot_general(a, b, (((1,), (1,)), ((), ())), preferred_element_type=F32)


def _dot_tn(a, b):
    return lax.dot_general(a, b, (((0,), (0,)), ((), ())), preferred_element_type=F32)


def _split_bf16(x):
    hi = x.astype(BF16)
    lo = (x - hi.astype(F32)).astype(BF16)
    return hi, lo


def _rms_rows(x, g):
    return x * lax.rsqrt(jnp.mean(x * x, axis=-1, keepdims=True) + NORM_EPS) * g


def _sigmoid(x):
    return 1.0 / (1.0 + jnp.exp(-x))


def _silu(x):
    return x * _sigmoid(x)


def _row_tile(m):
    return ROW_TILE if m % ROW_TILE == 0 else m


def _even_in_kernel(x_ref, g_ref, wm_ref, wa_ref, wgb_ref, wgate_ref, bgate_ref,
                    u_ref, ga_ref, q_ref, k_ref, v_ref, la_ref, gb_ref):
    xb = _rms_rows(x_ref[...], g_ref[...]).astype(BF16)

    def proj(lo, hi):
        return _dot(xb, wm_ref[:, lo:hi])

    u = proj(0, 512)
    for o in range(u_ref.shape[0]):
        u_ref[o] = u[:, o * LANES:(o + 1) * LANES].astype(u_ref.dtype)
    ga_ref[...] = proj(512, 1024).astype(ga_ref.dtype)
    q_ref[...] = (proj(1024, 1280) * (GLA_HEAD_K ** -0.5)).astype(q_ref.dtype)
    k_ref[...] = proj(1280, 1536).astype(k_ref.dtype)
    v_ref[...] = proj(1536, 2048).astype(v_ref.dtype)
    gb_ref[...] = _dot(xb, wgb_ref[...]).astype(gb_ref.dtype)
    a_low = _dot(xb, wa_ref[...])
    logit = _dot(a_low.astype(BF16), wgate_ref[...]) + bgate_ref[...]
    log_sig = jnp.minimum(logit, 0.0) - jnp.log1p(jnp.exp(-jnp.abs(logit)))
    la_ref[...] = log_sig * (1.0 / GLA_GATE_TAU)


def _even_in(x, g, wm, wa, wgb, wgate, bgate, act_dtype):
    m, d = x.shape
    tm = _row_tile(m)
    row = lambda n: pl.BlockSpec((tm, n), lambda i: (i, 0))
    no = 512 // LANES
    slab = pl.BlockSpec((no, tm, LANES), lambda i: (0, i, 0))
    outs = [(512, act_dtype), (256, act_dtype), (256, act_dtype), (512, act_dtype), (256, F32), (512, act_dtype)]
    return pl.pallas_call(
        _even_in_kernel,
        grid=(m // tm,),
        in_specs=[row(d), _full(g.shape), _full(wm.shape), _full(wa.shape), _full(wgb.shape),
                  _full(wgate.shape), _full(bgate.shape)],
        out_specs=[slab] + [row(n) for n, _ in outs],
        out_shape=[jax.ShapeDtypeStruct((no, m, LANES), act_dtype)]
        + [jax.ShapeDtypeStruct((m, n), dt) for n, dt in outs],
        compiler_params=_cparams(("parallel",)),
        name="even_in",
    )(x, g, wm, wa, wgb, wgate, bgate)


def _s5_state_kernel(u_ref, bp_ref, x0_ref, are_ref, aim_ref, xs_ref, xf_ref, loc_ref, *, nseq, nchunks):
    hw = OCT * S5_STATE
    loc_ref[...] = _dot(u_ref[0], bp_ref[0])
    a_re = are_ref[0]
    a_im = aim_ref[0]
    if nchunks == 1:
        x0 = x0_ref[0]
        xr, xi = x0[:, :hw], x0[:, hw:]
        loc = loc_ref[...]
        xf_ref[0, :, :hw] = a_re * xr - a_im * xi + loc[:, :hw]
        xf_ref[0, :, hw:] = a_re * xi + a_im * xr + loc[:, hw:]
        xs_ref[0] = x0.astype(xs_ref.dtype)
    else:
        def body(j, carry):
            new = []
            for b in range(nseq):
                xr, xi = carry[b]
                row = pl.ds(b * nchunks + j, 1)
                lr = loc_ref[row, :hw]
                li = loc_ref[row, hw:]
                loc_ref[row, :hw] = xr
                loc_ref[row, hw:] = xi
                new.append((a_re * xr - a_im * xi + lr, a_re * xi + a_im * xr + li))
            return tuple(new)

        init = tuple((x0_ref[0, b:b + 1, :hw], x0_ref[0, b:b + 1, hw:]) for b in range(nseq))
        fin = lax.fori_loop(0, nchunks, body, init, unroll=4)
        for b in range(nseq):
            xf_ref[0, b:b + 1, :hw] = fin[b][0]
            xf_ref[0, b:b + 1, hw:] = fin[b][1]
        xs_ref[0] = loc_ref[...].astype(xs_ref.dtype)


def _s5_state(u2, bp, x0, are, aim, nseq, nchunks):
    no, r, kk = u2.shape
    sw = bp.shape[2]
    blk = lambda a: pl.BlockSpec((1,) + a.shape[1:], lambda o: (o,) + (0,) * (a.ndim - 1))
    return pl.pallas_call(
        functools.partial(_s5_state_kernel, nseq=nseq, nchunks=nchunks),
        grid=(no,),
        in_specs=[blk(u2), blk(bp), blk(x0), blk(are), blk(aim)],
        out_specs=[pl.BlockSpec((1, r, sw), lambda o: (o, 0, 0)), blk(x0)],
        out_shape=[jax.ShapeDtypeStruct((no, r, sw), BF16), jax.ShapeDtypeStruct(x0.shape, F32)],
        scratch_shapes=[pltpu.VMEM((r, sw), F32)],
        compiler_params=_cparams(("parallel",)),
        name="s5_state",
    )(u2, bp, x0, are, aim)


def _s5_out_kernel(u_ref, xs_ref, wt_ref, cp_ref, y_ref):
    ntile = u_ref.shape[2] // MXU_TILE
    xs = xs_ref[0]
    for n in range(ntile):
        cols = slice(n * MXU_TILE, (n + 1) * MXU_TILE)
        acc = _dot(xs, cp_ref[0, :, cols])
        for k in range(n + 1):
            acc = acc + _dot(u_ref[0, :, k * MXU_TILE:(k + 1) * MXU_TILE], wt_ref[0, n - k])
        y_ref[0, :, cols] = acc.astype(y_ref.dtype)


def _s5_out(u2, xs, wt, cp, out_dtype):
    no, r, kk = u2.shape
    blk = lambda a: pl.BlockSpec((1,) + a.shape[1:], lambda o: (o,) + (0,) * (a.ndim - 1))
    return pl.pallas_call(
        _s5_out_kernel,
        grid=(no,),
        in_specs=[blk(u2), blk(xs), blk(wt), blk(cp)],
        out_specs=blk(u2),
        out_shape=jax.ShapeDtypeStruct(u2.shape, out_dtype),
        compiler_params=_cparams(("parallel",)),
        name="s5_out",
    )(u2, xs, wt, cp)


def _s5_params(lam_re, lam_im, log_dt, b_re, b_im, c_re, c_im):
    t = S5_CHUNK
    ng = lam_re.shape[0]
    no = ng // OCT
    dt = jnp.exp(log_dt)[:, None]
    a = lam_re * dt
    b = lam_im * dt
    n = jnp.arange(t + 1, dtype=F32)[None, :, None]
    mag = jnp.exp(n * a[:, None, :])
    pw_re = mag * jnp.cos(n * b[:, None, :])
    pw_im = mag * jnp.sin(n * b[:, None, :])
    em1_re = jnp.expm1(a) * jnp.cos(b) - 2.0 * jnp.sin(0.5 * b) ** 2
    em1_im = jnp.exp(a) * jnp.sin(b)
    den = lam_re * lam_re + lam_im * lam_im
    z_re = (em1_re * lam_re + em1_im * lam_im) / den
    z_im = (em1_im * lam_re - em1_re * lam_im) / den
    bb_re = z_re[..., None] * b_re - z_im[..., None] * b_im
    bb_im = z_re[..., None] * b_im + z_im[..., None] * b_re
    e_re, e_im = pw_re[:, :t, :, None], pw_im[:, :t, :, None]
    w_re = e_re * bb_re[:, None] - e_im * bb_im[:, None]
    w_im = e_re * bb_im[:, None] + e_im * bb_re[:, None]
    kd = (jnp.einsum('gdpc,gkp->gdck', w_re, c_re, precision=HIGHEST)
          - jnp.einsum('gdpc,gkp->gdck', w_im, c_im, precision=HIGHEST))
    eye = jnp.eye(OCT, dtype=F32)
    kd_pad = jnp.pad(kd, ((0, 0), (1, 0), (0, 0), (0, 0)))
    nl = t // 2
    lag1 = (2 * jnp.arange(nl)[:, None, None] + jnp.arange(2)[None, None, :] - jnp.arange(2)[None, :, None] + 1)
    x = kd_pad[:, lag1].reshape(no, OCT, nl, 2, 2, S5_GROUP, S5_GROUP)
    x = x.transpose(0, 2, 3, 1, 5, 4, 6)
    wt = (x[:, :, :, :, :, :, None, :] * eye[None, None, None, :, None, None, :, None]).astype(BF16)
    wt = wt.reshape(no, nl, MXU_TILE, MXU_TILE)
    bpg = jnp.stack([w_re[:, ::-1], w_im[:, ::-1]], axis=2)
    bpg = bpg.transpose(0, 1, 4, 2, 3).reshape(no, OCT, t, S5_GROUP, 2, S5_STATE)
    bpg = bpg.transpose(0, 2, 1, 3, 4, 5)
    bp = (bpg[:, :, :, :, :, None, :] * eye[None, None, :, None, None, :, None]).astype(BF16)
    bp = bp.reshape(no, t * LANES, 2 * OCT * S5_STATE)
    o_re, o_im = pw_re[:, 1:, None, :], pw_im[:, 1:, None, :]
    cl_re = c_re[:, None] * o_re - c_im[:, None] * o_im
    cl_im = c_re[:, None] * o_im + c_im[:, None] * o_re
    cpg = jnp.stack([cl_re, -cl_im], axis=1).reshape(no, OCT, 2, t, S5_GROUP, S5_STATE)
    cpg = cpg.transpose(0, 2, 1, 5, 3, 4)
    cp = (cpg[:, :, :, :, :, None, :] * eye[None, None, :, None, None, :, None]).astype(BF16)
    cp = cp.reshape(no, 2 * OCT * S5_STATE, t * LANES)
    return wt, bp, cp, pw_re, pw_im


def _s5_branch(u4, x0, ops, nseq, seqlen, out_dtype):
    wt, bp, cp, pw_re, pw_im = ops
    no = u4.shape[0]
    t = math.gcd(seqlen, S5_CHUNK)
    kk = t * LANES
    nchunks = seqlen // t
    r = nseq * nchunks
    u2 = u4.reshape(no, r, kk)
    are = pw_re[:, t].reshape(no, 1, OCT * S5_STATE)
    aim = pw_im[:, t].reshape(no, 1, OCT * S5_STATE)
    xs, xf = _s5_state(u2, bp[:, S5_CHUNK * LANES - kk:], x0, are, aim, nseq, nchunks)
    y2 = _s5_out(u2, xs, wt[:, :kk // MXU_TILE], cp[:, :, :kk], out_dtype)
    return y2.reshape(no, nseq * seqlen, LANES), xf


def _s5_state_in(re, im, no):
    nseq = re.shape[0]
    f = lambda a: a.reshape(nseq, no, OCT * S5_STATE).transpose(1, 0, 2)
    return jnp.concatenate([f(re), f(im)], axis=-1)


def _s5_state_out(xf):
    no, nseq, _ = xf.shape
    hw = OCT * S5_STATE
    f = lambda a: a.transpose(1, 0, 2).reshape(1, nseq, no * OCT, S5_STATE)
    return f(xf[:, :, :hw]), f(xf[:, :, hw:])


def _gla_kernel(q_ref, k_ref, v_ref, la_ref, gb_ref, s0_ref, ng_ref, o_ref, sf_ref, st_ref,
                *, nseq, nchunks, c):
    i = pl.program_id(1)

    @pl.when(i == 0)
    def _():
        st_ref[...] = s0_ref[...]

    nh, hk, hv = GLA_HEADS, GLA_HEAD_K, GLA_HEAD_V
    nch = nseq * nchunks
    tm = nch * c
    iota = lambda shape, d: lax.broadcasted_iota(jnp.int32, shape, d)
    ng = ng_ref[...]

    la_hi, la_lo = _split_bf16(la_ref[...])
    rt, ct = iota((tm, tm), 0), iota((tm, tm), 1)
    tril = ((rt // c == ct // c) & (rt >= ct)).astype(BF16)
    bcum = _dot(tril, la_hi) + _dot(tril, la_lo)
    seg = (iota((tm, nch * LANES), 0) // c == iota((tm, nch * LANES), 1) // LANES).astype(BF16)
    dec_all = jnp.exp(_dot_tn(la_hi, seg) + _dot_tn(la_lo, seg))
    q_all = q_ref[...].astype(F32)
    k_all = k_ref[...].astype(F32)
    q_dec_all = q_all * jnp.exp(bcum)
    k_dec_all = k_all * jnp.exp(-bcum)

    own_k = iota((nh * c, nh * hk), 0) // c == iota((nh * c, nh * hk), 1) // hk
    own_v = iota((nh * c, nh * hv), 0) // c == iota((nh * c, nh * hv), 1) // hv
    causal = iota((c, nh * c), 1) % c <= iota((c, nh * c), 0)
    zero_v = jnp.zeros((hk, hv), F32)

    sts = None
    for ci in range(nch):
        s, first, last = ci // nchunks, ci % nchunks == 0, ci % nchunks == nchunks - 1
        if first:
            sts = [st_ref[s, h] for h in range(nh)]
        rows = slice(ci * c, (ci + 1) * c)
        q_dec = q_dec_all[rows].astype(BF16)
        k_dec = k_dec_all[rows]
        b_c = bcum[rows]
        k_tail = (k_all[rows] * jnp.exp(b_c[c - 1:c] - b_c)).astype(BF16)
        v = v_ref[rows, :].astype(F32)
        k_bd = jnp.where(own_k, jnp.concatenate([k_dec] * nh, axis=0), 0.0).astype(BF16)
        v_bd = jnp.where(own_v, jnp.concatenate([v] * nh, axis=0), 0.0).astype(BF16)
        st_bd = jnp.concatenate(
            [jnp.concatenate([zero_v] * h + [sts[h]] + [zero_v] * (nh - 1 - h), axis=1) for h in range(nh)],
            axis=0).astype(BF16)
        att = jnp.where(causal, _dot_nt(q_dec, k_bd), 0.0)
        o = _dot(att.astype(BF16), v_bd) + _dot(q_dec, st_bd)
        kv = _dot_tn(k_tail, v.astype(BF16))
        for h in range(nh):
            ks = slice(h * hk, (h + 1) * hk)
            vs = slice(h * hv, (h + 1) * hv)
            sts[h] = dec_all[ks, ci * LANES:(ci + 1) * LANES] * sts[h] + kv[ks, vs]
            oh = _rms_rows(o[:, vs], ng) * _silu(gb_ref[rows, vs].astype(F32))
            o_ref[rows, vs] = oh.astype(o_ref.dtype)
        if last:
            for h in range(nh):
                st_ref[s, h] = sts[h]

    @pl.when(i == pl.num_programs(1) - 1)
    def _():
        sf_ref[...] = st_ref[...]


def _gla(q, k, v, la, gb, s0, ng, nseq_total, seqlen, out_dtype):
    c = math.gcd(seqlen, GLA_CHUNK)
    if seqlen >= 8 * c:
        nseq, nchunks = 1, 8
    else:
        nseq, nchunks = 8, seqlen // c
    assert nseq_total % nseq == 0 and seqlen % (nchunks * c) == 0
    nblk = seqlen // (nchunks * c)
    tm = nseq * nchunks * c
    row = lambda n: pl.BlockSpec((tm, n), lambda b, i: (b * nblk + i, 0))
    st_spec = pl.BlockSpec((nseq, GLA_HEADS, GLA_HEAD_K, GLA_HEAD_V), lambda b, i: (b, 0, 0, 0))
    m = q.shape[0]
    return pl.pallas_call(
        functools.partial(_gla_kernel, nseq=nseq, nchunks=nchunks, c=c),
        grid=(nseq_total // nseq, nblk),
        in_specs=[row(256), row(256), row(512), row(256), row(512), st_spec, _full(ng.shape)],
        out_specs=[row(512), st_spec],
        out_shape=[jax.ShapeDtypeStruct((m, 512), out_dtype), jax.ShapeDtypeStruct(s0.shape, F32)],
        scratch_shapes=[pltpu.VMEM((nseq, GLA_HEADS, GLA_HEAD_K, GLA_HEAD_V), F32)],
        compiler_params=_cparams(("parallel", "arbitrary")),
        name="gla",
    )(q, k, v, la, gb, s0, ng)


def _gelu_tanh(x):
    return 0.5 * x * (1.0 + jnp.tanh(math.sqrt(2.0 / math.pi) * (x + 0.044715 * (x * x * x))))


def _even_out_kernel(x_ref, y_ref, u_ref, ga_ref, ob_ref, d_ref, wglu_ref, bglu_ref, woa_ref, wob_ref, h_ref):
    no = y_ref.shape[0]
    y = jnp.concatenate([y_ref[o].astype(F32) for o in range(no)], axis=1)
    u = jnp.concatenate([u_ref[o].astype(F32) for o in range(no)], axis=1)
    z = _gelu_tanh(y + d_ref[...] * u)
    z = z * _sigmoid(_dot(z.astype(BF16), wglu_ref[...]) + bglu_ref[...])
    out_a = z * _silu(ga_ref[...].astype(F32))
    mix = _dot(out_a.astype(BF16), woa_ref[...]) + _dot(ob_ref[...].astype(BF16), wob_ref[...])
    h_ref[...] = x_ref[...] + mix


def _even_out(x, y, u, ga, ob, d, wglu, bglu, woa, wob):
    m, dm = x.shape
    tm = _row_tile(m)
    row = lambda n: pl.BlockSpec((tm, n), lambda i: (i, 0))
    slab = pl.BlockSpec((y.shape[0], tm, LANES), lambda i: (0, i, 0))
    return pl.pallas_call(
        _even_out_kernel,
        grid=(m // tm,),
        in_specs=[row(dm), slab, slab, row(512), row(512), _full(d.shape), _full(wglu.shape),
                  _full(bglu.shape), _full(woa.shape), _full(wob.shape)],
        out_specs=row(dm),
        out_shape=jax.ShapeDtypeStruct((m, dm), F32),
        compiler_params=_cparams(("parallel",)),
        name="even_out",
    )(x, y, u, ga, ob, d, wglu, bglu, woa, wob)


def _head_ones(n):
    r = lax.broadcasted_iota(jnp.int32, (n, n), 0) // SWA_HEAD_DIM
    c = lax.broadcasted_iota(jnp.int32, (n, n), 1) // SWA_HEAD_DIM
    return (r == c).astype(BF16)


def _rope_block(x, cos_t, sin_t, upper):
    swapped = jnp.where(upper, pltpu.roll(x, 32, 1), pltpu.roll(x, 96, 1))
    return x * cos_t + swapped * sin_t


def _odd_in_kernel(h_ref, g_ref, w_ref, gq_ref, gk_ref, cos_ref, sin_ref, q_ref, k_ref, v_ref, gate_ref):
    xb = _rms_rows(h_ref[...], g_ref[...]).astype(BF16)
    cos_t = cos_ref[...]
    sin_t = sin_ref[...]
    tm = xb.shape[0]
    lane = lax.broadcasted_iota(jnp.int32, (tm, LANES), 1)
    upper = (lane & 32) != 0
    ones4 = _head_ones(MXU_TILE)
    inv_d = 1.0 / SWA_HEAD_DIM
    qw = SWA_HEADS * SWA_HEAD_DIM
    kw = SWA_KV_HEADS * SWA_HEAD_DIM
    for j in range(qw // MXU_TILE):
        q = _dot(xb, w_ref[:, j * MXU_TILE:(j + 1) * MXU_TILE])
        ss = _dot((q * q).astype(BF16), ones4)
        qn = q * lax.rsqrt(ss * inv_d + NORM_EPS) * gq_ref[...]
        for e in range(MXU_TILE // LANES):
            cols = slice(j * MXU_TILE + e * LANES, j * MXU_TILE + (e + 1) * LANES)
            qe = _rope_block(qn[:, e * LANES:(e + 1) * LANES], cos_t, sin_t, upper)
            q_ref[:, cols] = (qe * (SWA_HEAD_DIM ** -0.5)).astype(q_ref.dtype)
    kv = _dot(xb, w_ref[:, qw:qw + 2 * kw])
    k = kv[:, :kw]
    ss = _dot((k * k).astype(BF16), ones4[:kw, :kw])
    kn = k * lax.rsqrt(ss * inv_d + NORM_EPS) * gk_ref[...]
    k_ref[...] = _rope_block(kn, cos_t, sin_t, upper)
    v_ref[...] = kv[:, kw:]
    gate_ref[...] = _dot(xb, w_ref[:, qw + 2 * kw:]).astype(gate_ref.dtype)


def _odd_in(h, g, w, gq, gk, cos_t, sin_t, act_dtype):
    m, dm = h.shape
    tm = _row_tile(m)
    row = lambda n: pl.BlockSpec((tm, n), lambda i: (i, 0))
    assert cos_t.shape[0] % tm == 0
    nper = cos_t.shape[0] // tm
    tab = pl.BlockSpec((tm, LANES), lambda i: (i % nper, 0))
    qw = SWA_HEADS * SWA_HEAD_DIM
    kw = SWA_KV_HEADS * SWA_HEAD_DIM
    return pl.pallas_call(
        _odd_in_kernel,
        grid=(m // tm,),
        in_specs=[row(dm), _full(g.shape), _full(w.shape), _full(gq.shape), _full(gk.shape), tab, tab],
        out_specs=[row(qw), row(kw), row(kw), row(qw)],
        out_shape=[jax.ShapeDtypeStruct((m, qw), act_dtype), jax.ShapeDtypeStruct((m, kw), F32),
                   jax.ShapeDtypeStruct((m, kw), F32), jax.ShapeDtypeStruct((m, qw), act_dtype)],
        compiler_params=_cparams(("parallel",)),
        name="odd_in",
    )(h, g, w, gq, gk, cos_t, sin_t)


def _rope_tables(pos):
    half = SWA_HEAD_DIM // 2
    inv_freq = ROPE_THETA ** (-jnp.arange(half, dtype=F32) / half)
    ang = pos.astype(F32)[:, None] * inv_freq[None, :]
    cos, sin = jnp.cos(ang), jnp.sin(ang)
    return jnp.concatenate([cos, cos, cos, cos], axis=1), jnp.concatenate([-sin, sin, -sin, sin], axis=1)


def _attn_prompt_kernel(sink_ref, q_ref, kc_ref, kp_ref, vc_ref, vp_ref, o_ref, p_ref, *, nqb):
    i = pl.program_id(1)
    w = SWA_WINDOW
    hd = SWA_HEAD_DIM
    npair = SWA_GROUP // 2
    lane2 = lax.broadcasted_iota(jnp.int32, (2 * w, LANES), 1)
    low = lane2 < hd
    rr = lax.broadcasted_iota(jnp.int32, (w, w), 0)
    cc = lax.broadcasted_iota(jnp.int32, (w, w), 1)
    tri = cc <= rr
    low_w = cc < hd
    r4 = lax.broadcasted_iota(jnp.int32, (4 * w, LANES), 0)
    c4 = lax.broadcasted_iota(jnp.int32, (4 * w, LANES), 1)
    den_cols = ((r4 < 2 * w) == (c4 < hd)).astype(BF16)
    unit = 0
    for jb in range(nqb):
        rows = slice(jb * w, (jb + 1) * w)
        if jb == 0:
            k_prev, v_prev = kp_ref[...], vp_ref[...]
            has_prev = i > 0
        else:
            prev_rows = slice((jb - 1) * w, jb * w)
            k_prev, v_prev = kc_ref[prev_rows, :], vc_ref[prev_rows, :]
            has_prev = None
        kcat = jnp.concatenate([k_prev, kc_ref[rows, :]], axis=0)
        vcat = jnp.concatenate([v_prev, vc_ref[rows, :]], axis=0)
        for kv in range(SWA_KV_HEADS):
            own = low if kv == 0 else jnp.logical_not(low)
            k_own = jnp.where(own, kcat, 0.0)
            v_own = jnp.where(own, vcat, 0.0)
            k_oth = pltpu.roll(k_own, hd, 1)
            v_oth = pltpu.roll(v_own, hd, 1)
            k_lo, k_hi = (k_own, k_oth) if kv == 0 else (k_oth, k_own)
            v_lo, v_hi = (v_own, v_oth) if kv == 0 else (v_oth, v_own)
            k_rhs = jnp.concatenate([k_lo, k_hi], axis=0).astype(BF16)
            v_rhs = jnp.concatenate([jnp.concatenate([v_lo, v_hi], axis=0).astype(BF16), den_cols], axis=1)
            qs = jnp.concatenate([q_ref[rows, (kv * npair + pr) * LANES:(kv * npair + pr + 1) * LANES]
                                  for pr in range(npair)], axis=0)
            s_all = _dot_nt(qs, k_rhs)
            buf = unit % 2
            unit += 1
            sink_terms = []
            for pr in range(npair):
                prow = slice(pr * w, (pr + 1) * w)
                pair_terms = []
                for e in range(2):
                    s_prev = s_all[prow, (2 * e) * w:(2 * e + 1) * w]
                    s_cur = s_all[prow, (2 * e + 1) * w:(2 * e + 2) * w]
                    if has_prev is not None:
                        s_prev = jnp.where(has_prev, s_prev, -jnp.inf)
                    sc = jnp.where(tri, s_cur, s_prev)
                    sink = sink_ref[2 * (kv * npair + pr) + e]
                    mx = jnp.maximum(jnp.max(sc, axis=-1, keepdims=True), sink)
                    pe = jnp.exp(sc - mx)
                    p_ref[buf, prow, (2 * e) * w:(2 * e + 1) * w] = jnp.where(tri, 0.0, pe).astype(BF16)
                    p_ref[buf, prow, (2 * e + 1) * w:(2 * e + 2) * w] = jnp.where(tri, pe, 0.0).astype(BF16)
                    pair_terms.append(jnp.exp(sink - mx))
                sink_terms.append(pair_terms)
            o_ext = _dot(p_ref[buf], v_rhs)
            for pr in range(npair):
                prow = slice(pr * w, (pr + 1) * w)
                cols = slice((kv * npair + pr) * LANES, (kv * npair + pr + 1) * LANES)
                st = jnp.where(low_w, sink_terms[pr][0], sink_terms[pr][1])
                o = o_ext[prow, :LANES] / (o_ext[prow, LANES:] + st)
                o_ref[rows, cols] = o.astype(o_ref.dtype)


def _attn_prompt(sinks, q, k, v, nseq, seqlen):
    w = SWA_WINDOW
    nqb = 4
    tm = nqb * w
    assert seqlen % tm == 0
    nblk = seqlen // tm
    qw = q.shape[1]
    row = lambda n: pl.BlockSpec((tm, n), lambda b, i: (b * nblk + i, 0))
    prev = lambda n: pl.BlockSpec((w, n), lambda b, i: (jnp.maximum((b * nblk + i) * nqb - 1, 0), 0))
    smem = pl.BlockSpec(memory_space=pltpu.SMEM)
    return pl.pallas_call(
        functools.partial(_attn_prompt_kernel, nqb=nqb),
        grid=(nseq, nblk),
        in_specs=[smem, row(qw), row(LANES), prev(LANES), row(LANES), prev(LANES)],
        out_specs=row(qw),
        out_shape=jax.ShapeDtypeStruct(q.shape, BF16),
        scratch_shapes=[pltpu.VMEM((2, 4 * w, 4 * w), BF16)],
        compiler_params=_cparams(("parallel", "arbitrary")),
        name="attn_prompt",
    )(sinks, q, k, k, v, v)


def _attn_sample_kernel(sink_ref, q_ref, kn_ref, vn_ref, kc_ref, vc_ref, o_ref, ko_ref, vo_ref, *, nseq, seqlen):
    nq = SWA_HEADS * seqlen
    ncache = kc_ref.shape[1]
    t_row = lax.broadcasted_iota(jnp.int32, (nq, ncache), 0) % seqlen
    c_col = lax.broadcasted_iota(jnp.int32, (nq, ncache), 1)
    cache_ok = c_col > t_row - (SWA_WINDOW - ncache)
    nnew = kn_ref.shape[1]
    t_row_n = lax.broadcasted_iota(jnp.int32, (nq, nnew), 0) % seqlen
    n_col = lax.broadcasted_iota(jnp.int32, (nq, nnew), 1)
    new_ok = n_col <= t_row_n
    sink = sink_ref[...]
    for s in range(nseq):
        q = q_ref[s]
        kn = kn_ref[s].astype(BF16)
        vn = vn_ref[s].astype(BF16)
        sc = jnp.where(cache_ok, _dot_nt(q, kc_ref[s].astype(BF16)), -jnp.inf)
        sn = jnp.where(new_ok, _dot_nt(q, kn), -jnp.inf)
        mx = jnp.maximum(jnp.maximum(jnp.max(sc, axis=-1, keepdims=True), jnp.max(sn, axis=-1, keepdims=True)),
                         sink[:, 0:1])
        pc = jnp.exp(sc - mx)
        pn = jnp.exp(sn - mx)
        den = (jnp.sum(pc, axis=-1, keepdims=True) + jnp.sum(pn, axis=-1, keepdims=True)
               + jnp.exp(sink[:, 0:1] - mx))
        inv = 1.0 / den
        o = _dot((pc * inv).astype(BF16), vc_ref[s].astype(BF16)) + _dot((pn * inv).astype(BF16), vn)
        o_ref[s] = o
        keep = ncache - seqlen
        ko_ref[s, 0:keep, :] = kc_ref[s, seqlen:ncache, :]
        ko_ref[s, keep:ncache, :] = kn_ref[s, 0:seqlen, :]
        vo_ref[s, 0:keep, :] = vc_ref[s, seqlen:ncache, :]
        vo_ref[s, keep:ncache, :] = vn_ref[s, 0:seqlen, :]


def _attn_sample(sink_rows, q, kn, vn, kc, vc, seqlen):
    n = q.shape[0]
    nseq = 8
    assert n % nseq == 0
    blk = lambda a: pl.BlockSpec((nseq,) + a.shape[1:], lambda i: (i, 0, 0))
    return pl.pallas_call(
        functools.partial(_attn_sample_kernel, nseq=nseq, seqlen=seqlen),
        grid=(n // nseq,),
        in_specs=[_full(sink_rows.shape), blk(q), blk(kn), blk(vn), blk(kc), blk(vc)],
        out_specs=[blk(q), blk(kc), blk(vc)],
        out_shape=[jax.ShapeDtypeStruct(q.shape, F32), jax.ShapeDtypeStruct(kc.shape, F32),
                   jax.ShapeDtypeStruct(vc.shape, F32)],
        compiler_params=_cparams(("parallel",)),
        name="attn_sample",
    )(sink_rows, q, kn, vn, kc, vc)


def _odd_out_kernel(h_ref, o_ref, gate_ref, w_ref, y_ref):
    og = o_ref[...].astype(F32) * _silu(gate_ref[...].astype(F32))
    y_ref[...] = h_ref[...] + _dot(og.astype(BF16), w_ref[...])


def _odd_out(h, o, gate, w):
    m, dm = h.shape
    tm = _row_tile(m)
    row = lambda n: pl.BlockSpec((tm, n), lambda i: (i, 0))
    return pl.pallas_call(
        _odd_out_kernel,
        grid=(m // tm,),
        in_specs=[row(dm), row(o.shape[1]), row(gate.shape[1]), _full(w.shape)],
        out_specs=row(dm),
        out_shape=jax.ShapeDtypeStruct((m, dm), F32),
        compiler_params=_cparams(("parallel",)),
        name="odd_out",
    )(h, o, gate, w)


def _even_layer(x, s5_x0, gla_s0, p, nseq, seqlen, act_dtype):
    u4, ga, q, k, v, la, gb = _even_in(x, p['norm_g'], p['wm'], p['wa'], p['wgb'], p['wgate'], p['bgate'], act_dtype)
    y4, s5_fin = _s5_branch(u4, s5_x0, p['s5_ops'], nseq, seqlen, act_dtype)
    ob, gla_fin = _gla(q, k, v, la, gb, gla_s0, p['gla_norm_g'], nseq, seqlen, act_dtype)
    h = _even_out(x, y4, u4, ga, ob, p['d'], p['wglu'], p['bglu'], p['woa'], p['wob'])
    return h, s5_fin, gla_fin


def _odd_common(h, pos, p, act_dtype):
    cos_t, sin_t = _rope_tables(pos)
    return _odd_in(h, p['norm_g'], p['w_in'], p['gq'], p['gk'], cos_t, sin_t, act_dtype)


def kernel(x_prompt, x_sample, state_s5_re, state_s5_im, state_gla, cache_swa_k, cache_swa_v,
           even_norm_g, even_w_in, s5_lambda_re, s5_lambda_im, s5_log_dt, s5_b_re, s5_b_im,
           s5_c_re, s5_c_im, s5_d, s5_w_glu, s5_b_glu, gla_w_gate, gla_b_gate, gla_norm_g,
           even_w_out, odd_norm_g, odd_w_in, swa_q_norm_g, swa_k_norm_g, swa_sinks, odd_w_out):
    nb, seq, dm = x_prompt.shape
    ns, dseq, _ = x_sample.shape
    ng = s5_lambda_re.shape[1]
    no = ng // OCT
    xp = x_prompt.reshape(nb * seq, dm)
    xs = x_sample.reshape(ns * dseq, dm)

    i = 0
    w_in = even_w_in[i]
    s5w = ng * S5_GROUP
    col_alow = 2 * s5w + 2 * GLA_HEADS * GLA_HEAD_K + GLA_HEADS * GLA_HEAD_V
    col_gb = col_alow + GLA_GATE_RANK
    pad_rank = LANES - GLA_GATE_RANK
    pe = {
        'norm_g': even_norm_g[i][None, :],
        'wm': w_in[:, :col_alow].astype(BF16),
        'wa': jnp.pad(w_in[:, col_alow:col_gb], ((0, 0), (0, pad_rank))).astype(BF16),
        'wgb': w_in[:, col_gb:].astype(BF16),
        'wgate': jnp.pad(gla_w_gate[i], ((0, pad_rank), (0, 0))).astype(BF16),
        'bgate': gla_b_gate[i][None, :],
        's5_ops': _s5_params(s5_lambda_re[i], s5_lambda_im[i], s5_log_dt[i], s5_b_re[i], s5_b_im[i],
                             s5_c_re[i], s5_c_im[i]),
        'gla_norm_g': gla_norm_g[i][None, :],
        'd': s5_d[i][None, :],
        'wglu': s5_w_glu[i].astype(BF16),
        'bglu': s5_b_glu[i][None, :],
        'woa': even_w_out[i][:s5w].astype(BF16),
        'wob': even_w_out[i][s5w:].astype(BF16),
    }
    s5_zero = jnp.zeros((no, nb, 2 * OCT * S5_STATE), F32)
    gla_zero = jnp.zeros((nb, GLA_HEADS, GLA_HEAD_K, GLA_HEAD_V), F32)
    hp, s5_p, gla_p = _even_layer(xp, s5_zero, gla_zero, pe, nb, seq, BF16)
    s5_init = _s5_state_in(state_s5_re[i], state_s5_im[i], no)
    hs, s5_s, gla_s = _even_layer(xs, s5_init, state_gla[i], pe, ns, dseq, F32)
    s5r_p, s5i_p = _s5_state_out(s5_p)
    s5r_s, s5i_s = _s5_state_out(s5_s)

    po = {
        'norm_g': odd_norm_g[i][None, :],
        'w_in': odd_w_in[i].astype(BF16),
        'gq': jnp.tile(swa_q_norm_g[i], MXU_TILE // SWA_HEAD_DIM)[None, :],
        'gk': jnp.tile(swa_k_norm_g[i], LANES // SWA_HEAD_DIM)[None, :],
    }
    w_out = odd_w_out[i].astype(BF16)
    sinks = swa_sinks[i]
    kvw = SWA_KV_HEADS * SWA_HEAD_DIM

    q, k, v, gate = _odd_common(hp, jnp.arange(seq, dtype=jnp.int32), po, BF16)
    o = _attn_prompt(sinks, q, k, v, nb, seq)
    y_prompt = _odd_out(hp, o, gate, w_out).reshape(nb, seq, dm)
    cache_len = min(SWA_WINDOW, seq)
    swk_p = k.reshape(nb, seq, SWA_KV_HEADS, SWA_HEAD_DIM)[None, :, seq - cache_len:]
    swv_p = v.reshape(nb, seq, SWA_KV_HEADS, SWA_HEAD_DIM)[None, :, seq - cache_len:]

    pos_s = jnp.tile(PAST_LEN + jnp.arange(dseq, dtype=jnp.int32), ns)
    q, k, v, gate = _odd_common(hs, pos_s, po, F32)
    ncache = cache_swa_k.shape[2]
    qh = q.reshape(ns, dseq, SWA_KV_HEADS, SWA_GROUP, SWA_HEAD_DIM).transpose(0, 2, 3, 1, 4)
    zq = jnp.zeros_like(qh[:, 0])
    q2 = jnp.stack([jnp.concatenate([qh[:, 0], zq], axis=-1), jnp.concatenate([zq, qh[:, 1]], axis=-1)], axis=1)
    q2 = q2.reshape(ns, SWA_HEADS * dseq, kvw).astype(BF16)
    sink_rows = jnp.broadcast_to(jnp.repeat(sinks, dseq)[:, None], (SWA_HEADS * dseq, LANES))
    kn = k.reshape(ns, dseq, kvw)
    vn = v.reshape(ns, dseq, kvw)
    npad = 16 - dseq
    kn_pad = jnp.pad(kn, ((0, 0), (0, npad), (0, 0)))
    vn_pad = jnp.pad(vn, ((0, 0), (0, npad), (0, 0)))
    kc = cache_swa_k[i].reshape(ns, ncache, kvw)
    vc = cache_swa_v[i].reshape(ns, ncache, kvw)
    o2, kc_new, vc_new = _attn_sample(sink_rows, q2, kn_pad, vn_pad, kc, vc, dseq)
    o2 = o2.reshape(ns, SWA_KV_HEADS, SWA_GROUP, dseq, SWA_KV_HEADS, SWA_HEAD_DIM)
    o = jnp.stack([o2[:, 0, :, :, 0], o2[:, 1, :, :, 1]], axis=1)
    o = o.transpose(0, 3, 1, 2, 4).reshape(ns * dseq, SWA_HEADS * SWA_HEAD_DIM)
    y_sample = _odd_out(hs, o, gate, w_out).reshape(ns, dseq, dm)
    swk_s = kc_new.reshape(1, ns, ncache, SWA_KV_HEADS, SWA_HEAD_DIM)
    swv_s = vc_new.reshape(1, ns, ncache, SWA_KV_HEADS, SWA_HEAD_DIM)

    return (y_prompt, y_sample,
            s5r_p, s5i_p, gla_p[None], swk_p, swv_p,
            s5r_s, s5i_s, gla_s[None], swk_s, swv_s)
```

```python
import functools
import math

import jax
import jax.numpy as jnp
from jax import lax
from jax.experimental import pallas as pl
from jax.experimental.pallas import tpu as pltpu

F32 = jnp.float32
BF16 = jnp.bfloat16

PAST_LEN = 8192
NORM_EPS = 1e-6
S5_GROUP = 16
S5_STATE = 64
S5_CHUNK = 16
GLA_HEADS = 4
GLA_HEAD_K = 64
GLA_HEAD_V = 128
GLA_GATE_RANK = 16
GLA_GATE_TAU = 16.0
GLA_CHUNK = 64
SWA_HEADS = 16
SWA_KV_HEADS = 2
SWA_GROUP = SWA_HEADS // SWA_KV_HEADS
SWA_HEAD_DIM = 64
SWA_WINDOW = 128
ROPE_THETA = 10000.0
LANES = 128
MXU_TILE = 256
OCT = LANES // S5_GROUP
ROW_TILE = 512
VMEM_LIMIT = 48 * 1024 * 1024

HIGHEST = lax.Precision.HIGHEST


def _cparams(sem):
    return pltpu.CompilerParams(dimension_semantics=sem, vmem_limit_bytes=VMEM_LIMIT)


def _full(shape):
    n = len(shape)
    return pl.BlockSpec(shape, lambda *_: (0,) * n)


def _dot(a, b):
    return jnp.dot(a, b, preferred_element_type=F32)


def _dot_nt(a, b):
    return lax.dot_general(a, b, (((1,), (1,)), ((), ())), preferred_element_type=F32)


def _dot_tn(a, b):
    return lax.dot_general(a, b, (((0,), (0,)), ((), ())), preferred_element_type=F32)


def _split_bf16(x):
    hi = x.astype(BF16)
    lo = (x - hi.astype(F32)).astype(BF16)
    return hi, lo


def _rms_rows(x, g):
    return x * lax.rsqrt(jnp.mean(x * x, axis=-1, keepdims=True) + NORM_EPS) * g


def _sigmoid(x):
    return 1.0 / (1.0 + jnp.exp(-x))


def _silu(x):
    return x * _sigmoid(x)


def _row_tile(m):
    return ROW_TILE if m % ROW_TILE == 0 else m


def _even_in_kernel(x_ref, g_ref, wm_ref, wa_ref, wgb_ref, wgate_ref, bgate_ref,
                    u_ref, ga_ref, q_ref, k_ref, v_ref, la_ref, gb_ref):
    xb = _rms_rows(x_ref[...], g_ref[...]).astype(BF16)

    def proj(lo, hi):
        return _dot(xb, wm_ref[:, lo:hi])

    u = proj(0, 512)
    for o in range(u_ref.shape[0]):
        u_ref[o] = u[:, o * LANES:(o + 1) * LANES].astype(u_ref.dtype)
    ga_ref[...] = proj(512, 1024).astype(ga_ref.dtype)
    q_ref[...] = (proj(1024, 1280) * (GLA_HEAD_K ** -0.5)).astype(q_ref.dtype)
    k_ref[...] = proj(1280, 1536).astype(k_ref.dtype)
    v_ref[...] = proj(1536, 2048).astype(v_ref.dtype)
    gb_ref[...] = _dot(xb, wgb_ref[...]).astype(gb_ref.dtype)
    a_low = _dot(xb, wa_ref[...])
    logit = _dot(a_low.astype(BF16), wgate_ref[...]) + bgate_ref[...]
    log_sig = jnp.minimum(logit, 0.0) - jnp.log1p(jnp.exp(-jnp.abs(logit)))
    la_ref[...] = log_sig * (1.0 / GLA_GATE_TAU)


def _even_in(x, g, wm, wa, wgb, wgate, bgate, act_dtype):
    m, d = x.shape
    tm = _row_tile(m)
    row = lambda n: pl.BlockSpec((tm, n), lambda i: (i, 0))
    no = 512 // LANES
    slab = pl.BlockSpec((no, tm, LANES), lambda i: (0, i, 0))
    outs = [(512, act_dtype), (256, act_dtype), (256, act_dtype), (512, act_dtype), (256, F32), (512, act_dtype)]
    return pl.pallas_call(
        _even_in_kernel,
        grid=(m // tm,),
        in_specs=[row(d), _full(g.shape), _full(wm.shape), _full(wa.shape), _full(wgb.shape),
                  _full(wgate.shape), _full(bgate.shape)],
        out_specs=[slab] + [row(n) for n, _ in outs],
        out_shape=[jax.ShapeDtypeStruct((no, m, LANES), act_dtype)]
        + [jax.ShapeDtypeStruct((m, n), dt) for n, dt in outs],
        compiler_params=_cparams(("parallel",)),
        name="even_in",
    )(x, g, wm, wa, wgb, wgate, bgate)


def _group_mask(shape, row_span, col_span):
    rg = (lax.broadcasted_iota(jnp.int32, shape, 0) // row_span) % OCT
    cg = (lax.broadcasted_iota(jnp.int32, shape, 1) // col_span) % OCT
    return rg == cg


def _s5_state_kernel(u_ref, bre_ref, bim_ref, x0_ref, are_ref, aim_ref, xs_ref, xf_ref, loc_ref, *, nseq, nchunks):
    hw = OCT * S5_STATE
    kk = u_ref.shape[2]
    rep = hw // LANES
    bp = jnp.concatenate([bre_ref[0]] * rep + [bim_ref[0]] * rep, axis=1)
    bp = jnp.where(_group_mask((kk, 2 * hw), S5_GROUP, S5_STATE), bp, 0.0).astype(BF16)
    loc_ref[...] = _dot(u_ref[0], bp)
    a_re = are_ref[0]
    a_im = aim_ref[0]
    if nchunks == 1:
        x0 = x0_ref[0]
        xr, xi = x0[:, :hw], x0[:, hw:]
        loc = loc_ref[...]
        xf_ref[0, :, :hw] = a_re * xr - a_im * xi + loc[:, :hw]
        xf_ref[0, :, hw:] = a_re * xi + a_im * xr + loc[:, hw:]
        xs_ref[0] = x0.astype(xs_ref.dtype)
    else:
        def body(j, carry):
            new = []
            for b in range(nseq):
                xr, xi = carry[b]
                row = pl.ds(b * nchunks + j, 1)
                lr = loc_ref[row, :hw]
                li = loc_ref[row, hw:]
                loc_ref[row, :hw] = xr
                loc_ref[row, hw:] = xi
                new.append((a_re * xr - a_im * xi + lr, a_re * xi + a_im * xr + li))
            return tuple(new)

        init = tuple((x0_ref[0, b:b + 1, :hw], x0_ref[0, b:b + 1, hw:]) for b in range(nseq))
        fin = lax.fori_loop(0, nchunks, body, init, unroll=4)
        for b in range(nseq):
            xf_ref[0, b:b + 1, :hw] = fin[b][0]
            xf_ref[0, b:b + 1, hw:] = fin[b][1]
        xs_ref[0] = loc_ref[...].astype(xs_ref.dtype)


def _s5_state(u2, bre, bim, x0, are, aim, nseq, nchunks):
    no, r, kk = u2.shape
    sw = x0.shape[2]
    blk = lambda a: pl.BlockSpec((1,) + a.shape[1:], lambda o: (o,) + (0,) * (a.ndim - 1))
    return pl.pallas_call(
        functools.partial(_s5_state_kernel, nseq=nseq, nchunks=nchunks),
        grid=(no,),
        in_specs=[blk(u2), blk(bre), blk(bim), blk(x0), blk(are), blk(aim)],
        out_specs=[pl.BlockSpec((1, r, sw), lambda o: (o, 0, 0)), blk(x0)],
        out_shape=[jax.ShapeDtypeStruct((no, r, sw), BF16), jax.ShapeDtypeStruct(x0.shape, F32)],
        scratch_shapes=[pltpu.VMEM((r, sw), F32)],
        compiler_params=_cparams(("parallel",)),
        name="s5_state",
    )(u2, bre, bim, x0, are, aim)


def _s5_out_kernel(u_ref, xs_ref, wc_ref, cc_ref, y_ref):
    kk = u_ref.shape[2]
    sw = xs_ref.shape[2]
    ntile = kk // MXU_TILE
    cc = cc_ref[0]
    cp = jnp.concatenate([cc[:S5_STATE]] * OCT + [cc[S5_STATE:]] * OCT, axis=0)
    cp = jnp.where(_group_mask((sw, kk), S5_STATE, S5_GROUP), cp, 0.0).astype(BF16)
    wmask = _group_mask((MXU_TILE, MXU_TILE), S5_GROUP, S5_GROUP)
    wts = []
    for d in range(ntile):
        wc = wc_ref[0, d]
        full = jnp.concatenate([wc[:S5_GROUP]] * OCT + [wc[S5_GROUP:]] * OCT, axis=0)
        wts.append(jnp.where(wmask, full, 0.0).astype(BF16))
    xs = xs_ref[0]
    for n in range(ntile):
        cols = slice(n * MXU_TILE, (n + 1) * MXU_TILE)
        acc = _dot(xs, cp[:, cols])
        for k in range(n + 1):
            acc = acc + _dot(u_ref[0, :, k * MXU_TILE:(k + 1) * MXU_TILE], wts[n - k])
        y_ref[0, :, cols] = acc.astype(y_ref.dtype)


def _s5_out(u2, xs, wc, cc, out_dtype):
    no, r, kk = u2.shape
    blk = lambda a: pl.BlockSpec((1,) + a.shape[1:], lambda o: (o,) + (0,) * (a.ndim - 1))
    return pl.pallas_call(
        _s5_out_kernel,
        grid=(no,),
        in_specs=[blk(u2), blk(xs), blk(wc), blk(cc)],
        out_specs=blk(u2),
        out_shape=jax.ShapeDtypeStruct(u2.shape, out_dtype),
        compiler_params=_cparams(("parallel",)),
        name="s5_out",
    )(u2, xs, wc, cc)


def _s5_params(lam_re, lam_im, log_dt, b_re, b_im, c_re, c_im):
    t = S5_CHUNK
    ng = lam_re.shape[0]
    no = ng // OCT
    dt = jnp.exp(log_dt)[:, None]
    a = lam_re * dt
    b = lam_im * dt
    n = jnp.arange(t + 1, dtype=F32)[None, :, None]
    mag = jnp.exp(n * a[:, None, :])
    pw_re = mag * jnp.cos(n * b[:, None, :])
    pw_im = mag * jnp.sin(n * b[:, None, :])
    em1_re = jnp.expm1(a) * jnp.cos(b) - 2.0 * jnp.sin(0.5 * b) ** 2
    em1_im = jnp.exp(a) * jnp.sin(b)
    den = lam_re * lam_re + lam_im * lam_im
    z_re = (em1_re * lam_re + em1_im * lam_im) / den
    z_im = (em1_im * lam_re - em1_re * lam_im) / den
    bb_re = z_re[..., None] * b_re - z_im[..., None] * b_im
    bb_im = z_re[..., None] * b_im + z_im[..., None] * b_re
    e_re, e_im = pw_re[:, :t, :, None], pw_im[:, :t, :, None]
    w_re = e_re * bb_re[:, None] - e_im * bb_im[:, None]
    w_im = e_re * bb_im[:, None] + e_im * bb_re[:, None]
    kd = (jnp.einsum('gdpc,gkp->gdck', w_re, c_re, precision=HIGHEST)
          - jnp.einsum('gdpc,gkp->gdck', w_im, c_im, precision=HIGHEST))
    kd_pad = jnp.pad(kd, ((0, 0), (1, 0), (0, 0), (0, 0)))
    nl = t // 2
    lag1 = (2 * jnp.arange(nl)[:, None, None] + jnp.arange(2)[None, None, :] - jnp.arange(2)[None, :, None] + 1)
    x = kd_pad[:, lag1].reshape(no, OCT, nl, 2, 2, S5_GROUP, S5_GROUP)
    wc = x.transpose(0, 2, 3, 5, 4, 1, 6).reshape(no, nl, 2 * S5_GROUP, MXU_TILE)

    def in_op(w):
        w = w[:, ::-1].transpose(0, 1, 3, 2).reshape(no, OCT, t, S5_GROUP, S5_STATE).transpose(0, 2, 1, 3, 4)
        return jnp.concatenate([w, w], axis=-1).reshape(no, t * LANES, 2 * S5_STATE)

    bre, bim = in_op(w_re), in_op(w_im)
    o_re, o_im = pw_re[:, 1:, None, :], pw_im[:, 1:, None, :]
    cl_re = c_re[:, None] * o_re - c_im[:, None] * o_im
    cl_im = c_re[:, None] * o_im + c_im[:, None] * o_re

    def out_op(cl):
        cl = cl.reshape(no, OCT, t, S5_GROUP, S5_STATE).transpose(0, 4, 2, 1, 3)
        return cl.reshape(no, S5_STATE, t * LANES)

    cc = jnp.concatenate([out_op(cl_re), -out_op(cl_im)], axis=1)
    return wc, bre, bim, cc, pw_re, pw_im


def _s5_branch(u4, x0, ops, nseq, seqlen, out_dtype):
    wc, bre, bim, cc, pw_re, pw_im = ops
    no = u4.shape[0]
    t = math.gcd(seqlen, S5_CHUNK)
    kk = t * LANES
    nchunks = seqlen // t
    r = nseq * nchunks
    u2 = u4.reshape(no, r, kk)
    are = pw_re[:, t].reshape(no, 1, OCT * S5_STATE)
    aim = pw_im[:, t].reshape(no, 1, OCT * S5_STATE)
    tail = S5_CHUNK * LANES - kk
    xs, xf = _s5_state(u2, bre[:, tail:], bim[:, tail:], x0, are, aim, nseq, nchunks)
    y2 = _s5_out(u2, xs, wc[:, :kk // MXU_TILE], cc[:, :, :kk], out_dtype)
    return y2.reshape(no, nseq * seqlen, LANES), xf


def _s5_state_in(re, im, no):
    nseq = re.shape[0]
    f = lambda a: a.reshape(nseq, no, OCT * S5_STATE).transpose(1, 0, 2)
    return jnp.concatenate([f(re), f(im)], axis=-1)


def _s5_state_out(xf):
    no, nseq, _ = xf.shape
    hw = OCT * S5_STATE
    f = lambda a: a.transpose(1, 0, 2).reshape(1, nseq, no * OCT, S5_STATE)
    return f(xf[:, :, :hw]), f(xf[:, :, hw:])


def _gla_kernel(q_ref, k_ref, v_ref, la_ref, gb_ref, s0_ref, ng_ref, o_ref, sf_ref, st_ref,
                *, nseq, nchunks, c):
    i = pl.program_id(1)

    @pl.when(i == 0)
    def _():
        st_ref[...] = s0_ref[...]

    nh, hk, hv = GLA_HEADS, GLA_HEAD_K, GLA_HEAD_V
    nch = nseq * nchunks
    tm = nch * c
    iota = lambda shape, d: lax.broadcasted_iota(jnp.int32, shape, d)
    ng = ng_ref[...]

    la_hi, la_lo = _split_bf16(la_ref[...])
    rt, ct = iota((tm, tm), 0), iota((tm, tm), 1)
    tril = ((rt // c == ct // c) & (rt >= ct)).astype(BF16)
    bcum = _dot(tril, la_hi) + _dot(tril, la_lo)
    seg = (iota((tm, nch * LANES), 0) // c == iota((tm, nch * LANES), 1) // LANES).astype(BF16)
    dec_all = jnp.exp(_dot_tn(la_hi, seg) + _dot_tn(la_lo, seg))
    q_all = q_ref[...].astype(F32)
    k_all = k_ref[...].astype(F32)
    q_dec_all = q_all * jnp.exp(bcum)
    k_dec_all = k_all * jnp.exp(-bcum)

    own_k = iota((nh * c, nh * hk), 0) // c == iota((nh * c, nh * hk), 1) // hk
    own_v = iota((nh * c, nh * hv), 0) // c == iota((nh * c, nh * hv), 1) // hv
    causal = iota((c, nh * c), 1) % c <= iota((c, nh * c), 0)
    zero_v = jnp.zeros((hk, hv), F32)

    sts = None
    for ci in range(nch):
        s, first, last = ci // nchunks, ci % nchunks == 0, ci % nchunks == nchunks - 1
        if first:
            sts = [st_ref[s, h] for h in range(nh)]
        rows = slice(ci * c, (ci + 1) * c)
        q_dec = q_dec_all[rows].astype(BF16)
        k_dec = k_dec_all[rows]
        b_c = bcum[rows]
        k_tail = (k_all[rows] * jnp.exp(b_c[c - 1:c] - b_c)).astype(BF16)
        v = v_ref[rows, :].astype(F32)
        k_bd = jnp.where(own_k, jnp.concatenate([k_dec] * nh, axis=0), 0.0).astype(BF16)
        v_bd = jnp.where(own_v, jnp.concatenate([v] * nh, axis=0), 0.0).astype(BF16)
        st_bd = jnp.concatenate(
            [jnp.concatenate([zero_v] * h + [sts[h]] + [zero_v] * (nh - 1 - h), axis=1) for h in range(nh)],
            axis=0).astype(BF16)
        att = jnp.where(causal, _dot_nt(q_dec, k_bd), 0.0)
        o = _dot(att.astype(BF16), v_bd) + _dot(q_dec, st_bd)
        kv = _dot_tn(k_tail, v.astype(BF16))
        for h in range(nh):
            ks = slice(h * hk, (h + 1) * hk)
            vs = slice(h * hv, (h + 1) * hv)
            sts[h] = dec_all[ks, ci * LANES:(ci + 1) * LANES] * sts[h] + kv[ks, vs]
            oh = _rms_rows(o[:, vs], ng) * _silu(gb_ref[rows, vs].astype(F32))
            o_ref[rows, vs] = oh.astype(o_ref.dtype)
        if last:
            for h in range(nh):
                st_ref[s, h] = sts[h]

    @pl.when(i == pl.num_programs(1) - 1)
    def _():
        sf_ref[...] = st_ref[...]


def _gla(q, k, v, la, gb, s0, ng, nseq_total, seqlen, out_dtype):
    c = math.gcd(seqlen, GLA_CHUNK)
    if seqlen >= 8 * c:
        nseq, nchunks = 1, 8
    else:
        nseq, nchunks = 8, seqlen // c
    assert nseq_total % nseq == 0 and seqlen % (nchunks * c) == 0
    nblk = seqlen // (nchunks * c)
    tm = nseq * nchunks * c
    row = lambda n: pl.BlockSpec((tm, n), lambda b, i: (b * nblk + i, 0))
    st_spec = pl.BlockSpec((nseq, GLA_HEADS, GLA_HEAD_K, GLA_HEAD_V), lambda b, i: (b, 0, 0, 0))
    m = q.shape[0]
    return pl.pallas_call(
        functools.partial(_gla_kernel, nseq=nseq, nchunks=nchunks, c=c),
        grid=(nseq_total // nseq, nblk),
        in_specs=[row(256), row(256), row(512), row(256), row(512), st_spec, _full(ng.shape)],
        out_specs=[row(512), st_spec],
        out_shape=[jax.ShapeDtypeStruct((m, 512), out_dtype), jax.ShapeDtypeStruct(s0.shape, F32)],
        scratch_shapes=[pltpu.VMEM((nseq, GLA_HEADS, GLA_HEAD_K, GLA_HEAD_V), F32)],
        compiler_params=_cparams(("parallel", "arbitrary")),
        name="gla",
    )(q, k, v, la, gb, s0, ng)


def _gelu_tanh(x):
    return 0.5 * x * (1.0 + jnp.tanh(math.sqrt(2.0 / math.pi) * (x + 0.044715 * (x * x * x))))


def _even_out_kernel(x_ref, y_ref, u_ref, ga_ref, ob_ref, d_ref, wglu_ref, bglu_ref, woa_ref, wob_ref, h_ref):
    no = y_ref.shape[0]
    y = jnp.concatenate([y_ref[o].astype(F32) for o in range(no)], axis=1)
    u = jnp.concatenate([u_ref[o].astype(F32) for o in range(no)], axis=1)
    z = _gelu_tanh(y + d_ref[...] * u)
    z = z * _sigmoid(_dot(z.astype(BF16), wglu_ref[...]) + bglu_ref[...])
    out_a = z * _silu(ga_ref[...].astype(F32))
    mix = _dot(out_a.astype(BF16), woa_ref[...]) + _dot(ob_ref[...].astype(BF16), wob_ref[...])
    h_ref[...] = x_ref[...] + mix


def _even_out(x, y, u, ga, ob, d, wglu, bglu, woa, wob):
    m, dm = x.shape
    tm = _row_tile(m)
    row = lambda n: pl.BlockSpec((tm, n), lambda i: (i, 0))
    slab = pl.BlockSpec((y.shape[0], tm, LANES), lambda i: (0, i, 0))
    return pl.pallas_call(
        _even_out_kernel,
        grid=(m // tm,),
        in_specs=[row(dm), slab, slab, row(512), row(512), _full(d.shape), _full(wglu.shape),
                  _full(bglu.shape), _full(woa.shape), _full(wob.shape)],
        out_specs=row(dm),
        out_shape=jax.ShapeDtypeStruct((m, dm), F32),
        compiler_params=_cparams(("parallel",)),
        name="even_out",
    )(x, y, u, ga, ob, d, wglu, bglu, woa, wob)


def _head_ones(n):
    r = lax.broadcasted_iota(jnp.int32, (n, n), 0) // SWA_HEAD_DIM
    c = lax.broadcasted_iota(jnp.int32, (n, n), 1) // SWA_HEAD_DIM
    return (r == c).astype(BF16)


def _rope_block(x, cos_t, sin_t, upper):
    swapped = jnp.where(upper, pltpu.roll(x, 32, 1), pltpu.roll(x, 96, 1))
    return x * cos_t + swapped * sin_t


def _odd_in_kernel(h_ref, g_ref, w_ref, gq_ref, gk_ref, cos_ref, sin_ref, q_ref, k_ref, v_ref, gate_ref):
    xb = _rms_rows(h_ref[...], g_ref[...]).astype(BF16)
    cos_t = cos_ref[...]
    sin_t = sin_ref[...]
    tm = xb.shape[0]
    lane = lax.broadcasted_iota(jnp.int32, (tm, LANES), 1)
    upper = (lane & 32) != 0
    ones4 = _head_ones(MXU_TILE)
    inv_d = 1.0 / SWA_HEAD_DIM
    qw = SWA_HEADS * SWA_HEAD_DIM
    kw = SWA_KV_HEADS * SWA_HEAD_DIM
    for j in range(qw // MXU_TILE):
        q = _dot(xb, w_ref[:, j * MXU_TILE:(j + 1) * MXU_TILE])
        ss = _dot((q * q).astype(BF16), ones4)
        qn = q * lax.rsqrt(ss * inv_d + NORM_EPS) * gq_ref[...]
        for e in range(MXU_TILE // LANES):
            cols = slice(j * MXU_TILE + e * LANES, j * MXU_TILE + (e + 1) * LANES)
            qe = _rope_block(qn[:, e * LANES:(e + 1) * LANES], cos_t, sin_t, upper)
            q_ref[:, cols] = (qe * (SWA_HEAD_DIM ** -0.5)).astype(q_ref.dtype)
    kv = _dot(xb, w_ref[:, qw:qw + 2 * kw])
    k = kv[:, :kw]
    ss = _dot((k * k).astype(BF16), ones4[:kw, :kw])
    kn = k * lax.rsqrt(ss * inv_d + NORM_EPS) * gk_ref[...]
    k_ref[...] = _rope_block(kn, cos_t, sin_t, upper)
    v_ref[...] = kv[:, kw:]
    gate_ref[...] = _dot(xb, w_ref[:, qw + 2 * kw:]).astype(gate_ref.dtype)


def _odd_in(h, g, w, gq, gk, cos_t, sin_t, act_dtype):
    m, dm = h.shape
    tm = _row_tile(m)
    row = lambda n: pl.BlockSpec((tm, n), lambda i: (i, 0))
    assert cos_t.shape[0] % tm == 0
    nper = cos_t.shape[0] // tm
    tab = pl.BlockSpec((tm, LANES), lambda i: (i % nper, 0))
    qw = SWA_HEADS * SWA_HEAD_DIM
    kw = SWA_KV_HEADS * SWA_HEAD_DIM
    return pl.pallas_call(
        _odd_in_kernel,
        grid=(m // tm,),
        in_specs=[row(dm), _full(g.shape), _full(w.shape), _full(gq.shape), _full(gk.shape), tab, tab],
        out_specs=[row(qw), row(kw), row(kw), row(qw)],
        out_shape=[jax.ShapeDtypeStruct((m, qw), act_dtype), jax.ShapeDtypeStruct((m, kw), F32),
                   jax.ShapeDtypeStruct((m, kw), F32), jax.ShapeDtypeStruct((m, qw), act_dtype)],
        compiler_params=_cparams(("parallel",)),
        name="odd_in",
    )(h, g, w, gq, gk, cos_t, sin_t)


def _rope_tables(pos):
    half = SWA_HEAD_DIM // 2
    inv_freq = ROPE_THETA ** (-jnp.arange(half, dtype=F32) / half)
    ang = pos.astype(F32)[:, None] * inv_freq[None, :]
    cos, sin = jnp.cos(ang), jnp.sin(ang)
    return jnp.concatenate([cos, cos, cos, cos], axis=1), jnp.concatenate([-sin, sin, -sin, sin], axis=1)


def _attn_prompt_kernel(sink_ref, q_ref, kc_ref, kp_ref, vc_ref, vp_ref, o_ref, p_ref, *, nqb):
    i = pl.program_id(1)
    w = SWA_WINDOW
    hd = SWA_HEAD_DIM
    npair = SWA_GROUP // 2
    lane2 = lax.broadcasted_iota(jnp.int32, (2 * w, LANES), 1)
    low = lane2 < hd
    rr = lax.broadcasted_iota(jnp.int32, (w, w), 0)
    cc = lax.broadcasted_iota(jnp.int32, (w, w), 1)
    tri = cc <= rr
    low_w = cc < hd
    r4 = lax.broadcasted_iota(jnp.int32, (4 * w, LANES), 0)
    c4 = lax.broadcasted_iota(jnp.int32, (4 * w, LANES), 1)
    den_cols = ((r4 < 2 * w) == (c4 < hd)).astype(BF16)
    unit = 0
    for jb in range(nqb):
        rows = slice(jb * w, (jb + 1) * w)
        if jb == 0:
            k_prev, v_prev = kp_ref[...], vp_ref[...]
            has_prev = i > 0
        else:
            prev_rows = slice((jb - 1) * w, jb * w)
            k_prev, v_prev = kc_ref[prev_rows, :], vc_ref[prev_rows, :]
            has_prev = None
        kcat = jnp.concatenate([k_prev, kc_ref[rows, :]], axis=0)
        vcat = jnp.concatenate([v_prev, vc_ref[rows, :]], axis=0)
        for kv in range(SWA_KV_HEADS):
            own = low if kv == 0 else jnp.logical_not(low)
            k_own = jnp.where(own, kcat, 0.0)
            v_own = jnp.where(own, vcat, 0.0)
            k_oth = pltpu.roll(k_own, hd, 1)
            v_oth = pltpu.roll(v_own, hd, 1)
            k_lo, k_hi = (k_own, k_oth) if kv == 0 else (k_oth, k_own)
            v_lo, v_hi = (v_own, v_oth) if kv == 0 else (v_oth, v_own)
            k_rhs = jnp.concatenate([k_lo, k_hi], axis=0).astype(BF16)
            v_rhs = jnp.concatenate([jnp.concatenate([v_lo, v_hi], axis=0).astype(BF16), den_cols], axis=1)
            qs = jnp.concatenate([q_ref[rows, (kv * npair + pr) * LANES:(kv * npair + pr + 1) * LANES]
                                  for pr in range(npair)], axis=0)
            s_all = _dot_nt(qs, k_rhs)
            buf = unit % 2
            unit += 1
            sink_terms = []
            for pr in range(npair):
                prow = slice(pr * w, (pr + 1) * w)
                pair_terms = []
                for e in range(2):
                    s_prev = s_all[prow, (2 * e) * w:(2 * e + 1) * w]
                    s_cur = s_all[prow, (2 * e + 1) * w:(2 * e + 2) * w]
                    if has_prev is not None:
                        s_prev = jnp.where(has_prev, s_prev, -jnp.inf)
                    sc = jnp.where(tri, s_cur, s_prev)
                    sink = sink_ref[2 * (kv * npair + pr) + e]
                    mx = jnp.maximum(jnp.max(sc, axis=-1, keepdims=True), sink)
                    pe = jnp.exp(sc - mx)
                    p_ref[buf, prow, (2 * e) * w:(2 * e + 1) * w] = jnp.where(tri, 0.0, pe).astype(BF16)
                    p_ref[buf, prow, (2 * e + 1) * w:(2 * e + 2) * w] = jnp.where(tri, pe, 0.0).astype(BF16)
                    pair_terms.append(jnp.exp(sink - mx))
                sink_terms.append(pair_terms)
            o_ext = _dot(p_ref[buf], v_rhs)
            for pr in range(npair):
                prow = slice(pr * w, (pr + 1) * w)
                cols = slice((kv * npair + pr) * LANES, (kv * npair + pr + 1) * LANES)
                st = jnp.where(low_w, sink_terms[pr][0], sink_terms[pr][1])
                o = o_ext[prow, :LANES] / (o_ext[prow, LANES:] + st)
                o_ref[rows, cols] = o.astype(o_ref.dtype)


def _attn_prompt(sinks, q, k, v, nseq, seqlen):
    w = SWA_WINDOW
    nqb = 4
    tm = nqb * w
    assert seqlen % tm == 0
    nblk = seqlen // tm
    qw = q.shape[1]
    row = lambda n: pl.BlockSpec((tm, n), lambda b, i: (b * nblk + i, 0))
    prev = lambda n: pl.BlockSpec((w, n), lambda b, i: (jnp.maximum((b * nblk + i) * nqb - 1, 0), 0))
    smem = pl.BlockSpec(memory_space=pltpu.SMEM)
    return pl.pallas_call(
        functools.partial(_attn_prompt_kernel, nqb=nqb),
        grid=(nseq, nblk),
        in_specs=[smem, row(qw), row(LANES), prev(LANES), row(LANES), prev(LANES)],
        out_specs=row(qw),
        out_shape=jax.ShapeDtypeStruct(q.shape, BF16),
        scratch_shapes=[pltpu.VMEM((2, 4 * w, 4 * w), BF16)],
        compiler_params=_cparams(("parallel", "arbitrary")),
        name="attn_prompt",
    )(sinks, q, k, k, v, v)


def _attn_sample_kernel(sink_ref, q_ref, kn_ref, vn_ref, kc_ref, vc_ref, o_ref, ko_ref, vo_ref, *, nseq, seqlen):
    nq = SWA_HEADS * seqlen
    ncache = kc_ref.shape[1]
    t_row = lax.broadcasted_iota(jnp.int32, (nq, ncache), 0) % seqlen
    c_col = lax.broadcasted_iota(jnp.int32, (nq, ncache), 1)
    cache_ok = c_col > t_row - (SWA_WINDOW - ncache)
    nnew = kn_ref.shape[1]
    t_row_n = lax.broadcasted_iota(jnp.int32, (nq, nnew), 0) % seqlen
    n_col = lax.broadcasted_iota(jnp.int32, (nq, nnew), 1)
    new_ok = n_col <= t_row_n
    sink = sink_ref[...]
    for s in range(nseq):
        q = q_ref[s]
        kn = kn_ref[s].astype(BF16)
        vn = vn_ref[s].astype(BF16)
        sc = jnp.where(cache_ok, _dot_nt(q, kc_ref[s].astype(BF16)), -jnp.inf)
        sn = jnp.where(new_ok, _dot_nt(q, kn), -jnp.inf)
        mx = jnp.maximum(jnp.maximum(jnp.max(sc, axis=-1, keepdims=True), jnp.max(sn, axis=-1, keepdims=True)),
                         sink[:, 0:1])
        pc = jnp.exp(sc - mx)
        pn = jnp.exp(sn - mx)
        den = (jnp.sum(pc, axis=-1, keepdims=True) + jnp.sum(pn, axis=-1, keepdims=True)
               + jnp.exp(sink[:, 0:1] - mx))
        inv = 1.0 / den
        o = _dot((pc * inv).astype(BF16), vc_ref[s].astype(BF16)) + _dot((pn * inv).astype(BF16), vn)
        o_ref[s] = o
        keep = ncache - seqlen
        ko_ref[s, 0:keep, :] = kc_ref[s, seqlen:ncache, :]
        ko_ref[s, keep:ncache, :] = kn_ref[s, 0:seqlen, :]
        vo_ref[s, 0:keep, :] = vc_ref[s, seqlen:ncache, :]
        vo_ref[s, keep:ncache, :] = vn_ref[s, 0:seqlen, :]


def _attn_sample(sink_rows, q, kn, vn, kc, vc, seqlen):
    n = q.shape[0]
    nseq = 8
    assert n % nseq == 0
    blk = lambda a: pl.BlockSpec((nseq,) + a.shape[1:], lambda i: (i, 0, 0))
    return pl.pallas_call(
        functools.partial(_attn_sample_kernel, nseq=nseq, seqlen=seqlen),
        grid=(n // nseq,),
        in_specs=[_full(sink_rows.shape), blk(q), blk(kn), blk(vn), blk(kc), blk(vc)],
        out_specs=[blk(q), blk(kc), blk(vc)],
        out_shape=[jax.ShapeDtypeStruct(q.shape, F32), jax.ShapeDtypeStruct(kc.shape, F32),
                   jax.ShapeDtypeStruct(vc.shape, F32)],
        compiler_params=_cparams(("parallel",)),
        name="attn_sample",
    )(sink_rows, q, kn, vn, kc, vc)


def _odd_out_kernel(h_ref, o_ref, gate_ref, w_ref, y_ref):
    og = o_ref[...].astype(F32) * _silu(gate_ref[...].astype(F32))
    y_ref[...] = h_ref[...] + _dot(og.astype(BF16), w_ref[...])


def _odd_out(h, o, gate, w):
    m, dm = h.shape
    tm = _row_tile(m)
    row = lambda n: pl.BlockSpec((tm, n), lambda i: (i, 0))
    return pl.pallas_call(
        _odd_out_kernel,
        grid=(m // tm,),
        in_specs=[row(dm), row(o.shape[1]), row(gate.shape[1]), _full(w.shape)],
        out_specs=row(dm),
        out_shape=jax.ShapeDtypeStruct((m, dm), F32),
        compiler_params=_cparams(("parallel",)),
        name="odd_out",
    )(h, o, gate, w)


def _even_layer(x, s5_x0, gla_s0, p, nseq, seqlen, act_dtype):
    u4, ga, q, k, v, la, gb = _even_in(x, p['norm_g'], p['wm'], p['wa'], p['wgb'], p['wgate'], p['bgate'], act_dtype)
    y4, s5_fin = _s5_branch(u4, s5_x0, p['s5_ops'], nseq, seqlen, act_dtype)
    ob, gla_fin = _gla(q, k, v, la, gb, gla_s0, p['gla_norm_g'], nseq, seqlen, act_dtype)
    h = _even_out(x, y4, u4, ga, ob, p['d'], p['wglu'], p['bglu'], p['woa'], p['wob'])
    return h, s5_fin, gla_fin


def _odd_common(h, pos, p, act_dtype):
    cos_t, sin_t = _rope_tables(pos)
    return _odd_in(h, p['norm_g'], p['w_in'], p['gq'], p['gk'], cos_t, sin_t, act_dtype)


def kernel(x_prompt, x_sample, state_s5_re, state_s5_im, state_gla, cache_swa_k, cache_swa_v,
           even_norm_g, even_w_in, s5_lambda_re, s5_lambda_im, s5_log_dt, s5_b_re, s5_b_im,
           s5_c_re, s5_c_im, s5_d, s5_w_glu, s5_b_glu, gla_w_gate, gla_b_gate, gla_norm_g,
           even_w_out, odd_norm_g, odd_w_in, swa_q_norm_g, swa_k_norm_g, swa_sinks, odd_w_out):
    nb, seq, dm = x_prompt.shape
    ns, dseq, _ = x_sample.shape
    ng = s5_lambda_re.shape[1]
    no = ng // OCT
    xp = x_prompt.reshape(nb * seq, dm)
    xs = x_sample.reshape(ns * dseq, dm)

    i = 0
    w_in = even_w_in[i]
    s5w = ng * S5_GROUP
    col_alow = 2 * s5w + 2 * GLA_HEADS * GLA_HEAD_K + GLA_HEADS * GLA_HEAD_V
    col_gb = col_alow + GLA_GATE_RANK
    pad_rank = LANES - GLA_GATE_RANK
    pe = {
        'norm_g': even_norm_g[i][None, :],
        'wm': w_in[:, :col_alow].astype(BF16),
        'wa': jnp.pad(w_in[:, col_alow:col_gb], ((0, 0), (0, pad_rank))).astype(BF16),
        'wgb': w_in[:, col_gb:].astype(BF16),
        'wgate': jnp.pad(gla_w_gate[i], ((0, pad_rank), (0, 0))).astype(BF16),
        'bgate': gla_b_gate[i][None, :],
        's5_ops': _s5_params(s5_lambda_re[i], s5_lambda_im[i], s5_log_dt[i], s5_b_re[i], s5_b_im[i],
                             s5_c_re[i], s5_c_im[i]),
        'gla_norm_g': gla_norm_g[i][None, :],
        'd': s5_d[i][None, :],
        'wglu': s5_w_glu[i].astype(BF16),
        'bglu': s5_b_glu[i][None, :],
        'woa': even_w_out[i][:s5w].astype(BF16),
        'wob': even_w_out[i][s5w:].astype(BF16),
    }
    s5_zero = jnp.zeros((no, nb, 2 * OCT * S5_STATE), F32)
    gla_zero = jnp.zeros((nb, GLA_HEADS, GLA_HEAD_K, GLA_HEAD_V), F32)
    hp, s5_p, gla_p = _even_layer(xp, s5_zero, gla_zero, pe, nb, seq, BF16)
    s5_init = _s5_state_in(state_s5_re[i], state_s5_im[i], no)
    hs, s5_s, gla_s = _even_layer(xs, s5_init, state_gla[i], pe, ns, dseq, F32)
    s5r_p, s5i_p = _s5_state_out(s5_p)
    s5r_s, s5i_s = _s5_state_out(s5_s)

    po = {
        'norm_g': odd_norm_g[i][None, :],
        'w_in': odd_w_in[i].astype(BF16),
        'gq': jnp.tile(swa_q_norm_g[i], MXU_TILE // SWA_HEAD_DIM)[None, :],
        'gk': jnp.tile(swa_k_norm_g[i], LANES // SWA_HEAD_DIM)[None, :],
    }
    w_out = odd_w_out[i].astype(BF16)
    sinks = swa_sinks[i]
    kvw = SWA_KV_HEADS * SWA_HEAD_DIM

    q, k, v, gate = _odd_common(hp, jnp.arange(seq, dtype=jnp.int32), po, BF16)
    o = _attn_prompt(sinks, q, k, v, nb, seq)
    y_prompt = _odd_out(hp, o, gate, w_out).reshape(nb, seq, dm)
    cache_len = min(SWA_WINDOW, seq)
    swk_p = k.reshape(nb, seq, SWA_KV_HEADS, SWA_HEAD_DIM)[None, :, seq - cache_len:]
    swv_p = v.reshape(nb, seq, SWA_KV_HEADS, SWA_HEAD_DIM)[None, :, seq - cache_len:]

    pos_s = jnp.tile(PAST_LEN + jnp.arange(dseq, dtype=jnp.int32), ns)
    q, k, v, gate = _odd_common(hs, pos_s, po, F32)
    ncache = cache_swa_k.shape[2]
    qh = q.reshape(ns, dseq, SWA_KV_HEADS, SWA_GROUP, SWA_HEAD_DIM).transpose(0, 2, 3, 1, 4)
    zq = jnp.zeros_like(qh[:, 0])
    q2 = jnp.stack([jnp.concatenate([qh[:, 0], zq], axis=-1), jnp.concatenate([zq, qh[:, 1]], axis=-1)], axis=1)
    q2 = q2.reshape(ns, SWA_HEADS * dseq, kvw).astype(BF16)
    sink_rows = jnp.broadcast_to(jnp.repeat(sinks, dseq)[:, None], (SWA_HEADS * dseq, LANES))
    kn = k.reshape(ns, dseq, kvw)
    vn = v.reshape(ns, dseq, kvw)
    npad = 16 - dseq
    kn_pad = jnp.pad(kn, ((0, 0), (0, npad), (0, 0)))
    vn_pad = jnp.pad(vn, ((0, 0), (0, npad), (0, 0)))
    kc = cache_swa_k[i].reshape(ns, ncache, kvw)
    vc = cache_swa_v[i].reshape(ns, ncache, kvw)
    o2, kc_new, vc_new = _attn_sample(sink_rows, q2, kn_pad, vn_pad, kc, vc, dseq)
    o2 = o2.reshape(ns, SWA_KV_HEADS, SWA_GROUP, dseq, SWA_KV_HEADS, SWA_HEAD_DIM)
    o = jnp.stack([o2[:, 0, :, :, 0], o2[:, 1, :, :, 1]], axis=1)
    o = o.transpose(0, 3, 1, 2, 4).reshape(ns * dseq, SWA_HEADS * SWA_HEAD_DIM)
    y_sample = _odd_out(hs, o, gate, w_out).reshape(ns, dseq, dm)
    swk_s = kc_new.reshape(1, ns, ncache, SWA_KV_HEADS, SWA_HEAD_DIM)
    swv_s = vc_new.reshape(1, ns, ncache, SWA_KV_HEADS, SWA_HEAD_DIM)

    return (y_prompt, y_sample,
            s5r_p, s5i_p, gla_p[None], swk_p, swv_p,
            s5r_s, s5i_s, gla_s[None], swk_s, swv_s)
```

```python
import functools
import math

import jax
import jax.numpy as jnp
import numpy as np
from jax import lax
from jax.experimental import pallas as pl
from jax.experimental.pallas import tpu as pltpu

F32 = jnp.float32
BF16 = jnp.bfloat16

PAST_LEN = 8192
NORM_EPS = 1e-6
S5_GROUP = 16
S5_STATE = 64
S5_CHUNK = 16
GLA_HEADS = 4
GLA_HEAD_K = 64
GLA_HEAD_V = 128
GLA_GATE_RANK = 16
GLA_GATE_TAU = 16.0
GLA_CHUNK = 64
SWA_HEADS = 16
SWA_KV_HEADS = 2
SWA_GROUP = SWA_HEADS // SWA_KV_HEADS
SWA_HEAD_DIM = 64
SWA_WINDOW = 128
ROPE_THETA = 10000.0
LOG2E = math.log2(math.e)
LANES = 128
MXU_TILE = 256
OCT = LANES // S5_GROUP
ROW_TILE = 512
VMEM_LIMIT = 48 * 1024 * 1024

HIGHEST = lax.Precision.HIGHEST


def _cparams(sem):
    return pltpu.CompilerParams(dimension_semantics=sem, vmem_limit_bytes=VMEM_LIMIT)


def _full(shape):
    n = len(shape)
    return pl.BlockSpec(shape, lambda *_: (0,) * n)


def _dot(a, b):
    return jnp.dot(a, b, preferred_element_type=F32)


def _dot_nt(a, b):
    return lax.dot_general(a, b, (((1,), (1,)), ((), ())), preferred_element_type=F32)


def _dot_tn(a, b):
    return lax.dot_general(a, b, (((0,), (0,)), ((), ())), preferred_element_type=F32)


def _split_bf16(x):
    hi = x.astype(BF16)
    lo = (x - hi.astype(F32)).astype(BF16)
    return hi, lo


def _rms_rows(x, g):
    return x * lax.rsqrt(jnp.mean(x * x, axis=-1, keepdims=True) + NORM_EPS) * g


def _sigmoid(x):
    return 1.0 / (1.0 + jnp.exp(-x))


def _silu(x):
    return x * _sigmoid(x)


def _row_tile(m):
    return ROW_TILE if m % ROW_TILE == 0 else m


def _even_in_kernel(x_ref, g_ref, wm_ref, wa_ref, wgb_ref, wgate_ref, bgate_ref,
                    u_ref, u2_ref, ga_ref, q_ref, k_ref, v_ref, la_ref, gb_ref, uscr_ref, *, t):
    xb = _rms_rows(x_ref[...], g_ref[...]).astype(BF16)

    def proj(lo, hi):
        return _dot(xb, wm_ref[:, lo:hi])

    u = proj(0, 512)
    nrow = u.shape[0] // t
    for o in range(u_ref.shape[0]):
        uo = u[:, o * LANES:(o + 1) * LANES]
        u_ref[o] = uo.astype(u_ref.dtype)
        uscr_ref[o] = uo
        for tt in range(t):
            piece = uscr_ref[o, pl.ds(tt, nrow, stride=t), :]
            u2_ref[o, :, tt * LANES:(tt + 1) * LANES] = piece.astype(u2_ref.dtype)
    ga_ref[...] = proj(512, 1024).astype(ga_ref.dtype)
    q_ref[...] = (proj(1024, 1280) * (GLA_HEAD_K ** -0.5)).astype(q_ref.dtype)
    k_ref[...] = proj(1280, 1536).astype(k_ref.dtype)
    v_ref[...] = proj(1536, 2048).astype(v_ref.dtype)
    gb_ref[...] = _dot(xb, wgb_ref[...]).astype(gb_ref.dtype)
    a_low = _dot(xb, wa_ref[...])
    logit = _dot(a_low.astype(BF16), wgate_ref[...]) + bgate_ref[...]
    log_sig = jnp.minimum(logit, 0.0) - jnp.log1p(jnp.exp(-jnp.abs(logit)))
    la_ref[...] = log_sig * (1.0 / GLA_GATE_TAU)


def _even_in(x, g, wm, wa, wgb, wgate, bgate, act_dtype, t):
    m, d = x.shape
    tm = _row_tile(m)
    row = lambda n: pl.BlockSpec((tm, n), lambda i: (i, 0))
    no = 512 // LANES
    slab = pl.BlockSpec((no, tm, LANES), lambda i: (0, i, 0))
    chunk = pl.BlockSpec((no, tm // t, t * LANES), lambda i: (0, i, 0))
    outs = [(512, act_dtype), (256, act_dtype), (256, act_dtype), (512, act_dtype), (256, F32), (512, act_dtype)]
    return pl.pallas_call(
        functools.partial(_even_in_kernel, t=t),
        grid=(m // tm,),
        in_specs=[row(d), _full(g.shape), _full(wm.shape), _full(wa.shape), _full(wgb.shape),
                  _full(wgate.shape), _full(bgate.shape)],
        out_specs=[slab, chunk] + [row(n) for n, _ in outs],
        out_shape=[jax.ShapeDtypeStruct((no, m, LANES), act_dtype),
                   jax.ShapeDtypeStruct((no, m // t, t * LANES), act_dtype)]
        + [jax.ShapeDtypeStruct((m, n), dt) for n, dt in outs],
        scratch_shapes=[pltpu.VMEM((no, tm, LANES), F32)],
        compiler_params=_cparams(("parallel",)),
        name="even_in",
    )(x, g, wm, wa, wgb, wgate, bgate)


def _group_mask(shape, row_span, col_span):
    rg = (lax.broadcasted_iota(jnp.int32, shape, 0) // row_span) % OCT
    cg = (lax.broadcasted_iota(jnp.int32, shape, 1) // col_span) % OCT
    return rg == cg


def _s5_state_kernel(u_ref, bre_ref, bim_ref, x0_ref, are_ref, aim_ref, xs_ref, xf_ref, loc_ref, *, nseq, nchunks):
    hw = OCT * S5_STATE
    kk = u_ref.shape[2]
    rep = hw // LANES
    bp = jnp.concatenate([bre_ref[0]] * rep + [bim_ref[0]] * rep, axis=1)
    bp = jnp.where(_group_mask((kk, 2 * hw), S5_GROUP, S5_STATE), bp, 0.0).astype(BF16)
    loc_ref[...] = _dot(u_ref[0], bp)
    a_re = are_ref[0]
    a_im = aim_ref[0]
    if nchunks == 1:
        x0 = x0_ref[0]
        xr, xi = x0[:, :hw], x0[:, hw:]
        loc = loc_ref[...]
        xf_ref[0, :, :hw] = a_re * xr - a_im * xi + loc[:, :hw]
        xf_ref[0, :, hw:] = a_re * xi + a_im * xr + loc[:, hw:]
        xs_ref[0] = x0.astype(xs_ref.dtype)
    else:
        def body(j, carry):
            new = []
            for b in range(nseq):
                xr, xi = carry[b]
                row = pl.ds(b * nchunks + j, 1)
                lr = loc_ref[row, :hw]
                li = loc_ref[row, hw:]
                loc_ref[row, :hw] = xr
                loc_ref[row, hw:] = xi
                new.append((a_re * xr - a_im * xi + lr, a_re * xi + a_im * xr + li))
            return tuple(new)

        init = tuple((x0_ref[0, b:b + 1, :hw], x0_ref[0, b:b + 1, hw:]) for b in range(nseq))
        fin = lax.fori_loop(0, nchunks, body, init, unroll=4)
        for b in range(nseq):
            xf_ref[0, b:b + 1, :hw] = fin[b][0]
            xf_ref[0, b:b + 1, hw:] = fin[b][1]
        xs_ref[0] = loc_ref[...].astype(xs_ref.dtype)


def _s5_state(u2, bre, bim, x0, are, aim, nseq, nchunks):
    no, r, kk = u2.shape
    sw = x0.shape[2]
    blk = lambda a: pl.BlockSpec((1,) + a.shape[1:], lambda o: (o,) + (0,) * (a.ndim - 1))
    return pl.pallas_call(
        functools.partial(_s5_state_kernel, nseq=nseq, nchunks=nchunks),
        grid=(no,),
        in_specs=[blk(u2), blk(bre), blk(bim), blk(x0), blk(are), blk(aim)],
        out_specs=[pl.BlockSpec((1, r, sw), lambda o: (o, 0, 0)), blk(x0)],
        out_shape=[jax.ShapeDtypeStruct((no, r, sw), BF16), jax.ShapeDtypeStruct(x0.shape, F32)],
        scratch_shapes=[pltpu.VMEM((r, sw), F32)],
        compiler_params=_cparams(("parallel",)),
        name="s5_state",
    )(u2, bre, bim, x0, are, aim)


def _s5_out_kernel(u_ref, xs_ref, wc_ref, cc_ref, y_ref, yscr_ref):
    r, kk = u_ref.shape[1], u_ref.shape[2]
    t = kk // LANES
    sw = xs_ref.shape[2]
    ntile = kk // MXU_TILE
    cc = cc_ref[0]
    cp = jnp.concatenate([cc[:S5_STATE]] * OCT + [cc[S5_STATE:]] * OCT, axis=0)
    cp = jnp.where(_group_mask((sw, kk), S5_STATE, S5_GROUP), cp, 0.0).astype(BF16)
    wmask = _group_mask((MXU_TILE, MXU_TILE), S5_GROUP, S5_GROUP)
    wts = []
    for d in range(ntile):
        wc = wc_ref[0, d]
        full = jnp.concatenate([wc[:S5_GROUP]] * OCT + [wc[S5_GROUP:]] * OCT, axis=0)
        wts.append(jnp.where(wmask, full, 0.0).astype(BF16))
    xs = xs_ref[0]
    for n in range(ntile):
        cols = slice(n * MXU_TILE, (n + 1) * MXU_TILE)
        acc = _dot(xs, cp[:, cols])
        for k in range(n + 1):
            acc = acc + _dot(u_ref[0, :, k * MXU_TILE:(k + 1) * MXU_TILE], wts[n - k])
        for e in range(MXU_TILE // LANES):
            yscr_ref[pl.ds(2 * n + e, r, stride=t), :] = acc[:, e * LANES:(e + 1) * LANES]
    y_ref[0] = yscr_ref[...].astype(y_ref.dtype)


def _s5_out(u2, xs, wc, cc, out_dtype):
    no, r, kk = u2.shape
    m = r * (kk // LANES)
    blk = lambda a: pl.BlockSpec((1,) + a.shape[1:], lambda o: (o,) + (0,) * (a.ndim - 1))
    return pl.pallas_call(
        _s5_out_kernel,
        grid=(no,),
        in_specs=[blk(u2), blk(xs), blk(wc), blk(cc)],
        out_specs=pl.BlockSpec((1, m, LANES), lambda o: (o, 0, 0)),
        out_shape=jax.ShapeDtypeStruct((no, m, LANES), out_dtype),
        scratch_shapes=[pltpu.VMEM((m, LANES), F32)],
        compiler_params=_cparams(("parallel",)),
        name="s5_out",
    )(u2, xs, wc, cc)


def _s5_params(lam_re, lam_im, log_dt, b_re, b_im, c_re, c_im):
    t = S5_CHUNK
    ng = lam_re.shape[0]
    no = ng // OCT
    dt = jnp.exp(log_dt)[:, None]
    a = lam_re * dt
    b = lam_im * dt
    n = jnp.arange(t + 1, dtype=F32)[None, :, None]
    mag = jnp.exp(n * a[:, None, :])
    pw_re = mag * jnp.cos(n * b[:, None, :])
    pw_im = mag * jnp.sin(n * b[:, None, :])
    em1_re = jnp.expm1(a) * jnp.cos(b) - 2.0 * jnp.sin(0.5 * b) ** 2
    em1_im = jnp.exp(a) * jnp.sin(b)
    den = lam_re * lam_re + lam_im * lam_im
    z_re = (em1_re * lam_re + em1_im * lam_im) / den
    z_im = (em1_im * lam_re - em1_re * lam_im) / den
    bb_re = z_re[..., None] * b_re - z_im[..., None] * b_im
    bb_im = z_re[..., None] * b_im + z_im[..., None] * b_re
    e_re, e_im = pw_re[:, :t, :, None], pw_im[:, :t, :, None]
    w_re = e_re * bb_re[:, None] - e_im * bb_im[:, None]
    w_im = e_re * bb_im[:, None] + e_im * bb_re[:, None]
    kd = (jnp.einsum('gdpc,gkp->gdck', w_re, c_re, precision=HIGHEST)
          - jnp.einsum('gdpc,gkp->gdck', w_im, c_im, precision=HIGHEST))
    kd_pad = jnp.pad(kd, ((0, 0), (1, 0), (0, 0), (0, 0)))
    nl = t // 2
    lag1 = (2 * jnp.arange(nl)[:, None, None] + jnp.arange(2)[None, None, :] - jnp.arange(2)[None, :, None] + 1)
    x = kd_pad[:, lag1].reshape(no, OCT, nl, 2, 2, S5_GROUP, S5_GROUP)
    wc = x.transpose(0, 2, 3, 5, 4, 1, 6).reshape(no, nl, 2 * S5_GROUP, MXU_TILE)

    def in_op(w):
        w = w[:, ::-1].transpose(0, 1, 3, 2).reshape(no, OCT, t, S5_GROUP, S5_STATE).transpose(0, 2, 1, 3, 4)
        return jnp.concatenate([w, w], axis=-1).reshape(no, t * LANES, 2 * S5_STATE)

    bre, bim = in_op(w_re), in_op(w_im)
    o_re, o_im = pw_re[:, 1:, None, :], pw_im[:, 1:, None, :]
    cl_re = c_re[:, None] * o_re - c_im[:, None] * o_im
    cl_im = c_re[:, None] * o_im + c_im[:, None] * o_re

    def out_op(cl):
        cl = cl.reshape(no, OCT, t, S5_GROUP, S5_STATE).transpose(0, 4, 2, 1, 3)
        return cl.reshape(no, S5_STATE, t * LANES)

    cc = jnp.concatenate([out_op(cl_re), -out_op(cl_im)], axis=1)
    return wc, bre, bim, cc, pw_re, pw_im


def _s5_chunk(seqlen):
    return math.gcd(seqlen, S5_CHUNK)


def _s5_branch(u2, x0, ops, nseq, seqlen, out_dtype):
    wc, bre, bim, cc, pw_re, pw_im = ops
    no = u2.shape[0]
    t = _s5_chunk(seqlen)
    kk = t * LANES
    nchunks = seqlen // t
    are = pw_re[:, t].reshape(no, 1, OCT * S5_STATE)
    aim = pw_im[:, t].reshape(no, 1, OCT * S5_STATE)
    tail = S5_CHUNK * LANES - kk
    xs, xf = _s5_state(u2, bre[:, tail:], bim[:, tail:], x0, are, aim, nseq, nchunks)
    return _s5_out(u2, xs, wc[:, :kk // MXU_TILE], cc[:, :, :kk], out_dtype), xf


def _s5_state_in(re, im, no):
    nseq = re.shape[0]
    f = lambda a: a.reshape(nseq, no, OCT * S5_STATE).transpose(1, 0, 2)
    return jnp.concatenate([f(re), f(im)], axis=-1)


def _s5_state_out(xf):
    no, nseq, _ = xf.shape
    hw = OCT * S5_STATE
    f = lambda a: a.transpose(1, 0, 2).reshape(1, nseq, no * OCT, S5_STATE)
    return f(xf[:, :, :hw]), f(xf[:, :, hw:])


def _gla_kernel(q_ref, k_ref, v_ref, la_ref, gb_ref, s0_ref, ng_ref, o_ref, sf_ref, st_ref,
                *, nseq, nchunks, c):
    i = pl.program_id(1)

    @pl.when(i == 0)
    def _():
        st_ref[...] = s0_ref[...]

    nh, hk, hv = GLA_HEADS, GLA_HEAD_K, GLA_HEAD_V
    nch = nseq * nchunks
    tm = nch * c
    iota = lambda shape, d: lax.broadcasted_iota(jnp.int32, shape, d)
    ng = ng_ref[...]

    la_hi, la_lo = _split_bf16(la_ref[...])
    rt, ct = iota((tm, tm), 0), iota((tm, tm), 1)
    tril = ((rt // c == ct // c) & (rt >= ct)).astype(BF16)
    bcum = _dot(tril, la_hi) + _dot(tril, la_lo)
    seg = (iota((tm, nch * LANES), 0) // c == iota((tm, nch * LANES), 1) // LANES).astype(BF16)
    dec_all = jnp.exp(_dot_tn(la_hi, seg) + _dot_tn(la_lo, seg))
    q_all = q_ref[...].astype(F32)
    k_all = k_ref[...].astype(F32)
    q_dec_all = q_all * jnp.exp(bcum)
    k_dec_all = k_all * jnp.exp(-bcum)

    own_k = iota((nh * c, nh * hk), 0) // c == iota((nh * c, nh * hk), 1) // hk
    own_v = iota((nh * c, nh * hv), 0) // c == iota((nh * c, nh * hv), 1) // hv
    causal = iota((c, nh * c), 1) % c <= iota((c, nh * c), 0)
    zero_v = jnp.zeros((hk, hv), F32)

    sts = None
    for ci in range(nch):
        s, first, last = ci // nchunks, ci % nchunks == 0, ci % nchunks == nchunks - 1
        if first:
            sts = [st_ref[s, h] for h in range(nh)]
        rows = slice(ci * c, (ci + 1) * c)
        q_dec = q_dec_all[rows].astype(BF16)
        k_dec = k_dec_all[rows]
        b_c = bcum[rows]
        k_tail = (k_all[rows] * jnp.exp(b_c[c - 1:c] - b_c)).astype(BF16)
        v = v_ref[rows, :].astype(F32)
        k_bd = jnp.where(own_k, jnp.concatenate([k_dec] * nh, axis=0), 0.0).astype(BF16)
        v_bd = jnp.where(own_v, jnp.concatenate([v] * nh, axis=0), 0.0).astype(BF16)
        st_bd = jnp.concatenate(
            [jnp.concatenate([zero_v] * h + [sts[h]] + [zero_v] * (nh - 1 - h), axis=1) for h in range(nh)],
            axis=0).astype(BF16)
        att = jnp.where(causal, _dot_nt(q_dec, k_bd), 0.0)
        o = _dot(att.astype(BF16), v_bd) + _dot(q_dec, st_bd)
        kv = _dot_tn(k_tail, v.astype(BF16))
        for h in range(nh):
            ks = slice(h * hk, (h + 1) * hk)
            vs = slice(h * hv, (h + 1) * hv)
            sts[h] = dec_all[ks, ci * LANES:(ci + 1) * LANES] * sts[h] + kv[ks, vs]
            oh = _rms_rows(o[:, vs], ng) * _silu(gb_ref[rows, vs].astype(F32))
            o_ref[rows, vs] = oh.astype(o_ref.dtype)
        if last:
            for h in range(nh):
                st_ref[s, h] = sts[h]

    @pl.when(i == pl.num_programs(1) - 1)
    def _():
        sf_ref[...] = st_ref[...]


def _gla(q, k, v, la, gb, s0, ng, nseq_total, seqlen, out_dtype):
    c = math.gcd(seqlen, GLA_CHUNK)
    if seqlen >= 8 * c:
        nseq, nchunks = 1, 8
    else:
        nseq, nchunks = 8, seqlen // c
    assert nseq_total % nseq == 0 and seqlen % (nchunks * c) == 0
    nblk = seqlen // (nchunks * c)
    tm = nseq * nchunks * c
    row = lambda n: pl.BlockSpec((tm, n), lambda b, i: (b * nblk + i, 0))
    st_spec = pl.BlockSpec((nseq, GLA_HEADS, GLA_HEAD_K, GLA_HEAD_V), lambda b, i: (b, 0, 0, 0))
    m = q.shape[0]
    return pl.pallas_call(
        functools.partial(_gla_kernel, nseq=nseq, nchunks=nchunks, c=c),
        grid=(nseq_total // nseq, nblk),
        in_specs=[row(256), row(256), row(512), row(256), row(512), st_spec, _full(ng.shape)],
        out_specs=[row(512), st_spec],
        out_shape=[jax.ShapeDtypeStruct((m, 512), out_dtype), jax.ShapeDtypeStruct(s0.shape, F32)],
        scratch_shapes=[pltpu.VMEM((nseq, GLA_HEADS, GLA_HEAD_K, GLA_HEAD_V), F32)],
        compiler_params=_cparams(("parallel", "arbitrary")),
        name="gla",
    )(q, k, v, la, gb, s0, ng)


def _gelu_tanh(x):
    return 0.5 * x * (1.0 + jnp.tanh(math.sqrt(2.0 / math.pi) * (x + 0.044715 * (x * x * x))))


def _even_out_rows(x_ref, y_ref, u_ref, ga_ref, ob_ref, d_ref, wglu_ref, bglu_ref, woa_ref, wob_ref):
    no = y_ref.shape[0]
    y = jnp.concatenate([y_ref[o].astype(F32) for o in range(no)], axis=1)
    u = jnp.concatenate([u_ref[o].astype(F32) for o in range(no)], axis=1)
    z = _gelu_tanh(y + d_ref[...] * u)
    z = z * _sigmoid(_dot(z.astype(BF16), wglu_ref[...]) + bglu_ref[...])
    out_a = z * _silu(ga_ref[...].astype(F32))
    mix = _dot(out_a.astype(BF16), woa_ref[...]) + _dot(ob_ref[...].astype(BF16), wob_ref[...])
    return x_ref[...] + mix


def _head_ones(n):
    r = lax.broadcasted_iota(jnp.int32, (n, n), 0) // SWA_HEAD_DIM
    c = lax.broadcasted_iota(jnp.int32, (n, n), 1) // SWA_HEAD_DIM
    return (r == c).astype(BF16)


def _rope_block(x, cos_t, sin_t, upper):
    swapped = jnp.where(upper, pltpu.roll(x, 32, 1), pltpu.roll(x, 96, 1))
    return x * cos_t + swapped * sin_t


def _mid_kernel(x_ref, y_ref, u_ref, ga_ref, ob_ref, d_ref, wglu_ref, bglu_ref, woa_ref, wob_ref,
                g_ref, w_ref, gq_ref, gk_ref, cos_ref, sin_ref, h_ref, q_ref, k_ref, v_ref, gate_ref):
    h = _even_out_rows(x_ref, y_ref, u_ref, ga_ref, ob_ref, d_ref, wglu_ref, bglu_ref, woa_ref, wob_ref)
    h_ref[...] = h
    xb = _rms_rows(h, g_ref[...]).astype(BF16)
    cos_t = cos_ref[...]
    sin_t = sin_ref[...]
    tm = xb.shape[0]
    lane = lax.broadcasted_iota(jnp.int32, (tm, LANES), 1)
    upper = (lane & 32) != 0
    ones4 = _head_ones(MXU_TILE)
    inv_d = 1.0 / SWA_HEAD_DIM
    qw = SWA_HEADS * SWA_HEAD_DIM
    kw = SWA_KV_HEADS * SWA_HEAD_DIM
    for j in range(qw // MXU_TILE):
        q = _dot(xb, w_ref[:, j * MXU_TILE:(j + 1) * MXU_TILE])
        ss = _dot((q * q).astype(BF16), ones4)
        qn = q * lax.rsqrt(ss * inv_d + NORM_EPS) * gq_ref[...]
        for e in range(MXU_TILE // LANES):
            cols = slice(j * MXU_TILE + e * LANES, j * MXU_TILE + (e + 1) * LANES)
            qe = _rope_block(qn[:, e * LANES:(e + 1) * LANES], cos_t, sin_t, upper)
            q_ref[:, cols] = (qe * (SWA_HEAD_DIM ** -0.5 * LOG2E)).astype(q_ref.dtype)
    kv = _dot(xb, w_ref[:, qw:qw + 2 * kw])
    k = kv[:, :kw]
    ss = _dot((k * k).astype(BF16), ones4[:kw, :kw])
    kn = k * lax.rsqrt(ss * inv_d + NORM_EPS) * gk_ref[...]
    k_ref[...] = _rope_block(kn, cos_t, sin_t, upper)
    v_ref[...] = kv[:, kw:]
    gate_ref[...] = _dot(xb, w_ref[:, qw + 2 * kw:]).astype(gate_ref.dtype)


def _mid(x, y, u, ga, ob, pe, po, cos_t, sin_t, act_dtype):
    m, dm = x.shape
    tm = _row_tile(m)
    row = lambda n: pl.BlockSpec((tm, n), lambda i: (i, 0))
    slab = pl.BlockSpec((y.shape[0], tm, LANES), lambda i: (0, i, 0))
    assert cos_t.shape[0] % tm == 0
    nper = cos_t.shape[0] // tm
    tab = pl.BlockSpec((tm, LANES), lambda i: (i % nper, 0))
    qw = SWA_HEADS * SWA_HEAD_DIM
    kw = SWA_KV_HEADS * SWA_HEAD_DIM
    weights = [pe['d'], pe['wglu'], pe['bglu'], pe['woa'], pe['wob'], po['norm_g'], po['w_in'], po['gq'], po['gk']]
    return pl.pallas_call(
        _mid_kernel,
        grid=(m // tm,),
        in_specs=[row(dm), slab, slab, row(512), row(512)] + [_full(w.shape) for w in weights] + [tab, tab],
        out_specs=[row(dm), row(qw), row(kw), row(kw), row(qw)],
        out_shape=[jax.ShapeDtypeStruct((m, dm), F32),
                   jax.ShapeDtypeStruct((m, qw), act_dtype), jax.ShapeDtypeStruct((m, kw), F32),
                   jax.ShapeDtypeStruct((m, kw), F32), jax.ShapeDtypeStruct((m, qw), act_dtype)],
        compiler_params=_cparams(("parallel",)),
        name="mid",
    )(x, y, u, ga, ob, *weights, cos_t, sin_t)


def _rope_tables(pos):
    half = SWA_HEAD_DIM // 2
    inv_freq = ROPE_THETA ** (-np.arange(half, dtype=np.float64) / half)
    ang = np.asarray(pos, np.float64)[:, None] * inv_freq[None, :]
    cos, sin = np.cos(ang), np.sin(ang)
    cos_t = np.concatenate([cos, cos, cos, cos], axis=1).astype(np.float32)
    sin_t = np.concatenate([-sin, sin, -sin, sin], axis=1).astype(np.float32)
    return jnp.asarray(cos_t), jnp.asarray(sin_t)


def _attn_prompt_kernel(sink_ref, q_ref, kc_ref, kp_ref, vc_ref, vp_ref, gate_ref, h_ref, wout_ref,
                        y_ref, p_ref, o_ref, *, nqb):
    i = pl.program_id(1)
    w = SWA_WINDOW
    hd = SWA_HEAD_DIM
    npair = SWA_GROUP // 2
    lane2 = lax.broadcasted_iota(jnp.int32, (2 * w, LANES), 1)
    low = lane2 < hd
    rr = lax.broadcasted_iota(jnp.int32, (w, w), 0)
    cc = lax.broadcasted_iota(jnp.int32, (w, w), 1)
    tri = cc <= rr
    low_w = cc < hd
    r4 = lax.broadcasted_iota(jnp.int32, (4 * w, LANES), 0)
    c4 = lax.broadcasted_iota(jnp.int32, (4 * w, LANES), 1)
    den_cols = ((r4 < 2 * w) == (c4 < hd)).astype(BF16)
    unit = 0
    for jb in range(nqb):
        rows = slice(jb * w, (jb + 1) * w)
        if jb == 0:
            k_prev, v_prev = kp_ref[...], vp_ref[...]
            has_prev = i > 0
        else:
            prev_rows = slice((jb - 1) * w, jb * w)
            k_prev, v_prev = kc_ref[prev_rows, :], vc_ref[prev_rows, :]
            has_prev = None
        kcat = jnp.concatenate([k_prev, kc_ref[rows, :]], axis=0)
        vcat = jnp.concatenate([v_prev, vc_ref[rows, :]], axis=0)
        for kv in range(SWA_KV_HEADS):
            own = low if kv == 0 else jnp.logical_not(low)
            k_own = jnp.where(own, kcat, 0.0)
            v_own = jnp.where(own, vcat, 0.0)
            k_oth = pltpu.roll(k_own, hd, 1)
            v_oth = pltpu.roll(v_own, hd, 1)
            k_lo, k_hi = (k_own, k_oth) if kv == 0 else (k_oth, k_own)
            v_lo, v_hi = (v_own, v_oth) if kv == 0 else (v_oth, v_own)
            k_rhs = jnp.concatenate([k_lo, k_hi], axis=0).astype(BF16)
            v_rhs = jnp.concatenate([jnp.concatenate([v_lo, v_hi], axis=0).astype(BF16), den_cols], axis=1)
            qs = jnp.concatenate([q_ref[rows, (kv * npair + pr) * LANES:(kv * npair + pr + 1) * LANES]
                                  for pr in range(npair)], axis=0)
            s_all = _dot_nt(qs, k_rhs)
            buf = unit % 2
            unit += 1
            sink_terms = []
            for pr in range(npair):
                prow = slice(pr * w, (pr + 1) * w)
                pair_terms = []
                for e in range(2):
                    s_prev = s_all[prow, (2 * e) * w:(2 * e + 1) * w]
                    s_cur = s_all[prow, (2 * e + 1) * w:(2 * e + 2) * w]
                    if has_prev is not None:
                        s_prev = jnp.where(has_prev, s_prev, -jnp.inf)
                    sc = jnp.where(tri, s_cur, s_prev)
                    sink = sink_ref[2 * (kv * npair + pr) + e] * LOG2E
                    mx = jnp.maximum(jnp.max(sc, axis=-1, keepdims=True), sink)
                    pe = jnp.exp2(sc - mx)
                    p_ref[buf, prow, (2 * e) * w:(2 * e + 1) * w] = jnp.where(tri, 0.0, pe).astype(BF16)
                    p_ref[buf, prow, (2 * e + 1) * w:(2 * e + 2) * w] = jnp.where(tri, pe, 0.0).astype(BF16)
                    pair_terms.append(jnp.exp2(sink - mx))
                sink_terms.append(pair_terms)
            o_ext = _dot(p_ref[buf], v_rhs)
            for pr in range(npair):
                prow = slice(pr * w, (pr + 1) * w)
                cols = slice((kv * npair + pr) * LANES, (kv * npair + pr + 1) * LANES)
                st = jnp.where(low_w, sink_terms[pr][0], sink_terms[pr][1])
                o = o_ext[prow, :LANES] / (o_ext[prow, LANES:] + st)
                o_ref[rows, cols] = o.astype(o_ref.dtype)
    og = o_ref[...].astype(F32) * _silu(gate_ref[...].astype(F32))
    y_ref[...] = h_ref[...] + _dot(og.astype(BF16), wout_ref[...])


def _attn_prompt(sinks, q, k, v, gate, h, w_out, nseq, seqlen):
    w = SWA_WINDOW
    nqb = 4
    tm = nqb * w
    assert seqlen % tm == 0
    nblk = seqlen // tm
    qw = q.shape[1]
    dm = h.shape[1]
    row = lambda n: pl.BlockSpec((tm, n), lambda b, i: (b * nblk + i, 0))
    prev = lambda n: pl.BlockSpec((w, n), lambda b, i: (jnp.maximum((b * nblk + i) * nqb - 1, 0), 0))
    smem = pl.BlockSpec(memory_space=pltpu.SMEM)
    return pl.pallas_call(
        functools.partial(_attn_prompt_kernel, nqb=nqb),
        grid=(nseq, nblk),
        in_specs=[smem, row(qw), row(LANES), prev(LANES), row(LANES), prev(LANES), row(qw), row(dm),
                  _full(w_out.shape)],
        out_specs=row(dm),
        out_shape=jax.ShapeDtypeStruct(h.shape, F32),
        scratch_shapes=[pltpu.VMEM((2, 4 * w, 4 * w), BF16), pltpu.VMEM((tm, qw), BF16)],
        compiler_params=_cparams(("parallel", "arbitrary")),
        name="attn_prompt",
    )(sinks, q, k, k, v, v, gate, h, w_out)


def _attn_sample_kernel(sink_ref, q_ref, kn_ref, vn_ref, kc_ref, vc_ref, o_ref, ko_ref, vo_ref, *, nseq, seqlen):
    nq = SWA_HEADS * seqlen
    ncache = kc_ref.shape[1]
    t_row = lax.broadcasted_iota(jnp.int32, (nq, ncache), 0) % seqlen
    c_col = lax.broadcasted_iota(jnp.int32, (nq, ncache), 1)
    cache_ok = c_col > t_row - (SWA_WINDOW - ncache)
    nnew = kn_ref.shape[1]
    t_row_n = lax.broadcasted_iota(jnp.int32, (nq, nnew), 0) % seqlen
    n_col = lax.broadcasted_iota(jnp.int32, (nq, nnew), 1)
    new_ok = n_col <= t_row_n
    sink = sink_ref[...] * LOG2E
    for s in range(nseq):
        q = q_ref[s]
        kn = kn_ref[s].astype(BF16)
        vn = vn_ref[s].astype(BF16)
        sc = jnp.where(cache_ok, _dot_nt(q, kc_ref[s].astype(BF16)), -jnp.inf)
        sn = jnp.where(new_ok, _dot_nt(q, kn), -jnp.inf)
        mx = jnp.maximum(jnp.maximum(jnp.max(sc, axis=-1, keepdims=True), jnp.max(sn, axis=-1, keepdims=True)),
                         sink[:, 0:1])
        pc = jnp.exp2(sc - mx)
        pn = jnp.exp2(sn - mx)
        den = (jnp.sum(pc, axis=-1, keepdims=True) + jnp.sum(pn, axis=-1, keepdims=True)
               + jnp.exp2(sink[:, 0:1] - mx))
        inv = 1.0 / den
        o = _dot((pc * inv).astype(BF16), vc_ref[s].astype(BF16)) + _dot((pn * inv).astype(BF16), vn)
        o_ref[s] = o
        keep = ncache - seqlen
        ko_ref[s, 0:keep, :] = kc_ref[s, seqlen:ncache, :]
        ko_ref[s, keep:ncache, :] = kn_ref[s, 0:seqlen, :]
        vo_ref[s, 0:keep, :] = vc_ref[s, seqlen:ncache, :]
        vo_ref[s, keep:ncache, :] = vn_ref[s, 0:seqlen, :]


def _attn_sample(sink_rows, q, kn, vn, kc, vc, seqlen):
    n = q.shape[0]
    nseq = 8
    assert n % nseq == 0
    blk = lambda a: pl.BlockSpec((nseq,) + a.shape[1:], lambda i: (i, 0, 0))
    return pl.pallas_call(
        functools.partial(_attn_sample_kernel, nseq=nseq, seqlen=seqlen),
        grid=(n // nseq,),
        in_specs=[_full(sink_rows.shape), blk(q), blk(kn), blk(vn), blk(kc), blk(vc)],
        out_specs=[blk(q), blk(kc), blk(vc)],
        out_shape=[jax.ShapeDtypeStruct(q.shape, F32), jax.ShapeDtypeStruct(kc.shape, F32),
                   jax.ShapeDtypeStruct(vc.shape, F32)],
        compiler_params=_cparams(("parallel",)),
        name="attn_sample",
    )(sink_rows, q, kn, vn, kc, vc)


def _odd_out_kernel(h_ref, o_ref, gate_ref, w_ref, y_ref):
    og = o_ref[...].astype(F32) * _silu(gate_ref[...].astype(F32))
    y_ref[...] = h_ref[...] + _dot(og.astype(BF16), w_ref[...])


def _odd_out(h, o, gate, w):
    m, dm = h.shape
    tm = _row_tile(m)
    row = lambda n: pl.BlockSpec((tm, n), lambda i: (i, 0))
    return pl.pallas_call(
        _odd_out_kernel,
        grid=(m // tm,),
        in_specs=[row(dm), row(o.shape[1]), row(gate.shape[1]), _full(w.shape)],
        out_specs=row(dm),
        out_shape=jax.ShapeDtypeStruct((m, dm), F32),
        compiler_params=_cparams(("parallel",)),
        name="odd_out",
    )(h, o, gate, w)


def _trunk(x, s5_x0, gla_s0, pe, po, pos, nseq, seqlen, act_dtype):
    u4, u2, ga, q, k, v, la, gb = _even_in(x, pe['norm_g'], pe['wm'], pe['wa'], pe['wgb'], pe['wgate'],
                                           pe['bgate'], act_dtype, _s5_chunk(seqlen))
    y4, s5_fin = _s5_branch(u2, s5_x0, pe['s5_ops'], nseq, seqlen, act_dtype)
    ob, gla_fin = _gla(q, k, v, la, gb, gla_s0, pe['gla_norm_g'], nseq, seqlen, act_dtype)
    cos_t, sin_t = _rope_tables(pos)
    h, q1, k1, v1, gate = _mid(x, y4, u4, ga, ob, pe, po, cos_t, sin_t, act_dtype)
    return h, q1, k1, v1, gate, s5_fin, gla_fin


def kernel(x_prompt, x_sample, state_s5_re, state_s5_im, state_gla, cache_swa_k, cache_swa_v,
           even_norm_g, even_w_in, s5_lambda_re, s5_lambda_im, s5_log_dt, s5_b_re, s5_b_im,
           s5_c_re, s5_c_im, s5_d, s5_w_glu, s5_b_glu, gla_w_gate, gla_b_gate, gla_norm_g,
           even_w_out, odd_norm_g, odd_w_in, swa_q_norm_g, swa_k_norm_g, swa_sinks, odd_w_out):
    nb, seq, dm = x_prompt.shape
    ns, dseq, _ = x_sample.shape
    ng = s5_lambda_re.shape[1]
    no = ng // OCT
    xp = x_prompt.reshape(nb * seq, dm)
    xs = x_sample.reshape(ns * dseq, dm)

    i = 0
    w_in = even_w_in[i]
    s5w = ng * S5_GROUP
    col_alow = 2 * s5w + 2 * GLA_HEADS * GLA_HEAD_K + GLA_HEADS * GLA_HEAD_V
    col_gb = col_alow + GLA_GATE_RANK
    pad_rank = LANES - GLA_GATE_RANK
    pe = {
        'norm_g': even_norm_g[i][None, :],
        'wm': w_in[:, :col_alow].astype(BF16),
        'wa': jnp.pad(w_in[:, col_alow:col_gb], ((0, 0), (0, pad_rank))).astype(BF16),
        'wgb': w_in[:, col_gb:].astype(BF16),
        'wgate': jnp.pad(gla_w_gate[i], ((0, pad_rank), (0, 0))).astype(BF16),
        'bgate': gla_b_gate[i][None, :],
        's5_ops': _s5_params(s5_lambda_re[i], s5_lambda_im[i], s5_log_dt[i], s5_b_re[i], s5_b_im[i],
                             s5_c_re[i], s5_c_im[i]),
        'gla_norm_g': gla_norm_g[i][None, :],
        'd': s5_d[i][None, :],
        'wglu': s5_w_glu[i].astype(BF16),
        'bglu': s5_b_glu[i][None, :],
        'woa': even_w_out[i][:s5w].astype(BF16),
        'wob': even_w_out[i][s5w:].astype(BF16),
    }
    po = {
        'norm_g': odd_norm_g[i][None, :],
        'w_in': odd_w_in[i].astype(BF16),
        'gq': jnp.tile(swa_q_norm_g[i], MXU_TILE // SWA_HEAD_DIM)[None, :],
        'gk': jnp.tile(swa_k_norm_g[i], LANES // SWA_HEAD_DIM)[None, :],
    }
    w_out = odd_w_out[i].astype(BF16)
    sinks = swa_sinks[i]
    kvw = SWA_KV_HEADS * SWA_HEAD_DIM

    s5_zero = jnp.zeros((no, nb, 2 * OCT * S5_STATE), F32)
    gla_zero = jnp.zeros((nb, GLA_HEADS, GLA_HEAD_K, GLA_HEAD_V), F32)
    hp, q, k, v, gate, s5_p, gla_p = _trunk(xp, s5_zero, gla_zero, pe, po, np.arange(seq), nb, seq, BF16)
    s5r_p, s5i_p = _s5_state_out(s5_p)
    y_prompt = _attn_prompt(sinks, q, k, v, gate, hp, w_out, nb, seq).reshape(nb, seq, dm)
    cache_len = min(SWA_WINDOW, seq)
    swk_p = k.reshape(nb, seq, SWA_KV_HEADS, SWA_HEAD_DIM)[None, :, seq - cache_len:]
    swv_p = v.reshape(nb, seq, SWA_KV_HEADS, SWA_HEAD_DIM)[None, :, seq - cache_len:]

    pos_s = np.tile(PAST_LEN + np.arange(dseq), ns)
    s5_init = _s5_state_in(state_s5_re[i], state_s5_im[i], no)
    hs, q, k, v, gate, s5_s, gla_s = _trunk(xs, s5_init, state_gla[i], pe, po, pos_s, ns, dseq, F32)
    s5r_s, s5i_s = _s5_state_out(s5_s)
    ncache = cache_swa_k.shape[2]
    qh = q.reshape(ns, dseq, SWA_KV_HEADS, SWA_GROUP, SWA_HEAD_DIM).transpose(0, 2, 3, 1, 4)
    zq = jnp.zeros_like(qh[:, 0])
    q2 = jnp.stack([jnp.concatenate([qh[:, 0], zq], axis=-1), jnp.concatenate([zq, qh[:, 1]], axis=-1)], axis=1)
    q2 = q2.reshape(ns, SWA_HEADS * dseq, kvw).astype(BF16)
    sink_rows = jnp.broadcast_to(jnp.repeat(sinks, dseq)[:, None], (SWA_HEADS * dseq, LANES))
    kn = k.reshape(ns, dseq, kvw)
    vn = v.reshape(ns, dseq, kvw)
    npad = 16 - dseq
    kn_pad = jnp.pad(kn, ((0, 0), (0, npad), (0, 0)))
    vn_pad = jnp.pad(vn, ((0, 0), (0, npad), (0, 0)))
    kc = cache_swa_k[i].reshape(ns, ncache, kvw)
    vc = cache_swa_v[i].reshape(ns, ncache, kvw)
    o2, kc_new, vc_new = _attn_sample(sink_rows, q2, kn_pad, vn_pad, kc, vc, dseq)
    o2 = o2.reshape(ns, SWA_KV_HEADS, SWA_GROUP, dseq, SWA_KV_HEADS, SWA_HEAD_DIM)
    o = jnp.stack([o2[:, 0, :, :, 0], o2[:, 1, :, :, 1]], axis=1)
    o = o.transpose(0, 3, 1, 2, 4).reshape(ns * dseq, SWA_HEADS * SWA_HEAD_DIM)
    y_sample = _odd_out(hs, o, gate, w_out).reshape(ns, dseq, dm)
    swk_s = kc_new.reshape(1, ns, ncache, SWA_KV_HEADS, SWA_HEAD_DIM)
    swv_s = vc_new.reshape(1, ns, ncache, SWA_KV_HEADS, SWA_HEAD_DIM)

    return (y_prompt, y_sample,
            s5r_p, s5i_p, gla_p[None], swk_p, swv_p,
            s5r_s, s5i_s, gla_s[None], swk_s, swv_s)
```

```python
import functools
import math

import jax
import jax.numpy as jnp
import numpy as np
from jax import lax
from jax.experimental import pallas as pl
from jax.experimental.pallas import tpu as pltpu

F32 = jnp.float32
BF16 = jnp.bfloat16

PAST_LEN = 8192
NORM_EPS = 1e-6
S5_GROUP = 16
S5_STATE = 64
S5_CHUNK = 16
GLA_HEADS = 4
GLA_HEAD_K = 64
GLA_HEAD_V = 128
GLA_GATE_RANK = 16
GLA_GATE_TAU = 16.0
GLA_CHUNK = 64
SWA_HEADS = 16
SWA_KV_HEADS = 2
SWA_GROUP = SWA_HEADS // SWA_KV_HEADS
SWA_HEAD_DIM = 64
SWA_WINDOW = 128
ROPE_THETA = 10000.0
LOG2E = math.log2(math.e)
LANES = 128
MXU_TILE = 256
OCT = LANES // S5_GROUP
ROW_TILE = 512
VMEM_LIMIT = 48 * 1024 * 1024

HIGHEST = lax.Precision.HIGHEST


def _cparams(sem):
    return pltpu.CompilerParams(dimension_semantics=sem, vmem_limit_bytes=VMEM_LIMIT)


def _full(shape):
    n = len(shape)
    return pl.BlockSpec(shape, lambda *_: (0,) * n)


def _dot(a, b):
    return jnp.dot(a, b, preferred_element_type=F32)


def _dot_nt(a, b):
    return lax.dot_general(a, b, (((1,), (1,)), ((), ())), preferred_element_type=F32)


def _dot_tn(a, b):
    return lax.dot_general(a, b, (((0,), (0,)), ((), ())), preferred_element_type=F32)


def _split_bf16(x):
    hi = x.astype(BF16)
    lo = (x - hi.astype(F32)).astype(BF16)
    return hi, lo


def _rms_rows(x, g):
    return x * lax.rsqrt(jnp.mean(x * x, axis=-1, keepdims=True) + NORM_EPS) * g


def _sigmoid(x):
    return 1.0 / (1.0 + jnp.exp(-x))


def _silu(x):
    return x * _sigmoid(x)


def _row_tile(m, tile=ROW_TILE):
    return tile if m % tile == 0 else m


def _even_in_kernel(x_ref, g_ref, wm_ref, wa_ref, wgb_ref, wgate_ref, bgate_ref,
                    u_ref, u2_ref, ga_ref, q_ref, k_ref, v_ref, la_ref, gb_ref, uscr_ref, *, t):
    xb = _rms_rows(x_ref[...], g_ref[...]).astype(BF16)

    def proj(lo, hi):
        return _dot(xb, wm_ref[:, lo:hi])

    u = proj(0, 512)
    nrow = u.shape[0] // t
    for o in range(u_ref.shape[0]):
        uo = u[:, o * LANES:(o + 1) * LANES]
        u_ref[o] = uo.astype(u_ref.dtype)
        uscr_ref[o] = uo
        for tt in range(t):
            piece = uscr_ref[o, pl.ds(tt, nrow, stride=t), :]
            u2_ref[o, :, tt * LANES:(tt + 1) * LANES] = piece.astype(u2_ref.dtype)
    ga_ref[...] = proj(512, 1024).astype(ga_ref.dtype)
    q_ref[...] = (proj(1024, 1280) * (GLA_HEAD_K ** -0.5)).astype(q_ref.dtype)
    k_ref[...] = proj(1280, 1536).astype(k_ref.dtype)
    v_ref[...] = proj(1536, 2048).astype(v_ref.dtype)
    gb_ref[...] = _dot(xb, wgb_ref[...]).astype(gb_ref.dtype)
    a_low = _dot(xb, wa_ref[...])
    logit = _dot(a_low.astype(BF16), wgate_ref[...]) + bgate_ref[...]
    log_sig = jnp.minimum(logit, 0.0) - jnp.log1p(jnp.exp(-jnp.abs(logit)))
    la_ref[...] = log_sig * (1.0 / GLA_GATE_TAU)


def _even_in(x, g, wm, wa, wgb, wgate, bgate, act_dtype, t):
    m, d = x.shape
    tm = _row_tile(m, 2 * ROW_TILE) if m > ROW_TILE else m
    row = lambda n: pl.BlockSpec((tm, n), lambda i: (i, 0))
    no = 512 // LANES
    slab = pl.BlockSpec((no, tm, LANES), lambda i: (0, i, 0))
    chunk = pl.BlockSpec((no, tm // t, t * LANES), lambda i: (0, i, 0))
    outs = [(512, act_dtype), (256, act_dtype), (256, act_dtype), (512, act_dtype), (256, F32), (512, act_dtype)]
    return pl.pallas_call(
        functools.partial(_even_in_kernel, t=t),
        grid=(m // tm,),
        in_specs=[row(d), _full(g.shape), _full(wm.shape), _full(wa.shape), _full(wgb.shape),
                  _full(wgate.shape), _full(bgate.shape)],
        out_specs=[slab, chunk] + [row(n) for n, _ in outs],
        out_shape=[jax.ShapeDtypeStruct((no, m, LANES), act_dtype),
                   jax.ShapeDtypeStruct((no, m // t, t * LANES), act_dtype)]
        + [jax.ShapeDtypeStruct((m, n), dt) for n, dt in outs],
        scratch_shapes=[pltpu.VMEM((no, tm, LANES), F32)],
        compiler_params=_cparams(("parallel",)),
        name="even_in",
    )(x, g, wm, wa, wgb, wgate, bgate)


def _group_mask(shape, row_span, col_span):
    rg = (lax.broadcasted_iota(jnp.int32, shape, 0) // row_span) % OCT
    cg = (lax.broadcasted_iota(jnp.int32, shape, 1) // col_span) % OCT
    return rg == cg


def _s5_state_kernel(u_ref, bre_ref, bim_ref, x0_ref, are_ref, aim_ref, xs_ref, xf_ref, loc_ref, *, nseq, nchunks):
    hw = OCT * S5_STATE
    kk = u_ref.shape[2]
    rep = hw // LANES
    bp = jnp.concatenate([bre_ref[0]] * rep + [bim_ref[0]] * rep, axis=1)
    bp = jnp.where(_group_mask((kk, 2 * hw), S5_GROUP, S5_STATE), bp, 0.0).astype(BF16)
    loc_ref[...] = _dot(u_ref[0], bp)
    a_re = are_ref[0]
    a_im = aim_ref[0]
    if nchunks == 1:
        x0 = x0_ref[0]
        xr, xi = x0[:, :hw], x0[:, hw:]
        loc = loc_ref[...]
        xf_ref[0, :, :hw] = a_re * xr - a_im * xi + loc[:, :hw]
        xf_ref[0, :, hw:] = a_re * xi + a_im * xr + loc[:, hw:]
        xs_ref[0] = x0.astype(xs_ref.dtype)
    else:
        def body(j, carry):
            new = []
            for b in range(nseq):
                xr, xi = carry[b]
                row = pl.ds(b * nchunks + j, 1)
                lr = loc_ref[row, :hw]
                li = loc_ref[row, hw:]
                loc_ref[row, :hw] = xr
                loc_ref[row, hw:] = xi
                new.append((a_re * xr - a_im * xi + lr, a_re * xi + a_im * xr + li))
            return tuple(new)

        init = tuple((x0_ref[0, b:b + 1, :hw], x0_ref[0, b:b + 1, hw:]) for b in range(nseq))
        fin = lax.fori_loop(0, nchunks, body, init, unroll=4)
        for b in range(nseq):
            xf_ref[0, b:b + 1, :hw] = fin[b][0]
            xf_ref[0, b:b + 1, hw:] = fin[b][1]
        xs_ref[0] = loc_ref[...].astype(xs_ref.dtype)


def _s5_state(u2, bre, bim, x0, are, aim, nseq, nchunks):
    no, r, kk = u2.shape
    sw = x0.shape[2]
    blk = lambda a: pl.BlockSpec((1,) + a.shape[1:], lambda o: (o,) + (0,) * (a.ndim - 1))
    return pl.pallas_call(
        functools.partial(_s5_state_kernel, nseq=nseq, nchunks=nchunks),
        grid=(no,),
        in_specs=[blk(u2), blk(bre), blk(bim), blk(x0), blk(are), blk(aim)],
        out_specs=[pl.BlockSpec((1, r, sw), lambda o: (o, 0, 0)), blk(x0)],
        out_shape=[jax.ShapeDtypeStruct((no, r, sw), BF16), jax.ShapeDtypeStruct(x0.shape, F32)],
        scratch_shapes=[pltpu.VMEM((r, sw), F32)],
        compiler_params=_cparams(("parallel",)),
        name="s5_state",
    )(u2, bre, bim, x0, are, aim)


def _s5_out_kernel(u_ref, xs_ref, wc_ref, cc_ref, y_ref, yscr_ref):
    r, kk = u_ref.shape[1], u_ref.shape[2]
    t = kk // LANES
    sw = xs_ref.shape[2]
    ntile = kk // MXU_TILE
    cc = cc_ref[0]
    cp = jnp.concatenate([cc[:S5_STATE]] * OCT + [cc[S5_STATE:]] * OCT, axis=0)
    cp = jnp.where(_group_mask((sw, kk), S5_STATE, S5_GROUP), cp, 0.0).astype(BF16)
    wmask = _group_mask((MXU_TILE, MXU_TILE), S5_GROUP, S5_GROUP)
    wts = []
    for d in range(ntile):
        wc = wc_ref[0, d]
        full = jnp.concatenate([wc[:S5_GROUP]] * OCT + [wc[S5_GROUP:]] * OCT, axis=0)
        wts.append(jnp.where(wmask, full, 0.0).astype(BF16))
    xs = xs_ref[0]
    for n in range(ntile):
        cols = slice(n * MXU_TILE, (n + 1) * MXU_TILE)
        acc = _dot(xs, cp[:, cols])
        for k in range(n + 1):
            acc = acc + _dot(u_ref[0, :, k * MXU_TILE:(k + 1) * MXU_TILE], wts[n - k])
        for e in range(MXU_TILE // LANES):
            yscr_ref[pl.ds(2 * n + e, r, stride=t), :] = acc[:, e * LANES:(e + 1) * LANES]
    y_ref[0] = yscr_ref[...].astype(y_ref.dtype)


def _s5_out(u2, xs, wc, cc, out_dtype):
    no, r, kk = u2.shape
    m = r * (kk // LANES)
    blk = lambda a: pl.BlockSpec((1,) + a.shape[1:], lambda o: (o,) + (0,) * (a.ndim - 1))
    return pl.pallas_call(
        _s5_out_kernel,
        grid=(no,),
        in_specs=[blk(u2), blk(xs), blk(wc), blk(cc)],
        out_specs=pl.BlockSpec((1, m, LANES), lambda o: (o, 0, 0)),
        out_shape=jax.ShapeDtypeStruct((no, m, LANES), out_dtype),
        scratch_shapes=[pltpu.VMEM((m, LANES), F32)],
        compiler_params=_cparams(("parallel",)),
        name="s5_out",
    )(u2, xs, wc, cc)


def _s5_params(lam_re, lam_im, log_dt, b_re, b_im, c_re, c_im):
    t = S5_CHUNK
    ng = lam_re.shape[0]
    no = ng // OCT
    dt = jnp.exp(log_dt)[:, None]
    a = lam_re * dt
    b = lam_im * dt
    n = jnp.arange(t + 1, dtype=F32)[None, :, None]
    mag = jnp.exp(n * a[:, None, :])
    pw_re = mag * jnp.cos(n * b[:, None, :])
    pw_im = mag * jnp.sin(n * b[:, None, :])
    em1_re = jnp.expm1(a) * jnp.cos(b) - 2.0 * jnp.sin(0.5 * b) ** 2
    em1_im = jnp.exp(a) * jnp.sin(b)
    den = lam_re * lam_re + lam_im * lam_im
    z_re = (em1_re * lam_re + em1_im * lam_im) / den
    z_im = (em1_im * lam_re - em1_re * lam_im) / den
    bb_re = z_re[..., None] * b_re - z_im[..., None] * b_im
    bb_im = z_re[..., None] * b_im + z_im[..., None] * b_re
    e_re, e_im = pw_re[:, :t, :, None], pw_im[:, :t, :, None]
    w_re = e_re * bb_re[:, None] - e_im * bb_im[:, None]
    w_im = e_re * bb_im[:, None] + e_im * bb_re[:, None]
    kd = (jnp.einsum('gdpc,gkp->gdck', w_re, c_re, precision=HIGHEST)
          - jnp.einsum('gdpc,gkp->gdck', w_im, c_im, precision=HIGHEST))
    kd_pad = jnp.pad(kd, ((0, 0), (1, 0), (0, 0), (0, 0)))
    nl = t // 2
    lag1 = (2 * jnp.arange(nl)[:, None, None] + jnp.arange(2)[None, None, :] - jnp.arange(2)[None, :, None] + 1)
    x = kd_pad[:, lag1].reshape(no, OCT, nl, 2, 2, S5_GROUP, S5_GROUP)
    wc = x.transpose(0, 2, 3, 5, 4, 1, 6).reshape(no, nl, 2 * S5_GROUP, MXU_TILE)

    def in_op(w):
        w = w[:, ::-1].transpose(0, 1, 3, 2).reshape(no, OCT, t, S5_GROUP, S5_STATE).transpose(0, 2, 1, 3, 4)
        return jnp.concatenate([w, w], axis=-1).reshape(no, t * LANES, 2 * S5_STATE)

    bre, bim = in_op(w_re), in_op(w_im)
    o_re, o_im = pw_re[:, 1:, None, :], pw_im[:, 1:, None, :]
    cl_re = c_re[:, None] * o_re - c_im[:, None] * o_im
    cl_im = c_re[:, None] * o_im + c_im[:, None] * o_re

    def out_op(cl):
        cl = cl.reshape(no, OCT, t, S5_GROUP, S5_STATE).transpose(0, 4, 2, 1, 3)
        return cl.reshape(no, S5_STATE, t * LANES)

    cc = jnp.concatenate([out_op(cl_re), -out_op(cl_im)], axis=1)
    return wc, bre, bim, cc, pw_re, pw_im


def _s5_chunk(seqlen):
    return math.gcd(seqlen, S5_CHUNK)


def _s5_branch(u2, x0, ops, nseq, seqlen, out_dtype):
    wc, bre, bim, cc, pw_re, pw_im = ops
    no = u2.shape[0]
    t = _s5_chunk(seqlen)
    kk = t * LANES
    nchunks = seqlen // t
    are = pw_re[:, t].reshape(no, 1, OCT * S5_STATE)
    aim = pw_im[:, t].reshape(no, 1, OCT * S5_STATE)
    tail = S5_CHUNK * LANES - kk
    xs, xf = _s5_state(u2, bre[:, tail:], bim[:, tail:], x0, are, aim, nseq, nchunks)
    return _s5_out(u2, xs, wc[:, :kk // MXU_TILE], cc[:, :, :kk], out_dtype), xf


def _s5_state_in(re, im, no):
    nseq = re.shape[0]
    f = lambda a: a.reshape(nseq, no, OCT * S5_STATE).transpose(1, 0, 2)
    return jnp.concatenate([f(re), f(im)], axis=-1)


def _s5_state_out(xf):
    no, nseq, _ = xf.shape
    hw = OCT * S5_STATE
    f = lambda a: a.transpose(1, 0, 2).reshape(1, nseq, no * OCT, S5_STATE)
    return f(xf[:, :, :hw]), f(xf[:, :, hw:])


def _gla_kernel(q_ref, k_ref, v_ref, la_ref, gb_ref, s0_ref, ng_ref, o_ref, sf_ref, st_ref,
                *, nseq, nchunks, c):
    i = pl.program_id(1)

    @pl.when(i == 0)
    def _():
        st_ref[...] = s0_ref[...]

    nh, hk, hv = GLA_HEADS, GLA_HEAD_K, GLA_HEAD_V
    nch = nseq * nchunks
    tm = nch * c
    iota = lambda shape, d: lax.broadcasted_iota(jnp.int32, shape, d)
    ng = ng_ref[...]

    la_hi, la_lo = _split_bf16(la_ref[...])
    rt, ct = iota((tm, tm), 0), iota((tm, tm), 1)
    tril = ((rt // c == ct // c) & (rt >= ct)).astype(BF16)
    bcum = _dot(tril, la_hi) + _dot(tril, la_lo)
    seg = (iota((tm, nch * LANES), 0) // c == iota((tm, nch * LANES), 1) // LANES).astype(BF16)
    dec_all = jnp.exp(_dot_tn(la_hi, seg) + _dot_tn(la_lo, seg))
    q_all = q_ref[...].astype(F32)
    k_all = k_ref[...].astype(F32)
    q_dec_all = q_all * jnp.exp(bcum)
    k_dec_all = k_all * jnp.exp(-bcum)

    own_k = iota((nh * c, nh * hk), 0) // c == iota((nh * c, nh * hk), 1) // hk
    own_v = iota((nh * c, nh * hv), 0) // c == iota((nh * c, nh * hv), 1) // hv
    causal = iota((c, nh * c), 1) % c <= iota((c, nh * c), 0)
    zero_v = jnp.zeros((hk, hv), F32)

    sts = None
    for ci in range(nch):
        s, first, last = ci // nchunks, ci % nchunks == 0, ci % nchunks == nchunks - 1
        if first:
            sts = [st_ref[s, h] for h in range(nh)]
        rows = slice(ci * c, (ci + 1) * c)
        q_dec = q_dec_all[rows].astype(BF16)
        k_dec = k_dec_all[rows]
        b_c = bcum[rows]
        k_tail = (k_all[rows] * jnp.exp(b_c[c - 1:c] - b_c)).astype(BF16)
        v = v_ref[rows, :].astype(F32)
        k_bd = jnp.where(own_k, jnp.concatenate([k_dec] * nh, axis=0), 0.0).astype(BF16)
        v_bd = jnp.where(own_v, jnp.concatenate([v] * nh, axis=0), 0.0).astype(BF16)
        st_bd = jnp.concatenate(
            [jnp.concatenate([zero_v] * h + [sts[h]] + [zero_v] * (nh - 1 - h), axis=1) for h in range(nh)],
            axis=0).astype(BF16)
        att = jnp.where(causal, _dot_nt(q_dec, k_bd), 0.0)
        o = _dot(att.astype(BF16), v_bd) + _dot(q_dec, st_bd)
        kv = _dot_tn(k_tail, v.astype(BF16))
        for h in range(nh):
            ks = slice(h * hk, (h + 1) * hk)
            vs = slice(h * hv, (h + 1) * hv)
            sts[h] = dec_all[ks, ci * LANES:(ci + 1) * LANES] * sts[h] + kv[ks, vs]
            oh = _rms_rows(o[:, vs], ng) * _silu(gb_ref[rows, vs].astype(F32))
            o_ref[rows, vs] = oh.astype(o_ref.dtype)
        if last:
            for h in range(nh):
                st_ref[s, h] = sts[h]

    @pl.when(i == pl.num_programs(1) - 1)
    def _():
        sf_ref[...] = st_ref[...]


def _gla(q, k, v, la, gb, s0, ng, nseq_total, seqlen, out_dtype):
    c = math.gcd(seqlen, GLA_CHUNK)
    if seqlen >= 8 * c:
        nseq, nchunks = 1, 8
    else:
        nseq, nchunks = 8, seqlen // c
    assert nseq_total % nseq == 0 and seqlen % (nchunks * c) == 0
    nblk = seqlen // (nchunks * c)
    tm = nseq * nchunks * c
    row = lambda n: pl.BlockSpec((tm, n), lambda b, i: (b * nblk + i, 0))
    st_spec = pl.BlockSpec((nseq, GLA_HEADS, GLA_HEAD_K, GLA_HEAD_V), lambda b, i: (b, 0, 0, 0))
    m = q.shape[0]
    return pl.pallas_call(
        functools.partial(_gla_kernel, nseq=nseq, nchunks=nchunks, c=c),
        grid=(nseq_total // nseq, nblk),
        in_specs=[row(256), row(256), row(512), row(256), row(512), st_spec, _full(ng.shape)],
        out_specs=[row(512), st_spec],
        out_shape=[jax.ShapeDtypeStruct((m, 512), out_dtype), jax.ShapeDtypeStruct(s0.shape, F32)],
        scratch_shapes=[pltpu.VMEM((nseq, GLA_HEADS, GLA_HEAD_K, GLA_HEAD_V), F32)],
        compiler_params=_cparams(("parallel", "arbitrary")),
        name="gla",
    )(q, k, v, la, gb, s0, ng)


def _gelu_tanh(x):
    return 0.5 * x * (1.0 + jnp.tanh(math.sqrt(2.0 / math.pi) * (x + 0.044715 * (x * x * x))))


def _even_out_rows(rows, x_ref, y_ref, u_ref, ga_ref, ob_ref, d_ref, wglu_ref, bglu_ref, woa_ref, wob_ref):
    no = y_ref.shape[0]
    y = jnp.concatenate([y_ref[o, rows, :].astype(F32) for o in range(no)], axis=1)
    u = jnp.concatenate([u_ref[o, rows, :].astype(F32) for o in range(no)], axis=1)
    z = _gelu_tanh(y + d_ref[...] * u)
    z = z * _sigmoid(_dot(z.astype(BF16), wglu_ref[...]) + bglu_ref[...])
    out_a = z * _silu(ga_ref[rows, :].astype(F32))
    mix = _dot(out_a.astype(BF16), woa_ref[...]) + _dot(ob_ref[rows, :].astype(BF16), wob_ref[...])
    return x_ref[rows, :] + mix


def _head_ones(n):
    r = lax.broadcasted_iota(jnp.int32, (n, n), 0) // SWA_HEAD_DIM
    c = lax.broadcasted_iota(jnp.int32, (n, n), 1) // SWA_HEAD_DIM
    return (r == c).astype(BF16)


def _rope_block(x, cos_t, sin_t, upper):
    swapped = jnp.where(upper, pltpu.roll(x, 32, 1), pltpu.roll(x, 96, 1))
    return x * cos_t + swapped * sin_t


def _mid_kernel(x_ref, y_ref, u_ref, ga_ref, ob_ref, d_ref, wglu_ref, bglu_ref, woa_ref, wob_ref,
                g_ref, w_ref, gq_ref, gk_ref, cos_ref, sin_ref, h_ref, q_ref, k_ref, v_ref, gate_ref):
    tm = x_ref.shape[0]
    rows = slice(0, tm)
    lane = lax.broadcasted_iota(jnp.int32, (tm, LANES), 1)
    upper = (lane & 32) != 0
    ones4 = _head_ones(MXU_TILE)
    inv_d = 1.0 / SWA_HEAD_DIM
    qw = SWA_HEADS * SWA_HEAD_DIM
    kw = SWA_KV_HEADS * SWA_HEAD_DIM
    nblk = qw // MXU_TILE
    h = _even_out_rows(rows, x_ref, y_ref, u_ref, ga_ref, ob_ref, d_ref, wglu_ref, bglu_ref, woa_ref, wob_ref)
    h_ref[...] = h
    xb = _rms_rows(h, g_ref[...]).astype(BF16)
    cos_t = cos_ref[...]
    sin_t = sin_ref[...]

    def q_block(j):
        return _dot(xb, w_ref[:, j * MXU_TILE:(j + 1) * MXU_TILE])

    def q_finish(j, q):
        ss = _dot((q * q).astype(BF16), ones4)
        qn = q * lax.rsqrt(ss * inv_d + NORM_EPS) * gq_ref[...]
        for e in range(MXU_TILE // LANES):
            cols = slice(j * MXU_TILE + e * LANES, j * MXU_TILE + (e + 1) * LANES)
            qe = _rope_block(qn[:, e * LANES:(e + 1) * LANES], cos_t, sin_t, upper)
            q_ref[:, cols] = (qe * (SWA_HEAD_DIM ** -0.5 * LOG2E)).astype(q_ref.dtype)

    q_next = q_block(0)
    for j in range(nblk):
        q_cur = q_next
        q_next = q_block(j + 1) if j + 1 < nblk else _dot(xb, w_ref[:, qw:qw + 2 * kw])
        q_finish(j, q_cur)
    kv = q_next
    gate = _dot(xb, w_ref[:, qw + 2 * kw:])
    k = kv[:, :kw]
    ss = _dot((k * k).astype(BF16), ones4[:kw, :kw])
    kn = k * lax.rsqrt(ss * inv_d + NORM_EPS) * gk_ref[...]
    k_ref[...] = _rope_block(kn, cos_t, sin_t, upper)
    v_ref[...] = kv[:, kw:]
    gate_ref[...] = gate.astype(gate_ref.dtype)


def _mid(x, y, u, ga, ob, pe, po, cos_t, sin_t, act_dtype):
    m, dm = x.shape
    tm = _row_tile(m)
    row = lambda n: pl.BlockSpec((tm, n), lambda i: (i, 0))
    slab = pl.BlockSpec((y.shape[0], tm, LANES), lambda i: (0, i, 0))
    assert cos_t.shape[0] % tm == 0
    nper = cos_t.shape[0] // tm
    tab = pl.BlockSpec((tm, LANES), lambda i: (i % nper, 0))
    qw = SWA_HEADS * SWA_HEAD_DIM
    kw = SWA_KV_HEADS * SWA_HEAD_DIM
    weights = [pe['d'], pe['wglu'], pe['bglu'], pe['woa'], pe['wob'], po['norm_g'], po['w_in'], po['gq'], po['gk']]
    return pl.pallas_call(
        _mid_kernel,
        grid=(m // tm,),
        in_specs=[row(dm), slab, slab, row(512), row(512)] + [_full(w.shape) for w in weights] + [tab, tab],
        out_specs=[row(dm), row(qw), row(kw), row(kw), row(qw)],
        out_shape=[jax.ShapeDtypeStruct((m, dm), F32),
                   jax.ShapeDtypeStruct((m, qw), act_dtype), jax.ShapeDtypeStruct((m, kw), F32),
                   jax.ShapeDtypeStruct((m, kw), F32), jax.ShapeDtypeStruct((m, qw), act_dtype)],
        compiler_params=_cparams(("parallel",)),
        name="mid",
    )(x, y, u, ga, ob, *weights, cos_t, sin_t)


def _rope_tables(pos):
    half = SWA_HEAD_DIM // 2
    inv_freq = ROPE_THETA ** (-np.arange(half, dtype=np.float64) / half)
    ang = np.asarray(pos, np.float64)[:, None] * inv_freq[None, :]
    cos, sin = np.cos(ang), np.sin(ang)
    cos_t = np.concatenate([cos, cos, cos, cos], axis=1).astype(np.float32)
    sin_t = np.concatenate([-sin, sin, -sin, sin], axis=1).astype(np.float32)
    return jnp.asarray(cos_t), jnp.asarray(sin_t)


def _attn_prompt_kernel(sink_ref, q_ref, kc_ref, kp_ref, vc_ref, vp_ref, gate_ref, h_ref, wout_ref,
                        y_ref, p_ref, o_ref, *, nqb):
    i = pl.program_id(1)
    w = SWA_WINDOW
    hd = SWA_HEAD_DIM
    npair = SWA_GROUP // 2
    lane2 = lax.broadcasted_iota(jnp.int32, (2 * w, LANES), 1)
    low = lane2 < hd
    rr = lax.broadcasted_iota(jnp.int32, (w, w), 0)
    cc = lax.broadcasted_iota(jnp.int32, (w, w), 1)
    tri = cc <= rr
    low_w = cc < hd
    r4 = lax.broadcasted_iota(jnp.int32, (4 * w, LANES), 0)
    c4 = lax.broadcasted_iota(jnp.int32, (4 * w, LANES), 1)
    den_cols = ((r4 < 2 * w) == (c4 < hd)).astype(BF16)
    units = [(jb, kv) for jb in range(nqb) for kv in range(SWA_KV_HEADS)]

    def scores(jb, kv):
        rows = slice(jb * w, (jb + 1) * w)
        if jb == 0:
            k_prev, v_prev = kp_ref[...], vp_ref[...]
        else:
            prev_rows = slice((jb - 1) * w, jb * w)
            k_prev, v_prev = kc_ref[prev_rows, :], vc_ref[prev_rows, :]
        kcat = jnp.concatenate([k_prev, kc_ref[rows, :]], axis=0)
        vcat = jnp.concatenate([v_prev, vc_ref[rows, :]], axis=0)
        own = low if kv == 0 else jnp.logical_not(low)
        k_own = jnp.where(own, kcat, 0.0)
        v_own = jnp.where(own, vcat, 0.0)
        k_oth = pltpu.roll(k_own, hd, 1)
        v_oth = pltpu.roll(v_own, hd, 1)
        k_lo, k_hi = (k_own, k_oth) if kv == 0 else (k_oth, k_own)
        v_lo, v_hi = (v_own, v_oth) if kv == 0 else (v_oth, v_own)
        k_rhs = jnp.concatenate([k_lo, k_hi], axis=0).astype(BF16)
        v_rhs = jnp.concatenate([jnp.concatenate([v_lo, v_hi], axis=0).astype(BF16), den_cols], axis=1)
        qs = jnp.concatenate([q_ref[rows, (kv * npair + pr) * LANES:(kv * npair + pr + 1) * LANES]
                              for pr in range(npair)], axis=0)
        return _dot_nt(qs, k_rhs), v_rhs

    def softmax_pv(kv, first, buf, s_all, v_rhs):
        sink_terms = []
        for pr in range(npair):
            prow = slice(pr * w, (pr + 1) * w)
            pair_terms = []
            for e in range(2):
                s_prev = s_all[prow, (2 * e) * w:(2 * e + 1) * w]
                s_cur = s_all[prow, (2 * e + 1) * w:(2 * e + 2) * w]
                if first:
                    s_prev = jnp.where(i > 0, s_prev, -jnp.inf)
                sc = jnp.where(tri, s_cur, s_prev)
                sink = sink_ref[2 * (kv * npair + pr) + e] * LOG2E
                mx = jnp.maximum(jnp.max(sc, axis=-1, keepdims=True), sink)
                pe = jnp.exp2(sc - mx)
                p_ref[buf, prow, (2 * e) * w:(2 * e + 1) * w] = jnp.where(tri, 0.0, pe).astype(BF16)
                p_ref[buf, prow, (2 * e + 1) * w:(2 * e + 2) * w] = jnp.where(tri, pe, 0.0).astype(BF16)
                pair_terms.append(jnp.exp2(sink - mx))
            sink_terms.append(pair_terms)
        return _dot(p_ref[buf], v_rhs), sink_terms

    def normalise(jb, kv, o_ext, sink_terms):
        rows = slice(jb * w, (jb + 1) * w)
        for pr in range(npair):
            prow = slice(pr * w, (pr + 1) * w)
            cols = slice((kv * npair + pr) * LANES, (kv * npair + pr + 1) * LANES)
            st = jnp.where(low_w, sink_terms[pr][0], sink_terms[pr][1])
            o = o_ext[prow, :LANES] / (o_ext[prow, LANES:] + st)
            o_ref[rows, cols] = o.astype(o_ref.dtype)

    pending = scores(*units[0])
    unfinished = None
    for n, (jb, kv) in enumerate(units):
        current = pending
        if n + 1 < len(units):
            pending = scores(*units[n + 1])
        result = softmax_pv(kv, jb == 0, n % 2, *current)
        if unfinished is not None:
            normalise(*unfinished)
        unfinished = (jb, kv) + result
    normalise(*unfinished)
    og = o_ref[...].astype(F32) * _silu(gate_ref[...].astype(F32))
    y_ref[...] = h_ref[...] + _dot(og.astype(BF16), wout_ref[...])


def _attn_prompt(sinks, q, k, v, gate, h, w_out, nseq, seqlen):
    w = SWA_WINDOW
    nqb = 4
    tm = nqb * w
    assert seqlen % tm == 0
    nblk = seqlen // tm
    qw = q.shape[1]
    dm = h.shape[1]
    row = lambda n: pl.BlockSpec((tm, n), lambda b, i: (b * nblk + i, 0))
    prev = lambda n: pl.BlockSpec((w, n), lambda b, i: (jnp.maximum((b * nblk + i) * nqb - 1, 0), 0))
    smem = pl.BlockSpec(memory_space=pltpu.SMEM)
    return pl.pallas_call(
        functools.partial(_attn_prompt_kernel, nqb=nqb),
        grid=(nseq, nblk),
        in_specs=[smem, row(qw), row(LANES), prev(LANES), row(LANES), prev(LANES), row(qw), row(dm),
                  _full(w_out.shape)],
        out_specs=row(dm),
        out_shape=jax.ShapeDtypeStruct(h.shape, F32),
        scratch_shapes=[pltpu.VMEM((2, 4 * w, 4 * w), BF16), pltpu.VMEM((tm, qw), BF16)],
        compiler_params=_cparams(("parallel", "arbitrary")),
        name="attn_prompt",
    )(sinks, q, k, k, v, v, gate, h, w_out)


def _attn_sample_kernel(sink_ref, q_ref, kn_ref, vn_ref, kc_ref, vc_ref, o_ref, ko_ref, vo_ref, *, seqlen):
    nq = SWA_HEADS * seqlen
    ncache = kc_ref.shape[1]
    t_row = lax.broadcasted_iota(jnp.int32, (nq, ncache), 0) % seqlen
    c_col = lax.broadcasted_iota(jnp.int32, (nq, ncache), 1)
    cache_ok = c_col > t_row - (SWA_WINDOW - ncache)
    nnew = kn_ref.shape[1]
    t_row_n = lax.broadcasted_iota(jnp.int32, (nq, nnew), 0) % seqlen
    n_col = lax.broadcasted_iota(jnp.int32, (nq, nnew), 1)
    new_ok = n_col <= t_row_n
    sink = (sink_ref[...] * LOG2E)[None, :, 0:1]
    bqk = lambda a, b: lax.dot_general(a, b, (((2,), (2,)), ((0,), (0,))), preferred_element_type=F32)
    bpv = lambda a, b: lax.dot_general(a, b, (((2,), (1,)), ((0,), (0,))), preferred_element_type=F32)
    q = q_ref[...]
    kc = kc_ref[...]
    vc = vc_ref[...]
    kn = kn_ref[...]
    vn = vn_ref[...]
    sc = jnp.where(cache_ok[None], bqk(q, kc.astype(BF16)), -jnp.inf)
    sn = jnp.where(new_ok[None], bqk(q, kn.astype(BF16)), -jnp.inf)
    mx = jnp.maximum(jnp.maximum(jnp.max(sc, axis=-1, keepdims=True), jnp.max(sn, axis=-1, keepdims=True)), sink)
    pc = jnp.exp2(sc - mx)
    pn = jnp.exp2(sn - mx)
    den = jnp.sum(pc, axis=-1, keepdims=True) + jnp.sum(pn, axis=-1, keepdims=True) + jnp.exp2(sink - mx)
    inv = 1.0 / den
    o_ref[...] = bpv((pc * inv).astype(BF16), vc.astype(BF16)) + bpv((pn * inv).astype(BF16), vn.astype(BF16))
    keep = ncache - seqlen
    ko_ref[:, 0:keep, :] = kc[:, seqlen:ncache, :]
    ko_ref[:, keep:ncache, :] = kn[:, 0:seqlen, :]
    vo_ref[:, 0:keep, :] = vc[:, seqlen:ncache, :]
    vo_ref[:, keep:ncache, :] = vn[:, 0:seqlen, :]


def _attn_sample(sink_rows, q, kn, vn, kc, vc, seqlen):
    n = q.shape[0]
    nseq = 8
    assert n % nseq == 0
    blk = lambda a: pl.BlockSpec((nseq,) + a.shape[1:], lambda i: (i, 0, 0))
    return pl.pallas_call(
        functools.partial(_attn_sample_kernel, seqlen=seqlen),
        grid=(n // nseq,),
        in_specs=[_full(sink_rows.shape), blk(q), blk(kn), blk(vn), blk(kc), blk(vc)],
        out_specs=[blk(q), blk(kc), blk(vc)],
        out_shape=[jax.ShapeDtypeStruct(q.shape, F32), jax.ShapeDtypeStruct(kc.shape, F32),
                   jax.ShapeDtypeStruct(vc.shape, F32)],
        compiler_params=_cparams(("parallel",)),
        name="attn_sample",
    )(sink_rows, q, kn, vn, kc, vc)


def _odd_out_kernel(h_ref, o_ref, gate_ref, w_ref, y_ref):
    og = o_ref[...].astype(F32) * _silu(gate_ref[...].astype(F32))
    y_ref[...] = h_ref[...] + _dot(og.astype(BF16), w_ref[...])


def _odd_out(h, o, gate, w):
    m, dm = h.shape
    tm = _row_tile(m)
    row = lambda n: pl.BlockSpec((tm, n), lambda i: (i, 0))
    return pl.pallas_call(
        _odd_out_kernel,
        grid=(m // tm,),
        in_specs=[row(dm), row(o.shape[1]), row(gate.shape[1]), _full(w.shape)],
        out_specs=row(dm),
        out_shape=jax.ShapeDtypeStruct((m, dm), F32),
        compiler_params=_cparams(("parallel",)),
        name="odd_out",
    )(h, o, gate, w)


def _trunk(x, s5_x0, gla_s0, pe, po, pos, nseq, seqlen, act_dtype):
    u4, u2, ga, q, k, v, la, gb = _even_in(x, pe['norm_g'], pe['wm'], pe['wa'], pe['wgb'], pe['wgate'],
                                           pe['bgate'], act_dtype, _s5_chunk(seqlen))
    y4, s5_fin = _s5_branch(u2, s5_x0, pe['s5_ops'], nseq, seqlen, act_dtype)
    ob, gla_fin = _gla(q, k, v, la, gb, gla_s0, pe['gla_norm_g'], nseq, seqlen, act_dtype)
    cos_t, sin_t = _rope_tables(pos)
    h, q1, k1, v1, gate = _mid(x, y4, u4, ga, ob, pe, po, cos_t, sin_t, act_dtype)
    return h, q1, k1, v1, gate, s5_fin, gla_fin


def kernel(x_prompt, x_sample, state_s5_re, state_s5_im, state_gla, cache_swa_k, cache_swa_v,
           even_norm_g, even_w_in, s5_lambda_re, s5_lambda_im, s5_log_dt, s5_b_re, s5_b_im,
           s5_c_re, s5_c_im, s5_d, s5_w_glu, s5_b_glu, gla_w_gate, gla_b_gate, gla_norm_g,
           even_w_out, odd_norm_g, odd_w_in, swa_q_norm_g, swa_k_norm_g, swa_sinks, odd_w_out):
    nb, seq, dm = x_prompt.shape
    ns, dseq, _ = x_sample.shape
    ng = s5_lambda_re.shape[1]
    no = ng // OCT
    xp = x_prompt.reshape(nb * seq, dm)
    xs = x_sample.reshape(ns * dseq, dm)

    i = 0
    w_in = even_w_in[i]
    s5w = ng * S5_GROUP
    col_alow = 2 * s5w + 2 * GLA_HEADS * GLA_HEAD_K + GLA_HEADS * GLA_HEAD_V
    col_gb = col_alow + GLA_GATE_RANK
    pad_rank = LANES - GLA_GATE_RANK
    pe = {
        'norm_g': even_norm_g[i][None, :],
        'wm': w_in[:, :col_alow].astype(BF16),
        'wa': jnp.pad(w_in[:, col_alow:col_gb], ((0, 0), (0, pad_rank))).astype(BF16),
        'wgb': w_in[:, col_gb:].astype(BF16),
        'wgate': jnp.pad(gla_w_gate[i], ((0, pad_rank), (0, 0))).astype(BF16),
        'bgate': gla_b_gate[i][None, :],
        's5_ops': _s5_params(s5_lambda_re[i], s5_lambda_im[i], s5_log_dt[i], s5_b_re[i], s5_b_im[i],
                             s5_c_re[i], s5_c_im[i]),
        'gla_norm_g': gla_norm_g[i][None, :],
        'd': s5_d[i][None, :],
        'wglu': s5_w_glu[i].astype(BF16),
        'bglu': s5_b_glu[i][None, :],
        'woa': even_w_out[i][:s5w].astype(BF16),
        'wob': even_w_out[i][s5w:].astype(BF16),
    }
    po = {
        'norm_g': odd_norm_g[i][None, :],
        'w_in': odd_w_in[i].astype(BF16),
        'gq': jnp.tile(swa_q_norm_g[i], MXU_TILE // SWA_HEAD_DIM)[None, :],
        'gk': jnp.tile(swa_k_norm_g[i], LANES // SWA_HEAD_DIM)[None, :],
    }
    w_out = odd_w_out[i].astype(BF16)
    sinks = swa_sinks[i]
    kvw = SWA_KV_HEADS * SWA_HEAD_DIM

    s5_zero = jnp.zeros((no, nb, 2 * OCT * S5_STATE), F32)
    gla_zero = jnp.zeros((nb, GLA_HEADS, GLA_HEAD_K, GLA_HEAD_V), F32)
    hp, q, k, v, gate, s5_p, gla_p = _trunk(xp, s5_zero, gla_zero, pe, po, np.arange(seq), nb, seq, BF16)
    s5r_p, s5i_p = _s5_state_out(s5_p)
    y_prompt = _attn_prompt(sinks, q, k, v, gate, hp, w_out, nb, seq).reshape(nb, seq, dm)
    cache_len = min(SWA_WINDOW, seq)
    swk_p = k.reshape(nb, seq, SWA_KV_HEADS, SWA_HEAD_DIM)[None, :, seq - cache_len:]
    swv_p = v.reshape(nb, seq, SWA_KV_HEADS, SWA_HEAD_DIM)[None, :, seq - cache_len:]

    pos_s = np.tile(PAST_LEN + np.arange(dseq), ns)
    s5_init = _s5_state_in(state_s5_re[i], state_s5_im[i], no)
    hs, q, k, v, gate, s5_s, gla_s = _trunk(xs, s5_init, state_gla[i], pe, po, pos_s, ns, dseq, F32)
    s5r_s, s5i_s = _s5_state_out(s5_s)
    ncache = cache_swa_k.shape[2]
    qh = q.reshape(ns, dseq, SWA_KV_HEADS, SWA_GROUP, SWA_HEAD_DIM).transpose(0, 2, 3, 1, 4)
    zq = jnp.zeros_like(qh[:, 0])
    q2 = jnp.stack([jnp.concatenate([qh[:, 0], zq], axis=-1), jnp.concatenate([zq, qh[:, 1]], axis=-1)], axis=1)
    q2 = q2.reshape(ns, SWA_HEADS * dseq, kvw).astype(BF16)
    sink_rows = jnp.broadcast_to(jnp.repeat(sinks, dseq)[:, None], (SWA_HEADS * dseq, LANES))
    kn = k.reshape(ns, dseq, kvw)
    vn = v.reshape(ns, dseq, kvw)
    npad = 16 - dseq
    kn_pad = jnp.pad(kn, ((0, 0), (0, npad), (0, 0)))
    vn_pad = jnp.pad(vn, ((0, 0), (0, npad), (0, 0)))
    kc = cache_swa_k[i].reshape(ns, ncache, kvw)
    vc = cache_swa_v[i].reshape(ns, ncache, kvw)
    o2, kc_new, vc_new = _attn_sample(sink_rows, q2, kn_pad, vn_pad, kc, vc, dseq)
    o2 = o2.reshape(ns, SWA_KV_HEADS, SWA_GROUP, dseq, SWA_KV_HEADS, SWA_HEAD_DIM)
    o = jnp.stack([o2[:, 0, :, :, 0], o2[:, 1, :, :, 1]], axis=1)
    o = o.transpose(0, 3, 1, 2, 4).reshape(ns * dseq, SWA_HEADS * SWA_HEAD_DIM)
    y_sample = _odd_out(hs, o, gate, w_out).reshape(ns, dseq, dm)
    swk_s = kc_new.reshape(1, ns, ncache, SWA_KV_HEADS, SWA_HEAD_DIM)
    swv_s = vc_new.reshape(1, ns, ncache, SWA_KV_HEADS, SWA_HEAD_DIM)

    return (y_prompt, y_sample,
            s5r_p, s5i_p, gla_p[None], swk_p, swv_p,
            s5r_s, s5i_s, gla_s[None], swk_s, swv_s)
```

```python
import functools
import math

import jax
import jax.numpy as jnp
import numpy as np
from jax import lax
from jax.experimental import pallas as pl
from jax.experimental.pallas import tpu as pltpu

F32 = jnp.float32
BF16 = jnp.bfloat16

PAST_LEN = 8192
NORM_EPS = 1e-6
S5_GROUP = 16
S5_STATE = 64
S5_CHUNK = 16
GLA_HEADS = 4
GLA_HEAD_K = 64
GLA_HEAD_V = 128
GLA_GATE_RANK = 16
GLA_GATE_TAU = 16.0
GLA_CHUNK = 64
SWA_HEADS = 16
SWA_KV_HEADS = 2
SWA_GROUP = SWA_HEADS // SWA_KV_HEADS
SWA_HEAD_DIM = 64
SWA_WINDOW = 128
ROPE_THETA = 10000.0
LOG2E = math.log2(math.e)
LANES = 128
MXU_TILE = 256
OCT = LANES // S5_GROUP
ROW_TILE = 512
VMEM_LIMIT = 48 * 1024 * 1024

HIGHEST = lax.Precision.HIGHEST


def _cparams(sem):
    return pltpu.CompilerParams(dimension_semantics=sem, vmem_limit_bytes=VMEM_LIMIT)


def _full(shape):
    n = len(shape)
    return pl.BlockSpec(shape, lambda *_: (0,) * n)


def _dot(a, b):
    return jnp.dot(a, b, preferred_element_type=F32)


def _dot_nt(a, b):
    return lax.dot_general(a, b, (((1,), (1,)), ((), ())), preferred_element_type=F32)


def _dot_tn(a, b):
    return lax.dot_general(a, b, (((0,), (0,)), ((), ())), preferred_element_type=F32)


def _split_bf16(x):
    hi = x.astype(BF16)
    lo = (x - hi.astype(F32)).astype(BF16)
    return hi, lo


def _rms_rows(x, g):
    return x * lax.rsqrt(jnp.mean(x * x, axis=-1, keepdims=True) + NORM_EPS) * g


def _sigmoid(x):
    return 1.0 / (1.0 + jnp.exp(-x))


def _silu(x):
    return x * _sigmoid(x)


def _row_tile(m, tile=ROW_TILE):
    return tile if m % tile == 0 else m


def _even_in_kernel(x_ref, g_ref, wm_ref, wa_ref, wgb_ref, wgate_ref, bgate_ref,
                    u_ref, u2_ref, ga_ref, q_ref, k_ref, v_ref, la_ref, gb_ref, uscr_ref, *, t):
    xb = _rms_rows(x_ref[...], g_ref[...]).astype(BF16)

    def proj(lo, hi):
        return _dot(xb, wm_ref[:, lo:hi])

    u = proj(0, 512)
    nrow = u.shape[0] // t
    for o in range(u_ref.shape[0]):
        uo = u[:, o * LANES:(o + 1) * LANES]
        u_ref[o] = uo.astype(u_ref.dtype)
        uscr_ref[o] = uo
        for tt in range(t):
            piece = uscr_ref[o, pl.ds(tt, nrow, stride=t), :]
            u2_ref[o, :, tt * LANES:(tt + 1) * LANES] = piece.astype(u2_ref.dtype)
    ga_ref[...] = proj(512, 1024).astype(ga_ref.dtype)
    q_ref[...] = (proj(1024, 1280) * (GLA_HEAD_K ** -0.5)).astype(q_ref.dtype)
    k_ref[...] = proj(1280, 1536).astype(k_ref.dtype)
    v_ref[...] = proj(1536, 2048).astype(v_ref.dtype)
    gb_ref[...] = _dot(xb, wgb_ref[...]).astype(gb_ref.dtype)
    a_low = _dot(xb, wa_ref[...])
    logit = _dot(a_low.astype(BF16), wgate_ref[...]) + bgate_ref[...]
    log_sig = jnp.minimum(logit, 0.0) - jnp.log1p(jnp.exp(-jnp.abs(logit)))
    la_ref[...] = log_sig * (1.0 / GLA_GATE_TAU)


def _even_in(x, g, wm, wa, wgb, wgate, bgate, act_dtype, t):
    m, d = x.shape
    tm = _row_tile(m, 2 * ROW_TILE) if m > ROW_TILE else m
    row = lambda n: pl.BlockSpec((tm, n), lambda i: (i, 0))
    no = 512 // LANES
    slab = pl.BlockSpec((no, tm, LANES), lambda i: (0, i, 0))
    chunk = pl.BlockSpec((no, tm // t, t * LANES), lambda i: (0, i, 0))
    outs = [(512, act_dtype), (256, act_dtype), (256, act_dtype), (512, act_dtype), (256, F32), (512, act_dtype)]
    return pl.pallas_call(
        functools.partial(_even_in_kernel, t=t),
        grid=(m // tm,),
        in_specs=[row(d), _full(g.shape), _full(wm.shape), _full(wa.shape), _full(wgb.shape),
                  _full(wgate.shape), _full(bgate.shape)],
        out_specs=[slab, chunk] + [row(n) for n, _ in outs],
        out_shape=[jax.ShapeDtypeStruct((no, m, LANES), act_dtype),
                   jax.ShapeDtypeStruct((no, m // t, t * LANES), act_dtype)]
        + [jax.ShapeDtypeStruct((m, n), dt) for n, dt in outs],
        scratch_shapes=[pltpu.VMEM((no, tm, LANES), F32)],
        compiler_params=_cparams(("parallel",)),
        name="even_in",
    )(x, g, wm, wa, wgb, wgate, bgate)


def _group_mask(shape, row_span, col_span):
    rg = (lax.broadcasted_iota(jnp.int32, shape, 0) // row_span) % OCT
    cg = (lax.broadcasted_iota(jnp.int32, shape, 1) // col_span) % OCT
    return rg == cg


def _s5_state_kernel(u_ref, bre_ref, bim_ref, x0_ref, are_ref, aim_ref, xs_ref, xf_ref, loc_ref, *, nseq, nchunks):
    hw = OCT * S5_STATE
    kk = u_ref.shape[2]
    rep = hw // LANES
    bp = jnp.concatenate([bre_ref[0]] * rep + [bim_ref[0]] * rep, axis=1)
    bp = jnp.where(_group_mask((kk, 2 * hw), S5_GROUP, S5_STATE), bp, 0.0).astype(BF16)
    loc_ref[...] = _dot(u_ref[0], bp)
    a_re = are_ref[0]
    a_im = aim_ref[0]
    if nchunks == 1:
        x0 = x0_ref[0]
        xr, xi = x0[:, :hw], x0[:, hw:]
        loc = loc_ref[...]
        xf_ref[0, :, :hw] = a_re * xr - a_im * xi + loc[:, :hw]
        xf_ref[0, :, hw:] = a_re * xi + a_im * xr + loc[:, hw:]
        xs_ref[0] = x0.astype(xs_ref.dtype)
    else:
        def body(j, carry):
            new = []
            for b in range(nseq):
                xr, xi = carry[b]
                row = pl.ds(b * nchunks + j, 1)
                lr = loc_ref[row, :hw]
                li = loc_ref[row, hw:]
                loc_ref[row, :hw] = xr
                loc_ref[row, hw:] = xi
                new.append((a_re * xr - a_im * xi + lr, a_re * xi + a_im * xr + li))
            return tuple(new)

        init = tuple((x0_ref[0, b:b + 1, :hw], x0_ref[0, b:b + 1, hw:]) for b in range(nseq))
        fin = lax.fori_loop(0, nchunks, body, init, unroll=4)
        for b in range(nseq):
            xf_ref[0, b:b + 1, :hw] = fin[b][0]
            xf_ref[0, b:b + 1, hw:] = fin[b][1]
        xs_ref[0] = loc_ref[...].astype(xs_ref.dtype)


def _s5_state(u2, bre, bim, x0, are, aim, nseq, nchunks):
    no, r, kk = u2.shape
    sw = x0.shape[2]
    blk = lambda a: pl.BlockSpec((1,) + a.shape[1:], lambda o: (o,) + (0,) * (a.ndim - 1))
    return pl.pallas_call(
        functools.partial(_s5_state_kernel, nseq=nseq, nchunks=nchunks),
        grid=(no,),
        in_specs=[blk(u2), blk(bre), blk(bim), blk(x0), blk(are), blk(aim)],
        out_specs=[pl.BlockSpec((1, r, sw), lambda o: (o, 0, 0)), blk(x0)],
        out_shape=[jax.ShapeDtypeStruct((no, r, sw), BF16), jax.ShapeDtypeStruct(x0.shape, F32)],
        scratch_shapes=[pltpu.VMEM((r, sw), F32)],
        compiler_params=_cparams(("parallel",)),
        name="s5_state",
    )(u2, bre, bim, x0, are, aim)


def _s5_out_kernel(u_ref, xs_ref, wc_ref, cc_ref, y_ref, yscr_ref):
    r, kk = u_ref.shape[1], u_ref.shape[2]
    t = kk // LANES
    sw = xs_ref.shape[2]
    ntile = kk // MXU_TILE
    cc = cc_ref[0]
    cp = jnp.concatenate([cc[:S5_STATE]] * OCT + [cc[S5_STATE:]] * OCT, axis=0)
    cp = jnp.where(_group_mask((sw, kk), S5_STATE, S5_GROUP), cp, 0.0).astype(BF16)
    wmask = _group_mask((MXU_TILE, MXU_TILE), S5_GROUP, S5_GROUP)
    wts = []
    for d in range(ntile):
        wc = wc_ref[0, d]
        full = jnp.concatenate([wc[:S5_GROUP]] * OCT + [wc[S5_GROUP:]] * OCT, axis=0)
        wts.append(jnp.where(wmask, full, 0.0).astype(BF16))
    xs = xs_ref[0]
    for n in range(ntile):
        cols = slice(n * MXU_TILE, (n + 1) * MXU_TILE)
        acc = _dot(xs, cp[:, cols])
        for k in range(n + 1):
            acc = acc + _dot(u_ref[0, :, k * MXU_TILE:(k + 1) * MXU_TILE], wts[n - k])
        for e in range(MXU_TILE // LANES):
            yscr_ref[pl.ds(2 * n + e, r, stride=t), :] = acc[:, e * LANES:(e + 1) * LANES]
    y_ref[0] = yscr_ref[...].astype(y_ref.dtype)


def _s5_out(u2, xs, wc, cc, out_dtype):
    no, r, kk = u2.shape
    m = r * (kk // LANES)
    blk = lambda a: pl.BlockSpec((1,) + a.shape[1:], lambda o: (o,) + (0,) * (a.ndim - 1))
    return pl.pallas_call(
        _s5_out_kernel,
        grid=(no,),
        in_specs=[blk(u2), blk(xs), blk(wc), blk(cc)],
        out_specs=pl.BlockSpec((1, m, LANES), lambda o: (o, 0, 0)),
        out_shape=jax.ShapeDtypeStruct((no, m, LANES), out_dtype),
        scratch_shapes=[pltpu.VMEM((m, LANES), F32)],
        compiler_params=_cparams(("parallel",)),
        name="s5_out",
    )(u2, xs, wc, cc)


def _s5_params(lam_re, lam_im, log_dt, b_re, b_im, c_re, c_im):
    t = S5_CHUNK
    ng = lam_re.shape[0]
    no = ng // OCT
    dt = jnp.exp(log_dt)[:, None]
    a = lam_re * dt
    b = lam_im * dt
    n = jnp.arange(t + 1, dtype=F32)[None, :, None]
    mag = jnp.exp(n * a[:, None, :])
    pw_re = mag * jnp.cos(n * b[:, None, :])
    pw_im = mag * jnp.sin(n * b[:, None, :])
    em1_re = jnp.expm1(a) * jnp.cos(b) - 2.0 * jnp.sin(0.5 * b) ** 2
    em1_im = jnp.exp(a) * jnp.sin(b)
    den = lam_re * lam_re + lam_im * lam_im
    z_re = (em1_re * lam_re + em1_im * lam_im) / den
    z_im = (em1_im * lam_re - em1_re * lam_im) / den
    bb_re = z_re[..., None] * b_re - z_im[..., None] * b_im
    bb_im = z_re[..., None] * b_im + z_im[..., None] * b_re
    e_re, e_im = pw_re[:, :t, :, None], pw_im[:, :t, :, None]
    w_re = e_re * bb_re[:, None] - e_im * bb_im[:, None]
    w_im = e_re * bb_im[:, None] + e_im * bb_re[:, None]
    kd = (jnp.einsum('gdpc,gkp->gdck', w_re, c_re, precision=HIGHEST)
          - jnp.einsum('gdpc,gkp->gdck', w_im, c_im, precision=HIGHEST))
    kd_pad = jnp.pad(kd, ((0, 0), (1, 0), (0, 0), (0, 0)))
    nl = t // 2
    lag1 = (2 * jnp.arange(nl)[:, None, None] + jnp.arange(2)[None, None, :] - jnp.arange(2)[None, :, None] + 1)
    x = kd_pad[:, lag1].reshape(no, OCT, nl, 2, 2, S5_GROUP, S5_GROUP)
    wc = x.transpose(0, 2, 3, 5, 4, 1, 6).reshape(no, nl, 2 * S5_GROUP, MXU_TILE)

    def in_op(w):
        w = w[:, ::-1].transpose(0, 1, 3, 2).reshape(no, OCT, t, S5_GROUP, S5_STATE).transpose(0, 2, 1, 3, 4)
        return jnp.concatenate([w, w], axis=-1).reshape(no, t * LANES, 2 * S5_STATE)

    bre, bim = in_op(w_re), in_op(w_im)
    o_re, o_im = pw_re[:, 1:, None, :], pw_im[:, 1:, None, :]
    cl_re = c_re[:, None] * o_re - c_im[:, None] * o_im
    cl_im = c_re[:, None] * o_im + c_im[:, None] * o_re

    def out_op(cl):
        cl = cl.reshape(no, OCT, t, S5_GROUP, S5_STATE).transpose(0, 4, 2, 1, 3)
        return cl.reshape(no, S5_STATE, t * LANES)

    cc = jnp.concatenate([out_op(cl_re), -out_op(cl_im)], axis=1)
    return wc, bre, bim, cc, pw_re, pw_im


def _s5_chunk(seqlen):
    return math.gcd(seqlen, S5_CHUNK)


def _s5_branch(u2, x0, ops, nseq, seqlen, out_dtype):
    wc, bre, bim, cc, pw_re, pw_im = ops
    no = u2.shape[0]
    t = _s5_chunk(seqlen)
    kk = t * LANES
    nchunks = seqlen // t
    are = pw_re[:, t].reshape(no, 1, OCT * S5_STATE)
    aim = pw_im[:, t].reshape(no, 1, OCT * S5_STATE)
    tail = S5_CHUNK * LANES - kk
    xs, xf = _s5_state(u2, bre[:, tail:], bim[:, tail:], x0, are, aim, nseq, nchunks)
    return _s5_out(u2, xs, wc[:, :kk // MXU_TILE], cc[:, :, :kk], out_dtype), xf


def _s5_state_in(re, im, no):
    nseq = re.shape[0]
    f = lambda a: a.reshape(nseq, no, OCT * S5_STATE).transpose(1, 0, 2)
    return jnp.concatenate([f(re), f(im)], axis=-1)


def _s5_state_out(xf):
    no, nseq, _ = xf.shape
    hw = OCT * S5_STATE
    f = lambda a: a.transpose(1, 0, 2).reshape(1, nseq, no * OCT, S5_STATE)
    return f(xf[:, :, :hw]), f(xf[:, :, hw:])


def _gla_kernel(q_ref, k_ref, v_ref, la_ref, gb_ref, s0_ref, ng_ref, o_ref, sf_ref, st_ref,
                *, nseq, nchunks, c):
    i = pl.program_id(1)

    @pl.when(i == 0)
    def _():
        st_ref[...] = s0_ref[...]

    nh, hk, hv = GLA_HEADS, GLA_HEAD_K, GLA_HEAD_V
    nch = nseq * nchunks
    tm = nch * c
    iota = lambda shape, d: lax.broadcasted_iota(jnp.int32, shape, d)
    ng = ng_ref[...]

    tb = min(tm, MXU_TILE)
    rt, ct = iota((tb, tb), 0), iota((tb, tb), 1)
    tril = ((rt // c == ct // c) & (rt >= ct)).astype(BF16)
    parts = []
    for r0 in range(0, tm, tb):
        la_hi, la_lo = _split_bf16(la_ref[r0:r0 + tb, :])
        parts.append(_dot(tril, la_hi) + _dot(tril, la_lo))
    bcum = jnp.concatenate(parts, axis=0)
    e_hi, e_lo = _split_bf16(jnp.exp(jnp.concatenate([bcum[(ci + 1) * c - 1:(ci + 1) * c] for ci in range(nch)],
                                                     axis=0)))
    pick = (iota((nch, LANES), 0) == iota((nch, LANES), 1)).astype(BF16)
    dec_t = _dot_tn(e_hi, pick) + _dot_tn(e_lo, pick)
    q_all = q_ref[...].astype(F32)
    k_all = k_ref[...].astype(F32)
    q_dec_all = q_all * jnp.exp(bcum)
    k_dec_all = k_all * jnp.exp(-bcum)

    own_k = iota((nh * c, nh * hk), 0) // c == iota((nh * c, nh * hk), 1) // hk
    own_v = iota((nh * c, nh * hv), 0) // c == iota((nh * c, nh * hv), 1) // hv
    causal = iota((c, nh * c), 1) % c <= iota((c, nh * c), 0)
    zero_v = jnp.zeros((hk, hv), F32)

    def intra(ci):
        rows = slice(ci * c, (ci + 1) * c)
        q_dec = q_dec_all[rows].astype(BF16)
        k_dec = k_dec_all[rows]
        b_c = bcum[rows]
        k_tail = (k_all[rows] * jnp.exp(b_c[c - 1:c] - b_c)).astype(BF16)
        v = v_ref[rows, :].astype(F32)
        k_bd = jnp.where(own_k, jnp.concatenate([k_dec] * nh, axis=0), 0.0).astype(BF16)
        v_bd = jnp.where(own_v, jnp.concatenate([v] * nh, axis=0), 0.0).astype(BF16)
        att = jnp.where(causal, _dot_nt(q_dec, k_bd), 0.0)
        o_intra = _dot(att.astype(BF16), v_bd)
        vb = v.astype(BF16)
        kvs = []
        for h0 in range(0, nh, 2):
            kv2 = _dot_tn(k_tail[:, h0 * hk:(h0 + 2) * hk], vb[:, h0 * hv:(h0 + 2) * hv])
            kvs += [kv2[:hk, :hv], kv2[hk:, hv:]]
        return q_dec, o_intra, kvs

    def carry(ci, sts, q_dec, o_intra, kvs):
        st_bd = jnp.concatenate(
            [jnp.concatenate([zero_v] * h + [sts[h]] + [zero_v] * (nh - 1 - h), axis=1) for h in range(nh)],
            axis=0).astype(BF16)
        o = o_intra + _dot(q_dec, st_bd)
        new = [dec_t[h * hk:(h + 1) * hk, ci:ci + 1] * sts[h] + kvs[h] for h in range(nh)]
        return o, new

    def finish(ci, o):
        rows = slice(ci * c, (ci + 1) * c)
        for h in range(nh):
            vs = slice(h * hv, (h + 1) * hv)
            oh = _rms_rows(o[:, vs], ng) * _silu(gb_ref[rows, vs].astype(F32))
            o_ref[rows, vs] = oh.astype(o_ref.dtype)

    sts = None
    pending = intra(0)
    unfinished = None
    for ci in range(nch):
        s, first, last = ci // nchunks, ci % nchunks == 0, ci % nchunks == nchunks - 1
        current = pending
        if ci + 1 < nch:
            pending = intra(ci + 1)
        if first:
            sts = [st_ref[s, h] for h in range(nh)]
        o, sts = carry(ci, sts, *current)
        if last:
            for h in range(nh):
                st_ref[s, h] = sts[h]
        if unfinished is not None:
            finish(*unfinished)
        unfinished = (ci, o)
    finish(*unfinished)

    @pl.when(i == pl.num_programs(1) - 1)
    def _():
        sf_ref[...] = st_ref[...]


def _gla(q, k, v, la, gb, s0, ng, nseq_total, seqlen, out_dtype):
    c = math.gcd(seqlen, GLA_CHUNK)
    if seqlen >= 8 * c:
        nseq, nchunks = 1, 8
    else:
        nseq, nchunks = 8, seqlen // c
    assert nseq_total % nseq == 0 and seqlen % (nchunks * c) == 0
    nblk = seqlen // (nchunks * c)
    tm = nseq * nchunks * c
    row = lambda n: pl.BlockSpec((tm, n), lambda b, i: (b * nblk + i, 0))
    st_spec = pl.BlockSpec((nseq, GLA_HEADS, GLA_HEAD_K, GLA_HEAD_V), lambda b, i: (b, 0, 0, 0))
    m = q.shape[0]
    return pl.pallas_call(
        functools.partial(_gla_kernel, nseq=nseq, nchunks=nchunks, c=c),
        grid=(nseq_total // nseq, nblk),
        in_specs=[row(256), row(256), row(512), row(256), row(512), st_spec, _full(ng.shape)],
        out_specs=[row(512), st_spec],
        out_shape=[jax.ShapeDtypeStruct((m, 512), out_dtype), jax.ShapeDtypeStruct(s0.shape, F32)],
        scratch_shapes=[pltpu.VMEM((nseq, GLA_HEADS, GLA_HEAD_K, GLA_HEAD_V), F32)],
        compiler_params=_cparams(("parallel", "arbitrary")),
        name="gla",
    )(q, k, v, la, gb, s0, ng)


def _gelu_tanh(x):
    return 0.5 * x * (1.0 + jnp.tanh(math.sqrt(2.0 / math.pi) * (x + 0.044715 * (x * x * x))))


def _even_out_tile(parts, x_ref, y_ref, u_ref, ga_ref, ob_ref, d_ref, wglu_ref, bglu_ref, woa_ref, wob_ref):
    no = y_ref.shape[0]

    def pre(rows):
        y = jnp.concatenate([y_ref[o, rows, :].astype(F32) for o in range(no)], axis=1)
        u = jnp.concatenate([u_ref[o, rows, :].astype(F32) for o in range(no)], axis=1)
        return _gelu_tanh(y + d_ref[...] * u)

    def glu(z):
        return _dot(z.astype(BF16), wglu_ref[...])

    def gated(rows, z, g):
        return (z * _sigmoid(g + bglu_ref[...]) * _silu(ga_ref[rows, :].astype(F32))).astype(BF16)

    def mix(rows, out_a):
        return _dot(out_a, woa_ref[...]) + _dot(ob_ref[rows, :].astype(BF16), wob_ref[...])

    zs, gs, ms = [], [], []
    for rows in parts:
        zs.append(pre(rows))
        gs.append(glu(zs[-1]))
    for rows, z, g in zip(parts, zs, gs):
        ms.append(mix(rows, gated(rows, z, g)))
    return [x_ref[rows, :] + m for rows, m in zip(parts, ms)]


def _head_ones(n):
    r = lax.broadcasted_iota(jnp.int32, (n, n), 0) // SWA_HEAD_DIM
    c = lax.broadcasted_iota(jnp.int32, (n, n), 1) // SWA_HEAD_DIM
    return (r == c).astype(BF16)


def _rope_block(x, cos_t, sin_t, upper):
    swapped = jnp.where(upper, pltpu.roll(x, 32, 1), pltpu.roll(x, 96, 1))
    return x * cos_t + swapped * sin_t


def _mid_kernel(x_ref, y_ref, u_ref, ga_ref, ob_ref, d_ref, wglu_ref, bglu_ref, woa_ref, wob_ref,
                g_ref, w_ref, gq_ref, gk_ref, cos_ref, sin_ref, h_ref, q_ref, k_ref, v_ref, gate_ref):
    tm = x_ref.shape[0]
    lane = lax.broadcasted_iota(jnp.int32, (tm, LANES), 1)
    upper = (lane & 32) != 0
    ones4 = _head_ones(MXU_TILE)
    inv_d = 1.0 / SWA_HEAD_DIM
    qw = SWA_HEADS * SWA_HEAD_DIM
    kw = SWA_KV_HEADS * SWA_HEAD_DIM
    nblk = qw // MXU_TILE
    h, = _even_out_tile([slice(0, tm)], x_ref, y_ref, u_ref, ga_ref, ob_ref, d_ref, wglu_ref, bglu_ref, woa_ref,
                        wob_ref)
    h_ref[...] = h
    xb = _rms_rows(h, g_ref[...]).astype(BF16)
    cos_t = cos_ref[...]
    sin_t = sin_ref[...]

    def q_block(j):
        return _dot(xb, w_ref[:, j * MXU_TILE:(j + 1) * MXU_TILE])

    def q_finish(j, q):
        ss = _dot((q * q).astype(BF16), ones4)
        qn = q * lax.rsqrt(ss * inv_d + NORM_EPS) * gq_ref[...]
        for e in range(MXU_TILE // LANES):
            cols = slice(j * MXU_TILE + e * LANES, j * MXU_TILE + (e + 1) * LANES)
            qe = _rope_block(qn[:, e * LANES:(e + 1) * LANES], cos_t, sin_t, upper)
            q_ref[:, cols] = (qe * (SWA_HEAD_DIM ** -0.5 * LOG2E)).astype(q_ref.dtype)

    q_next = q_block(0)
    for j in range(nblk):
        q_cur = q_next
        q_next = q_block(j + 1) if j + 1 < nblk else _dot(xb, w_ref[:, qw:qw + 2 * kw])
        q_finish(j, q_cur)
    kv = q_next
    gate = _dot(xb, w_ref[:, qw + 2 * kw:])
    k = kv[:, :kw]
    ss = _dot((k * k).astype(BF16), ones4[:kw, :kw])
    kn = k * lax.rsqrt(ss * inv_d + NORM_EPS) * gk_ref[...]
    k_ref[...] = _rope_block(kn, cos_t, sin_t, upper)
    v_ref[...] = kv[:, kw:]
    gate_ref[...] = gate.astype(gate_ref.dtype)


def _mid(x, y, u, ga, ob, pe, po, cos_t, sin_t, act_dtype):
    m, dm = x.shape
    tm = _row_tile(m)
    row = lambda n: pl.BlockSpec((tm, n), lambda i: (i, 0))
    slab = pl.BlockSpec((y.shape[0], tm, LANES), lambda i: (0, i, 0))
    assert cos_t.shape[0] % tm == 0
    nper = cos_t.shape[0] // tm
    tab = pl.BlockSpec((tm, LANES), lambda i: (i % nper, 0))
    qw = SWA_HEADS * SWA_HEAD_DIM
    kw = SWA_KV_HEADS * SWA_HEAD_DIM
    weights = [pe['d'], pe['wglu'], pe['bglu'], pe['woa'], pe['wob'], po['norm_g'], po['w_in'], po['gq'], po['gk']]
    return pl.pallas_call(
        _mid_kernel,
        grid=(m // tm,),
        in_specs=[row(dm), slab, slab, row(512), row(512)] + [_full(w.shape) for w in weights] + [tab, tab],
        out_specs=[row(dm), row(qw), row(kw), row(kw), row(qw)],
        out_shape=[jax.ShapeDtypeStruct((m, dm), F32),
                   jax.ShapeDtypeStruct((m, qw), act_dtype), jax.ShapeDtypeStruct((m, kw), F32),
                   jax.ShapeDtypeStruct((m, kw), F32), jax.ShapeDtypeStruct((m, qw), act_dtype)],
        compiler_params=_cparams(("parallel",)),
        name="mid",
    )(x, y, u, ga, ob, *weights, cos_t, sin_t)


def _rope_tables(pos):
    half = SWA_HEAD_DIM // 2
    inv_freq = ROPE_THETA ** (-np.arange(half, dtype=np.float64) / half)
    ang = np.asarray(pos, np.float64)[:, None] * inv_freq[None, :]
    cos, sin = np.cos(ang), np.sin(ang)
    cos_t = np.concatenate([cos, cos, cos, cos], axis=1).astype(np.float32)
    sin_t = np.concatenate([-sin, sin, -sin, sin], axis=1).astype(np.float32)
    return jnp.asarray(cos_t), jnp.asarray(sin_t)


def _attn_prompt_kernel(sink_ref, q_ref, kc_ref, kp_ref, vc_ref, vp_ref, gate_ref, h_ref, wout_ref,
                        y_ref, p_ref, o_ref, *, nqb):
    i = pl.program_id(1)
    w = SWA_WINDOW
    hd = SWA_HEAD_DIM
    npair = SWA_GROUP // 2
    lane2 = lax.broadcasted_iota(jnp.int32, (2 * w, LANES), 1)
    low = lane2 < hd
    rr = lax.broadcasted_iota(jnp.int32, (w, w), 0)
    cc = lax.broadcasted_iota(jnp.int32, (w, w), 1)
    tri = cc <= rr
    low_w = cc < hd
    r4 = lax.broadcasted_iota(jnp.int32, (4 * w, LANES), 0)
    c4 = lax.broadcasted_iota(jnp.int32, (4 * w, LANES), 1)
    den_cols = ((r4 < 2 * w) == (c4 < hd)).astype(BF16)
    units = [(jb, kv) for jb in range(nqb) for kv in range(SWA_KV_HEADS)]

    def scores(jb, kv):
        rows = slice(jb * w, (jb + 1) * w)
        if jb == 0:
            k_prev, v_prev = kp_ref[...], vp_ref[...]
        else:
            prev_rows = slice((jb - 1) * w, jb * w)
            k_prev, v_prev = kc_ref[prev_rows, :], vc_ref[prev_rows, :]
        kcat = jnp.concatenate([k_prev, kc_ref[rows, :]], axis=0)
        vcat = jnp.concatenate([v_prev, vc_ref[rows, :]], axis=0)
        own = low if kv == 0 else jnp.logical_not(low)
        k_own = jnp.where(own, kcat, 0.0)
        v_own = jnp.where(own, vcat, 0.0)
        k_oth = pltpu.roll(k_own, hd, 1)
        v_oth = pltpu.roll(v_own, hd, 1)
        k_lo, k_hi = (k_own, k_oth) if kv == 0 else (k_oth, k_own)
        v_lo, v_hi = (v_own, v_oth) if kv == 0 else (v_oth, v_own)
        k_rhs = jnp.concatenate([k_lo, k_hi], axis=0).astype(BF16)
        v_rhs = jnp.concatenate([jnp.concatenate([v_lo, v_hi], axis=0).astype(BF16), den_cols], axis=1)
        qs = jnp.concatenate([q_ref[rows, (kv * npair + pr) * LANES:(kv * npair + pr + 1) * LANES]
                              for pr in range(npair)], axis=0)
        return _dot_nt(qs, k_rhs), v_rhs

    def softmax_pv(kv, first, buf, s_all, v_rhs):
        sink_terms = []
        for pr in range(npair):
            prow = slice(pr * w, (pr + 1) * w)
            pair_terms = []
            for e in range(2):
                s_prev = s_all[prow, (2 * e) * w:(2 * e + 1) * w]
                s_cur = s_all[prow, (2 * e + 1) * w:(2 * e + 2) * w]
                if first:
                    s_prev = jnp.where(i > 0, s_prev, -jnp.inf)
                sc = jnp.where(tri, s_cur, s_prev)
                sink = sink_ref[2 * (kv * npair + pr) + e] * LOG2E
                mx = jnp.maximum(jnp.max(sc, axis=-1, keepdims=True), sink)
                pe = jnp.exp2(sc - mx)
                p_ref[buf, prow, (2 * e) * w:(2 * e + 1) * w] = jnp.where(tri, 0.0, pe).astype(BF16)
                p_ref[buf, prow, (2 * e + 1) * w:(2 * e + 2) * w] = jnp.where(tri, pe, 0.0).astype(BF16)
                pair_terms.append(jnp.exp2(sink - mx))
            sink_terms.append(pair_terms)
        return _dot(p_ref[buf], v_rhs), sink_terms

    def normalise(jb, kv, o_ext, sink_terms):
        rows = slice(jb * w, (jb + 1) * w)
        for pr in range(npair):
            prow = slice(pr * w, (pr + 1) * w)
            cols = slice((kv * npair + pr) * LANES, (kv * npair + pr + 1) * LANES)
            st = jnp.where(low_w, sink_terms[pr][0], sink_terms[pr][1])
            o = o_ext[prow, :LANES] / (o_ext[prow, LANES:] + st)
            o_ref[rows, cols] = o.astype(o_ref.dtype)

    pending = scores(*units[0])
    unfinished = None
    for n, (jb, kv) in enumerate(units):
        current = pending
        if n + 1 < len(units):
            pending = scores(*units[n + 1])
        result = softmax_pv(kv, jb == 0, n % 2, *current)
        if unfinished is not None:
            normalise(*unfinished)
        unfinished = (jb, kv) + result
    normalise(*unfinished)
    og = o_ref[...].astype(F32) * _silu(gate_ref[...].astype(F32))
    y_ref[...] = h_ref[...] + _dot(og.astype(BF16), wout_ref[...])


def _attn_prompt(sinks, q, k, v, gate, h, w_out, nseq, seqlen):
    w = SWA_WINDOW
    nqb = 4
    tm = nqb * w
    assert seqlen % tm == 0
    nblk = seqlen // tm
    qw = q.shape[1]
    dm = h.shape[1]
    row = lambda n: pl.BlockSpec((tm, n), lambda b, i: (b * nblk + i, 0))
    prev = lambda n: pl.BlockSpec((w, n), lambda b, i: (jnp.maximum((b * nblk + i) * nqb - 1, 0), 0))
    smem = pl.BlockSpec(memory_space=pltpu.SMEM)
    return pl.pallas_call(
        functools.partial(_attn_prompt_kernel, nqb=nqb),
        grid=(nseq, nblk),
        in_specs=[smem, row(qw), row(LANES), prev(LANES), row(LANES), prev(LANES), row(qw), row(dm),
                  _full(w_out.shape)],
        out_specs=row(dm),
        out_shape=jax.ShapeDtypeStruct(h.shape, F32),
        scratch_shapes=[pltpu.VMEM((2, 4 * w, 4 * w), BF16), pltpu.VMEM((tm, qw), BF16)],
        compiler_params=_cparams(("parallel", "arbitrary")),
        name="attn_prompt",
    )(sinks, q, k, k, v, v, gate, h, w_out)


def _attn_sample_kernel(sink_ref, q_ref, kn_ref, vn_ref, kc_ref, vc_ref, o_ref, ko_ref, vo_ref, *, seqlen):
    nq = SWA_HEADS * seqlen
    ncache = kc_ref.shape[1]
    t_row = lax.broadcasted_iota(jnp.int32, (nq, ncache), 0) % seqlen
    c_col = lax.broadcasted_iota(jnp.int32, (nq, ncache), 1)
    cache_ok = c_col > t_row - (SWA_WINDOW - ncache)
    nnew = kn_ref.shape[1]
    t_row_n = lax.broadcasted_iota(jnp.int32, (nq, nnew), 0) % seqlen
    n_col = lax.broadcasted_iota(jnp.int32, (nq, nnew), 1)
    new_ok = n_col <= t_row_n
    sink = (sink_ref[...] * LOG2E)[None, :, 0:1]
    bqk = lambda a, b: lax.dot_general(a, b, (((2,), (2,)), ((0,), (0,))), preferred_element_type=F32)
    bpv = lambda a, b: lax.dot_general(a, b, (((2,), (1,)), ((0,), (0,))), preferred_element_type=F32)
    q = q_ref[...]
    kc = kc_ref[...]
    vc = vc_ref[...]
    kn = kn_ref[...]
    vn = vn_ref[...]
    sc = jnp.where(cache_ok[None], bqk(q, kc.astype(BF16)), -jnp.inf)
    sn = jnp.where(new_ok[None], bqk(q, kn.astype(BF16)), -jnp.inf)
    mx = jnp.maximum(jnp.maximum(jnp.max(sc, axis=-1, keepdims=True), jnp.max(sn, axis=-1, keepdims=True)), sink)
    pc = jnp.exp2(sc - mx)
    pn = jnp.exp2(sn - mx)
    den = jnp.sum(pc, axis=-1, keepdims=True) + jnp.sum(pn, axis=-1, keepdims=True) + jnp.exp2(sink - mx)
    inv = 1.0 / den
    o_ref[...] = bpv((pc * inv).astype(BF16), vc.astype(BF16)) + bpv((pn * inv).astype(BF16), vn.astype(BF16))
    keep = ncache - seqlen
    ko_ref[:, 0:keep, :] = kc[:, seqlen:ncache, :]
    ko_ref[:, keep:ncache, :] = kn[:, 0:seqlen, :]
    vo_ref[:, 0:keep, :] = vc[:, seqlen:ncache, :]
    vo_ref[:, keep:ncache, :] = vn[:, 0:seqlen, :]


def _attn_sample(sink_rows, q, kn, vn, kc, vc, seqlen):
    n = q.shape[0]
    nseq = 8
    assert n % nseq == 0
    blk = lambda a: pl.BlockSpec((nseq,) + a.shape[1:], lambda i: (i, 0, 0))
    return pl.pallas_call(
        functools.partial(_attn_sample_kernel, seqlen=seqlen),
        grid=(n // nseq,),
        in_specs=[_full(sink_rows.shape), blk(q), blk(kn), blk(vn), blk(kc), blk(vc)],
        out_specs=[blk(q), blk(kc), blk(vc)],
        out_shape=[jax.ShapeDtypeStruct(q.shape, F32), jax.ShapeDtypeStruct(kc.shape, F32),
                   jax.ShapeDtypeStruct(vc.shape, F32)],
        compiler_params=_cparams(("parallel",)),
        name="attn_sample",
    )(sink_rows, q, kn, vn, kc, vc)


def _odd_out_kernel(h_ref, o_ref, gate_ref, w_ref, y_ref):
    og = o_ref[...].astype(F32) * _silu(gate_ref[...].astype(F32))
    y_ref[...] = h_ref[...] + _dot(og.astype(BF16), w_ref[...])


def _odd_out(h, o, gate, w):
    m, dm = h.shape
    tm = _row_tile(m)
    row = lambda n: pl.BlockSpec((tm, n), lambda i: (i, 0))
    return pl.pallas_call(
        _odd_out_kernel,
        grid=(m // tm,),
        in_specs=[row(dm), row(o.shape[1]), row(gate.shape[1]), _full(w.shape)],
        out_specs=row(dm),
        out_shape=jax.ShapeDtypeStruct((m, dm), F32),
        compiler_params=_cparams(("parallel",)),
        name="odd_out",
    )(h, o, gate, w)


def _trunk(x, s5_x0, gla_s0, pe, po, pos, nseq, seqlen, act_dtype):
    u4, u2, ga, q, k, v, la, gb = _even_in(x, pe['norm_g'], pe['wm'], pe['wa'], pe['wgb'], pe['wgate'],
                                           pe['bgate'], act_dtype, _s5_chunk(seqlen))
    y4, s5_fin = _s5_branch(u2, s5_x0, pe['s5_ops'], nseq, seqlen, act_dtype)
    ob, gla_fin = _gla(q, k, v, la, gb, gla_s0, pe['gla_norm_g'], nseq, seqlen, act_dtype)
    cos_t, sin_t = _rope_tables(pos)
    h, q1, k1, v1, gate = _mid(x, y4, u4, ga, ob, pe, po, cos_t, sin_t, act_dtype)
    return h, q1, k1, v1, gate, s5_fin, gla_fin


def kernel(x_prompt, x_sample, state_s5_re, state_s5_im, state_gla, cache_swa_k, cache_swa_v,
           even_norm_g, even_w_in, s5_lambda_re, s5_lambda_im, s5_log_dt, s5_b_re, s5_b_im,
           s5_c_re, s5_c_im, s5_d, s5_w_glu, s5_b_glu, gla_w_gate, gla_b_gate, gla_norm_g,
           even_w_out, odd_norm_g, odd_w_in, swa_q_norm_g, swa_k_norm_g, swa_sinks, odd_w_out):
    nb, seq, dm = x_prompt.shape
    ns, dseq, _ = x_sample.shape
    ng = s5_lambda_re.shape[1]
    no = ng // OCT
    xp = x_prompt.reshape(nb * seq, dm)
    xs = x_sample.reshape(ns * dseq, dm)

    i = 0
    w_in = even_w_in[i]
    s5w = ng * S5_GROUP
    col_alow = 2 * s5w + 2 * GLA_HEADS * GLA_HEAD_K + GLA_HEADS * GLA_HEAD_V
    col_gb = col_alow + GLA_GATE_RANK
    pad_rank = LANES - GLA_GATE_RANK
    pe = {
        'norm_g': even_norm_g[i][None, :],
        'wm': w_in[:, :col_alow].astype(BF16),
        'wa': jnp.pad(w_in[:, col_alow:col_gb], ((0, 0), (0, pad_rank))).astype(BF16),
        'wgb': w_in[:, col_gb:].astype(BF16),
        'wgate': jnp.pad(gla_w_gate[i], ((0, pad_rank), (0, 0))).astype(BF16),
        'bgate': gla_b_gate[i][None, :],
        's5_ops': _s5_params(s5_lambda_re[i], s5_lambda_im[i], s5_log_dt[i], s5_b_re[i], s5_b_im[i],
                             s5_c_re[i], s5_c_im[i]),
        'gla_norm_g': gla_norm_g[i][None, :],
        'd': s5_d[i][None, :],
        'wglu': s5_w_glu[i].astype(BF16),
        'bglu': s5_b_glu[i][None, :],
        'woa': even_w_out[i][:s5w].astype(BF16),
        'wob': even_w_out[i][s5w:].astype(BF16),
    }
    po = {
        'norm_g': odd_norm_g[i][None, :],
        'w_in': odd_w_in[i].astype(BF16),
        'gq': jnp.tile(swa_q_norm_g[i], MXU_TILE // SWA_HEAD_DIM)[None, :],
        'gk': jnp.tile(swa_k_norm_g[i], LANES // SWA_HEAD_DIM)[None, :],
    }
    w_out = odd_w_out[i].astype(BF16)
    sinks = swa_sinks[i]
    kvw = SWA_KV_HEADS * SWA_HEAD_DIM

    s5_zero = jnp.zeros((no, nb, 2 * OCT * S5_STATE), F32)
    gla_zero = jnp.zeros((nb, GLA_HEADS, GLA_HEAD_K, GLA_HEAD_V), F32)
    hp, q, k, v, gate, s5_p, gla_p = _trunk(xp, s5_zero, gla_zero, pe, po, np.arange(seq), nb, seq, BF16)
    s5r_p, s5i_p = _s5_state_out(s5_p)
    y_prompt = _attn_prompt(sinks, q, k, v, gate, hp, w_out, nb, seq).reshape(nb, seq, dm)
    cache_len = min(SWA_WINDOW, seq)
    swk_p = k.reshape(nb, seq, SWA_KV_HEADS, SWA_HEAD_DIM)[None, :, seq - cache_len:]
    swv_p = v.reshape(nb, seq, SWA_KV_HEADS, SWA_HEAD_DIM)[None, :, seq - cache_len:]

    pos_s = np.tile(PAST_LEN + np.arange(dseq), ns)
    s5_init = _s5_state_in(state_s5_re[i], state_s5_im[i], no)
    hs, q, k, v, gate, s5_s, gla_s = _trunk(xs, s5_init, state_gla[i], pe, po, pos_s, ns, dseq, F32)
    s5r_s, s5i_s = _s5_state_out(s5_s)
    ncache = cache_swa_k.shape[2]
    qh = q.reshape(ns, dseq, SWA_KV_HEADS, SWA_GROUP, SWA_HEAD_DIM).transpose(0, 2, 3, 1, 4)
    zq = jnp.zeros_like(qh[:, 0])
    q2 = jnp.stack([jnp.concatenate([qh[:, 0], zq], axis=-1), jnp.concatenate([zq, qh[:, 1]], axis=-1)], axis=1)
    q2 = q2.reshape(ns, SWA_HEADS * dseq, kvw).astype(BF16)
    sink_rows = jnp.broadcast_to(jnp.repeat(sinks, dseq)[:, None], (SWA_HEADS * dseq, LANES))
    kn = k.reshape(ns, dseq, kvw)
    vn = v.reshape(ns, dseq, kvw)
    npad = 16 - dseq
    kn_pad = jnp.pad(kn, ((0, 0), (0, npad), (0, 0)))
    vn_pad = jnp.pad(vn, ((0, 0), (0, npad), (0, 0)))
    kc = cache_swa_k[i].reshape(ns, ncache, kvw)
    vc = cache_swa_v[i].reshape(ns, ncache, kvw)
    o2, kc_new, vc_new = _attn_sample(sink_rows, q2, kn_pad, vn_pad, kc, vc, dseq)
    o2 = o2.reshape(ns, SWA_KV_HEADS, SWA_GROUP, dseq, SWA_KV_HEADS, SWA_HEAD_DIM)
    o = jnp.stack([o2[:, 0, :, :, 0], o2[:, 1, :, :, 1]], axis=1)
    o = o.transpose(0, 3, 1, 2, 4).reshape(ns * dseq, SWA_HEADS * SWA_HEAD_DIM)
    y_sample = _odd_out(hs, o, gate, w_out).reshape(ns, dseq, dm)
    swk_s = kc_new.reshape(1, ns, ncache, SWA_KV_HEADS, SWA_HEAD_DIM)
    swv_s = vc_new.reshape(1, ns, ncache, SWA_KV_HEADS, SWA_HEAD_DIM)

    return (y_prompt, y_sample,
            s5r_p, s5i_p, gla_p[None], swk_p, swv_p,
            s5r_s, s5i_s, gla_s[None], swk_s, swv_s)
```

```python
import functools
import math

import jax
import jax.numpy as jnp
import numpy as np
from jax import lax
from jax.experimental import pallas as pl
from jax.experimental.pallas import tpu as pltpu

F32 = jnp.float32
BF16 = jnp.bfloat16

PAST_LEN = 8192
NORM_EPS = 1e-6
S5_GROUP = 16
S5_STATE = 64
S5_CHUNK = 16
GLA_HEADS = 4
GLA_HEAD_K = 64
GLA_HEAD_V = 128
GLA_GATE_RANK = 16
GLA_GATE_TAU = 16.0
GLA_CHUNK = 64
SWA_HEADS = 16
SWA_KV_HEADS = 2
SWA_GROUP = SWA_HEADS // SWA_KV_HEADS
SWA_HEAD_DIM = 64
SWA_WINDOW = 128
ROPE_THETA = 10000.0
LOG2E = math.log2(math.e)
LANES = 128
MXU_TILE = 256
OCT = LANES // S5_GROUP
ROW_TILE = 512
VMEM_LIMIT = 48 * 1024 * 1024


def _cparams(sem):
    return pltpu.CompilerParams(dimension_semantics=sem, vmem_limit_bytes=VMEM_LIMIT)


def _full(shape):
    n = len(shape)
    return pl.BlockSpec(shape, lambda *_: (0,) * n)


def _dot(a, b):
    return jnp.dot(a, b, preferred_element_type=F32)


def _dot_nt(a, b):
    return lax.dot_general(a, b, (((1,), (1,)), ((), ())), preferred_element_type=F32)


def _dot_tn(a, b):
    return lax.dot_general(a, b, (((0,), (0,)), ((), ())), preferred_element_type=F32)


def _split_bf16(x):
    hi = x.astype(BF16)
    lo = (x - hi.astype(F32)).astype(BF16)
    return hi, lo


def _rms_rows(x, g):
    return x * lax.rsqrt(jnp.mean(x * x, axis=-1, keepdims=True) + NORM_EPS) * g


def _sigmoid(x):
    return 1.0 / (1.0 + jnp.exp(-x))


def _silu(x):
    return x * _sigmoid(x)


def _row_tile(m, tile=ROW_TILE):
    return tile if m % tile == 0 else m


def _even_in_kernel(x_ref, g_ref, wm_ref, wa_ref, wgb_ref, wgate_ref, bgate_ref,
                    u_ref, u2_ref, ga_ref, q_ref, k_ref, v_ref, la_ref, gb_ref, uscr_ref, *, t):
    xb = _rms_rows(x_ref[...], g_ref[...]).astype(BF16)

    def proj(lo, hi):
        return _dot(xb, wm_ref[:, lo:hi])

    u = proj(0, 512)
    nrow = u.shape[0] // t
    for o in range(u_ref.shape[0]):
        uo = u[:, o * LANES:(o + 1) * LANES]
        u_ref[o] = uo.astype(u_ref.dtype)
        uscr_ref[o] = uo
        for tt in range(t):
            piece = uscr_ref[o, pl.ds(tt, nrow, stride=t), :]
            u2_ref[o, :, tt * LANES:(tt + 1) * LANES] = piece.astype(u2_ref.dtype)
    ga_ref[...] = proj(512, 1024).astype(ga_ref.dtype)
    q_ref[...] = (proj(1024, 1280) * (GLA_HEAD_K ** -0.5)).astype(q_ref.dtype)
    k_ref[...] = proj(1280, 1536).astype(k_ref.dtype)
    v_ref[...] = proj(1536, 2048).astype(v_ref.dtype)
    gb_ref[...] = _dot(xb, wgb_ref[...]).astype(gb_ref.dtype)
    a_low = _dot(xb, wa_ref[...])
    logit = _dot(a_low.astype(BF16), wgate_ref[...]) + bgate_ref[...]
    log_sig = jnp.minimum(logit, 0.0) - jnp.log1p(jnp.exp(-jnp.abs(logit)))
    la_ref[...] = log_sig * (1.0 / GLA_GATE_TAU)


def _even_in(x, g, wm, wa, wgb, wgate, bgate, act_dtype, t):
    m, d = x.shape
    tm = _row_tile(m, 2 * ROW_TILE) if m > ROW_TILE else m
    row = lambda n: pl.BlockSpec((tm, n), lambda i: (i, 0))
    no = 512 // LANES
    slab = pl.BlockSpec((no, tm, LANES), lambda i: (0, i, 0))
    chunk = pl.BlockSpec((no, tm // t, t * LANES), lambda i: (0, i, 0))
    outs = [(512, act_dtype), (256, act_dtype), (256, act_dtype), (512, act_dtype), (256, F32), (512, act_dtype)]
    return pl.pallas_call(
        functools.partial(_even_in_kernel, t=t),
        grid=(m // tm,),
        in_specs=[row(d), _full(g.shape), _full(wm.shape), _full(wa.shape), _full(wgb.shape),
                  _full(wgate.shape), _full(bgate.shape)],
        out_specs=[slab, chunk] + [row(n) for n, _ in outs],
        out_shape=[jax.ShapeDtypeStruct((no, m, LANES), act_dtype),
                   jax.ShapeDtypeStruct((no, m // t, t * LANES), act_dtype)]
        + [jax.ShapeDtypeStruct((m, n), dt) for n, dt in outs],
        scratch_shapes=[pltpu.VMEM((no, tm, LANES), F32)],
        compiler_params=_cparams(("parallel",)),
        name="even_in",
    )(x, g, wm, wa, wgb, wgate, bgate)


def _group_mask(shape, row_span, col_span):
    rg = (lax.broadcasted_iota(jnp.int32, shape, 0) // row_span) % OCT
    cg = (lax.broadcasted_iota(jnp.int32, shape, 1) // col_span) % OCT
    return rg == cg


def _s5_state_kernel(u_ref, bre_ref, bim_ref, x0_ref, are_ref, aim_ref, xs_ref, xf_ref, loc_ref, *, nseq, nchunks):
    hw = OCT * S5_STATE
    kk = u_ref.shape[2]
    rep = hw // LANES
    bp = jnp.concatenate([bre_ref[0]] * rep + [bim_ref[0]] * rep, axis=1)
    bp = jnp.where(_group_mask((kk, 2 * hw), S5_GROUP, S5_STATE), bp, 0.0).astype(BF16)
    loc_ref[...] = _dot(u_ref[0], bp)
    a_re = are_ref[0]
    a_im = aim_ref[0]
    if nchunks == 1:
        x0 = x0_ref[0]
        xr, xi = x0[:, :hw], x0[:, hw:]
        loc = loc_ref[...]
        xf_ref[0, :, :hw] = a_re * xr - a_im * xi + loc[:, :hw]
        xf_ref[0, :, hw:] = a_re * xi + a_im * xr + loc[:, hw:]
        xs_ref[0] = x0.astype(xs_ref.dtype)
    else:
        def body(j, carry):
            new = []
            for b in range(nseq):
                xr, xi = carry[b]
                row = pl.ds(b * nchunks + j, 1)
                lr = loc_ref[row, :hw]
                li = loc_ref[row, hw:]
                loc_ref[row, :hw] = xr
                loc_ref[row, hw:] = xi
                new.append((a_re * xr - a_im * xi + lr, a_re * xi + a_im * xr + li))
            return tuple(new)

        init = tuple((x0_ref[0, b:b + 1, :hw], x0_ref[0, b:b + 1, hw:]) for b in range(nseq))
        fin = lax.fori_loop(0, nchunks, body, init, unroll=4)
        for b in range(nseq):
            xf_ref[0, b:b + 1, :hw] = fin[b][0]
            xf_ref[0, b:b + 1, hw:] = fin[b][1]
        xs_ref[0] = loc_ref[...].astype(xs_ref.dtype)


def _s5_state(u2, bre, bim, x0, are, aim, nseq, nchunks):
    no, r, kk = u2.shape
    sw = x0.shape[2]
    blk = lambda a: pl.BlockSpec((1,) + a.shape[1:], lambda o: (o,) + (0,) * (a.ndim - 1))
    return pl.pallas_call(
        functools.partial(_s5_state_kernel, nseq=nseq, nchunks=nchunks),
        grid=(no,),
        in_specs=[blk(u2), blk(bre), blk(bim), blk(x0), blk(are), blk(aim)],
        out_specs=[pl.BlockSpec((1, r, sw), lambda o: (o, 0, 0)), blk(x0)],
        out_shape=[jax.ShapeDtypeStruct((no, r, sw), BF16), jax.ShapeDtypeStruct(x0.shape, F32)],
        scratch_shapes=[pltpu.VMEM((r, sw), F32)],
        compiler_params=_cparams(("parallel",)),
        name="s5_state",
    )(u2, bre, bim, x0, are, aim)


def _dot_nt_f32(a, b):
    a_hi, a_lo = _split_bf16(a)
    b_hi, b_lo = _split_bf16(b)
    return _dot_nt(a_hi, b_hi) + _dot_nt(a_hi, b_lo) + _dot_nt(a_lo, b_hi)


def _s5_out_kernel(u_ref, xs_ref, bre_ref, bim_ref, cre_ref, cim_ref, zre_ref, zim_ref, y_ref, yscr_ref):
    r, kk = u_ref.shape[1], u_ref.shape[2]
    t = kk // LANES
    sw = xs_ref.shape[2]
    ntile = kk // MXU_TILE
    rep = sw // (2 * LANES)
    taps = _dot_nt_f32(bre_ref[0], zre_ref[0]) - _dot_nt_f32(bim_ref[0], zim_ref[0])
    tmask = _group_mask((LANES, LANES), S5_GROUP, S5_GROUP)

    def tap(lag):
        if lag < 0:
            return jnp.zeros((LANES, LANES), F32)
        s = t - 1 - lag
        return jnp.where(tmask, taps[s * LANES:(s + 1) * LANES], 0.0)

    wts = [jnp.concatenate([jnp.concatenate([tap(2 * d), tap(2 * d + 1)], axis=1),
                            jnp.concatenate([tap(2 * d - 1), tap(2 * d)], axis=1)], axis=0).astype(BF16)
           for d in range(ntile)]
    cpt = jnp.concatenate([cre_ref[0]] * rep + [cim_ref[0]] * rep, axis=1)
    cpt = jnp.where(_group_mask((kk, sw), S5_GROUP, S5_STATE), cpt, 0.0).astype(BF16)
    xs = xs_ref[0]
    for n in range(ntile):
        acc = _dot_nt(xs, cpt[n * MXU_TILE:(n + 1) * MXU_TILE])
        for k in range(n + 1):
            acc = acc + _dot(u_ref[0, :, k * MXU_TILE:(k + 1) * MXU_TILE], wts[n - k])
        for e in range(MXU_TILE // LANES):
            yscr_ref[pl.ds(2 * n + e, r, stride=t), :] = acc[:, e * LANES:(e + 1) * LANES]
    y_ref[0] = yscr_ref[...].astype(y_ref.dtype)


def _s5_out(u2, xs, bre, bim, cre, cim, zre, zim, out_dtype):
    no, r, kk = u2.shape
    m = r * (kk // LANES)
    blk = lambda a: pl.BlockSpec((1,) + a.shape[1:], lambda o: (o,) + (0,) * (a.ndim - 1))
    ops = (u2, xs, bre, bim, cre, cim, zre, zim)
    return pl.pallas_call(
        _s5_out_kernel,
        grid=(no,),
        in_specs=[blk(a) for a in ops],
        out_specs=pl.BlockSpec((1, m, LANES), lambda o: (o, 0, 0)),
        out_shape=jax.ShapeDtypeStruct((no, m, LANES), out_dtype),
        scratch_shapes=[pltpu.VMEM((m, LANES), F32)],
        compiler_params=_cparams(("parallel",)),
        name="s5_out",
    )(*ops)


def _s5_params(lam_re, lam_im, log_dt, b_re, b_im, c_re, c_im):
    t = S5_CHUNK
    ng = lam_re.shape[0]
    no = ng // OCT
    dt = jnp.exp(log_dt)[:, None]
    a = lam_re * dt
    b = lam_im * dt
    n = jnp.arange(t + 1, dtype=F32)[None, :, None]
    mag = jnp.exp(n * a[:, None, :])
    pw_re = mag * jnp.cos(n * b[:, None, :])
    pw_im = mag * jnp.sin(n * b[:, None, :])
    em1_re = jnp.expm1(a) * jnp.cos(b) - 2.0 * jnp.sin(0.5 * b) ** 2
    em1_im = jnp.exp(a) * jnp.sin(b)
    den = lam_re * lam_re + lam_im * lam_im
    z_re = (em1_re * lam_re + em1_im * lam_im) / den
    z_im = (em1_im * lam_re - em1_re * lam_im) / den
    bb_re = z_re[..., None] * b_re - z_im[..., None] * b_im
    bb_im = z_re[..., None] * b_im + z_im[..., None] * b_re
    def rows(w):
        w = w.reshape(no, OCT, t, S5_GROUP, S5_STATE).transpose(0, 2, 1, 3, 4)
        return jnp.concatenate([w, w], axis=-1).reshape(no, t * LANES, 2 * S5_STATE)

    bt_re = bb_re.transpose(0, 2, 1)[:, None]
    bt_im = bb_im.transpose(0, 2, 1)[:, None]
    r_re, r_im = pw_re[:, t - 1::-1, None, :], pw_im[:, t - 1::-1, None, :]
    bre = rows(r_re * bt_re - r_im * bt_im)
    bim = rows(r_re * bt_im + r_im * bt_re)
    o_re, o_im = pw_re[:, 1:, None, :], pw_im[:, 1:, None, :]
    cre = rows(c_re[:, None] * o_re - c_im[:, None] * o_im)
    cim = rows(-(c_re[:, None] * o_im + c_im[:, None] * o_re))
    zpad = lambda c: jnp.pad(c, ((0, 0), (0, 0), (0, S5_STATE))).reshape(no, LANES, 2 * S5_STATE)
    return bre, bim, cre, cim, zpad(c_re), zpad(c_im), pw_re, pw_im


def _s5_chunk(seqlen):
    return math.gcd(seqlen, S5_CHUNK)


def _s5_branch(u2, x0, ops, nseq, seqlen, out_dtype):
    bre, bim, cre, cim, zre, zim, pw_re, pw_im = ops
    no = u2.shape[0]
    t = _s5_chunk(seqlen)
    kk = t * LANES
    nchunks = seqlen // t
    are = pw_re[:, t].reshape(no, 1, OCT * S5_STATE)
    aim = pw_im[:, t].reshape(no, 1, OCT * S5_STATE)
    tail = S5_CHUNK * LANES - kk
    bre, bim = bre[:, tail:], bim[:, tail:]
    xs, xf = _s5_state(u2, bre, bim, x0, are, aim, nseq, nchunks)
    return _s5_out(u2, xs, bre, bim, cre[:, :kk], cim[:, :kk], zre, zim, out_dtype), xf


def _s5_state_in(re, im, no):
    nseq = re.shape[0]
    f = lambda a: a.reshape(nseq, no, OCT * S5_STATE).transpose(1, 0, 2)
    return jnp.concatenate([f(re), f(im)], axis=-1)


def _s5_state_out(xf):
    no, nseq, _ = xf.shape
    hw = OCT * S5_STATE
    f = lambda a: a.transpose(1, 0, 2).reshape(1, nseq, no * OCT, S5_STATE)
    return f(xf[:, :, :hw]), f(xf[:, :, hw:])


def _gla_kernel(q_ref, k_ref, v_ref, la_ref, gb_ref, s0_ref, ng_ref, o_ref, sf_ref, st_ref,
                *, nseq, nchunks, c):
    i = pl.program_id(1)

    @pl.when(i == 0)
    def _():
        st_ref[...] = s0_ref[...]

    nh, hk, hv = GLA_HEADS, GLA_HEAD_K, GLA_HEAD_V
    nch = nseq * nchunks
    tm = nch * c
    iota = lambda shape, d: lax.broadcasted_iota(jnp.int32, shape, d)
    ng = ng_ref[...]

    tb = min(tm, MXU_TILE)
    rt, ct = iota((tb, tb), 0), iota((tb, tb), 1)
    tril = ((rt // c == ct // c) & (rt >= ct)).astype(BF16)
    parts = []
    for r0 in range(0, tm, tb):
        la_hi, la_lo = _split_bf16(la_ref[r0:r0 + tb, :])
        parts.append(_dot(tril, la_hi) + _dot(tril, la_lo))
    bcum = jnp.concatenate(parts, axis=0)
    e_hi, e_lo = _split_bf16(jnp.exp(jnp.concatenate([bcum[(ci + 1) * c - 1:(ci + 1) * c] for ci in range(nch)],
                                                     axis=0)))
    pick = (iota((nch, LANES), 0) == iota((nch, LANES), 1)).astype(BF16)
    dec_t = _dot_tn(e_hi, pick) + _dot_tn(e_lo, pick)
    q_all = q_ref[...].astype(F32)
    k_all = k_ref[...].astype(F32)
    q_dec_all = q_all * jnp.exp(bcum)
    k_dec_all = k_all * jnp.exp(-bcum)

    own_k = iota((nh * c, nh * hk), 0) // c == iota((nh * c, nh * hk), 1) // hk
    own_v = iota((nh * c, nh * hv), 0) // c == iota((nh * c, nh * hv), 1) // hv
    causal = iota((c, nh * c), 1) % c <= iota((c, nh * c), 0)
    zero_v = jnp.zeros((hk, hv), F32)

    def intra(ci):
        rows = slice(ci * c, (ci + 1) * c)
        q_dec = q_dec_all[rows].astype(BF16)
        k_dec = k_dec_all[rows]
        b_c = bcum[rows]
        k_tail = (k_all[rows] * jnp.exp(b_c[c - 1:c] - b_c)).astype(BF16)
        v = v_ref[rows, :].astype(F32)
        k_bd = jnp.where(own_k, jnp.concatenate([k_dec] * nh, axis=0), 0.0).astype(BF16)
        v_bd = jnp.where(own_v, jnp.concatenate([v] * nh, axis=0), 0.0).astype(BF16)
        att = jnp.where(causal, _dot_nt(q_dec, k_bd), 0.0)
        o_intra = _dot(att.astype(BF16), v_bd)
        vb = v.astype(BF16)
        kvs = []
        for h0 in range(0, nh, 2):
            kv2 = _dot_tn(k_tail[:, h0 * hk:(h0 + 2) * hk], vb[:, h0 * hv:(h0 + 2) * hv])
            kvs += [kv2[:hk, :hv], kv2[hk:, hv:]]
        return q_dec, o_intra, kvs

    def carry(ci, sts, q_dec, o_intra, kvs):
        st_bd = jnp.concatenate(
            [jnp.concatenate([zero_v] * h + [sts[h]] + [zero_v] * (nh - 1 - h), axis=1) for h in range(nh)],
            axis=0).astype(BF16)
        o = o_intra + _dot(q_dec, st_bd)
        new = [dec_t[h * hk:(h + 1) * hk, ci:ci + 1] * sts[h] + kvs[h] for h in range(nh)]
        return o, new

    def finish(ci, o):
        rows = slice(ci * c, (ci + 1) * c)
        for h in range(nh):
            vs = slice(h * hv, (h + 1) * hv)
            oh = _rms_rows(o[:, vs], ng) * _silu(gb_ref[rows, vs].astype(F32))
            o_ref[rows, vs] = oh.astype(o_ref.dtype)

    sts = None
    pending = intra(0)
    unfinished = None
    for ci in range(nch):
        s, first, last = ci // nchunks, ci % nchunks == 0, ci % nchunks == nchunks - 1
        current = pending
        if ci + 1 < nch:
            pending = intra(ci + 1)
        if first:
            sts = [st_ref[s, h] for h in range(nh)]
        o, sts = carry(ci, sts, *current)
        if last:
            for h in range(nh):
                st_ref[s, h] = sts[h]
        if unfinished is not None:
            finish(*unfinished)
        unfinished = (ci, o)
    finish(*unfinished)

    @pl.when(i == pl.num_programs(1) - 1)
    def _():
        sf_ref[...] = st_ref[...]


def _gla(q, k, v, la, gb, s0, ng, nseq_total, seqlen, out_dtype):
    c = math.gcd(seqlen, GLA_CHUNK)
    if seqlen >= 8 * c:
        nseq, nchunks = 1, 8
    else:
        nseq, nchunks = 8, seqlen // c
    assert nseq_total % nseq == 0 and seqlen % (nchunks * c) == 0
    nblk = seqlen // (nchunks * c)
    tm = nseq * nchunks * c
    row = lambda n: pl.BlockSpec((tm, n), lambda b, i: (b * nblk + i, 0))
    st_spec = pl.BlockSpec((nseq, GLA_HEADS, GLA_HEAD_K, GLA_HEAD_V), lambda b, i: (b, 0, 0, 0))
    m = q.shape[0]
    return pl.pallas_call(
        functools.partial(_gla_kernel, nseq=nseq, nchunks=nchunks, c=c),
        grid=(nseq_total // nseq, nblk),
        in_specs=[row(256), row(256), row(512), row(256), row(512), st_spec, _full(ng.shape)],
        out_specs=[row(512), st_spec],
        out_shape=[jax.ShapeDtypeStruct((m, 512), out_dtype), jax.ShapeDtypeStruct(s0.shape, F32)],
        scratch_shapes=[pltpu.VMEM((nseq, GLA_HEADS, GLA_HEAD_K, GLA_HEAD_V), F32)],
        compiler_params=_cparams(("parallel", "arbitrary")),
        name="gla",
    )(q, k, v, la, gb, s0, ng)


def _gelu_tanh(x):
    return 0.5 * x * (1.0 + jnp.tanh(math.sqrt(2.0 / math.pi) * (x + 0.044715 * (x * x * x))))


def _even_out_tile(parts, x_ref, y_ref, u_ref, ga_ref, ob_ref, d_ref, wglu_ref, bglu_ref, woa_ref, wob_ref):
    no = y_ref.shape[0]

    def pre(rows):
        y = jnp.concatenate([y_ref[o, rows, :].astype(F32) for o in range(no)], axis=1)
        u = jnp.concatenate([u_ref[o, rows, :].astype(F32) for o in range(no)], axis=1)
        return _gelu_tanh(y + d_ref[...] * u)

    def glu(z):
        return _dot(z.astype(BF16), wglu_ref[...])

    def gated(rows, z, g):
        return (z * _sigmoid(g + bglu_ref[...]) * _silu(ga_ref[rows, :].astype(F32))).astype(BF16)

    def mix(rows, out_a):
        return _dot(out_a, woa_ref[...]) + _dot(ob_ref[rows, :].astype(BF16), wob_ref[...])

    zs, gs, ms = [], [], []
    for rows in parts:
        zs.append(pre(rows))
        gs.append(glu(zs[-1]))
    for rows, z, g in zip(parts, zs, gs):
        ms.append(mix(rows, gated(rows, z, g)))
    return [x_ref[rows, :] + m for rows, m in zip(parts, ms)]


def _head_ones(n):
    r = lax.broadcasted_iota(jnp.int32, (n, n), 0) // SWA_HEAD_DIM
    c = lax.broadcasted_iota(jnp.int32, (n, n), 1) // SWA_HEAD_DIM
    return (r == c).astype(BF16)


def _rope_block(x, cos_t, sin_t, upper):
    swapped = jnp.where(upper, pltpu.roll(x, 32, 1), pltpu.roll(x, 96, 1))
    return x * cos_t + swapped * sin_t


def _mid_kernel(x_ref, y_ref, u_ref, ga_ref, ob_ref, d_ref, wglu_ref, bglu_ref, woa_ref, wob_ref,
                g_ref, w_ref, gq_ref, gk_ref, cos_ref, sin_ref, h_ref, q_ref, k_ref, v_ref, gate_ref):
    tm = x_ref.shape[0]
    lane = lax.broadcasted_iota(jnp.int32, (tm, LANES), 1)
    upper = (lane & 32) != 0
    ones4 = _head_ones(MXU_TILE)
    inv_d = 1.0 / SWA_HEAD_DIM
    qw = SWA_HEADS * SWA_HEAD_DIM
    kw = SWA_KV_HEADS * SWA_HEAD_DIM
    nblk = qw // MXU_TILE
    h, = _even_out_tile([slice(0, tm)], x_ref, y_ref, u_ref, ga_ref, ob_ref, d_ref, wglu_ref, bglu_ref, woa_ref,
                        wob_ref)
    h_ref[...] = h
    xb = _rms_rows(h, g_ref[...]).astype(BF16)
    cos_t = cos_ref[...]
    sin_t = sin_ref[...]

    def q_block(j):
        return _dot(xb, w_ref[:, j * MXU_TILE:(j + 1) * MXU_TILE])

    def q_finish(j, q):
        ss = _dot((q * q).astype(BF16), ones4)
        qn = q * lax.rsqrt(ss * inv_d + NORM_EPS) * gq_ref[...]
        for e in range(MXU_TILE // LANES):
            cols = slice(j * MXU_TILE + e * LANES, j * MXU_TILE + (e + 1) * LANES)
            qe = _rope_block(qn[:, e * LANES:(e + 1) * LANES], cos_t, sin_t, upper)
            q_ref[:, cols] = (qe * (SWA_HEAD_DIM ** -0.5 * LOG2E)).astype(q_ref.dtype)

    q_next = q_block(0)
    for j in range(nblk):
        q_cur = q_next
        q_next = q_block(j + 1) if j + 1 < nblk else _dot(xb, w_ref[:, qw:qw + 2 * kw])
        q_finish(j, q_cur)
    kv = q_next
    gate = _dot(xb, w_ref[:, qw + 2 * kw:])
    k = kv[:, :kw]
    ss = _dot((k * k).astype(BF16), ones4[:kw, :kw])
    kn = k * lax.rsqrt(ss * inv_d + NORM_EPS) * gk_ref[...]
    k_ref[...] = _rope_block(kn, cos_t, sin_t, upper)
    v_ref[...] = kv[:, kw:]
    gate_ref[...] = gate.astype(gate_ref.dtype)


def _mid(x, y, u, ga, ob, pe, po, cos_t, sin_t, act_dtype):
    m, dm = x.shape
    tm = _row_tile(m)
    row = lambda n: pl.BlockSpec((tm, n), lambda i: (i, 0))
    slab = pl.BlockSpec((y.shape[0], tm, LANES), lambda i: (0, i, 0))
    assert cos_t.shape[0] % tm == 0
    nper = cos_t.shape[0] // tm
    tab = pl.BlockSpec((tm, LANES), lambda i: (i % nper, 0))
    qw = SWA_HEADS * SWA_HEAD_DIM
    kw = SWA_KV_HEADS * SWA_HEAD_DIM
    weights = [pe['d'], pe['wglu'], pe['bglu'], pe['woa'], pe['wob'], po['norm_g'], po['w_in'], po['gq'], po['gk']]
    return pl.pallas_call(
        _mid_kernel,
        grid=(m // tm,),
        in_specs=[row(dm), slab, slab, row(512), row(512)] + [_full(w.shape) for w in weights] + [tab, tab],
        out_specs=[row(dm), row(qw), row(kw), row(kw), row(qw)],
        out_shape=[jax.ShapeDtypeStruct((m, dm), F32),
                   jax.ShapeDtypeStruct((m, qw), act_dtype), jax.ShapeDtypeStruct((m, kw), F32),
                   jax.ShapeDtypeStruct((m, kw), F32), jax.ShapeDtypeStruct((m, qw), act_dtype)],
        compiler_params=_cparams(("parallel",)),
        name="mid",
    )(x, y, u, ga, ob, *weights, cos_t, sin_t)


def _rope_tables(pos):
    half = SWA_HEAD_DIM // 2
    inv_freq = ROPE_THETA ** (-np.arange(half, dtype=np.float64) / half)
    ang = np.asarray(pos, np.float64)[:, None] * inv_freq[None, :]
    cos, sin = np.cos(ang), np.sin(ang)
    cos_t = np.concatenate([cos, cos, cos, cos], axis=1).astype(np.float32)
    sin_t = np.concatenate([-sin, sin, -sin, sin], axis=1).astype(np.float32)
    return jnp.asarray(cos_t), jnp.asarray(sin_t)


def _attn_prompt_kernel(sink_ref, q_ref, kc_ref, kp_ref, vc_ref, vp_ref, gate_ref, h_ref, wout_ref,
                        y_ref, p_ref, o_ref, *, nqb):
    i = pl.program_id(1)
    w = SWA_WINDOW
    hd = SWA_HEAD_DIM
    npair = SWA_GROUP // 2
    lane2 = lax.broadcasted_iota(jnp.int32, (2 * w, LANES), 1)
    low = lane2 < hd
    rr = lax.broadcasted_iota(jnp.int32, (w, w), 0)
    cc = lax.broadcasted_iota(jnp.int32, (w, w), 1)
    tri = cc <= rr
    low_w = cc < hd
    r4 = lax.broadcasted_iota(jnp.int32, (4 * w, LANES), 0)
    c4 = lax.broadcasted_iota(jnp.int32, (4 * w, LANES), 1)
    den_cols = ((r4 < 2 * w) == (c4 < hd)).astype(BF16)
    units = [(jb, kv) for jb in range(nqb) for kv in range(SWA_KV_HEADS)]

    def scores(jb, kv):
        rows = slice(jb * w, (jb + 1) * w)
        if jb == 0:
            k_prev, v_prev = kp_ref[...], vp_ref[...]
        else:
            prev_rows = slice((jb - 1) * w, jb * w)
            k_prev, v_prev = kc_ref[prev_rows, :], vc_ref[prev_rows, :]
        kcat = jnp.concatenate([k_prev, kc_ref[rows, :]], axis=0)
        vcat = jnp.concatenate([v_prev, vc_ref[rows, :]], axis=0)
        own = low if kv == 0 else jnp.logical_not(low)
        k_own = jnp.where(own, kcat, 0.0)
        v_own = jnp.where(own, vcat, 0.0)
        k_oth = pltpu.roll(k_own, hd, 1)
        v_oth = pltpu.roll(v_own, hd, 1)
        k_lo, k_hi = (k_own, k_oth) if kv == 0 else (k_oth, k_own)
        v_lo, v_hi = (v_own, v_oth) if kv == 0 else (v_oth, v_own)
        k_rhs = jnp.concatenate([k_lo, k_hi], axis=0).astype(BF16)
        v_rhs = jnp.concatenate([jnp.concatenate([v_lo, v_hi], axis=0).astype(BF16), den_cols], axis=1)
        qs = jnp.concatenate([q_ref[rows, (kv * npair + pr) * LANES:(kv * npair + pr + 1) * LANES]
                              for pr in range(npair)], axis=0)
        return _dot_nt(qs, k_rhs), v_rhs

    def softmax_pv(kv, first, buf, s_all, v_rhs):
        sink_terms = []
        for pr in range(npair):
            prow = slice(pr * w, (pr + 1) * w)
            pair_terms = []
            for e in range(2):
                s_prev = s_all[prow, (2 * e) * w:(2 * e + 1) * w]
                s_cur = s_all[prow, (2 * e + 1) * w:(2 * e + 2) * w]
                if first:
                    s_prev = jnp.where(i > 0, s_prev, -jnp.inf)
                sc = jnp.where(tri, s_cur, s_prev)
                sink = sink_ref[2 * (kv * npair + pr) + e] * LOG2E
                mx = jnp.maximum(jnp.max(sc, axis=-1, keepdims=True), sink)
                pe = jnp.exp2(sc - mx)
                p_ref[buf, prow, (2 * e) * w:(2 * e + 1) * w] = jnp.where(tri, 0.0, pe).astype(BF16)
                p_ref[buf, prow, (2 * e + 1) * w:(2 * e + 2) * w] = jnp.where(tri, pe, 0.0).astype(BF16)
                pair_terms.append(jnp.exp2(sink - mx))
            sink_terms.append(pair_terms)
        return _dot(p_ref[buf], v_rhs), sink_terms

    def normalise(jb, kv, o_ext, sink_terms):
        rows = slice(jb * w, (jb + 1) * w)
        for pr in range(npair):
            prow = slice(pr * w, (pr + 1) * w)
            cols = slice((kv * npair + pr) * LANES, (kv * npair + pr + 1) * LANES)
            st = jnp.where(low_w, sink_terms[pr][0], sink_terms[pr][1])
            o = o_ext[prow, :LANES] / (o_ext[prow, LANES:] + st)
            o_ref[rows, cols] = o.astype(o_ref.dtype)

    pending = scores(*units[0])
    unfinished = None
    for n, (jb, kv) in enumerate(units):
        current = pending
        if n + 1 < len(units):
            pending = scores(*units[n + 1])
        result = softmax_pv(kv, jb == 0, n % 2, *current)
        if unfinished is not None:
            normalise(*unfinished)
        unfinished = (jb, kv) + result
    normalise(*unfinished)
    og = o_ref[...].astype(F32) * _silu(gate_ref[...].astype(F32))
    y_ref[...] = h_ref[...] + _dot(og.astype(BF16), wout_ref[...])


def _attn_prompt(sinks, q, k, v, gate, h, w_out, nseq, seqlen):
    w = SWA_WINDOW
    nqb = 4
    tm = nqb * w
    assert seqlen % tm == 0
    nblk = seqlen // tm
    qw = q.shape[1]
    dm = h.shape[1]
    row = lambda n: pl.BlockSpec((tm, n), lambda b, i: (b * nblk + i, 0))
    prev = lambda n: pl.BlockSpec((w, n), lambda b, i: (jnp.maximum((b * nblk + i) * nqb - 1, 0), 0))
    smem = pl.BlockSpec(memory_space=pltpu.SMEM)
    return pl.pallas_call(
        functools.partial(_attn_prompt_kernel, nqb=nqb),
        grid=(nseq, nblk),
        in_specs=[smem, row(qw), row(LANES), prev(LANES), row(LANES), prev(LANES), row(qw), row(dm),
                  _full(w_out.shape)],
        out_specs=row(dm),
        out_shape=jax.ShapeDtypeStruct(h.shape, F32),
        scratch_shapes=[pltpu.VMEM((2, 4 * w, 4 * w), BF16), pltpu.VMEM((tm, qw), BF16)],
        compiler_params=_cparams(("parallel", "arbitrary")),
        name="attn_prompt",
    )(sinks, q, k, k, v, v, gate, h, w_out)


def _attn_sample_kernel(sink_ref, q_ref, kn_ref, vn_ref, kc_ref, vc_ref, o_ref, ko_ref, vo_ref, *, seqlen):
    nq = SWA_HEADS * seqlen
    ncache = kc_ref.shape[1]
    t_row = lax.broadcasted_iota(jnp.int32, (nq, ncache), 0) % seqlen
    c_col = lax.broadcasted_iota(jnp.int32, (nq, ncache), 1)
    cache_ok = c_col > t_row - (SWA_WINDOW - ncache)
    nnew = kn_ref.shape[1]
    t_row_n = lax.broadcasted_iota(jnp.int32, (nq, nnew), 0) % seqlen
    n_col = lax.broadcasted_iota(jnp.int32, (nq, nnew), 1)
    new_ok = n_col <= t_row_n
    sink = (sink_ref[...] * LOG2E)[None, :, 0:1]
    bqk = lambda a, b: lax.dot_general(a, b, (((2,), (2,)), ((0,), (0,))), preferred_element_type=F32)
    bpv = lambda a, b: lax.dot_general(a, b, (((2,), (1,)), ((0,), (0,))), preferred_element_type=F32)
    q = q_ref[...]
    kc = kc_ref[...]
    vc = vc_ref[...]
    kn = kn_ref[...]
    vn = vn_ref[...]
    sc = jnp.where(cache_ok[None], bqk(q, kc.astype(BF16)), -jnp.inf)
    sn = jnp.where(new_ok[None], bqk(q, kn.astype(BF16)), -jnp.inf)
    mx = jnp.maximum(jnp.maximum(jnp.max(sc, axis=-1, keepdims=True), jnp.max(sn, axis=-1, keepdims=True)), sink)
    pc = jnp.exp2(sc - mx)
    pn = jnp.exp2(sn - mx)
    den = jnp.sum(pc, axis=-1, keepdims=True) + jnp.sum(pn, axis=-1, keepdims=True) + jnp.exp2(sink - mx)
    inv = 1.0 / den
    o_ref[...] = bpv((pc * inv).astype(BF16), vc.astype(BF16)) + bpv((pn * inv).astype(BF16), vn.astype(BF16))
    keep = ncache - seqlen
    ko_ref[:, 0:keep, :] = kc[:, seqlen:ncache, :]
    ko_ref[:, keep:ncache, :] = kn[:, 0:seqlen, :]
    vo_ref[:, 0:keep, :] = vc[:, seqlen:ncache, :]
    vo_ref[:, keep:ncache, :] = vn[:, 0:seqlen, :]


def _attn_sample(sink_rows, q, kn, vn, kc, vc, seqlen):
    n = q.shape[0]
    nseq = 8
    assert n % nseq == 0
    blk = lambda a: pl.BlockSpec((nseq,) + a.shape[1:], lambda i: (i, 0, 0))
    return pl.pallas_call(
        functools.partial(_attn_sample_kernel, seqlen=seqlen),
        grid=(n // nseq,),
        in_specs=[_full(sink_rows.shape), blk(q), blk(kn), blk(vn), blk(kc), blk(vc)],
        out_specs=[blk(q), blk(kc), blk(vc)],
        out_shape=[jax.ShapeDtypeStruct(q.shape, F32), jax.ShapeDtypeStruct(kc.shape, F32),
                   jax.ShapeDtypeStruct(vc.shape, F32)],
        compiler_params=_cparams(("parallel",)),
        name="attn_sample",
    )(sink_rows, q, kn, vn, kc, vc)


def _odd_out_kernel(h_ref, o_ref, gate_ref, w_ref, y_ref):
    og = o_ref[...].astype(F32) * _silu(gate_ref[...].astype(F32))
    y_ref[...] = h_ref[...] + _dot(og.astype(BF16), w_ref[...])


def _odd_out(h, o, gate, w):
    m, dm = h.shape
    tm = _row_tile(m)
    row = lambda n: pl.BlockSpec((tm, n), lambda i: (i, 0))
    return pl.pallas_call(
        _odd_out_kernel,
        grid=(m // tm,),
        in_specs=[row(dm), row(o.shape[1]), row(gate.shape[1]), _full(w.shape)],
        out_specs=row(dm),
        out_shape=jax.ShapeDtypeStruct((m, dm), F32),
        compiler_params=_cparams(("parallel",)),
        name="odd_out",
    )(h, o, gate, w)


def _trunk(x, s5_x0, gla_s0, pe, po, pos, nseq, seqlen, act_dtype):
    u4, u2, ga, q, k, v, la, gb = _even_in(x, pe['norm_g'], pe['wm'], pe['wa'], pe['wgb'], pe['wgate'],
                                           pe['bgate'], act_dtype, _s5_chunk(seqlen))
    y4, s5_fin = _s5_branch(u2, s5_x0, pe['s5_ops'], nseq, seqlen, act_dtype)
    ob, gla_fin = _gla(q, k, v, la, gb, gla_s0, pe['gla_norm_g'], nseq, seqlen, act_dtype)
    cos_t, sin_t = _rope_tables(pos)
    h, q1, k1, v1, gate = _mid(x, y4, u4, ga, ob, pe, po, cos_t, sin_t, act_dtype)
    return h, q1, k1, v1, gate, s5_fin, gla_fin


def kernel(x_prompt, x_sample, state_s5_re, state_s5_im, state_gla, cache_swa_k, cache_swa_v,
           even_norm_g, even_w_in, s5_lambda_re, s5_lambda_im, s5_log_dt, s5_b_re, s5_b_im,
           s5_c_re, s5_c_im, s5_d, s5_w_glu, s5_b_glu, gla_w_gate, gla_b_gate, gla_norm_g,
           even_w_out, odd_norm_g, odd_w_in, swa_q_norm_g, swa_k_norm_g, swa_sinks, odd_w_out):
    nb, seq, dm = x_prompt.shape
    ns, dseq, _ = x_sample.shape
    ng = s5_lambda_re.shape[1]
    no = ng // OCT
    xp = x_prompt.reshape(nb * seq, dm)
    xs = x_sample.reshape(ns * dseq, dm)

    i = 0
    w_in = even_w_in[i]
    s5w = ng * S5_GROUP
    col_alow = 2 * s5w + 2 * GLA_HEADS * GLA_HEAD_K + GLA_HEADS * GLA_HEAD_V
    col_gb = col_alow + GLA_GATE_RANK
    pad_rank = LANES - GLA_GATE_RANK
    pe = {
        'norm_g': even_norm_g[i][None, :],
        'wm': w_in[:, :col_alow].astype(BF16),
        'wa': jnp.pad(w_in[:, col_alow:col_gb], ((0, 0), (0, pad_rank))).astype(BF16),
        'wgb': w_in[:, col_gb:].astype(BF16),
        'wgate': jnp.pad(gla_w_gate[i], ((0, pad_rank), (0, 0))).astype(BF16),
        'bgate': gla_b_gate[i][None, :],
        's5_ops': _s5_params(s5_lambda_re[i], s5_lambda_im[i], s5_log_dt[i], s5_b_re[i], s5_b_im[i],
                             s5_c_re[i], s5_c_im[i]),
        'gla_norm_g': gla_norm_g[i][None, :],
        'd': s5_d[i][None, :],
        'wglu': s5_w_glu[i].astype(BF16),
        'bglu': s5_b_glu[i][None, :],
        'woa': even_w_out[i][:s5w].astype(BF16),
        'wob': even_w_out[i][s5w:].astype(BF16),
    }
    po = {
        'norm_g': odd_norm_g[i][None, :],
        'w_in': odd_w_in[i].astype(BF16),
        'gq': jnp.tile(swa_q_norm_g[i], MXU_TILE // SWA_HEAD_DIM)[None, :],
        'gk': jnp.tile(swa_k_norm_g[i], LANES // SWA_HEAD_DIM)[None, :],
    }
    w_out = odd_w_out[i].astype(BF16)
    sinks = swa_sinks[i]
    kvw = SWA_KV_HEADS * SWA_HEAD_DIM

    s5_zero = jnp.zeros((no, nb, 2 * OCT * S5_STATE), F32)
    gla_zero = jnp.zeros((nb, GLA_HEADS, GLA_HEAD_K, GLA_HEAD_V), F32)
    hp, q, k, v, gate, s5_p, gla_p = _trunk(xp, s5_zero, gla_zero, pe, po, np.arange(seq), nb, seq, BF16)
    s5r_p, s5i_p = _s5_state_out(s5_p)
    y_prompt = _attn_prompt(sinks, q, k, v, gate, hp, w_out, nb, seq).reshape(nb, seq, dm)
    cache_len = min(SWA_WINDOW, seq)
    tail = lambda a: (a.reshape(nb, seq, kvw)[:, seq - cache_len:]
                      .reshape(1, nb, cache_len, SWA_KV_HEADS, SWA_HEAD_DIM))
    swk_p, swv_p = tail(k), tail(v)

    pos_s = np.tile(PAST_LEN + np.arange(dseq), ns)
    s5_init = _s5_state_in(state_s5_re[i], state_s5_im[i], no)
    hs, q, k, v, gate, s5_s, gla_s = _trunk(xs, s5_init, state_gla[i], pe, po, pos_s, ns, dseq, F32)
    s5r_s, s5i_s = _s5_state_out(s5_s)
    ncache = cache_swa_k.shape[2]
    qh = q.reshape(ns, dseq, SWA_KV_HEADS, SWA_GROUP, SWA_HEAD_DIM).transpose(0, 2, 3, 1, 4)
    zq = jnp.zeros_like(qh[:, 0])
    q2 = jnp.stack([jnp.concatenate([qh[:, 0], zq], axis=-1), jnp.concatenate([zq, qh[:, 1]], axis=-1)], axis=1)
    q2 = q2.reshape(ns, SWA_HEADS * dseq, kvw).astype(BF16)
    sink_rows = jnp.broadcast_to(jnp.repeat(sinks, dseq)[:, None], (SWA_HEADS * dseq, LANES))
    kn = k.reshape(ns, dseq, kvw)
    vn = v.reshape(ns, dseq, kvw)
    npad = 16 - dseq
    kn_pad = jnp.pad(kn, ((0, 0), (0, npad), (0, 0)))
    vn_pad = jnp.pad(vn, ((0, 0), (0, npad), (0, 0)))
    kc = cache_swa_k[i].reshape(ns, ncache, kvw)
    vc = cache_swa_v[i].reshape(ns, ncache, kvw)
    o2, kc_new, vc_new = _attn_sample(sink_rows, q2, kn_pad, vn_pad, kc, vc, dseq)
    o2 = o2.reshape(ns, SWA_KV_HEADS, SWA_GROUP, dseq, SWA_KV_HEADS, SWA_HEAD_DIM)
    o = jnp.stack([o2[:, 0, :, :, 0], o2[:, 1, :, :, 1]], axis=1)
    o = o.transpose(0, 3, 1, 2, 4).reshape(ns * dseq, SWA_HEADS * SWA_HEAD_DIM)
    y_sample = _odd_out(hs, o, gate, w_out).reshape(ns, dseq, dm)
    swk_s = kc_new.reshape(1, ns, ncache, SWA_KV_HEADS, SWA_HEAD_DIM)
    swv_s = vc_new.reshape(1, ns, ncache, SWA_KV_HEADS, SWA_HEAD_DIM)

    return (y_prompt, y_sample,
            s5r_p, s5i_p, gla_p[None], swk_p, swv_p,
            s5r_s, s5i_s, gla_s[None], swk_s, swv_s)
```

```python
import functools
import math

import jax
import jax.numpy as jnp
import numpy as np
from jax import lax
from jax.experimental import pallas as pl
from jax.experimental.pallas import tpu as pltpu

F32 = jnp.float32
BF16 = jnp.bfloat16

PAST_LEN = 8192
NORM_EPS = 1e-6
S5_GROUP = 16
S5_STATE = 64
S5_CHUNK = 16
GLA_HEADS = 4
GLA_HEAD_K = 64
GLA_HEAD_V = 128
GLA_GATE_RANK = 16
GLA_GATE_TAU = 16.0
GLA_CHUNK = 64
SWA_HEADS = 16
SWA_KV_HEADS = 2
SWA_GROUP = SWA_HEADS // SWA_KV_HEADS
SWA_HEAD_DIM = 64
SWA_WINDOW = 128
ROPE_THETA = 10000.0
LOG2E = math.log2(math.e)
LANES = 128
MXU_TILE = 256
OCT = LANES // S5_GROUP
ROW_TILE = 512
VMEM_LIMIT = 48 * 1024 * 1024


def _cparams(sem):
    return pltpu.CompilerParams(dimension_semantics=sem, vmem_limit_bytes=VMEM_LIMIT)


def _full(shape):
    n = len(shape)
    return pl.BlockSpec(shape, lambda *_: (0,) * n)


def _dot(a, b):
    return jnp.dot(a, b, preferred_element_type=F32)


def _dot_nt(a, b):
    return lax.dot_general(a, b, (((1,), (1,)), ((), ())), preferred_element_type=F32)


def _dot_tn(a, b):
    return lax.dot_general(a, b, (((0,), (0,)), ((), ())), preferred_element_type=F32)


def _split_bf16(x):
    hi = x.astype(BF16)
    lo = (x - hi.astype(F32)).astype(BF16)
    return hi, lo


def _rms_rows(x, g):
    return x * lax.rsqrt(jnp.mean(x * x, axis=-1, keepdims=True) + NORM_EPS) * g


def _sigmoid(x):
    return 1.0 / (1.0 + jnp.exp(-x))


def _silu(x):
    return x * _sigmoid(x)


def _row_tile(m, tile=ROW_TILE):
    return tile if m % tile == 0 else m


def _even_in_kernel(x_ref, g_ref, wm_ref, wa_ref, wgb_ref, wgate_ref, bgate_ref,
                    u_ref, u2_ref, ga_ref, q_ref, k_ref, v_ref, la_ref, gb_ref, uscr_ref, *, t):
    xb = _rms_rows(x_ref[...], g_ref[...]).astype(BF16)

    def proj(lo, hi):
        return _dot(xb, wm_ref[:, lo:hi])

    u = proj(0, 512)
    nrow = u.shape[0] // t
    for o in range(u_ref.shape[0]):
        uo = u[:, o * LANES:(o + 1) * LANES]
        u_ref[o] = uo.astype(u_ref.dtype)
        uscr_ref[o] = uo
        for tt in range(t):
            piece = uscr_ref[o, pl.ds(tt, nrow, stride=t), :]
            u2_ref[o, :, tt * LANES:(tt + 1) * LANES] = piece.astype(u2_ref.dtype)
    ga_ref[...] = proj(512, 1024).astype(ga_ref.dtype)
    q_ref[...] = (proj(1024, 1280) * (GLA_HEAD_K ** -0.5)).astype(q_ref.dtype)
    k_ref[...] = proj(1280, 1536).astype(k_ref.dtype)
    v_ref[...] = proj(1536, 2048).astype(v_ref.dtype)
    gb_ref[...] = _dot(xb, wgb_ref[...]).astype(gb_ref.dtype)
    a_low = _dot(xb, wa_ref[...])
    logit = _dot(a_low.astype(BF16), wgate_ref[...]) + bgate_ref[...]
    log_sig = jnp.minimum(logit, 0.0) - jnp.log1p(jnp.exp(-jnp.abs(logit)))
    la_ref[...] = log_sig * (1.0 / GLA_GATE_TAU)


def _even_in(x, g, wm, wa, wgb, wgate, bgate, act_dtype, t):
    m, d = x.shape
    tm = _row_tile(m, 2 * ROW_TILE) if m > ROW_TILE else m
    row = lambda n: pl.BlockSpec((tm, n), lambda i: (i, 0))
    no = 512 // LANES
    slab = pl.BlockSpec((no, tm, LANES), lambda i: (0, i, 0))
    chunk = pl.BlockSpec((no, tm // t, t * LANES), lambda i: (0, i, 0))
    outs = [(512, act_dtype), (256, act_dtype), (256, act_dtype), (512, act_dtype), (256, F32), (512, act_dtype)]
    return pl.pallas_call(
        functools.partial(_even_in_kernel, t=t),
        grid=(m // tm,),
        in_specs=[row(d), _full(g.shape), _full(wm.shape), _full(wa.shape), _full(wgb.shape),
                  _full(wgate.shape), _full(bgate.shape)],
        out_specs=[slab, chunk] + [row(n) for n, _ in outs],
        out_shape=[jax.ShapeDtypeStruct((no, m, LANES), act_dtype),
                   jax.ShapeDtypeStruct((no, m // t, t * LANES), act_dtype)]
        + [jax.ShapeDtypeStruct((m, n), dt) for n, dt in outs],
        scratch_shapes=[pltpu.VMEM((no, tm, LANES), F32)],
        compiler_params=_cparams(("parallel",)),
        name="even_in",
    )(x, g, wm, wa, wgb, wgate, bgate)


def _group_mask(shape, row_span, col_span):
    rg = (lax.broadcasted_iota(jnp.int32, shape, 0) // row_span) % OCT
    cg = (lax.broadcasted_iota(jnp.int32, shape, 1) // col_span) % OCT
    return rg == cg


def _s5_state_kernel(u_ref, bre_ref, bim_ref, x0_ref, are_ref, aim_ref, xs_ref, xf_ref, loc_ref, *, nseq, nchunks):
    hw = OCT * S5_STATE
    kk = u_ref.shape[2]
    rep = hw // LANES
    bp = jnp.concatenate([bre_ref[0]] * rep + [bim_ref[0]] * rep, axis=1)
    bp = jnp.where(_group_mask((kk, 2 * hw), S5_GROUP, S5_STATE), bp, 0.0).astype(BF16)
    loc_ref[...] = _dot(u_ref[0], bp)
    a_re = are_ref[0]
    a_im = aim_ref[0]
    if nchunks == 1:
        x0 = x0_ref[0]
        xr, xi = x0[:, :hw], x0[:, hw:]
        loc = loc_ref[...]
        xf_ref[0, :, :hw] = a_re * xr - a_im * xi + loc[:, :hw]
        xf_ref[0, :, hw:] = a_re * xi + a_im * xr + loc[:, hw:]
        xs_ref[0] = x0.astype(xs_ref.dtype)
    else:
        def body(j, carry):
            new = []
            for b in range(nseq):
                xr, xi = carry[b]
                row = pl.ds(b * nchunks + j, 1)
                lr = loc_ref[row, :hw]
                li = loc_ref[row, hw:]
                loc_ref[row, :hw] = xr
                loc_ref[row, hw:] = xi
                new.append((a_re * xr - a_im * xi + lr, a_re * xi + a_im * xr + li))
            return tuple(new)

        init = tuple((x0_ref[0, b:b + 1, :hw], x0_ref[0, b:b + 1, hw:]) for b in range(nseq))
        fin = lax.fori_loop(0, nchunks, body, init, unroll=4)
        for b in range(nseq):
            xf_ref[0, b:b + 1, :hw] = fin[b][0]
            xf_ref[0, b:b + 1, hw:] = fin[b][1]
        xs_ref[0] = loc_ref[...].astype(xs_ref.dtype)


def _s5_state(u2, bre, bim, x0, are, aim, nseq, nchunks):
    no, r, kk = u2.shape
    sw = x0.shape[2]
    blk = lambda a: pl.BlockSpec((1,) + a.shape[1:], lambda o: (o,) + (0,) * (a.ndim - 1))
    return pl.pallas_call(
        functools.partial(_s5_state_kernel, nseq=nseq, nchunks=nchunks),
        grid=(no,),
        in_specs=[blk(u2), blk(bre), blk(bim), blk(x0), blk(are), blk(aim)],
        out_specs=[pl.BlockSpec((1, r, sw), lambda o: (o, 0, 0)), blk(x0)],
        out_shape=[jax.ShapeDtypeStruct((no, r, sw), BF16), jax.ShapeDtypeStruct(x0.shape, F32)],
        scratch_shapes=[pltpu.VMEM((r, sw), F32)],
        compiler_params=_cparams(("parallel",)),
        name="s5_state",
    )(u2, bre, bim, x0, are, aim)


def _dot_nt_f32(a, b):
    a_hi, a_lo = _split_bf16(a)
    b_hi, b_lo = _split_bf16(b)
    return _dot_nt(a_hi, b_hi) + _dot_nt(a_hi, b_lo) + _dot_nt(a_lo, b_hi)


def _s5_out_kernel(u_ref, xs_ref, bre_ref, bim_ref, cre_ref, cim_ref, zre_ref, zim_ref, y_ref, yscr_ref):
    r, kk = u_ref.shape[1], u_ref.shape[2]
    t = kk // LANES
    sw = xs_ref.shape[2]
    ntile = kk // MXU_TILE
    rep = sw // (2 * LANES)
    taps = _dot_nt_f32(bre_ref[0], zre_ref[0]) - _dot_nt_f32(bim_ref[0], zim_ref[0])
    tmask = _group_mask((LANES, LANES), S5_GROUP, S5_GROUP)

    def tap(lag):
        if lag < 0:
            return jnp.zeros((LANES, LANES), F32)
        s = t - 1 - lag
        return jnp.where(tmask, taps[s * LANES:(s + 1) * LANES], 0.0)

    wts = [jnp.concatenate([jnp.concatenate([tap(2 * d), tap(2 * d + 1)], axis=1),
                            jnp.concatenate([tap(2 * d - 1), tap(2 * d)], axis=1)], axis=0).astype(BF16)
           for d in range(ntile)]
    cpt = jnp.concatenate([cre_ref[0]] * rep + [cim_ref[0]] * rep, axis=1)
    cpt = jnp.where(_group_mask((kk, sw), S5_GROUP, S5_STATE), cpt, 0.0).astype(BF16)
    xs = xs_ref[0]
    for n in range(ntile):
        acc = _dot_nt(xs, cpt[n * MXU_TILE:(n + 1) * MXU_TILE])
        for k in range(n + 1):
            acc = acc + _dot(u_ref[0, :, k * MXU_TILE:(k + 1) * MXU_TILE], wts[n - k])
        for e in range(MXU_TILE // LANES):
            yscr_ref[pl.ds(2 * n + e, r, stride=t), :] = acc[:, e * LANES:(e + 1) * LANES]
    y_ref[0] = yscr_ref[...].astype(y_ref.dtype)


def _s5_out(u2, xs, bre, bim, cre, cim, zre, zim, out_dtype):
    no, r, kk = u2.shape
    m = r * (kk // LANES)
    blk = lambda a: pl.BlockSpec((1,) + a.shape[1:], lambda o: (o,) + (0,) * (a.ndim - 1))
    ops = (u2, xs, bre, bim, cre, cim, zre, zim)
    return pl.pallas_call(
        _s5_out_kernel,
        grid=(no,),
        in_specs=[blk(a) for a in ops],
        out_specs=pl.BlockSpec((1, m, LANES), lambda o: (o, 0, 0)),
        out_shape=jax.ShapeDtypeStruct((no, m, LANES), out_dtype),
        scratch_shapes=[pltpu.VMEM((m, LANES), F32)],
        compiler_params=_cparams(("parallel",)),
        name="s5_out",
    )(*ops)


def _s5_params(lam_re, lam_im, log_dt, b_re, b_im, c_re, c_im):
    t = S5_CHUNK
    ng = lam_re.shape[0]
    no = ng // OCT
    dt = jnp.exp(log_dt)[:, None]
    a = lam_re * dt
    b = lam_im * dt
    n = jnp.arange(t + 1, dtype=F32)[None, :, None]
    mag = jnp.exp(n * a[:, None, :])
    pw_re = mag * jnp.cos(n * b[:, None, :])
    pw_im = mag * jnp.sin(n * b[:, None, :])
    em1_re = jnp.expm1(a) * jnp.cos(b) - 2.0 * jnp.sin(0.5 * b) ** 2
    em1_im = jnp.exp(a) * jnp.sin(b)
    den = lam_re * lam_re + lam_im * lam_im
    z_re = (em1_re * lam_re + em1_im * lam_im) / den
    z_im = (em1_im * lam_re - em1_re * lam_im) / den
    bb_re = z_re[..., None] * b_re - z_im[..., None] * b_im
    bb_im = z_re[..., None] * b_im + z_im[..., None] * b_re
    def rows(w):
        w = w.reshape(no, OCT, t, S5_GROUP, S5_STATE).transpose(0, 2, 1, 3, 4)
        return jnp.concatenate([w, w], axis=-1).reshape(no, t * LANES, 2 * S5_STATE)

    bt_re = bb_re.transpose(0, 2, 1)[:, None]
    bt_im = bb_im.transpose(0, 2, 1)[:, None]
    r_re, r_im = pw_re[:, t - 1::-1, None, :], pw_im[:, t - 1::-1, None, :]
    bre = rows(r_re * bt_re - r_im * bt_im)
    bim = rows(r_re * bt_im + r_im * bt_re)
    o_re, o_im = pw_re[:, 1:, None, :], pw_im[:, 1:, None, :]
    cre = rows(c_re[:, None] * o_re - c_im[:, None] * o_im)
    cim = rows(-(c_re[:, None] * o_im + c_im[:, None] * o_re))
    zpad = lambda c: jnp.pad(c, ((0, 0), (0, 0), (0, S5_STATE))).reshape(no, LANES, 2 * S5_STATE)
    return bre, bim, cre, cim, zpad(c_re), zpad(c_im), pw_re, pw_im


def _s5_chunk(seqlen):
    return math.gcd(seqlen, S5_CHUNK)


def _s5_branch(u2, x0, ops, nseq, seqlen, out_dtype):
    bre, bim, cre, cim, zre, zim, pw_re, pw_im = ops
    no = u2.shape[0]
    t = _s5_chunk(seqlen)
    kk = t * LANES
    nchunks = seqlen // t
    are = pw_re[:, t].reshape(no, 1, OCT * S5_STATE)
    aim = pw_im[:, t].reshape(no, 1, OCT * S5_STATE)
    tail = S5_CHUNK * LANES - kk
    bre, bim = bre[:, tail:], bim[:, tail:]
    xs, xf = _s5_state(u2, bre, bim, x0, are, aim, nseq, nchunks)
    return _s5_out(u2, xs, bre, bim, cre[:, :kk], cim[:, :kk], zre, zim, out_dtype), xf


def _s5_state_in(re, im, no):
    nseq = re.shape[0]
    f = lambda a: a.reshape(nseq, no, OCT * S5_STATE).transpose(1, 0, 2)
    return jnp.concatenate([f(re), f(im)], axis=-1)


def _s5_state_out(xf):
    no, nseq, _ = xf.shape
    hw = OCT * S5_STATE
    f = lambda a: a.transpose(1, 0, 2).reshape(1, nseq, no * OCT, S5_STATE)
    return f(xf[:, :, :hw]), f(xf[:, :, hw:])


def _gla_kernel(q_ref, k_ref, v_ref, la_ref, gb_ref, s0_ref, ng_ref, o_ref, sf_ref, st_ref,
                *, nseq, nchunks, c):
    i = pl.program_id(1)

    @pl.when(i == 0)
    def _():
        st_ref[...] = s0_ref[...]

    nh, hk, hv = GLA_HEADS, GLA_HEAD_K, GLA_HEAD_V
    nch = nseq * nchunks
    tm = nch * c
    iota = lambda shape, d: lax.broadcasted_iota(jnp.int32, shape, d)
    ng = ng_ref[...]

    tb = min(tm, MXU_TILE)
    rt, ct = iota((tb, tb), 0), iota((tb, tb), 1)
    tril = ((rt // c == ct // c) & (rt >= ct)).astype(BF16)
    parts = []
    for r0 in range(0, tm, tb):
        la_hi, la_lo = _split_bf16(la_ref[r0:r0 + tb, :])
        parts.append(_dot(tril, la_hi) + _dot(tril, la_lo))
    bcum = jnp.concatenate(parts, axis=0)
    e_hi, e_lo = _split_bf16(jnp.exp(jnp.concatenate([bcum[(ci + 1) * c - 1:(ci + 1) * c] for ci in range(nch)],
                                                     axis=0)))
    pick = (iota((nch, LANES), 0) == iota((nch, LANES), 1)).astype(BF16)
    dec_t = _dot_tn(e_hi, pick) + _dot_tn(e_lo, pick)
    q_all = q_ref[...].astype(F32)
    k_all = k_ref[...].astype(F32)
    q_dec_all = q_all * jnp.exp(bcum)
    k_dec_all = k_all * jnp.exp(-bcum)

    own_k = iota((nh * c, nh * hk), 0) // c == iota((nh * c, nh * hk), 1) // hk
    own_v = iota((nh * c, nh * hv), 0) // c == iota((nh * c, nh * hv), 1) // hv
    causal = iota((c, nh * c), 1) % c <= iota((c, nh * c), 0)
    zero_v = jnp.zeros((hk, hv), F32)

    def intra(ci):
        rows = slice(ci * c, (ci + 1) * c)
        q_dec = q_dec_all[rows].astype(BF16)
        k_dec = k_dec_all[rows]
        b_c = bcum[rows]
        k_tail = (k_all[rows] * jnp.exp(b_c[c - 1:c] - b_c)).astype(BF16)
        v = v_ref[rows, :].astype(F32)
        k_bd = jnp.where(own_k, jnp.concatenate([k_dec] * nh, axis=0), 0.0).astype(BF16)
        v_bd = jnp.where(own_v, jnp.concatenate([v] * nh, axis=0), 0.0).astype(BF16)
        att = jnp.where(causal, _dot_nt(q_dec, k_bd), 0.0)
        o_intra = _dot(att.astype(BF16), v_bd)
        vb = v.astype(BF16)
        kvs = []
        for h0 in range(0, nh, 2):
            kv2 = _dot_tn(k_tail[:, h0 * hk:(h0 + 2) * hk], vb[:, h0 * hv:(h0 + 2) * hv])
            kvs += [kv2[:hk, :hv], kv2[hk:, hv:]]
        return q_dec, o_intra, kvs

    def carry(ci, sts, q_dec, o_intra, kvs):
        st_bd = jnp.concatenate(
            [jnp.concatenate([zero_v] * h + [sts[h]] + [zero_v] * (nh - 1 - h), axis=1) for h in range(nh)],
            axis=0).astype(BF16)
        o = o_intra + _dot(q_dec, st_bd)
        new = [dec_t[h * hk:(h + 1) * hk, ci:ci + 1] * sts[h] + kvs[h] for h in range(nh)]
        return o, new

    def finish(ci, o):
        rows = slice(ci * c, (ci + 1) * c)
        for h in range(nh):
            vs = slice(h * hv, (h + 1) * hv)
            oh = _rms_rows(o[:, vs], ng) * _silu(gb_ref[rows, vs].astype(F32))
            o_ref[rows, vs] = oh.astype(o_ref.dtype)

    sts = None
    pending = intra(0)
    unfinished = None
    for ci in range(nch):
        s, first, last = ci // nchunks, ci % nchunks == 0, ci % nchunks == nchunks - 1
        current = pending
        if ci + 1 < nch:
            pending = intra(ci + 1)
        if first:
            sts = [st_ref[s, h] for h in range(nh)]
        o, sts = carry(ci, sts, *current)
        if last:
            for h in range(nh):
                st_ref[s, h] = sts[h]
        if unfinished is not None:
            finish(*unfinished)
        unfinished = (ci, o)
    finish(*unfinished)

    @pl.when(i == pl.num_programs(1) - 1)
    def _():
        sf_ref[...] = st_ref[...]


def _gla(q, k, v, la, gb, s0, ng, nseq_total, seqlen, out_dtype):
    c = math.gcd(seqlen, GLA_CHUNK)
    if seqlen >= 8 * c:
        nseq, nchunks = 1, 8
    else:
        nseq, nchunks = 8, seqlen // c
    assert nseq_total % nseq == 0 and seqlen % (nchunks * c) == 0
    nblk = seqlen // (nchunks * c)
    tm = nseq * nchunks * c
    row = lambda n: pl.BlockSpec((tm, n), lambda b, i: (b * nblk + i, 0))
    st_spec = pl.BlockSpec((nseq, GLA_HEADS, GLA_HEAD_K, GLA_HEAD_V), lambda b, i: (b, 0, 0, 0))
    m = q.shape[0]
    return pl.pallas_call(
        functools.partial(_gla_kernel, nseq=nseq, nchunks=nchunks, c=c),
        grid=(nseq_total // nseq, nblk),
        in_specs=[row(256), row(256), row(512), row(256), row(512), st_spec, _full(ng.shape)],
        out_specs=[row(512), st_spec],
        out_shape=[jax.ShapeDtypeStruct((m, 512), out_dtype), jax.ShapeDtypeStruct(s0.shape, F32)],
        scratch_shapes=[pltpu.VMEM((nseq, GLA_HEADS, GLA_HEAD_K, GLA_HEAD_V), F32)],
        compiler_params=_cparams(("parallel", "arbitrary")),
        name="gla",
    )(q, k, v, la, gb, s0, ng)


def _gelu_tanh(x):
    return 0.5 * x * (1.0 + jnp.tanh(math.sqrt(2.0 / math.pi) * (x + 0.044715 * (x * x * x))))


def _head_ones(n):
    r = lax.broadcasted_iota(jnp.int32, (n, n), 0) // SWA_HEAD_DIM
    c = lax.broadcasted_iota(jnp.int32, (n, n), 1) // SWA_HEAD_DIM
    return (r == c).astype(BF16)


def _rope_block(x, cos_t, sin_t, upper):
    swapped = jnp.where(upper, pltpu.roll(x, 32, 1), pltpu.roll(x, 96, 1))
    return x * cos_t + swapped * sin_t


def _mid_kernel(x_ref, y_ref, u_ref, ga_ref, ob_ref, d_ref, wglu_ref, bglu_ref, woa_ref, wob_ref,
                g_ref, w_ref, gq_ref, gk_ref, cos_ref, sin_ref, h_ref, q_ref, k_ref, v_ref, gate_ref):
    tm = x_ref.shape[0]
    lane = lax.broadcasted_iota(jnp.int32, (tm, LANES), 1)
    upper = (lane & 32) != 0
    ones4 = _head_ones(MXU_TILE)
    inv_d = 1.0 / SWA_HEAD_DIM
    qw = SWA_HEADS * SWA_HEAD_DIM
    kw = SWA_KV_HEADS * SWA_HEAD_DIM
    nblk = qw // MXU_TILE
    no = y_ref.shape[0]

    y = jnp.concatenate([y_ref[o].astype(F32) for o in range(no)], axis=1)
    u = jnp.concatenate([u_ref[o].astype(F32) for o in range(no)], axis=1)
    z = _gelu_tanh(y + d_ref[...] * u)
    z = z * _sigmoid(_dot(z.astype(BF16), wglu_ref[...]) + bglu_ref[...])
    out_a = z * _silu(ga_ref[...].astype(F32))
    mix = _dot(out_a.astype(BF16), woa_ref[...]) + _dot(ob_ref[...].astype(BF16), wob_ref[...])
    h = x_ref[...] + mix
    h_ref[...] = h

    xb = _rms_rows(h, g_ref[...]).astype(BF16)
    cos_t = cos_ref[...]
    sin_t = sin_ref[...]

    def q_finish(j, q):
        ss = _dot((q * q).astype(BF16), ones4)
        qn = q * lax.rsqrt(ss * inv_d + NORM_EPS) * gq_ref[...]
        for e in range(MXU_TILE // LANES):
            cols = slice(j * MXU_TILE + e * LANES, j * MXU_TILE + (e + 1) * LANES)
            qe = _rope_block(qn[:, e * LANES:(e + 1) * LANES], cos_t, sin_t, upper)
            q_ref[:, cols] = (qe * (SWA_HEAD_DIM ** -0.5 * LOG2E)).astype(q_ref.dtype)

    q_next = _dot(xb, w_ref[:, 0:MXU_TILE])
    for j in range(nblk):
        q_cur = q_next
        if j + 1 < nblk:
            q_next = _dot(xb, w_ref[:, (j + 1) * MXU_TILE:(j + 2) * MXU_TILE])
        else:
            q_next = _dot(xb, w_ref[:, qw:qw + 2 * kw])
        q_finish(j, q_cur)
    kv = q_next
    gate = _dot(xb, w_ref[:, qw + 2 * kw:])
    k = kv[:, :kw]
    ss = _dot((k * k).astype(BF16), ones4[:kw, :kw])
    kn = k * lax.rsqrt(ss * inv_d + NORM_EPS) * gk_ref[...]
    k_ref[...] = _rope_block(kn, cos_t, sin_t, upper)
    v_ref[...] = kv[:, kw:]
    gate_ref[...] = _silu(gate).astype(gate_ref.dtype)


def _mid(x, y, u, ga, ob, pe, po, cos_t, sin_t, act_dtype):
    m, dm = x.shape
    tm = _row_tile(m)
    row = lambda n: pl.BlockSpec((tm, n), lambda i: (i, 0))
    slab = pl.BlockSpec((y.shape[0], tm, LANES), lambda i: (0, i, 0))
    assert cos_t.shape[0] % tm == 0
    nper = cos_t.shape[0] // tm
    tab = pl.BlockSpec((tm, LANES), lambda i: (i % nper, 0))
    qw = SWA_HEADS * SWA_HEAD_DIM
    kw = SWA_KV_HEADS * SWA_HEAD_DIM
    weights = [pe['d'], pe['wglu'], pe['bglu'], pe['woa'], pe['wob'], po['norm_g'], po['w_in'], po['gq'], po['gk']]
    return pl.pallas_call(
        _mid_kernel,
        grid=(m // tm,),
        in_specs=[row(dm), slab, slab, row(512), row(512)] + [_full(w.shape) for w in weights] + [tab, tab],
        out_specs=[row(dm), row(qw), row(kw), row(kw), row(qw)],
        out_shape=[jax.ShapeDtypeStruct((m, dm), F32),
                   jax.ShapeDtypeStruct((m, qw), act_dtype), jax.ShapeDtypeStruct((m, kw), F32),
                   jax.ShapeDtypeStruct((m, kw), F32), jax.ShapeDtypeStruct((m, qw), act_dtype)],
        compiler_params=_cparams(("parallel",)),
        name="mid",
    )(x, y, u, ga, ob, *weights, cos_t, sin_t)


def _rope_tables(pos):
    half = SWA_HEAD_DIM // 2
    inv_freq = ROPE_THETA ** (-np.arange(half, dtype=np.float64) / half)
    ang = np.asarray(pos, np.float64)[:, None] * inv_freq[None, :]
    cos, sin = np.cos(ang), np.sin(ang)
    cos_t = np.concatenate([cos, cos, cos, cos], axis=1).astype(np.float32)
    sin_t = np.concatenate([-sin, sin, -sin, sin], axis=1).astype(np.float32)
    return jnp.asarray(cos_t), jnp.asarray(sin_t)


def _attn_prompt_kernel(sink_ref, q_ref, kc_ref, kp_ref, vc_ref, vp_ref, gate_ref, h_ref, wout_ref,
                        y_ref, p_ref, o_ref, *, nqb):
    i = pl.program_id(1)
    w = SWA_WINDOW
    hd = SWA_HEAD_DIM
    npair = SWA_GROUP // 2
    lane2 = lax.broadcasted_iota(jnp.int32, (2 * w, LANES), 1)
    low = lane2 < hd
    rr = lax.broadcasted_iota(jnp.int32, (w, w), 0)
    cc = lax.broadcasted_iota(jnp.int32, (w, w), 1)
    tri = cc <= rr
    low_w = cc < hd
    r4 = lax.broadcasted_iota(jnp.int32, (4 * w, LANES), 0)
    c4 = lax.broadcasted_iota(jnp.int32, (4 * w, LANES), 1)
    den_cols = ((r4 < 2 * w) == (c4 < hd)).astype(BF16)
    units = [(jb, kv) for jb in range(nqb) for kv in range(SWA_KV_HEADS)]

    def scores(jb, kv):
        rows = slice(jb * w, (jb + 1) * w)
        if jb == 0:
            k_prev, v_prev = kp_ref[...], vp_ref[...]
        else:
            prev_rows = slice((jb - 1) * w, jb * w)
            k_prev, v_prev = kc_ref[prev_rows, :], vc_ref[prev_rows, :]
        kcat = jnp.concatenate([k_prev, kc_ref[rows, :]], axis=0)
        vcat = jnp.concatenate([v_prev, vc_ref[rows, :]], axis=0)
        own = low if kv == 0 else jnp.logical_not(low)
        k_own = jnp.where(own, kcat, 0.0)
        v_own = jnp.where(own, vcat, 0.0)
        k_oth = pltpu.roll(k_own, hd, 1)
        v_oth = pltpu.roll(v_own, hd, 1)
        k_lo, k_hi = (k_own, k_oth) if kv == 0 else (k_oth, k_own)
        v_lo, v_hi = (v_own, v_oth) if kv == 0 else (v_oth, v_own)
        k_rhs = jnp.concatenate([k_lo, k_hi], axis=0).astype(BF16)
        v_rhs = jnp.concatenate([jnp.concatenate([v_lo, v_hi], axis=0).astype(BF16), den_cols], axis=1)
        qs = jnp.concatenate([q_ref[rows, (kv * npair + pr) * LANES:(kv * npair + pr + 1) * LANES]
                              for pr in range(npair)], axis=0)
        return _dot_nt(qs, k_rhs), v_rhs

    def softmax_pv(kv, first, buf, s_all, v_rhs):
        sink_terms = []
        for pr in range(npair):
            prow = slice(pr * w, (pr + 1) * w)
            pair_terms = []
            for e in range(2):
                s_prev = s_all[prow, (2 * e) * w:(2 * e + 1) * w]
                s_cur = s_all[prow, (2 * e + 1) * w:(2 * e + 2) * w]
                if first:
                    s_prev = jnp.where(i > 0, s_prev, -jnp.inf)
                sc = jnp.where(tri, s_cur, s_prev)
                sink = sink_ref[2 * (kv * npair + pr) + e] * LOG2E
                mx = jnp.maximum(jnp.max(sc, axis=-1, keepdims=True), sink)
                pe = jnp.exp2(sc - mx)
                p_ref[buf, prow, (2 * e) * w:(2 * e + 1) * w] = jnp.where(tri, 0.0, pe).astype(BF16)
                p_ref[buf, prow, (2 * e + 1) * w:(2 * e + 2) * w] = jnp.where(tri, pe, 0.0).astype(BF16)
                pair_terms.append(jnp.exp2(sink - mx))
            sink_terms.append(pair_terms)
        return _dot(p_ref[buf], v_rhs), sink_terms

    def normalise(jb, kv, o_ext, sink_terms):
        rows = slice(jb * w, (jb + 1) * w)
        for pr in range(npair):
            prow = slice(pr * w, (pr + 1) * w)
            cols = slice((kv * npair + pr) * LANES, (kv * npair + pr + 1) * LANES)
            st = jnp.where(low_w, sink_terms[pr][0], sink_terms[pr][1])
            o = o_ext[prow, :LANES] / (o_ext[prow, LANES:] + st)
            o_ref[rows, cols] = o.astype(o_ref.dtype)

    def project(rows):
        og = o_ref[rows, :].astype(F32) * gate_ref[rows, :].astype(F32)
        y_ref[rows, :] = h_ref[rows, :] + _dot(og.astype(BF16), wout_ref[...])

    group = 2
    pending = scores(*units[0])
    unfinished = None
    for n, (jb, kv) in enumerate(units):
        current = pending
        if n + 1 < len(units):
            pending = scores(*units[n + 1])
        result = softmax_pv(kv, jb == 0, n % 2, *current)
        if unfinished is not None:
            normalise(*unfinished)
            done_jb, done_kv = unfinished[:2]
            if done_kv == SWA_KV_HEADS - 1 and (done_jb + 1) % group == 0:
                project(slice((done_jb + 1 - group) * w, (done_jb + 1) * w))
        unfinished = (jb, kv) + result
    normalise(*unfinished)
    project(slice((nqb - group) * w, nqb * w))


def _attn_prompt(sinks, q, k, v, gate, h, w_out, nseq, seqlen):
    w = SWA_WINDOW
    nqb = 8
    tm = nqb * w
    assert seqlen % tm == 0
    nblk = seqlen // tm
    qw = q.shape[1]
    dm = h.shape[1]
    row = lambda n: pl.BlockSpec((tm, n), lambda b, i: (b * nblk + i, 0))
    prev = lambda n: pl.BlockSpec((w, n), lambda b, i: (jnp.maximum((b * nblk + i) * nqb - 1, 0), 0))
    smem = pl.BlockSpec(memory_space=pltpu.SMEM)
    return pl.pallas_call(
        functools.partial(_attn_prompt_kernel, nqb=nqb),
        grid=(nseq, nblk),
        in_specs=[smem, row(qw), row(LANES), prev(LANES), row(LANES), prev(LANES), row(qw), row(dm),
                  _full(w_out.shape)],
        out_specs=row(dm),
        out_shape=jax.ShapeDtypeStruct(h.shape, F32),
        scratch_shapes=[pltpu.VMEM((2, 4 * w, 4 * w), BF16), pltpu.VMEM((tm, qw), BF16)],
        compiler_params=_cparams(("parallel", "arbitrary")),
        name="attn_prompt",
    )(sinks, q, k, k, v, v, gate, h, w_out)


def _attn_sample_kernel(sink_ref, q_ref, kn_ref, vn_ref, kc_ref, vc_ref, o_ref, ko_ref, vo_ref, *, seqlen):
    nq = SWA_HEADS * seqlen
    ncache = kc_ref.shape[1]
    t_row = lax.broadcasted_iota(jnp.int32, (nq, ncache), 0) % seqlen
    c_col = lax.broadcasted_iota(jnp.int32, (nq, ncache), 1)
    cache_ok = c_col > t_row - (SWA_WINDOW - ncache)
    nnew = kn_ref.shape[1]
    t_row_n = lax.broadcasted_iota(jnp.int32, (nq, nnew), 0) % seqlen
    n_col = lax.broadcasted_iota(jnp.int32, (nq, nnew), 1)
    new_ok = n_col <= t_row_n
    sink = (sink_ref[...] * LOG2E)[None, :, 0:1]
    bqk = lambda a, b: lax.dot_general(a, b, (((2,), (2,)), ((0,), (0,))), preferred_element_type=F32)
    bpv = lambda a, b: lax.dot_general(a, b, (((2,), (1,)), ((0,), (0,))), preferred_element_type=F32)
    q = q_ref[...]
    kc = kc_ref[...]
    vc = vc_ref[...]
    kn = kn_ref[...]
    vn = vn_ref[...]
    sc = jnp.where(cache_ok[None], bqk(q, kc.astype(BF16)), -jnp.inf)
    sn = jnp.where(new_ok[None], bqk(q, kn.astype(BF16)), -jnp.inf)
    mx = jnp.maximum(jnp.maximum(jnp.max(sc, axis=-1, keepdims=True), jnp.max(sn, axis=-1, keepdims=True)), sink)
    pc = jnp.exp2(sc - mx)
    pn = jnp.exp2(sn - mx)
    den = jnp.sum(pc, axis=-1, keepdims=True) + jnp.sum(pn, axis=-1, keepdims=True) + jnp.exp2(sink - mx)
    inv = 1.0 / den
    o_ref[...] = bpv((pc * inv).astype(BF16), vc.astype(BF16)) + bpv((pn * inv).astype(BF16), vn.astype(BF16))
    keep = ncache - seqlen
    ko_ref[:, 0:keep, :] = kc[:, seqlen:ncache, :]
    ko_ref[:, keep:ncache, :] = kn[:, 0:seqlen, :]
    vo_ref[:, 0:keep, :] = vc[:, seqlen:ncache, :]
    vo_ref[:, keep:ncache, :] = vn[:, 0:seqlen, :]


def _attn_sample(sink_rows, q, kn, vn, kc, vc, seqlen):
    n = q.shape[0]
    nseq = 8
    assert n % nseq == 0
    blk = lambda a: pl.BlockSpec((nseq,) + a.shape[1:], lambda i: (i, 0, 0))
    return pl.pallas_call(
        functools.partial(_attn_sample_kernel, seqlen=seqlen),
        grid=(n // nseq,),
        in_specs=[_full(sink_rows.shape), blk(q), blk(kn), blk(vn), blk(kc), blk(vc)],
        out_specs=[blk(q), blk(kc), blk(vc)],
        out_shape=[jax.ShapeDtypeStruct(q.shape, F32), jax.ShapeDtypeStruct(kc.shape, F32),
                   jax.ShapeDtypeStruct(vc.shape, F32)],
        compiler_params=_cparams(("parallel",)),
        name="attn_sample",
    )(sink_rows, q, kn, vn, kc, vc)


def _odd_out_kernel(h_ref, o_ref, gate_ref, w_ref, y_ref):
    og = o_ref[...].astype(F32) * gate_ref[...].astype(F32)
    y_ref[...] = h_ref[...] + _dot(og.astype(BF16), w_ref[...])


def _odd_out(h, o, gate, w):
    m, dm = h.shape
    tm = _row_tile(m)
    row = lambda n: pl.BlockSpec((tm, n), lambda i: (i, 0))
    return pl.pallas_call(
        _odd_out_kernel,
        grid=(m // tm,),
        in_specs=[row(dm), row(o.shape[1]), row(gate.shape[1]), _full(w.shape)],
        out_specs=row(dm),
        out_shape=jax.ShapeDtypeStruct((m, dm), F32),
        compiler_params=_cparams(("parallel",)),
        name="odd_out",
    )(h, o, gate, w)


def _trunk(x, s5_x0, gla_s0, pe, po, pos, nseq, seqlen, act_dtype):
    u4, u2, ga, q, k, v, la, gb = _even_in(x, pe['norm_g'], pe['wm'], pe['wa'], pe['wgb'], pe['wgate'],
                                           pe['bgate'], act_dtype, _s5_chunk(seqlen))
    y4, s5_fin = _s5_branch(u2, s5_x0, pe['s5_ops'], nseq, seqlen, act_dtype)
    ob, gla_fin = _gla(q, k, v, la, gb, gla_s0, pe['gla_norm_g'], nseq, seqlen, act_dtype)
    cos_t, sin_t = _rope_tables(pos)
    h, q1, k1, v1, gate = _mid(x, y4, u4, ga, ob, pe, po, cos_t, sin_t, act_dtype)
    return h, q1, k1, v1, gate, s5_fin, gla_fin


def kernel(x_prompt, x_sample, state_s5_re, state_s5_im, state_gla, cache_swa_k, cache_swa_v,
           even_norm_g, even_w_in, s5_lambda_re, s5_lambda_im, s5_log_dt, s5_b_re, s5_b_im,
           s5_c_re, s5_c_im, s5_d, s5_w_glu, s5_b_glu, gla_w_gate, gla_b_gate, gla_norm_g,
           even_w_out, odd_norm_g, odd_w_in, swa_q_norm_g, swa_k_norm_g, swa_sinks, odd_w_out):
    nb, seq, dm = x_prompt.shape
    ns, dseq, _ = x_sample.shape
    ng = s5_lambda_re.shape[1]
    no = ng // OCT
    xp = x_prompt.reshape(nb * seq, dm)
    xs = x_sample.reshape(ns * dseq, dm)

    i = 0
    w_in = even_w_in[i]
    s5w = ng * S5_GROUP
    col_alow = 2 * s5w + 2 * GLA_HEADS * GLA_HEAD_K + GLA_HEADS * GLA_HEAD_V
    col_gb = col_alow + GLA_GATE_RANK
    pad_rank = LANES - GLA_GATE_RANK
    pe = {
        'norm_g': even_norm_g[i][None, :],
        'wm': w_in[:, :col_alow].astype(BF16),
        'wa': jnp.pad(w_in[:, col_alow:col_gb], ((0, 0), (0, pad_rank))).astype(BF16),
        'wgb': w_in[:, col_gb:].astype(BF16),
        'wgate': jnp.pad(gla_w_gate[i], ((0, pad_rank), (0, 0))).astype(BF16),
        'bgate': gla_b_gate[i][None, :],
        's5_ops': _s5_params(s5_lambda_re[i], s5_lambda_im[i], s5_log_dt[i], s5_b_re[i], s5_b_im[i],
                             s5_c_re[i], s5_c_im[i]),
        'gla_norm_g': gla_norm_g[i][None, :],
        'd': s5_d[i][None, :],
        'wglu': s5_w_glu[i].astype(BF16),
        'bglu': s5_b_glu[i][None, :],
        'woa': even_w_out[i][:s5w].astype(BF16),
        'wob': even_w_out[i][s5w:].astype(BF16),
    }
    po = {
        'norm_g': odd_norm_g[i][None, :],
        'w_in': odd_w_in[i].astype(BF16),
        'gq': jnp.tile(swa_q_norm_g[i], MXU_TILE // SWA_HEAD_DIM)[None, :],
        'gk': jnp.tile(swa_k_norm_g[i], LANES // SWA_HEAD_DIM)[None, :],
    }
    w_out = odd_w_out[i].astype(BF16)
    sinks = swa_sinks[i]
    kvw = SWA_KV_HEADS * SWA_HEAD_DIM

    s5_zero = jnp.zeros((no, nb, 2 * OCT * S5_STATE), F32)
    gla_zero = jnp.zeros((nb, GLA_HEADS, GLA_HEAD_K, GLA_HEAD_V), F32)
    hp, q, k, v, gate, s5_p, gla_p = _trunk(xp, s5_zero, gla_zero, pe, po, np.arange(seq), nb, seq, BF16)
    s5r_p, s5i_p = _s5_state_out(s5_p)
    y_prompt = _attn_prompt(sinks, q, k, v, gate, hp, w_out, nb, seq).reshape(nb, seq, dm)
    cache_len = min(SWA_WINDOW, seq)
    tail = lambda a: (a.reshape(nb, seq, kvw)[:, seq - cache_len:]
                      .reshape(1, nb, cache_len, SWA_KV_HEADS, SWA_HEAD_DIM))
    swk_p, swv_p = tail(k), tail(v)

    pos_s = np.tile(PAST_LEN + np.arange(dseq), ns)
    s5_init = _s5_state_in(state_s5_re[i], state_s5_im[i], no)
    hs, q, k, v, gate, s5_s, gla_s = _trunk(xs, s5_init, state_gla[i], pe, po, pos_s, ns, dseq, F32)
    s5r_s, s5i_s = _s5_state_out(s5_s)
    ncache = cache_swa_k.shape[2]
    qh = q.reshape(ns, dseq, SWA_KV_HEADS, SWA_GROUP, SWA_HEAD_DIM).transpose(0, 2, 3, 1, 4)
    zq = jnp.zeros_like(qh[:, 0])
    q2 = jnp.stack([jnp.concatenate([qh[:, 0], zq], axis=-1), jnp.concatenate([zq, qh[:, 1]], axis=-1)], axis=1)
    q2 = q2.reshape(ns, SWA_HEADS * dseq, kvw).astype(BF16)
    sink_rows = jnp.broadcast_to(jnp.repeat(sinks, dseq)[:, None], (SWA_HEADS * dseq, LANES))
    kn = k.reshape(ns, dseq, kvw)
    vn = v.reshape(ns, dseq, kvw)
    npad = 16 - dseq
    kn_pad = jnp.pad(kn, ((0, 0), (0, npad), (0, 0)))
    vn_pad = jnp.pad(vn, ((0, 0), (0, npad), (0, 0)))
    kc = cache_swa_k[i].reshape(ns, ncache, kvw)
    vc = cache_swa_v[i].reshape(ns, ncache, kvw)
    o2, kc_new, vc_new = _attn_sample(sink_rows, q2, kn_pad, vn_pad, kc, vc, dseq)
    o2 = o2.reshape(ns, SWA_KV_HEADS, SWA_GROUP, dseq, SWA_KV_HEADS, SWA_HEAD_DIM)
    o = jnp.stack([o2[:, 0, :, :, 0], o2[:, 1, :, :, 1]], axis=1)
    o = o.transpose(0, 3, 1, 2, 4).reshape(ns * dseq, SWA_HEADS * SWA_HEAD_DIM)
    y_sample = _odd_out(hs, o, gate, w_out).reshape(ns, dseq, dm)
    swk_s = kc_new.reshape(1, ns, ncache, SWA_KV_HEADS, SWA_HEAD_DIM)
    swv_s = vc_new.reshape(1, ns, ncache, SWA_KV_HEADS, SWA_HEAD_DIM)

    return (y_prompt, y_sample,
            s5r_p, s5i_p, gla_p[None], swk_p, swv_p,
            s5r_s, s5i_s, gla_s[None], swk_s, swv_s)
```

```python
import functools
import math

import jax
import jax.numpy as jnp
import numpy as np
from jax import lax
from jax.experimental import pallas as pl
from jax.experimental.pallas import tpu as pltpu

F32 = jnp.float32
BF16 = jnp.bfloat16

PAST_LEN = 8192
NORM_EPS = 1e-6
S5_GROUP = 16
S5_STATE = 64
S5_CHUNK = 16
GLA_HEADS = 4
GLA_HEAD_K = 64
GLA_HEAD_V = 128
GLA_GATE_RANK = 16
GLA_GATE_TAU = 16.0
GLA_CHUNK = 64
SWA_HEADS = 16
SWA_KV_HEADS = 2
SWA_GROUP = SWA_HEADS // SWA_KV_HEADS
SWA_HEAD_DIM = 64
SWA_WINDOW = 128
ROPE_THETA = 10000.0
LOG2E = math.log2(math.e)
LANES = 128
MXU_TILE = 256
OCT = LANES // S5_GROUP
ROW_TILE = 512
VMEM_LIMIT = 48 * 1024 * 1024


def _cparams(sem):
    return pltpu.CompilerParams(dimension_semantics=sem, vmem_limit_bytes=VMEM_LIMIT)


def _full(shape):
    n = len(shape)
    return pl.BlockSpec(shape, lambda *_: (0,) * n)


def _dot(a, b):
    return jnp.dot(a, b, preferred_element_type=F32)


def _dot_nt(a, b):
    return lax.dot_general(a, b, (((1,), (1,)), ((), ())), preferred_element_type=F32)


def _dot_tn(a, b):
    return lax.dot_general(a, b, (((0,), (0,)), ((), ())), preferred_element_type=F32)


def _split_bf16(x):
    hi = x.astype(BF16)
    lo = (x - hi.astype(F32)).astype(BF16)
    return hi, lo


def _rms_rows(x, g):
    return x * lax.rsqrt(jnp.mean(x * x, axis=-1, keepdims=True) + NORM_EPS) * g


def _sigmoid(x):
    return 1.0 / (1.0 + jnp.exp(-x))


def _silu(x):
    return x * _sigmoid(x)


def _row_tile(m, tile=ROW_TILE):
    return tile if m % tile == 0 else m


def _even_in_kernel(x_ref, g_ref, wm_ref, wa_ref, wgb_ref, wgate_ref, bgate_ref,
                    u_ref, u2_ref, ga_ref, q_ref, k_ref, v_ref, la_ref, gb_ref, uscr_ref, *, t):
    xb = _rms_rows(x_ref[...], g_ref[...]).astype(BF16)

    def proj(lo, hi):
        return _dot(xb, wm_ref[:, lo:hi])

    u = proj(0, 512)
    nrow = u.shape[0] // t
    for o in range(u_ref.shape[0]):
        uo = u[:, o * LANES:(o + 1) * LANES]
        u_ref[o] = uo.astype(u_ref.dtype)
        uscr_ref[o] = uo
        for tt in range(t):
            piece = uscr_ref[o, pl.ds(tt, nrow, stride=t), :]
            u2_ref[o, :, tt * LANES:(tt + 1) * LANES] = piece.astype(u2_ref.dtype)
    ga_ref[...] = proj(512, 1024).astype(ga_ref.dtype)
    q_ref[...] = (proj(1024, 1280) * (GLA_HEAD_K ** -0.5)).astype(q_ref.dtype)
    k_ref[...] = proj(1280, 1536).astype(k_ref.dtype)
    v_ref[...] = proj(1536, 2048).astype(v_ref.dtype)
    gb_ref[...] = _dot(xb, wgb_ref[...]).astype(gb_ref.dtype)
    a_low = _dot(xb, wa_ref[...])
    logit = _dot(a_low.astype(BF16), wgate_ref[...]) + bgate_ref[...]
    log_sig = jnp.minimum(logit, 0.0) - jnp.log1p(jnp.exp(-jnp.abs(logit)))
    la_ref[...] = log_sig * (1.0 / GLA_GATE_TAU)


def _even_in(x, g, wm, wa, wgb, wgate, bgate, act_dtype, t):
    m, d = x.shape
    tm = _row_tile(m, 2 * ROW_TILE) if m > ROW_TILE else m
    row = lambda n: pl.BlockSpec((tm, n), lambda i: (i, 0))
    no = 512 // LANES
    slab = pl.BlockSpec((no, tm, LANES), lambda i: (0, i, 0))
    chunk = pl.BlockSpec((no, tm // t, t * LANES), lambda i: (0, i, 0))
    outs = [(512, act_dtype), (256, act_dtype), (256, act_dtype), (512, act_dtype), (256, F32), (512, act_dtype)]
    return pl.pallas_call(
        functools.partial(_even_in_kernel, t=t),
        grid=(m // tm,),
        in_specs=[row(d), _full(g.shape), _full(wm.shape), _full(wa.shape), _full(wgb.shape),
                  _full(wgate.shape), _full(bgate.shape)],
        out_specs=[slab, chunk] + [row(n) for n, _ in outs],
        out_shape=[jax.ShapeDtypeStruct((no, m, LANES), act_dtype),
                   jax.ShapeDtypeStruct((no, m // t, t * LANES), act_dtype)]
        + [jax.ShapeDtypeStruct((m, n), dt) for n, dt in outs],
        scratch_shapes=[pltpu.VMEM((no, tm, LANES), F32)],
        compiler_params=_cparams(("parallel",)),
        name="even_in",
    )(x, g, wm, wa, wgb, wgate, bgate)


def _group_mask(shape, row_span, col_span):
    rg = (lax.broadcasted_iota(jnp.int32, shape, 0) // row_span) % OCT
    cg = (lax.broadcasted_iota(jnp.int32, shape, 1) // col_span) % OCT
    return rg == cg


def _s5_state_kernel(u_ref, bre_ref, bim_ref, x0_ref, are_ref, aim_ref, xs_ref, xf_ref, loc_ref, *, nseq, nchunks):
    hw = OCT * S5_STATE
    kk = u_ref.shape[2]
    bp = jnp.concatenate([bre_ref[0]] * OCT + [bim_ref[0]] * OCT, axis=1)
    bp = jnp.where(_group_mask((kk, 2 * hw), S5_GROUP, S5_STATE), bp, 0.0).astype(BF16)
    loc_ref[...] = _dot(u_ref[0], bp)
    a_re = are_ref[0]
    a_im = aim_ref[0]
    if nchunks == 1:
        x0 = x0_ref[0]
        xr, xi = x0[:, :hw], x0[:, hw:]
        loc = loc_ref[...]
        xf_ref[0, :, :hw] = a_re * xr - a_im * xi + loc[:, :hw]
        xf_ref[0, :, hw:] = a_re * xi + a_im * xr + loc[:, hw:]
        xs_ref[0] = x0.astype(xs_ref.dtype)
    else:
        def body(j, carry):
            new = []
            for b in range(nseq):
                xr, xi = carry[b]
                row = pl.ds(b * nchunks + j, 1)
                lr = loc_ref[row, :hw]
                li = loc_ref[row, hw:]
                loc_ref[row, :hw] = xr
                loc_ref[row, hw:] = xi
                new.append((a_re * xr - a_im * xi + lr, a_re * xi + a_im * xr + li))
            return tuple(new)

        init = tuple((x0_ref[0, b:b + 1, :hw], x0_ref[0, b:b + 1, hw:]) for b in range(nseq))
        fin = lax.fori_loop(0, nchunks, body, init, unroll=4)
        for b in range(nseq):
            xf_ref[0, b:b + 1, :hw] = fin[b][0]
            xf_ref[0, b:b + 1, hw:] = fin[b][1]
        xs_ref[0] = loc_ref[...].astype(xs_ref.dtype)


def _s5_state(u2, bre, bim, x0, are, aim, nseq, nchunks):
    no, r, kk = u2.shape
    sw = x0.shape[2]
    blk = lambda a: pl.BlockSpec((1,) + a.shape[1:], lambda o: (o,) + (0,) * (a.ndim - 1))
    return pl.pallas_call(
        functools.partial(_s5_state_kernel, nseq=nseq, nchunks=nchunks),
        grid=(no,),
        in_specs=[blk(u2), blk(bre), blk(bim), blk(x0), blk(are), blk(aim)],
        out_specs=[pl.BlockSpec((1, r, sw), lambda o: (o, 0, 0)), blk(x0)],
        out_shape=[jax.ShapeDtypeStruct((no, r, sw), BF16), jax.ShapeDtypeStruct(x0.shape, F32)],
        scratch_shapes=[pltpu.VMEM((r, sw), F32)],
        compiler_params=_cparams(("parallel",)),
        name="s5_state",
    )(u2, bre, bim, x0, are, aim)


def _dot_nt_f32(a, b):
    a_hi, a_lo = _split_bf16(a)
    b_hi, b_lo = _split_bf16(b)
    return _dot_nt(a_hi, b_hi) + _dot_nt(a_hi, b_lo) + _dot_nt(a_lo, b_hi)


def _s5_out_kernel(u_ref, xs_ref, bre_ref, bim_ref, cre_ref, cim_ref, zre_ref, zim_ref, y_ref, yscr_ref):
    r, kk = u_ref.shape[1], u_ref.shape[2]
    t = kk // LANES
    sw = xs_ref.shape[2]
    ntile = kk // MXU_TILE
    taps = _dot_nt_f32(bre_ref[0], zre_ref[0]) - _dot_nt_f32(bim_ref[0], zim_ref[0])
    tmask = _group_mask((LANES, LANES), S5_GROUP, S5_GROUP)

    def tap(lag):
        if lag < 0:
            return jnp.zeros((LANES, LANES), F32)
        s = t - 1 - lag
        return jnp.where(tmask, taps[s * LANES:(s + 1) * LANES], 0.0)

    wts = [jnp.concatenate([jnp.concatenate([tap(2 * d), tap(2 * d + 1)], axis=1),
                            jnp.concatenate([tap(2 * d - 1), tap(2 * d)], axis=1)], axis=0).astype(BF16)
           for d in range(ntile)]
    cpt = jnp.concatenate([cre_ref[0]] * OCT + [cim_ref[0]] * OCT, axis=1)
    cpt = jnp.where(_group_mask((kk, sw), S5_GROUP, S5_STATE), cpt, 0.0).astype(BF16)
    xs = xs_ref[0]
    for n in range(ntile):
        acc = _dot_nt(xs, cpt[n * MXU_TILE:(n + 1) * MXU_TILE])
        for k in range(n + 1):
            acc = acc + _dot(u_ref[0, :, k * MXU_TILE:(k + 1) * MXU_TILE], wts[n - k])
        for e in range(MXU_TILE // LANES):
            yscr_ref[pl.ds(2 * n + e, r, stride=t), :] = acc[:, e * LANES:(e + 1) * LANES]
    y_ref[0] = yscr_ref[...].astype(y_ref.dtype)


def _s5_out(u2, xs, bre, bim, cre, cim, zre, zim, out_dtype):
    no, r, kk = u2.shape
    m = r * (kk // LANES)
    blk = lambda a: pl.BlockSpec((1,) + a.shape[1:], lambda o: (o,) + (0,) * (a.ndim - 1))
    ops = (u2, xs, bre, bim, cre, cim, zre, zim)
    return pl.pallas_call(
        _s5_out_kernel,
        grid=(no,),
        in_specs=[blk(a) for a in ops],
        out_specs=pl.BlockSpec((1, m, LANES), lambda o: (o, 0, 0)),
        out_shape=jax.ShapeDtypeStruct((no, m, LANES), out_dtype),
        scratch_shapes=[pltpu.VMEM((m, LANES), F32)],
        compiler_params=_cparams(("parallel",)),
        name="s5_out",
    )(*ops)


def _s5_params(lam_re, lam_im, log_dt, b_re, b_im, c_re, c_im):
    t = S5_CHUNK
    ng = lam_re.shape[0]
    no = ng // OCT
    dt = jnp.exp(log_dt)[:, None]
    a = lam_re * dt
    b = lam_im * dt
    n = jnp.arange(t + 1, dtype=F32)[None, :, None]
    mag = jnp.exp(n * a[:, None, :])
    pw_re = mag * jnp.cos(n * b[:, None, :])
    pw_im = mag * jnp.sin(n * b[:, None, :])
    em1_re = jnp.expm1(a) * jnp.cos(b) - 2.0 * jnp.sin(0.5 * b) ** 2
    em1_im = jnp.exp(a) * jnp.sin(b)
    den = lam_re * lam_re + lam_im * lam_im
    z_re = (em1_re * lam_re + em1_im * lam_im) / den
    z_im = (em1_im * lam_re - em1_re * lam_im) / den
    bb_re = z_re[..., None] * b_re - z_im[..., None] * b_im
    bb_im = z_re[..., None] * b_im + z_im[..., None] * b_re
    def rows(w):
        return w.reshape(no, OCT, t, S5_GROUP, S5_STATE).transpose(0, 2, 1, 3, 4).reshape(no, t * LANES, S5_STATE)

    bt_re = bb_re.transpose(0, 2, 1)[:, None]
    bt_im = bb_im.transpose(0, 2, 1)[:, None]
    r_re, r_im = pw_re[:, t - 1::-1, None, :], pw_im[:, t - 1::-1, None, :]
    bre = rows(r_re * bt_re - r_im * bt_im)
    bim = rows(r_re * bt_im + r_im * bt_re)
    o_re, o_im = pw_re[:, 1:, None, :], pw_im[:, 1:, None, :]
    cre = rows(c_re[:, None] * o_re - c_im[:, None] * o_im)
    cim = rows(-(c_re[:, None] * o_im + c_im[:, None] * o_re))
    return bre, bim, cre, cim, c_re.reshape(no, LANES, S5_STATE), c_im.reshape(no, LANES, S5_STATE), pw_re, pw_im


def _s5_chunk(seqlen):
    return math.gcd(seqlen, S5_CHUNK)


def _s5_branch(u2, x0, ops, nseq, seqlen, out_dtype):
    bre, bim, cre, cim, zre, zim, pw_re, pw_im = ops
    no = u2.shape[0]
    t = _s5_chunk(seqlen)
    kk = t * LANES
    nchunks = seqlen // t
    are = pw_re[:, t].reshape(no, 1, OCT * S5_STATE)
    aim = pw_im[:, t].reshape(no, 1, OCT * S5_STATE)
    tail = S5_CHUNK * LANES - kk
    bre, bim = bre[:, tail:], bim[:, tail:]
    xs, xf = _s5_state(u2, bre, bim, x0, are, aim, nseq, nchunks)
    return _s5_out(u2, xs, bre, bim, cre[:, :kk], cim[:, :kk], zre, zim, out_dtype), xf


def _s5_state_in(re, im, no):
    nseq = re.shape[0]
    f = lambda a: a.reshape(nseq, no, OCT * S5_STATE).transpose(1, 0, 2)
    return jnp.concatenate([f(re), f(im)], axis=-1)


def _s5_state_out(xf):
    no, nseq, _ = xf.shape
    hw = OCT * S5_STATE
    f = lambda a: a.transpose(1, 0, 2).reshape(1, nseq, no * OCT, S5_STATE)
    return f(xf[:, :, :hw]), f(xf[:, :, hw:])


def _gla_kernel(q_ref, k_ref, v_ref, la_ref, gb_ref, s0_ref, ng_ref, o_ref, sf_ref, st_ref,
                *, nseq, nchunks, c):
    i = pl.program_id(1)

    @pl.when(i == 0)
    def _():
        st_ref[...] = s0_ref[...]

    nh, hk, hv = GLA_HEADS, GLA_HEAD_K, GLA_HEAD_V
    nch = nseq * nchunks
    tm = nch * c
    iota = lambda shape, d: lax.broadcasted_iota(jnp.int32, shape, d)
    ng = ng_ref[...]

    tb = min(tm, MXU_TILE)
    rt, ct = iota((tb, tb), 0), iota((tb, tb), 1)
    tril = ((rt // c == ct // c) & (rt >= ct)).astype(BF16)
    parts = []
    for r0 in range(0, tm, tb):
        la_hi, la_lo = _split_bf16(la_ref[r0:r0 + tb, :])
        parts.append(_dot(tril, la_hi) + _dot(tril, la_lo))
    bcum = jnp.concatenate(parts, axis=0)
    e_hi, e_lo = _split_bf16(jnp.exp(jnp.concatenate([bcum[(ci + 1) * c - 1:(ci + 1) * c] for ci in range(nch)],
                                                     axis=0)))
    pick = (iota((nch, LANES), 0) == iota((nch, LANES), 1)).astype(BF16)
    dec_t = _dot_tn(e_hi, pick) + _dot_tn(e_lo, pick)
    q_all = q_ref[...].astype(F32)
    k_all = k_ref[...].astype(F32)
    q_dec_all = q_all * jnp.exp(bcum)
    k_dec_all = k_all * jnp.exp(-bcum)

    own_k = iota((nh * c, nh * hk), 0) // c == iota((nh * c, nh * hk), 1) // hk
    own_v = iota((nh * c, nh * hv), 0) // c == iota((nh * c, nh * hv), 1) // hv
    causal = iota((c, nh * c), 1) % c <= iota((c, nh * c), 0)
    zero_v = jnp.zeros((hk, hv), F32)

    def intra(ci):
        rows = slice(ci * c, (ci + 1) * c)
        q_dec = q_dec_all[rows].astype(BF16)
        k_dec = k_dec_all[rows]
        b_c = bcum[rows]
        k_tail = (k_all[rows] * jnp.exp(b_c[c - 1:c] - b_c)).astype(BF16)
        v = v_ref[rows, :].astype(F32)
        k_bd = jnp.where(own_k, jnp.concatenate([k_dec] * nh, axis=0), 0.0).astype(BF16)
        v_bd = jnp.where(own_v, jnp.concatenate([v] * nh, axis=0), 0.0).astype(BF16)
        att = jnp.where(causal, _dot_nt(q_dec, k_bd), 0.0)
        o_intra = _dot(att.astype(BF16), v_bd)
        vb = v.astype(BF16)
        kvs = []
        for h0 in range(0, nh, 2):
            kv2 = _dot_tn(k_tail[:, h0 * hk:(h0 + 2) * hk], vb[:, h0 * hv:(h0 + 2) * hv])
            kvs += [kv2[:hk, :hv], kv2[hk:, hv:]]
        return q_dec, o_intra, kvs

    def carry(ci, sts, q_dec, o_intra, kvs):
        st_bd = jnp.concatenate(
            [jnp.concatenate([zero_v] * h + [sts[h]] + [zero_v] * (nh - 1 - h), axis=1) for h in range(nh)],
            axis=0).astype(BF16)
        o = o_intra + _dot(q_dec, st_bd)
        new = [dec_t[h * hk:(h + 1) * hk, ci:ci + 1] * sts[h] + kvs[h] for h in range(nh)]
        return o, new

    def finish(ci, o):
        rows = slice(ci * c, (ci + 1) * c)
        for h in range(nh):
            vs = slice(h * hv, (h + 1) * hv)
            oh = _rms_rows(o[:, vs], ng) * _silu(gb_ref[rows, vs].astype(F32))
            o_ref[rows, vs] = oh.astype(o_ref.dtype)

    sts = None
    pending = intra(0)
    unfinished = None
    for ci in range(nch):
        s, first, last = ci // nchunks, ci % nchunks == 0, ci % nchunks == nchunks - 1
        current = pending
        if ci + 1 < nch:
            pending = intra(ci + 1)
        if first:
            sts = [st_ref[s, h] for h in range(nh)]
        o, sts = carry(ci, sts, *current)
        if last:
            for h in range(nh):
                st_ref[s, h] = sts[h]
        if unfinished is not None:
            finish(*unfinished)
        unfinished = (ci, o)
    finish(*unfinished)

    @pl.when(i == pl.num_programs(1) - 1)
    def _():
        sf_ref[...] = st_ref[...]


def _gla(q, k, v, la, gb, s0, ng, nseq_total, seqlen, out_dtype):
    c = math.gcd(seqlen, GLA_CHUNK)
    if seqlen >= 8 * c:
        nseq, nchunks = 1, 8
    else:
        nseq, nchunks = 8, seqlen // c
    assert nseq_total % nseq == 0 and seqlen % (nchunks * c) == 0
    nblk = seqlen // (nchunks * c)
    tm = nseq * nchunks * c
    row = lambda n: pl.BlockSpec((tm, n), lambda b, i: (b * nblk + i, 0))
    st_spec = pl.BlockSpec((nseq, GLA_HEADS, GLA_HEAD_K, GLA_HEAD_V), lambda b, i: (b, 0, 0, 0))
    m = q.shape[0]
    return pl.pallas_call(
        functools.partial(_gla_kernel, nseq=nseq, nchunks=nchunks, c=c),
        grid=(nseq_total // nseq, nblk),
        in_specs=[row(256), row(256), row(512), row(256), row(512), st_spec, _full(ng.shape)],
        out_specs=[row(512), st_spec],
        out_shape=[jax.ShapeDtypeStruct((m, 512), out_dtype), jax.ShapeDtypeStruct(s0.shape, F32)],
        scratch_shapes=[pltpu.VMEM((nseq, GLA_HEADS, GLA_HEAD_K, GLA_HEAD_V), F32)],
        compiler_params=_cparams(("parallel", "arbitrary")),
        name="gla",
    )(q, k, v, la, gb, s0, ng)


def _gelu_tanh(x):
    return 0.5 * x * (1.0 + jnp.tanh(math.sqrt(2.0 / math.pi) * (x + 0.044715 * (x * x * x))))


def _head_ones(n):
    r = lax.broadcasted_iota(jnp.int32, (n, n), 0) // SWA_HEAD_DIM
    c = lax.broadcasted_iota(jnp.int32, (n, n), 1) // SWA_HEAD_DIM
    return (r == c).astype(BF16)


def _rope_block(x, cos_t, sin_t, upper):
    swapped = jnp.where(upper, pltpu.roll(x, 32, 1), pltpu.roll(x, 96, 1))
    return x * cos_t + swapped * sin_t


def _mid_kernel(x_ref, y_ref, u_ref, ga_ref, ob_ref, d_ref, wglu_ref, bglu_ref, woa_ref, wob_ref,
                g_ref, w_ref, gq_ref, gk_ref, cos_ref, sin_ref, h_ref, q_ref, k_ref, v_ref, gate_ref):
    tm = x_ref.shape[0]
    lane = lax.broadcasted_iota(jnp.int32, (tm, LANES), 1)
    upper = (lane & 32) != 0
    ones4 = _head_ones(MXU_TILE)
    inv_d = 1.0 / SWA_HEAD_DIM
    qw = SWA_HEADS * SWA_HEAD_DIM
    kw = SWA_KV_HEADS * SWA_HEAD_DIM
    nblk = qw // MXU_TILE
    no = y_ref.shape[0]

    y = jnp.concatenate([y_ref[o].astype(F32) for o in range(no)], axis=1)
    u = jnp.concatenate([u_ref[o].astype(F32) for o in range(no)], axis=1)
    z = _gelu_tanh(y + d_ref[...] * u)
    z = z * _sigmoid(_dot(z.astype(BF16), wglu_ref[...]) + bglu_ref[...])
    out_a = z * _silu(ga_ref[...].astype(F32))
    mix = _dot(out_a.astype(BF16), woa_ref[...]) + _dot(ob_ref[...].astype(BF16), wob_ref[...])
    h = x_ref[...] + mix
    h_ref[...] = h

    xb = _rms_rows(h, g_ref[...]).astype(BF16)
    cos_t = cos_ref[...]
    sin_t = sin_ref[...]

    def q_finish(j, q):
        ss = _dot((q * q).astype(BF16), ones4)
        qn = q * lax.rsqrt(ss * inv_d + NORM_EPS) * gq_ref[...]
        for e in range(MXU_TILE // LANES):
            cols = slice(j * MXU_TILE + e * LANES, j * MXU_TILE + (e + 1) * LANES)
            qe = _rope_block(qn[:, e * LANES:(e + 1) * LANES], cos_t, sin_t, upper)
            q_ref[:, cols] = (qe * (SWA_HEAD_DIM ** -0.5 * LOG2E)).astype(q_ref.dtype)

    q_next = _dot(xb, w_ref[:, 0:MXU_TILE])
    for j in range(nblk):
        q_cur = q_next
        if j + 1 < nblk:
            q_next = _dot(xb, w_ref[:, (j + 1) * MXU_TILE:(j + 2) * MXU_TILE])
        else:
            q_next = _dot(xb, w_ref[:, qw:qw + 2 * kw])
        q_finish(j, q_cur)
    kv = q_next
    gate = _dot(xb, w_ref[:, qw + 2 * kw:])
    k = kv[:, :kw]
    ss = _dot((k * k).astype(BF16), ones4[:kw, :kw])
    kn = k * lax.rsqrt(ss * inv_d + NORM_EPS) * gk_ref[...]
    k_ref[...] = _rope_block(kn, cos_t, sin_t, upper)
    v_ref[...] = kv[:, kw:]
    gate_ref[...] = _silu(gate).astype(gate_ref.dtype)


def _mid(x, y, u, ga, ob, pe, po, cos_t, sin_t, act_dtype):
    m, dm = x.shape
    tm = _row_tile(m)
    row = lambda n: pl.BlockSpec((tm, n), lambda i: (i, 0))
    slab = pl.BlockSpec((y.shape[0], tm, LANES), lambda i: (0, i, 0))
    assert cos_t.shape[0] % tm == 0
    nper = cos_t.shape[0] // tm
    tab = pl.BlockSpec((tm, LANES), lambda i: (i % nper, 0))
    qw = SWA_HEADS * SWA_HEAD_DIM
    kw = SWA_KV_HEADS * SWA_HEAD_DIM
    weights = [pe['d'], pe['wglu'], pe['bglu'], pe['woa'], pe['wob'], po['norm_g'], po['w_in'], po['gq'], po['gk']]
    return pl.pallas_call(
        _mid_kernel,
        grid=(m // tm,),
        in_specs=[row(dm), slab, slab, row(512), row(512)] + [_full(w.shape) for w in weights] + [tab, tab],
        out_specs=[row(dm), row(qw), row(kw), row(kw), row(qw)],
        out_shape=[jax.ShapeDtypeStruct((m, dm), F32),
                   jax.ShapeDtypeStruct((m, qw), act_dtype), jax.ShapeDtypeStruct((m, kw), F32),
                   jax.ShapeDtypeStruct((m, kw), F32), jax.ShapeDtypeStruct((m, qw), act_dtype)],
        compiler_params=_cparams(("parallel",)),
        name="mid",
    )(x, y, u, ga, ob, *weights, cos_t, sin_t)


def _rope_tables(pos):
    half = SWA_HEAD_DIM // 2
    inv_freq = ROPE_THETA ** (-np.arange(half, dtype=np.float64) / half)
    ang = np.asarray(pos, np.float64)[:, None] * inv_freq[None, :]
    cos, sin = np.cos(ang), np.sin(ang)
    cos_t = np.concatenate([cos, cos, cos, cos], axis=1).astype(np.float32)
    sin_t = np.concatenate([-sin, sin, -sin, sin], axis=1).astype(np.float32)
    return jnp.asarray(cos_t), jnp.asarray(sin_t)


def _attn_prompt_kernel(sink_ref, q_ref, kc_ref, kp_ref, vc_ref, vp_ref, gate_ref, h_ref, wout_ref,
                        y_ref, p_ref, o_ref, *, nqb):
    i = pl.program_id(1)
    w = SWA_WINDOW
    hd = SWA_HEAD_DIM
    npair = SWA_GROUP // 2
    lane2 = lax.broadcasted_iota(jnp.int32, (2 * w, LANES), 1)
    low = lane2 < hd
    rr = lax.broadcasted_iota(jnp.int32, (w, w), 0)
    cc = lax.broadcasted_iota(jnp.int32, (w, w), 1)
    tri = cc <= rr
    low_w = cc < hd
    r4 = lax.broadcasted_iota(jnp.int32, (4 * w, LANES), 0)
    c4 = lax.broadcasted_iota(jnp.int32, (4 * w, LANES), 1)
    den_cols = ((r4 < 2 * w) == (c4 < hd)).astype(BF16)
    units = [(jb, kv) for jb in range(nqb) for kv in range(SWA_KV_HEADS)]

    def scores(jb, kv):
        rows = slice(jb * w, (jb + 1) * w)
        if jb == 0:
            k_prev, v_prev = kp_ref[...], vp_ref[...]
        else:
            prev_rows = slice((jb - 1) * w, jb * w)
            k_prev, v_prev = kc_ref[prev_rows, :], vc_ref[prev_rows, :]
        kcat = jnp.concatenate([k_prev, kc_ref[rows, :]], axis=0)
        vcat = jnp.concatenate([v_prev, vc_ref[rows, :]], axis=0)
        own = low if kv == 0 else jnp.logical_not(low)
        k_own = jnp.where(own, kcat, 0.0)
        v_own = jnp.where(own, vcat, 0.0)
        k_oth = pltpu.roll(k_own, hd, 1)
        v_oth = pltpu.roll(v_own, hd, 1)
        k_lo, k_hi = (k_own, k_oth) if kv == 0 else (k_oth, k_own)
        v_lo, v_hi = (v_own, v_oth) if kv == 0 else (v_oth, v_own)
        k_rhs = jnp.concatenate([k_lo, k_hi], axis=0).astype(BF16)
        v_rhs = jnp.concatenate([jnp.concatenate([v_lo, v_hi], axis=0).astype(BF16), den_cols], axis=1)
        qs = jnp.concatenate([q_ref[rows, (kv * npair + pr) * LANES:(kv * npair + pr + 1) * LANES]
                              for pr in range(npair)], axis=0)
        return _dot_nt(qs, k_rhs), v_rhs

    def softmax_pv(kv, first, buf, s_all, v_rhs):
        sink_terms = []
        for pr in range(npair):
            prow = slice(pr * w, (pr + 1) * w)
            pair_terms = []
            for e in range(2):
                s_prev = s_all[prow, (2 * e) * w:(2 * e + 1) * w]
                s_cur = s_all[prow, (2 * e + 1) * w:(2 * e + 2) * w]
                if first:
                    s_prev = jnp.where(i > 0, s_prev, -jnp.inf)
                sc = jnp.where(tri, s_cur, s_prev)
                sink = sink_ref[2 * (kv * npair + pr) + e] * LOG2E
                mx = jnp.maximum(jnp.max(sc, axis=-1, keepdims=True), sink)
                pe = jnp.exp2(sc - mx)
                p_ref[buf, prow, (2 * e) * w:(2 * e + 1) * w] = jnp.where(tri, 0.0, pe).astype(BF16)
                p_ref[buf, prow, (2 * e + 1) * w:(2 * e + 2) * w] = jnp.where(tri, pe, 0.0).astype(BF16)
                pair_terms.append(jnp.exp2(sink - mx))
            sink_terms.append(pair_terms)
        return _dot(p_ref[buf], v_rhs), sink_terms

    def normalise(jb, kv, o_ext, sink_terms):
        rows = slice(jb * w, (jb + 1) * w)
        for pr in range(npair):
            prow = slice(pr * w, (pr + 1) * w)
            cols = slice((kv * npair + pr) * LANES, (kv * npair + pr + 1) * LANES)
            st = jnp.where(low_w, sink_terms[pr][0], sink_terms[pr][1])
            o = o_ext[prow, :LANES] / (o_ext[prow, LANES:] + st)
            o_ref[rows, cols] = o.astype(o_ref.dtype)

    def project(rows):
        og = o_ref[rows, :].astype(F32) * gate_ref[rows, :].astype(F32)
        y_ref[rows, :] = h_ref[rows, :] + _dot(og.astype(BF16), wout_ref[...])

    group = 2
    pending = scores(*units[0])
    unfinished = None
    for n, (jb, kv) in enumerate(units):
        current = pending
        if n + 1 < len(units):
            pending = scores(*units[n + 1])
        result = softmax_pv(kv, jb == 0, n % 2, *current)
        if unfinished is not None:
            normalise(*unfinished)
            done_jb, done_kv = unfinished[:2]
            if done_kv == SWA_KV_HEADS - 1 and (done_jb + 1) % group == 0:
                project(slice((done_jb + 1 - group) * w, (done_jb + 1) * w))
        unfinished = (jb, kv) + result
    normalise(*unfinished)
    project(slice((nqb - group) * w, nqb * w))


def _attn_prompt(sinks, q, k, v, gate, h, w_out, nseq, seqlen):
    w = SWA_WINDOW
    nqb = 8
    tm = nqb * w
    assert seqlen % tm == 0
    nblk = seqlen // tm
    qw = q.shape[1]
    dm = h.shape[1]
    row = lambda n: pl.BlockSpec((tm, n), lambda b, i: (b * nblk + i, 0))
    prev = lambda n: pl.BlockSpec((w, n), lambda b, i: (jnp.maximum((b * nblk + i) * nqb - 1, 0), 0))
    smem = pl.BlockSpec(memory_space=pltpu.SMEM)
    return pl.pallas_call(
        functools.partial(_attn_prompt_kernel, nqb=nqb),
        grid=(nseq, nblk),
        in_specs=[smem, row(qw), row(LANES), prev(LANES), row(LANES), prev(LANES), row(qw), row(dm),
                  _full(w_out.shape)],
        out_specs=row(dm),
        out_shape=jax.ShapeDtypeStruct(h.shape, F32),
        scratch_shapes=[pltpu.VMEM((2, 4 * w, 4 * w), BF16), pltpu.VMEM((tm, qw), BF16)],
        compiler_params=_cparams(("parallel", "arbitrary")),
        name="attn_prompt",
    )(sinks, q, k, k, v, v, gate, h, w_out)


def _attn_sample_kernel(sink_ref, q_ref, kn_ref, vn_ref, kc_ref, vc_ref, o_ref, ko_ref, vo_ref, *, seqlen):
    nq = SWA_HEADS * seqlen
    ncache = kc_ref.shape[1]
    t_row = lax.broadcasted_iota(jnp.int32, (nq, ncache), 0) % seqlen
    c_col = lax.broadcasted_iota(jnp.int32, (nq, ncache), 1)
    cache_ok = c_col > t_row - (SWA_WINDOW - ncache)
    nnew = kn_ref.shape[1]
    t_row_n = lax.broadcasted_iota(jnp.int32, (nq, nnew), 0) % seqlen
    n_col = lax.broadcasted_iota(jnp.int32, (nq, nnew), 1)
    new_ok = n_col <= t_row_n
    sink = (sink_ref[...] * LOG2E)[None, :, 0:1]
    bqk = lambda a, b: lax.dot_general(a, b, (((2,), (2,)), ((0,), (0,))), preferred_element_type=F32)
    bpv = lambda a, b: lax.dot_general(a, b, (((2,), (1,)), ((0,), (0,))), preferred_element_type=F32)
    q = q_ref[...]
    kc = kc_ref[...]
    vc = vc_ref[...]
    kn = kn_ref[...]
    vn = vn_ref[...]
    sc = jnp.where(cache_ok[None], bqk(q, kc.astype(BF16)), -jnp.inf)
    sn = jnp.where(new_ok[None], bqk(q, kn.astype(BF16)), -jnp.inf)
    mx = jnp.maximum(jnp.maximum(jnp.max(sc, axis=-1, keepdims=True), jnp.max(sn, axis=-1, keepdims=True)), sink)
    pc = jnp.exp2(sc - mx)
    pn = jnp.exp2(sn - mx)
    den = jnp.sum(pc, axis=-1, keepdims=True) + jnp.sum(pn, axis=-1, keepdims=True) + jnp.exp2(sink - mx)
    inv = 1.0 / den
    o_ref[...] = bpv((pc * inv).astype(BF16), vc.astype(BF16)) + bpv((pn * inv).astype(BF16), vn.astype(BF16))
    keep = ncache - seqlen
    ko_ref[:, 0:keep, :] = kc[:, seqlen:ncache, :]
    ko_ref[:, keep:ncache, :] = kn[:, 0:seqlen, :]
    vo_ref[:, 0:keep, :] = vc[:, seqlen:ncache, :]
    vo_ref[:, keep:ncache, :] = vn[:, 0:seqlen, :]


def _attn_sample(sink_rows, q, kn, vn, kc, vc, seqlen):
    n = q.shape[0]
    nseq = 16
    assert n % nseq == 0
    blk = lambda a: pl.BlockSpec((nseq,) + a.shape[1:], lambda i: (i, 0, 0))
    return pl.pallas_call(
        functools.partial(_attn_sample_kernel, seqlen=seqlen),
        grid=(n // nseq,),
        in_specs=[_full(sink_rows.shape), blk(q), blk(kn), blk(vn), blk(kc), blk(vc)],
        out_specs=[blk(q), blk(kc), blk(vc)],
        out_shape=[jax.ShapeDtypeStruct(q.shape, F32), jax.ShapeDtypeStruct(kc.shape, F32),
                   jax.ShapeDtypeStruct(vc.shape, F32)],
        compiler_params=_cparams(("parallel",)),
        name="attn_sample",
    )(sink_rows, q, kn, vn, kc, vc)


def _odd_out_kernel(h_ref, o_ref, gate_ref, w_ref, y_ref):
    og = o_ref[...].astype(F32) * gate_ref[...].astype(F32)
    y_ref[...] = h_ref[...] + _dot(og.astype(BF16), w_ref[...])


def _odd_out(h, o, gate, w):
    m, dm = h.shape
    tm = _row_tile(m)
    row = lambda n: pl.BlockSpec((tm, n), lambda i: (i, 0))
    return pl.pallas_call(
        _odd_out_kernel,
        grid=(m // tm,),
        in_specs=[row(dm), row(o.shape[1]), row(gate.shape[1]), _full(w.shape)],
        out_specs=row(dm),
        out_shape=jax.ShapeDtypeStruct((m, dm), F32),
        compiler_params=_cparams(("parallel",)),
        name="odd_out",
    )(h, o, gate, w)


def _trunk(x, s5_x0, gla_s0, pe, po, pos, nseq, seqlen, act_dtype):
    u4, u2, ga, q, k, v, la, gb = _even_in(x, pe['norm_g'], pe['wm'], pe['wa'], pe['wgb'], pe['wgate'],
                                           pe['bgate'], act_dtype, _s5_chunk(seqlen))
    y4, s5_fin = _s5_branch(u2, s5_x0, pe['s5_ops'], nseq, seqlen, act_dtype)
    ob, gla_fin = _gla(q, k, v, la, gb, gla_s0, pe['gla_norm_g'], nseq, seqlen, act_dtype)
    cos_t, sin_t = _rope_tables(pos)
    h, q1, k1, v1, gate = _mid(x, y4, u4, ga, ob, pe, po, cos_t, sin_t, act_dtype)
    return h, q1, k1, v1, gate, s5_fin, gla_fin


def kernel(x_prompt, x_sample, state_s5_re, state_s5_im, state_gla, cache_swa_k, cache_swa_v,
           even_norm_g, even_w_in, s5_lambda_re, s5_lambda_im, s5_log_dt, s5_b_re, s5_b_im,
           s5_c_re, s5_c_im, s5_d, s5_w_glu, s5_b_glu, gla_w_gate, gla_b_gate, gla_norm_g,
           even_w_out, odd_norm_g, odd_w_in, swa_q_norm_g, swa_k_norm_g, swa_sinks, odd_w_out):
    nb, seq, dm = x_prompt.shape
    ns, dseq, _ = x_sample.shape
    ng = s5_lambda_re.shape[1]
    no = ng // OCT
    xp = x_prompt.reshape(nb * seq, dm)
    xs = x_sample.reshape(ns * dseq, dm)

    i = 0
    w_in = even_w_in[i]
    s5w = ng * S5_GROUP
    col_alow = 2 * s5w + 2 * GLA_HEADS * GLA_HEAD_K + GLA_HEADS * GLA_HEAD_V
    col_gb = col_alow + GLA_GATE_RANK
    pad_rank = LANES - GLA_GATE_RANK
    pe = {
        'norm_g': even_norm_g[i][None, :],
        'wm': w_in[:, :col_alow].astype(BF16),
        'wa': jnp.pad(w_in[:, col_alow:col_gb], ((0, 0), (0, pad_rank))).astype(BF16),
        'wgb': w_in[:, col_gb:].astype(BF16),
        'wgate': jnp.pad(gla_w_gate[i], ((0, pad_rank), (0, 0))).astype(BF16),
        'bgate': gla_b_gate[i][None, :],
        's5_ops': _s5_params(s5_lambda_re[i], s5_lambda_im[i], s5_log_dt[i], s5_b_re[i], s5_b_im[i],
                             s5_c_re[i], s5_c_im[i]),
        'gla_norm_g': gla_norm_g[i][None, :],
        'd': s5_d[i][None, :],
        'wglu': s5_w_glu[i].astype(BF16),
        'bglu': s5_b_glu[i][None, :],
        'woa': even_w_out[i][:s5w].astype(BF16),
        'wob': even_w_out[i][s5w:].astype(BF16),
    }
    po = {
        'norm_g': odd_norm_g[i][None, :],
        'w_in': odd_w_in[i].astype(BF16),
        'gq': jnp.tile(swa_q_norm_g[i], MXU_TILE // SWA_HEAD_DIM)[None, :],
        'gk': jnp.tile(swa_k_norm_g[i], LANES // SWA_HEAD_DIM)[None, :],
    }
    w_out = odd_w_out[i].astype(BF16)
    sinks = swa_sinks[i]
    kvw = SWA_KV_HEADS * SWA_HEAD_DIM

    s5_zero = jnp.zeros((no, nb, 2 * OCT * S5_STATE), F32)
    gla_zero = jnp.zeros((nb, GLA_HEADS, GLA_HEAD_K, GLA_HEAD_V), F32)
    hp, q, k, v, gate, s5_p, gla_p = _trunk(xp, s5_zero, gla_zero, pe, po, np.arange(seq), nb, seq, BF16)
    s5r_p, s5i_p = _s5_state_out(s5_p)
    y_prompt = _attn_prompt(sinks, q, k, v, gate, hp, w_out, nb, seq).reshape(nb, seq, dm)
    cache_len = min(SWA_WINDOW, seq)
    tail = lambda a: (a.reshape(nb, seq, kvw)[:, seq - cache_len:]
                      .reshape(1, nb, cache_len, SWA_KV_HEADS, SWA_HEAD_DIM))
    swk_p, swv_p = tail(k), tail(v)

    pos_s = np.tile(PAST_LEN + np.arange(dseq), ns)
    s5_init = _s5_state_in(state_s5_re[i], state_s5_im[i], no)
    hs, q, k, v, gate, s5_s, gla_s = _trunk(xs, s5_init, state_gla[i], pe, po, pos_s, ns, dseq, F32)
    s5r_s, s5i_s = _s5_state_out(s5_s)
    ncache = cache_swa_k.shape[2]
    qh = q.reshape(ns, dseq, SWA_KV_HEADS, SWA_GROUP, SWA_HEAD_DIM).transpose(0, 2, 3, 1, 4)
    zq = jnp.zeros_like(qh[:, 0])
    q2 = jnp.stack([jnp.concatenate([qh[:, 0], zq], axis=-1), jnp.concatenate([zq, qh[:, 1]], axis=-1)], axis=1)
    q2 = q2.reshape(ns, SWA_HEADS * dseq, kvw).astype(BF16)
    sink_rows = jnp.broadcast_to(jnp.repeat(sinks, dseq)[:, None], (SWA_HEADS * dseq, LANES))
    kn = k.reshape(ns, dseq, kvw)
    vn = v.reshape(ns, dseq, kvw)
    npad = 16 - dseq
    kn_pad = jnp.pad(kn, ((0, 0), (0, npad), (0, 0)))
    vn_pad = jnp.pad(vn, ((0, 0), (0, npad), (0, 0)))
    kc = cache_swa_k[i].reshape(ns, ncache, kvw)
    vc = cache_swa_v[i].reshape(ns, ncache, kvw)
    o2, kc_new, vc_new = _attn_sample(sink_rows, q2, kn_pad, vn_pad, kc, vc, dseq)
    o2 = o2.reshape(ns, SWA_KV_HEADS, SWA_GROUP, dseq, SWA_KV_HEADS, SWA_HEAD_DIM)
    o = jnp.stack([o2[:, 0, :, :, 0], o2[:, 1, :, :, 1]], axis=1)
    o = o.transpose(0, 3, 1, 2, 4).reshape(ns * dseq, SWA_HEADS * SWA_HEAD_DIM)
    y_sample = _odd_out(hs, o, gate, w_out).reshape(ns, dseq, dm)
    swk_s = kc_new.reshape(1, ns, ncache, SWA_KV_HEADS, SWA_HEAD_DIM)
    swv_s = vc_new.reshape(1, ns, ncache, SWA_KV_HEADS, SWA_HEAD_DIM)

    return (y_prompt, y_sample,
            s5r_p, s5i_p, gla_p[None], swk_p, swv_p,
            s5r_s, s5i_s, gla_s[None], swk_s, swv_s)
```

```python
import functools
import math

import jax
import jax.numpy as jnp
import numpy as np
from jax import lax
from jax.experimental import pallas as pl
from jax.experimental.pallas import tpu as pltpu

F32 = jnp.float32
BF16 = jnp.bfloat16

PAST_LEN = 8192
NORM_EPS = 1e-6
S5_GROUP = 16
S5_STATE = 64
S5_CHUNK = 16
GLA_HEADS = 4
GLA_HEAD_K = 64
GLA_HEAD_V = 128
GLA_GATE_RANK = 16
GLA_GATE_TAU = 16.0
GLA_CHUNK = 64
SWA_HEADS = 16
SWA_KV_HEADS = 2
SWA_GROUP = SWA_HEADS // SWA_KV_HEADS
SWA_HEAD_DIM = 64
SWA_WINDOW = 128
ROPE_THETA = 10000.0
LOG2E = math.log2(math.e)
LANES = 128
MXU_TILE = 256
OCT = LANES // S5_GROUP
ROW_TILE = 512
VMEM_LIMIT = 48 * 1024 * 1024


def _cparams(sem):
    return pltpu.CompilerParams(dimension_semantics=sem, vmem_limit_bytes=VMEM_LIMIT)


def _full(shape):
    n = len(shape)
    return pl.BlockSpec(shape, lambda *_: (0,) * n)


def _dot(a, b):
    return jnp.dot(a, b, preferred_element_type=F32)


def _dot_nt(a, b):
    return lax.dot_general(a, b, (((1,), (1,)), ((), ())), preferred_element_type=F32)


def _dot_tn(a, b):
    return lax.dot_general(a, b, (((0,), (0,)), ((), ())), preferred_element_type=F32)


def _split_bf16(x):
    hi = x.astype(BF16)
    lo = (x - hi.astype(F32)).astype(BF16)
    return hi, lo


def _rms_rows(x, g):
    return x * lax.rsqrt(jnp.mean(x * x, axis=-1, keepdims=True) + NORM_EPS) * g


def _sigmoid(x):
    return 1.0 / (1.0 + jnp.exp(-x))


def _silu(x):
    return x * _sigmoid(x)


def _row_tile(m, tile=ROW_TILE):
    return tile if m % tile == 0 else m


def _even_in_kernel(x_ref, g_ref, wm_ref, wa_ref, wgb_ref, wgate_ref, bgate_ref,
                    u_ref, u2_ref, ga_ref, q_ref, k_ref, v_ref, la_ref, gb_ref, uscr_ref, *, t):
    xb = _rms_rows(x_ref[...], g_ref[...]).astype(BF16)

    def proj(lo, hi):
        return _dot(xb, wm_ref[:, lo:hi])

    u = proj(0, 512)
    nrow = u.shape[0] // t
    for o in range(u_ref.shape[0]):
        uo = u[:, o * LANES:(o + 1) * LANES]
        u_ref[o] = uo.astype(u_ref.dtype)
        uscr_ref[o] = uo
        for tt in range(t):
            piece = uscr_ref[o, pl.ds(tt, nrow, stride=t), :]
            u2_ref[o, :, tt * LANES:(tt + 1) * LANES] = piece.astype(u2_ref.dtype)
    ga_ref[...] = proj(512, 1024).astype(ga_ref.dtype)
    q_ref[...] = (proj(1024, 1280) * (GLA_HEAD_K ** -0.5)).astype(q_ref.dtype)
    k_ref[...] = proj(1280, 1536).astype(k_ref.dtype)
    v_ref[...] = proj(1536, 2048).astype(v_ref.dtype)
    gb_ref[...] = _dot(xb, wgb_ref[...]).astype(gb_ref.dtype)
    a_low = _dot(xb, wa_ref[...])
    logit = _dot(a_low.astype(BF16), wgate_ref[...]) + bgate_ref[...]
    log_sig = jnp.minimum(logit, 0.0) - jnp.log1p(jnp.exp(-jnp.abs(logit)))
    la_ref[...] = log_sig * (1.0 / GLA_GATE_TAU)


def _even_in(x, g, wm, wa, wgb, wgate, bgate, act_dtype, t):
    m, d = x.shape
    tm = _row_tile(m, 2 * ROW_TILE) if m > ROW_TILE else m
    row = lambda n: pl.BlockSpec((tm, n), lambda i: (i, 0))
    no = 512 // LANES
    slab = pl.BlockSpec((no, tm, LANES), lambda i: (0, i, 0))
    chunk = pl.BlockSpec((no, tm // t, t * LANES), lambda i: (0, i, 0))
    outs = [(512, act_dtype), (256, act_dtype), (256, act_dtype), (512, act_dtype), (256, F32), (512, act_dtype)]
    return pl.pallas_call(
        functools.partial(_even_in_kernel, t=t),
        grid=(m // tm,),
        in_specs=[row(d), _full(g.shape), _full(wm.shape), _full(wa.shape), _full(wgb.shape),
                  _full(wgate.shape), _full(bgate.shape)],
        out_specs=[slab, chunk] + [row(n) for n, _ in outs],
        out_shape=[jax.ShapeDtypeStruct((no, m, LANES), act_dtype),
                   jax.ShapeDtypeStruct((no, m // t, t * LANES), act_dtype)]
        + [jax.ShapeDtypeStruct((m, n), dt) for n, dt in outs],
        scratch_shapes=[pltpu.VMEM((no, tm, LANES), F32)],
        compiler_params=_cparams(("parallel",)),
        name="even_in",
    )(x, g, wm, wa, wgb, wgate, bgate)


def _group_mask(shape, row_span, col_span):
    rg = (lax.broadcasted_iota(jnp.int32, shape, 0) // row_span) % OCT
    cg = (lax.broadcasted_iota(jnp.int32, shape, 1) // col_span) % OCT
    return rg == cg


def _s5_state_kernel(u_ref, bre_ref, bim_ref, x0_ref, are_ref, aim_ref, xs_ref, xf_ref, loc_ref, *, nseq, nchunks):
    hw = OCT * S5_STATE
    kk = u_ref.shape[2]
    bp = jnp.concatenate([bre_ref[0]] * OCT + [bim_ref[0]] * OCT, axis=1)
    bp = jnp.where(_group_mask((kk, 2 * hw), S5_GROUP, S5_STATE), bp, 0.0).astype(BF16)
    loc_ref[...] = _dot(u_ref[0], bp)
    a_re = are_ref[0]
    a_im = aim_ref[0]
    if nchunks == 1:
        x0 = x0_ref[0]
        xr, xi = x0[:, :hw], x0[:, hw:]
        loc = loc_ref[...]
        xf_ref[0, :, :hw] = a_re * xr - a_im * xi + loc[:, :hw]
        xf_ref[0, :, hw:] = a_re * xi + a_im * xr + loc[:, hw:]
        xs_ref[0] = x0.astype(xs_ref.dtype)
    else:
        def body(j, carry):
            new = []
            for b in range(nseq):
                xr, xi = carry[b]
                row = pl.ds(b * nchunks + j, 1)
                lr = loc_ref[row, :hw]
                li = loc_ref[row, hw:]
                loc_ref[row, :hw] = xr
                loc_ref[row, hw:] = xi
                new.append((a_re * xr - a_im * xi + lr, a_re * xi + a_im * xr + li))
            return tuple(new)

        init = tuple((x0_ref[0, b:b + 1, :hw], x0_ref[0, b:b + 1, hw:]) for b in range(nseq))
        fin = lax.fori_loop(0, nchunks, body, init, unroll=4)
        for b in range(nseq):
            xf_ref[0, b:b + 1, :hw] = fin[b][0]
            xf_ref[0, b:b + 1, hw:] = fin[b][1]
        xs_ref[0] = loc_ref[...].astype(xs_ref.dtype)


def _s5_state(u2, bre, bim, x0, are, aim, nseq, nchunks):
    no, r, kk = u2.shape
    sw = x0.shape[2]
    blk = lambda a: pl.BlockSpec((1,) + a.shape[1:], lambda o: (o,) + (0,) * (a.ndim - 1))
    return pl.pallas_call(
        functools.partial(_s5_state_kernel, nseq=nseq, nchunks=nchunks),
        grid=(no,),
        in_specs=[blk(u2), blk(bre), blk(bim), blk(x0), blk(are), blk(aim)],
        out_specs=[pl.BlockSpec((1, r, sw), lambda o: (o, 0, 0)), blk(x0)],
        out_shape=[jax.ShapeDtypeStruct((no, r, sw), BF16), jax.ShapeDtypeStruct(x0.shape, F32)],
        scratch_shapes=[pltpu.VMEM((r, sw), F32)],
        compiler_params=_cparams(("parallel",)),
        name="s5_state",
    )(u2, bre, bim, x0, are, aim)


def _dot_nt_f32(a, b):
    a_hi, a_lo = _split_bf16(a)
    b_hi, b_lo = _split_bf16(b)
    return _dot_nt(a_hi, b_hi) + _dot_nt(a_hi, b_lo) + _dot_nt(a_lo, b_hi)


def _s5_out_kernel(u_ref, xs_ref, bre_ref, bim_ref, cre_ref, cim_ref, zre_ref, zim_ref, y_ref, yscr_ref):
    r, kk = u_ref.shape[1], u_ref.shape[2]
    t = kk // LANES
    sw = xs_ref.shape[2]
    ntile = kk // MXU_TILE
    taps = _dot_nt_f32(bre_ref[0], zre_ref[0]) - _dot_nt_f32(bim_ref[0], zim_ref[0])
    tmask = _group_mask((LANES, LANES), S5_GROUP, S5_GROUP)

    def tap(lag):
        if lag < 0:
            return jnp.zeros((LANES, LANES), F32)
        s = t - 1 - lag
        return jnp.where(tmask, taps[s * LANES:(s + 1) * LANES], 0.0)

    wts = [jnp.concatenate([jnp.concatenate([tap(2 * d), tap(2 * d + 1)], axis=1),
                            jnp.concatenate([tap(2 * d - 1), tap(2 * d)], axis=1)], axis=0).astype(BF16)
           for d in range(ntile)]
    cpt = jnp.concatenate([cre_ref[0]] * OCT + [cim_ref[0]] * OCT, axis=1)
    cpt = jnp.where(_group_mask((kk, sw), S5_GROUP, S5_STATE), cpt, 0.0).astype(BF16)
    xs = xs_ref[0]
    for n in range(ntile):
        acc = _dot_nt(xs, cpt[n * MXU_TILE:(n + 1) * MXU_TILE])
        for k in range(n + 1):
            acc = acc + _dot(u_ref[0, :, k * MXU_TILE:(k + 1) * MXU_TILE], wts[n - k])
        for e in range(MXU_TILE // LANES):
            yscr_ref[pl.ds(2 * n + e, r, stride=t), :] = acc[:, e * LANES:(e + 1) * LANES]
    y_ref[0] = yscr_ref[...].astype(y_ref.dtype)


def _s5_out(u2, xs, bre, bim, cre, cim, zre, zim, out_dtype):
    no, r, kk = u2.shape
    m = r * (kk // LANES)
    blk = lambda a: pl.BlockSpec((1,) + a.shape[1:], lambda o: (o,) + (0,) * (a.ndim - 1))
    ops = (u2, xs, bre, bim, cre, cim, zre, zim)
    return pl.pallas_call(
        _s5_out_kernel,
        grid=(no,),
        in_specs=[blk(a) for a in ops],
        out_specs=pl.BlockSpec((1, m, LANES), lambda o: (o, 0, 0)),
        out_shape=jax.ShapeDtypeStruct((no, m, LANES), out_dtype),
        scratch_shapes=[pltpu.VMEM((m, LANES), F32)],
        compiler_params=_cparams(("parallel",)),
        name="s5_out",
    )(*ops)


def _s5_params(lam_re, lam_im, log_dt, b_re, b_im, c_re, c_im):
    t = S5_CHUNK
    ng = lam_re.shape[0]
    no = ng // OCT
    dt = jnp.exp(log_dt)[:, None]
    a = lam_re * dt
    b = lam_im * dt
    n = jnp.arange(t + 1, dtype=F32)[None, :, None]
    mag = jnp.exp(n * a[:, None, :])
    pw_re = mag * jnp.cos(n * b[:, None, :])
    pw_im = mag * jnp.sin(n * b[:, None, :])
    em1_re = jnp.expm1(a) * jnp.cos(b) - 2.0 * jnp.sin(0.5 * b) ** 2
    em1_im = jnp.exp(a) * jnp.sin(b)
    den = lam_re * lam_re + lam_im * lam_im
    z_re = (em1_re * lam_re + em1_im * lam_im) / den
    z_im = (em1_im * lam_re - em1_re * lam_im) / den
    bb_re = z_re[..., None] * b_re - z_im[..., None] * b_im
    bb_im = z_re[..., None] * b_im + z_im[..., None] * b_re
    def rows(w):
        return w.reshape(no, OCT, t, S5_GROUP, S5_STATE).transpose(0, 2, 1, 3, 4).reshape(no, t * LANES, S5_STATE)

    bt_re = bb_re.transpose(0, 2, 1)[:, None]
    bt_im = bb_im.transpose(0, 2, 1)[:, None]
    r_re, r_im = pw_re[:, t - 1::-1, None, :], pw_im[:, t - 1::-1, None, :]
    bre = rows(r_re * bt_re - r_im * bt_im)
    bim = rows(r_re * bt_im + r_im * bt_re)
    o_re, o_im = pw_re[:, 1:, None, :], pw_im[:, 1:, None, :]
    cre = rows(c_re[:, None] * o_re - c_im[:, None] * o_im)
    cim = rows(-(c_re[:, None] * o_im + c_im[:, None] * o_re))
    return bre, bim, cre, cim, c_re.reshape(no, LANES, S5_STATE), c_im.reshape(no, LANES, S5_STATE), pw_re, pw_im


def _s5_chunk(seqlen):
    return math.gcd(seqlen, S5_CHUNK)


def _s5_branch(u2, x0, ops, nseq, seqlen, out_dtype):
    bre, bim, cre, cim, zre, zim, pw_re, pw_im = ops
    no = u2.shape[0]
    t = _s5_chunk(seqlen)
    kk = t * LANES
    nchunks = seqlen // t
    are = pw_re[:, t].reshape(no, 1, OCT * S5_STATE)
    aim = pw_im[:, t].reshape(no, 1, OCT * S5_STATE)
    tail = S5_CHUNK * LANES - kk
    bre, bim = bre[:, tail:], bim[:, tail:]
    xs, xf = _s5_state(u2, bre, bim, x0, are, aim, nseq, nchunks)
    return _s5_out(u2, xs, bre, bim, cre[:, :kk], cim[:, :kk], zre, zim, out_dtype), xf


def _s5_state_in(re, im, no):
    nseq = re.shape[0]
    f = lambda a: a.reshape(nseq, no, OCT * S5_STATE).transpose(1, 0, 2)
    return jnp.concatenate([f(re), f(im)], axis=-1)


def _s5_state_out(xf):
    no, nseq, _ = xf.shape
    hw = OCT * S5_STATE
    f = lambda a: a.transpose(1, 0, 2).reshape(1, nseq, no * OCT, S5_STATE)
    return f(xf[:, :, :hw]), f(xf[:, :, hw:])


def _gla_kernel(q_ref, k_ref, v_ref, la_ref, gb_ref, s0_ref, ng_ref, o_ref, sf_ref, st_ref,
                *, nseq, nchunks, c):
    i = pl.program_id(1)

    @pl.when(i == 0)
    def _():
        st_ref[...] = s0_ref[...]

    nh, hk, hv = GLA_HEADS, GLA_HEAD_K, GLA_HEAD_V
    nch = nseq * nchunks
    tm = nch * c
    iota = lambda shape, d: lax.broadcasted_iota(jnp.int32, shape, d)
    ng = ng_ref[...]

    tb = min(tm, MXU_TILE)
    rt, ct = iota((tb, tb), 0), iota((tb, tb), 1)
    tril = ((rt // c == ct // c) & (rt >= ct)).astype(BF16)
    parts = []
    for r0 in range(0, tm, tb):
        la_hi, la_lo = _split_bf16(la_ref[r0:r0 + tb, :])
        parts.append(_dot(tril, la_hi) + _dot(tril, la_lo))
    bcum = jnp.concatenate(parts, axis=0)
    e_hi, e_lo = _split_bf16(jnp.exp(jnp.concatenate([bcum[(ci + 1) * c - 1:(ci + 1) * c] for ci in range(nch)],
                                                     axis=0)))
    pick = (iota((nch, LANES), 0) == iota((nch, LANES), 1)).astype(BF16)
    dec_t = _dot_tn(e_hi, pick) + _dot_tn(e_lo, pick)
    q_all = q_ref[...].astype(F32)
    k_all = k_ref[...].astype(F32)
    q_dec_all = q_all * jnp.exp(bcum)
    k_dec_all = k_all * jnp.exp(-bcum)

    own_k = iota((nh * c, nh * hk), 0) // c == iota((nh * c, nh * hk), 1) // hk
    own_v = iota((nh * c, nh * hv), 0) // c == iota((nh * c, nh * hv), 1) // hv
    causal = iota((c, nh * c), 1) % c <= iota((c, nh * c), 0)
    zero_v = jnp.zeros((hk, hv), F32)

    def intra(ci):
        rows = slice(ci * c, (ci + 1) * c)
        q_dec = q_dec_all[rows].astype(BF16)
        k_dec = k_dec_all[rows]
        b_c = bcum[rows]
        k_tail = (k_all[rows] * jnp.exp(b_c[c - 1:c] - b_c)).astype(BF16)
        v = v_ref[rows, :].astype(F32)
        k_bd = jnp.where(own_k, jnp.concatenate([k_dec] * nh, axis=0), 0.0).astype(BF16)
        v_bd = jnp.where(own_v, jnp.concatenate([v] * nh, axis=0), 0.0).astype(BF16)
        att = jnp.where(causal, _dot_nt(q_dec, k_bd), 0.0)
        o_intra = _dot(att.astype(BF16), v_bd)
        vb = v.astype(BF16)
        kvs = []
        for h0 in range(0, nh, 2):
            kv2 = _dot_tn(k_tail[:, h0 * hk:(h0 + 2) * hk], vb[:, h0 * hv:(h0 + 2) * hv])
            kvs += [kv2[:hk, :hv], kv2[hk:, hv:]]
        return q_dec, o_intra, kvs

    def carry(ci, sts, q_dec, o_intra, kvs):
        st_bd = jnp.concatenate(
            [jnp.concatenate([zero_v] * h + [sts[h]] + [zero_v] * (nh - 1 - h), axis=1) for h in range(nh)],
            axis=0).astype(BF16)
        o = o_intra + _dot(q_dec, st_bd)
        new = [dec_t[h * hk:(h + 1) * hk, ci:ci + 1] * sts[h] + kvs[h] for h in range(nh)]
        return o, new

    def finish(ci, o):
        rows = slice(ci * c, (ci + 1) * c)
        for h in range(nh):
            vs = slice(h * hv, (h + 1) * hv)
            oh = _rms_rows(o[:, vs], ng) * _silu(gb_ref[rows, vs].astype(F32))
            o_ref[rows, vs] = oh.astype(o_ref.dtype)

    sts = None
    pending = intra(0)
    unfinished = None
    for ci in range(nch):
        s, first, last = ci // nchunks, ci % nchunks == 0, ci % nchunks == nchunks - 1
        current = pending
        if ci + 1 < nch:
            pending = intra(ci + 1)
        if first:
            sts = [st_ref[s, h] for h in range(nh)]
        o, sts = carry(ci, sts, *current)
        if last:
            for h in range(nh):
                st_ref[s, h] = sts[h]
        if unfinished is not None:
            finish(*unfinished)
        unfinished = (ci, o)
    finish(*unfinished)

    @pl.when(i == pl.num_programs(1) - 1)
    def _():
        sf_ref[...] = st_ref[...]


def _gla(q, k, v, la, gb, s0, ng, nseq_total, seqlen, out_dtype):
    c = math.gcd(seqlen, GLA_CHUNK)
    if seqlen >= 8 * c:
        nseq, nchunks = 1, 8
    else:
        nseq, nchunks = 8, seqlen // c
    assert nseq_total % nseq == 0 and seqlen % (nchunks * c) == 0
    nblk = seqlen // (nchunks * c)
    tm = nseq * nchunks * c
    row = lambda n: pl.BlockSpec((tm, n), lambda b, i: (b * nblk + i, 0))
    st_spec = pl.BlockSpec((nseq, GLA_HEADS, GLA_HEAD_K, GLA_HEAD_V), lambda b, i: (b, 0, 0, 0))
    m = q.shape[0]
    return pl.pallas_call(
        functools.partial(_gla_kernel, nseq=nseq, nchunks=nchunks, c=c),
        grid=(nseq_total // nseq, nblk),
        in_specs=[row(256), row(256), row(512), row(256), row(512), st_spec, _full(ng.shape)],
        out_specs=[row(512), st_spec],
        out_shape=[jax.ShapeDtypeStruct((m, 512), out_dtype), jax.ShapeDtypeStruct(s0.shape, F32)],
        scratch_shapes=[pltpu.VMEM((nseq, GLA_HEADS, GLA_HEAD_K, GLA_HEAD_V), F32)],
        compiler_params=_cparams(("parallel", "arbitrary")),
        name="gla",
    )(q, k, v, la, gb, s0, ng)


def _gelu_tanh(x):
    return 0.5 * x * (1.0 + jnp.tanh(math.sqrt(2.0 / math.pi) * (x + 0.044715 * (x * x * x))))


def _head_ones(n):
    r = lax.broadcasted_iota(jnp.int32, (n, n), 0) // SWA_HEAD_DIM
    c = lax.broadcasted_iota(jnp.int32, (n, n), 1) // SWA_HEAD_DIM
    return (r == c).astype(BF16)


def _rope_block(x, cos_t, sin_t, upper):
    swapped = jnp.where(upper, pltpu.roll(x, 32, 1), pltpu.roll(x, 96, 1))
    return x * cos_t + swapped * sin_t


def _mid_kernel(x_ref, y_ref, u_ref, ga_ref, ob_ref, d_ref, wglu_ref, bglu_ref, woa_ref, wob_ref,
                g_ref, w_ref, gq_ref, gk_ref, cos_ref, sin_ref, h_ref, q_ref, k_ref, v_ref, gate_ref):
    tm = x_ref.shape[0]
    lane = lax.broadcasted_iota(jnp.int32, (tm, LANES), 1)
    upper = (lane & 32) != 0
    ones4 = _head_ones(MXU_TILE)
    inv_d = 1.0 / SWA_HEAD_DIM
    qw = SWA_HEADS * SWA_HEAD_DIM
    kw = SWA_KV_HEADS * SWA_HEAD_DIM
    nblk = qw // MXU_TILE
    no = y_ref.shape[0]

    y = jnp.concatenate([y_ref[o].astype(F32) for o in range(no)], axis=1)
    u = jnp.concatenate([u_ref[o].astype(F32) for o in range(no)], axis=1)
    z = _gelu_tanh(y + d_ref[...] * u)
    z = z * _sigmoid(_dot(z.astype(BF16), wglu_ref[...]) + bglu_ref[...])
    out_a = z * _silu(ga_ref[...].astype(F32))
    mix = _dot(out_a.astype(BF16), woa_ref[...]) + _dot(ob_ref[...].astype(BF16), wob_ref[...])
    h = x_ref[...] + mix
    h_ref[...] = h

    xb = _rms_rows(h, g_ref[...]).astype(BF16)
    cos_t = cos_ref[...]
    sin_t = sin_ref[...]

    def q_finish(j, q):
        ss = _dot((q * q).astype(BF16), ones4)
        qn = q * lax.rsqrt(ss * inv_d + NORM_EPS) * gq_ref[...]
        for e in range(MXU_TILE // LANES):
            cols = slice(j * MXU_TILE + e * LANES, j * MXU_TILE + (e + 1) * LANES)
            qe = _rope_block(qn[:, e * LANES:(e + 1) * LANES], cos_t, sin_t, upper)
            q_ref[:, cols] = (qe * (SWA_HEAD_DIM ** -0.5 * LOG2E)).astype(q_ref.dtype)

    q_next = _dot(xb, w_ref[:, 0:MXU_TILE])
    for j in range(nblk):
        q_cur = q_next
        if j + 1 < nblk:
            q_next = _dot(xb, w_ref[:, (j + 1) * MXU_TILE:(j + 2) * MXU_TILE])
        else:
            q_next = _dot(xb, w_ref[:, qw:qw + 2 * kw])
        q_finish(j, q_cur)
    kv = q_next
    gate = _dot(xb, w_ref[:, qw + 2 * kw:])
    k = kv[:, :kw]
    ss = _dot((k * k).astype(BF16), ones4[:kw, :kw])
    kn = k * lax.rsqrt(ss * inv_d + NORM_EPS) * gk_ref[...]
    k_ref[...] = _rope_block(kn, cos_t, sin_t, upper)
    v_ref[...] = kv[:, kw:]
    gate_ref[...] = _silu(gate).astype(gate_ref.dtype)


def _mid(x, y, u, ga, ob, pe, po, cos_t, sin_t, act_dtype):
    m, dm = x.shape
    tm = _row_tile(m)
    row = lambda n: pl.BlockSpec((tm, n), lambda i: (i, 0))
    slab = pl.BlockSpec((y.shape[0], tm, LANES), lambda i: (0, i, 0))
    assert cos_t.shape[0] % tm == 0
    nper = cos_t.shape[0] // tm
    tab = pl.BlockSpec((tm, LANES), lambda i: (i % nper, 0))
    qw = SWA_HEADS * SWA_HEAD_DIM
    kw = SWA_KV_HEADS * SWA_HEAD_DIM
    weights = [pe['d'], pe['wglu'], pe['bglu'], pe['woa'], pe['wob'], po['norm_g'], po['w_in'], po['gq'], po['gk']]
    return pl.pallas_call(
        _mid_kernel,
        grid=(m // tm,),
        in_specs=[row(dm), slab, slab, row(512), row(512)] + [_full(w.shape) for w in weights] + [tab, tab],
        out_specs=[row(dm), row(qw), row(kw), row(kw), row(qw)],
        out_shape=[jax.ShapeDtypeStruct((m, dm), F32),
                   jax.ShapeDtypeStruct((m, qw), act_dtype), jax.ShapeDtypeStruct((m, kw), F32),
                   jax.ShapeDtypeStruct((m, kw), F32), jax.ShapeDtypeStruct((m, qw), act_dtype)],
        compiler_params=_cparams(("parallel",)),
        name="mid",
    )(x, y, u, ga, ob, *weights, cos_t, sin_t)


def _rope_tables(pos):
    half = SWA_HEAD_DIM // 2
    inv_freq = ROPE_THETA ** (-np.arange(half, dtype=np.float64) / half)
    ang = np.asarray(pos, np.float64)[:, None] * inv_freq[None, :]
    cos, sin = np.cos(ang), np.sin(ang)
    cos_t = np.concatenate([cos, cos, cos, cos], axis=1).astype(np.float32)
    sin_t = np.concatenate([-sin, sin, -sin, sin], axis=1).astype(np.float32)
    return jnp.asarray(cos_t), jnp.asarray(sin_t)


def _attn_prompt_kernel(sink_ref, q_ref, kc_ref, kp_ref, vc_ref, vp_ref, gate_ref, h_ref, wout_ref,
                        y_ref, p_ref, o_ref, *, nqb):
    i = pl.program_id(1)
    w = SWA_WINDOW
    hd = SWA_HEAD_DIM
    npair = SWA_GROUP // 2
    lane2 = lax.broadcasted_iota(jnp.int32, (2 * w, LANES), 1)
    low = lane2 < hd
    rr = lax.broadcasted_iota(jnp.int32, (w, w), 0)
    cc = lax.broadcasted_iota(jnp.int32, (w, w), 1)
    tri = cc <= rr
    low_w = cc < hd
    r4 = lax.broadcasted_iota(jnp.int32, (4 * w, LANES), 0)
    c4 = lax.broadcasted_iota(jnp.int32, (4 * w, LANES), 1)
    den_cols = ((r4 < 2 * w) == (c4 < hd)).astype(BF16)
    units = [(jb, kv) for jb in range(nqb) for kv in range(SWA_KV_HEADS)]

    def scores(jb, kv):
        rows = slice(jb * w, (jb + 1) * w)
        if jb == 0:
            k_prev, v_prev = kp_ref[...], vp_ref[...]
        else:
            prev_rows = slice((jb - 1) * w, jb * w)
            k_prev, v_prev = kc_ref[prev_rows, :], vc_ref[prev_rows, :]
        kcat = jnp.concatenate([k_prev, kc_ref[rows, :]], axis=0)
        vcat = jnp.concatenate([v_prev, vc_ref[rows, :]], axis=0)
        own = low if kv == 0 else jnp.logical_not(low)
        k_own = jnp.where(own, kcat, 0.0)
        v_own = jnp.where(own, vcat, 0.0)
        k_oth = pltpu.roll(k_own, hd, 1)
        v_oth = pltpu.roll(v_own, hd, 1)
        k_lo, k_hi = (k_own, k_oth) if kv == 0 else (k_oth, k_own)
        v_lo, v_hi = (v_own, v_oth) if kv == 0 else (v_oth, v_own)
        k_rhs = jnp.concatenate([k_lo, k_hi], axis=0).astype(BF16)
        v_rhs = jnp.concatenate([jnp.concatenate([v_lo, v_hi], axis=0).astype(BF16), den_cols], axis=1)
        qs = jnp.concatenate([q_ref[rows, (kv * npair + pr) * LANES:(kv * npair + pr + 1) * LANES]
                              for pr in range(npair)], axis=0)
        return _dot_nt(qs, k_rhs), v_rhs

    def softmax_pv(kv, first, buf, s_all, v_rhs):
        sink_terms = []
        for pr in range(npair):
            prow = slice(pr * w, (pr + 1) * w)
            pair_terms = []
            for e in range(2):
                s_prev = s_all[prow, (2 * e) * w:(2 * e + 1) * w]
                s_cur = s_all[prow, (2 * e + 1) * w:(2 * e + 2) * w]
                if first:
                    s_prev = jnp.where(i > 0, s_prev, -jnp.inf)
                sc = jnp.where(tri, s_cur, s_prev)
                sink = sink_ref[2 * (kv * npair + pr) + e] * LOG2E
                mx = jnp.maximum(jnp.max(sc, axis=-1, keepdims=True), sink)
                pe = jnp.exp2(sc - mx)
                p_ref[buf, prow, (2 * e) * w:(2 * e + 1) * w] = jnp.where(tri, 0.0, pe).astype(BF16)
                p_ref[buf, prow, (2 * e + 1) * w:(2 * e + 2) * w] = jnp.where(tri, pe, 0.0).astype(BF16)
                pair_terms.append(jnp.exp2(sink - mx))
            sink_terms.append(pair_terms)
        return _dot(p_ref[buf], v_rhs), sink_terms

    def normalise(jb, kv, o_ext, sink_terms):
        rows = slice(jb * w, (jb + 1) * w)
        for pr in range(npair):
            prow = slice(pr * w, (pr + 1) * w)
            cols = slice((kv * npair + pr) * LANES, (kv * npair + pr + 1) * LANES)
            st = jnp.where(low_w, sink_terms[pr][0], sink_terms[pr][1])
            o = o_ext[prow, :LANES] / (o_ext[prow, LANES:] + st)
            o_ref[rows, cols] = o.astype(o_ref.dtype)

    def project(rows):
        og = o_ref[rows, :].astype(F32) * gate_ref[rows, :].astype(F32)
        y_ref[rows, :] = h_ref[rows, :] + _dot(og.astype(BF16), wout_ref[...])

    group = 2
    pending = scores(*units[0])
    unfinished = None
    for n, (jb, kv) in enumerate(units):
        current = pending
        if n + 1 < len(units):
            pending = scores(*units[n + 1])
        result = softmax_pv(kv, jb == 0, n % 2, *current)
        if unfinished is not None:
            normalise(*unfinished)
            done_jb, done_kv = unfinished[:2]
            if done_kv == SWA_KV_HEADS - 1 and (done_jb + 1) % group == 0:
                project(slice((done_jb + 1 - group) * w, (done_jb + 1) * w))
        unfinished = (jb, kv) + result
    normalise(*unfinished)
    project(slice((nqb - group) * w, nqb * w))


def _attn_prompt(sinks, q, k, v, gate, h, w_out, nseq, seqlen):
    w = SWA_WINDOW
    nqb = 8
    tm = nqb * w
    assert seqlen % tm == 0
    nblk = seqlen // tm
    qw = q.shape[1]
    dm = h.shape[1]
    row = lambda n: pl.BlockSpec((tm, n), lambda b, i: (b * nblk + i, 0))
    prev = lambda n: pl.BlockSpec((w, n), lambda b, i: (jnp.maximum((b * nblk + i) * nqb - 1, 0), 0))
    smem = pl.BlockSpec(memory_space=pltpu.SMEM)
    return pl.pallas_call(
        functools.partial(_attn_prompt_kernel, nqb=nqb),
        grid=(nseq, nblk),
        in_specs=[smem, row(qw), row(LANES), prev(LANES), row(LANES), prev(LANES), row(qw), row(dm),
                  _full(w_out.shape)],
        out_specs=row(dm),
        out_shape=jax.ShapeDtypeStruct(h.shape, F32),
        scratch_shapes=[pltpu.VMEM((2, 4 * w, 4 * w), BF16), pltpu.VMEM((tm, qw), BF16)],
        compiler_params=_cparams(("parallel", "arbitrary")),
        name="attn_prompt",
    )(sinks, q, k, k, v, v, gate, h, w_out)


def _attn_sample_kernel(sink_ref, q_ref, kn_ref, vn_ref, kc_ref, vc_ref, o_ref, ko_ref, vo_ref, q2_ref, *, seqlen):
    nq = SWA_HEADS * seqlen
    hd = SWA_HEAD_DIM
    heads = [(kv, g) for kv in range(SWA_KV_HEADS) for g in range(SWA_GROUP)]

    def stacked(kv, g):
        r0 = (kv * SWA_GROUP + g) * seqlen
        return slice(r0, r0 + seqlen), slice(kv * hd, (kv + 1) * hd)

    q2_ref[...] = jnp.zeros(q2_ref.shape, q2_ref.dtype)
    for kv, g in heads:
        rows, lanes = stacked(kv, g)
        h = kv * SWA_GROUP + g
        q2_ref[:, rows, lanes] = q_ref[:, :, h * hd:(h + 1) * hd]
    ncache = kc_ref.shape[1]
    t_row = lax.broadcasted_iota(jnp.int32, (nq, ncache), 0) % seqlen
    c_col = lax.broadcasted_iota(jnp.int32, (nq, ncache), 1)
    cache_ok = c_col > t_row - (SWA_WINDOW - ncache)
    nnew = kn_ref.shape[1]
    t_row_n = lax.broadcasted_iota(jnp.int32, (nq, nnew), 0) % seqlen
    n_col = lax.broadcasted_iota(jnp.int32, (nq, nnew), 1)
    new_ok = n_col <= t_row_n
    sink = (sink_ref[...] * LOG2E)[None, :, 0:1]
    bqk = lambda a, b: lax.dot_general(a, b, (((2,), (2,)), ((0,), (0,))), preferred_element_type=F32)
    bpv = lambda a, b: lax.dot_general(a, b, (((2,), (1,)), ((0,), (0,))), preferred_element_type=F32)
    q = q2_ref[...].astype(BF16)
    kc = kc_ref[...]
    vc = vc_ref[...]
    kn = kn_ref[...]
    vn = vn_ref[...]
    sc = jnp.where(cache_ok[None], bqk(q, kc.astype(BF16)), -jnp.inf)
    sn = jnp.where(new_ok[None], bqk(q, kn.astype(BF16)), -jnp.inf)
    mx = jnp.maximum(jnp.maximum(jnp.max(sc, axis=-1, keepdims=True), jnp.max(sn, axis=-1, keepdims=True)), sink)
    pc = jnp.exp2(sc - mx)
    pn = jnp.exp2(sn - mx)
    den = jnp.sum(pc, axis=-1, keepdims=True) + jnp.sum(pn, axis=-1, keepdims=True) + jnp.exp2(sink - mx)
    inv = 1.0 / den
    o2 = bpv((pc * inv).astype(BF16), vc.astype(BF16)) + bpv((pn * inv).astype(BF16), vn.astype(BF16))
    for kv, g in heads:
        rows, lanes = stacked(kv, g)
        h = kv * SWA_GROUP + g
        o_ref[:, :, h * hd:(h + 1) * hd] = o2[:, rows, lanes]
    keep = ncache - seqlen
    ko_ref[:, 0:keep, :] = kc[:, seqlen:ncache, :]
    ko_ref[:, keep:ncache, :] = kn[:, 0:seqlen, :]
    vo_ref[:, 0:keep, :] = vc[:, seqlen:ncache, :]
    vo_ref[:, keep:ncache, :] = vn[:, 0:seqlen, :]


def _attn_sample(sink_rows, q, kn, vn, kc, vc):
    n, seqlen, _ = q.shape
    nseq = 16
    assert n % nseq == 0
    blk = lambda a: pl.BlockSpec((nseq,) + a.shape[1:], lambda i: (i, 0, 0))
    return pl.pallas_call(
        functools.partial(_attn_sample_kernel, seqlen=seqlen),
        grid=(n // nseq,),
        in_specs=[_full(sink_rows.shape), blk(q), blk(kn), blk(vn), blk(kc), blk(vc)],
        out_specs=[blk(q), blk(kc), blk(vc)],
        out_shape=[jax.ShapeDtypeStruct(q.shape, F32), jax.ShapeDtypeStruct(kc.shape, F32),
                   jax.ShapeDtypeStruct(vc.shape, F32)],
        scratch_shapes=[pltpu.VMEM((nseq, SWA_HEADS * seqlen, SWA_KV_HEADS * SWA_HEAD_DIM), F32)],
        compiler_params=_cparams(("parallel",)),
        name="attn_sample",
    )(sink_rows, q, kn, vn, kc, vc)


def _odd_out_kernel(h_ref, o_ref, gate_ref, w_ref, y_ref):
    og = o_ref[...].astype(F32) * gate_ref[...].astype(F32)
    y_ref[...] = h_ref[...] + _dot(og.astype(BF16), w_ref[...])


def _odd_out(h, o, gate, w):
    m, dm = h.shape
    tm = _row_tile(m)
    row = lambda n: pl.BlockSpec((tm, n), lambda i: (i, 0))
    return pl.pallas_call(
        _odd_out_kernel,
        grid=(m // tm,),
        in_specs=[row(dm), row(o.shape[1]), row(gate.shape[1]), _full(w.shape)],
        out_specs=row(dm),
        out_shape=jax.ShapeDtypeStruct((m, dm), F32),
        compiler_params=_cparams(("parallel",)),
        name="odd_out",
    )(h, o, gate, w)


def _trunk(x, s5_x0, gla_s0, pe, po, pos, nseq, seqlen, act_dtype):
    u4, u2, ga, q, k, v, la, gb = _even_in(x, pe['norm_g'], pe['wm'], pe['wa'], pe['wgb'], pe['wgate'],
                                           pe['bgate'], act_dtype, _s5_chunk(seqlen))
    y4, s5_fin = _s5_branch(u2, s5_x0, pe['s5_ops'], nseq, seqlen, act_dtype)
    ob, gla_fin = _gla(q, k, v, la, gb, gla_s0, pe['gla_norm_g'], nseq, seqlen, act_dtype)
    cos_t, sin_t = _rope_tables(pos)
    h, q1, k1, v1, gate = _mid(x, y4, u4, ga, ob, pe, po, cos_t, sin_t, act_dtype)
    return h, q1, k1, v1, gate, s5_fin, gla_fin


def kernel(x_prompt, x_sample, state_s5_re, state_s5_im, state_gla, cache_swa_k, cache_swa_v,
           even_norm_g, even_w_in, s5_lambda_re, s5_lambda_im, s5_log_dt, s5_b_re, s5_b_im,
           s5_c_re, s5_c_im, s5_d, s5_w_glu, s5_b_glu, gla_w_gate, gla_b_gate, gla_norm_g,
           even_w_out, odd_norm_g, odd_w_in, swa_q_norm_g, swa_k_norm_g, swa_sinks, odd_w_out):
    nb, seq, dm = x_prompt.shape
    ns, dseq, _ = x_sample.shape
    ng = s5_lambda_re.shape[1]
    no = ng // OCT
    xp = x_prompt.reshape(nb * seq, dm)
    xs = x_sample.reshape(ns * dseq, dm)

    i = 0
    w_in = even_w_in[i]
    s5w = ng * S5_GROUP
    col_alow = 2 * s5w + 2 * GLA_HEADS * GLA_HEAD_K + GLA_HEADS * GLA_HEAD_V
    col_gb = col_alow + GLA_GATE_RANK
    pad_rank = LANES - GLA_GATE_RANK
    pe = {
        'norm_g': even_norm_g[i][None, :],
        'wm': w_in[:, :col_alow].astype(BF16),
        'wa': jnp.pad(w_in[:, col_alow:col_gb], ((0, 0), (0, pad_rank))).astype(BF16),
        'wgb': w_in[:, col_gb:].astype(BF16),
        'wgate': jnp.pad(gla_w_gate[i], ((0, pad_rank), (0, 0))).astype(BF16),
        'bgate': gla_b_gate[i][None, :],
        's5_ops': _s5_params(s5_lambda_re[i], s5_lambda_im[i], s5_log_dt[i], s5_b_re[i], s5_b_im[i],
                             s5_c_re[i], s5_c_im[i]),
        'gla_norm_g': gla_norm_g[i][None, :],
        'd': s5_d[i][None, :],
        'wglu': s5_w_glu[i].astype(BF16),
        'bglu': s5_b_glu[i][None, :],
        'woa': even_w_out[i][:s5w].astype(BF16),
        'wob': even_w_out[i][s5w:].astype(BF16),
    }
    po = {
        'norm_g': odd_norm_g[i][None, :],
        'w_in': odd_w_in[i].astype(BF16),
        'gq': jnp.tile(swa_q_norm_g[i], MXU_TILE // SWA_HEAD_DIM)[None, :],
        'gk': jnp.tile(swa_k_norm_g[i], LANES // SWA_HEAD_DIM)[None, :],
    }
    w_out = odd_w_out[i].astype(BF16)
    sinks = swa_sinks[i]
    kvw = SWA_KV_HEADS * SWA_HEAD_DIM

    s5_zero = jnp.zeros((no, nb, 2 * OCT * S5_STATE), F32)
    gla_zero = jnp.zeros((nb, GLA_HEADS, GLA_HEAD_K, GLA_HEAD_V), F32)
    hp, q, k, v, gate, s5_p, gla_p = _trunk(xp, s5_zero, gla_zero, pe, po, np.arange(seq), nb, seq, BF16)
    s5r_p, s5i_p = _s5_state_out(s5_p)
    y_prompt = _attn_prompt(sinks, q, k, v, gate, hp, w_out, nb, seq).reshape(nb, seq, dm)
    cache_len = min(SWA_WINDOW, seq)
    tail = lambda a: (a.reshape(nb, seq, kvw)[:, seq - cache_len:]
                      .reshape(1, nb, cache_len, SWA_KV_HEADS, SWA_HEAD_DIM))
    swk_p, swv_p = tail(k), tail(v)

    pos_s = np.tile(PAST_LEN + np.arange(dseq), ns)
    s5_init = _s5_state_in(state_s5_re[i], state_s5_im[i], no)
    hs, q, k, v, gate, s5_s, gla_s = _trunk(xs, s5_init, state_gla[i], pe, po, pos_s, ns, dseq, F32)
    s5r_s, s5i_s = _s5_state_out(s5_s)
    ncache = cache_swa_k.shape[2]
    sink_rows = jnp.broadcast_to(jnp.repeat(sinks, dseq)[:, None], (SWA_HEADS * dseq, LANES))
    kn = k.reshape(ns, dseq, kvw)
    vn = v.reshape(ns, dseq, kvw)
    npad = 16 - dseq
    kn_pad = jnp.pad(kn, ((0, 0), (0, npad), (0, 0)))
    vn_pad = jnp.pad(vn, ((0, 0), (0, npad), (0, 0)))
    kc = cache_swa_k[i].reshape(ns, ncache, kvw)
    vc = cache_swa_v[i].reshape(ns, ncache, kvw)
    o, kc_new, vc_new = _attn_sample(sink_rows, q.reshape(ns, dseq, -1), kn_pad, vn_pad, kc, vc)
    y_sample = _odd_out(hs, o.reshape(ns * dseq, -1), gate, w_out).reshape(ns, dseq, dm)
    swk_s = kc_new.reshape(1, ns, ncache, SWA_KV_HEADS, SWA_HEAD_DIM)
    swv_s = vc_new.reshape(1, ns, ncache, SWA_KV_HEADS, SWA_HEAD_DIM)

    return (y_prompt, y_sample,
            s5r_p, s5i_p, gla_p[None], swk_p, swv_p,
            s5r_s, s5i_s, gla_s[None], swk_s, swv_s)
```

```python
import functools
import math

import jax
import jax.numpy as jnp
import numpy as np
from jax import lax
from jax.experimental import pallas as pl
from jax.experimental.pallas import tpu as pltpu

F32 = jnp.float32
BF16 = jnp.bfloat16

PAST_LEN = 8192
NORM_EPS = 1e-6
S5_GROUP = 16
S5_STATE = 64
S5_CHUNK = 16
GLA_HEADS = 4
GLA_HEAD_K = 64
GLA_HEAD_V = 128
GLA_GATE_RANK = 16
GLA_GATE_TAU = 16.0
GLA_CHUNK = 64
SWA_HEADS = 16
SWA_KV_HEADS = 2
SWA_GROUP = SWA_HEADS // SWA_KV_HEADS
SWA_HEAD_DIM = 64
SWA_WINDOW = 128
ROPE_THETA = 10000.0
LOG2E = math.log2(math.e)
LANES = 128
MXU_TILE = 256
OCT = LANES // S5_GROUP
ROW_TILE = 512
VMEM_LIMIT = 48 * 1024 * 1024


def _cparams(sem):
    return pltpu.CompilerParams(dimension_semantics=sem, vmem_limit_bytes=VMEM_LIMIT)


def _full(shape):
    n = len(shape)
    return pl.BlockSpec(shape, lambda *_: (0,) * n)


def _dot(a, b):
    return jnp.dot(a, b, preferred_element_type=F32)


def _dot_nt(a, b):
    return lax.dot_general(a, b, (((1,), (1,)), ((), ())), preferred_element_type=F32)


def _dot_tn(a, b):
    return lax.dot_general(a, b, (((0,), (0,)), ((), ())), preferred_element_type=F32)


def _split_bf16(x):
    hi = x.astype(BF16)
    lo = (x - hi.astype(F32)).astype(BF16)
    return hi, lo


def _rms_rows(x, g):
    return x * lax.rsqrt(jnp.mean(x * x, axis=-1, keepdims=True) + NORM_EPS) * g


def _sigmoid(x):
    return 1.0 / (1.0 + jnp.exp(-x))


def _silu(x):
    return x * _sigmoid(x)


def _row_tile(m, tile=ROW_TILE):
    return tile if m % tile == 0 else m


def _even_in_kernel(x_ref, g_ref, wm_ref, wa_ref, wgb_ref, wgate_ref, bgate_ref,
                    u_ref, u2_ref, ga_ref, q_ref, k_ref, v_ref, la_ref, gb_ref, uscr_ref, *, t):
    xb = _rms_rows(x_ref[...], g_ref[...]).astype(BF16)

    def proj(lo, hi):
        return _dot(xb, wm_ref[:, lo:hi])

    u = proj(0, 512)
    nrow = u.shape[0] // t
    for o in range(u_ref.shape[0]):
        uo = u[:, o * LANES:(o + 1) * LANES]
        u_ref[o] = uo.astype(u_ref.dtype)
        uscr_ref[o] = uo
        for tt in range(t):
            piece = uscr_ref[o, pl.ds(tt, nrow, stride=t), :]
            u2_ref[o, :, tt * LANES:(tt + 1) * LANES] = piece.astype(u2_ref.dtype)
    ga_ref[...] = _silu(proj(512, 1024)).astype(ga_ref.dtype)
    q_ref[...] = (proj(1024, 1280) * (GLA_HEAD_K ** -0.5)).astype(q_ref.dtype)
    k_ref[...] = proj(1280, 1536).astype(k_ref.dtype)
    v_ref[...] = proj(1536, 2048).astype(v_ref.dtype)
    gb_ref[...] = _silu(_dot(xb, wgb_ref[...])).astype(gb_ref.dtype)
    a_low = _dot(xb, wa_ref[...])
    logit = _dot(a_low.astype(BF16), wgate_ref[...]) + bgate_ref[...]
    log_sig = jnp.minimum(logit, 0.0) - jnp.log1p(jnp.exp(-jnp.abs(logit)))
    la_ref[...] = log_sig * (1.0 / GLA_GATE_TAU)


def _even_in(x, g, wm, wa, wgb, wgate, bgate, act_dtype, t):
    m, d = x.shape
    tm = _row_tile(m, 2 * ROW_TILE) if m > ROW_TILE else m
    row = lambda n: pl.BlockSpec((tm, n), lambda i: (i, 0))
    no = 512 // LANES
    slab = pl.BlockSpec((no, tm, LANES), lambda i: (0, i, 0))
    chunk = pl.BlockSpec((no, tm // t, t * LANES), lambda i: (0, i, 0))
    outs = [(512, act_dtype), (256, act_dtype), (256, act_dtype), (512, act_dtype), (256, F32), (512, act_dtype)]
    return pl.pallas_call(
        functools.partial(_even_in_kernel, t=t),
        grid=(m // tm,),
        in_specs=[row(d), _full(g.shape), _full(wm.shape), _full(wa.shape), _full(wgb.shape),
                  _full(wgate.shape), _full(bgate.shape)],
        out_specs=[slab, chunk] + [row(n) for n, _ in outs],
        out_shape=[jax.ShapeDtypeStruct((no, m, LANES), act_dtype),
                   jax.ShapeDtypeStruct((no, m // t, t * LANES), act_dtype)]
        + [jax.ShapeDtypeStruct((m, n), dt) for n, dt in outs],
        scratch_shapes=[pltpu.VMEM((no, tm, LANES), F32)],
        compiler_params=_cparams(("parallel",)),
        name="even_in",
    )(x, g, wm, wa, wgb, wgate, bgate)


def _group_mask(shape, row_span, col_span):
    rg = (lax.broadcasted_iota(jnp.int32, shape, 0) // row_span) % OCT
    cg = (lax.broadcasted_iota(jnp.int32, shape, 1) // col_span) % OCT
    return rg == cg


def _s5_state_kernel(u_ref, bre_ref, bim_ref, x0_ref, are_ref, aim_ref, xs_ref, xf_ref, loc_ref, *, nseq, nchunks):
    hw = OCT * S5_STATE
    kk = u_ref.shape[2]
    bp = jnp.concatenate([bre_ref[0]] * OCT + [bim_ref[0]] * OCT, axis=1)
    bp = jnp.where(_group_mask((kk, 2 * hw), S5_GROUP, S5_STATE), bp, 0.0).astype(BF16)
    loc_ref[...] = _dot(u_ref[0], bp)
    a_re = are_ref[0]
    a_im = aim_ref[0]
    if nchunks == 1:
        x0 = x0_ref[0]
        xr, xi = x0[:, :hw], x0[:, hw:]
        loc = loc_ref[...]
        xf_ref[0, :, :hw] = a_re * xr - a_im * xi + loc[:, :hw]
        xf_ref[0, :, hw:] = a_re * xi + a_im * xr + loc[:, hw:]
        xs_ref[0] = x0.astype(xs_ref.dtype)
    else:
        def body(j, carry):
            new = []
            for b in range(nseq):
                xr, xi = carry[b]
                row = pl.ds(b * nchunks + j, 1)
                lr = loc_ref[row, :hw]
                li = loc_ref[row, hw:]
                loc_ref[row, :hw] = xr
                loc_ref[row, hw:] = xi
                new.append((a_re * xr - a_im * xi + lr, a_re * xi + a_im * xr + li))
            return tuple(new)

        init = tuple((x0_ref[0, b:b + 1, :hw], x0_ref[0, b:b + 1, hw:]) for b in range(nseq))
        fin = lax.fori_loop(0, nchunks, body, init, unroll=4)
        for b in range(nseq):
            xf_ref[0, b:b + 1, :hw] = fin[b][0]
            xf_ref[0, b:b + 1, hw:] = fin[b][1]
        xs_ref[0] = loc_ref[...].astype(xs_ref.dtype)


def _s5_state(u2, bre, bim, x0, are, aim, nseq, nchunks):
    no, r, kk = u2.shape
    sw = x0.shape[2]
    blk = lambda a: pl.BlockSpec((1,) + a.shape[1:], lambda o: (o,) + (0,) * (a.ndim - 1))
    return pl.pallas_call(
        functools.partial(_s5_state_kernel, nseq=nseq, nchunks=nchunks),
        grid=(no,),
        in_specs=[blk(u2), blk(bre), blk(bim), blk(x0), blk(are), blk(aim)],
        out_specs=[pl.BlockSpec((1, r, sw), lambda o: (o, 0, 0)), blk(x0)],
        out_shape=[jax.ShapeDtypeStruct((no, r, sw), BF16), jax.ShapeDtypeStruct(x0.shape, F32)],
        scratch_shapes=[pltpu.VMEM((r, sw), F32)],
        compiler_params=_cparams(("parallel",)),
        name="s5_state",
    )(u2, bre, bim, x0, are, aim)


def _dot_nt_f32(a, b):
    a_hi, a_lo = _split_bf16(a)
    b_hi, b_lo = _split_bf16(b)
    return _dot_nt(a_hi, b_hi) + _dot_nt(a_hi, b_lo) + _dot_nt(a_lo, b_hi)


def _s5_out_kernel(u_ref, xs_ref, bre_ref, bim_ref, cre_ref, cim_ref, zre_ref, zim_ref, y_ref, yscr_ref):
    r, kk = u_ref.shape[1], u_ref.shape[2]
    t = kk // LANES
    sw = xs_ref.shape[2]
    ntile = kk // MXU_TILE
    taps = _dot_nt_f32(bre_ref[0], zre_ref[0]) - _dot_nt_f32(bim_ref[0], zim_ref[0])
    tmask = _group_mask((LANES, LANES), S5_GROUP, S5_GROUP)

    def tap(lag):
        if lag < 0:
            return jnp.zeros((LANES, LANES), F32)
        s = t - 1 - lag
        return jnp.where(tmask, taps[s * LANES:(s + 1) * LANES], 0.0)

    wts = [jnp.concatenate([jnp.concatenate([tap(2 * d), tap(2 * d + 1)], axis=1),
                            jnp.concatenate([tap(2 * d - 1), tap(2 * d)], axis=1)], axis=0).astype(BF16)
           for d in range(ntile)]
    cpt = jnp.concatenate([cre_ref[0]] * OCT + [cim_ref[0]] * OCT, axis=1)
    cpt = jnp.where(_group_mask((kk, sw), S5_GROUP, S5_STATE), cpt, 0.0).astype(BF16)
    xs = xs_ref[0]
    for n in range(ntile):
        acc = _dot_nt(xs, cpt[n * MXU_TILE:(n + 1) * MXU_TILE])
        for k in range(n + 1):
            acc = acc + _dot(u_ref[0, :, k * MXU_TILE:(k + 1) * MXU_TILE], wts[n - k])
        for e in range(MXU_TILE // LANES):
            yscr_ref[pl.ds(2 * n + e, r, stride=t), :] = acc[:, e * LANES:(e + 1) * LANES]
    y_ref[0] = yscr_ref[...].astype(y_ref.dtype)


def _s5_out(u2, xs, bre, bim, cre, cim, zre, zim, out_dtype):
    no, r, kk = u2.shape
    m = r * (kk // LANES)
    blk = lambda a: pl.BlockSpec((1,) + a.shape[1:], lambda o: (o,) + (0,) * (a.ndim - 1))
    ops = (u2, xs, bre, bim, cre, cim, zre, zim)
    return pl.pallas_call(
        _s5_out_kernel,
        grid=(no,),
        in_specs=[blk(a) for a in ops],
        out_specs=pl.BlockSpec((1, m, LANES), lambda o: (o, 0, 0)),
        out_shape=jax.ShapeDtypeStruct((no, m, LANES), out_dtype),
        scratch_shapes=[pltpu.VMEM((m, LANES), F32)],
        compiler_params=_cparams(("parallel",)),
        name="s5_out",
    )(*ops)


def _s5_params(lam_re, lam_im, log_dt, b_re, b_im, c_re, c_im):
    t = S5_CHUNK
    ng = lam_re.shape[0]
    no = ng // OCT
    dt = jnp.exp(log_dt)[:, None]
    a = lam_re * dt
    b = lam_im * dt
    n = jnp.arange(t + 1, dtype=F32)[None, :, None]
    mag = jnp.exp(n * a[:, None, :])
    pw_re = mag * jnp.cos(n * b[:, None, :])
    pw_im = mag * jnp.sin(n * b[:, None, :])
    em1_re = jnp.expm1(a) * jnp.cos(b) - 2.0 * jnp.sin(0.5 * b) ** 2
    em1_im = jnp.exp(a) * jnp.sin(b)
    den = lam_re * lam_re + lam_im * lam_im
    z_re = (em1_re * lam_re + em1_im * lam_im) / den
    z_im = (em1_im * lam_re - em1_re * lam_im) / den
    bb_re = z_re[..., None] * b_re - z_im[..., None] * b_im
    bb_im = z_re[..., None] * b_im + z_im[..., None] * b_re
    def rows(w):
        return w.reshape(no, OCT, t, S5_GROUP, S5_STATE).transpose(0, 2, 1, 3, 4).reshape(no, t * LANES, S5_STATE)

    bt_re = bb_re.transpose(0, 2, 1)[:, None]
    bt_im = bb_im.transpose(0, 2, 1)[:, None]
    r_re, r_im = pw_re[:, t - 1::-1, None, :], pw_im[:, t - 1::-1, None, :]
    bre = rows(r_re * bt_re - r_im * bt_im)
    bim = rows(r_re * bt_im + r_im * bt_re)
    o_re, o_im = pw_re[:, 1:, None, :], pw_im[:, 1:, None, :]
    cre = rows(c_re[:, None] * o_re - c_im[:, None] * o_im)
    cim = rows(-(c_re[:, None] * o_im + c_im[:, None] * o_re))
    return bre, bim, cre, cim, c_re.reshape(no, LANES, S5_STATE), c_im.reshape(no, LANES, S5_STATE), pw_re, pw_im


def _s5_chunk(seqlen):
    return math.gcd(seqlen, S5_CHUNK)


def _s5_branch(u2, x0, ops, nseq, seqlen, out_dtype):
    bre, bim, cre, cim, zre, zim, pw_re, pw_im = ops
    no = u2.shape[0]
    t = _s5_chunk(seqlen)
    kk = t * LANES
    nchunks = seqlen // t
    are = pw_re[:, t].reshape(no, 1, OCT * S5_STATE)
    aim = pw_im[:, t].reshape(no, 1, OCT * S5_STATE)
    tail = S5_CHUNK * LANES - kk
    bre, bim = bre[:, tail:], bim[:, tail:]
    xs, xf = _s5_state(u2, bre, bim, x0, are, aim, nseq, nchunks)
    return _s5_out(u2, xs, bre, bim, cre[:, :kk], cim[:, :kk], zre, zim, out_dtype), xf


def _s5_state_in(re, im, no):
    nseq = re.shape[0]
    f = lambda a: a.reshape(nseq, no, OCT * S5_STATE).transpose(1, 0, 2)
    return jnp.concatenate([f(re), f(im)], axis=-1)


def _s5_state_out(xf):
    no, nseq, _ = xf.shape
    hw = OCT * S5_STATE
    f = lambda a: a.transpose(1, 0, 2).reshape(1, nseq, no * OCT, S5_STATE)
    return f(xf[:, :, :hw]), f(xf[:, :, hw:])


def _gla_kernel(q_ref, k_ref, v_ref, la_ref, gb_ref, s0_ref, ng_ref, o_ref, sf_ref, st_ref,
                *, nseq, nchunks, c):
    i = pl.program_id(1)

    @pl.when(i == 0)
    def _():
        st_ref[...] = s0_ref[...]

    nh, hk, hv = GLA_HEADS, GLA_HEAD_K, GLA_HEAD_V
    nch = nseq * nchunks
    tm = nch * c
    iota = lambda shape, d: lax.broadcasted_iota(jnp.int32, shape, d)
    ng = ng_ref[...]

    tb = min(tm, MXU_TILE)
    rt, ct = iota((tb, tb), 0), iota((tb, tb), 1)
    tril = ((rt // c == ct // c) & (rt >= ct)).astype(BF16)
    parts = []
    for r0 in range(0, tm, tb):
        la_hi, la_lo = _split_bf16(la_ref[r0:r0 + tb, :])
        parts.append(_dot(tril, la_hi) + _dot(tril, la_lo))
    bcum = jnp.concatenate(parts, axis=0)
    e_hi, e_lo = _split_bf16(jnp.exp(jnp.concatenate([bcum[(ci + 1) * c - 1:(ci + 1) * c] for ci in range(nch)],
                                                     axis=0)))
    pick = (iota((nch, LANES), 0) == iota((nch, LANES), 1)).astype(BF16)
    dec_t = _dot_tn(e_hi, pick) + _dot_tn(e_lo, pick)
    q_all = q_ref[...].astype(F32)
    k_all = k_ref[...].astype(F32)
    q_dec_all = q_all * jnp.exp(bcum)
    k_dec_all = k_all * jnp.exp(-bcum)

    own_k = iota((nh * c, nh * hk), 0) // c == iota((nh * c, nh * hk), 1) // hk
    own_v = iota((nh * c, nh * hv), 0) // c == iota((nh * c, nh * hv), 1) // hv
    causal = iota((c, nh * c), 1) % c <= iota((c, nh * c), 0)
    zero_v = jnp.zeros((hk, hv), F32)

    def intra(ci):
        rows = slice(ci * c, (ci + 1) * c)
        q_dec = q_dec_all[rows].astype(BF16)
        k_dec = k_dec_all[rows]
        b_c = bcum[rows]
        k_tail = (k_all[rows] * jnp.exp(b_c[c - 1:c] - b_c)).astype(BF16)
        v = v_ref[rows, :].astype(F32)
        k_bd = jnp.where(own_k, jnp.concatenate([k_dec] * nh, axis=0), 0.0).astype(BF16)
        v_bd = jnp.where(own_v, jnp.concatenate([v] * nh, axis=0), 0.0).astype(BF16)
        att = jnp.where(causal, _dot_nt(q_dec, k_bd), 0.0)
        o_intra = _dot(att.astype(BF16), v_bd)
        vb = v.astype(BF16)
        kvs = []
        for h0 in range(0, nh, 2):
            kv2 = _dot_tn(k_tail[:, h0 * hk:(h0 + 2) * hk], vb[:, h0 * hv:(h0 + 2) * hv])
            kvs += [kv2[:hk, :hv], kv2[hk:, hv:]]
        return q_dec, o_intra, kvs

    def carry(ci, sts, q_dec, o_intra, kvs):
        st_bd = jnp.concatenate(
            [jnp.concatenate([zero_v] * h + [sts[h]] + [zero_v] * (nh - 1 - h), axis=1) for h in range(nh)],
            axis=0).astype(BF16)
        o = o_intra + _dot(q_dec, st_bd)
        new = [dec_t[h * hk:(h + 1) * hk, ci:ci + 1] * sts[h] + kvs[h] for h in range(nh)]
        return o, new

    def finish(ci, o):
        rows = slice(ci * c, (ci + 1) * c)
        for h in range(nh):
            vs = slice(h * hv, (h + 1) * hv)
            oh = _rms_rows(o[:, vs], ng) * gb_ref[rows, vs].astype(F32)
            o_ref[rows, vs] = oh.astype(o_ref.dtype)

    sts = None
    pending = intra(0)
    unfinished = None
    for ci in range(nch):
        s, first, last = ci // nchunks, ci % nchunks == 0, ci % nchunks == nchunks - 1
        current = pending
        if ci + 1 < nch:
            pending = intra(ci + 1)
        if first:
            sts = [st_ref[s, h] for h in range(nh)]
        o, sts = carry(ci, sts, *current)
        if last:
            for h in range(nh):
                st_ref[s, h] = sts[h]
        if unfinished is not None:
            finish(*unfinished)
        unfinished = (ci, o)
    finish(*unfinished)

    @pl.when(i == pl.num_programs(1) - 1)
    def _():
        sf_ref[...] = st_ref[...]


def _gla(q, k, v, la, gb, s0, ng, nseq_total, seqlen, out_dtype):
    c = math.gcd(seqlen, GLA_CHUNK)
    if seqlen >= 8 * c:
        nseq, nchunks = 1, 8
    else:
        nseq, nchunks = 8, seqlen // c
    assert nseq_total % nseq == 0 and seqlen % (nchunks * c) == 0
    nblk = seqlen // (nchunks * c)
    tm = nseq * nchunks * c
    row = lambda n: pl.BlockSpec((tm, n), lambda b, i: (b * nblk + i, 0))
    st_spec = pl.BlockSpec((nseq, GLA_HEADS, GLA_HEAD_K, GLA_HEAD_V), lambda b, i: (b, 0, 0, 0))
    m = q.shape[0]
    return pl.pallas_call(
        functools.partial(_gla_kernel, nseq=nseq, nchunks=nchunks, c=c),
        grid=(nseq_total // nseq, nblk),
        in_specs=[row(256), row(256), row(512), row(256), row(512), st_spec, _full(ng.shape)],
        out_specs=[row(512), st_spec],
        out_shape=[jax.ShapeDtypeStruct((m, 512), out_dtype), jax.ShapeDtypeStruct(s0.shape, F32)],
        scratch_shapes=[pltpu.VMEM((nseq, GLA_HEADS, GLA_HEAD_K, GLA_HEAD_V), F32)],
        compiler_params=_cparams(("parallel", "arbitrary")),
        name="gla",
    )(q, k, v, la, gb, s0, ng)


def _gelu_tanh(x):
    return 0.5 * x * (1.0 + jnp.tanh(math.sqrt(2.0 / math.pi) * (x + 0.044715 * (x * x * x))))


def _head_ones(n):
    r = lax.broadcasted_iota(jnp.int32, (n, n), 0) // SWA_HEAD_DIM
    c = lax.broadcasted_iota(jnp.int32, (n, n), 1) // SWA_HEAD_DIM
    return (r == c).astype(BF16)


def _rope_block(x, cos_t, sin_t, upper):
    swapped = jnp.where(upper, pltpu.roll(x, 32, 1), pltpu.roll(x, 96, 1))
    return x * cos_t + swapped * sin_t


def _mid_kernel(x_ref, y_ref, u_ref, ga_ref, ob_ref, d_ref, wglu_ref, bglu_ref, woa_ref, wob_ref,
                g_ref, w_ref, gq_ref, gk_ref, cos_ref, sin_ref, h_ref, q_ref, k_ref, v_ref, gate_ref):
    tm = x_ref.shape[0]
    lane = lax.broadcasted_iota(jnp.int32, (tm, LANES), 1)
    upper = (lane & 32) != 0
    ones4 = _head_ones(MXU_TILE)
    inv_d = 1.0 / SWA_HEAD_DIM
    qw = SWA_HEADS * SWA_HEAD_DIM
    kw = SWA_KV_HEADS * SWA_HEAD_DIM
    nblk = qw // MXU_TILE
    no = y_ref.shape[0]

    mix_b = _dot(ob_ref[...].astype(BF16), wob_ref[...])
    y = jnp.concatenate([y_ref[o].astype(F32) for o in range(no)], axis=1)
    u = jnp.concatenate([u_ref[o].astype(F32) for o in range(no)], axis=1)
    z = _gelu_tanh(y + d_ref[...] * u)
    z = z * _sigmoid(_dot(z.astype(BF16), wglu_ref[...]) + bglu_ref[...])
    out_a = z * ga_ref[...].astype(F32)
    h = x_ref[...] + (_dot(out_a.astype(BF16), woa_ref[...]) + mix_b)
    h_ref[...] = h

    xb = _rms_rows(h, g_ref[...]).astype(BF16)
    cos_t = cos_ref[...]
    sin_t = sin_ref[...]

    def gate_finish(gate):
        gate_ref[...] = _silu(gate).astype(gate_ref.dtype)

    def q_finish(j, q):
        ss = _dot((q * q).astype(BF16), ones4)
        qn = q * lax.rsqrt(ss * inv_d + NORM_EPS) * gq_ref[...]
        for e in range(MXU_TILE // LANES):
            cols = slice(j * MXU_TILE + e * LANES, j * MXU_TILE + (e + 1) * LANES)
            qe = _rope_block(qn[:, e * LANES:(e + 1) * LANES], cos_t, sin_t, upper)
            q_ref[:, cols] = (qe * (SWA_HEAD_DIM ** -0.5 * LOG2E)).astype(q_ref.dtype)

    def kv_finish(kv):
        k = kv[:, :kw]
        ss = _dot((k * k).astype(BF16), ones4[:kw, :kw])
        kn = k * lax.rsqrt(ss * inv_d + NORM_EPS) * gk_ref[...]
        k_ref[...] = _rope_block(kn, cos_t, sin_t, upper)
        v_ref[...] = kv[:, kw:]

    work = [(slice(j * MXU_TILE, (j + 1) * MXU_TILE), functools.partial(q_finish, j)) for j in range(nblk)]
    work += [(slice(qw, qw + 2 * kw), kv_finish)]
    work += [(slice(qw + 2 * kw, None), gate_finish)]
    pending = _dot(xb, w_ref[:, work[0][0]])
    for n, (_, finish) in enumerate(work):
        current = pending
        if n + 1 < len(work):
            pending = _dot(xb, w_ref[:, work[n + 1][0]])
        finish(current)


def _mid(x, y, u, ga, ob, pe, po, cos_t, sin_t, act_dtype):
    m, dm = x.shape
    tm = _row_tile(m)
    row = lambda n: pl.BlockSpec((tm, n), lambda i: (i, 0))
    slab = pl.BlockSpec((y.shape[0], tm, LANES), lambda i: (0, i, 0))
    assert cos_t.shape[0] % tm == 0
    nper = cos_t.shape[0] // tm
    tab = pl.BlockSpec((tm, LANES), lambda i: (i % nper, 0))
    qw = SWA_HEADS * SWA_HEAD_DIM
    kw = SWA_KV_HEADS * SWA_HEAD_DIM
    weights = [pe['d'], pe['wglu'], pe['bglu'], pe['woa'], pe['wob'], po['norm_g'], po['w_in'], po['gq'], po['gk']]
    return pl.pallas_call(
        _mid_kernel,
        grid=(m // tm,),
        in_specs=[row(dm), slab, slab, row(512), row(512)] + [_full(w.shape) for w in weights] + [tab, tab],
        out_specs=[row(dm), row(qw), row(kw), row(kw), row(qw)],
        out_shape=[jax.ShapeDtypeStruct((m, dm), F32),
                   jax.ShapeDtypeStruct((m, qw), act_dtype), jax.ShapeDtypeStruct((m, kw), F32),
                   jax.ShapeDtypeStruct((m, kw), F32), jax.ShapeDtypeStruct((m, qw), act_dtype)],
        compiler_params=_cparams(("parallel",)),
        name="mid",
    )(x, y, u, ga, ob, *weights, cos_t, sin_t)


def _rope_tables(pos):
    half = SWA_HEAD_DIM // 2
    inv_freq = ROPE_THETA ** (-np.arange(half, dtype=np.float64) / half)
    ang = np.asarray(pos, np.float64)[:, None] * inv_freq[None, :]
    cos, sin = np.cos(ang), np.sin(ang)
    cos_t = np.concatenate([cos, cos, cos, cos], axis=1).astype(np.float32)
    sin_t = np.concatenate([-sin, sin, -sin, sin], axis=1).astype(np.float32)
    return jnp.asarray(cos_t), jnp.asarray(sin_t)


def _attn_prompt_kernel(sink_ref, q_ref, kc_ref, kp_ref, vc_ref, vp_ref, gate_ref, h_ref, wout_ref,
                        y_ref, p_ref, o_ref, *, nqb):
    i = pl.program_id(1)
    w = SWA_WINDOW
    hd = SWA_HEAD_DIM
    npair = SWA_GROUP // 2
    lane2 = lax.broadcasted_iota(jnp.int32, (2 * w, LANES), 1)
    low = lane2 < hd
    rr = lax.broadcasted_iota(jnp.int32, (w, w), 0)
    cc = lax.broadcasted_iota(jnp.int32, (w, w), 1)
    tri = cc <= rr
    low_w = cc < hd
    r4 = lax.broadcasted_iota(jnp.int32, (4 * w, LANES), 0)
    c4 = lax.broadcasted_iota(jnp.int32, (4 * w, LANES), 1)
    den_cols = ((r4 < 2 * w) == (c4 < hd)).astype(BF16)
    units = [(jb, kv) for jb in range(nqb) for kv in range(SWA_KV_HEADS)]

    def scores(jb, kv):
        rows = slice(jb * w, (jb + 1) * w)
        if jb == 0:
            k_prev, v_prev = kp_ref[...], vp_ref[...]
        else:
            prev_rows = slice((jb - 1) * w, jb * w)
            k_prev, v_prev = kc_ref[prev_rows, :], vc_ref[prev_rows, :]
        kcat = jnp.concatenate([k_prev, kc_ref[rows, :]], axis=0)
        vcat = jnp.concatenate([v_prev, vc_ref[rows, :]], axis=0)
        own = low if kv == 0 else jnp.logical_not(low)
        k_own = jnp.where(own, kcat, 0.0)
        v_own = jnp.where(own, vcat, 0.0)
        k_oth = pltpu.roll(k_own, hd, 1)
        v_oth = pltpu.roll(v_own, hd, 1)
        k_lo, k_hi = (k_own, k_oth) if kv == 0 else (k_oth, k_own)
        v_lo, v_hi = (v_own, v_oth) if kv == 0 else (v_oth, v_own)
        k_rhs = jnp.concatenate([k_lo, k_hi], axis=0).astype(BF16)
        v_rhs = jnp.concatenate([jnp.concatenate([v_lo, v_hi], axis=0).astype(BF16), den_cols], axis=1)
        qs = jnp.concatenate([q_ref[rows, (kv * npair + pr) * LANES:(kv * npair + pr + 1) * LANES]
                              for pr in range(npair)], axis=0)
        return _dot_nt(qs, k_rhs), v_rhs

    def softmax_pv(kv, first, buf, s_all, v_rhs):
        sink_terms = []
        for pr in range(npair):
            prow = slice(pr * w, (pr + 1) * w)
            pair_terms = []
            for e in range(2):
                s_prev = s_all[prow, (2 * e) * w:(2 * e + 1) * w]
                s_cur = s_all[prow, (2 * e + 1) * w:(2 * e + 2) * w]
                if first:
                    s_prev = jnp.where(i > 0, s_prev, -jnp.inf)
                sc = jnp.where(tri, s_cur, s_prev)
                sink = sink_ref[2 * (kv * npair + pr) + e] * LOG2E
                mx = jnp.maximum(jnp.max(sc, axis=-1, keepdims=True), sink)
                pe = jnp.exp2(sc - mx)
                p_ref[buf, prow, (2 * e) * w:(2 * e + 1) * w] = jnp.where(tri, 0.0, pe).astype(BF16)
                p_ref[buf, prow, (2 * e + 1) * w:(2 * e + 2) * w] = jnp.where(tri, pe, 0.0).astype(BF16)
                pair_terms.append(jnp.exp2(sink - mx))
            sink_terms.append(pair_terms)
        return _dot(p_ref[buf], v_rhs), sink_terms

    def normalise(jb, kv, o_ext, sink_terms):
        rows = slice(jb * w, (jb + 1) * w)
        for pr in range(npair):
            prow = slice(pr * w, (pr + 1) * w)
            cols = slice((kv * npair + pr) * LANES, (kv * npair + pr + 1) * LANES)
            st = jnp.where(low_w, sink_terms[pr][0], sink_terms[pr][1])
            o = o_ext[prow, :LANES] / (o_ext[prow, LANES:] + st)
            o_ref[rows, cols] = o.astype(o_ref.dtype)

    def project(rows):
        og = o_ref[rows, :].astype(F32) * gate_ref[rows, :].astype(F32)
        y_ref[rows, :] = h_ref[rows, :] + _dot(og.astype(BF16), wout_ref[...])

    group = 2
    pending = scores(*units[0])
    unfinished = None
    for n, (jb, kv) in enumerate(units):
        current = pending
        if n + 1 < len(units):
            pending = scores(*units[n + 1])
        result = softmax_pv(kv, jb == 0, n % 2, *current)
        if unfinished is not None:
            normalise(*unfinished)
            done_jb, done_kv = unfinished[:2]
            if done_kv == SWA_KV_HEADS - 1 and (done_jb + 1) % group == 0:
                project(slice((done_jb + 1 - group) * w, (done_jb + 1) * w))
        unfinished = (jb, kv) + result
    normalise(*unfinished)
    project(slice((nqb - group) * w, nqb * w))


def _attn_prompt(sinks, q, k, v, gate, h, w_out, nseq, seqlen):
    w = SWA_WINDOW
    nqb = 8
    tm = nqb * w
    assert seqlen % tm == 0
    nblk = seqlen // tm
    qw = q.shape[1]
    dm = h.shape[1]
    row = lambda n: pl.BlockSpec((tm, n), lambda b, i: (b * nblk + i, 0))
    prev = lambda n: pl.BlockSpec((w, n), lambda b, i: (jnp.maximum((b * nblk + i) * nqb - 1, 0), 0))
    smem = pl.BlockSpec(memory_space=pltpu.SMEM)
    return pl.pallas_call(
        functools.partial(_attn_prompt_kernel, nqb=nqb),
        grid=(nseq, nblk),
        in_specs=[smem, row(qw), row(LANES), prev(LANES), row(LANES), prev(LANES), row(qw), row(dm),
                  _full(w_out.shape)],
        out_specs=row(dm),
        out_shape=jax.ShapeDtypeStruct(h.shape, F32),
        scratch_shapes=[pltpu.VMEM((2, 4 * w, 4 * w), BF16), pltpu.VMEM((tm, qw), BF16)],
        compiler_params=_cparams(("parallel", "arbitrary")),
        name="attn_prompt",
    )(sinks, q, k, k, v, v, gate, h, w_out)


def _attn_sample_kernel(sink_ref, q_ref, kn_ref, vn_ref, kc_ref, vc_ref, o_ref, ko_ref, vo_ref, q2_ref, *, seqlen):
    nq = SWA_HEADS * seqlen
    hd = SWA_HEAD_DIM
    heads = [(kv, g) for kv in range(SWA_KV_HEADS) for g in range(SWA_GROUP)]

    def stacked(kv, g):
        r0 = (kv * SWA_GROUP + g) * seqlen
        return slice(r0, r0 + seqlen), slice(kv * hd, (kv + 1) * hd)

    q2_ref[...] = jnp.zeros(q2_ref.shape, q2_ref.dtype)
    for kv, g in heads:
        rows, lanes = stacked(kv, g)
        h = kv * SWA_GROUP + g
        q2_ref[:, rows, lanes] = q_ref[:, :, h * hd:(h + 1) * hd]
    ncache = kc_ref.shape[1]
    t_row = lax.broadcasted_iota(jnp.int32, (nq, ncache), 0) % seqlen
    c_col = lax.broadcasted_iota(jnp.int32, (nq, ncache), 1)
    cache_ok = c_col > t_row - (SWA_WINDOW - ncache)
    nnew = kn_ref.shape[1]
    t_row_n = lax.broadcasted_iota(jnp.int32, (nq, nnew), 0) % seqlen
    n_col = lax.broadcasted_iota(jnp.int32, (nq, nnew), 1)
    new_ok = n_col <= t_row_n
    sink = (sink_ref[...] * LOG2E)[None, :, 0:1]
    bqk = lambda a, b: lax.dot_general(a, b, (((2,), (2,)), ((0,), (0,))), preferred_element_type=F32)
    bpv = lambda a, b: lax.dot_general(a, b, (((2,), (1,)), ((0,), (0,))), preferred_element_type=F32)
    q = q2_ref[...].astype(BF16)
    kc = kc_ref[...]
    vc = vc_ref[...]
    kn = kn_ref[...]
    vn = vn_ref[...]
    sc = jnp.where(cache_ok[None], bqk(q, kc.astype(BF16)), -jnp.inf)
    sn = jnp.where(new_ok[None], bqk(q, kn.astype(BF16)), -jnp.inf)
    mx = jnp.maximum(jnp.maximum(jnp.max(sc, axis=-1, keepdims=True), jnp.max(sn, axis=-1, keepdims=True)), sink)
    pc = jnp.exp2(sc - mx)
    pn = jnp.exp2(sn - mx)
    den = jnp.sum(pc, axis=-1, keepdims=True) + jnp.sum(pn, axis=-1, keepdims=True) + jnp.exp2(sink - mx)
    inv = 1.0 / den
    o2 = bpv((pc * inv).astype(BF16), vc.astype(BF16)) + bpv((pn * inv).astype(BF16), vn.astype(BF16))
    for kv, g in heads:
        rows, lanes = stacked(kv, g)
        h = kv * SWA_GROUP + g
        o_ref[:, :, h * hd:(h + 1) * hd] = o2[:, rows, lanes]
    keep = ncache - seqlen
    ko_ref[:, 0:keep, :] = kc[:, seqlen:ncache, :]
    ko_ref[:, keep:ncache, :] = kn[:, 0:seqlen, :]
    vo_ref[:, 0:keep, :] = vc[:, seqlen:ncache, :]
    vo_ref[:, keep:ncache, :] = vn[:, 0:seqlen, :]


def _attn_sample(sink_rows, q, kn, vn, kc, vc):
    n, seqlen, _ = q.shape
    nseq = 16
    assert n % nseq == 0
    blk = lambda a: pl.BlockSpec((nseq,) + a.shape[1:], lambda i: (i, 0, 0))
    return pl.pallas_call(
        functools.partial(_attn_sample_kernel, seqlen=seqlen),
        grid=(n // nseq,),
        in_specs=[_full(sink_rows.shape), blk(q), blk(kn), blk(vn), blk(kc), blk(vc)],
        out_specs=[blk(q), blk(kc), blk(vc)],
        out_shape=[jax.ShapeDtypeStruct(q.shape, F32), jax.ShapeDtypeStruct(kc.shape, F32),
                   jax.ShapeDtypeStruct(vc.shape, F32)],
        scratch_shapes=[pltpu.VMEM((nseq, SWA_HEADS * seqlen, SWA_KV_HEADS * SWA_HEAD_DIM), F32)],
        compiler_params=_cparams(("parallel",)),
        name="attn_sample",
    )(sink_rows, q, kn, vn, kc, vc)


def _odd_out_kernel(h_ref, o_ref, gate_ref, w_ref, y_ref):
    og = o_ref[...].astype(F32) * gate_ref[...].astype(F32)
    y_ref[...] = h_ref[...] + _dot(og.astype(BF16), w_ref[...])


def _odd_out(h, o, gate, w):
    m, dm = h.shape
    tm = _row_tile(m)
    row = lambda n: pl.BlockSpec((tm, n), lambda i: (i, 0))
    return pl.pallas_call(
        _odd_out_kernel,
        grid=(m // tm,),
        in_specs=[row(dm), row(o.shape[1]), row(gate.shape[1]), _full(w.shape)],
        out_specs=row(dm),
        out_shape=jax.ShapeDtypeStruct((m, dm), F32),
        compiler_params=_cparams(("parallel",)),
        name="odd_out",
    )(h, o, gate, w)


def _trunk(x, s5_x0, gla_s0, pe, po, pos, nseq, seqlen, act_dtype):
    u4, u2, ga, q, k, v, la, gb = _even_in(x, pe['norm_g'], pe['wm'], pe['wa'], pe['wgb'], pe['wgate'],
                                           pe['bgate'], act_dtype, _s5_chunk(seqlen))
    y4, s5_fin = _s5_branch(u2, s5_x0, pe['s5_ops'], nseq, seqlen, act_dtype)
    ob, gla_fin = _gla(q, k, v, la, gb, gla_s0, pe['gla_norm_g'], nseq, seqlen, act_dtype)
    cos_t, sin_t = _rope_tables(pos)
    h, q1, k1, v1, gate = _mid(x, y4, u4, ga, ob, pe, po, cos_t, sin_t, act_dtype)
    return h, q1, k1, v1, gate, s5_fin, gla_fin


def kernel(x_prompt, x_sample, state_s5_re, state_s5_im, state_gla, cache_swa_k, cache_swa_v,
           even_norm_g, even_w_in, s5_lambda_re, s5_lambda_im, s5_log_dt, s5_b_re, s5_b_im,
           s5_c_re, s5_c_im, s5_d, s5_w_glu, s5_b_glu, gla_w_gate, gla_b_gate, gla_norm_g,
           even_w_out, odd_norm_g, odd_w_in, swa_q_norm_g, swa_k_norm_g, swa_sinks, odd_w_out):
    nb, seq, dm = x_prompt.shape
    ns, dseq, _ = x_sample.shape
    ng = s5_lambda_re.shape[1]
    no = ng // OCT
    xp = x_prompt.reshape(nb * seq, dm)
    xs = x_sample.reshape(ns * dseq, dm)

    i = 0
    w_in = even_w_in[i]
    s5w = ng * S5_GROUP
    col_alow = 2 * s5w + 2 * GLA_HEADS * GLA_HEAD_K + GLA_HEADS * GLA_HEAD_V
    col_gb = col_alow + GLA_GATE_RANK
    pad_rank = LANES - GLA_GATE_RANK
    pe = {
        'norm_g': even_norm_g[i][None, :],
        'wm': w_in[:, :col_alow].astype(BF16),
        'wa': jnp.pad(w_in[:, col_alow:col_gb], ((0, 0), (0, pad_rank))).astype(BF16),
        'wgb': w_in[:, col_gb:].astype(BF16),
        'wgate': jnp.pad(gla_w_gate[i], ((0, pad_rank), (0, 0))).astype(BF16),
        'bgate': gla_b_gate[i][None, :],
        's5_ops': _s5_params(s5_lambda_re[i], s5_lambda_im[i], s5_log_dt[i], s5_b_re[i], s5_b_im[i],
                             s5_c_re[i], s5_c_im[i]),
        'gla_norm_g': gla_norm_g[i][None, :],
        'd': s5_d[i][None, :],
        'wglu': s5_w_glu[i].astype(BF16),
        'bglu': s5_b_glu[i][None, :],
        'woa': even_w_out[i][:s5w].astype(BF16),
        'wob': even_w_out[i][s5w:].astype(BF16),
    }
    po = {
        'norm_g': odd_norm_g[i][None, :],
        'w_in': odd_w_in[i].astype(BF16),
        'gq': jnp.tile(swa_q_norm_g[i], MXU_TILE // SWA_HEAD_DIM)[None, :],
        'gk': jnp.tile(swa_k_norm_g[i], LANES // SWA_HEAD_DIM)[None, :],
    }
    w_out = odd_w_out[i].astype(BF16)
    sinks = swa_sinks[i]
    kvw = SWA_KV_HEADS * SWA_HEAD_DIM

    s5_zero = jnp.zeros((no, nb, 2 * OCT * S5_STATE), F32)
    gla_zero = jnp.zeros((nb, GLA_HEADS, GLA_HEAD_K, GLA_HEAD_V), F32)
    hp, q, k, v, gate, s5_p, gla_p = _trunk(xp, s5_zero, gla_zero, pe, po, np.arange(seq), nb, seq, BF16)
    s5r_p, s5i_p = _s5_state_out(s5_p)
    y_prompt = _attn_prompt(sinks, q, k, v, gate, hp, w_out, nb, seq).reshape(nb, seq, dm)
    cache_len = min(SWA_WINDOW, seq)
    tail = lambda a: (a.reshape(nb, seq, kvw)[:, seq - cache_len:]
                      .reshape(1, nb, cache_len, SWA_KV_HEADS, SWA_HEAD_DIM))
    swk_p, swv_p = tail(k), tail(v)

    pos_s = np.tile(PAST_LEN + np.arange(dseq), ns)
    s5_init = _s5_state_in(state_s5_re[i], state_s5_im[i], no)
    hs, q, k, v, gate, s5_s, gla_s = _trunk(xs, s5_init, state_gla[i], pe, po, pos_s, ns, dseq, F32)
    s5r_s, s5i_s = _s5_state_out(s5_s)
    ncache = cache_swa_k.shape[2]
    sink_rows = jnp.broadcast_to(jnp.repeat(sinks, dseq)[:, None], (SWA_HEADS * dseq, LANES))
    kn = k.reshape(ns, dseq, kvw)
    vn = v.reshape(ns, dseq, kvw)
    npad = 16 - dseq
    kn_pad = jnp.pad(kn, ((0, 0), (0, npad), (0, 0)))
    vn_pad = jnp.pad(vn, ((0, 0), (0, npad), (0, 0)))
    kc = cache_swa_k[i].reshape(ns, ncache, kvw)
    vc = cache_swa_v[i].reshape(ns, ncache, kvw)
    o, kc_new, vc_new = _attn_sample(sink_rows, q.reshape(ns, dseq, -1), kn_pad, vn_pad, kc, vc)
    y_sample = _odd_out(hs, o.reshape(ns * dseq, -1), gate, w_out).reshape(ns, dseq, dm)
    swk_s = kc_new.reshape(1, ns, ncache, SWA_KV_HEADS, SWA_HEAD_DIM)
    swv_s = vc_new.reshape(1, ns, ncache, SWA_KV_HEADS, SWA_HEAD_DIM)

    return (y_prompt, y_sample,
            s5r_p, s5i_p, gla_p[None], swk_p, swv_p,
            s5r_s, s5i_s, gla_s[None], swk_s, swv_s)
```

```python
import functools
import math

import jax
import jax.numpy as jnp
import numpy as np
from jax import lax
from jax.experimental import pallas as pl
from jax.experimental.pallas import tpu as pltpu

F32 = jnp.float32
BF16 = jnp.bfloat16

PAST_LEN = 8192
NORM_EPS = 1e-6
S5_GROUP = 16
S5_STATE = 64
S5_CHUNK = 16
GLA_HEADS = 4
GLA_HEAD_K = 64
GLA_HEAD_V = 128
GLA_GATE_RANK = 16
GLA_GATE_TAU = 16.0
GLA_CHUNK = 64
SWA_HEADS = 16
SWA_KV_HEADS = 2
SWA_GROUP = SWA_HEADS // SWA_KV_HEADS
SWA_HEAD_DIM = 64
SWA_WINDOW = 128
ROPE_THETA = 10000.0
LOG2E = math.log2(math.e)
LANES = 128
MXU_TILE = 256
OCT = LANES // S5_GROUP
ROW_TILE = 512
VMEM_LIMIT = 48 * 1024 * 1024


def _cparams(sem):
    return pltpu.CompilerParams(dimension_semantics=sem, vmem_limit_bytes=VMEM_LIMIT)


def _full(shape):
    n = len(shape)
    return pl.BlockSpec(shape, lambda *_: (0,) * n)


def _dot(a, b):
    return jnp.dot(a, b, preferred_element_type=F32)


def _dot_nt(a, b):
    return lax.dot_general(a, b, (((1,), (1,)), ((), ())), preferred_element_type=F32)


def _dot_tn(a, b):
    return lax.dot_general(a, b, (((0,), (0,)), ((), ())), preferred_element_type=F32)


def _split_bf16(x):
    hi = x.astype(BF16)
    lo = (x - hi.astype(F32)).astype(BF16)
    return hi, lo


def _rms_rows(x, g):
    return x * lax.rsqrt(jnp.mean(x * x, axis=-1, keepdims=True) + NORM_EPS) * g


def _sigmoid(x):
    return 1.0 / (1.0 + jnp.exp(-x))


def _silu(x):
    return x * _sigmoid(x)


def _row_tile(m, tile=ROW_TILE):
    return tile if m % tile == 0 else m


def _even_in_kernel(x_ref, g_ref, wm_ref, wa_ref, wgb_ref, wgate_ref, bgate_ref,
                    u_ref, u2_ref, ga_ref, q_ref, k_ref, v_ref, la_ref, gb_ref, uscr_ref, *, t):
    xb = _rms_rows(x_ref[...], g_ref[...]).astype(BF16)

    def proj(lo, hi):
        return _dot(xb, wm_ref[:, lo:hi])

    u = proj(0, 512)
    nrow = u.shape[0] // t
    for o in range(u_ref.shape[0]):
        uo = u[:, o * LANES:(o + 1) * LANES]
        u_ref[o] = uo.astype(u_ref.dtype)
        uscr_ref[o] = uo
        for tt in range(t):
            piece = uscr_ref[o, pl.ds(tt, nrow, stride=t), :]
            u2_ref[o, :, tt * LANES:(tt + 1) * LANES] = piece.astype(u2_ref.dtype)
    ga_ref[...] = _silu(proj(512, 1024)).astype(ga_ref.dtype)
    q_ref[...] = (proj(1024, 1280) * (GLA_HEAD_K ** -0.5)).astype(q_ref.dtype)
    k_ref[...] = proj(1280, 1536).astype(k_ref.dtype)
    v_ref[...] = proj(1536, 2048).astype(v_ref.dtype)
    gb_ref[...] = _silu(_dot(xb, wgb_ref[...])).astype(gb_ref.dtype)
    a_low = _dot(xb, wa_ref[...])
    logit = _dot(a_low.astype(BF16), wgate_ref[...]) + bgate_ref[...]
    log_sig = jnp.minimum(logit, 0.0) - jnp.log1p(jnp.exp(-jnp.abs(logit)))
    la_ref[...] = log_sig * (1.0 / GLA_GATE_TAU)


def _even_in(x, g, wm, wa, wgb, wgate, bgate, act_dtype, t):
    m, d = x.shape
    tm = _row_tile(m, 2 * ROW_TILE) if m > ROW_TILE else m
    row = lambda n: pl.BlockSpec((tm, n), lambda i: (i, 0))
    no = 512 // LANES
    slab = pl.BlockSpec((no, tm, LANES), lambda i: (0, i, 0))
    chunk = pl.BlockSpec((no, tm // t, t * LANES), lambda i: (0, i, 0))
    outs = [(512, act_dtype), (256, act_dtype), (256, act_dtype), (512, act_dtype), (256, F32), (512, act_dtype)]
    return pl.pallas_call(
        functools.partial(_even_in_kernel, t=t),
        grid=(m // tm,),
        in_specs=[row(d), _full(g.shape), _full(wm.shape), _full(wa.shape), _full(wgb.shape),
                  _full(wgate.shape), _full(bgate.shape)],
        out_specs=[slab, chunk] + [row(n) for n, _ in outs],
        out_shape=[jax.ShapeDtypeStruct((no, m, LANES), act_dtype),
                   jax.ShapeDtypeStruct((no, m // t, t * LANES), act_dtype)]
        + [jax.ShapeDtypeStruct((m, n), dt) for n, dt in outs],
        scratch_shapes=[pltpu.VMEM((no, tm, LANES), F32)],
        compiler_params=_cparams(("parallel",)),
        name="even_in",
    )(x, g, wm, wa, wgb, wgate, bgate)


def _group_mask(shape, row_span, col_span):
    rg = (lax.broadcasted_iota(jnp.int32, shape, 0) // row_span) % OCT
    cg = (lax.broadcasted_iota(jnp.int32, shape, 1) // col_span) % OCT
    return rg == cg


def _s5_state_kernel(u_ref, bre_ref, bim_ref, x0_ref, are_ref, aim_ref, xs_ref, xf_ref, loc_ref, *, nseq, nchunks):
    hw = OCT * S5_STATE
    kk = u_ref.shape[2]
    bp = jnp.concatenate([bre_ref[0]] * OCT + [bim_ref[0]] * OCT, axis=1)
    bp = jnp.where(_group_mask((kk, 2 * hw), S5_GROUP, S5_STATE), bp, 0.0).astype(BF16)
    loc_ref[...] = _dot(u_ref[0], bp)
    a_re = are_ref[0]
    a_im = aim_ref[0]
    if nchunks == 1:
        x0 = x0_ref[0]
        xr, xi = x0[:, :hw], x0[:, hw:]
        loc = loc_ref[...]
        xf_ref[0, :, :hw] = a_re * xr - a_im * xi + loc[:, :hw]
        xf_ref[0, :, hw:] = a_re * xi + a_im * xr + loc[:, hw:]
        xs_ref[0] = x0.astype(xs_ref.dtype)
    else:
        def body(j, carry):
            new = []
            for b in range(nseq):
                xr, xi = carry[b]
                row = pl.ds(b * nchunks + j, 1)
                lr = loc_ref[row, :hw]
                li = loc_ref[row, hw:]
                loc_ref[row, :hw] = xr
                loc_ref[row, hw:] = xi
                new.append((a_re * xr - a_im * xi + lr, a_re * xi + a_im * xr + li))
            return tuple(new)

        init = tuple((x0_ref[0, b:b + 1, :hw], x0_ref[0, b:b + 1, hw:]) for b in range(nseq))
        fin = lax.fori_loop(0, nchunks, body, init, unroll=4)
        for b in range(nseq):
            xf_ref[0, b:b + 1, :hw] = fin[b][0]
            xf_ref[0, b:b + 1, hw:] = fin[b][1]
        xs_ref[0] = loc_ref[...].astype(xs_ref.dtype)


def _s5_state(u2, bre, bim, x0, are, aim, nseq, nchunks):
    no, r, kk = u2.shape
    sw = x0.shape[2]
    blk = lambda a: pl.BlockSpec((1,) + a.shape[1:], lambda o: (o,) + (0,) * (a.ndim - 1))
    return pl.pallas_call(
        functools.partial(_s5_state_kernel, nseq=nseq, nchunks=nchunks),
        grid=(no,),
        in_specs=[blk(u2), blk(bre), blk(bim), blk(x0), blk(are), blk(aim)],
        out_specs=[pl.BlockSpec((1, r, sw), lambda o: (o, 0, 0)), blk(x0)],
        out_shape=[jax.ShapeDtypeStruct((no, r, sw), BF16), jax.ShapeDtypeStruct(x0.shape, F32)],
        scratch_shapes=[pltpu.VMEM((r, sw), F32)],
        compiler_params=_cparams(("parallel",)),
        name="s5_state",
    )(u2, bre, bim, x0, are, aim)


def _dot_nt_f32(a, b):
    a_hi, a_lo = _split_bf16(a)
    b_hi, b_lo = _split_bf16(b)
    return _dot_nt(a_hi, b_hi) + _dot_nt(a_hi, b_lo) + _dot_nt(a_lo, b_hi)


def _s5_out_kernel(u_ref, xs_ref, bre_ref, bim_ref, cre_ref, cim_ref, zre_ref, zim_ref, y_ref, yscr_ref):
    r, kk = u_ref.shape[1], u_ref.shape[2]
    t = kk // LANES
    sw = xs_ref.shape[2]
    ntile = kk // MXU_TILE
    taps = _dot_nt_f32(bre_ref[0], zre_ref[0]) - _dot_nt_f32(bim_ref[0], zim_ref[0])
    tmask = _group_mask((LANES, LANES), S5_GROUP, S5_GROUP)

    def tap(lag):
        if lag < 0:
            return jnp.zeros((LANES, LANES), F32)
        s = t - 1 - lag
        return jnp.where(tmask, taps[s * LANES:(s + 1) * LANES], 0.0)

    wts = [jnp.concatenate([jnp.concatenate([tap(2 * d), tap(2 * d + 1)], axis=1),
                            jnp.concatenate([tap(2 * d - 1), tap(2 * d)], axis=1)], axis=0).astype(BF16)
           for d in range(ntile)]
    cpt = jnp.concatenate([cre_ref[0]] * OCT + [cim_ref[0]] * OCT, axis=1)
    cpt = jnp.where(_group_mask((kk, sw), S5_GROUP, S5_STATE), cpt, 0.0).astype(BF16)
    xs = xs_ref[0]
    for n in range(ntile):
        acc = _dot_nt(xs, cpt[n * MXU_TILE:(n + 1) * MXU_TILE])
        for k in range(n + 1):
            acc = acc + _dot(u_ref[0, :, k * MXU_TILE:(k + 1) * MXU_TILE], wts[n - k])
        for e in range(MXU_TILE // LANES):
            yscr_ref[pl.ds(2 * n + e, r, stride=t), :] = acc[:, e * LANES:(e + 1) * LANES]
    y_ref[0] = yscr_ref[...].astype(y_ref.dtype)


def _s5_out(u2, xs, bre, bim, cre, cim, zre, zim, out_dtype):
    no, r, kk = u2.shape
    m = r * (kk // LANES)
    blk = lambda a: pl.BlockSpec((1,) + a.shape[1:], lambda o: (o,) + (0,) * (a.ndim - 1))
    ops = (u2, xs, bre, bim, cre, cim, zre, zim)
    return pl.pallas_call(
        _s5_out_kernel,
        grid=(no,),
        in_specs=[blk(a) for a in ops],
        out_specs=pl.BlockSpec((1, m, LANES), lambda o: (o, 0, 0)),
        out_shape=jax.ShapeDtypeStruct((no, m, LANES), out_dtype),
        scratch_shapes=[pltpu.VMEM((m, LANES), F32)],
        compiler_params=_cparams(("parallel",)),
        name="s5_out",
    )(*ops)


def _s5_params(lam_re, lam_im, log_dt, b_re, b_im, c_re, c_im):
    t = S5_CHUNK
    ng = lam_re.shape[0]
    no = ng // OCT
    dt = jnp.exp(log_dt)[:, None]
    a = lam_re * dt
    b = lam_im * dt
    n = jnp.arange(t + 1, dtype=F32)[None, :, None]
    mag = jnp.exp(n * a[:, None, :])
    pw_re = mag * jnp.cos(n * b[:, None, :])
    pw_im = mag * jnp.sin(n * b[:, None, :])
    em1_re = jnp.expm1(a) * jnp.cos(b) - 2.0 * jnp.sin(0.5 * b) ** 2
    em1_im = jnp.exp(a) * jnp.sin(b)
    den = lam_re * lam_re + lam_im * lam_im
    z_re = (em1_re * lam_re + em1_im * lam_im) / den
    z_im = (em1_im * lam_re - em1_re * lam_im) / den
    bb_re = z_re[..., None] * b_re - z_im[..., None] * b_im
    bb_im = z_re[..., None] * b_im + z_im[..., None] * b_re
    def rows(w):
        return w.reshape(no, OCT, t, S5_GROUP, S5_STATE).transpose(0, 2, 1, 3, 4).reshape(no, t * LANES, S5_STATE)

    bt_re = bb_re.transpose(0, 2, 1)[:, None]
    bt_im = bb_im.transpose(0, 2, 1)[:, None]
    r_re, r_im = pw_re[:, t - 1::-1, None, :], pw_im[:, t - 1::-1, None, :]
    bre = rows(r_re * bt_re - r_im * bt_im)
    bim = rows(r_re * bt_im + r_im * bt_re)
    o_re, o_im = pw_re[:, 1:, None, :], pw_im[:, 1:, None, :]
    cre = rows(c_re[:, None] * o_re - c_im[:, None] * o_im)
    cim = rows(-(c_re[:, None] * o_im + c_im[:, None] * o_re))
    return bre, bim, cre, cim, c_re.reshape(no, LANES, S5_STATE), c_im.reshape(no, LANES, S5_STATE), pw_re, pw_im


def _s5_chunk(seqlen):
    return math.gcd(seqlen, S5_CHUNK)


def _s5_branch(u2, x0, ops, nseq, seqlen, out_dtype):
    bre, bim, cre, cim, zre, zim, pw_re, pw_im = ops
    no = u2.shape[0]
    t = _s5_chunk(seqlen)
    kk = t * LANES
    nchunks = seqlen // t
    are = pw_re[:, t].reshape(no, 1, OCT * S5_STATE)
    aim = pw_im[:, t].reshape(no, 1, OCT * S5_STATE)
    tail = S5_CHUNK * LANES - kk
    bre, bim = bre[:, tail:], bim[:, tail:]
    xs, xf = _s5_state(u2, bre, bim, x0, are, aim, nseq, nchunks)
    return _s5_out(u2, xs, bre, bim, cre[:, :kk], cim[:, :kk], zre, zim, out_dtype), xf


def _s5_state_in(re, im, no):
    nseq = re.shape[0]
    f = lambda a: a.reshape(nseq, no, OCT * S5_STATE).transpose(1, 0, 2)
    return jnp.concatenate([f(re), f(im)], axis=-1)


def _s5_state_out(xf):
    no, nseq, _ = xf.shape
    hw = OCT * S5_STATE
    f = lambda a: a.transpose(1, 0, 2).reshape(1, nseq, no * OCT, S5_STATE)
    return f(xf[:, :, :hw]), f(xf[:, :, hw:])


def _gla_kernel(q_ref, k_ref, v_ref, la_ref, gb_ref, s0_ref, ng_ref, o_ref, sf_ref, st_ref,
                *, nseq, nchunks, c):
    i = pl.program_id(1)

    @pl.when(i == 0)
    def _():
        st_ref[...] = s0_ref[...]

    nh, hk, hv = GLA_HEADS, GLA_HEAD_K, GLA_HEAD_V
    nch = nseq * nchunks
    tm = nch * c
    iota = lambda shape, d: lax.broadcasted_iota(jnp.int32, shape, d)
    ng = ng_ref[...]

    tb = min(tm, MXU_TILE)
    rt, ct = iota((tb, tb), 0), iota((tb, tb), 1)
    tril = ((rt // c == ct // c) & (rt >= ct)).astype(BF16)
    parts = []
    for r0 in range(0, tm, tb):
        la_hi, la_lo = _split_bf16(la_ref[r0:r0 + tb, :])
        parts.append(_dot(tril, la_hi) + _dot(tril, la_lo))
    bcum = jnp.concatenate(parts, axis=0)
    e_hi, e_lo = _split_bf16(jnp.exp(jnp.concatenate([bcum[(ci + 1) * c - 1:(ci + 1) * c] for ci in range(nch)],
                                                     axis=0)))
    pick = (iota((nch, LANES), 0) == iota((nch, LANES), 1)).astype(BF16)
    dec_t = _dot_tn(e_hi, pick) + _dot_tn(e_lo, pick)
    q_all = q_ref[...].astype(F32)
    k_all = k_ref[...].astype(F32)
    q_dec_all = q_all * jnp.exp(bcum)
    k_dec_all = k_all * jnp.exp(-bcum)

    own_k = iota((nh * c, nh * hk), 0) // c == iota((nh * c, nh * hk), 1) // hk
    own_v = iota((nh * c, nh * hv), 0) // c == iota((nh * c, nh * hv), 1) // hv
    causal = iota((c, nh * c), 1) % c <= iota((c, nh * c), 0)
    zero_v = jnp.zeros((hk, hv), F32)

    def intra(ci):
        rows = slice(ci * c, (ci + 1) * c)
        q_dec = q_dec_all[rows].astype(BF16)
        k_dec = k_dec_all[rows]
        b_c = bcum[rows]
        k_tail = (k_all[rows] * jnp.exp(b_c[c - 1:c] - b_c)).astype(BF16)
        v = v_ref[rows, :].astype(F32)
        k_bd = jnp.where(own_k, jnp.concatenate([k_dec] * nh, axis=0), 0.0).astype(BF16)
        v_bd = jnp.where(own_v, jnp.concatenate([v] * nh, axis=0), 0.0).astype(BF16)
        att = jnp.where(causal, _dot_nt(q_dec, k_bd), 0.0)
        o_intra = _dot(att.astype(BF16), v_bd)
        vb = v.astype(BF16)
        kvs = []
        for h0 in range(0, nh, 2):
            kv2 = _dot_tn(k_tail[:, h0 * hk:(h0 + 2) * hk], vb[:, h0 * hv:(h0 + 2) * hv])
            kvs += [kv2[:hk, :hv], kv2[hk:, hv:]]
        return q_dec, o_intra, kvs

    def carry(ci, sts, q_dec, o_intra, kvs):
        st_bd = jnp.concatenate(
            [jnp.concatenate([zero_v] * h + [sts[h]] + [zero_v] * (nh - 1 - h), axis=1) for h in range(nh)],
            axis=0).astype(BF16)
        o = o_intra + _dot(q_dec, st_bd)
        new = [dec_t[h * hk:(h + 1) * hk, ci:ci + 1] * sts[h] + kvs[h] for h in range(nh)]
        return o, new

    def finish(ci, o):
        rows = slice(ci * c, (ci + 1) * c)
        for h in range(nh):
            vs = slice(h * hv, (h + 1) * hv)
            oh = _rms_rows(o[:, vs], ng) * gb_ref[rows, vs].astype(F32)
            o_ref[rows, vs] = oh.astype(o_ref.dtype)

    sts = None
    pending = intra(0)
    unfinished = None
    for ci in range(nch):
        s, first, last = ci // nchunks, ci % nchunks == 0, ci % nchunks == nchunks - 1
        current = pending
        if ci + 1 < nch:
            pending = intra(ci + 1)
        if first:
            sts = [st_ref[s, h] for h in range(nh)]
        o, sts = carry(ci, sts, *current)
        if last:
            for h in range(nh):
                st_ref[s, h] = sts[h]
        if unfinished is not None:
            finish(*unfinished)
        unfinished = (ci, o)
    finish(*unfinished)

    @pl.when(i == pl.num_programs(1) - 1)
    def _():
        sf_ref[...] = st_ref[...]


def _gla(q, k, v, la, gb, s0, ng, nseq_total, seqlen, out_dtype):
    c = math.gcd(seqlen, GLA_CHUNK)
    if seqlen >= 16 * c:
        nseq, nchunks = 1, 16
    else:
        nseq, nchunks = 16, seqlen // c
    assert nseq_total % nseq == 0 and seqlen % (nchunks * c) == 0
    nblk = seqlen // (nchunks * c)
    tm = nseq * nchunks * c
    row = lambda n: pl.BlockSpec((tm, n), lambda b, i: (b * nblk + i, 0))
    st_spec = pl.BlockSpec((nseq, GLA_HEADS, GLA_HEAD_K, GLA_HEAD_V), lambda b, i: (b, 0, 0, 0))
    m = q.shape[0]
    return pl.pallas_call(
        functools.partial(_gla_kernel, nseq=nseq, nchunks=nchunks, c=c),
        grid=(nseq_total // nseq, nblk),
        in_specs=[row(256), row(256), row(512), row(256), row(512), st_spec, _full(ng.shape)],
        out_specs=[row(512), st_spec],
        out_shape=[jax.ShapeDtypeStruct((m, 512), out_dtype), jax.ShapeDtypeStruct(s0.shape, F32)],
        scratch_shapes=[pltpu.VMEM((nseq, GLA_HEADS, GLA_HEAD_K, GLA_HEAD_V), F32)],
        compiler_params=_cparams(("parallel", "arbitrary")),
        name="gla",
    )(q, k, v, la, gb, s0, ng)


def _gelu_tanh(x):
    return 0.5 * x * (1.0 + jnp.tanh(math.sqrt(2.0 / math.pi) * (x + 0.044715 * (x * x * x))))


def _head_ones(n):
    r = lax.broadcasted_iota(jnp.int32, (n, n), 0) // SWA_HEAD_DIM
    c = lax.broadcasted_iota(jnp.int32, (n, n), 1) // SWA_HEAD_DIM
    return (r == c).astype(BF16)


def _rope_block(x, cos_t, sin_t, upper):
    swapped = jnp.where(upper, pltpu.roll(x, 32, 1), pltpu.roll(x, 96, 1))
    return x * cos_t + swapped * sin_t


def _mid_kernel(x_ref, y_ref, u_ref, ga_ref, ob_ref, d_ref, wglu_ref, bglu_ref, woa_ref, wob_ref,
                g_ref, w_ref, gq_ref, gk_ref, cos_ref, sin_ref, h_ref, q_ref, k_ref, v_ref, gate_ref):
    tm = x_ref.shape[0]
    lane = lax.broadcasted_iota(jnp.int32, (tm, LANES), 1)
    upper = (lane & 32) != 0
    ones4 = _head_ones(MXU_TILE)
    inv_d = 1.0 / SWA_HEAD_DIM
    qw = SWA_HEADS * SWA_HEAD_DIM
    kw = SWA_KV_HEADS * SWA_HEAD_DIM
    nblk = qw // MXU_TILE
    no = y_ref.shape[0]

    mix_b = _dot(ob_ref[...].astype(BF16), wob_ref[...])
    y = jnp.concatenate([y_ref[o].astype(F32) for o in range(no)], axis=1)
    u = jnp.concatenate([u_ref[o].astype(F32) for o in range(no)], axis=1)
    z = _gelu_tanh(y + d_ref[...] * u)
    z = z * _sigmoid(_dot(z.astype(BF16), wglu_ref[...]) + bglu_ref[...])
    out_a = z * ga_ref[...].astype(F32)
    h = x_ref[...] + (_dot(out_a.astype(BF16), woa_ref[...]) + mix_b)
    h_ref[...] = h

    xb = _rms_rows(h, g_ref[...]).astype(BF16)
    cos_t = cos_ref[...]
    sin_t = sin_ref[...]

    def gate_finish(gate):
        gate_ref[...] = _silu(gate).astype(gate_ref.dtype)

    def q_finish(j, q):
        ss = _dot((q * q).astype(BF16), ones4)
        qn = q * lax.rsqrt(ss * inv_d + NORM_EPS) * gq_ref[...]
        for e in range(MXU_TILE // LANES):
            cols = slice(j * MXU_TILE + e * LANES, j * MXU_TILE + (e + 1) * LANES)
            qe = _rope_block(qn[:, e * LANES:(e + 1) * LANES], cos_t, sin_t, upper)
            q_ref[:, cols] = (qe * (SWA_HEAD_DIM ** -0.5 * LOG2E)).astype(q_ref.dtype)

    def kv_finish(kv):
        k = kv[:, :kw]
        ss = _dot((k * k).astype(BF16), ones4[:kw, :kw])
        kn = k * lax.rsqrt(ss * inv_d + NORM_EPS) * gk_ref[...]
        k_ref[...] = _rope_block(kn, cos_t, sin_t, upper)
        v_ref[...] = kv[:, kw:]

    work = [(slice(j * MXU_TILE, (j + 1) * MXU_TILE), functools.partial(q_finish, j)) for j in range(nblk)]
    work += [(slice(qw, qw + 2 * kw), kv_finish)]
    work += [(slice(qw + 2 * kw, None), gate_finish)]
    pending = _dot(xb, w_ref[:, work[0][0]])
    for n, (_, finish) in enumerate(work):
        current = pending
        if n + 1 < len(work):
            pending = _dot(xb, w_ref[:, work[n + 1][0]])
        finish(current)


def _mid(x, y, u, ga, ob, pe, po, cos_t, sin_t, act_dtype):
    m, dm = x.shape
    tm = _row_tile(m)
    row = lambda n: pl.BlockSpec((tm, n), lambda i: (i, 0))
    slab = pl.BlockSpec((y.shape[0], tm, LANES), lambda i: (0, i, 0))
    assert cos_t.shape[0] % tm == 0
    nper = cos_t.shape[0] // tm
    tab = pl.BlockSpec((tm, LANES), lambda i: (i % nper, 0))
    qw = SWA_HEADS * SWA_HEAD_DIM
    kw = SWA_KV_HEADS * SWA_HEAD_DIM
    weights = [pe['d'], pe['wglu'], pe['bglu'], pe['woa'], pe['wob'], po['norm_g'], po['w_in'], po['gq'], po['gk']]
    return pl.pallas_call(
        _mid_kernel,
        grid=(m // tm,),
        in_specs=[row(dm), slab, slab, row(512), row(512)] + [_full(w.shape) for w in weights] + [tab, tab],
        out_specs=[row(dm), row(qw), row(kw), row(kw), row(qw)],
        out_shape=[jax.ShapeDtypeStruct((m, dm), F32),
                   jax.ShapeDtypeStruct((m, qw), act_dtype), jax.ShapeDtypeStruct((m, kw), F32),
                   jax.ShapeDtypeStruct((m, kw), F32), jax.ShapeDtypeStruct((m, qw), act_dtype)],
        compiler_params=_cparams(("parallel",)),
        name="mid",
    )(x, y, u, ga, ob, *weights, cos_t, sin_t)


def _rope_tables(pos):
    half = SWA_HEAD_DIM // 2
    inv_freq = ROPE_THETA ** (-np.arange(half, dtype=np.float64) / half)
    ang = np.asarray(pos, np.float64)[:, None] * inv_freq[None, :]
    cos, sin = np.cos(ang), np.sin(ang)
    cos_t = np.concatenate([cos, cos, cos, cos], axis=1).astype(np.float32)
    sin_t = np.concatenate([-sin, sin, -sin, sin], axis=1).astype(np.float32)
    return jnp.asarray(cos_t), jnp.asarray(sin_t)


def _attn_prompt_kernel(sink_ref, q_ref, kc_ref, kp_ref, vc_ref, vp_ref, gate_ref, h_ref, wout_ref,
                        y_ref, p_ref, o_ref, *, nqb):
    i = pl.program_id(1)
    w = SWA_WINDOW
    hd = SWA_HEAD_DIM
    npair = SWA_GROUP // 2
    lane2 = lax.broadcasted_iota(jnp.int32, (2 * w, LANES), 1)
    low = lane2 < hd
    rr = lax.broadcasted_iota(jnp.int32, (w, w), 0)
    cc = lax.broadcasted_iota(jnp.int32, (w, w), 1)
    tri = cc <= rr
    low_w = cc < hd
    r4 = lax.broadcasted_iota(jnp.int32, (4 * w, LANES), 0)
    c4 = lax.broadcasted_iota(jnp.int32, (4 * w, LANES), 1)
    den_cols = ((r4 < 2 * w) == (c4 < hd)).astype(BF16)
    units = [(jb, kv) for jb in range(nqb) for kv in range(SWA_KV_HEADS)]

    def scores(jb, kv):
        rows = slice(jb * w, (jb + 1) * w)
        if jb == 0:
            k_prev, v_prev = kp_ref[...], vp_ref[...]
        else:
            prev_rows = slice((jb - 1) * w, jb * w)
            k_prev, v_prev = kc_ref[prev_rows, :], vc_ref[prev_rows, :]
        kcat = jnp.concatenate([k_prev, kc_ref[rows, :]], axis=0)
        vcat = jnp.concatenate([v_prev, vc_ref[rows, :]], axis=0)
        own = low if kv == 0 else jnp.logical_not(low)
        k_own = jnp.where(own, kcat, 0.0)
        v_own = jnp.where(own, vcat, 0.0)
        k_oth = pltpu.roll(k_own, hd, 1)
        v_oth = pltpu.roll(v_own, hd, 1)
        k_lo, k_hi = (k_own, k_oth) if kv == 0 else (k_oth, k_own)
        v_lo, v_hi = (v_own, v_oth) if kv == 0 else (v_oth, v_own)
        k_rhs = jnp.concatenate([k_lo, k_hi], axis=0).astype(BF16)
        v_rhs = jnp.concatenate([jnp.concatenate([v_lo, v_hi], axis=0).astype(BF16), den_cols], axis=1)
        qs = jnp.concatenate([q_ref[rows, (kv * npair + pr) * LANES:(kv * npair + pr + 1) * LANES]
                              for pr in range(npair)], axis=0)
        return _dot_nt(qs, k_rhs), v_rhs

    def softmax_pv(kv, first, buf, s_all, v_rhs):
        sink_terms = []
        for pr in range(npair):
            prow = slice(pr * w, (pr + 1) * w)
            pair_terms = []
            for e in range(2):
                s_prev = s_all[prow, (2 * e) * w:(2 * e + 1) * w]
                s_cur = s_all[prow, (2 * e + 1) * w:(2 * e + 2) * w]
                if first:
                    s_prev = jnp.where(i > 0, s_prev, -jnp.inf)
                sc = jnp.where(tri, s_cur, s_prev)
                sink = sink_ref[2 * (kv * npair + pr) + e] * LOG2E
                mx = jnp.maximum(jnp.max(sc, axis=-1, keepdims=True), sink)
                pe = jnp.exp2(sc - mx)
                p_ref[buf, prow, (2 * e) * w:(2 * e + 1) * w] = jnp.where(tri, 0.0, pe).astype(BF16)
                p_ref[buf, prow, (2 * e + 1) * w:(2 * e + 2) * w] = jnp.where(tri, pe, 0.0).astype(BF16)
                pair_terms.append(jnp.exp2(sink - mx))
            sink_terms.append(pair_terms)
        return _dot(p_ref[buf], v_rhs), sink_terms

    def normalise(jb, kv, o_ext, sink_terms):
        rows = slice(jb * w, (jb + 1) * w)
        for pr in range(npair):
            prow = slice(pr * w, (pr + 1) * w)
            cols = slice((kv * npair + pr) * LANES, (kv * npair + pr + 1) * LANES)
            st = jnp.where(low_w, sink_terms[pr][0], sink_terms[pr][1])
            o = o_ext[prow, :LANES] / (o_ext[prow, LANES:] + st)
            o_ref[rows, cols] = o.astype(o_ref.dtype)

    def project(rows):
        og = o_ref[rows, :].astype(F32) * gate_ref[rows, :].astype(F32)
        y_ref[rows, :] = h_ref[rows, :] + _dot(og.astype(BF16), wout_ref[...])

    group = 2
    pending = scores(*units[0])
    unfinished = None
    for n, (jb, kv) in enumerate(units):
        current = pending
        if n + 1 < len(units):
            pending = scores(*units[n + 1])
        result = softmax_pv(kv, jb == 0, n % 2, *current)
        if unfinished is not None:
            normalise(*unfinished)
            done_jb, done_kv = unfinished[:2]
            if done_kv == SWA_KV_HEADS - 1 and (done_jb + 1) % group == 0:
                project(slice((done_jb + 1 - group) * w, (done_jb + 1) * w))
        unfinished = (jb, kv) + result
    normalise(*unfinished)
    project(slice((nqb - group) * w, nqb * w))


def _attn_prompt(sinks, q, k, v, gate, h, w_out, nseq, seqlen):
    w = SWA_WINDOW
    nqb = 8
    tm = nqb * w
    assert seqlen % tm == 0
    nblk = seqlen // tm
    qw = q.shape[1]
    dm = h.shape[1]
    row = lambda n: pl.BlockSpec((tm, n), lambda b, i: (b * nblk + i, 0))
    prev = lambda n: pl.BlockSpec((w, n), lambda b, i: (jnp.maximum((b * nblk + i) * nqb - 1, 0), 0))
    smem = pl.BlockSpec(memory_space=pltpu.SMEM)
    return pl.pallas_call(
        functools.partial(_attn_prompt_kernel, nqb=nqb),
        grid=(nseq, nblk),
        in_specs=[smem, row(qw), row(LANES), prev(LANES), row(LANES), prev(LANES), row(qw), row(dm),
                  _full(w_out.shape)],
        out_specs=row(dm),
        out_shape=jax.ShapeDtypeStruct(h.shape, F32),
        scratch_shapes=[pltpu.VMEM((2, 4 * w, 4 * w), BF16), pltpu.VMEM((tm, qw), BF16)],
        compiler_params=_cparams(("parallel", "arbitrary")),
        name="attn_prompt",
    )(sinks, q, k, k, v, v, gate, h, w_out)


def _attn_sample_kernel(sink_ref, q_ref, kn_ref, vn_ref, kc_ref, vc_ref, o_ref, ko_ref, vo_ref, q2_ref, *, seqlen):
    nq = SWA_HEADS * seqlen
    hd = SWA_HEAD_DIM
    heads = [(kv, g) for kv in range(SWA_KV_HEADS) for g in range(SWA_GROUP)]

    def stacked(kv, g):
        r0 = (kv * SWA_GROUP + g) * seqlen
        return slice(r0, r0 + seqlen), slice(kv * hd, (kv + 1) * hd)

    q2_ref[...] = jnp.zeros(q2_ref.shape, q2_ref.dtype)
    for kv, g in heads:
        rows, lanes = stacked(kv, g)
        h = kv * SWA_GROUP + g
        q2_ref[:, rows, lanes] = q_ref[:, :, h * hd:(h + 1) * hd]
    ncache = kc_ref.shape[1]
    t_row = lax.broadcasted_iota(jnp.int32, (nq, ncache), 0) % seqlen
    c_col = lax.broadcasted_iota(jnp.int32, (nq, ncache), 1)
    cache_ok = c_col > t_row - (SWA_WINDOW - ncache)
    nnew = kn_ref.shape[1]
    t_row_n = lax.broadcasted_iota(jnp.int32, (nq, nnew), 0) % seqlen
    n_col = lax.broadcasted_iota(jnp.int32, (nq, nnew), 1)
    new_ok = n_col <= t_row_n
    sink = (sink_ref[...] * LOG2E)[None, :, 0:1]
    bqk = lambda a, b: lax.dot_general(a, b, (((2,), (2,)), ((0,), (0,))), preferred_element_type=F32)
    bpv = lambda a, b: lax.dot_general(a, b, (((2,), (1,)), ((0,), (0,))), preferred_element_type=F32)
    q = q2_ref[...].astype(BF16)
    kc = kc_ref[...]
    vc = vc_ref[...]
    kn = kn_ref[...]
    vn = vn_ref[...]
    sc = jnp.where(cache_ok[None], bqk(q, kc.astype(BF16)), -jnp.inf)
    sn = jnp.where(new_ok[None], bqk(q, kn.astype(BF16)), -jnp.inf)
    mx = jnp.maximum(jnp.maximum(jnp.max(sc, axis=-1, keepdims=True), jnp.max(sn, axis=-1, keepdims=True)), sink)
    pc = jnp.exp2(sc - mx)
    pn = jnp.exp2(sn - mx)
    den = jnp.sum(pc, axis=-1, keepdims=True) + jnp.sum(pn, axis=-1, keepdims=True) + jnp.exp2(sink - mx)
    inv = 1.0 / den
    o2 = bpv((pc * inv).astype(BF16), vc.astype(BF16)) + bpv((pn * inv).astype(BF16), vn.astype(BF16))
    for kv, g in heads:
        rows, lanes = stacked(kv, g)
        h = kv * SWA_GROUP + g
        o_ref[:, :, h * hd:(h + 1) * hd] = o2[:, rows, lanes]
    keep = ncache - seqlen
    ko_ref[:, 0:keep, :] = kc[:, seqlen:ncache, :]
    ko_ref[:, keep:ncache, :] = kn[:, 0:seqlen, :]
    vo_ref[:, 0:keep, :] = vc[:, seqlen:ncache, :]
    vo_ref[:, keep:ncache, :] = vn[:, 0:seqlen, :]


def _attn_sample(sink_rows, q, kn, vn, kc, vc):
    n, seqlen, _ = q.shape
    nseq = 16
    assert n % nseq == 0
    blk = lambda a: pl.BlockSpec((nseq,) + a.shape[1:], lambda i: (i, 0, 0))
    return pl.pallas_call(
        functools.partial(_attn_sample_kernel, seqlen=seqlen),
        grid=(n // nseq,),
        in_specs=[_full(sink_rows.shape), blk(q), blk(kn), blk(vn), blk(kc), blk(vc)],
        out_specs=[blk(q), blk(kc), blk(vc)],
        out_shape=[jax.ShapeDtypeStruct(q.shape, F32), jax.ShapeDtypeStruct(kc.shape, F32),
                   jax.ShapeDtypeStruct(vc.shape, F32)],
        scratch_shapes=[pltpu.VMEM((nseq, SWA_HEADS * seqlen, SWA_KV_HEADS * SWA_HEAD_DIM), F32)],
        compiler_params=_cparams(("parallel",)),
        name="attn_sample",
    )(sink_rows, q, kn, vn, kc, vc)


def _odd_out_kernel(h_ref, o_ref, gate_ref, w_ref, y_ref):
    og = o_ref[...].astype(F32) * gate_ref[...].astype(F32)
    y_ref[...] = h_ref[...] + _dot(og.astype(BF16), w_ref[...])


def _odd_out(h, o, gate, w):
    m, dm = h.shape
    tm = _row_tile(m)
    row = lambda n: pl.BlockSpec((tm, n), lambda i: (i, 0))
    return pl.pallas_call(
        _odd_out_kernel,
        grid=(m // tm,),
        in_specs=[row(dm), row(o.shape[1]), row(gate.shape[1]), _full(w.shape)],
        out_specs=row(dm),
        out_shape=jax.ShapeDtypeStruct((m, dm), F32),
        compiler_params=_cparams(("parallel",)),
        name="odd_out",
    )(h, o, gate, w)


def _trunk(x, s5_x0, gla_s0, pe, po, pos, nseq, seqlen, act_dtype):
    u4, u2, ga, q, k, v, la, gb = _even_in(x, pe['norm_g'], pe['wm'], pe['wa'], pe['wgb'], pe['wgate'],
                                           pe['bgate'], act_dtype, _s5_chunk(seqlen))
    y4, s5_fin = _s5_branch(u2, s5_x0, pe['s5_ops'], nseq, seqlen, act_dtype)
    ob, gla_fin = _gla(q, k, v, la, gb, gla_s0, pe['gla_norm_g'], nseq, seqlen, act_dtype)
    cos_t, sin_t = _rope_tables(pos)
    h, q1, k1, v1, gate = _mid(x, y4, u4, ga, ob, pe, po, cos_t, sin_t, act_dtype)
    return h, q1, k1, v1, gate, s5_fin, gla_fin


def kernel(x_prompt, x_sample, state_s5_re, state_s5_im, state_gla, cache_swa_k, cache_swa_v,
           even_norm_g, even_w_in, s5_lambda_re, s5_lambda_im, s5_log_dt, s5_b_re, s5_b_im,
           s5_c_re, s5_c_im, s5_d, s5_w_glu, s5_b_glu, gla_w_gate, gla_b_gate, gla_norm_g,
           even_w_out, odd_norm_g, odd_w_in, swa_q_norm_g, swa_k_norm_g, swa_sinks, odd_w_out):
    nb, seq, dm = x_prompt.shape
    ns, dseq, _ = x_sample.shape
    ng = s5_lambda_re.shape[1]
    no = ng // OCT
    xp = x_prompt.reshape(nb * seq, dm)
    xs = x_sample.reshape(ns * dseq, dm)

    i = 0
    w_in = even_w_in[i]
    s5w = ng * S5_GROUP
    col_alow = 2 * s5w + 2 * GLA_HEADS * GLA_HEAD_K + GLA_HEADS * GLA_HEAD_V
    col_gb = col_alow + GLA_GATE_RANK
    pad_rank = LANES - GLA_GATE_RANK
    pe = {
        'norm_g': even_norm_g[i][None, :],
        'wm': w_in[:, :col_alow].astype(BF16),
        'wa': jnp.pad(w_in[:, col_alow:col_gb], ((0, 0), (0, pad_rank))).astype(BF16),
        'wgb': w_in[:, col_gb:].astype(BF16),
        'wgate': jnp.pad(gla_w_gate[i], ((0, pad_rank), (0, 0))).astype(BF16),
        'bgate': gla_b_gate[i][None, :],
        's5_ops': _s5_params(s5_lambda_re[i], s5_lambda_im[i], s5_log_dt[i], s5_b_re[i], s5_b_im[i],
                             s5_c_re[i], s5_c_im[i]),
        'gla_norm_g': gla_norm_g[i][None, :],
        'd': s5_d[i][None, :],
        'wglu': s5_w_glu[i].astype(BF16),
        'bglu': s5_b_glu[i][None, :],
        'woa': even_w_out[i][:s5w].astype(BF16),
        'wob': even_w_out[i][s5w:].astype(BF16),
    }
    po = {
        'norm_g': odd_norm_g[i][None, :],
        'w_in': odd_w_in[i].astype(BF16),
        'gq': jnp.tile(swa_q_norm_g[i], MXU_TILE // SWA_HEAD_DIM)[None, :],
        'gk': jnp.tile(swa_k_norm_g[i], LANES // SWA_HEAD_DIM)[None, :],
    }
    w_out = odd_w_out[i].astype(BF16)
    sinks = swa_sinks[i]
    kvw = SWA_KV_HEADS * SWA_HEAD_DIM

    s5_zero = jnp.zeros((no, nb, 2 * OCT * S5_STATE), F32)
    gla_zero = jnp.zeros((nb, GLA_HEADS, GLA_HEAD_K, GLA_HEAD_V), F32)
    hp, q, k, v, gate, s5_p, gla_p = _trunk(xp, s5_zero, gla_zero, pe, po, np.arange(seq), nb, seq, BF16)
    s5r_p, s5i_p = _s5_state_out(s5_p)
    y_prompt = _attn_prompt(sinks, q, k, v, gate, hp, w_out, nb, seq).reshape(nb, seq, dm)
    cache_len = min(SWA_WINDOW, seq)
    tail = lambda a: (a.reshape(nb, seq, kvw)[:, seq - cache_len:]
                      .reshape(1, nb, cache_len, SWA_KV_HEADS, SWA_HEAD_DIM))
    swk_p, swv_p = tail(k), tail(v)

    pos_s = np.tile(PAST_LEN + np.arange(dseq), ns)
    s5_init = _s5_state_in(state_s5_re[i], state_s5_im[i], no)
    hs, q, k, v, gate, s5_s, gla_s = _trunk(xs, s5_init, state_gla[i], pe, po, pos_s, ns, dseq, F32)
    s5r_s, s5i_s = _s5_state_out(s5_s)
    ncache = cache_swa_k.shape[2]
    sink_rows = jnp.broadcast_to(jnp.repeat(sinks, dseq)[:, None], (SWA_HEADS * dseq, LANES))
    kn = k.reshape(ns, dseq, kvw)
    vn = v.reshape(ns, dseq, kvw)
    npad = 16 - dseq
    kn_pad = jnp.pad(kn, ((0, 0), (0, npad), (0, 0)))
    vn_pad = jnp.pad(vn, ((0, 0), (0, npad), (0, 0)))
    kc = cache_swa_k[i].reshape(ns, ncache, kvw)
    vc = cache_swa_v[i].reshape(ns, ncache, kvw)
    o, kc_new, vc_new = _attn_sample(sink_rows, q.reshape(ns, dseq, -1), kn_pad, vn_pad, kc, vc)
    y_sample = _odd_out(hs, o.reshape(ns * dseq, -1), gate, w_out).reshape(ns, dseq, dm)
    swk_s = kc_new.reshape(1, ns, ncache, SWA_KV_HEADS, SWA_HEAD_DIM)
    swv_s = vc_new.reshape(1, ns, ncache, SWA_KV_HEADS, SWA_HEAD_DIM)

    return (y_prompt, y_sample,
            s5r_p, s5i_p, gla_p[None], swk_p, swv_p,
            s5r_s, s5i_s, gla_s[None], swk_s, swv_s)
```

```python
import functools
import math

import jax
import jax.numpy as jnp
import numpy as np
from jax import lax
from jax.experimental import pallas as pl
from jax.experimental.pallas import tpu as pltpu

F32 = jnp.float32
BF16 = jnp.bfloat16

PAST_LEN = 8192
NORM_EPS = 1e-6
S5_GROUP = 16
S5_STATE = 64
S5_CHUNK = 16
GLA_HEADS = 4
GLA_HEAD_K = 64
GLA_HEAD_V = 128
GLA_GATE_RANK = 16
GLA_GATE_TAU = 16.0
GLA_CHUNK = 64
SWA_HEADS = 16
SWA_KV_HEADS = 2
SWA_GROUP = SWA_HEADS // SWA_KV_HEADS
SWA_HEAD_DIM = 64
SWA_WINDOW = 128
ROPE_THETA = 10000.0
LOG2E = math.log2(math.e)
LANES = 128
MXU_TILE = 256
OCT = LANES // S5_GROUP
ROW_TILE = 512
VMEM_LIMIT = 48 * 1024 * 1024


def _cparams(sem):
    return pltpu.CompilerParams(dimension_semantics=sem, vmem_limit_bytes=VMEM_LIMIT)


def _full(shape):
    n = len(shape)
    return pl.BlockSpec(shape, lambda *_: (0,) * n)


def _dot(a, b):
    return jnp.dot(a, b, preferred_element_type=F32)


def _dot_nt(a, b):
    return lax.dot_general(a, b, (((1,), (1,)), ((), ())), preferred_element_type=F32)


def _dot_tn(a, b):
    return lax.dot_general(a, b, (((0,), (0,)), ((), ())), preferred_element_type=F32)


def _split_bf16(x):
    hi = x.astype(BF16)
    lo = (x - hi.astype(F32)).astype(BF16)
    return hi, lo


def _rms_rows(x, g):
    return x * lax.rsqrt(jnp.mean(x * x, axis=-1, keepdims=True) + NORM_EPS) * g


def _sigmoid(x):
    return 1.0 / (1.0 + jnp.exp(-x))


def _silu(x):
    return x * _sigmoid(x)


def _row_tile(m, tile=ROW_TILE):
    return tile if m % tile == 0 else m


def _even_in_kernel(x_ref, g_ref, w_ref, wgate_ref, bgate_ref,
                    u_ref, u2_ref, ga_ref, q_ref, k_ref, v_ref, la_ref, gb_ref, uscr_ref, *, t):
    xb = _rms_rows(x_ref[...], g_ref[...]).astype(BF16)

    def proj(lo, hi):
        return _dot(xb, w_ref[:, lo:hi])

    u = proj(0, 512)
    nrow = u.shape[0] // t
    for o in range(u_ref.shape[0]):
        uo = u[:, o * LANES:(o + 1) * LANES]
        u_ref[o] = uo.astype(u_ref.dtype)
        uscr_ref[o] = uo
        for tt in range(t):
            piece = uscr_ref[o, pl.ds(tt, nrow, stride=t), :]
            u2_ref[o, :, tt * LANES:(tt + 1) * LANES] = piece.astype(u2_ref.dtype)
    ga_ref[...] = _silu(proj(512, 1024)).astype(ga_ref.dtype)
    q_ref[...] = (proj(1024, 1280) * (GLA_HEAD_K ** -0.5)).astype(q_ref.dtype)
    k_ref[...] = proj(1280, 1536).astype(k_ref.dtype)
    v_ref[...] = proj(1536, 2048).astype(v_ref.dtype)
    gb_ref[...] = _silu(proj(2048 + LANES, 2560 + LANES)).astype(gb_ref.dtype)
    a_low = proj(2048, 2048 + LANES)
    logit = _dot(a_low.astype(BF16), wgate_ref[...]) + bgate_ref[...]
    log_sig = jnp.minimum(logit, 0.0) - jnp.log1p(jnp.exp(-jnp.abs(logit)))
    la_ref[...] = log_sig * (1.0 / GLA_GATE_TAU)


def _even_in(x, g, w, wgate, bgate, act_dtype, t):
    m, d = x.shape
    tm = _row_tile(m, 2 * ROW_TILE) if m > ROW_TILE else m
    row = lambda n: pl.BlockSpec((tm, n), lambda i: (i, 0))
    no = 512 // LANES
    slab = pl.BlockSpec((no, tm, LANES), lambda i: (0, i, 0))
    chunk = pl.BlockSpec((no, tm // t, t * LANES), lambda i: (0, i, 0))
    outs = [(512, act_dtype), (256, act_dtype), (256, act_dtype), (512, act_dtype), (256, F32), (512, act_dtype)]
    return pl.pallas_call(
        functools.partial(_even_in_kernel, t=t),
        grid=(m // tm,),
        in_specs=[row(d), _full(g.shape), _full(w.shape), _full(wgate.shape), _full(bgate.shape)],
        out_specs=[slab, chunk] + [row(n) for n, _ in outs],
        out_shape=[jax.ShapeDtypeStruct((no, m, LANES), act_dtype),
                   jax.ShapeDtypeStruct((no, m // t, t * LANES), act_dtype)]
        + [jax.ShapeDtypeStruct((m, n), dt) for n, dt in outs],
        scratch_shapes=[pltpu.VMEM((no, tm, LANES), F32)],
        compiler_params=_cparams(("parallel",)),
        name="even_in",
    )(x, g, w, wgate, bgate)


def _group_mask(shape, row_span, col_span):
    rg = (lax.broadcasted_iota(jnp.int32, shape, 0) // row_span) % OCT
    cg = (lax.broadcasted_iota(jnp.int32, shape, 1) // col_span) % OCT
    return rg == cg


def _s5_state_kernel(u_ref, bre_ref, bim_ref, x0_ref, are_ref, aim_ref, xs_ref, xf_ref, loc_ref, *, nseq, nchunks):
    hw = OCT * S5_STATE
    kk = u_ref.shape[2]
    bp = jnp.concatenate([bre_ref[0]] * OCT + [bim_ref[0]] * OCT, axis=1)
    bp = jnp.where(_group_mask((kk, 2 * hw), S5_GROUP, S5_STATE), bp, 0.0).astype(BF16)
    loc_ref[...] = _dot(u_ref[0], bp)
    a_re = are_ref[0]
    a_im = aim_ref[0]
    if nchunks == 1:
        x0 = x0_ref[0]
        xr, xi = x0[:, :hw], x0[:, hw:]
        loc = loc_ref[...]
        xf_ref[0, :, :hw] = a_re * xr - a_im * xi + loc[:, :hw]
        xf_ref[0, :, hw:] = a_re * xi + a_im * xr + loc[:, hw:]
        xs_ref[0] = x0.astype(xs_ref.dtype)
    else:
        def body(j, carry):
            new = []
            for b in range(nseq):
                xr, xi = carry[b]
                row = pl.ds(b * nchunks + j, 1)
                lr = loc_ref[row, :hw]
                li = loc_ref[row, hw:]
                loc_ref[row, :hw] = xr
                loc_ref[row, hw:] = xi
                new.append((a_re * xr - a_im * xi + lr, a_re * xi + a_im * xr + li))
            return tuple(new)

        init = tuple((x0_ref[0, b:b + 1, :hw], x0_ref[0, b:b + 1, hw:]) for b in range(nseq))
        fin = lax.fori_loop(0, nchunks, body, init, unroll=4)
        for b in range(nseq):
            xf_ref[0, b:b + 1, :hw] = fin[b][0]
            xf_ref[0, b:b + 1, hw:] = fin[b][1]
        xs_ref[0] = loc_ref[...].astype(xs_ref.dtype)


def _s5_state(u2, bre, bim, x0, are, aim, nseq, nchunks):
    no, r, kk = u2.shape
    sw = x0.shape[2]
    blk = lambda a: pl.BlockSpec((1,) + a.shape[1:], lambda o: (o,) + (0,) * (a.ndim - 1))
    return pl.pallas_call(
        functools.partial(_s5_state_kernel, nseq=nseq, nchunks=nchunks),
        grid=(no,),
        in_specs=[blk(u2), blk(bre), blk(bim), blk(x0), blk(are), blk(aim)],
        out_specs=[pl.BlockSpec((1, r, sw), lambda o: (o, 0, 0)), blk(x0)],
        out_shape=[jax.ShapeDtypeStruct((no, r, sw), BF16), jax.ShapeDtypeStruct(x0.shape, F32)],
        scratch_shapes=[pltpu.VMEM((r, sw), F32)],
        compiler_params=_cparams(("parallel",)),
        name="s5_state",
    )(u2, bre, bim, x0, are, aim)


def _dot_nt_f32(a, b):
    a_hi, a_lo = _split_bf16(a)
    b_hi, b_lo = _split_bf16(b)
    return _dot_nt(a_hi, b_hi) + _dot_nt(a_hi, b_lo) + _dot_nt(a_lo, b_hi)


def _s5_out_kernel(u_ref, xs_ref, bre_ref, bim_ref, cre_ref, cim_ref, zre_ref, zim_ref, y_ref, yscr_ref):
    r, kk = u_ref.shape[1], u_ref.shape[2]
    t = kk // LANES
    sw = xs_ref.shape[2]
    ntile = kk // MXU_TILE
    taps = _dot_nt_f32(bre_ref[0], zre_ref[0]) - _dot_nt_f32(bim_ref[0], zim_ref[0])
    tmask = _group_mask((LANES, LANES), S5_GROUP, S5_GROUP)

    def tap(lag):
        if lag < 0:
            return jnp.zeros((LANES, LANES), F32)
        s = t - 1 - lag
        return jnp.where(tmask, taps[s * LANES:(s + 1) * LANES], 0.0)

    wts = [jnp.concatenate([jnp.concatenate([tap(2 * d), tap(2 * d + 1)], axis=1),
                            jnp.concatenate([tap(2 * d - 1), tap(2 * d)], axis=1)], axis=0).astype(BF16)
           for d in range(ntile)]
    cpt = jnp.concatenate([cre_ref[0]] * OCT + [cim_ref[0]] * OCT, axis=1)
    cpt = jnp.where(_group_mask((kk, sw), S5_GROUP, S5_STATE), cpt, 0.0).astype(BF16)
    xs = xs_ref[0]
    for n in range(ntile):
        acc = _dot_nt(xs, cpt[n * MXU_TILE:(n + 1) * MXU_TILE])
        for k in range(n + 1):
            acc = acc + _dot(u_ref[0, :, k * MXU_TILE:(k + 1) * MXU_TILE], wts[n - k])
        for e in range(MXU_TILE // LANES):
            yscr_ref[pl.ds(2 * n + e, r, stride=t), :] = acc[:, e * LANES:(e + 1) * LANES]
    y_ref[0] = yscr_ref[...].astype(y_ref.dtype)


def _s5_out(u2, xs, bre, bim, cre, cim, zre, zim, out_dtype):
    no, r, kk = u2.shape
    m = r * (kk // LANES)
    blk = lambda a: pl.BlockSpec((1,) + a.shape[1:], lambda o: (o,) + (0,) * (a.ndim - 1))
    ops = (u2, xs, bre, bim, cre, cim, zre, zim)
    return pl.pallas_call(
        _s5_out_kernel,
        grid=(no,),
        in_specs=[blk(a) for a in ops],
        out_specs=pl.BlockSpec((1, m, LANES), lambda o: (o, 0, 0)),
        out_shape=jax.ShapeDtypeStruct((no, m, LANES), out_dtype),
        scratch_shapes=[pltpu.VMEM((m, LANES), F32)],
        compiler_params=_cparams(("parallel",)),
        name="s5_out",
    )(*ops)


def _s5_params(lam_re, lam_im, log_dt, b_re, b_im, c_re, c_im):
    t = S5_CHUNK
    ng = lam_re.shape[0]
    no = ng // OCT
    dt = jnp.exp(log_dt)[:, None]
    a = lam_re * dt
    b = lam_im * dt
    n = jnp.arange(t + 1, dtype=F32)[None, :, None]
    mag = jnp.exp(n * a[:, None, :])
    pw_re = mag * jnp.cos(n * b[:, None, :])
    pw_im = mag * jnp.sin(n * b[:, None, :])
    em1_re = jnp.expm1(a) * jnp.cos(b) - 2.0 * jnp.sin(0.5 * b) ** 2
    em1_im = jnp.exp(a) * jnp.sin(b)
    den = lam_re * lam_re + lam_im * lam_im
    z_re = (em1_re * lam_re + em1_im * lam_im) / den
    z_im = (em1_im * lam_re - em1_re * lam_im) / den
    bb_re = z_re[..., None] * b_re - z_im[..., None] * b_im
    bb_im = z_re[..., None] * b_im + z_im[..., None] * b_re
    def rows(w):
        return w.reshape(no, OCT, t, S5_GROUP, S5_STATE).transpose(0, 2, 1, 3, 4).reshape(no, t * LANES, S5_STATE)

    bt_re = bb_re.transpose(0, 2, 1)[:, None]
    bt_im = bb_im.transpose(0, 2, 1)[:, None]
    r_re, r_im = pw_re[:, t - 1::-1, None, :], pw_im[:, t - 1::-1, None, :]
    bre = rows(r_re * bt_re - r_im * bt_im)
    bim = rows(r_re * bt_im + r_im * bt_re)
    o_re, o_im = pw_re[:, 1:, None, :], pw_im[:, 1:, None, :]
    cre = rows(c_re[:, None] * o_re - c_im[:, None] * o_im)
    cim = rows(-(c_re[:, None] * o_im + c_im[:, None] * o_re))
    return bre, bim, cre, cim, c_re.reshape(no, LANES, S5_STATE), c_im.reshape(no, LANES, S5_STATE), pw_re, pw_im


def _s5_chunk(seqlen):
    return math.gcd(seqlen, S5_CHUNK)


def _s5_branch(u2, x0, ops, nseq, seqlen, out_dtype):
    bre, bim, cre, cim, zre, zim, pw_re, pw_im = ops
    no = u2.shape[0]
    t = _s5_chunk(seqlen)
    kk = t * LANES
    nchunks = seqlen // t
    are = pw_re[:, t].reshape(no, 1, OCT * S5_STATE)
    aim = pw_im[:, t].reshape(no, 1, OCT * S5_STATE)
    tail = S5_CHUNK * LANES - kk
    bre, bim = bre[:, tail:], bim[:, tail:]
    xs, xf = _s5_state(u2, bre, bim, x0, are, aim, nseq, nchunks)
    return _s5_out(u2, xs, bre, bim, cre[:, :kk], cim[:, :kk], zre, zim, out_dtype), xf


def _s5_state_in(re, im, no):
    nseq = re.shape[0]
    f = lambda a: a.reshape(nseq, no, OCT * S5_STATE).transpose(1, 0, 2)
    return jnp.concatenate([f(re), f(im)], axis=-1)


def _s5_state_out(xf):
    no, nseq, _ = xf.shape
    hw = OCT * S5_STATE
    f = lambda a: a.transpose(1, 0, 2).reshape(1, nseq, no * OCT, S5_STATE)
    return f(xf[:, :, :hw]), f(xf[:, :, hw:])


def _gla_kernel(q_ref, k_ref, v_ref, la_ref, gb_ref, s0_ref, ng_ref, o_ref, sf_ref, st_ref,
                *, nseq, nchunks, c):
    i = pl.program_id(1)

    @pl.when(i == 0)
    def _():
        st_ref[...] = s0_ref[...]

    nh, hk, hv = GLA_HEADS, GLA_HEAD_K, GLA_HEAD_V
    nch = nseq * nchunks
    tm = nch * c
    iota = lambda shape, d: lax.broadcasted_iota(jnp.int32, shape, d)
    ng = ng_ref[...]

    tb = min(tm, MXU_TILE)
    rt, ct = iota((tb, tb), 0), iota((tb, tb), 1)
    tril = ((rt // c == ct // c) & (rt >= ct)).astype(BF16)
    parts = []
    for r0 in range(0, tm, tb):
        la_hi, la_lo = _split_bf16(la_ref[r0:r0 + tb, :])
        parts.append(_dot(tril, la_hi) + _dot(tril, la_lo))
    bcum = jnp.concatenate(parts, axis=0)
    e_hi, e_lo = _split_bf16(jnp.exp(jnp.concatenate([bcum[(ci + 1) * c - 1:(ci + 1) * c] for ci in range(nch)],
                                                     axis=0)))
    pick = (iota((nch, LANES), 0) == iota((nch, LANES), 1)).astype(BF16)
    dec_t = _dot_tn(e_hi, pick) + _dot_tn(e_lo, pick)
    q_all = q_ref[...].astype(F32)
    k_all = k_ref[...].astype(F32)
    q_dec_all = q_all * jnp.exp(bcum)
    k_dec_all = k_all * jnp.exp(-bcum)

    own_k = iota((nh * c, nh * hk), 0) // c == iota((nh * c, nh * hk), 1) // hk
    own_v = iota((nh * c, nh * hv), 0) // c == iota((nh * c, nh * hv), 1) // hv
    causal = iota((c, nh * c), 1) % c <= iota((c, nh * c), 0)
    zero_v = jnp.zeros((hk, hv), F32)

    def intra(ci):
        rows = slice(ci * c, (ci + 1) * c)
        q_dec = q_dec_all[rows].astype(BF16)
        k_dec = k_dec_all[rows]
        b_c = bcum[rows]
        k_tail = (k_all[rows] * jnp.exp(b_c[c - 1:c] - b_c)).astype(BF16)
        v = v_ref[rows, :].astype(F32)
        k_bd = jnp.where(own_k, jnp.concatenate([k_dec] * nh, axis=0), 0.0).astype(BF16)
        v_bd = jnp.where(own_v, jnp.concatenate([v] * nh, axis=0), 0.0).astype(BF16)
        att = jnp.where(causal, _dot_nt(q_dec, k_bd), 0.0)
        o_intra = _dot(att.astype(BF16), v_bd)
        vb = v.astype(BF16)
        kvs = []
        for h0 in range(0, nh, 2):
            kv2 = _dot_tn(k_tail[:, h0 * hk:(h0 + 2) * hk], vb[:, h0 * hv:(h0 + 2) * hv])
            kvs += [kv2[:hk, :hv], kv2[hk:, hv:]]
        return q_dec, o_intra, kvs

    def carry(ci, sts, q_dec, o_intra, kvs):
        st_bd = jnp.concatenate(
            [jnp.concatenate([zero_v] * h + [sts[h]] + [zero_v] * (nh - 1 - h), axis=1) for h in range(nh)],
            axis=0).astype(BF16)
        o = o_intra + _dot(q_dec, st_bd)
        new = [dec_t[h * hk:(h + 1) * hk, ci:ci + 1] * sts[h] + kvs[h] for h in range(nh)]
        return o, new

    def finish(ci, o):
        rows = slice(ci * c, (ci + 1) * c)
        for h in range(nh):
            vs = slice(h * hv, (h + 1) * hv)
            oh = _rms_rows(o[:, vs], ng) * gb_ref[rows, vs].astype(F32)
            o_ref[rows, vs] = oh.astype(o_ref.dtype)

    sts = None
    pending = intra(0)
    unfinished = None
    for ci in range(nch):
        s, first, last = ci // nchunks, ci % nchunks == 0, ci % nchunks == nchunks - 1
        current = pending
        if ci + 1 < nch:
            pending = intra(ci + 1)
        if first:
            sts = [st_ref[s, h] for h in range(nh)]
        o, sts = carry(ci, sts, *current)
        if last:
            for h in range(nh):
                st_ref[s, h] = sts[h]
        if unfinished is not None:
            finish(*unfinished)
        unfinished = (ci, o)
    finish(*unfinished)

    @pl.when(i == pl.num_programs(1) - 1)
    def _():
        sf_ref[...] = st_ref[...]


def _gla(q, k, v, la, gb, s0, ng, nseq_total, seqlen, out_dtype):
    c = math.gcd(seqlen, GLA_CHUNK)
    if seqlen >= 16 * c:
        nseq, nchunks = 1, 16
    else:
        nseq, nchunks = 16, seqlen // c
    assert nseq_total % nseq == 0 and seqlen % (nchunks * c) == 0
    nblk = seqlen // (nchunks * c)
    tm = nseq * nchunks * c
    row = lambda n: pl.BlockSpec((tm, n), lambda b, i: (b * nblk + i, 0))
    st_spec = pl.BlockSpec((nseq, GLA_HEADS, GLA_HEAD_K, GLA_HEAD_V), lambda b, i: (b, 0, 0, 0))
    m = q.shape[0]
    return pl.pallas_call(
        functools.partial(_gla_kernel, nseq=nseq, nchunks=nchunks, c=c),
        grid=(nseq_total // nseq, nblk),
        in_specs=[row(256), row(256), row(512), row(256), row(512), st_spec, _full(ng.shape)],
        out_specs=[row(512), st_spec],
        out_shape=[jax.ShapeDtypeStruct((m, 512), out_dtype), jax.ShapeDtypeStruct(s0.shape, F32)],
        scratch_shapes=[pltpu.VMEM((nseq, GLA_HEADS, GLA_HEAD_K, GLA_HEAD_V), F32)],
        compiler_params=_cparams(("parallel", "arbitrary")),
        name="gla",
    )(q, k, v, la, gb, s0, ng)


def _gelu_tanh(x):
    return 0.5 * x * (1.0 + jnp.tanh(math.sqrt(2.0 / math.pi) * (x + 0.044715 * (x * x * x))))


def _head_ones(n):
    r = lax.broadcasted_iota(jnp.int32, (n, n), 0) // SWA_HEAD_DIM
    c = lax.broadcasted_iota(jnp.int32, (n, n), 1) // SWA_HEAD_DIM
    return (r == c).astype(BF16)


def _rope_block(x, cos_t, sin_t, upper):
    swapped = jnp.where(upper, pltpu.roll(x, 32, 1), pltpu.roll(x, 96, 1))
    return x * cos_t + swapped * sin_t


def _mid_kernel(x_ref, y_ref, u_ref, ga_ref, ob_ref, d_ref, wglu_ref, bglu_ref, wo_ref,
                g_ref, w_ref, gq_ref, gk_ref, cos_ref, sin_ref, h_ref, q_ref, k_ref, v_ref, gate_ref):
    tm = x_ref.shape[0]
    lane = lax.broadcasted_iota(jnp.int32, (tm, LANES), 1)
    upper = (lane & 32) != 0
    ones4 = _head_ones(MXU_TILE)
    inv_d = 1.0 / SWA_HEAD_DIM
    qw = SWA_HEADS * SWA_HEAD_DIM
    kw = SWA_KV_HEADS * SWA_HEAD_DIM
    nblk = qw // MXU_TILE
    no = y_ref.shape[0]

    na = ga_ref.shape[1]
    mix_b = _dot(ob_ref[...].astype(BF16), wo_ref[na:, :])
    y = jnp.concatenate([y_ref[o].astype(F32) for o in range(no)], axis=1)
    u = jnp.concatenate([u_ref[o].astype(F32) for o in range(no)], axis=1)
    z = _gelu_tanh(y + d_ref[...] * u)
    z = z * _sigmoid(_dot(z.astype(BF16), wglu_ref[...]) + bglu_ref[...])
    out_a = z * ga_ref[...].astype(F32)
    h = x_ref[...] + (_dot(out_a.astype(BF16), wo_ref[:na, :]) + mix_b)
    h_ref[...] = h

    xb = _rms_rows(h, g_ref[...]).astype(BF16)
    cos_t = cos_ref[...]
    sin_t = sin_ref[...]

    def gate_finish(gate):
        gate_ref[...] = _silu(gate).astype(gate_ref.dtype)

    def q_finish(j, q):
        ss = _dot((q * q).astype(BF16), ones4)
        qn = q * lax.rsqrt(ss * inv_d + NORM_EPS) * gq_ref[...]
        for e in range(MXU_TILE // LANES):
            cols = slice(j * MXU_TILE + e * LANES, j * MXU_TILE + (e + 1) * LANES)
            qe = _rope_block(qn[:, e * LANES:(e + 1) * LANES], cos_t, sin_t, upper)
            q_ref[:, cols] = (qe * (SWA_HEAD_DIM ** -0.5 * LOG2E)).astype(q_ref.dtype)

    def kv_finish(kv):
        k = kv[:, :kw]
        ss = _dot((k * k).astype(BF16), ones4[:kw, :kw])
        kn = k * lax.rsqrt(ss * inv_d + NORM_EPS) * gk_ref[...]
        k_ref[...] = _rope_block(kn, cos_t, sin_t, upper)
        v_ref[...] = kv[:, kw:]

    work = [(slice(j * MXU_TILE, (j + 1) * MXU_TILE), functools.partial(q_finish, j)) for j in range(nblk)]
    work += [(slice(qw, qw + 2 * kw), kv_finish)]
    work += [(slice(qw + 2 * kw, None), gate_finish)]
    pending = _dot(xb, w_ref[:, work[0][0]])
    for n, (_, finish) in enumerate(work):
        current = pending
        if n + 1 < len(work):
            pending = _dot(xb, w_ref[:, work[n + 1][0]])
        finish(current)


def _mid(x, y, u, ga, ob, pe, po, cos_t, sin_t, act_dtype):
    m, dm = x.shape
    tm = _row_tile(m)
    row = lambda n: pl.BlockSpec((tm, n), lambda i: (i, 0))
    slab = pl.BlockSpec((y.shape[0], tm, LANES), lambda i: (0, i, 0))
    assert cos_t.shape[0] % tm == 0
    nper = cos_t.shape[0] // tm
    tab = pl.BlockSpec((tm, LANES), lambda i: (i % nper, 0))
    qw = SWA_HEADS * SWA_HEAD_DIM
    kw = SWA_KV_HEADS * SWA_HEAD_DIM
    weights = [pe['d'], pe['wglu'], pe['bglu'], pe['wo'], po['norm_g'], po['w_in'], po['gq'], po['gk']]
    return pl.pallas_call(
        _mid_kernel,
        grid=(m // tm,),
        in_specs=[row(dm), slab, slab, row(512), row(512)] + [_full(w.shape) for w in weights] + [tab, tab],
        out_specs=[row(dm), row(qw), row(kw), row(kw), row(qw)],
        out_shape=[jax.ShapeDtypeStruct((m, dm), F32),
                   jax.ShapeDtypeStruct((m, qw), act_dtype), jax.ShapeDtypeStruct((m, kw), F32),
                   jax.ShapeDtypeStruct((m, kw), F32), jax.ShapeDtypeStruct((m, qw), act_dtype)],
        compiler_params=_cparams(("parallel",)),
        name="mid",
    )(x, y, u, ga, ob, *weights, cos_t, sin_t)


def _rope_tables(pos):
    half = SWA_HEAD_DIM // 2
    inv_freq = ROPE_THETA ** (-np.arange(half, dtype=np.float64) / half)
    ang = np.asarray(pos, np.float64)[:, None] * inv_freq[None, :]
    cos, sin = np.cos(ang), np.sin(ang)
    cos_t = np.concatenate([cos, cos, cos, cos], axis=1).astype(np.float32)
    sin_t = np.concatenate([-sin, sin, -sin, sin], axis=1).astype(np.float32)
    return jnp.asarray(cos_t), jnp.asarray(sin_t)


def _attn_prompt_kernel(sink_ref, q_ref, kc_ref, kp_ref, vc_ref, vp_ref, gate_ref, h_ref, wout_ref,
                        y_ref, p_ref, o_ref, *, nqb):
    i = pl.program_id(1)
    w = SWA_WINDOW
    hd = SWA_HEAD_DIM
    npair = SWA_GROUP // 2
    lane2 = lax.broadcasted_iota(jnp.int32, (2 * w, LANES), 1)
    low = lane2 < hd
    rr = lax.broadcasted_iota(jnp.int32, (w, w), 0)
    cc = lax.broadcasted_iota(jnp.int32, (w, w), 1)
    tri = cc <= rr
    low_w = cc < hd
    r4 = lax.broadcasted_iota(jnp.int32, (4 * w, LANES), 0)
    c4 = lax.broadcasted_iota(jnp.int32, (4 * w, LANES), 1)
    den_cols = ((r4 < 2 * w) == (c4 < hd)).astype(BF16)
    units = [(jb, kv) for jb in range(nqb) for kv in range(SWA_KV_HEADS)]

    def scores(jb, kv):
        rows = slice(jb * w, (jb + 1) * w)
        if jb == 0:
            k_prev, v_prev = kp_ref[...], vp_ref[...]
        else:
            prev_rows = slice((jb - 1) * w, jb * w)
            k_prev, v_prev = kc_ref[prev_rows, :], vc_ref[prev_rows, :]
        kcat = jnp.concatenate([k_prev, kc_ref[rows, :]], axis=0)
        vcat = jnp.concatenate([v_prev, vc_ref[rows, :]], axis=0)
        own = low if kv == 0 else jnp.logical_not(low)
        k_own = jnp.where(own, kcat, 0.0)
        v_own = jnp.where(own, vcat, 0.0)
        k_oth = pltpu.roll(k_own, hd, 1)
        v_oth = pltpu.roll(v_own, hd, 1)
        k_lo, k_hi = (k_own, k_oth) if kv == 0 else (k_oth, k_own)
        v_lo, v_hi = (v_own, v_oth) if kv == 0 else (v_oth, v_own)
        k_rhs = jnp.concatenate([k_lo, k_hi], axis=0).astype(BF16)
        v_rhs = jnp.concatenate([jnp.concatenate([v_lo, v_hi], axis=0).astype(BF16), den_cols], axis=1)
        qs = jnp.concatenate([q_ref[rows, (kv * npair + pr) * LANES:(kv * npair + pr + 1) * LANES]
                              for pr in range(npair)], axis=0)
        return _dot_nt(qs, k_rhs), v_rhs

    def softmax_pv(kv, first, buf, s_all, v_rhs):
        sink_terms = []
        for pr in range(npair):
            prow = slice(pr * w, (pr + 1) * w)
            pair_terms = []
            for e in range(2):
                s_prev = s_all[prow, (2 * e) * w:(2 * e + 1) * w]
                s_cur = s_all[prow, (2 * e + 1) * w:(2 * e + 2) * w]
                if first:
                    s_prev = jnp.where(i > 0, s_prev, -jnp.inf)
                sc = jnp.where(tri, s_cur, s_prev)
                sink = sink_ref[2 * (kv * npair + pr) + e] * LOG2E
                mx = jnp.maximum(jnp.max(sc, axis=-1, keepdims=True), sink)
                pe = jnp.exp2(sc - mx)
                p_ref[buf, prow, (2 * e) * w:(2 * e + 1) * w] = jnp.where(tri, 0.0, pe).astype(BF16)
                p_ref[buf, prow, (2 * e + 1) * w:(2 * e + 2) * w] = jnp.where(tri, pe, 0.0).astype(BF16)
                pair_terms.append(jnp.exp2(sink - mx))
            sink_terms.append(pair_terms)
        return _dot(p_ref[buf], v_rhs), sink_terms

    def normalise(jb, kv, o_ext, sink_terms):
        rows = slice(jb * w, (jb + 1) * w)
        for pr in range(npair):
            prow = slice(pr * w, (pr + 1) * w)
            cols = slice((kv * npair + pr) * LANES, (kv * npair + pr + 1) * LANES)
            st = jnp.where(low_w, sink_terms[pr][0], sink_terms[pr][1])
            o = o_ext[prow, :LANES] / (o_ext[prow, LANES:] + st)
            o_ref[rows, cols] = o.astype(o_ref.dtype)

    def project(rows):
        og = o_ref[rows, :].astype(F32) * gate_ref[rows, :].astype(F32)
        y_ref[rows, :] = h_ref[rows, :] + _dot(og.astype(BF16), wout_ref[...])

    group = 2
    pending = scores(*units[0])
    unfinished = None
    for n, (jb, kv) in enumerate(units):
        current = pending
        if n + 1 < len(units):
            pending = scores(*units[n + 1])
        result = softmax_pv(kv, jb == 0, n % 2, *current)
        if unfinished is not None:
            normalise(*unfinished)
            done_jb, done_kv = unfinished[:2]
            if done_kv == SWA_KV_HEADS - 1 and (done_jb + 1) % group == 0:
                project(slice((done_jb + 1 - group) * w, (done_jb + 1) * w))
        unfinished = (jb, kv) + result
    normalise(*unfinished)
    project(slice((nqb - group) * w, nqb * w))


def _attn_prompt(sinks, q, k, v, gate, h, w_out, nseq, seqlen):
    w = SWA_WINDOW
    nqb = 8
    tm = nqb * w
    assert seqlen % tm == 0
    nblk = seqlen // tm
    qw = q.shape[1]
    dm = h.shape[1]
    row = lambda n: pl.BlockSpec((tm, n), lambda b, i: (b * nblk + i, 0))
    prev = lambda n: pl.BlockSpec((w, n), lambda b, i: (jnp.maximum((b * nblk + i) * nqb - 1, 0), 0))
    smem = pl.BlockSpec(memory_space=pltpu.SMEM)
    return pl.pallas_call(
        functools.partial(_attn_prompt_kernel, nqb=nqb),
        grid=(nseq, nblk),
        in_specs=[smem, row(qw), row(LANES), prev(LANES), row(LANES), prev(LANES), row(qw), row(dm),
                  _full(w_out.shape)],
        out_specs=row(dm),
        out_shape=jax.ShapeDtypeStruct(h.shape, F32),
        scratch_shapes=[pltpu.VMEM((2, 4 * w, 4 * w), BF16), pltpu.VMEM((tm, qw), BF16)],
        compiler_params=_cparams(("parallel", "arbitrary")),
        name="attn_prompt",
    )(sinks, q, k, k, v, v, gate, h, w_out)


def _attn_sample_kernel(sink_ref, q_ref, kn_ref, vn_ref, kc_ref, vc_ref, o_ref, ko_ref, vo_ref, q2_ref, *, seqlen):
    nq = SWA_HEADS * seqlen
    hd = SWA_HEAD_DIM
    heads = [(kv, g) for kv in range(SWA_KV_HEADS) for g in range(SWA_GROUP)]

    def stacked(kv, g):
        r0 = (kv * SWA_GROUP + g) * seqlen
        return slice(r0, r0 + seqlen), slice(kv * hd, (kv + 1) * hd)

    q2_ref[...] = jnp.zeros(q2_ref.shape, q2_ref.dtype)
    for kv, g in heads:
        rows, lanes = stacked(kv, g)
        h = kv * SWA_GROUP + g
        q2_ref[:, rows, lanes] = q_ref[:, :, h * hd:(h + 1) * hd]
    ncache = kc_ref.shape[1]
    t_row = lax.broadcasted_iota(jnp.int32, (nq, ncache), 0) % seqlen
    c_col = lax.broadcasted_iota(jnp.int32, (nq, ncache), 1)
    cache_ok = c_col > t_row - (SWA_WINDOW - ncache)
    nnew = kn_ref.shape[1]
    t_row_n = lax.broadcasted_iota(jnp.int32, (nq, nnew), 0) % seqlen
    n_col = lax.broadcasted_iota(jnp.int32, (nq, nnew), 1)
    new_ok = n_col <= t_row_n
    sink = (sink_ref[...] * LOG2E)[None, :, 0:1]
    bqk = lambda a, b: lax.dot_general(a, b, (((2,), (2,)), ((0,), (0,))), preferred_element_type=F32)
    bpv = lambda a, b: lax.dot_general(a, b, (((2,), (1,)), ((0,), (0,))), preferred_element_type=F32)
    q = q2_ref[...].astype(BF16)
    kc = kc_ref[...]
    vc = vc_ref[...]
    kn = kn_ref[...]
    vn = vn_ref[...]
    sc = jnp.where(cache_ok[None], bqk(q, kc.astype(BF16)), -jnp.inf)
    sn = jnp.where(new_ok[None], bqk(q, kn.astype(BF16)), -jnp.inf)
    mx = jnp.maximum(jnp.maximum(jnp.max(sc, axis=-1, keepdims=True), jnp.max(sn, axis=-1, keepdims=True)), sink)
    pc = jnp.exp2(sc - mx)
    pn = jnp.exp2(sn - mx)
    den = jnp.sum(pc, axis=-1, keepdims=True) + jnp.sum(pn, axis=-1, keepdims=True) + jnp.exp2(sink - mx)
    inv = 1.0 / den
    o2 = bpv((pc * inv).astype(BF16), vc.astype(BF16)) + bpv((pn * inv).astype(BF16), vn.astype(BF16))
    for kv, g in heads:
        rows, lanes = stacked(kv, g)
        h = kv * SWA_GROUP + g
        o_ref[:, :, h * hd:(h + 1) * hd] = o2[:, rows, lanes]
    keep = ncache - seqlen
    ko_ref[:, 0:keep, :] = kc[:, seqlen:ncache, :]
    ko_ref[:, keep:ncache, :] = kn[:, 0:seqlen, :]
    vo_ref[:, 0:keep, :] = vc[:, seqlen:ncache, :]
    vo_ref[:, keep:ncache, :] = vn[:, 0:seqlen, :]


def _attn_sample(sink_rows, q, kn, vn, kc, vc):
    n, seqlen, _ = q.shape
    nseq = 16
    assert n % nseq == 0
    blk = lambda a: pl.BlockSpec((nseq,) + a.shape[1:], lambda i: (i, 0, 0))
    return pl.pallas_call(
        functools.partial(_attn_sample_kernel, seqlen=seqlen),
        grid=(n // nseq,),
        in_specs=[_full(sink_rows.shape), blk(q), blk(kn), blk(vn), blk(kc), blk(vc)],
        out_specs=[blk(q), blk(kc), blk(vc)],
        out_shape=[jax.ShapeDtypeStruct(q.shape, F32), jax.ShapeDtypeStruct(kc.shape, F32),
                   jax.ShapeDtypeStruct(vc.shape, F32)],
        scratch_shapes=[pltpu.VMEM((nseq, SWA_HEADS * seqlen, SWA_KV_HEADS * SWA_HEAD_DIM), F32)],
        compiler_params=_cparams(("parallel",)),
        name="attn_sample",
    )(sink_rows, q, kn, vn, kc, vc)


def _odd_out_kernel(h_ref, o_ref, gate_ref, w_ref, y_ref):
    og = o_ref[...].astype(F32) * gate_ref[...].astype(F32)
    y_ref[...] = h_ref[...] + _dot(og.astype(BF16), w_ref[...])


def _odd_out(h, o, gate, w):
    m, dm = h.shape
    tm = _row_tile(m)
    row = lambda n: pl.BlockSpec((tm, n), lambda i: (i, 0))
    return pl.pallas_call(
        _odd_out_kernel,
        grid=(m // tm,),
        in_specs=[row(dm), row(o.shape[1]), row(gate.shape[1]), _full(w.shape)],
        out_specs=row(dm),
        out_shape=jax.ShapeDtypeStruct((m, dm), F32),
        compiler_params=_cparams(("parallel",)),
        name="odd_out",
    )(h, o, gate, w)


def _trunk(x, s5_x0, gla_s0, pe, po, pos, nseq, seqlen, act_dtype):
    u4, u2, ga, q, k, v, la, gb = _even_in(x, pe['norm_g'], pe['w_in'], pe['wgate'], pe['bgate'], act_dtype,
                                           _s5_chunk(seqlen))
    y4, s5_fin = _s5_branch(u2, s5_x0, pe['s5_ops'], nseq, seqlen, act_dtype)
    ob, gla_fin = _gla(q, k, v, la, gb, gla_s0, pe['gla_norm_g'], nseq, seqlen, act_dtype)
    cos_t, sin_t = _rope_tables(pos)
    h, q1, k1, v1, gate = _mid(x, y4, u4, ga, ob, pe, po, cos_t, sin_t, act_dtype)
    return h, q1, k1, v1, gate, s5_fin, gla_fin


def kernel(x_prompt, x_sample, state_s5_re, state_s5_im, state_gla, cache_swa_k, cache_swa_v,
           even_norm_g, even_w_in, s5_lambda_re, s5_lambda_im, s5_log_dt, s5_b_re, s5_b_im,
           s5_c_re, s5_c_im, s5_d, s5_w_glu, s5_b_glu, gla_w_gate, gla_b_gate, gla_norm_g,
           even_w_out, odd_norm_g, odd_w_in, swa_q_norm_g, swa_k_norm_g, swa_sinks, odd_w_out):
    nb, seq, dm = x_prompt.shape
    ns, dseq, _ = x_sample.shape
    ng = s5_lambda_re.shape[1]
    no = ng // OCT
    xp = x_prompt.reshape(nb * seq, dm)
    xs = x_sample.reshape(ns * dseq, dm)

    i = 0
    w_in = even_w_in[i]
    s5w = ng * S5_GROUP
    col_alow = 2 * s5w + 2 * GLA_HEADS * GLA_HEAD_K + GLA_HEADS * GLA_HEAD_V
    col_gb = col_alow + GLA_GATE_RANK
    pad_rank = LANES - GLA_GATE_RANK
    pe = {
        'norm_g': even_norm_g[i][None, :],
        'w_in': jnp.concatenate([w_in[:, :col_alow], jnp.pad(w_in[:, col_alow:col_gb], ((0, 0), (0, pad_rank))),
                                 w_in[:, col_gb:]], axis=1).astype(BF16),
        'wgate': jnp.pad(gla_w_gate[i], ((0, pad_rank), (0, 0))).astype(BF16),
        'bgate': gla_b_gate[i][None, :],
        's5_ops': _s5_params(s5_lambda_re[i], s5_lambda_im[i], s5_log_dt[i], s5_b_re[i], s5_b_im[i],
                             s5_c_re[i], s5_c_im[i]),
        'gla_norm_g': gla_norm_g[i][None, :],
        'd': s5_d[i][None, :],
        'wglu': s5_w_glu[i].astype(BF16),
        'bglu': s5_b_glu[i][None, :],
        'wo': even_w_out[i].astype(BF16),
    }
    po = {
        'norm_g': odd_norm_g[i][None, :],
        'w_in': odd_w_in[i].astype(BF16),
        'gq': jnp.tile(swa_q_norm_g[i], MXU_TILE // SWA_HEAD_DIM)[None, :],
        'gk': jnp.tile(swa_k_norm_g[i], LANES // SWA_HEAD_DIM)[None, :],
    }
    w_out = odd_w_out[i].astype(BF16)
    sinks = swa_sinks[i]
    kvw = SWA_KV_HEADS * SWA_HEAD_DIM

    s5_zero = jnp.zeros((no, nb, 2 * OCT * S5_STATE), F32)
    gla_zero = jnp.zeros((nb, GLA_HEADS, GLA_HEAD_K, GLA_HEAD_V), F32)
    hp, q, k, v, gate, s5_p, gla_p = _trunk(xp, s5_zero, gla_zero, pe, po, np.arange(seq), nb, seq, BF16)
    s5r_p, s5i_p = _s5_state_out(s5_p)
    y_prompt = _attn_prompt(sinks, q, k, v, gate, hp, w_out, nb, seq).reshape(nb, seq, dm)
    cache_len = min(SWA_WINDOW, seq)
    tail = lambda a: (a.reshape(nb, seq, kvw)[:, seq - cache_len:]
                      .reshape(1, nb, cache_len, SWA_KV_HEADS, SWA_HEAD_DIM))
    swk_p, swv_p = tail(k), tail(v)

    pos_s = np.tile(PAST_LEN + np.arange(dseq), ns)
    s5_init = _s5_state_in(state_s5_re[i], state_s5_im[i], no)
    hs, q, k, v, gate, s5_s, gla_s = _trunk(xs, s5_init, state_gla[i], pe, po, pos_s, ns, dseq, F32)
    s5r_s, s5i_s = _s5_state_out(s5_s)
    ncache = cache_swa_k.shape[2]
    sink_rows = jnp.broadcast_to(jnp.repeat(sinks, dseq)[:, None], (SWA_HEADS * dseq, LANES))
    kn = k.reshape(ns, dseq, kvw)
    vn = v.reshape(ns, dseq, kvw)
    npad = 16 - dseq
    kn_pad = jnp.pad(kn, ((0, 0), (0, npad), (0, 0)))
    vn_pad = jnp.pad(vn, ((0, 0), (0, npad), (0, 0)))
    kc = cache_swa_k[i].reshape(ns, ncache, kvw)
    vc = cache_swa_v[i].reshape(ns, ncache, kvw)
    o, kc_new, vc_new = _attn_sample(sink_rows, q.reshape(ns, dseq, -1), kn_pad, vn_pad, kc, vc)
    y_sample = _odd_out(hs, o.reshape(ns * dseq, -1), gate, w_out).reshape(ns, dseq, dm)
    swk_s = kc_new.reshape(1, ns, ncache, SWA_KV_HEADS, SWA_HEAD_DIM)
    swv_s = vc_new.reshape(1, ns, ncache, SWA_KV_HEADS, SWA_HEAD_DIM)

    return (y_prompt, y_sample,
            s5r_p, s5i_p, gla_p[None], swk_p, swv_p,
            s5r_s, s5i_s, gla_s[None], swk_s, swv_s)
```

```python
import functools
import math

import jax
import jax.numpy as jnp
import numpy as np
from jax import lax
from jax.experimental import pallas as pl
from jax.experimental.pallas import tpu as pltpu

F32 = jnp.float32
BF16 = jnp.bfloat16

PAST_LEN = 8192
NORM_EPS = 1e-6
S5_GROUP = 16
S5_STATE = 64
S5_CHUNK = 16
GLA_HEADS = 4
GLA_HEAD_K = 64
GLA_HEAD_V = 128
GLA_GATE_RANK = 16
GLA_GATE_TAU = 16.0
GLA_CHUNK = 64
S5_WIDTH = 512
GLA_K_WIDTH = GLA_HEADS * GLA_HEAD_K
GLA_V_WIDTH = GLA_HEADS * GLA_HEAD_V
SWA_HEADS = 16
SWA_KV_HEADS = 2
SWA_GROUP = SWA_HEADS // SWA_KV_HEADS
SWA_HEAD_DIM = 64
SWA_WINDOW = 128
ROPE_THETA = 10000.0
LOG2E = math.log2(math.e)
LANES = 128
BF16_ROWS = 16
MXU_TILE = 256
OCT = LANES // S5_GROUP
ROW_TILE = 512
VMEM_LIMIT = 48 * 1024 * 1024
C_U = 0
C_GATE_A = C_U + S5_WIDTH
C_Q = C_GATE_A + S5_WIDTH
C_K = C_Q + GLA_K_WIDTH
C_V = C_K + GLA_K_WIDTH
C_CODE = C_V + GLA_V_WIDTH
C_GATE_B = C_CODE + LANES
C_END = C_GATE_B + GLA_V_WIDTH


def _cparams(sem):
    return pltpu.CompilerParams(dimension_semantics=sem, vmem_limit_bytes=VMEM_LIMIT)


def _full(shape):
    n = len(shape)
    return pl.BlockSpec(shape, lambda *_: (0,) * n)


def _dot(a, b):
    return jnp.dot(a, b, preferred_element_type=F32)


def _dot_nt(a, b):
    return lax.dot_general(a, b, (((1,), (1,)), ((), ())), preferred_element_type=F32)


def _dot_tn(a, b):
    return lax.dot_general(a, b, (((0,), (0,)), ((), ())), preferred_element_type=F32)


def _split_bf16(x):
    hi = x.astype(BF16)
    lo = (x - hi.astype(F32)).astype(BF16)
    return hi, lo


def _rms_rows(x, g):
    return x * lax.rsqrt(jnp.mean(x * x, axis=-1, keepdims=True) + NORM_EPS) * g


def _sigmoid(x):
    return 1.0 / (1.0 + jnp.exp(-x))


def _silu(x):
    return x * _sigmoid(x)


def _row_tile(m, tile=ROW_TILE):
    return tile if m % tile == 0 else m


def _even_in_kernel(x_ref, g_ref, w_ref, wgate_ref, bgate_ref,
                    u_ref, u2_ref, ga_ref, q_ref, k_ref, v_ref, la_ref, gb_ref, uscr_ref, *, t):
    xb = _rms_rows(x_ref[...], g_ref[...]).astype(BF16)

    def proj(lo, hi):
        return _dot(xb, w_ref[:, lo:hi])

    u = proj(C_U, C_GATE_A)
    nrow = u.shape[0] // t
    for o in range(u_ref.shape[0]):
        uo = u[:, o * LANES:(o + 1) * LANES]
        u_ref[o] = uo.astype(u_ref.dtype)
        uscr_ref[o] = uo
        for tt in range(t):
            piece = uscr_ref[o, pl.ds(tt, nrow, stride=t), :]
            u2_ref[o, :, tt * LANES:(tt + 1) * LANES] = piece.astype(u2_ref.dtype)
    ga_ref[...] = _silu(proj(C_GATE_A, C_Q)).astype(ga_ref.dtype)
    q_ref[...] = (proj(C_Q, C_K) * (GLA_HEAD_K ** -0.5)).astype(q_ref.dtype)
    k_ref[...] = proj(C_K, C_V).astype(k_ref.dtype)
    v_ref[...] = proj(C_V, C_CODE).astype(v_ref.dtype)
    gb_ref[...] = _silu(proj(C_GATE_B, C_END)).astype(gb_ref.dtype)
    a_low = proj(C_CODE, C_GATE_B)
    logit = _dot(a_low.astype(BF16), wgate_ref[...]) + bgate_ref[...]
    log_sig = jnp.minimum(logit, 0.0) - jnp.log1p(jnp.exp(-jnp.abs(logit)))
    la_ref[...] = log_sig * (1.0 / GLA_GATE_TAU)


def _even_in(x, g, w, wgate, bgate, act_dtype, t):
    m, d = x.shape
    tm = _row_tile(m, 2 * ROW_TILE) if m > ROW_TILE else m
    row = lambda n: pl.BlockSpec((tm, n), lambda i: (i, 0))
    assert w.shape[1] == C_END
    no = S5_WIDTH // LANES
    slab = pl.BlockSpec((no, tm, LANES), lambda i: (0, i, 0))
    chunk = pl.BlockSpec((no, tm // t, t * LANES), lambda i: (0, i, 0))
    outs = [(S5_WIDTH, act_dtype), (GLA_K_WIDTH, act_dtype), (GLA_K_WIDTH, act_dtype), (GLA_V_WIDTH, act_dtype),
            (GLA_K_WIDTH, F32), (GLA_V_WIDTH, act_dtype)]
    return pl.pallas_call(
        functools.partial(_even_in_kernel, t=t),
        grid=(m // tm,),
        in_specs=[row(d), _full(g.shape), _full(w.shape), _full(wgate.shape), _full(bgate.shape)],
        out_specs=[slab, chunk] + [row(n) for n, _ in outs],
        out_shape=[jax.ShapeDtypeStruct((no, m, LANES), act_dtype),
                   jax.ShapeDtypeStruct((no, m // t, t * LANES), act_dtype)]
        + [jax.ShapeDtypeStruct((m, n), dt) for n, dt in outs],
        scratch_shapes=[pltpu.VMEM((no, tm, LANES), F32)],
        compiler_params=_cparams(("parallel",)),
        name="even_in",
    )(x, g, w, wgate, bgate)


def _group_mask(shape, row_span, col_span):
    rg = (lax.broadcasted_iota(jnp.int32, shape, 0) // row_span) % OCT
    cg = (lax.broadcasted_iota(jnp.int32, shape, 1) // col_span) % OCT
    return rg == cg


def _s5_state_kernel(u_ref, bre_ref, bim_ref, x0_ref, are_ref, aim_ref, xs_ref, xf_ref, loc_ref, *, nseq, nchunks):
    hw = OCT * S5_STATE
    kk = u_ref.shape[2]
    bp = jnp.concatenate([bre_ref[0]] * OCT + [bim_ref[0]] * OCT, axis=1)
    bp = jnp.where(_group_mask((kk, 2 * hw), S5_GROUP, S5_STATE), bp, 0.0).astype(BF16)
    loc_ref[...] = _dot(u_ref[0], bp)
    a_re = are_ref[0]
    a_im = aim_ref[0]
    if nchunks == 1:
        x0 = x0_ref[0]
        xr, xi = x0[:, :hw], x0[:, hw:]
        loc = loc_ref[...]
        xf_ref[0, :, :hw] = a_re * xr - a_im * xi + loc[:, :hw]
        xf_ref[0, :, hw:] = a_re * xi + a_im * xr + loc[:, hw:]
        xs_ref[0] = x0.astype(xs_ref.dtype)
    else:
        def body(j, carry):
            new = []
            for b in range(nseq):
                xr, xi = carry[b]
                row = pl.ds(b * nchunks + j, 1)
                lr = loc_ref[row, :hw]
                li = loc_ref[row, hw:]
                loc_ref[row, :hw] = xr
                loc_ref[row, hw:] = xi
                new.append((a_re * xr - a_im * xi + lr, a_re * xi + a_im * xr + li))
            return tuple(new)

        init = tuple((x0_ref[0, b:b + 1, :hw], x0_ref[0, b:b + 1, hw:]) for b in range(nseq))
        fin = lax.fori_loop(0, nchunks, body, init, unroll=4)
        for b in range(nseq):
            xf_ref[0, b:b + 1, :hw] = fin[b][0]
            xf_ref[0, b:b + 1, hw:] = fin[b][1]
        xs_ref[0] = loc_ref[...].astype(xs_ref.dtype)


def _s5_state(u2, bre, bim, x0, are, aim, nseq, nchunks):
    no, r, kk = u2.shape
    sw = x0.shape[2]
    blk = lambda a: pl.BlockSpec((1,) + a.shape[1:], lambda o: (o,) + (0,) * (a.ndim - 1))
    return pl.pallas_call(
        functools.partial(_s5_state_kernel, nseq=nseq, nchunks=nchunks),
        grid=(no,),
        in_specs=[blk(u2), blk(bre), blk(bim), blk(x0), blk(are), blk(aim)],
        out_specs=[pl.BlockSpec((1, r, sw), lambda o: (o, 0, 0)), blk(x0)],
        out_shape=[jax.ShapeDtypeStruct((no, r, sw), BF16), jax.ShapeDtypeStruct(x0.shape, F32)],
        scratch_shapes=[pltpu.VMEM((r, sw), F32)],
        compiler_params=_cparams(("parallel",)),
        name="s5_state",
    )(u2, bre, bim, x0, are, aim)


def _dot_nt_f32(a, b):
    a_hi, a_lo = _split_bf16(a)
    b_hi, b_lo = _split_bf16(b)
    return _dot_nt(a_hi, b_hi) + _dot_nt(a_hi, b_lo) + _dot_nt(a_lo, b_hi)


def _s5_out_kernel(u_ref, xs_ref, bre_ref, bim_ref, cre_ref, cim_ref, zre_ref, zim_ref, y_ref, yscr_ref):
    r, kk = u_ref.shape[1], u_ref.shape[2]
    t = kk // LANES
    sw = xs_ref.shape[2]
    ntile = kk // MXU_TILE
    taps = _dot_nt_f32(bre_ref[0], zre_ref[0]) - _dot_nt_f32(bim_ref[0], zim_ref[0])
    tmask = _group_mask((LANES, LANES), S5_GROUP, S5_GROUP)

    def tap(lag):
        if lag < 0:
            return jnp.zeros((LANES, LANES), F32)
        s = t - 1 - lag
        return jnp.where(tmask, taps[s * LANES:(s + 1) * LANES], 0.0)

    wts = [jnp.concatenate([jnp.concatenate([tap(2 * d), tap(2 * d + 1)], axis=1),
                            jnp.concatenate([tap(2 * d - 1), tap(2 * d)], axis=1)], axis=0).astype(BF16)
           for d in range(ntile)]
    cpt = jnp.concatenate([cre_ref[0]] * OCT + [cim_ref[0]] * OCT, axis=1)
    cpt = jnp.where(_group_mask((kk, sw), S5_GROUP, S5_STATE), cpt, 0.0).astype(BF16)
    xs = xs_ref[0]
    for n in range(ntile):
        acc = _dot_nt(xs, cpt[n * MXU_TILE:(n + 1) * MXU_TILE])
        for k in range(n + 1):
            acc = acc + _dot(u_ref[0, :, k * MXU_TILE:(k + 1) * MXU_TILE], wts[n - k])
        for e in range(MXU_TILE // LANES):
            yscr_ref[pl.ds(2 * n + e, r, stride=t), :] = acc[:, e * LANES:(e + 1) * LANES]
    y_ref[0] = yscr_ref[...].astype(y_ref.dtype)


def _s5_out(u2, xs, bre, bim, cre, cim, zre, zim, out_dtype):
    no, r, kk = u2.shape
    m = r * (kk // LANES)
    blk = lambda a: pl.BlockSpec((1,) + a.shape[1:], lambda o: (o,) + (0,) * (a.ndim - 1))
    ops = (u2, xs, bre, bim, cre, cim, zre, zim)
    return pl.pallas_call(
        _s5_out_kernel,
        grid=(no,),
        in_specs=[blk(a) for a in ops],
        out_specs=pl.BlockSpec((1, m, LANES), lambda o: (o, 0, 0)),
        out_shape=jax.ShapeDtypeStruct((no, m, LANES), out_dtype),
        scratch_shapes=[pltpu.VMEM((m, LANES), F32)],
        compiler_params=_cparams(("parallel",)),
        name="s5_out",
    )(*ops)


def _s5_params(lam_re, lam_im, log_dt, b_re, b_im, c_re, c_im):
    t = S5_CHUNK
    ng = lam_re.shape[0]
    no = ng // OCT
    dt = jnp.exp(log_dt)[:, None]
    a = lam_re * dt
    b = lam_im * dt
    n = jnp.arange(t + 1, dtype=F32)[None, :, None]
    mag = jnp.exp(n * a[:, None, :])
    pw_re = mag * jnp.cos(n * b[:, None, :])
    pw_im = mag * jnp.sin(n * b[:, None, :])
    em1_re = jnp.expm1(a) * jnp.cos(b) - 2.0 * jnp.sin(0.5 * b) ** 2
    em1_im = jnp.exp(a) * jnp.sin(b)
    den = lam_re * lam_re + lam_im * lam_im
    z_re = (em1_re * lam_re + em1_im * lam_im) / den
    z_im = (em1_im * lam_re - em1_re * lam_im) / den
    bb_re = z_re[..., None] * b_re - z_im[..., None] * b_im
    bb_im = z_re[..., None] * b_im + z_im[..., None] * b_re
    def rows(w):
        return w.reshape(no, OCT, t, S5_GROUP, S5_STATE).transpose(0, 2, 1, 3, 4).reshape(no, t * LANES, S5_STATE)

    bt_re = bb_re.transpose(0, 2, 1)[:, None]
    bt_im = bb_im.transpose(0, 2, 1)[:, None]
    r_re, r_im = pw_re[:, t - 1::-1, None, :], pw_im[:, t - 1::-1, None, :]
    bre = rows(r_re * bt_re - r_im * bt_im)
    bim = rows(r_re * bt_im + r_im * bt_re)
    o_re, o_im = pw_re[:, 1:, None, :], pw_im[:, 1:, None, :]
    cre = rows(c_re[:, None] * o_re - c_im[:, None] * o_im)
    cim = rows(-(c_re[:, None] * o_im + c_im[:, None] * o_re))
    return bre, bim, cre, cim, c_re.reshape(no, LANES, S5_STATE), c_im.reshape(no, LANES, S5_STATE), pw_re, pw_im


def _s5_chunk(seqlen):
    return math.gcd(seqlen, S5_CHUNK)


def _s5_branch(u2, x0, ops, nseq, seqlen, out_dtype):
    bre, bim, cre, cim, zre, zim, pw_re, pw_im = ops
    no = u2.shape[0]
    t = _s5_chunk(seqlen)
    kk = t * LANES
    nchunks = seqlen // t
    are = pw_re[:, t].reshape(no, 1, OCT * S5_STATE)
    aim = pw_im[:, t].reshape(no, 1, OCT * S5_STATE)
    tail = S5_CHUNK * LANES - kk
    bre, bim = bre[:, tail:], bim[:, tail:]
    xs, xf = _s5_state(u2, bre, bim, x0, are, aim, nseq, nchunks)
    return _s5_out(u2, xs, bre, bim, cre[:, :kk], cim[:, :kk], zre, zim, out_dtype), xf


def _s5_state_in(re, im, no):
    nseq = re.shape[0]
    f = lambda a: a.reshape(nseq, no, OCT * S5_STATE).transpose(1, 0, 2)
    return jnp.concatenate([f(re), f(im)], axis=-1)


def _s5_state_out(xf):
    no, nseq, _ = xf.shape
    hw = OCT * S5_STATE
    f = lambda a: a.transpose(1, 0, 2).reshape(1, nseq, no * OCT, S5_STATE)
    return f(xf[:, :, :hw]), f(xf[:, :, hw:])


def _gla_kernel(q_ref, k_ref, v_ref, la_ref, gb_ref, s0_ref, ng_ref, o_ref, sf_ref, st_ref,
                *, nseq, nchunks, c):
    i = pl.program_id(1)

    @pl.when(i == 0)
    def _():
        st_ref[...] = s0_ref[...]

    nh, hk, hv = GLA_HEADS, GLA_HEAD_K, GLA_HEAD_V
    nch = nseq * nchunks
    tm = nch * c
    iota = lambda shape, d: lax.broadcasted_iota(jnp.int32, shape, d)
    ng = ng_ref[...]

    tb = min(tm, MXU_TILE)
    rt, ct = iota((tb, tb), 0), iota((tb, tb), 1)
    tril = ((rt // c == ct // c) & (rt >= ct)).astype(BF16)
    parts = []
    for r0 in range(0, tm, tb):
        la_hi, la_lo = _split_bf16(la_ref[r0:r0 + tb, :])
        parts.append(_dot(tril, la_hi) + _dot(tril, la_lo))
    bcum = jnp.concatenate(parts, axis=0)
    e_hi, e_lo = _split_bf16(jnp.exp(jnp.concatenate([bcum[(ci + 1) * c - 1:(ci + 1) * c] for ci in range(nch)],
                                                     axis=0)))
    pick = (iota((nch, LANES), 0) == iota((nch, LANES), 1)).astype(BF16)
    dec_t = _dot_tn(e_hi, pick) + _dot_tn(e_lo, pick)
    q_all = q_ref[...].astype(F32)
    k_all = k_ref[...].astype(F32)
    q_dec_all = q_all * jnp.exp(bcum)
    k_dec_all = k_all * jnp.exp(-bcum)

    own_k = iota((nh * c, nh * hk), 0) // c == iota((nh * c, nh * hk), 1) // hk
    own_v = iota((nh * c, nh * hv), 0) // c == iota((nh * c, nh * hv), 1) // hv
    causal = iota((c, nh * c), 1) % c <= iota((c, nh * c), 0)
    zero_v = jnp.zeros((hk, hv), F32)

    def intra(ci):
        rows = slice(ci * c, (ci + 1) * c)
        q_dec = q_dec_all[rows].astype(BF16)
        k_dec = k_dec_all[rows]
        b_c = bcum[rows]
        k_tail = (k_all[rows] * jnp.exp(b_c[c - 1:c] - b_c)).astype(BF16)
        v = v_ref[rows, :].astype(F32)
        k_bd = jnp.where(own_k, jnp.concatenate([k_dec] * nh, axis=0), 0.0).astype(BF16)
        v_bd = jnp.where(own_v, jnp.concatenate([v] * nh, axis=0), 0.0).astype(BF16)
        att = jnp.where(causal, _dot_nt(q_dec, k_bd), 0.0)
        o_intra = _dot(att.astype(BF16), v_bd)
        vb = v.astype(BF16)
        kvs = []
        for h0 in range(0, nh, 2):
            kv2 = _dot_tn(k_tail[:, h0 * hk:(h0 + 2) * hk], vb[:, h0 * hv:(h0 + 2) * hv])
            kvs += [kv2[:hk, :hv], kv2[hk:, hv:]]
        return q_dec, o_intra, kvs

    def carry(ci, sts, q_dec, o_intra, kvs):
        st_bd = jnp.concatenate(
            [jnp.concatenate([zero_v] * h + [sts[h]] + [zero_v] * (nh - 1 - h), axis=1) for h in range(nh)],
            axis=0).astype(BF16)
        o = o_intra + _dot(q_dec, st_bd)
        new = [dec_t[h * hk:(h + 1) * hk, ci:ci + 1] * sts[h] + kvs[h] for h in range(nh)]
        return o, new

    def finish(ci, o):
        rows = slice(ci * c, (ci + 1) * c)
        for h in range(nh):
            vs = slice(h * hv, (h + 1) * hv)
            oh = _rms_rows(o[:, vs], ng) * gb_ref[rows, vs].astype(F32)
            o_ref[rows, vs] = oh.astype(o_ref.dtype)

    sts = None
    pending = intra(0)
    unfinished = None
    for ci in range(nch):
        s, first, last = ci // nchunks, ci % nchunks == 0, ci % nchunks == nchunks - 1
        current = pending
        if ci + 1 < nch:
            pending = intra(ci + 1)
        if first:
            sts = [st_ref[s, h] for h in range(nh)]
        o, sts = carry(ci, sts, *current)
        if last:
            for h in range(nh):
                st_ref[s, h] = sts[h]
        if unfinished is not None:
            finish(*unfinished)
        unfinished = (ci, o)
    finish(*unfinished)

    @pl.when(i == pl.num_programs(1) - 1)
    def _():
        sf_ref[...] = st_ref[...]


def _gla(q, k, v, la, gb, s0, ng, nseq_total, seqlen, out_dtype):
    c = math.gcd(seqlen, GLA_CHUNK)
    if seqlen >= 16 * c:
        nseq, nchunks = 1, 16
    else:
        nseq, nchunks = 16, seqlen // c
    assert nseq_total % nseq == 0 and seqlen % (nchunks * c) == 0
    nblk = seqlen // (nchunks * c)
    tm = nseq * nchunks * c
    row = lambda n: pl.BlockSpec((tm, n), lambda b, i: (b * nblk + i, 0))
    st_spec = pl.BlockSpec((nseq, GLA_HEADS, GLA_HEAD_K, GLA_HEAD_V), lambda b, i: (b, 0, 0, 0))
    m = q.shape[0]
    return pl.pallas_call(
        functools.partial(_gla_kernel, nseq=nseq, nchunks=nchunks, c=c),
        grid=(nseq_total // nseq, nblk),
        in_specs=[row(GLA_K_WIDTH), row(GLA_K_WIDTH), row(GLA_V_WIDTH), row(GLA_K_WIDTH), row(GLA_V_WIDTH),
                  st_spec, _full(ng.shape)],
        out_specs=[row(GLA_V_WIDTH), st_spec],
        out_shape=[jax.ShapeDtypeStruct((m, GLA_V_WIDTH), out_dtype), jax.ShapeDtypeStruct(s0.shape, F32)],
        scratch_shapes=[pltpu.VMEM((nseq, GLA_HEADS, GLA_HEAD_K, GLA_HEAD_V), F32)],
        compiler_params=_cparams(("parallel", "arbitrary")),
        name="gla",
    )(q, k, v, la, gb, s0, ng)


def _gelu_tanh(x):
    return 0.5 * x * (1.0 + jnp.tanh(math.sqrt(2.0 / math.pi) * (x + 0.044715 * (x * x * x))))


def _head_ones(n):
    r = lax.broadcasted_iota(jnp.int32, (n, n), 0) // SWA_HEAD_DIM
    c = lax.broadcasted_iota(jnp.int32, (n, n), 1) // SWA_HEAD_DIM
    return (r == c).astype(BF16)


def _rope_block(x, cos_t, sin_t, upper):
    half = SWA_HEAD_DIM // 2
    swapped = jnp.where(upper, pltpu.roll(x, half, 1), pltpu.roll(x, LANES - half, 1))
    return x * cos_t + swapped * sin_t


def _mid_kernel(x_ref, y_ref, u_ref, ga_ref, ob_ref, d_ref, wglu_ref, bglu_ref, wo_ref,
                g_ref, w_ref, gq_ref, gk_ref, cos_ref, sin_ref, h_ref, q_ref, k_ref, v_ref, gate_ref):
    tm = x_ref.shape[0]
    lane = lax.broadcasted_iota(jnp.int32, (tm, LANES), 1)
    upper = (lane & (SWA_HEAD_DIM // 2)) != 0
    ones4 = _head_ones(MXU_TILE)
    inv_d = 1.0 / SWA_HEAD_DIM
    qw = SWA_HEADS * SWA_HEAD_DIM
    kw = SWA_KV_HEADS * SWA_HEAD_DIM
    nblk = qw // MXU_TILE
    no = y_ref.shape[0]

    na = ga_ref.shape[1]
    mix_b = _dot(ob_ref[...].astype(BF16), wo_ref[na:, :])
    y = jnp.concatenate([y_ref[o].astype(F32) for o in range(no)], axis=1)
    u = jnp.concatenate([u_ref[o].astype(F32) for o in range(no)], axis=1)
    z = _gelu_tanh(y + d_ref[...] * u)
    z = z * _sigmoid(_dot(z.astype(BF16), wglu_ref[...]) + bglu_ref[...])
    out_a = z * ga_ref[...].astype(F32)
    h = x_ref[...] + (_dot(out_a.astype(BF16), wo_ref[:na, :]) + mix_b)
    h_ref[...] = h

    xb = _rms_rows(h, g_ref[...]).astype(BF16)
    cos_t = cos_ref[...]
    sin_t = sin_ref[...]

    def gate_finish(gate):
        gate_ref[...] = _silu(gate).astype(gate_ref.dtype)

    def q_finish(j, q):
        ss = _dot((q * q).astype(BF16), ones4)
        qn = q * lax.rsqrt(ss * inv_d + NORM_EPS) * gq_ref[...]
        for e in range(MXU_TILE // LANES):
            cols = slice(j * MXU_TILE + e * LANES, j * MXU_TILE + (e + 1) * LANES)
            qe = _rope_block(qn[:, e * LANES:(e + 1) * LANES], cos_t, sin_t, upper)
            q_ref[:, cols] = (qe * (SWA_HEAD_DIM ** -0.5 * LOG2E)).astype(q_ref.dtype)

    def kv_finish(kv):
        k = kv[:, :kw]
        ss = _dot((k * k).astype(BF16), ones4[:kw, :kw])
        kn = k * lax.rsqrt(ss * inv_d + NORM_EPS) * gk_ref[...]
        k_ref[...] = _rope_block(kn, cos_t, sin_t, upper)
        v_ref[...] = kv[:, kw:]

    work = [(slice(j * MXU_TILE, (j + 1) * MXU_TILE), functools.partial(q_finish, j)) for j in range(nblk)]
    work += [(slice(qw, qw + 2 * kw), kv_finish)]
    work += [(slice(qw + 2 * kw, None), gate_finish)]
    pending = _dot(xb, w_ref[:, work[0][0]])
    for n, (_, finish) in enumerate(work):
        current = pending
        if n + 1 < len(work):
            pending = _dot(xb, w_ref[:, work[n + 1][0]])
        finish(current)


def _mid(x, y, u, ga, ob, pe, po, cos_t, sin_t, act_dtype):
    m, dm = x.shape
    tm = _row_tile(m)
    row = lambda n: pl.BlockSpec((tm, n), lambda i: (i, 0))
    slab = pl.BlockSpec((y.shape[0], tm, LANES), lambda i: (0, i, 0))
    assert cos_t.shape[0] % tm == 0
    nper = cos_t.shape[0] // tm
    tab = pl.BlockSpec((tm, LANES), lambda i: (i % nper, 0))
    qw = SWA_HEADS * SWA_HEAD_DIM
    kw = SWA_KV_HEADS * SWA_HEAD_DIM
    weights = [pe['d'], pe['wglu'], pe['bglu'], pe['wo'], po['norm_g'], po['w_in'], po['gq'], po['gk']]
    return pl.pallas_call(
        _mid_kernel,
        grid=(m // tm,),
        in_specs=[row(dm), slab, slab, row(S5_WIDTH), row(GLA_V_WIDTH)] + [_full(w.shape) for w in weights]
        + [tab, tab],
        out_specs=[row(dm), row(qw), row(kw), row(kw), row(qw)],
        out_shape=[jax.ShapeDtypeStruct((m, dm), F32),
                   jax.ShapeDtypeStruct((m, qw), act_dtype), jax.ShapeDtypeStruct((m, kw), F32),
                   jax.ShapeDtypeStruct((m, kw), F32), jax.ShapeDtypeStruct((m, qw), act_dtype)],
        compiler_params=_cparams(("parallel",)),
        name="mid",
    )(x, y, u, ga, ob, *weights, cos_t, sin_t)


def _rope_tables(pos):
    half = SWA_HEAD_DIM // 2
    inv_freq = ROPE_THETA ** (-np.arange(half, dtype=np.float64) / half)
    ang = np.asarray(pos, np.float64)[:, None] * inv_freq[None, :]
    cos, sin = np.cos(ang), np.sin(ang)
    cos_t = np.concatenate([cos, cos, cos, cos], axis=1).astype(np.float32)
    sin_t = np.concatenate([-sin, sin, -sin, sin], axis=1).astype(np.float32)
    return jnp.asarray(cos_t), jnp.asarray(sin_t)


def _attn_prompt_kernel(sink_ref, q_ref, kc_ref, kp_ref, vc_ref, vp_ref, gate_ref, h_ref, wout_ref,
                        y_ref, p_ref, o_ref, *, nqb):
    i = pl.program_id(1)
    w = SWA_WINDOW
    hd = SWA_HEAD_DIM
    npair = SWA_GROUP // 2
    lane2 = lax.broadcasted_iota(jnp.int32, (2 * w, LANES), 1)
    low = lane2 < hd
    rr = lax.broadcasted_iota(jnp.int32, (w, w), 0)
    cc = lax.broadcasted_iota(jnp.int32, (w, w), 1)
    tri = cc <= rr
    low_w = cc < hd
    r4 = lax.broadcasted_iota(jnp.int32, (4 * w, LANES), 0)
    c4 = lax.broadcasted_iota(jnp.int32, (4 * w, LANES), 1)
    den_cols = ((r4 < 2 * w) == (c4 < hd)).astype(BF16)
    units = [(jb, kv) for jb in range(nqb) for kv in range(SWA_KV_HEADS)]

    def scores(jb, kv):
        rows = slice(jb * w, (jb + 1) * w)
        if jb == 0:
            k_prev, v_prev = kp_ref[...], vp_ref[...]
        else:
            prev_rows = slice((jb - 1) * w, jb * w)
            k_prev, v_prev = kc_ref[prev_rows, :], vc_ref[prev_rows, :]
        kcat = jnp.concatenate([k_prev, kc_ref[rows, :]], axis=0)
        vcat = jnp.concatenate([v_prev, vc_ref[rows, :]], axis=0)
        own = low if kv == 0 else jnp.logical_not(low)
        k_own = jnp.where(own, kcat, 0.0)
        v_own = jnp.where(own, vcat, 0.0)
        k_oth = pltpu.roll(k_own, hd, 1)
        v_oth = pltpu.roll(v_own, hd, 1)
        k_lo, k_hi = (k_own, k_oth) if kv == 0 else (k_oth, k_own)
        v_lo, v_hi = (v_own, v_oth) if kv == 0 else (v_oth, v_own)
        k_rhs = jnp.concatenate([k_lo, k_hi], axis=0).astype(BF16)
        v_rhs = jnp.concatenate([jnp.concatenate([v_lo, v_hi], axis=0).astype(BF16), den_cols], axis=1)
        qs = jnp.concatenate([q_ref[rows, (kv * npair + pr) * LANES:(kv * npair + pr + 1) * LANES]
                              for pr in range(npair)], axis=0)
        return _dot_nt(qs, k_rhs), v_rhs

    def softmax_pv(kv, first, buf, s_all, v_rhs):
        sink_terms = []
        for pr in range(npair):
            prow = slice(pr * w, (pr + 1) * w)
            pair_terms = []
            for e in range(2):
                s_prev = s_all[prow, (2 * e) * w:(2 * e + 1) * w]
                s_cur = s_all[prow, (2 * e + 1) * w:(2 * e + 2) * w]
                if first:
                    s_prev = jnp.where(i > 0, s_prev, -jnp.inf)
                sc = jnp.where(tri, s_cur, s_prev)
                sink = sink_ref[2 * (kv * npair + pr) + e] * LOG2E
                mx = jnp.maximum(jnp.max(sc, axis=-1, keepdims=True), sink)
                pe = jnp.exp2(sc - mx)
                p_ref[buf, prow, (2 * e) * w:(2 * e + 1) * w] = jnp.where(tri, 0.0, pe).astype(BF16)
                p_ref[buf, prow, (2 * e + 1) * w:(2 * e + 2) * w] = jnp.where(tri, pe, 0.0).astype(BF16)
                pair_terms.append(jnp.exp2(sink - mx))
            sink_terms.append(pair_terms)
        return _dot(p_ref[buf], v_rhs), sink_terms

    def normalise(jb, kv, o_ext, sink_terms):
        rows = slice(jb * w, (jb + 1) * w)
        for pr in range(npair):
            prow = slice(pr * w, (pr + 1) * w)
            cols = slice((kv * npair + pr) * LANES, (kv * npair + pr + 1) * LANES)
            st = jnp.where(low_w, sink_terms[pr][0], sink_terms[pr][1])
            o = o_ext[prow, :LANES] / (o_ext[prow, LANES:] + st)
            o_ref[rows, cols] = o.astype(o_ref.dtype)

    def project(rows):
        og = o_ref[rows, :].astype(F32) * gate_ref[rows, :].astype(F32)
        y_ref[rows, :] = h_ref[rows, :] + _dot(og.astype(BF16), wout_ref[...])

    group = 2
    pending = scores(*units[0])
    unfinished = None
    for n, (jb, kv) in enumerate(units):
        current = pending
        if n + 1 < len(units):
            pending = scores(*units[n + 1])
        result = softmax_pv(kv, jb == 0, n % 2, *current)
        if unfinished is not None:
            normalise(*unfinished)
            done_jb, done_kv = unfinished[:2]
            if done_kv == SWA_KV_HEADS - 1 and (done_jb + 1) % group == 0:
                project(slice((done_jb + 1 - group) * w, (done_jb + 1) * w))
        unfinished = (jb, kv) + result
    normalise(*unfinished)
    project(slice((nqb - group) * w, nqb * w))


def _attn_prompt(sinks, q, k, v, gate, h, w_out, nseq, seqlen):
    w = SWA_WINDOW
    nqb = 8
    tm = nqb * w
    assert seqlen % tm == 0
    nblk = seqlen // tm
    qw = q.shape[1]
    dm = h.shape[1]
    row = lambda n: pl.BlockSpec((tm, n), lambda b, i: (b * nblk + i, 0))
    prev = lambda n: pl.BlockSpec((w, n), lambda b, i: (jnp.maximum((b * nblk + i) * nqb - 1, 0), 0))
    smem = pl.BlockSpec(memory_space=pltpu.SMEM)
    return pl.pallas_call(
        functools.partial(_attn_prompt_kernel, nqb=nqb),
        grid=(nseq, nblk),
        in_specs=[smem, row(qw), row(LANES), prev(LANES), row(LANES), prev(LANES), row(qw), row(dm),
                  _full(w_out.shape)],
        out_specs=row(dm),
        out_shape=jax.ShapeDtypeStruct(h.shape, F32),
        scratch_shapes=[pltpu.VMEM((2, 4 * w, 4 * w), BF16), pltpu.VMEM((tm, qw), BF16)],
        compiler_params=_cparams(("parallel", "arbitrary")),
        name="attn_prompt",
    )(sinks, q, k, k, v, v, gate, h, w_out)


def _attn_sample_kernel(sink_ref, q_ref, kn_ref, vn_ref, kc_ref, vc_ref, o_ref, ko_ref, vo_ref, q2_ref, *, seqlen):
    nq = SWA_HEADS * seqlen
    hd = SWA_HEAD_DIM
    heads = [(kv, g) for kv in range(SWA_KV_HEADS) for g in range(SWA_GROUP)]

    def stacked(kv, g):
        r0 = (kv * SWA_GROUP + g) * seqlen
        return slice(r0, r0 + seqlen), slice(kv * hd, (kv + 1) * hd)

    q2_ref[...] = jnp.zeros(q2_ref.shape, q2_ref.dtype)
    for kv, g in heads:
        rows, lanes = stacked(kv, g)
        h = kv * SWA_GROUP + g
        q2_ref[:, rows, lanes] = q_ref[:, :, h * hd:(h + 1) * hd]
    ncache = kc_ref.shape[1]
    t_row = lax.broadcasted_iota(jnp.int32, (nq, ncache), 0) % seqlen
    c_col = lax.broadcasted_iota(jnp.int32, (nq, ncache), 1)
    cache_ok = c_col > t_row - (SWA_WINDOW - ncache)
    nnew = kn_ref.shape[1]
    t_row_n = lax.broadcasted_iota(jnp.int32, (nq, nnew), 0) % seqlen
    n_col = lax.broadcasted_iota(jnp.int32, (nq, nnew), 1)
    new_ok = n_col <= t_row_n
    sink = (sink_ref[...] * LOG2E)[None, :, 0:1]
    bqk = lambda a, b: lax.dot_general(a, b, (((2,), (2,)), ((0,), (0,))), preferred_element_type=F32)
    bpv = lambda a, b: lax.dot_general(a, b, (((2,), (1,)), ((0,), (0,))), preferred_element_type=F32)
    q = q2_ref[...].astype(BF16)
    kc = kc_ref[...]
    vc = vc_ref[...]
    kn = kn_ref[...]
    vn = vn_ref[...]
    sc = jnp.where(cache_ok[None], bqk(q, kc.astype(BF16)), -jnp.inf)
    sn = jnp.where(new_ok[None], bqk(q, kn.astype(BF16)), -jnp.inf)
    mx = jnp.maximum(jnp.maximum(jnp.max(sc, axis=-1, keepdims=True), jnp.max(sn, axis=-1, keepdims=True)), sink)
    pc = jnp.exp2(sc - mx)
    pn = jnp.exp2(sn - mx)
    den = jnp.sum(pc, axis=-1, keepdims=True) + jnp.sum(pn, axis=-1, keepdims=True) + jnp.exp2(sink - mx)
    inv = 1.0 / den
    o2 = bpv((pc * inv).astype(BF16), vc.astype(BF16)) + bpv((pn * inv).astype(BF16), vn.astype(BF16))
    for kv, g in heads:
        rows, lanes = stacked(kv, g)
        h = kv * SWA_GROUP + g
        o_ref[:, :, h * hd:(h + 1) * hd] = o2[:, rows, lanes]
    keep = ncache - seqlen
    ko_ref[:, 0:keep, :] = kc[:, seqlen:ncache, :]
    ko_ref[:, keep:ncache, :] = kn[:, 0:seqlen, :]
    vo_ref[:, 0:keep, :] = vc[:, seqlen:ncache, :]
    vo_ref[:, keep:ncache, :] = vn[:, 0:seqlen, :]


def _attn_sample(sink_rows, q, kn, vn, kc, vc):
    n, seqlen, _ = q.shape
    nseq = 16
    assert n % nseq == 0
    blk = lambda a: pl.BlockSpec((nseq,) + a.shape[1:], lambda i: (i, 0, 0))
    return pl.pallas_call(
        functools.partial(_attn_sample_kernel, seqlen=seqlen),
        grid=(n // nseq,),
        in_specs=[_full(sink_rows.shape), blk(q), blk(kn), blk(vn), blk(kc), blk(vc)],
        out_specs=[blk(q), blk(kc), blk(vc)],
        out_shape=[jax.ShapeDtypeStruct(q.shape, F32), jax.ShapeDtypeStruct(kc.shape, F32),
                   jax.ShapeDtypeStruct(vc.shape, F32)],
        scratch_shapes=[pltpu.VMEM((nseq, SWA_HEADS * seqlen, SWA_KV_HEADS * SWA_HEAD_DIM), F32)],
        compiler_params=_cparams(("parallel",)),
        name="attn_sample",
    )(sink_rows, q, kn, vn, kc, vc)


def _odd_out_kernel(h_ref, o_ref, gate_ref, w_ref, y_ref):
    og = o_ref[...].astype(F32) * gate_ref[...].astype(F32)
    y_ref[...] = h_ref[...] + _dot(og.astype(BF16), w_ref[...])


def _odd_out(h, o, gate, w):
    m, dm = h.shape
    tm = _row_tile(m)
    row = lambda n: pl.BlockSpec((tm, n), lambda i: (i, 0))
    return pl.pallas_call(
        _odd_out_kernel,
        grid=(m // tm,),
        in_specs=[row(dm), row(o.shape[1]), row(gate.shape[1]), _full(w.shape)],
        out_specs=row(dm),
        out_shape=jax.ShapeDtypeStruct((m, dm), F32),
        compiler_params=_cparams(("parallel",)),
        name="odd_out",
    )(h, o, gate, w)


def _trunk(x, s5_x0, gla_s0, pe, po, pos, nseq, seqlen, act_dtype):
    u4, u2, ga, q, k, v, la, gb = _even_in(x, pe['norm_g'], pe['w_in'], pe['wgate'], pe['bgate'], act_dtype,
                                           _s5_chunk(seqlen))
    y4, s5_fin = _s5_branch(u2, s5_x0, pe['s5_ops'], nseq, seqlen, act_dtype)
    ob, gla_fin = _gla(q, k, v, la, gb, gla_s0, pe['gla_norm_g'], nseq, seqlen, act_dtype)
    cos_t, sin_t = _rope_tables(pos)
    h, q1, k1, v1, gate = _mid(x, y4, u4, ga, ob, pe, po, cos_t, sin_t, act_dtype)
    return h, q1, k1, v1, gate, s5_fin, gla_fin


def kernel(x_prompt, x_sample, state_s5_re, state_s5_im, state_gla, cache_swa_k, cache_swa_v,
           even_norm_g, even_w_in, s5_lambda_re, s5_lambda_im, s5_log_dt, s5_b_re, s5_b_im,
           s5_c_re, s5_c_im, s5_d, s5_w_glu, s5_b_glu, gla_w_gate, gla_b_gate, gla_norm_g,
           even_w_out, odd_norm_g, odd_w_in, swa_q_norm_g, swa_k_norm_g, swa_sinks, odd_w_out):
    nb, seq, dm = x_prompt.shape
    ns, dseq, _ = x_sample.shape
    ng = s5_lambda_re.shape[1]
    no = ng // OCT
    xp = x_prompt.reshape(nb * seq, dm)
    xs = x_sample.reshape(ns * dseq, dm)

    i = 0
    w_in = even_w_in[i]
    assert ng * S5_GROUP == S5_WIDTH
    col_alow = C_CODE
    col_gb = col_alow + GLA_GATE_RANK
    pad_rank = LANES - GLA_GATE_RANK
    pe = {
        'norm_g': even_norm_g[i][None, :],
        'w_in': jnp.concatenate([w_in[:, :col_alow], jnp.pad(w_in[:, col_alow:col_gb], ((0, 0), (0, pad_rank))),
                                 w_in[:, col_gb:]], axis=1).astype(BF16),
        'wgate': jnp.pad(gla_w_gate[i], ((0, pad_rank), (0, 0))).astype(BF16),
        'bgate': gla_b_gate[i][None, :],
        's5_ops': _s5_params(s5_lambda_re[i], s5_lambda_im[i], s5_log_dt[i], s5_b_re[i], s5_b_im[i],
                             s5_c_re[i], s5_c_im[i]),
        'gla_norm_g': gla_norm_g[i][None, :],
        'd': s5_d[i][None, :],
        'wglu': s5_w_glu[i].astype(BF16),
        'bglu': s5_b_glu[i][None, :],
        'wo': even_w_out[i].astype(BF16),
    }
    po = {
        'norm_g': odd_norm_g[i][None, :],
        'w_in': odd_w_in[i].astype(BF16),
        'gq': jnp.tile(swa_q_norm_g[i], MXU_TILE // SWA_HEAD_DIM)[None, :],
        'gk': jnp.tile(swa_k_norm_g[i], LANES // SWA_HEAD_DIM)[None, :],
    }
    w_out = odd_w_out[i].astype(BF16)
    sinks = swa_sinks[i]
    kvw = SWA_KV_HEADS * SWA_HEAD_DIM

    s5_zero = jnp.zeros((no, nb, 2 * OCT * S5_STATE), F32)
    gla_zero = jnp.zeros((nb, GLA_HEADS, GLA_HEAD_K, GLA_HEAD_V), F32)
    hp, q, k, v, gate, s5_p, gla_p = _trunk(xp, s5_zero, gla_zero, pe, po, np.arange(seq), nb, seq, BF16)
    s5r_p, s5i_p = _s5_state_out(s5_p)
    y_prompt = _attn_prompt(sinks, q, k, v, gate, hp, w_out, nb, seq).reshape(nb, seq, dm)
    cache_len = min(SWA_WINDOW, seq)
    tail = lambda a: (a.reshape(nb, seq, kvw)[:, seq - cache_len:]
                      .reshape(1, nb, cache_len, SWA_KV_HEADS, SWA_HEAD_DIM))
    swk_p, swv_p = tail(k), tail(v)

    pos_s = np.tile(PAST_LEN + np.arange(dseq), ns)
    s5_init = _s5_state_in(state_s5_re[i], state_s5_im[i], no)
    hs, q, k, v, gate, s5_s, gla_s = _trunk(xs, s5_init, state_gla[i], pe, po, pos_s, ns, dseq, F32)
    s5r_s, s5i_s = _s5_state_out(s5_s)
    ncache = cache_swa_k.shape[2]
    sink_rows = jnp.broadcast_to(jnp.repeat(sinks, dseq)[:, None], (SWA_HEADS * dseq, LANES))
    kn = k.reshape(ns, dseq, kvw)
    vn = v.reshape(ns, dseq, kvw)
    npad = BF16_ROWS - dseq
    kn_pad = jnp.pad(kn, ((0, 0), (0, npad), (0, 0)))
    vn_pad = jnp.pad(vn, ((0, 0), (0, npad), (0, 0)))
    kc = cache_swa_k[i].reshape(ns, ncache, kvw)
    vc = cache_swa_v[i].reshape(ns, ncache, kvw)
    o, kc_new, vc_new = _attn_sample(sink_rows, q.reshape(ns, dseq, -1), kn_pad, vn_pad, kc, vc)
    y_sample = _odd_out(hs, o.reshape(ns * dseq, -1), gate, w_out).reshape(ns, dseq, dm)
    swk_s = kc_new.reshape(1, ns, ncache, SWA_KV_HEADS, SWA_HEAD_DIM)
    swv_s = vc_new.reshape(1, ns, ncache, SWA_KV_HEADS, SWA_HEAD_DIM)

    return (y_prompt, y_sample,
            s5r_p, s5i_p, gla_p[None], swk_p, swv_p,
            s5r_s, s5i_s, gla_s[None], swk_s, swv_s)
```

```python
import functools
import math

import jax
import jax.numpy as jnp
import numpy as np
from jax import lax
from jax.experimental import pallas as pl
from jax.experimental.pallas import tpu as pltpu

F32 = jnp.float32
BF16 = jnp.bfloat16

PAST_LEN = 8192
NORM_EPS = 1e-6
S5_GROUP = 16
S5_STATE = 64
S5_CHUNK = 16
GLA_HEADS = 4
GLA_HEAD_K = 64
GLA_HEAD_V = 128
GLA_GATE_RANK = 16
GLA_GATE_TAU = 16.0
GLA_CHUNK = 64
S5_WIDTH = 512
GLA_K_WIDTH = GLA_HEADS * GLA_HEAD_K
GLA_V_WIDTH = GLA_HEADS * GLA_HEAD_V
SWA_HEADS = 16
SWA_KV_HEADS = 2
SWA_GROUP = SWA_HEADS // SWA_KV_HEADS
SWA_HEAD_DIM = 64
SWA_WINDOW = 128
ROPE_THETA = 10000.0
LOG2E = math.log2(math.e)
LANES = 128
BF16_ROWS = 16
MXU_TILE = 256
OCT = LANES // S5_GROUP
ROW_TILE = 512
VMEM_LIMIT = 48 * 1024 * 1024
C_U = 0
C_GATE_A = C_U + S5_WIDTH
C_Q = C_GATE_A + S5_WIDTH
C_K = C_Q + GLA_K_WIDTH
C_V = C_K + GLA_K_WIDTH
C_CODE = C_V + GLA_V_WIDTH
C_GATE_B = C_CODE + LANES
C_END = C_GATE_B + GLA_V_WIDTH


def _cparams(sem):
    return pltpu.CompilerParams(dimension_semantics=sem, vmem_limit_bytes=VMEM_LIMIT)


def _full(shape):
    n = len(shape)
    return pl.BlockSpec(shape, lambda *_: (0,) * n)


def _dot(a, b):
    return jnp.dot(a, b, preferred_element_type=F32)


def _dot_nt(a, b):
    return lax.dot_general(a, b, (((1,), (1,)), ((), ())), preferred_element_type=F32)


def _dot_tn(a, b):
    return lax.dot_general(a, b, (((0,), (0,)), ((), ())), preferred_element_type=F32)


def _split_bf16(x):
    hi = x.astype(BF16)
    lo = (x - hi.astype(F32)).astype(BF16)
    return hi, lo


def _rms_rows(x, g):
    return x * lax.rsqrt(jnp.mean(x * x, axis=-1, keepdims=True) + NORM_EPS) * g


def _sigmoid(x):
    return 1.0 / (1.0 + jnp.exp(-x))


def _silu(x):
    return x * _sigmoid(x)


def _row_tile(m, tile=ROW_TILE):
    return tile if m % tile == 0 else m


def _even_in_kernel(x_ref, g_ref, w_ref, wgate_ref, bgate_ref,
                    u_ref, u2_ref, ga_ref, q_ref, k_ref, v_ref, la_ref, gb_ref, uscr_ref, *, t):
    xb = _rms_rows(x_ref[...], g_ref[...]).astype(BF16)

    def proj(lo, hi):
        return _dot(xb, w_ref[:, lo:hi])

    u = proj(C_U, C_GATE_A)
    nrow = u.shape[0] // t
    for o in range(u_ref.shape[0]):
        uo = u[:, o * LANES:(o + 1) * LANES]
        u_ref[o] = uo.astype(u_ref.dtype)
        uscr_ref[o] = uo
        for tt in range(t):
            piece = uscr_ref[o, pl.ds(tt, nrow, stride=t), :]
            u2_ref[o, :, tt * LANES:(tt + 1) * LANES] = piece.astype(u2_ref.dtype)
    ga_ref[...] = _silu(proj(C_GATE_A, C_Q)).astype(ga_ref.dtype)
    q_ref[...] = (proj(C_Q, C_K) * (GLA_HEAD_K ** -0.5)).astype(q_ref.dtype)
    k_ref[...] = proj(C_K, C_V).astype(k_ref.dtype)
    v_ref[...] = proj(C_V, C_CODE).astype(v_ref.dtype)
    gb_ref[...] = _silu(proj(C_GATE_B, C_END)).astype(gb_ref.dtype)
    a_low = proj(C_CODE, C_GATE_B)
    logit = _dot(a_low.astype(BF16), wgate_ref[...]) + bgate_ref[...]
    log_sig = jnp.minimum(logit, 0.0) - jnp.log1p(jnp.exp(-jnp.abs(logit)))
    la_ref[...] = log_sig * (1.0 / GLA_GATE_TAU)


def _even_in(x, g, w, wgate, bgate, act_dtype, t):
    m, d = x.shape
    tm = _row_tile(m, 2 * ROW_TILE) if m > ROW_TILE else m
    row = lambda n: pl.BlockSpec((tm, n), lambda i: (i, 0))
    assert w.shape[1] == C_END
    no = S5_WIDTH // LANES
    slab = pl.BlockSpec((no, tm, LANES), lambda i: (0, i, 0))
    chunk = pl.BlockSpec((no, tm // t, t * LANES), lambda i: (0, i, 0))
    outs = [(S5_WIDTH, act_dtype), (GLA_K_WIDTH, act_dtype), (GLA_K_WIDTH, act_dtype), (GLA_V_WIDTH, act_dtype),
            (GLA_K_WIDTH, F32), (GLA_V_WIDTH, act_dtype)]
    return pl.pallas_call(
        functools.partial(_even_in_kernel, t=t),
        grid=(m // tm,),
        in_specs=[row(d), _full(g.shape), _full(w.shape), _full(wgate.shape), _full(bgate.shape)],
        out_specs=[slab, chunk] + [row(n) for n, _ in outs],
        out_shape=[jax.ShapeDtypeStruct((no, m, LANES), act_dtype),
                   jax.ShapeDtypeStruct((no, m // t, t * LANES), act_dtype)]
        + [jax.ShapeDtypeStruct((m, n), dt) for n, dt in outs],
        scratch_shapes=[pltpu.VMEM((no, tm, LANES), F32)],
        compiler_params=_cparams(("parallel",)),
        name="even_in",
    )(x, g, w, wgate, bgate)


def _group_mask(shape, row_span, col_span):
    rg = (lax.broadcasted_iota(jnp.int32, shape, 0) // row_span) % OCT
    cg = (lax.broadcasted_iota(jnp.int32, shape, 1) // col_span) % OCT
    return rg == cg


def _s5_state_kernel(u_ref, bre_ref, bim_ref, x0_ref, are_ref, aim_ref, xs_ref, xf_ref, loc_ref, *, nseq, nchunks):
    hw = OCT * S5_STATE
    kk = u_ref.shape[2]
    bp = jnp.concatenate([bre_ref[0]] * OCT + [bim_ref[0]] * OCT, axis=1)
    bp = jnp.where(_group_mask((kk, 2 * hw), S5_GROUP, S5_STATE), bp, 0.0).astype(BF16)
    loc_ref[...] = _dot(u_ref[0], bp)
    a_re = are_ref[0]
    a_im = aim_ref[0]
    if nchunks == 1:
        x0 = x0_ref[0]
        xr, xi = x0[:, :hw], x0[:, hw:]
        loc = loc_ref[...]
        xf_ref[0, :, :hw] = a_re * xr - a_im * xi + loc[:, :hw]
        xf_ref[0, :, hw:] = a_re * xi + a_im * xr + loc[:, hw:]
        xs_ref[0] = x0.astype(xs_ref.dtype)
    else:
        def body(j, carry):
            new = []
            for b in range(nseq):
                xr, xi = carry[b]
                row = pl.ds(b * nchunks + j, 1)
                lr = loc_ref[row, :hw]
                li = loc_ref[row, hw:]
                loc_ref[row, :hw] = xr
                loc_ref[row, hw:] = xi
                new.append((a_re * xr - a_im * xi + lr, a_re * xi + a_im * xr + li))
            return tuple(new)

        init = tuple((x0_ref[0, b:b + 1, :hw], x0_ref[0, b:b + 1, hw:]) for b in range(nseq))
        fin = lax.fori_loop(0, nchunks, body, init, unroll=4)
        for b in range(nseq):
            xf_ref[0, b:b + 1, :hw] = fin[b][0]
            xf_ref[0, b:b + 1, hw:] = fin[b][1]
        xs_ref[0] = loc_ref[...].astype(xs_ref.dtype)


def _s5_state(u2, bre, bim, x0, are, aim, nseq, nchunks):
    no, r, kk = u2.shape
    sw = x0.shape[2]
    blk = lambda a: pl.BlockSpec((1,) + a.shape[1:], lambda o: (o,) + (0,) * (a.ndim - 1))
    return pl.pallas_call(
        functools.partial(_s5_state_kernel, nseq=nseq, nchunks=nchunks),
        grid=(no,),
        in_specs=[blk(u2), blk(bre), blk(bim), blk(x0), blk(are), blk(aim)],
        out_specs=[pl.BlockSpec((1, r, sw), lambda o: (o, 0, 0)), blk(x0)],
        out_shape=[jax.ShapeDtypeStruct((no, r, sw), BF16), jax.ShapeDtypeStruct(x0.shape, F32)],
        scratch_shapes=[pltpu.VMEM((r, sw), F32)],
        compiler_params=_cparams(("parallel",)),
        name="s5_state",
    )(u2, bre, bim, x0, are, aim)


def _dot_nt_f32(a, b):
    a_hi, a_lo = _split_bf16(a)
    b_hi, b_lo = _split_bf16(b)
    return _dot_nt(a_hi, b_hi) + _dot_nt(a_hi, b_lo) + _dot_nt(a_lo, b_hi)


def _s5_out_kernel(u_ref, xs_ref, bre_ref, bim_ref, cre_ref, cim_ref, zre_ref, zim_ref, y_ref, yscr_ref):
    r, kk = u_ref.shape[1], u_ref.shape[2]
    t = kk // LANES
    sw = xs_ref.shape[2]
    ntile = kk // MXU_TILE
    taps = _dot_nt_f32(bre_ref[0], zre_ref[0]) - _dot_nt_f32(bim_ref[0], zim_ref[0])
    tmask = _group_mask((LANES, LANES), S5_GROUP, S5_GROUP)

    def tap(lag):
        if lag < 0:
            return jnp.zeros((LANES, LANES), F32)
        s = t - 1 - lag
        return jnp.where(tmask, taps[s * LANES:(s + 1) * LANES], 0.0)

    wts = [jnp.concatenate([jnp.concatenate([tap(2 * d), tap(2 * d + 1)], axis=1),
                            jnp.concatenate([tap(2 * d - 1), tap(2 * d)], axis=1)], axis=0).astype(BF16)
           for d in range(ntile)]
    cpt = jnp.concatenate([cre_ref[0]] * OCT + [cim_ref[0]] * OCT, axis=1)
    cpt = jnp.where(_group_mask((kk, sw), S5_GROUP, S5_STATE), cpt, 0.0).astype(BF16)
    xs = xs_ref[0]
    for n in range(ntile):
        acc = _dot_nt(xs, cpt[n * MXU_TILE:(n + 1) * MXU_TILE])
        for k in range(n + 1):
            acc = acc + _dot(u_ref[0, :, k * MXU_TILE:(k + 1) * MXU_TILE], wts[n - k])
        for e in range(MXU_TILE // LANES):
            yscr_ref[pl.ds(2 * n + e, r, stride=t), :] = acc[:, e * LANES:(e + 1) * LANES]
    y_ref[0] = yscr_ref[...].astype(y_ref.dtype)


def _s5_out(u2, xs, bre, bim, cre, cim, zre, zim, out_dtype):
    no, r, kk = u2.shape
    m = r * (kk // LANES)
    blk = lambda a: pl.BlockSpec((1,) + a.shape[1:], lambda o: (o,) + (0,) * (a.ndim - 1))
    ops = (u2, xs, bre, bim, cre, cim, zre, zim)
    return pl.pallas_call(
        _s5_out_kernel,
        grid=(no,),
        in_specs=[blk(a) for a in ops],
        out_specs=pl.BlockSpec((1, m, LANES), lambda o: (o, 0, 0)),
        out_shape=jax.ShapeDtypeStruct((no, m, LANES), out_dtype),
        scratch_shapes=[pltpu.VMEM((m, LANES), F32)],
        compiler_params=_cparams(("parallel",)),
        name="s5_out",
    )(*ops)


def _s5_params(lam_re, lam_im, log_dt, b_re, b_im, c_re, c_im):
    t = S5_CHUNK
    ng = lam_re.shape[0]
    no = ng // OCT
    dt = jnp.exp(log_dt)[:, None]
    a = lam_re * dt
    b = lam_im * dt
    n = jnp.arange(t + 1, dtype=F32)[None, :, None]
    mag = jnp.exp(n * a[:, None, :])
    pw_re = mag * jnp.cos(n * b[:, None, :])
    pw_im = mag * jnp.sin(n * b[:, None, :])
    em1_re = jnp.expm1(a) * jnp.cos(b) - 2.0 * jnp.sin(0.5 * b) ** 2
    em1_im = jnp.exp(a) * jnp.sin(b)
    den = lam_re * lam_re + lam_im * lam_im
    z_re = (em1_re * lam_re + em1_im * lam_im) / den
    z_im = (em1_im * lam_re - em1_re * lam_im) / den
    bb_re = z_re[..., None] * b_re - z_im[..., None] * b_im
    bb_im = z_re[..., None] * b_im + z_im[..., None] * b_re
    def rows(w):
        return w.reshape(no, OCT, t, S5_GROUP, S5_STATE).transpose(0, 2, 1, 3, 4).reshape(no, t * LANES, S5_STATE)

    bt_re = bb_re.transpose(0, 2, 1)[:, None]
    bt_im = bb_im.transpose(0, 2, 1)[:, None]
    r_re, r_im = pw_re[:, t - 1::-1, None, :], pw_im[:, t - 1::-1, None, :]
    bre = rows(r_re * bt_re - r_im * bt_im)
    bim = rows(r_re * bt_im + r_im * bt_re)
    o_re, o_im = pw_re[:, 1:, None, :], pw_im[:, 1:, None, :]
    cre = rows(c_re[:, None] * o_re - c_im[:, None] * o_im)
    cim = rows(-(c_re[:, None] * o_im + c_im[:, None] * o_re))
    return bre, bim, cre, cim, c_re.reshape(no, LANES, S5_STATE), c_im.reshape(no, LANES, S5_STATE), pw_re, pw_im


def _s5_chunk(seqlen):
    return math.gcd(seqlen, S5_CHUNK)


def _s5_branch(u2, x0, ops, nseq, seqlen, out_dtype):
    bre, bim, cre, cim, zre, zim, pw_re, pw_im = ops
    no = u2.shape[0]
    t = _s5_chunk(seqlen)
    kk = t * LANES
    nchunks = seqlen // t
    are = pw_re[:, t].reshape(no, 1, OCT * S5_STATE)
    aim = pw_im[:, t].reshape(no, 1, OCT * S5_STATE)
    tail = S5_CHUNK * LANES - kk
    bre, bim = bre[:, tail:], bim[:, tail:]
    xs, xf = _s5_state(u2, bre, bim, x0, are, aim, nseq, nchunks)
    return _s5_out(u2, xs, bre, bim, cre[:, :kk], cim[:, :kk], zre, zim, out_dtype), xf


def _s5_state_in(re, im, no):
    nseq = re.shape[0]
    f = lambda a: a.reshape(nseq, no, OCT * S5_STATE).transpose(1, 0, 2)
    return jnp.concatenate([f(re), f(im)], axis=-1)


def _s5_state_out(xf):
    no, nseq, _ = xf.shape
    hw = OCT * S5_STATE
    f = lambda a: a.transpose(1, 0, 2).reshape(1, nseq, no * OCT, S5_STATE)
    return f(xf[:, :, :hw]), f(xf[:, :, hw:])


def _gla_kernel(q_ref, k_ref, v_ref, la_ref, gb_ref, s0_ref, ng_ref, o_ref, sf_ref, st_ref,
                *, nseq, nchunks, c):
    i = pl.program_id(1)

    @pl.when(i == 0)
    def _():
        st_ref[...] = s0_ref[...]

    nh, hk, hv = GLA_HEADS, GLA_HEAD_K, GLA_HEAD_V
    nch = nseq * nchunks
    tm = nch * c
    iota = lambda shape, d: lax.broadcasted_iota(jnp.int32, shape, d)
    ng = ng_ref[...]

    tb = min(tm, MXU_TILE)
    rt, ct = iota((tb, tb), 0), iota((tb, tb), 1)
    tril = ((rt // c == ct // c) & (rt >= ct)).astype(BF16)
    parts = []
    for r0 in range(0, tm, tb):
        la_hi, la_lo = _split_bf16(la_ref[r0:r0 + tb, :])
        parts.append(_dot(tril, la_hi) + _dot(tril, la_lo))
    bcum = jnp.concatenate(parts, axis=0)
    e_hi, e_lo = _split_bf16(jnp.exp(jnp.concatenate([bcum[(ci + 1) * c - 1:(ci + 1) * c] for ci in range(nch)],
                                                     axis=0)))
    pick = (iota((nch, LANES), 0) == iota((nch, LANES), 1)).astype(BF16)
    dec_t = _dot_tn(e_hi, pick) + _dot_tn(e_lo, pick)
    q_all = q_ref[...].astype(F32)
    k_all = k_ref[...].astype(F32)
    q_dec_all = q_all * jnp.exp(bcum)
    k_dec_all = k_all * jnp.exp(-bcum)

    own_k = iota((nh * c, nh * hk), 0) // c == iota((nh * c, nh * hk), 1) // hk
    own_v = iota((nh * c, nh * hv), 0) // c == iota((nh * c, nh * hv), 1) // hv
    causal = iota((c, nh * c), 1) % c <= iota((c, nh * c), 0)
    zero_v = jnp.zeros((hk, hv), F32)

    def intra(ci):
        rows = slice(ci * c, (ci + 1) * c)
        q_dec = q_dec_all[rows].astype(BF16)
        k_dec = k_dec_all[rows]
        b_c = bcum[rows]
        k_tail = (k_all[rows] * jnp.exp(b_c[c - 1:c] - b_c)).astype(BF16)
        v = v_ref[rows, :].astype(F32)
        k_bd = jnp.where(own_k, jnp.concatenate([k_dec] * nh, axis=0), 0.0).astype(BF16)
        v_bd = jnp.where(own_v, jnp.concatenate([v] * nh, axis=0), 0.0).astype(BF16)
        att = jnp.where(causal, _dot_nt(q_dec, k_bd), 0.0)
        o_intra = _dot(att.astype(BF16), v_bd)
        vb = v.astype(BF16)
        kvs = []
        for h0 in range(0, nh, 2):
            kv2 = _dot_tn(k_tail[:, h0 * hk:(h0 + 2) * hk], vb[:, h0 * hv:(h0 + 2) * hv])
            kvs += [kv2[:hk, :hv], kv2[hk:, hv:]]
        return q_dec, o_intra, kvs

    def carry(ci, sts, q_dec, o_intra, kvs):
        st_bd = jnp.concatenate(
            [jnp.concatenate([zero_v] * h + [sts[h]] + [zero_v] * (nh - 1 - h), axis=1) for h in range(nh)],
            axis=0).astype(BF16)
        o = o_intra + _dot(q_dec, st_bd)
        new = [dec_t[h * hk:(h + 1) * hk, ci:ci + 1] * sts[h] + kvs[h] for h in range(nh)]
        return o, new

    def finish(ci, o):
        rows = slice(ci * c, (ci + 1) * c)
        for h in range(nh):
            vs = slice(h * hv, (h + 1) * hv)
            oh = _rms_rows(o[:, vs], ng) * gb_ref[rows, vs].astype(F32)
            o_ref[rows, vs] = oh.astype(o_ref.dtype)

    sts = None
    pending = intra(0)
    unfinished = None
    for ci in range(nch):
        s, first, last = ci // nchunks, ci % nchunks == 0, ci % nchunks == nchunks - 1
        current = pending
        if ci + 1 < nch:
            pending = intra(ci + 1)
        if first:
            sts = [st_ref[s, h] for h in range(nh)]
        o, sts = carry(ci, sts, *current)
        if last:
            for h in range(nh):
                st_ref[s, h] = sts[h]
        if unfinished is not None:
            finish(*unfinished)
        unfinished = (ci, o)
    finish(*unfinished)

    @pl.when(i == pl.num_programs(1) - 1)
    def _():
        sf_ref[...] = st_ref[...]


def _gla(q, k, v, la, gb, s0, ng, nseq_total, seqlen, out_dtype):
    c = math.gcd(seqlen, GLA_CHUNK)
    per_seq = seqlen // c
    if per_seq >= 8:
        nseq, nchunks = 1, next(n for n in (32, 16, 8, 1) if per_seq % n == 0)
    else:
        nseq, nchunks = 16, seqlen // c
    assert nseq_total % nseq == 0 and seqlen % (nchunks * c) == 0
    nblk = seqlen // (nchunks * c)
    tm = nseq * nchunks * c
    row = lambda n: pl.BlockSpec((tm, n), lambda b, i: (b * nblk + i, 0))
    st_spec = pl.BlockSpec((nseq, GLA_HEADS, GLA_HEAD_K, GLA_HEAD_V), lambda b, i: (b, 0, 0, 0))
    m = q.shape[0]
    return pl.pallas_call(
        functools.partial(_gla_kernel, nseq=nseq, nchunks=nchunks, c=c),
        grid=(nseq_total // nseq, nblk),
        in_specs=[row(GLA_K_WIDTH), row(GLA_K_WIDTH), row(GLA_V_WIDTH), row(GLA_K_WIDTH), row(GLA_V_WIDTH),
                  st_spec, _full(ng.shape)],
        out_specs=[row(GLA_V_WIDTH), st_spec],
        out_shape=[jax.ShapeDtypeStruct((m, GLA_V_WIDTH), out_dtype), jax.ShapeDtypeStruct(s0.shape, F32)],
        scratch_shapes=[pltpu.VMEM((nseq, GLA_HEADS, GLA_HEAD_K, GLA_HEAD_V), F32)],
        compiler_params=_cparams(("parallel", "arbitrary")),
        name="gla",
    )(q, k, v, la, gb, s0, ng)


def _gelu_tanh(x):
    return 0.5 * x * (1.0 + jnp.tanh(math.sqrt(2.0 / math.pi) * (x + 0.044715 * (x * x * x))))


def _head_ones(n):
    r = lax.broadcasted_iota(jnp.int32, (n, n), 0) // SWA_HEAD_DIM
    c = lax.broadcasted_iota(jnp.int32, (n, n), 1) // SWA_HEAD_DIM
    return (r == c).astype(BF16)


def _rope_block(x, cos_t, sin_t, upper):
    half = SWA_HEAD_DIM // 2
    swapped = jnp.where(upper, pltpu.roll(x, half, 1), pltpu.roll(x, LANES - half, 1))
    return x * cos_t + swapped * sin_t


def _mid_kernel(x_ref, y_ref, u_ref, ga_ref, ob_ref, d_ref, wglu_ref, bglu_ref, wo_ref,
                g_ref, w_ref, gq_ref, gk_ref, cos_ref, sin_ref, h_ref, q_ref, k_ref, v_ref, gate_ref):
    tm = x_ref.shape[0]
    lane = lax.broadcasted_iota(jnp.int32, (tm, LANES), 1)
    upper = (lane & (SWA_HEAD_DIM // 2)) != 0
    ones4 = _head_ones(MXU_TILE)
    inv_d = 1.0 / SWA_HEAD_DIM
    qw = SWA_HEADS * SWA_HEAD_DIM
    kw = SWA_KV_HEADS * SWA_HEAD_DIM
    nblk = qw // MXU_TILE
    no = y_ref.shape[0]

    na = ga_ref.shape[1]
    mix_b = _dot(ob_ref[...].astype(BF16), wo_ref[na:, :])
    y = jnp.concatenate([y_ref[o].astype(F32) for o in range(no)], axis=1)
    u = jnp.concatenate([u_ref[o].astype(F32) for o in range(no)], axis=1)
    z = _gelu_tanh(y + d_ref[...] * u)
    z = z * _sigmoid(_dot(z.astype(BF16), wglu_ref[...]) + bglu_ref[...])
    out_a = z * ga_ref[...].astype(F32)
    h = x_ref[...] + (_dot(out_a.astype(BF16), wo_ref[:na, :]) + mix_b)
    h_ref[...] = h

    xb = _rms_rows(h, g_ref[...]).astype(BF16)
    cos_t = cos_ref[...]
    sin_t = sin_ref[...]

    def gate_finish(gate):
        gate_ref[...] = _silu(gate).astype(gate_ref.dtype)

    def q_finish(j, q):
        ss = _dot((q * q).astype(BF16), ones4)
        qn = q * lax.rsqrt(ss * inv_d + NORM_EPS) * gq_ref[...]
        for e in range(MXU_TILE // LANES):
            cols = slice(j * MXU_TILE + e * LANES, j * MXU_TILE + (e + 1) * LANES)
            qe = _rope_block(qn[:, e * LANES:(e + 1) * LANES], cos_t, sin_t, upper)
            q_ref[:, cols] = (qe * (SWA_HEAD_DIM ** -0.5 * LOG2E)).astype(q_ref.dtype)

    def kv_finish(kv):
        k = kv[:, :kw]
        ss = _dot((k * k).astype(BF16), ones4[:kw, :kw])
        kn = k * lax.rsqrt(ss * inv_d + NORM_EPS) * gk_ref[...]
        k_ref[...] = _rope_block(kn, cos_t, sin_t, upper)
        v_ref[...] = kv[:, kw:]

    work = [(slice(j * MXU_TILE, (j + 1) * MXU_TILE), functools.partial(q_finish, j)) for j in range(nblk)]
    work += [(slice(qw, qw + 2 * kw), kv_finish)]
    work += [(slice(qw + 2 * kw, None), gate_finish)]
    pending = _dot(xb, w_ref[:, work[0][0]])
    for n, (_, finish) in enumerate(work):
        current = pending
        if n + 1 < len(work):
            pending = _dot(xb, w_ref[:, work[n + 1][0]])
        finish(current)


def _mid(x, y, u, ga, ob, pe, po, cos_t, sin_t, act_dtype):
    m, dm = x.shape
    tm = _row_tile(m)
    row = lambda n: pl.BlockSpec((tm, n), lambda i: (i, 0))
    slab = pl.BlockSpec((y.shape[0], tm, LANES), lambda i: (0, i, 0))
    assert cos_t.shape[0] % tm == 0
    nper = cos_t.shape[0] // tm
    tab = pl.BlockSpec((tm, LANES), lambda i: (i % nper, 0))
    qw = SWA_HEADS * SWA_HEAD_DIM
    kw = SWA_KV_HEADS * SWA_HEAD_DIM
    weights = [pe['d'], pe['wglu'], pe['bglu'], pe['wo'], po['norm_g'], po['w_in'], po['gq'], po['gk']]
    return pl.pallas_call(
        _mid_kernel,
        grid=(m // tm,),
        in_specs=[row(dm), slab, slab, row(S5_WIDTH), row(GLA_V_WIDTH)] + [_full(w.shape) for w in weights]
        + [tab, tab],
        out_specs=[row(dm), row(qw), row(kw), row(kw), row(qw)],
        out_shape=[jax.ShapeDtypeStruct((m, dm), F32),
                   jax.ShapeDtypeStruct((m, qw), act_dtype), jax.ShapeDtypeStruct((m, kw), F32),
                   jax.ShapeDtypeStruct((m, kw), F32), jax.ShapeDtypeStruct((m, qw), act_dtype)],
        compiler_params=_cparams(("parallel",)),
        name="mid",
    )(x, y, u, ga, ob, *weights, cos_t, sin_t)


def _rope_tables(pos):
    half = SWA_HEAD_DIM // 2
    inv_freq = ROPE_THETA ** (-np.arange(half, dtype=np.float64) / half)
    ang = np.asarray(pos, np.float64)[:, None] * inv_freq[None, :]
    cos, sin = np.cos(ang), np.sin(ang)
    cos_t = np.concatenate([cos, cos, cos, cos], axis=1).astype(np.float32)
    sin_t = np.concatenate([-sin, sin, -sin, sin], axis=1).astype(np.float32)
    return jnp.asarray(cos_t), jnp.asarray(sin_t)


def _attn_prompt_kernel(sink_ref, q_ref, kc_ref, kp_ref, vc_ref, vp_ref, gate_ref, h_ref, wout_ref,
                        y_ref, p_ref, o_ref, *, nqb):
    i = pl.program_id(1)
    w = SWA_WINDOW
    hd = SWA_HEAD_DIM
    npair = SWA_GROUP // 2
    lane2 = lax.broadcasted_iota(jnp.int32, (2 * w, LANES), 1)
    low = lane2 < hd
    rr = lax.broadcasted_iota(jnp.int32, (w, w), 0)
    cc = lax.broadcasted_iota(jnp.int32, (w, w), 1)
    tri = cc <= rr
    low_w = cc < hd
    r4 = lax.broadcasted_iota(jnp.int32, (4 * w, LANES), 0)
    c4 = lax.broadcasted_iota(jnp.int32, (4 * w, LANES), 1)
    den_cols = ((r4 < 2 * w) == (c4 < hd)).astype(BF16)
    units = [(jb, kv) for jb in range(nqb) for kv in range(SWA_KV_HEADS)]

    def scores(jb, kv):
        rows = slice(jb * w, (jb + 1) * w)
        if jb == 0:
            k_prev, v_prev = kp_ref[...], vp_ref[...]
        else:
            prev_rows = slice((jb - 1) * w, jb * w)
            k_prev, v_prev = kc_ref[prev_rows, :], vc_ref[prev_rows, :]
        kcat = jnp.concatenate([k_prev, kc_ref[rows, :]], axis=0)
        vcat = jnp.concatenate([v_prev, vc_ref[rows, :]], axis=0)
        own = low if kv == 0 else jnp.logical_not(low)
        k_own = jnp.where(own, kcat, 0.0)
        v_own = jnp.where(own, vcat, 0.0)
        k_oth = pltpu.roll(k_own, hd, 1)
        v_oth = pltpu.roll(v_own, hd, 1)
        k_lo, k_hi = (k_own, k_oth) if kv == 0 else (k_oth, k_own)
        v_lo, v_hi = (v_own, v_oth) if kv == 0 else (v_oth, v_own)
        k_rhs = jnp.concatenate([k_lo, k_hi], axis=0).astype(BF16)
        v_rhs = jnp.concatenate([jnp.concatenate([v_lo, v_hi], axis=0).astype(BF16), den_cols], axis=1)
        qs = jnp.concatenate([q_ref[rows, (kv * npair + pr) * LANES:(kv * npair + pr + 1) * LANES]
                              for pr in range(npair)], axis=0)
        return _dot_nt(qs, k_rhs), v_rhs

    def softmax_pv(kv, first, buf, s_all, v_rhs):
        sink_terms = []
        for pr in range(npair):
            prow = slice(pr * w, (pr + 1) * w)
            pair_terms = []
            for e in range(2):
                s_prev = s_all[prow, (2 * e) * w:(2 * e + 1) * w]
                s_cur = s_all[prow, (2 * e + 1) * w:(2 * e + 2) * w]
                if first:
                    s_prev = jnp.where(i > 0, s_prev, -jnp.inf)
                sc = jnp.where(tri, s_cur, s_prev)
                sink = sink_ref[2 * (kv * npair + pr) + e] * LOG2E
                mx = jnp.maximum(jnp.max(sc, axis=-1, keepdims=True), sink)
                pe = jnp.exp2(sc - mx)
                p_ref[buf, prow, (2 * e) * w:(2 * e + 1) * w] = jnp.where(tri, 0.0, pe).astype(BF16)
                p_ref[buf, prow, (2 * e + 1) * w:(2 * e + 2) * w] = jnp.where(tri, pe, 0.0).astype(BF16)
                pair_terms.append(jnp.exp2(sink - mx))
            sink_terms.append(pair_terms)
        return _dot(p_ref[buf], v_rhs), sink_terms

    def normalise(jb, kv, o_ext, sink_terms):
        rows = slice(jb * w, (jb + 1) * w)
        for pr in range(npair):
            prow = slice(pr * w, (pr + 1) * w)
            cols = slice((kv * npair + pr) * LANES, (kv * npair + pr + 1) * LANES)
            st = jnp.where(low_w, sink_terms[pr][0], sink_terms[pr][1])
            o = o_ext[prow, :LANES] / (o_ext[prow, LANES:] + st)
            o_ref[rows, cols] = o.astype(o_ref.dtype)

    def project(rows):
        og = o_ref[rows, :].astype(F32) * gate_ref[rows, :].astype(F32)
        y_ref[rows, :] = h_ref[rows, :] + _dot(og.astype(BF16), wout_ref[...])

    group = 2
    pending = scores(*units[0])
    unfinished = None
    for n, (jb, kv) in enumerate(units):
        current = pending
        if n + 1 < len(units):
            pending = scores(*units[n + 1])
        result = softmax_pv(kv, jb == 0, n % 2, *current)
        if unfinished is not None:
            normalise(*unfinished)
            done_jb, done_kv = unfinished[:2]
            if done_kv == SWA_KV_HEADS - 1 and (done_jb + 1) % group == 0:
                project(slice((done_jb + 1 - group) * w, (done_jb + 1) * w))
        unfinished = (jb, kv) + result
    normalise(*unfinished)
    project(slice((nqb - group) * w, nqb * w))


def _attn_prompt(sinks, q, k, v, gate, h, w_out, nseq, seqlen):
    w = SWA_WINDOW
    nqb = 8
    tm = nqb * w
    assert seqlen % tm == 0
    nblk = seqlen // tm
    qw = q.shape[1]
    dm = h.shape[1]
    row = lambda n: pl.BlockSpec((tm, n), lambda b, i: (b * nblk + i, 0))
    prev = lambda n: pl.BlockSpec((w, n), lambda b, i: (jnp.maximum((b * nblk + i) * nqb - 1, 0), 0))
    smem = pl.BlockSpec(memory_space=pltpu.SMEM)
    return pl.pallas_call(
        functools.partial(_attn_prompt_kernel, nqb=nqb),
        grid=(nseq, nblk),
        in_specs=[smem, row(qw), row(LANES), prev(LANES), row(LANES), prev(LANES), row(qw), row(dm),
                  _full(w_out.shape)],
        out_specs=row(dm),
        out_shape=jax.ShapeDtypeStruct(h.shape, F32),
        scratch_shapes=[pltpu.VMEM((2, 4 * w, 4 * w), BF16), pltpu.VMEM((tm, qw), BF16)],
        compiler_params=_cparams(("parallel", "arbitrary")),
        name="attn_prompt",
    )(sinks, q, k, k, v, v, gate, h, w_out)


def _attn_sample_kernel(sink_ref, q_ref, kn_ref, vn_ref, kc_ref, vc_ref, o_ref, ko_ref, vo_ref, q2_ref, *, seqlen):
    nq = SWA_HEADS * seqlen
    hd = SWA_HEAD_DIM
    heads = [(kv, g) for kv in range(SWA_KV_HEADS) for g in range(SWA_GROUP)]

    def stacked(kv, g):
        r0 = (kv * SWA_GROUP + g) * seqlen
        return slice(r0, r0 + seqlen), slice(kv * hd, (kv + 1) * hd)

    q2_ref[...] = jnp.zeros(q2_ref.shape, q2_ref.dtype)
    for kv, g in heads:
        rows, lanes = stacked(kv, g)
        h = kv * SWA_GROUP + g
        q2_ref[:, rows, lanes] = q_ref[:, :, h * hd:(h + 1) * hd]
    ncache = kc_ref.shape[1]
    t_row = lax.broadcasted_iota(jnp.int32, (nq, ncache), 0) % seqlen
    c_col = lax.broadcasted_iota(jnp.int32, (nq, ncache), 1)
    cache_ok = c_col > t_row - (SWA_WINDOW - ncache)
    nnew = kn_ref.shape[1]
    t_row_n = lax.broadcasted_iota(jnp.int32, (nq, nnew), 0) % seqlen
    n_col = lax.broadcasted_iota(jnp.int32, (nq, nnew), 1)
    new_ok = n_col <= t_row_n
    sink = (sink_ref[...] * LOG2E)[None, :, 0:1]
    bqk = lambda a, b: lax.dot_general(a, b, (((2,), (2,)), ((0,), (0,))), preferred_element_type=F32)
    bpv = lambda a, b: lax.dot_general(a, b, (((2,), (1,)), ((0,), (0,))), preferred_element_type=F32)
    q = q2_ref[...].astype(BF16)
    kc = kc_ref[...]
    vc = vc_ref[...]
    kn = kn_ref[...]
    vn = vn_ref[...]
    sc = jnp.where(cache_ok[None], bqk(q, kc.astype(BF16)), -jnp.inf)
    sn = jnp.where(new_ok[None], bqk(q, kn.astype(BF16)), -jnp.inf)
    mx = jnp.maximum(jnp.maximum(jnp.max(sc, axis=-1, keepdims=True), jnp.max(sn, axis=-1, keepdims=True)), sink)
    pc = jnp.exp2(sc - mx)
    pn = jnp.exp2(sn - mx)
    den = jnp.sum(pc, axis=-1, keepdims=True) + jnp.sum(pn, axis=-1, keepdims=True) + jnp.exp2(sink - mx)
    inv = 1.0 / den
    o2 = bpv((pc * inv).astype(BF16), vc.astype(BF16)) + bpv((pn * inv).astype(BF16), vn.astype(BF16))
    for kv, g in heads:
        rows, lanes = stacked(kv, g)
        h = kv * SWA_GROUP + g
        o_ref[:, :, h * hd:(h + 1) * hd] = o2[:, rows, lanes]
    keep = ncache - seqlen
    ko_ref[:, 0:keep, :] = kc[:, seqlen:ncache, :]
    ko_ref[:, keep:ncache, :] = kn[:, 0:seqlen, :]
    vo_ref[:, 0:keep, :] = vc[:, seqlen:ncache, :]
    vo_ref[:, keep:ncache, :] = vn[:, 0:seqlen, :]


def _attn_sample(sink_rows, q, kn, vn, kc, vc):
    n, seqlen, _ = q.shape
    nseq = 16
    assert n % nseq == 0
    blk = lambda a: pl.BlockSpec((nseq,) + a.shape[1:], lambda i: (i, 0, 0))
    return pl.pallas_call(
        functools.partial(_attn_sample_kernel, seqlen=seqlen),
        grid=(n // nseq,),
        in_specs=[_full(sink_rows.shape), blk(q), blk(kn), blk(vn), blk(kc), blk(vc)],
        out_specs=[blk(q), blk(kc), blk(vc)],
        out_shape=[jax.ShapeDtypeStruct(q.shape, F32), jax.ShapeDtypeStruct(kc.shape, F32),
                   jax.ShapeDtypeStruct(vc.shape, F32)],
        scratch_shapes=[pltpu.VMEM((nseq, SWA_HEADS * seqlen, SWA_KV_HEADS * SWA_HEAD_DIM), F32)],
        compiler_params=_cparams(("parallel",)),
        name="attn_sample",
    )(sink_rows, q, kn, vn, kc, vc)


def _odd_out_kernel(h_ref, o_ref, gate_ref, w_ref, y_ref):
    og = o_ref[...].astype(F32) * gate_ref[...].astype(F32)
    y_ref[...] = h_ref[...] + _dot(og.astype(BF16), w_ref[...])


def _odd_out(h, o, gate, w):
    m, dm = h.shape
    tm = _row_tile(m)
    row = lambda n: pl.BlockSpec((tm, n), lambda i: (i, 0))
    return pl.pallas_call(
        _odd_out_kernel,
        grid=(m // tm,),
        in_specs=[row(dm), row(o.shape[1]), row(gate.shape[1]), _full(w.shape)],
        out_specs=row(dm),
        out_shape=jax.ShapeDtypeStruct((m, dm), F32),
        compiler_params=_cparams(("parallel",)),
        name="odd_out",
    )(h, o, gate, w)


def _trunk(x, s5_x0, gla_s0, pe, po, pos, nseq, seqlen, act_dtype):
    u4, u2, ga, q, k, v, la, gb = _even_in(x, pe['norm_g'], pe['w_in'], pe['wgate'], pe['bgate'], act_dtype,
                                           _s5_chunk(seqlen))
    y4, s5_fin = _s5_branch(u2, s5_x0, pe['s5_ops'], nseq, seqlen, act_dtype)
    ob, gla_fin = _gla(q, k, v, la, gb, gla_s0, pe['gla_norm_g'], nseq, seqlen, act_dtype)
    cos_t, sin_t = _rope_tables(pos)
    h, q1, k1, v1, gate = _mid(x, y4, u4, ga, ob, pe, po, cos_t, sin_t, act_dtype)
    return h, q1, k1, v1, gate, s5_fin, gla_fin


def kernel(x_prompt, x_sample, state_s5_re, state_s5_im, state_gla, cache_swa_k, cache_swa_v,
           even_norm_g, even_w_in, s5_lambda_re, s5_lambda_im, s5_log_dt, s5_b_re, s5_b_im,
           s5_c_re, s5_c_im, s5_d, s5_w_glu, s5_b_glu, gla_w_gate, gla_b_gate, gla_norm_g,
           even_w_out, odd_norm_g, odd_w_in, swa_q_norm_g, swa_k_norm_g, swa_sinks, odd_w_out):
    nb, seq, dm = x_prompt.shape
    ns, dseq, _ = x_sample.shape
    ng = s5_lambda_re.shape[1]
    no = ng // OCT
    xp = x_prompt.reshape(nb * seq, dm)
    xs = x_sample.reshape(ns * dseq, dm)

    i = 0
    w_in = even_w_in[i]
    assert ng * S5_GROUP == S5_WIDTH
    col_alow = C_CODE
    col_gb = col_alow + GLA_GATE_RANK
    pad_rank = LANES - GLA_GATE_RANK
    pe = {
        'norm_g': even_norm_g[i][None, :],
        'w_in': jnp.concatenate([w_in[:, :col_alow], jnp.pad(w_in[:, col_alow:col_gb], ((0, 0), (0, pad_rank))),
                                 w_in[:, col_gb:]], axis=1).astype(BF16),
        'wgate': jnp.pad(gla_w_gate[i], ((0, pad_rank), (0, 0))).astype(BF16),
        'bgate': gla_b_gate[i][None, :],
        's5_ops': _s5_params(s5_lambda_re[i], s5_lambda_im[i], s5_log_dt[i], s5_b_re[i], s5_b_im[i],
                             s5_c_re[i], s5_c_im[i]),
        'gla_norm_g': gla_norm_g[i][None, :],
        'd': s5_d[i][None, :],
        'wglu': s5_w_glu[i].astype(BF16),
        'bglu': s5_b_glu[i][None, :],
        'wo': even_w_out[i].astype(BF16),
    }
    po = {
        'norm_g': odd_norm_g[i][None, :],
        'w_in': odd_w_in[i].astype(BF16),
        'gq': jnp.tile(swa_q_norm_g[i], MXU_TILE // SWA_HEAD_DIM)[None, :],
        'gk': jnp.tile(swa_k_norm_g[i], LANES // SWA_HEAD_DIM)[None, :],
    }
    w_out = odd_w_out[i].astype(BF16)
    sinks = swa_sinks[i]
    kvw = SWA_KV_HEADS * SWA_HEAD_DIM

    s5_zero = jnp.zeros((no, nb, 2 * OCT * S5_STATE), F32)
    gla_zero = jnp.zeros((nb, GLA_HEADS, GLA_HEAD_K, GLA_HEAD_V), F32)
    hp, q, k, v, gate, s5_p, gla_p = _trunk(xp, s5_zero, gla_zero, pe, po, np.arange(seq), nb, seq, BF16)
    s5r_p, s5i_p = _s5_state_out(s5_p)
    y_prompt = _attn_prompt(sinks, q, k, v, gate, hp, w_out, nb, seq).reshape(nb, seq, dm)
    cache_len = min(SWA_WINDOW, seq)
    tail = lambda a: (a.reshape(nb, seq, kvw)[:, seq - cache_len:]
                      .reshape(1, nb, cache_len, SWA_KV_HEADS, SWA_HEAD_DIM))
    swk_p, swv_p = tail(k), tail(v)

    pos_s = np.tile(PAST_LEN + np.arange(dseq), ns)
    s5_init = _s5_state_in(state_s5_re[i], state_s5_im[i], no)
    hs, q, k, v, gate, s5_s, gla_s = _trunk(xs, s5_init, state_gla[i], pe, po, pos_s, ns, dseq, F32)
    s5r_s, s5i_s = _s5_state_out(s5_s)
    ncache = cache_swa_k.shape[2]
    sink_rows = jnp.broadcast_to(jnp.repeat(sinks, dseq)[:, None], (SWA_HEADS * dseq, LANES))
    kn = k.reshape(ns, dseq, kvw)
    vn = v.reshape(ns, dseq, kvw)
    npad = BF16_ROWS - dseq
    kn_pad = jnp.pad(kn, ((0, 0), (0, npad), (0, 0)))
    vn_pad = jnp.pad(vn, ((0, 0), (0, npad), (0, 0)))
    kc = cache_swa_k[i].reshape(ns, ncache, kvw)
    vc = cache_swa_v[i].reshape(ns, ncache, kvw)
    o, kc_new, vc_new = _attn_sample(sink_rows, q.reshape(ns, dseq, -1), kn_pad, vn_pad, kc, vc)
    y_sample = _odd_out(hs, o.reshape(ns * dseq, -1), gate, w_out).reshape(ns, dseq, dm)
    swk_s = kc_new.reshape(1, ns, ncache, SWA_KV_HEADS, SWA_HEAD_DIM)
    swv_s = vc_new.reshape(1, ns, ncache, SWA_KV_HEADS, SWA_HEAD_DIM)

    return (y_prompt, y_sample,
            s5r_p, s5i_p, gla_p[None], swk_p, swv_p,
            s5r_s, s5i_s, gla_s[None], swk_s, swv_s)
```

```python
import functools
import math

import jax
import jax.numpy as jnp
import numpy as np
from jax import lax
from jax.experimental import pallas as pl
from jax.experimental.pallas import tpu as pltpu

F32 = jnp.float32
BF16 = jnp.bfloat16

PAST_LEN = 8192
NORM_EPS = 1e-6
S5_GROUP = 16
S5_STATE = 64
S5_CHUNK = 16
GLA_HEADS = 4
GLA_HEAD_K = 64
GLA_HEAD_V = 128
GLA_GATE_RANK = 16
GLA_GATE_TAU = 16.0
GLA_CHUNK = 64
S5_WIDTH = 512
GLA_K_WIDTH = GLA_HEADS * GLA_HEAD_K
GLA_V_WIDTH = GLA_HEADS * GLA_HEAD_V
SWA_HEADS = 16
SWA_KV_HEADS = 2
SWA_GROUP = SWA_HEADS // SWA_KV_HEADS
SWA_HEAD_DIM = 64
SWA_WINDOW = 128
ROPE_THETA = 10000.0
LOG2E = math.log2(math.e)
LANES = 128
BF16_ROWS = 16
MXU_TILE = 256
OCT = LANES // S5_GROUP
ROW_TILE = 512
VMEM_LIMIT = 48 * 1024 * 1024
C_U = 0
C_GATE_A = C_U + S5_WIDTH
C_Q = C_GATE_A + S5_WIDTH
C_K = C_Q + GLA_K_WIDTH
C_V = C_K + GLA_K_WIDTH
C_CODE = C_V + GLA_V_WIDTH
C_GATE_B = C_CODE + LANES
C_END = C_GATE_B + GLA_V_WIDTH


def _cparams(sem):
    return pltpu.CompilerParams(dimension_semantics=sem, vmem_limit_bytes=VMEM_LIMIT)


def _full(shape):
    n = len(shape)
    return pl.BlockSpec(shape, lambda *_: (0,) * n)


def _dot(a, b):
    return jnp.dot(a, b, preferred_element_type=F32)


def _dot_nt(a, b):
    return lax.dot_general(a, b, (((1,), (1,)), ((), ())), preferred_element_type=F32)


def _dot_tn(a, b):
    return lax.dot_general(a, b, (((0,), (0,)), ((), ())), preferred_element_type=F32)


def _split_bf16(x):
    hi = x.astype(BF16)
    lo = (x - hi.astype(F32)).astype(BF16)
    return hi, lo


def _rms_rows(x, g):
    return x * lax.rsqrt(jnp.mean(x * x, axis=-1, keepdims=True) + NORM_EPS) * g


def _sigmoid(x):
    return 1.0 / (1.0 + jnp.exp(-x))


def _silu(x):
    return x * _sigmoid(x)


def _row_tile(m, tile=ROW_TILE):
    return tile if m % tile == 0 else m


def _even_in_kernel(x_ref, g_ref, w_ref, wgate_ref, bgate_ref,
                    u2_ref, ga_ref, q_ref, k_ref, v_ref, la_ref, gb_ref, uscr_ref, *, t):
    xb = _rms_rows(x_ref[...], g_ref[...]).astype(BF16)

    def proj(lo, hi):
        return _dot(xb, w_ref[:, lo:hi])

    u = proj(C_U, C_GATE_A)
    nrow = u.shape[0] // t
    for o in range(u2_ref.shape[0]):
        uscr_ref[o] = u[:, o * LANES:(o + 1) * LANES]
        for tt in range(t):
            piece = uscr_ref[o, pl.ds(tt, nrow, stride=t), :]
            u2_ref[o, :, tt * LANES:(tt + 1) * LANES] = piece.astype(u2_ref.dtype)
    ga_ref[...] = _silu(proj(C_GATE_A, C_Q)).astype(ga_ref.dtype)
    q_ref[...] = (proj(C_Q, C_K) * (GLA_HEAD_K ** -0.5)).astype(q_ref.dtype)
    k_ref[...] = proj(C_K, C_V).astype(k_ref.dtype)
    v_ref[...] = proj(C_V, C_CODE).astype(v_ref.dtype)
    gb_ref[...] = _silu(proj(C_GATE_B, C_END)).astype(gb_ref.dtype)
    a_low = proj(C_CODE, C_GATE_B)
    logit = _dot(a_low.astype(BF16), wgate_ref[...]) + bgate_ref[...]
    log_sig = jnp.minimum(logit, 0.0) - jnp.log1p(jnp.exp(-jnp.abs(logit)))
    la_ref[...] = log_sig * (1.0 / GLA_GATE_TAU)


def _even_in(x, g, w, wgate, bgate, act_dtype, t):
    m, d = x.shape
    tm = _row_tile(m, 2 * ROW_TILE) if m > ROW_TILE else m
    row = lambda n: pl.BlockSpec((tm, n), lambda i: (i, 0))
    assert w.shape[1] == C_END
    no = S5_WIDTH // LANES
    chunk = pl.BlockSpec((no, tm // t, t * LANES), lambda i: (0, i, 0))
    outs = [(S5_WIDTH, act_dtype), (GLA_K_WIDTH, act_dtype), (GLA_K_WIDTH, act_dtype), (GLA_V_WIDTH, act_dtype),
            (GLA_K_WIDTH, F32), (GLA_V_WIDTH, act_dtype)]
    return pl.pallas_call(
        functools.partial(_even_in_kernel, t=t),
        grid=(m // tm,),
        in_specs=[row(d), _full(g.shape), _full(w.shape), _full(wgate.shape), _full(bgate.shape)],
        out_specs=[chunk] + [row(n) for n, _ in outs],
        out_shape=[jax.ShapeDtypeStruct((no, m // t, t * LANES), act_dtype)]
        + [jax.ShapeDtypeStruct((m, n), dt) for n, dt in outs],
        scratch_shapes=[pltpu.VMEM((no, tm, LANES), F32)],
        compiler_params=_cparams(("parallel",)),
        name="even_in",
    )(x, g, w, wgate, bgate)


def _group_mask(shape, row_span, col_span):
    rg = (lax.broadcasted_iota(jnp.int32, shape, 0) // row_span) % OCT
    cg = (lax.broadcasted_iota(jnp.int32, shape, 1) // col_span) % OCT
    return rg == cg


def _s5_state_kernel(u_ref, bre_ref, bim_ref, x0_ref, are_ref, aim_ref, xs_ref, xf_ref, loc_ref, *, nseq, nchunks):
    hw = OCT * S5_STATE
    kk = u_ref.shape[2]
    bp = jnp.concatenate([bre_ref[0]] * OCT + [bim_ref[0]] * OCT, axis=1)
    bp = jnp.where(_group_mask((kk, 2 * hw), S5_GROUP, S5_STATE), bp, 0.0).astype(BF16)
    loc_ref[...] = _dot(u_ref[0], bp)
    a_re = are_ref[0]
    a_im = aim_ref[0]
    if nchunks == 1:
        x0 = x0_ref[0]
        xr, xi = x0[:, :hw], x0[:, hw:]
        loc = loc_ref[...]
        xf_ref[0, :, :hw] = a_re * xr - a_im * xi + loc[:, :hw]
        xf_ref[0, :, hw:] = a_re * xi + a_im * xr + loc[:, hw:]
        xs_ref[0] = x0.astype(xs_ref.dtype)
    else:
        def body(j, carry):
            new = []
            for b in range(nseq):
                xr, xi = carry[b]
                row = pl.ds(b * nchunks + j, 1)
                lr = loc_ref[row, :hw]
                li = loc_ref[row, hw:]
                loc_ref[row, :hw] = xr
                loc_ref[row, hw:] = xi
                new.append((a_re * xr - a_im * xi + lr, a_re * xi + a_im * xr + li))
            return tuple(new)

        init = tuple((x0_ref[0, b:b + 1, :hw], x0_ref[0, b:b + 1, hw:]) for b in range(nseq))
        fin = lax.fori_loop(0, nchunks, body, init, unroll=4)
        for b in range(nseq):
            xf_ref[0, b:b + 1, :hw] = fin[b][0]
            xf_ref[0, b:b + 1, hw:] = fin[b][1]
        xs_ref[0] = loc_ref[...].astype(xs_ref.dtype)


def _s5_state(u2, bre, bim, x0, are, aim, nseq, nchunks):
    no, r, kk = u2.shape
    sw = x0.shape[2]
    blk = lambda a: pl.BlockSpec((1,) + a.shape[1:], lambda o: (o,) + (0,) * (a.ndim - 1))
    return pl.pallas_call(
        functools.partial(_s5_state_kernel, nseq=nseq, nchunks=nchunks),
        grid=(no,),
        in_specs=[blk(u2), blk(bre), blk(bim), blk(x0), blk(are), blk(aim)],
        out_specs=[pl.BlockSpec((1, r, sw), lambda o: (o, 0, 0)), blk(x0)],
        out_shape=[jax.ShapeDtypeStruct((no, r, sw), BF16), jax.ShapeDtypeStruct(x0.shape, F32)],
        scratch_shapes=[pltpu.VMEM((r, sw), F32)],
        compiler_params=_cparams(("parallel",)),
        name="s5_state",
    )(u2, bre, bim, x0, are, aim)


def _dot_nt_f32(a, b):
    a_hi, a_lo = _split_bf16(a)
    b_hi, b_lo = _split_bf16(b)
    return _dot_nt(a_hi, b_hi) + _dot_nt(a_hi, b_lo) + _dot_nt(a_lo, b_hi)


def _s5_out_kernel(u_ref, xs_ref, bre_ref, bim_ref, cre_ref, cim_ref, zre_ref, zim_ref, d_ref, y_ref, yscr_ref):
    r, kk = u_ref.shape[1], u_ref.shape[2]
    t = kk // LANES
    sw = xs_ref.shape[2]
    ntile = kk // MXU_TILE
    taps = _dot_nt_f32(bre_ref[0], zre_ref[0]) - _dot_nt_f32(bim_ref[0], zim_ref[0])
    tmask = _group_mask((LANES, LANES), S5_GROUP, S5_GROUP)

    def tap(lag):
        if lag < 0:
            return jnp.zeros((LANES, LANES), F32)
        s = t - 1 - lag
        return jnp.where(tmask, taps[s * LANES:(s + 1) * LANES], 0.0)

    wts = [jnp.concatenate([jnp.concatenate([tap(2 * d), tap(2 * d + 1)], axis=1),
                            jnp.concatenate([tap(2 * d - 1), tap(2 * d)], axis=1)], axis=0).astype(BF16)
           for d in range(ntile)]
    cpt = jnp.concatenate([cre_ref[0]] * OCT + [cim_ref[0]] * OCT, axis=1)
    cpt = jnp.where(_group_mask((kk, sw), S5_GROUP, S5_STATE), cpt, 0.0).astype(BF16)
    xs = xs_ref[0]
    skip = jnp.concatenate([d_ref[0]] * (MXU_TILE // LANES), axis=1)
    for n in range(ntile):
        cols = slice(n * MXU_TILE, (n + 1) * MXU_TILE)
        acc = _dot_nt(xs, cpt[cols]) + skip * u_ref[0, :, cols].astype(F32)
        for k in range(n + 1):
            acc = acc + _dot(u_ref[0, :, k * MXU_TILE:(k + 1) * MXU_TILE], wts[n - k])
        for e in range(MXU_TILE // LANES):
            yscr_ref[pl.ds(2 * n + e, r, stride=t), :] = acc[:, e * LANES:(e + 1) * LANES]
    y_ref[0] = yscr_ref[...].astype(y_ref.dtype)


def _s5_out(u2, xs, bre, bim, cre, cim, zre, zim, d, out_dtype):
    no, r, kk = u2.shape
    m = r * (kk // LANES)
    blk = lambda a: pl.BlockSpec((1,) + a.shape[1:], lambda o: (o,) + (0,) * (a.ndim - 1))
    ops = (u2, xs, bre, bim, cre, cim, zre, zim, d)
    return pl.pallas_call(
        _s5_out_kernel,
        grid=(no,),
        in_specs=[blk(a) for a in ops],
        out_specs=pl.BlockSpec((1, m, LANES), lambda o: (o, 0, 0)),
        out_shape=jax.ShapeDtypeStruct((no, m, LANES), out_dtype),
        scratch_shapes=[pltpu.VMEM((m, LANES), F32)],
        compiler_params=_cparams(("parallel",)),
        name="s5_out",
    )(*ops)


def _s5_params(lam_re, lam_im, log_dt, b_re, b_im, c_re, c_im):
    t = S5_CHUNK
    ng = lam_re.shape[0]
    no = ng // OCT
    dt = jnp.exp(log_dt)[:, None]
    a = lam_re * dt
    b = lam_im * dt
    n = jnp.arange(t + 1, dtype=F32)[None, :, None]
    mag = jnp.exp(n * a[:, None, :])
    pw_re = mag * jnp.cos(n * b[:, None, :])
    pw_im = mag * jnp.sin(n * b[:, None, :])
    em1_re = jnp.expm1(a) * jnp.cos(b) - 2.0 * jnp.sin(0.5 * b) ** 2
    em1_im = jnp.exp(a) * jnp.sin(b)
    den = lam_re * lam_re + lam_im * lam_im
    z_re = (em1_re * lam_re + em1_im * lam_im) / den
    z_im = (em1_im * lam_re - em1_re * lam_im) / den
    bb_re = z_re[..., None] * b_re - z_im[..., None] * b_im
    bb_im = z_re[..., None] * b_im + z_im[..., None] * b_re
    def rows(w):
        return w.reshape(no, OCT, t, S5_GROUP, S5_STATE).transpose(0, 2, 1, 3, 4).reshape(no, t * LANES, S5_STATE)

    bt_re = bb_re.transpose(0, 2, 1)[:, None]
    bt_im = bb_im.transpose(0, 2, 1)[:, None]
    r_re, r_im = pw_re[:, t - 1::-1, None, :], pw_im[:, t - 1::-1, None, :]
    bre = rows(r_re * bt_re - r_im * bt_im)
    bim = rows(r_re * bt_im + r_im * bt_re)
    o_re, o_im = pw_re[:, 1:, None, :], pw_im[:, 1:, None, :]
    cre = rows(c_re[:, None] * o_re - c_im[:, None] * o_im)
    cim = rows(-(c_re[:, None] * o_im + c_im[:, None] * o_re))
    return bre, bim, cre, cim, c_re.reshape(no, LANES, S5_STATE), c_im.reshape(no, LANES, S5_STATE), pw_re, pw_im


def _s5_chunk(seqlen):
    return math.gcd(seqlen, S5_CHUNK)


def _s5_branch(u2, x0, ops, d, nseq, seqlen, out_dtype):
    bre, bim, cre, cim, zre, zim, pw_re, pw_im = ops
    no = u2.shape[0]
    t = _s5_chunk(seqlen)
    kk = t * LANES
    nchunks = seqlen // t
    are = pw_re[:, t].reshape(no, 1, OCT * S5_STATE)
    aim = pw_im[:, t].reshape(no, 1, OCT * S5_STATE)
    tail = S5_CHUNK * LANES - kk
    bre, bim = bre[:, tail:], bim[:, tail:]
    xs, xf = _s5_state(u2, bre, bim, x0, are, aim, nseq, nchunks)
    return _s5_out(u2, xs, bre, bim, cre[:, :kk], cim[:, :kk], zre, zim, d.reshape(no, 1, LANES), out_dtype), xf


def _s5_state_in(re, im, no):
    nseq = re.shape[0]
    f = lambda a: a.reshape(nseq, no, OCT * S5_STATE).transpose(1, 0, 2)
    return jnp.concatenate([f(re), f(im)], axis=-1)


def _s5_state_out(xf):
    no, nseq, _ = xf.shape
    hw = OCT * S5_STATE
    f = lambda a: a.transpose(1, 0, 2).reshape(1, nseq, no * OCT, S5_STATE)
    return f(xf[:, :, :hw]), f(xf[:, :, hw:])


def _gla_kernel(q_ref, k_ref, v_ref, la_ref, gb_ref, s0_ref, ng_ref, o_ref, sf_ref, st_ref,
                *, nseq, nchunks, c):
    i = pl.program_id(1)

    @pl.when(i == 0)
    def _():
        st_ref[...] = s0_ref[...]

    nh, hk, hv = GLA_HEADS, GLA_HEAD_K, GLA_HEAD_V
    nch = nseq * nchunks
    tm = nch * c
    iota = lambda shape, d: lax.broadcasted_iota(jnp.int32, shape, d)
    ng = ng_ref[...]

    tb = min(tm, MXU_TILE)
    rt, ct = iota((tb, tb), 0), iota((tb, tb), 1)
    tril = ((rt // c == ct // c) & (rt >= ct)).astype(BF16)
    parts = []
    for r0 in range(0, tm, tb):
        la_hi, la_lo = _split_bf16(la_ref[r0:r0 + tb, :])
        parts.append(_dot(tril, la_hi) + _dot(tril, la_lo))
    bcum = jnp.concatenate(parts, axis=0)
    e_hi, e_lo = _split_bf16(jnp.exp(jnp.concatenate([bcum[(ci + 1) * c - 1:(ci + 1) * c] for ci in range(nch)],
                                                     axis=0)))
    pick = (iota((nch, LANES), 0) == iota((nch, LANES), 1)).astype(BF16)
    dec_t = _dot_tn(e_hi, pick) + _dot_tn(e_lo, pick)
    q_all = q_ref[...].astype(F32)
    k_all = k_ref[...].astype(F32)
    q_dec_all = q_all * jnp.exp(bcum)
    k_dec_all = k_all * jnp.exp(-bcum)

    own_k = iota((nh * c, nh * hk), 0) // c == iota((nh * c, nh * hk), 1) // hk
    own_v = iota((nh * c, nh * hv), 0) // c == iota((nh * c, nh * hv), 1) // hv
    causal = iota((c, nh * c), 1) % c <= iota((c, nh * c), 0)
    zero_v = jnp.zeros((hk, hv), F32)

    def intra(ci):
        rows = slice(ci * c, (ci + 1) * c)
        q_dec = q_dec_all[rows].astype(BF16)
        k_dec = k_dec_all[rows]
        b_c = bcum[rows]
        k_tail = (k_all[rows] * jnp.exp(b_c[c - 1:c] - b_c)).astype(BF16)
        v = v_ref[rows, :].astype(F32)
        k_bd = jnp.where(own_k, jnp.concatenate([k_dec] * nh, axis=0), 0.0).astype(BF16)
        v_bd = jnp.where(own_v, jnp.concatenate([v] * nh, axis=0), 0.0).astype(BF16)
        att = jnp.where(causal, _dot_nt(q_dec, k_bd), 0.0)
        o_intra = _dot(att.astype(BF16), v_bd)
        vb = v.astype(BF16)
        kvs = []
        for h0 in range(0, nh, 2):
            kv2 = _dot_tn(k_tail[:, h0 * hk:(h0 + 2) * hk], vb[:, h0 * hv:(h0 + 2) * hv])
            kvs += [kv2[:hk, :hv], kv2[hk:, hv:]]
        return q_dec, o_intra, kvs

    def carry(ci, sts, q_dec, o_intra, kvs):
        st_bd = jnp.concatenate(
            [jnp.concatenate([zero_v] * h + [sts[h]] + [zero_v] * (nh - 1 - h), axis=1) for h in range(nh)],
            axis=0).astype(BF16)
        o = o_intra + _dot(q_dec, st_bd)
        new = [dec_t[h * hk:(h + 1) * hk, ci:ci + 1] * sts[h] + kvs[h] for h in range(nh)]
        return o, new

    def finish(ci, o):
        rows = slice(ci * c, (ci + 1) * c)
        for h in range(nh):
            vs = slice(h * hv, (h + 1) * hv)
            oh = _rms_rows(o[:, vs], ng) * gb_ref[rows, vs].astype(F32)
            o_ref[rows, vs] = oh.astype(o_ref.dtype)

    sts = None
    pending = intra(0)
    unfinished = None
    for ci in range(nch):
        s, first, last = ci // nchunks, ci % nchunks == 0, ci % nchunks == nchunks - 1
        current = pending
        if ci + 1 < nch:
            pending = intra(ci + 1)
        if first:
            sts = [st_ref[s, h] for h in range(nh)]
        o, sts = carry(ci, sts, *current)
        if last:
            for h in range(nh):
                st_ref[s, h] = sts[h]
        if unfinished is not None:
            finish(*unfinished)
        unfinished = (ci, o)
    finish(*unfinished)

    @pl.when(i == pl.num_programs(1) - 1)
    def _():
        sf_ref[...] = st_ref[...]


def _gla(q, k, v, la, gb, s0, ng, nseq_total, seqlen, out_dtype):
    c = math.gcd(seqlen, GLA_CHUNK)
    per_seq = seqlen // c
    if per_seq >= 8:
        nseq, nchunks = 1, next(n for n in (32, 16, 8, 1) if per_seq % n == 0)
    else:
        nseq, nchunks = 16, seqlen // c
    assert nseq_total % nseq == 0 and seqlen % (nchunks * c) == 0
    nblk = seqlen // (nchunks * c)
    tm = nseq * nchunks * c
    row = lambda n: pl.BlockSpec((tm, n), lambda b, i: (b * nblk + i, 0))
    st_spec = pl.BlockSpec((nseq, GLA_HEADS, GLA_HEAD_K, GLA_HEAD_V), lambda b, i: (b, 0, 0, 0))
    m = q.shape[0]
    return pl.pallas_call(
        functools.partial(_gla_kernel, nseq=nseq, nchunks=nchunks, c=c),
        grid=(nseq_total // nseq, nblk),
        in_specs=[row(GLA_K_WIDTH), row(GLA_K_WIDTH), row(GLA_V_WIDTH), row(GLA_K_WIDTH), row(GLA_V_WIDTH),
                  st_spec, _full(ng.shape)],
        out_specs=[row(GLA_V_WIDTH), st_spec],
        out_shape=[jax.ShapeDtypeStruct((m, GLA_V_WIDTH), out_dtype), jax.ShapeDtypeStruct(s0.shape, F32)],
        scratch_shapes=[pltpu.VMEM((nseq, GLA_HEADS, GLA_HEAD_K, GLA_HEAD_V), F32)],
        compiler_params=_cparams(("parallel", "arbitrary")),
        name="gla",
    )(q, k, v, la, gb, s0, ng)


def _gelu_tanh(x):
    return 0.5 * x * (1.0 + jnp.tanh(math.sqrt(2.0 / math.pi) * (x + 0.044715 * (x * x * x))))


def _head_ones(n):
    r = lax.broadcasted_iota(jnp.int32, (n, n), 0) // SWA_HEAD_DIM
    c = lax.broadcasted_iota(jnp.int32, (n, n), 1) // SWA_HEAD_DIM
    return (r == c).astype(BF16)


def _rope_block(x, cos_t, sin_t, upper):
    half = SWA_HEAD_DIM // 2
    swapped = jnp.where(upper, pltpu.roll(x, half, 1), pltpu.roll(x, LANES - half, 1))
    return x * cos_t + swapped * sin_t


def _mid_kernel(x_ref, y_ref, ga_ref, ob_ref, wglu_ref, bglu_ref, wo_ref,
                g_ref, w_ref, gq_ref, gk_ref, cos_ref, sin_ref, h_ref, q_ref, k_ref, v_ref, gate_ref):
    tm = x_ref.shape[0]
    lane = lax.broadcasted_iota(jnp.int32, (tm, LANES), 1)
    upper = (lane & (SWA_HEAD_DIM // 2)) != 0
    ones4 = _head_ones(MXU_TILE)
    inv_d = 1.0 / SWA_HEAD_DIM
    qw = SWA_HEADS * SWA_HEAD_DIM
    kw = SWA_KV_HEADS * SWA_HEAD_DIM
    nblk = qw // MXU_TILE
    no = y_ref.shape[0]

    na = ga_ref.shape[1]
    mix_b = _dot(ob_ref[...].astype(BF16), wo_ref[na:, :])
    z = _gelu_tanh(jnp.concatenate([y_ref[o].astype(F32) for o in range(no)], axis=1))
    z = z * _sigmoid(_dot(z.astype(BF16), wglu_ref[...]) + bglu_ref[...])
    out_a = z * ga_ref[...].astype(F32)
    h = x_ref[...] + (_dot(out_a.astype(BF16), wo_ref[:na, :]) + mix_b)
    h_ref[...] = h

    xb = _rms_rows(h, g_ref[...]).astype(BF16)
    cos_t = cos_ref[...]
    sin_t = sin_ref[...]

    def gate_finish(gate):
        gate_ref[...] = _silu(gate).astype(gate_ref.dtype)

    def q_finish(j, q):
        ss = _dot((q * q).astype(BF16), ones4)
        qn = q * lax.rsqrt(ss * inv_d + NORM_EPS) * gq_ref[...]
        for e in range(MXU_TILE // LANES):
            cols = slice(j * MXU_TILE + e * LANES, j * MXU_TILE + (e + 1) * LANES)
            qe = _rope_block(qn[:, e * LANES:(e + 1) * LANES], cos_t, sin_t, upper)
            q_ref[:, cols] = (qe * (SWA_HEAD_DIM ** -0.5 * LOG2E)).astype(q_ref.dtype)

    def kv_finish(kv):
        k = kv[:, :kw]
        ss = _dot((k * k).astype(BF16), ones4[:kw, :kw])
        kn = k * lax.rsqrt(ss * inv_d + NORM_EPS) * gk_ref[...]
        k_ref[...] = _rope_block(kn, cos_t, sin_t, upper)
        v_ref[...] = kv[:, kw:]

    work = [(slice(j * MXU_TILE, (j + 1) * MXU_TILE), functools.partial(q_finish, j)) for j in range(nblk)]
    work += [(slice(qw, qw + 2 * kw), kv_finish)]
    work += [(slice(qw + 2 * kw, None), gate_finish)]
    pending = _dot(xb, w_ref[:, work[0][0]])
    for n, (_, finish) in enumerate(work):
        current = pending
        if n + 1 < len(work):
            pending = _dot(xb, w_ref[:, work[n + 1][0]])
        finish(current)


def _mid(x, y, ga, ob, pe, po, cos_t, sin_t, act_dtype):
    m, dm = x.shape
    tm = _row_tile(m)
    row = lambda n: pl.BlockSpec((tm, n), lambda i: (i, 0))
    slab = pl.BlockSpec((y.shape[0], tm, LANES), lambda i: (0, i, 0))
    assert cos_t.shape[0] % tm == 0
    nper = cos_t.shape[0] // tm
    tab = pl.BlockSpec((tm, LANES), lambda i: (i % nper, 0))
    qw = SWA_HEADS * SWA_HEAD_DIM
    kw = SWA_KV_HEADS * SWA_HEAD_DIM
    weights = [pe['wglu'], pe['bglu'], pe['wo'], po['norm_g'], po['w_in'], po['gq'], po['gk']]
    return pl.pallas_call(
        _mid_kernel,
        grid=(m // tm,),
        in_specs=[row(dm), slab, row(S5_WIDTH), row(GLA_V_WIDTH)] + [_full(w.shape) for w in weights] + [tab, tab],
        out_specs=[row(dm), row(qw), row(kw), row(kw), row(qw)],
        out_shape=[jax.ShapeDtypeStruct((m, dm), F32),
                   jax.ShapeDtypeStruct((m, qw), act_dtype), jax.ShapeDtypeStruct((m, kw), F32),
                   jax.ShapeDtypeStruct((m, kw), F32), jax.ShapeDtypeStruct((m, qw), act_dtype)],
        compiler_params=_cparams(("parallel",)),
        name="mid",
    )(x, y, ga, ob, *weights, cos_t, sin_t)


def _rope_tables(pos):
    half = SWA_HEAD_DIM // 2
    inv_freq = ROPE_THETA ** (-np.arange(half, dtype=np.float64) / half)
    ang = np.asarray(pos, np.float64)[:, None] * inv_freq[None, :]
    cos, sin = np.cos(ang), np.sin(ang)
    cos_t = np.concatenate([cos, cos, cos, cos], axis=1).astype(np.float32)
    sin_t = np.concatenate([-sin, sin, -sin, sin], axis=1).astype(np.float32)
    return jnp.asarray(cos_t), jnp.asarray(sin_t)


def _attn_prompt_kernel(sink_ref, q_ref, kc_ref, kp_ref, vc_ref, vp_ref, gate_ref, h_ref, wout_ref,
                        y_ref, p_ref, o_ref, *, nqb):
    i = pl.program_id(1)
    w = SWA_WINDOW
    hd = SWA_HEAD_DIM
    npair = SWA_GROUP // 2
    lane2 = lax.broadcasted_iota(jnp.int32, (2 * w, LANES), 1)
    low = lane2 < hd
    rr = lax.broadcasted_iota(jnp.int32, (w, w), 0)
    cc = lax.broadcasted_iota(jnp.int32, (w, w), 1)
    tri = cc <= rr
    low_w = cc < hd
    r4 = lax.broadcasted_iota(jnp.int32, (4 * w, LANES), 0)
    c4 = lax.broadcasted_iota(jnp.int32, (4 * w, LANES), 1)
    den_cols = ((r4 < 2 * w) == (c4 < hd)).astype(BF16)
    units = [(jb, kv) for jb in range(nqb) for kv in range(SWA_KV_HEADS)]

    def scores(jb, kv):
        rows = slice(jb * w, (jb + 1) * w)
        if jb == 0:
            k_prev, v_prev = kp_ref[...], vp_ref[...]
        else:
            prev_rows = slice((jb - 1) * w, jb * w)
            k_prev, v_prev = kc_ref[prev_rows, :], vc_ref[prev_rows, :]
        kcat = jnp.concatenate([k_prev, kc_ref[rows, :]], axis=0)
        vcat = jnp.concatenate([v_prev, vc_ref[rows, :]], axis=0)
        own = low if kv == 0 else jnp.logical_not(low)
        k_own = jnp.where(own, kcat, 0.0)
        v_own = jnp.where(own, vcat, 0.0)
        k_oth = pltpu.roll(k_own, hd, 1)
        v_oth = pltpu.roll(v_own, hd, 1)
        k_lo, k_hi = (k_own, k_oth) if kv == 0 else (k_oth, k_own)
        v_lo, v_hi = (v_own, v_oth) if kv == 0 else (v_oth, v_own)
        k_rhs = jnp.concatenate([k_lo, k_hi], axis=0).astype(BF16)
        v_rhs = jnp.concatenate([jnp.concatenate([v_lo, v_hi], axis=0).astype(BF16), den_cols], axis=1)
        qs = jnp.concatenate([q_ref[rows, (kv * npair + pr) * LANES:(kv * npair + pr + 1) * LANES]
                              for pr in range(npair)], axis=0)
        return _dot_nt(qs, k_rhs), v_rhs

    def softmax_pv(kv, first, buf, s_all, v_rhs):
        sink_terms = []
        for pr in range(npair):
            prow = slice(pr * w, (pr + 1) * w)
            pair_terms = []
            for e in range(2):
                s_prev = s_all[prow, (2 * e) * w:(2 * e + 1) * w]
                s_cur = s_all[prow, (2 * e + 1) * w:(2 * e + 2) * w]
                if first:
                    s_prev = jnp.where(i > 0, s_prev, -jnp.inf)
                sc = jnp.where(tri, s_cur, s_prev)
                sink = sink_ref[2 * (kv * npair + pr) + e] * LOG2E
                mx = jnp.maximum(jnp.max(sc, axis=-1, keepdims=True), sink)
                pe = jnp.exp2(sc - mx)
                p_ref[buf, prow, (2 * e) * w:(2 * e + 1) * w] = jnp.where(tri, 0.0, pe).astype(BF16)
                p_ref[buf, prow, (2 * e + 1) * w:(2 * e + 2) * w] = jnp.where(tri, pe, 0.0).astype(BF16)
                pair_terms.append(jnp.exp2(sink - mx))
            sink_terms.append(pair_terms)
        return _dot(p_ref[buf], v_rhs), sink_terms

    def normalise(jb, kv, o_ext, sink_terms):
        rows = slice(jb * w, (jb + 1) * w)
        for pr in range(npair):
            prow = slice(pr * w, (pr + 1) * w)
            cols = slice((kv * npair + pr) * LANES, (kv * npair + pr + 1) * LANES)
            st = jnp.where(low_w, sink_terms[pr][0], sink_terms[pr][1])
            o = o_ext[prow, :LANES] / (o_ext[prow, LANES:] + st)
            o_ref[rows, cols] = o.astype(o_ref.dtype)

    def project(rows):
        og = o_ref[rows, :].astype(F32) * gate_ref[rows, :].astype(F32)
        y_ref[rows, :] = h_ref[rows, :] + _dot(og.astype(BF16), wout_ref[...])

    group = 2
    pending = scores(*units[0])
    unfinished = None
    for n, (jb, kv) in enumerate(units):
        current = pending
        if n + 1 < len(units):
            pending = scores(*units[n + 1])
        result = softmax_pv(kv, jb == 0, n % 2, *current)
        if unfinished is not None:
            normalise(*unfinished)
            done_jb, done_kv = unfinished[:2]
            if done_kv == SWA_KV_HEADS - 1 and (done_jb + 1) % group == 0:
                project(slice((done_jb + 1 - group) * w, (done_jb + 1) * w))
        unfinished = (jb, kv) + result
    normalise(*unfinished)
    project(slice((nqb - group) * w, nqb * w))


def _attn_prompt(sinks, q, k, v, gate, h, w_out, nseq, seqlen):
    w = SWA_WINDOW
    nqb = 8
    tm = nqb * w
    assert seqlen % tm == 0
    nblk = seqlen // tm
    qw = q.shape[1]
    dm = h.shape[1]
    row = lambda n: pl.BlockSpec((tm, n), lambda b, i: (b * nblk + i, 0))
    prev = lambda n: pl.BlockSpec((w, n), lambda b, i: (jnp.maximum((b * nblk + i) * nqb - 1, 0), 0))
    smem = pl.BlockSpec(memory_space=pltpu.SMEM)
    return pl.pallas_call(
        functools.partial(_attn_prompt_kernel, nqb=nqb),
        grid=(nseq, nblk),
        in_specs=[smem, row(qw), row(LANES), prev(LANES), row(LANES), prev(LANES), row(qw), row(dm),
                  _full(w_out.shape)],
        out_specs=row(dm),
        out_shape=jax.ShapeDtypeStruct(h.shape, F32),
        scratch_shapes=[pltpu.VMEM((2, 4 * w, 4 * w), BF16), pltpu.VMEM((tm, qw), BF16)],
        compiler_params=_cparams(("parallel", "arbitrary")),
        name="attn_prompt",
    )(sinks, q, k, k, v, v, gate, h, w_out)


def _attn_sample_kernel(sink_ref, q_ref, kn_ref, vn_ref, kc_ref, vc_ref, o_ref, ko_ref, vo_ref, q2_ref, *, seqlen):
    nq = SWA_HEADS * seqlen
    hd = SWA_HEAD_DIM
    heads = [(kv, g) for kv in range(SWA_KV_HEADS) for g in range(SWA_GROUP)]

    def stacked(kv, g):
        r0 = (kv * SWA_GROUP + g) * seqlen
        return slice(r0, r0 + seqlen), slice(kv * hd, (kv + 1) * hd)

    q2_ref[...] = jnp.zeros(q2_ref.shape, q2_ref.dtype)
    for kv, g in heads:
        rows, lanes = stacked(kv, g)
        h = kv * SWA_GROUP + g
        q2_ref[:, rows, lanes] = q_ref[:, :, h * hd:(h + 1) * hd]
    ncache = kc_ref.shape[1]
    t_row = lax.broadcasted_iota(jnp.int32, (nq, ncache), 0) % seqlen
    c_col = lax.broadcasted_iota(jnp.int32, (nq, ncache), 1)
    cache_ok = c_col > t_row - (SWA_WINDOW - ncache)
    nnew = kn_ref.shape[1]
    t_row_n = lax.broadcasted_iota(jnp.int32, (nq, nnew), 0) % seqlen
    n_col = lax.broadcasted_iota(jnp.int32, (nq, nnew), 1)
    new_ok = n_col <= t_row_n
    sink = (sink_ref[...] * LOG2E)[None, :, 0:1]
    bqk = lambda a, b: lax.dot_general(a, b, (((2,), (2,)), ((0,), (0,))), preferred_element_type=F32)
    bpv = lambda a, b: lax.dot_general(a, b, (((2,), (1,)), ((0,), (0,))), preferred_element_type=F32)
    q = q2_ref[...].astype(BF16)
    kc = kc_ref[...]
    vc = vc_ref[...]
    kn = kn_ref[...]
    vn = vn_ref[...]
    sc = jnp.where(cache_ok[None], bqk(q, kc.astype(BF16)), -jnp.inf)
    sn = jnp.where(new_ok[None], bqk(q, kn.astype(BF16)), -jnp.inf)
    mx = jnp.maximum(jnp.maximum(jnp.max(sc, axis=-1, keepdims=True), jnp.max(sn, axis=-1, keepdims=True)), sink)
    pc = jnp.exp2(sc - mx)
    pn = jnp.exp2(sn - mx)
    den = jnp.sum(pc, axis=-1, keepdims=True) + jnp.sum(pn, axis=-1, keepdims=True) + jnp.exp2(sink - mx)
    inv = 1.0 / den
    o2 = bpv((pc * inv).astype(BF16), vc.astype(BF16)) + bpv((pn * inv).astype(BF16), vn.astype(BF16))
    for kv, g in heads:
        rows, lanes = stacked(kv, g)
        h = kv * SWA_GROUP + g
        o_ref[:, :, h * hd:(h + 1) * hd] = o2[:, rows, lanes]
    keep = ncache - seqlen
    ko_ref[:, 0:keep, :] = kc[:, seqlen:ncache, :]
    ko_ref[:, keep:ncache, :] = kn[:, 0:seqlen, :]
    vo_ref[:, 0:keep, :] = vc[:, seqlen:ncache, :]
    vo_ref[:, keep:ncache, :] = vn[:, 0:seqlen, :]


def _attn_sample(sink_rows, q, kn, vn, kc, vc):
    n, seqlen, _ = q.shape
    nseq = 16
    assert n % nseq == 0
    blk = lambda a: pl.BlockSpec((nseq,) + a.shape[1:], lambda i: (i, 0, 0))
    return pl.pallas_call(
        functools.partial(_attn_sample_kernel, seqlen=seqlen),
        grid=(n // nseq,),
        in_specs=[_full(sink_rows.shape), blk(q), blk(kn), blk(vn), blk(kc), blk(vc)],
        out_specs=[blk(q), blk(kc), blk(vc)],
        out_shape=[jax.ShapeDtypeStruct(q.shape, F32), jax.ShapeDtypeStruct(kc.shape, F32),
                   jax.ShapeDtypeStruct(vc.shape, F32)],
        scratch_shapes=[pltpu.VMEM((nseq, SWA_HEADS * seqlen, SWA_KV_HEADS * SWA_HEAD_DIM), F32)],
        compiler_params=_cparams(("parallel",)),
        name="attn_sample",
    )(sink_rows, q, kn, vn, kc, vc)


def _odd_out_kernel(h_ref, o_ref, gate_ref, w_ref, y_ref):
    og = o_ref[...].astype(F32) * gate_ref[...].astype(F32)
    y_ref[...] = h_ref[...] + _dot(og.astype(BF16), w_ref[...])


def _odd_out(h, o, gate, w):
    m, dm = h.shape
    tm = _row_tile(m)
    row = lambda n: pl.BlockSpec((tm, n), lambda i: (i, 0))
    return pl.pallas_call(
        _odd_out_kernel,
        grid=(m // tm,),
        in_specs=[row(dm), row(o.shape[1]), row(gate.shape[1]), _full(w.shape)],
        out_specs=row(dm),
        out_shape=jax.ShapeDtypeStruct((m, dm), F32),
        compiler_params=_cparams(("parallel",)),
        name="odd_out",
    )(h, o, gate, w)


def _trunk(x, s5_x0, gla_s0, pe, po, pos, nseq, seqlen, act_dtype):
    u2, ga, q, k, v, la, gb = _even_in(x, pe['norm_g'], pe['w_in'], pe['wgate'], pe['bgate'], act_dtype,
                                       _s5_chunk(seqlen))
    y4, s5_fin = _s5_branch(u2, s5_x0, pe['s5_ops'], pe['d'], nseq, seqlen, act_dtype)
    ob, gla_fin = _gla(q, k, v, la, gb, gla_s0, pe['gla_norm_g'], nseq, seqlen, act_dtype)
    cos_t, sin_t = _rope_tables(pos)
    h, q1, k1, v1, gate = _mid(x, y4, ga, ob, pe, po, cos_t, sin_t, act_dtype)
    return h, q1, k1, v1, gate, s5_fin, gla_fin


def kernel(x_prompt, x_sample, state_s5_re, state_s5_im, state_gla, cache_swa_k, cache_swa_v,
           even_norm_g, even_w_in, s5_lambda_re, s5_lambda_im, s5_log_dt, s5_b_re, s5_b_im,
           s5_c_re, s5_c_im, s5_d, s5_w_glu, s5_b_glu, gla_w_gate, gla_b_gate, gla_norm_g,
           even_w_out, odd_norm_g, odd_w_in, swa_q_norm_g, swa_k_norm_g, swa_sinks, odd_w_out):
    nb, seq, dm = x_prompt.shape
    ns, dseq, _ = x_sample.shape
    ng = s5_lambda_re.shape[1]
    no = ng // OCT
    xp = x_prompt.reshape(nb * seq, dm)
    xs = x_sample.reshape(ns * dseq, dm)

    i = 0
    w_in = even_w_in[i]
    assert ng * S5_GROUP == S5_WIDTH
    col_alow = C_CODE
    col_gb = col_alow + GLA_GATE_RANK
    pad_rank = LANES - GLA_GATE_RANK
    pe = {
        'norm_g': even_norm_g[i][None, :],
        'w_in': jnp.concatenate([w_in[:, :col_alow], jnp.pad(w_in[:, col_alow:col_gb], ((0, 0), (0, pad_rank))),
                                 w_in[:, col_gb:]], axis=1).astype(BF16),
        'wgate': jnp.pad(gla_w_gate[i], ((0, pad_rank), (0, 0))).astype(BF16),
        'bgate': gla_b_gate[i][None, :],
        's5_ops': _s5_params(s5_lambda_re[i], s5_lambda_im[i], s5_log_dt[i], s5_b_re[i], s5_b_im[i],
                             s5_c_re[i], s5_c_im[i]),
        'gla_norm_g': gla_norm_g[i][None, :],
        'd': s5_d[i],
        'wglu': s5_w_glu[i].astype(BF16),
        'bglu': s5_b_glu[i][None, :],
        'wo': even_w_out[i].astype(BF16),
    }
    po = {
        'norm_g': odd_norm_g[i][None, :],
        'w_in': odd_w_in[i].astype(BF16),
        'gq': jnp.tile(swa_q_norm_g[i], MXU_TILE // SWA_HEAD_DIM)[None, :],
        'gk': jnp.tile(swa_k_norm_g[i], LANES // SWA_HEAD_DIM)[None, :],
    }
    w_out = odd_w_out[i].astype(BF16)
    sinks = swa_sinks[i]
    kvw = SWA_KV_HEADS * SWA_HEAD_DIM

    s5_zero = jnp.zeros((no, nb, 2 * OCT * S5_STATE), F32)
    gla_zero = jnp.zeros((nb, GLA_HEADS, GLA_HEAD_K, GLA_HEAD_V), F32)
    hp, q, k, v, gate, s5_p, gla_p = _trunk(xp, s5_zero, gla_zero, pe, po, np.arange(seq), nb, seq, BF16)
    s5r_p, s5i_p = _s5_state_out(s5_p)
    y_prompt = _attn_prompt(sinks, q, k, v, gate, hp, w_out, nb, seq).reshape(nb, seq, dm)
    cache_len = min(SWA_WINDOW, seq)
    tail = lambda a: (a.reshape(nb, seq, kvw)[:, seq - cache_len:]
                      .reshape(1, nb, cache_len, SWA_KV_HEADS, SWA_HEAD_DIM))
    swk_p, swv_p = tail(k), tail(v)

    pos_s = np.tile(PAST_LEN + np.arange(dseq), ns)
    s5_init = _s5_state_in(state_s5_re[i], state_s5_im[i], no)
    hs, q, k, v, gate, s5_s, gla_s = _trunk(xs, s5_init, state_gla[i], pe, po, pos_s, ns, dseq, F32)
    s5r_s, s5i_s = _s5_state_out(s5_s)
    ncache = cache_swa_k.shape[2]
    sink_rows = jnp.broadcast_to(jnp.repeat(sinks, dseq)[:, None], (SWA_HEADS * dseq, LANES))
    kn = k.reshape(ns, dseq, kvw)
    vn = v.reshape(ns, dseq, kvw)
    npad = BF16_ROWS - dseq
    kn_pad = jnp.pad(kn, ((0, 0), (0, npad), (0, 0)))
    vn_pad = jnp.pad(vn, ((0, 0), (0, npad), (0, 0)))
    kc = cache_swa_k[i].reshape(ns, ncache, kvw)
    vc = cache_swa_v[i].reshape(ns, ncache, kvw)
    o, kc_new, vc_new = _attn_sample(sink_rows, q.reshape(ns, dseq, -1), kn_pad, vn_pad, kc, vc)
    y_sample = _odd_out(hs, o.reshape(ns * dseq, -1), gate, w_out).reshape(ns, dseq, dm)
    swk_s = kc_new.reshape(1, ns, ncache, SWA_KV_HEADS, SWA_HEAD_DIM)
    swv_s = vc_new.reshape(1, ns, ncache, SWA_KV_HEADS, SWA_HEAD_DIM)

    return (y_prompt, y_sample,
            s5r_p, s5i_p, gla_p[None], swk_p, swv_p,
            s5r_s, s5i_s, gla_s[None], swk_s, swv_s)
```

```python
import functools
import math

import jax
import jax.numpy as jnp
import numpy as np
from jax import lax
from jax.experimental import pallas as pl
from jax.experimental.pallas import tpu as pltpu

F32 = jnp.float32
BF16 = jnp.bfloat16

PAST_LEN = 8192
NORM_EPS = 1e-6
S5_GROUP = 16
S5_STATE = 64
S5_CHUNK = 16
GLA_HEADS = 4
GLA_HEAD_K = 64
GLA_HEAD_V = 128
GLA_GATE_RANK = 16
GLA_GATE_TAU = 16.0
GLA_CHUNK = 64
S5_WIDTH = 512
GLA_K_WIDTH = GLA_HEADS * GLA_HEAD_K
GLA_V_WIDTH = GLA_HEADS * GLA_HEAD_V
SWA_HEADS = 16
SWA_KV_HEADS = 2
SWA_GROUP = SWA_HEADS // SWA_KV_HEADS
SWA_HEAD_DIM = 64
SWA_WINDOW = 128
ROPE_THETA = 10000.0
LOG2E = math.log2(math.e)
LANES = 128
BF16_ROWS = 16
MXU_TILE = 256
OCT = LANES // S5_GROUP
ROW_TILE = 512
VMEM_LIMIT = 48 * 1024 * 1024
C_U = 0
C_GATE_A = C_U + S5_WIDTH
C_Q = C_GATE_A + S5_WIDTH
C_K = C_Q + GLA_K_WIDTH
C_V = C_K + GLA_K_WIDTH
C_CODE = C_V + GLA_V_WIDTH
C_GATE_B = C_CODE + LANES
C_END = C_GATE_B + GLA_V_WIDTH


def _cparams(sem):
    return pltpu.CompilerParams(dimension_semantics=sem, vmem_limit_bytes=VMEM_LIMIT)


def _full(shape):
    n = len(shape)
    return pl.BlockSpec(shape, lambda *_: (0,) * n)


def _dot(a, b):
    return jnp.dot(a, b, preferred_element_type=F32)


def _dot_nt(a, b):
    return lax.dot_general(a, b, (((1,), (1,)), ((), ())), preferred_element_type=F32)


def _dot_tn(a, b):
    return lax.dot_general(a, b, (((0,), (0,)), ((), ())), preferred_element_type=F32)


def _split_bf16(x):
    hi = x.astype(BF16)
    lo = (x - hi.astype(F32)).astype(BF16)
    return hi, lo


def _rms_rows(x, g):
    return x * lax.rsqrt(jnp.mean(x * x, axis=-1, keepdims=True) + NORM_EPS) * g


def _sigmoid(x):
    return 1.0 / (1.0 + jnp.exp(-x))


def _silu(x):
    return x * _sigmoid(x)


def _row_tile(m, tile=ROW_TILE):
    return tile if m % tile == 0 else m


def _even_in_kernel(x_ref, g_ref, w_ref, wgate_ref, bgate_ref,
                    u2_ref, ga_ref, q_ref, k_ref, v_ref, la_ref, gb_ref, uscr_ref, *, t):
    xb = _rms_rows(x_ref[...], g_ref[...]).astype(BF16)

    def proj(lo, hi):
        return _dot(xb, w_ref[:, lo:hi])

    u = proj(C_U, C_GATE_A)
    nrow = u.shape[0] // t
    for o in range(u2_ref.shape[0]):
        uscr_ref[o] = u[:, o * LANES:(o + 1) * LANES]
        for tt in range(t):
            piece = uscr_ref[o, pl.ds(tt, nrow, stride=t), :]
            u2_ref[o, :, tt * LANES:(tt + 1) * LANES] = piece.astype(u2_ref.dtype)
    ga_ref[...] = _silu(proj(C_GATE_A, C_Q)).astype(ga_ref.dtype)
    q_ref[...] = (proj(C_Q, C_K) * (GLA_HEAD_K ** -0.5)).astype(q_ref.dtype)
    k_ref[...] = proj(C_K, C_V).astype(k_ref.dtype)
    v_ref[...] = proj(C_V, C_CODE).astype(v_ref.dtype)
    gb_ref[...] = _silu(proj(C_GATE_B, C_END)).astype(gb_ref.dtype)
    a_low = proj(C_CODE, C_GATE_B)
    logit = _dot(a_low.astype(BF16), wgate_ref[...]) + bgate_ref[...]
    log_sig = jnp.minimum(logit, 0.0) - jnp.log1p(jnp.exp(-jnp.abs(logit)))
    la_ref[...] = log_sig * (1.0 / GLA_GATE_TAU)


def _even_in(x, g, w, wgate, bgate, act_dtype, t):
    m, d = x.shape
    tm = _row_tile(m, 2 * ROW_TILE) if m > ROW_TILE else m
    row = lambda n: pl.BlockSpec((tm, n), lambda i: (i, 0))
    assert w.shape[1] == C_END
    no = S5_WIDTH // LANES
    chunk = pl.BlockSpec((no, tm // t, t * LANES), lambda i: (0, i, 0))
    outs = [(S5_WIDTH, act_dtype), (GLA_K_WIDTH, act_dtype), (GLA_K_WIDTH, act_dtype), (GLA_V_WIDTH, act_dtype),
            (GLA_K_WIDTH, F32), (GLA_V_WIDTH, act_dtype)]
    return pl.pallas_call(
        functools.partial(_even_in_kernel, t=t),
        grid=(m // tm,),
        in_specs=[row(d), _full(g.shape), _full(w.shape), _full(wgate.shape), _full(bgate.shape)],
        out_specs=[chunk] + [row(n) for n, _ in outs],
        out_shape=[jax.ShapeDtypeStruct((no, m // t, t * LANES), act_dtype)]
        + [jax.ShapeDtypeStruct((m, n), dt) for n, dt in outs],
        scratch_shapes=[pltpu.VMEM((no, tm, LANES), F32)],
        compiler_params=_cparams(("parallel",)),
        name="even_in",
    )(x, g, w, wgate, bgate)


def _group_mask(shape, row_span, col_span):
    rg = (lax.broadcasted_iota(jnp.int32, shape, 0) // row_span) % OCT
    cg = (lax.broadcasted_iota(jnp.int32, shape, 1) // col_span) % OCT
    return rg == cg


def _s5_chunk_states(u_ref, bre_ref, bim_ref, x0_ref, are_ref, aim_ref, xf_ref, loc_ref, nseq, nchunks):
    hw = OCT * S5_STATE
    kk = u_ref.shape[2]
    bp = jnp.concatenate([bre_ref[0]] * OCT + [bim_ref[0]] * OCT, axis=1)
    bp = jnp.where(_group_mask((kk, 2 * hw), S5_GROUP, S5_STATE), bp, 0.0).astype(BF16)
    loc_ref[...] = _dot(u_ref[0], bp)
    a_re = are_ref[0]
    a_im = aim_ref[0]
    if nchunks == 1:
        x0 = x0_ref[0]
        xr, xi = x0[:, :hw], x0[:, hw:]
        loc = loc_ref[...]
        xf_ref[0, :, :hw] = a_re * xr - a_im * xi + loc[:, :hw]
        xf_ref[0, :, hw:] = a_re * xi + a_im * xr + loc[:, hw:]
        return x0.astype(BF16)
    else:
        def body(j, carry):
            new = []
            for b in range(nseq):
                xr, xi = carry[b]
                row = pl.ds(b * nchunks + j, 1)
                lr = loc_ref[row, :hw]
                li = loc_ref[row, hw:]
                loc_ref[row, :hw] = xr
                loc_ref[row, hw:] = xi
                new.append((a_re * xr - a_im * xi + lr, a_re * xi + a_im * xr + li))
            return tuple(new)

        init = tuple((x0_ref[0, b:b + 1, :hw], x0_ref[0, b:b + 1, hw:]) for b in range(nseq))
        fin = lax.fori_loop(0, nchunks, body, init, unroll=4)
        for b in range(nseq):
            xf_ref[0, b:b + 1, :hw] = fin[b][0]
            xf_ref[0, b:b + 1, hw:] = fin[b][1]
        return loc_ref[...].astype(BF16)


def _dot_nt_f32(a, b):
    a_hi, a_lo = _split_bf16(a)
    b_hi, b_lo = _split_bf16(b)
    return _dot_nt(a_hi, b_hi) + _dot_nt(a_hi, b_lo) + _dot_nt(a_lo, b_hi)


def _s5_kernel(u_ref, bre_ref, bim_ref, cre_ref, cim_ref, zre_ref, zim_ref, d_ref, x0_ref, are_ref, aim_ref,
               y_ref, xf_ref, loc_ref, yscr_ref, *, nseq, nchunks):
    r, kk = u_ref.shape[1], u_ref.shape[2]
    t = kk // LANES
    sw = x0_ref.shape[2]
    ntile = kk // MXU_TILE
    xs = _s5_chunk_states(u_ref, bre_ref, bim_ref, x0_ref, are_ref, aim_ref, xf_ref, loc_ref, nseq, nchunks)
    taps = _dot_nt_f32(bre_ref[0], zre_ref[0]) - _dot_nt_f32(bim_ref[0], zim_ref[0])
    tmask = _group_mask((LANES, LANES), S5_GROUP, S5_GROUP)

    def tap(lag):
        if lag < 0:
            return jnp.zeros((LANES, LANES), F32)
        s = t - 1 - lag
        return jnp.where(tmask, taps[s * LANES:(s + 1) * LANES], 0.0)

    wts = [jnp.concatenate([jnp.concatenate([tap(2 * d), tap(2 * d + 1)], axis=1),
                            jnp.concatenate([tap(2 * d - 1), tap(2 * d)], axis=1)], axis=0).astype(BF16)
           for d in range(ntile)]
    cpt = jnp.concatenate([cre_ref[0]] * OCT + [cim_ref[0]] * OCT, axis=1)
    cpt = jnp.where(_group_mask((kk, sw), S5_GROUP, S5_STATE), cpt, 0.0).astype(BF16)
    skip =jnp.concatenate([d_ref[0]] * (MXU_TILE // LANES), axis=1)
    for n in range(ntile):
        cols = slice(n * MXU_TILE, (n + 1) * MXU_TILE)
        acc = _dot_nt(xs, cpt[cols]) + skip * u_ref[0, :, cols].astype(F32)
        for k in range(n + 1):
            acc = acc + _dot(u_ref[0, :, k * MXU_TILE:(k + 1) * MXU_TILE], wts[n - k])
        for e in range(MXU_TILE // LANES):
            yscr_ref[pl.ds(2 * n + e, r, stride=t), :] = acc[:, e * LANES:(e + 1) * LANES]
    y_ref[0] = yscr_ref[...].astype(y_ref.dtype)


def _s5(u2, bre, bim, cre, cim, zre, zim, d, x0, are, aim, nseq, nchunks, out_dtype):
    no, r, kk = u2.shape
    m = r * (kk // LANES)
    sw = x0.shape[2]
    blk = lambda a: pl.BlockSpec((1,) + a.shape[1:], lambda o: (o,) + (0,) * (a.ndim - 1))
    ops = (u2, bre, bim, cre, cim, zre, zim, d, x0, are, aim)
    return pl.pallas_call(
        functools.partial(_s5_kernel, nseq=nseq, nchunks=nchunks),
        grid=(no,),
        in_specs=[blk(a) for a in ops],
        out_specs=[pl.BlockSpec((1, m, LANES), lambda o: (o, 0, 0)), blk(x0)],
        out_shape=[jax.ShapeDtypeStruct((no, m, LANES), out_dtype), jax.ShapeDtypeStruct(x0.shape, F32)],
        scratch_shapes=[pltpu.VMEM((r, sw), F32), pltpu.VMEM((m, LANES), F32)],
        compiler_params=_cparams(("parallel",)),
        name="s5",
    )(*ops)


def _s5_params(lam_re, lam_im, log_dt, b_re, b_im, c_re, c_im):
    t = S5_CHUNK
    ng = lam_re.shape[0]
    no = ng // OCT
    dt = jnp.exp(log_dt)[:, None]
    a = lam_re * dt
    b = lam_im * dt
    n = jnp.arange(t + 1, dtype=F32)[None, :, None]
    mag = jnp.exp(n * a[:, None, :])
    pw_re = mag * jnp.cos(n * b[:, None, :])
    pw_im = mag * jnp.sin(n * b[:, None, :])
    em1_re = jnp.expm1(a) * jnp.cos(b) - 2.0 * jnp.sin(0.5 * b) ** 2
    em1_im = jnp.exp(a) * jnp.sin(b)
    den = lam_re * lam_re + lam_im * lam_im
    z_re = (em1_re * lam_re + em1_im * lam_im) / den
    z_im = (em1_im * lam_re - em1_re * lam_im) / den
    bb_re = z_re[..., None] * b_re - z_im[..., None] * b_im
    bb_im = z_re[..., None] * b_im + z_im[..., None] * b_re
    def rows(w):
        return w.reshape(no, OCT, t, S5_GROUP, S5_STATE).transpose(0, 2, 1, 3, 4).reshape(no, t * LANES, S5_STATE)

    bt_re = bb_re.transpose(0, 2, 1)[:, None]
    bt_im = bb_im.transpose(0, 2, 1)[:, None]
    nr = (t - 1) - jnp.arange(t, dtype=F32)[None, :, None]
    mag_r = jnp.exp(nr * a[:, None, :])
    r_re = (mag_r * jnp.cos(nr * b[:, None, :]))[:, :, None, :]
    r_im = (mag_r * jnp.sin(nr * b[:, None, :]))[:, :, None, :]
    bre = rows(r_re * bt_re - r_im * bt_im)
    bim = rows(r_re * bt_im + r_im * bt_re)
    o_re, o_im = pw_re[:, 1:, None, :], pw_im[:, 1:, None, :]
    cre = rows(c_re[:, None] * o_re - c_im[:, None] * o_im)
    cim = rows(-(c_re[:, None] * o_im + c_im[:, None] * o_re))
    return bre, bim, cre, cim, c_re.reshape(no, LANES, S5_STATE), c_im.reshape(no, LANES, S5_STATE), pw_re, pw_im


def _s5_chunk(seqlen):
    return math.gcd(seqlen, S5_CHUNK)


def _s5_branch(u2, x0, ops, d, nseq, seqlen, out_dtype):
    bre, bim, cre, cim, zre, zim, pw_re, pw_im = ops
    no = u2.shape[0]
    t = _s5_chunk(seqlen)
    kk = t * LANES
    nchunks = seqlen // t
    are = pw_re[:, t].reshape(no, 1, OCT * S5_STATE)
    aim = pw_im[:, t].reshape(no, 1, OCT * S5_STATE)
    tail = S5_CHUNK * LANES - kk
    return _s5(u2, bre[:, tail:], bim[:, tail:], cre[:, :kk], cim[:, :kk], zre, zim, d.reshape(no, 1, LANES),
               x0, are, aim, nseq, nchunks, out_dtype)


def _s5_state_in(re, im, no):
    nseq = re.shape[0]
    f = lambda a: a.reshape(nseq, no, OCT * S5_STATE).transpose(1, 0, 2)
    return jnp.concatenate([f(re), f(im)], axis=-1)


def _s5_state_out(xf):
    no, nseq, _ = xf.shape
    hw = OCT * S5_STATE
    f = lambda a: a.transpose(1, 0, 2).reshape(1, nseq, no * OCT, S5_STATE)
    return f(xf[:, :, :hw]), f(xf[:, :, hw:])


def _gla_kernel(q_ref, k_ref, v_ref, la_ref, gb_ref, s0_ref, ng_ref, o_ref, sf_ref, st_ref,
                *, nseq, nchunks, c):
    i = pl.program_id(1)

    @pl.when(i == 0)
    def _():
        st_ref[...] = s0_ref[...]

    nh, hk, hv = GLA_HEADS, GLA_HEAD_K, GLA_HEAD_V
    nch = nseq * nchunks
    tm = nch * c
    iota = lambda shape, d: lax.broadcasted_iota(jnp.int32, shape, d)
    ng = ng_ref[...]

    tb = min(tm, MXU_TILE)
    rt, ct = iota((tb, tb), 0), iota((tb, tb), 1)
    tril = ((rt // c == ct // c) & (rt >= ct)).astype(BF16)
    parts = []
    for r0 in range(0, tm, tb):
        la_hi, la_lo = _split_bf16(la_ref[r0:r0 + tb, :])
        parts.append(_dot(tril, la_hi) + _dot(tril, la_lo))
    bcum = jnp.concatenate(parts, axis=0)
    e_hi, e_lo = _split_bf16(jnp.exp(jnp.concatenate([bcum[(ci + 1) * c - 1:(ci + 1) * c] for ci in range(nch)],
                                                     axis=0)))
    pick = (iota((nch, LANES), 0) == iota((nch, LANES), 1)).astype(BF16)
    dec_t = _dot_tn(e_hi, pick) + _dot_tn(e_lo, pick)
    q_all = q_ref[...].astype(F32)
    k_all = k_ref[...].astype(F32)
    q_dec_all = q_all * jnp.exp(bcum)
    k_dec_all = k_all * jnp.exp(-bcum)

    own_k = iota((nh * c, nh * hk), 0) // c == iota((nh * c, nh * hk), 1) // hk
    own_v = iota((nh * c, nh * hv), 0) // c == iota((nh * c, nh * hv), 1) // hv
    causal = iota((c, nh * c), 1) % c <= iota((c, nh * c), 0)
    zero_v = jnp.zeros((hk, hv), F32)

    def intra(ci):
        rows = slice(ci * c, (ci + 1) * c)
        q_dec = q_dec_all[rows].astype(BF16)
        k_dec = k_dec_all[rows]
        b_c = bcum[rows]
        k_tail = (k_all[rows] * jnp.exp(b_c[c - 1:c] - b_c)).astype(BF16)
        v = v_ref[rows, :].astype(F32)
        k_bd = jnp.where(own_k, jnp.concatenate([k_dec] * nh, axis=0), 0.0).astype(BF16)
        v_bd = jnp.where(own_v, jnp.concatenate([v] * nh, axis=0), 0.0).astype(BF16)
        att = jnp.where(causal, _dot_nt(q_dec, k_bd), 0.0)
        o_intra = _dot(att.astype(BF16), v_bd)
        vb = v.astype(BF16)
        kvs = []
        for h0 in range(0, nh, 2):
            kv2 = _dot_tn(k_tail[:, h0 * hk:(h0 + 2) * hk], vb[:, h0 * hv:(h0 + 2) * hv])
            kvs += [kv2[:hk, :hv], kv2[hk:, hv:]]
        return q_dec, o_intra, kvs

    def carry(ci, sts, q_dec, o_intra, kvs):
        st_bd = jnp.concatenate(
            [jnp.concatenate([zero_v] * h + [sts[h]] + [zero_v] * (nh - 1 - h), axis=1) for h in range(nh)],
            axis=0).astype(BF16)
        o = o_intra + _dot(q_dec, st_bd)
        new = [dec_t[h * hk:(h + 1) * hk, ci:ci + 1] * sts[h] + kvs[h] for h in range(nh)]
        return o, new

    def finish(ci, o):
        rows = slice(ci * c, (ci + 1) * c)
        for h in range(nh):
            vs = slice(h * hv, (h + 1) * hv)
            oh = _rms_rows(o[:, vs], ng) * gb_ref[rows, vs].astype(F32)
            o_ref[rows, vs] = oh.astype(o_ref.dtype)

    sts = None
    pending = intra(0)
    unfinished = None
    for ci in range(nch):
        s, first, last = ci // nchunks, ci % nchunks == 0, ci % nchunks == nchunks - 1
        current = pending
        if ci + 1 < nch:
            pending = intra(ci + 1)
        if first:
            sts = [st_ref[s, h] for h in range(nh)]
        o, sts = carry(ci, sts, *current)
        if last:
            for h in range(nh):
                st_ref[s, h] = sts[h]
        if unfinished is not None:
            finish(*unfinished)
        unfinished = (ci, o)
    finish(*unfinished)

    @pl.when(i == pl.num_programs(1) - 1)
    def _():
        sf_ref[...] = st_ref[...]


def _gla(q, k, v, la, gb, s0, ng, nseq_total, seqlen, out_dtype):
    c = math.gcd(seqlen, GLA_CHUNK)
    per_seq = seqlen // c
    if per_seq >= 8:
        nseq, nchunks = 1, next(n for n in (32, 16, 8, 1) if per_seq % n == 0)
    else:
        nseq, nchunks = 16, seqlen // c
    assert nseq_total % nseq == 0 and seqlen % (nchunks * c) == 0
    nblk = seqlen // (nchunks * c)
    tm = nseq * nchunks * c
    row = lambda n: pl.BlockSpec((tm, n), lambda b, i: (b * nblk + i, 0))
    st_spec = pl.BlockSpec((nseq, GLA_HEADS, GLA_HEAD_K, GLA_HEAD_V), lambda b, i: (b, 0, 0, 0))
    m = q.shape[0]
    return pl.pallas_call(
        functools.partial(_gla_kernel, nseq=nseq, nchunks=nchunks, c=c),
        grid=(nseq_total // nseq, nblk),
        in_specs=[row(GLA_K_WIDTH), row(GLA_K_WIDTH), row(GLA_V_WIDTH), row(GLA_K_WIDTH), row(GLA_V_WIDTH),
                  st_spec, _full(ng.shape)],
        out_specs=[row(GLA_V_WIDTH), st_spec],
        out_shape=[jax.ShapeDtypeStruct((m, GLA_V_WIDTH), out_dtype), jax.ShapeDtypeStruct(s0.shape, F32)],
        scratch_shapes=[pltpu.VMEM((nseq, GLA_HEADS, GLA_HEAD_K, GLA_HEAD_V), F32)],
        compiler_params=_cparams(("parallel", "arbitrary")),
        name="gla",
    )(q, k, v, la, gb, s0, ng)


def _gelu_tanh(x):
    return 0.5 * x * (1.0 + jnp.tanh(math.sqrt(2.0 / math.pi) * (x + 0.044715 * (x * x * x))))


def _head_ones(n):
    r = lax.broadcasted_iota(jnp.int32, (n, n), 0) // SWA_HEAD_DIM
    c = lax.broadcasted_iota(jnp.int32, (n, n), 1) // SWA_HEAD_DIM
    return (r == c).astype(BF16)


def _rope_block(x, cos_t, sin_t, upper):
    half = SWA_HEAD_DIM // 2
    swapped = jnp.where(upper, pltpu.roll(x, half, 1), pltpu.roll(x, LANES - half, 1))
    return x * cos_t + swapped * sin_t


def _mid_kernel(x_ref, y_ref, ga_ref, ob_ref, wglu_ref, bglu_ref, wo_ref,
                g_ref, w_ref, gq_ref, gk_ref, cos_ref, sin_ref, h_ref, q_ref, k_ref, v_ref, gate_ref):
    tm = x_ref.shape[0]
    lane = lax.broadcasted_iota(jnp.int32, (tm, LANES), 1)
    upper = (lane & (SWA_HEAD_DIM // 2)) != 0
    ones4 = _head_ones(MXU_TILE)
    inv_d = 1.0 / SWA_HEAD_DIM
    qw = SWA_HEADS * SWA_HEAD_DIM
    kw = SWA_KV_HEADS * SWA_HEAD_DIM
    nblk = qw // MXU_TILE
    no = y_ref.shape[0]

    na = ga_ref.shape[1]
    mix_b = _dot(ob_ref[...].astype(BF16), wo_ref[na:, :])
    z = _gelu_tanh(jnp.concatenate([y_ref[o].astype(F32) for o in range(no)], axis=1))
    z = z * _sigmoid(_dot(z.astype(BF16), wglu_ref[...]) + bglu_ref[...])
    out_a = z * ga_ref[...].astype(F32)
    h = x_ref[...] + (_dot(out_a.astype(BF16), wo_ref[:na, :]) + mix_b)
    h_ref[...] = h

    xb = _rms_rows(h, g_ref[...]).astype(BF16)
    cos_t = cos_ref[...]
    sin_t = sin_ref[...]

    def gate_finish(gate):
        gate_ref[...] = _silu(gate).astype(gate_ref.dtype)

    def q_finish(j, q):
        ss = _dot((q * q).astype(BF16), ones4)
        qn = q * lax.rsqrt(ss * inv_d + NORM_EPS) * gq_ref[...]
        for e in range(MXU_TILE // LANES):
            cols = slice(j * MXU_TILE + e * LANES, j * MXU_TILE + (e + 1) * LANES)
            qe = _rope_block(qn[:, e * LANES:(e + 1) * LANES], cos_t, sin_t, upper)
            q_ref[:, cols] = (qe * (SWA_HEAD_DIM ** -0.5 * LOG2E)).astype(q_ref.dtype)

    def kv_finish(kv):
        k = kv[:, :kw]
        ss = _dot((k * k).astype(BF16), ones4[:kw, :kw])
        kn = k * lax.rsqrt(ss * inv_d + NORM_EPS) * gk_ref[...]
        k_ref[...] = _rope_block(kn, cos_t, sin_t, upper)
        v_ref[...] = kv[:, kw:]

    work = [(slice(j * MXU_TILE, (j + 1) * MXU_TILE), functools.partial(q_finish, j)) for j in range(nblk)]
    work += [(slice(qw, qw + 2 * kw), kv_finish)]
    work += [(slice(qw + 2 * kw, None), gate_finish)]
    pending = _dot(xb, w_ref[:, work[0][0]])
    for n, (_, finish) in enumerate(work):
        current = pending
        if n + 1 < len(work):
            pending = _dot(xb, w_ref[:, work[n + 1][0]])
        finish(current)


def _mid(x, y, ga, ob, pe, po, cos_t, sin_t, act_dtype):
    m, dm = x.shape
    tm = _row_tile(m)
    row = lambda n: pl.BlockSpec((tm, n), lambda i: (i, 0))
    slab = pl.BlockSpec((y.shape[0], tm, LANES), lambda i: (0, i, 0))
    assert cos_t.shape[0] % tm == 0
    nper = cos_t.shape[0] // tm
    tab = pl.BlockSpec((tm, LANES), lambda i: (i % nper, 0))
    qw = SWA_HEADS * SWA_HEAD_DIM
    kw = SWA_KV_HEADS * SWA_HEAD_DIM
    weights = [pe['wglu'], pe['bglu'], pe['wo'], po['norm_g'], po['w_in'], po['gq'], po['gk']]
    return pl.pallas_call(
        _mid_kernel,
        grid=(m // tm,),
        in_specs=[row(dm), slab, row(S5_WIDTH), row(GLA_V_WIDTH)] + [_full(w.shape) for w in weights] + [tab, tab],
        out_specs=[row(dm), row(qw), row(kw), row(kw), row(qw)],
        out_shape=[jax.ShapeDtypeStruct((m, dm), F32),
                   jax.ShapeDtypeStruct((m, qw), act_dtype), jax.ShapeDtypeStruct((m, kw), F32),
                   jax.ShapeDtypeStruct((m, kw), F32), jax.ShapeDtypeStruct((m, qw), act_dtype)],
        compiler_params=_cparams(("parallel",)),
        name="mid",
    )(x, y, ga, ob, *weights, cos_t, sin_t)


def _rope_tables(pos):
    half = SWA_HEAD_DIM // 2
    inv_freq = ROPE_THETA ** (-np.arange(half, dtype=np.float64) / half)
    ang = np.asarray(pos, np.float64)[:, None] * inv_freq[None, :]
    cos, sin = np.cos(ang), np.sin(ang)
    cos_t = np.concatenate([cos, cos, cos, cos], axis=1).astype(np.float32)
    sin_t = np.concatenate([-sin, sin, -sin, sin], axis=1).astype(np.float32)
    return jnp.asarray(cos_t), jnp.asarray(sin_t)


def _attn_prompt_kernel(sink_ref, q_ref, kc_ref, kp_ref, vc_ref, vp_ref, gate_ref, h_ref, wout_ref,
                        y_ref, p_ref, o_ref, *, nqb):
    i = pl.program_id(1)
    w = SWA_WINDOW
    hd = SWA_HEAD_DIM
    npair = SWA_GROUP // 2
    lane2 = lax.broadcasted_iota(jnp.int32, (2 * w, LANES), 1)
    low = lane2 < hd
    rr = lax.broadcasted_iota(jnp.int32, (w, w), 0)
    cc = lax.broadcasted_iota(jnp.int32, (w, w), 1)
    tri = cc <= rr
    low_w = cc < hd
    r4 = lax.broadcasted_iota(jnp.int32, (4 * w, LANES), 0)
    c4 = lax.broadcasted_iota(jnp.int32, (4 * w, LANES), 1)
    den_cols = ((r4 < 2 * w) == (c4 < hd)).astype(BF16)
    units = [(jb, kv) for jb in range(nqb) for kv in range(SWA_KV_HEADS)]

    def scores(jb, kv):
        rows = slice(jb * w, (jb + 1) * w)
        if jb == 0:
            k_prev, v_prev = kp_ref[...], vp_ref[...]
        else:
            prev_rows = slice((jb - 1) * w, jb * w)
            k_prev, v_prev = kc_ref[prev_rows, :], vc_ref[prev_rows, :]
        kcat = jnp.concatenate([k_prev, kc_ref[rows, :]], axis=0)
        vcat = jnp.concatenate([v_prev, vc_ref[rows, :]], axis=0)
        own = low if kv == 0 else jnp.logical_not(low)
        k_own = jnp.where(own, kcat, 0.0)
        v_own = jnp.where(own, vcat, 0.0)
        k_oth = pltpu.roll(k_own, hd, 1)
        v_oth = pltpu.roll(v_own, hd, 1)
        k_lo, k_hi = (k_own, k_oth) if kv == 0 else (k_oth, k_own)
        v_lo, v_hi = (v_own, v_oth) if kv == 0 else (v_oth, v_own)
        k_rhs = jnp.concatenate([k_lo, k_hi], axis=0).astype(BF16)
        v_rhs = jnp.concatenate([jnp.concatenate([v_lo, v_hi], axis=0).astype(BF16), den_cols], axis=1)
        qs = jnp.concatenate([q_ref[rows, (kv * npair + pr) * LANES:(kv * npair + pr + 1) * LANES]
                              for pr in range(npair)], axis=0)
        return _dot_nt(qs, k_rhs), v_rhs

    def softmax_pv(kv, first, buf, s_all, v_rhs):
        sink_terms = []
        for pr in range(npair):
            prow = slice(pr * w, (pr + 1) * w)
            pair_terms = []
            for e in range(2):
                s_prev = s_all[prow, (2 * e) * w:(2 * e + 1) * w]
                s_cur = s_all[prow, (2 * e + 1) * w:(2 * e + 2) * w]
                if first:
                    s_prev = jnp.where(i > 0, s_prev, -jnp.inf)
                sc = jnp.where(tri, s_cur, s_prev)
                sink = sink_ref[2 * (kv * npair + pr) + e] * LOG2E
                mx = jnp.maximum(jnp.max(sc, axis=-1, keepdims=True), sink)
                pe = jnp.exp2(sc - mx)
                p_ref[buf, prow, (2 * e) * w:(2 * e + 1) * w] = jnp.where(tri, 0.0, pe).astype(BF16)
                p_ref[buf, prow, (2 * e + 1) * w:(2 * e + 2) * w] = jnp.where(tri, pe, 0.0).astype(BF16)
                pair_terms.append(jnp.exp2(sink - mx))
            sink_terms.append(pair_terms)
        return _dot(p_ref[buf], v_rhs), sink_terms

    def normalise(jb, kv, o_ext, sink_terms):
        rows = slice(jb * w, (jb + 1) * w)
        for pr in range(npair):
            prow = slice(pr * w, (pr + 1) * w)
            cols = slice((kv * npair + pr) * LANES, (kv * npair + pr + 1) * LANES)
            st = jnp.where(low_w, sink_terms[pr][0], sink_terms[pr][1])
            o = o_ext[prow, :LANES] / (o_ext[prow, LANES:] + st)
            o_ref[rows, cols] = o.astype(o_ref.dtype)

    def project(rows):
        og = o_ref[rows, :].astype(F32) * gate_ref[rows, :].astype(F32)
        y_ref[rows, :] = h_ref[rows, :] + _dot(og.astype(BF16), wout_ref[...])

    group = 2
    pending = scores(*units[0])
    unfinished = None
    for n, (jb, kv) in enumerate(units):
        current = pending
        if n + 1 < len(units):
            pending = scores(*units[n + 1])
        result = softmax_pv(kv, jb == 0, n % 2, *current)
        if unfinished is not None:
            normalise(*unfinished)
            done_jb, done_kv = unfinished[:2]
            if done_kv == SWA_KV_HEADS - 1 and (done_jb + 1) % group == 0:
                project(slice((done_jb + 1 - group) * w, (done_jb + 1) * w))
        unfinished = (jb, kv) + result
    normalise(*unfinished)
    project(slice((nqb - group) * w, nqb * w))


def _attn_prompt(sinks, q, k, v, gate, h, w_out, nseq, seqlen):
    w = SWA_WINDOW
    nqb = 8
    tm = nqb * w
    assert seqlen % tm == 0
    nblk = seqlen // tm
    qw = q.shape[1]
    dm = h.shape[1]
    row = lambda n: pl.BlockSpec((tm, n), lambda b, i: (b * nblk + i, 0))
    prev = lambda n: pl.BlockSpec((w, n), lambda b, i: (jnp.maximum((b * nblk + i) * nqb - 1, 0), 0))
    smem = pl.BlockSpec(memory_space=pltpu.SMEM)
    return pl.pallas_call(
        functools.partial(_attn_prompt_kernel, nqb=nqb),
        grid=(nseq, nblk),
        in_specs=[smem, row(qw), row(LANES), prev(LANES), row(LANES), prev(LANES), row(qw), row(dm),
                  _full(w_out.shape)],
        out_specs=row(dm),
        out_shape=jax.ShapeDtypeStruct(h.shape, F32),
        scratch_shapes=[pltpu.VMEM((2, 4 * w, 4 * w), BF16), pltpu.VMEM((tm, qw), BF16)],
        compiler_params=_cparams(("parallel", "arbitrary")),
        name="attn_prompt",
    )(sinks, q, k, k, v, v, gate, h, w_out)


def _attn_sample_kernel(sink_ref, q_ref, kn_ref, vn_ref, kc_ref, vc_ref, o_ref, ko_ref, vo_ref, q2_ref, *, seqlen):
    nq = SWA_HEADS * seqlen
    hd = SWA_HEAD_DIM
    heads = [(kv, g) for kv in range(SWA_KV_HEADS) for g in range(SWA_GROUP)]

    def stacked(kv, g):
        r0 = (kv * SWA_GROUP + g) * seqlen
        return slice(r0, r0 + seqlen), slice(kv * hd, (kv + 1) * hd)

    q2_ref[...] = jnp.zeros(q2_ref.shape, q2_ref.dtype)
    for kv, g in heads:
        rows, lanes = stacked(kv, g)
        h = kv * SWA_GROUP + g
        q2_ref[:, rows, lanes] = q_ref[:, :, h * hd:(h + 1) * hd]
    ncache = kc_ref.shape[1]
    t_row = lax.broadcasted_iota(jnp.int32, (nq, ncache), 0) % seqlen
    c_col = lax.broadcasted_iota(jnp.int32, (nq, ncache), 1)
    cache_ok = c_col > t_row - (SWA_WINDOW - ncache)
    nnew = kn_ref.shape[1]
    t_row_n = lax.broadcasted_iota(jnp.int32, (nq, nnew), 0) % seqlen
    n_col = lax.broadcasted_iota(jnp.int32, (nq, nnew), 1)
    new_ok = n_col <= t_row_n
    sink = (sink_ref[...] * LOG2E)[None, :, 0:1]
    bqk = lambda a, b: lax.dot_general(a, b, (((2,), (2,)), ((0,), (0,))), preferred_element_type=F32)
    bpv = lambda a, b: lax.dot_general(a, b, (((2,), (1,)), ((0,), (0,))), preferred_element_type=F32)
    q = q2_ref[...].astype(BF16)
    kc = kc_ref[...]
    vc = vc_ref[...]
    kn = kn_ref[...]
    vn = vn_ref[...]
    sc = jnp.where(cache_ok[None], bqk(q, kc.astype(BF16)), -jnp.inf)
    sn = jnp.where(new_ok[None], bqk(q, kn.astype(BF16)), -jnp.inf)
    mx = jnp.maximum(jnp.maximum(jnp.max(sc, axis=-1, keepdims=True), jnp.max(sn, axis=-1, keepdims=True)), sink)
    pc = jnp.exp2(sc - mx)
    pn = jnp.exp2(sn - mx)
    den = jnp.sum(pc, axis=-1, keepdims=True) + jnp.sum(pn, axis=-1, keepdims=True) + jnp.exp2(sink - mx)
    inv = 1.0 / den
    o2 = bpv((pc * inv).astype(BF16), vc.astype(BF16)) + bpv((pn * inv).astype(BF16), vn.astype(BF16))
    for kv, g in heads:
        rows, lanes = stacked(kv, g)
        h = kv * SWA_GROUP + g
        o_ref[:, :, h * hd:(h + 1) * hd] = o2[:, rows, lanes]
    keep = ncache - seqlen
    ko_ref[:, 0:keep, :] = kc[:, seqlen:ncache, :]
    ko_ref[:, keep:ncache, :] = kn[:, 0:seqlen, :]
    vo_ref[:, 0:keep, :] = vc[:, seqlen:ncache, :]
    vo_ref[:, keep:ncache, :] = vn[:, 0:seqlen, :]


def _attn_sample(sink_rows, q, kn, vn, kc, vc):
    n, seqlen, _ = q.shape
    nseq = 16
    assert n % nseq == 0
    blk = lambda a: pl.BlockSpec((nseq,) + a.shape[1:], lambda i: (i, 0, 0))
    return pl.pallas_call(
        functools.partial(_attn_sample_kernel, seqlen=seqlen),
        grid=(n // nseq,),
        in_specs=[_full(sink_rows.shape), blk(q), blk(kn), blk(vn), blk(kc), blk(vc)],
        out_specs=[blk(q), blk(kc), blk(vc)],
        out_shape=[jax.ShapeDtypeStruct(q.shape, F32), jax.ShapeDtypeStruct(kc.shape, F32),
                   jax.ShapeDtypeStruct(vc.shape, F32)],
        scratch_shapes=[pltpu.VMEM((nseq, SWA_HEADS * seqlen, SWA_KV_HEADS * SWA_HEAD_DIM), F32)],
        compiler_params=_cparams(("parallel",)),
        name="attn_sample",
    )(sink_rows, q, kn, vn, kc, vc)


def _odd_out_kernel(h_ref, o_ref, gate_ref, w_ref, y_ref):
    og = o_ref[...].astype(F32) * gate_ref[...].astype(F32)
    y_ref[...] = h_ref[...] + _dot(og.astype(BF16), w_ref[...])


def _odd_out(h, o, gate, w):
    m, dm = h.shape
    tm = _row_tile(m)
    row = lambda n: pl.BlockSpec((tm, n), lambda i: (i, 0))
    return pl.pallas_call(
        _odd_out_kernel,
        grid=(m // tm,),
        in_specs=[row(dm), row(o.shape[1]), row(gate.shape[1]), _full(w.shape)],
        out_specs=row(dm),
        out_shape=jax.ShapeDtypeStruct((m, dm), F32),
        compiler_params=_cparams(("parallel",)),
        name="odd_out",
    )(h, o, gate, w)


def _trunk(x, s5_x0, gla_s0, pe, po, pos, nseq, seqlen, act_dtype):
    u2, ga, q, k, v, la, gb = _even_in(x, pe['norm_g'], pe['w_in'], pe['wgate'], pe['bgate'], act_dtype,
                                       _s5_chunk(seqlen))
    y4, s5_fin = _s5_branch(u2, s5_x0, pe['s5_ops'], pe['d'], nseq, seqlen, act_dtype)
    ob, gla_fin = _gla(q, k, v, la, gb, gla_s0, pe['gla_norm_g'], nseq, seqlen, act_dtype)
    cos_t, sin_t = _rope_tables(pos)
    h, q1, k1, v1, gate = _mid(x, y4, ga, ob, pe, po, cos_t, sin_t, act_dtype)
    return h, q1, k1, v1, gate, s5_fin, gla_fin


def kernel(x_prompt, x_sample, state_s5_re, state_s5_im, state_gla, cache_swa_k, cache_swa_v,
           even_norm_g, even_w_in, s5_lambda_re, s5_lambda_im, s5_log_dt, s5_b_re, s5_b_im,
           s5_c_re, s5_c_im, s5_d, s5_w_glu, s5_b_glu, gla_w_gate, gla_b_gate, gla_norm_g,
           even_w_out, odd_norm_g, odd_w_in, swa_q_norm_g, swa_k_norm_g, swa_sinks, odd_w_out):
    nb, seq, dm = x_prompt.shape
    ns, dseq, _ = x_sample.shape
    ng = s5_lambda_re.shape[1]
    no = ng // OCT
    xp = x_prompt.reshape(nb * seq, dm)
    xs = x_sample.reshape(ns * dseq, dm)

    i = 0
    w_in = even_w_in[i]
    assert ng * S5_GROUP == S5_WIDTH
    col_alow = C_CODE
    col_gb = col_alow + GLA_GATE_RANK
    pad_rank = LANES - GLA_GATE_RANK
    pe = {
        'norm_g': even_norm_g[i][None, :],
        'w_in': jnp.concatenate([w_in[:, :col_alow], jnp.pad(w_in[:, col_alow:col_gb], ((0, 0), (0, pad_rank))),
                                 w_in[:, col_gb:]], axis=1).astype(BF16),
        'wgate': jnp.pad(gla_w_gate[i], ((0, pad_rank), (0, 0))).astype(BF16),
        'bgate': gla_b_gate[i][None, :],
        's5_ops': _s5_params(s5_lambda_re[i], s5_lambda_im[i], s5_log_dt[i], s5_b_re[i], s5_b_im[i],
                             s5_c_re[i], s5_c_im[i]),
        'gla_norm_g': gla_norm_g[i][None, :],
        'd': s5_d[i],
        'wglu': s5_w_glu[i].astype(BF16),
        'bglu': s5_b_glu[i][None, :],
        'wo': even_w_out[i].astype(BF16),
    }
    po = {
        'norm_g': odd_norm_g[i][None, :],
        'w_in': odd_w_in[i].astype(BF16),
        'gq': jnp.tile(swa_q_norm_g[i], MXU_TILE // SWA_HEAD_DIM)[None, :],
        'gk': jnp.tile(swa_k_norm_g[i], LANES // SWA_HEAD_DIM)[None, :],
    }
    w_out = odd_w_out[i].astype(BF16)
    sinks = swa_sinks[i]
    kvw = SWA_KV_HEADS * SWA_HEAD_DIM

    s5_zero = jnp.zeros((no, nb, 2 * OCT * S5_STATE), F32)
    gla_zero = jnp.zeros((nb, GLA_HEADS, GLA_HEAD_K, GLA_HEAD_V), F32)
    hp, q, k, v, gate, s5_p, gla_p = _trunk(xp, s5_zero, gla_zero, pe, po, np.arange(seq), nb, seq, BF16)
    s5r_p, s5i_p = _s5_state_out(s5_p)
    y_prompt = _attn_prompt(sinks, q, k, v, gate, hp, w_out, nb, seq).reshape(nb, seq, dm)
    cache_len = min(SWA_WINDOW, seq)
    tail = lambda a: (a.reshape(nb, seq, kvw)[:, seq - cache_len:]
                      .reshape(1, nb, cache_len, SWA_KV_HEADS, SWA_HEAD_DIM))
    swk_p, swv_p = tail(k), tail(v)

    pos_s = np.tile(PAST_LEN + np.arange(dseq), ns)
    s5_init = _s5_state_in(state_s5_re[i], state_s5_im[i], no)
    hs, q, k, v, gate, s5_s, gla_s = _trunk(xs, s5_init, state_gla[i], pe, po, pos_s, ns, dseq, F32)
    s5r_s, s5i_s = _s5_state_out(s5_s)
    ncache = cache_swa_k.shape[2]
    sink_rows = jnp.broadcast_to(jnp.repeat(sinks, dseq)[:, None], (SWA_HEADS * dseq, LANES))
    kn = k.reshape(ns, dseq, kvw)
    vn = v.reshape(ns, dseq, kvw)
    npad = BF16_ROWS - dseq
    kn_pad = jnp.pad(kn, ((0, 0), (0, npad), (0, 0)))
    vn_pad = jnp.pad(vn, ((0, 0), (0, npad), (0, 0)))
    kc = cache_swa_k[i].reshape(ns, ncache, kvw)
    vc = cache_swa_v[i].reshape(ns, ncache, kvw)
    o, kc_new, vc_new = _attn_sample(sink_rows, q.reshape(ns, dseq, -1), kn_pad, vn_pad, kc, vc)
    y_sample = _odd_out(hs, o.reshape(ns * dseq, -1), gate, w_out).reshape(ns, dseq, dm)
    swk_s = kc_new.reshape(1, ns, ncache, SWA_KV_HEADS, SWA_HEAD_DIM)
    swv_s = vc_new.reshape(1, ns, ncache, SWA_KV_HEADS, SWA_HEAD_DIM)

    return (y_prompt, y_sample,
            s5r_p, s5i_p, gla_p[None], swk_p, swv_p,
            s5r_s, s5i_s, gla_s[None], swk_s, swv_s)
```

```python
import functools
import math

import jax
import jax.numpy as jnp
import numpy as np
from jax import lax
from jax.experimental import pallas as pl
from jax.experimental.pallas import tpu as pltpu

F32 = jnp.float32
BF16 = jnp.bfloat16

PAST_LEN = 8192
NORM_EPS = 1e-6
S5_GROUP = 16
S5_STATE = 64
S5_CHUNK = 16
GLA_HEADS = 4
GLA_HEAD_K = 64
GLA_HEAD_V = 128
GLA_GATE_RANK = 16
GLA_GATE_TAU = 16.0
GLA_CHUNK = 64
S5_WIDTH = 512
GLA_K_WIDTH = GLA_HEADS * GLA_HEAD_K
GLA_V_WIDTH = GLA_HEADS * GLA_HEAD_V
SWA_HEADS = 16
SWA_KV_HEADS = 2
SWA_GROUP = SWA_HEADS // SWA_KV_HEADS
SWA_HEAD_DIM = 64
SWA_WINDOW = 128
ROPE_THETA = 10000.0
LOG2E = math.log2(math.e)
LANES = 128
BF16_ROWS = 16
MXU_TILE = 256
OCT = LANES // S5_GROUP
ROW_TILE = 512
VMEM_LIMIT = 48 * 1024 * 1024
C_U = 0
C_GATE_A = C_U + S5_WIDTH
C_Q = C_GATE_A + S5_WIDTH
C_K = C_Q + GLA_K_WIDTH
C_V = C_K + GLA_K_WIDTH
C_CODE = C_V + GLA_V_WIDTH
C_GATE_B = C_CODE + LANES
C_END = C_GATE_B + GLA_V_WIDTH


def _cparams(sem):
    return pltpu.CompilerParams(dimension_semantics=sem, vmem_limit_bytes=VMEM_LIMIT)


def _full(shape):
    n = len(shape)
    return pl.BlockSpec(shape, lambda *_: (0,) * n)


def _dot(a, b):
    return jnp.dot(a, b, preferred_element_type=F32)


def _dot_nt(a, b):
    return lax.dot_general(a, b, (((1,), (1,)), ((), ())), preferred_element_type=F32)


def _dot_tn(a, b):
    return lax.dot_general(a, b, (((0,), (0,)), ((), ())), preferred_element_type=F32)


def _split_bf16(x):
    hi = x.astype(BF16)
    lo = (x - hi.astype(F32)).astype(BF16)
    return hi, lo


def _rms_rows(x, g):
    return x * lax.rsqrt(jnp.mean(x * x, axis=-1, keepdims=True) + NORM_EPS) * g


def _sigmoid(x):
    return 1.0 / (1.0 + jnp.exp(-x))


def _silu(x):
    return x * _sigmoid(x)


def _row_tile(m, tile=ROW_TILE):
    return tile if m % tile == 0 else m


def _even_in_kernel(x_ref, g_ref, w_ref, wgate_ref, bgate_ref,
                    u2_ref, ga_ref, q_ref, k_ref, v_ref, la_ref, gb_ref, uscr_ref, *, t):
    xb = _rms_rows(x_ref[...], g_ref[...]).astype(BF16)

    def proj(lo, hi):
        return _dot(xb, w_ref[:, lo:hi])

    u = proj(C_U, C_GATE_A)
    nrow = u.shape[0] // t
    for o in range(u2_ref.shape[0]):
        uscr_ref[o] = u[:, o * LANES:(o + 1) * LANES]
        for tt in range(t):
            piece = uscr_ref[o, pl.ds(tt, nrow, stride=t), :]
            u2_ref[o, :, tt * LANES:(tt + 1) * LANES] = piece.astype(u2_ref.dtype)
    ga_ref[...] = _silu(proj(C_GATE_A, C_Q)).astype(ga_ref.dtype)
    q_ref[...] = (proj(C_Q, C_K) * (GLA_HEAD_K ** -0.5)).astype(q_ref.dtype)
    k_ref[...] = proj(C_K, C_V).astype(k_ref.dtype)
    v_ref[...] = proj(C_V, C_CODE).astype(v_ref.dtype)
    gb_ref[...] = _silu(proj(C_GATE_B, C_END)).astype(gb_ref.dtype)
    a_low = proj(C_CODE, C_GATE_B)
    logit = _dot(a_low.astype(BF16), wgate_ref[...]) + bgate_ref[...]
    log_sig = jnp.minimum(logit, 0.0) - jnp.log1p(jnp.exp(-jnp.abs(logit)))
    la_ref[...] = log_sig * (1.0 / GLA_GATE_TAU)


def _even_in(x, g, w, wgate, bgate, act_dtype, t):
    m, d = x.shape
    tm = _row_tile(m, 2 * ROW_TILE) if m > ROW_TILE else m
    row = lambda n: pl.BlockSpec((tm, n), lambda i: (i, 0))
    assert w.shape[1] == C_END
    no = S5_WIDTH // LANES
    chunk = pl.BlockSpec((no, tm // t, t * LANES), lambda i: (0, i, 0))
    outs = [(S5_WIDTH, act_dtype), (GLA_K_WIDTH, act_dtype), (GLA_K_WIDTH, act_dtype), (GLA_V_WIDTH, act_dtype),
            (GLA_K_WIDTH, F32), (GLA_V_WIDTH, act_dtype)]
    return pl.pallas_call(
        functools.partial(_even_in_kernel, t=t),
        grid=(m // tm,),
        in_specs=[row(d), _full(g.shape), _full(w.shape), _full(wgate.shape), _full(bgate.shape)],
        out_specs=[chunk] + [row(n) for n, _ in outs],
        out_shape=[jax.ShapeDtypeStruct((no, m // t, t * LANES), act_dtype)]
        + [jax.ShapeDtypeStruct((m, n), dt) for n, dt in outs],
        scratch_shapes=[pltpu.VMEM((no, tm, LANES), F32)],
        compiler_params=_cparams(("parallel",)),
        name="even_in",
    )(x, g, w, wgate, bgate)


def _group_mask(shape, row_span, col_span):
    rg = (lax.broadcasted_iota(jnp.int32, shape, 0) // row_span) % OCT
    cg = (lax.broadcasted_iota(jnp.int32, shape, 1) // col_span) % OCT
    return rg == cg


def _s5_state_kernel(u_ref, bre_ref, bim_ref, x0_ref, are_ref, aim_ref, xs_ref, xf_ref, loc_ref, *, nseq, nchunks):
    hw = OCT * S5_STATE
    kk = u_ref.shape[2]
    bp = jnp.concatenate([bre_ref[0]] * OCT + [bim_ref[0]] * OCT, axis=1)
    bp = jnp.where(_group_mask((kk, 2 * hw), S5_GROUP, S5_STATE), bp, 0.0).astype(BF16)
    loc_ref[...] = _dot(u_ref[0], bp)
    a_re = are_ref[0]
    a_im = aim_ref[0]
    if nchunks == 1:
        x0 = x0_ref[0]
        xr, xi = x0[:, :hw], x0[:, hw:]
        loc = loc_ref[...]
        xf_ref[0, :, :hw] = a_re * xr - a_im * xi + loc[:, :hw]
        xf_ref[0, :, hw:] = a_re * xi + a_im * xr + loc[:, hw:]
        xs_ref[0] = x0.astype(xs_ref.dtype)
    else:
        def body(j, carry):
            new = []
            for b in range(nseq):
                xr, xi = carry[b]
                row = pl.ds(b * nchunks + j, 1)
                lr = loc_ref[row, :hw]
                li = loc_ref[row, hw:]
                loc_ref[row, :hw] = xr
                loc_ref[row, hw:] = xi
                new.append((a_re * xr - a_im * xi + lr, a_re * xi + a_im * xr + li))
            return tuple(new)

        init = tuple((x0_ref[0, b:b + 1, :hw], x0_ref[0, b:b + 1, hw:]) for b in range(nseq))
        fin = lax.fori_loop(0, nchunks, body, init, unroll=4)
        for b in range(nseq):
            xf_ref[0, b:b + 1, :hw] = fin[b][0]
            xf_ref[0, b:b + 1, hw:] = fin[b][1]
        xs_ref[0] = loc_ref[...].astype(xs_ref.dtype)


def _s5_state(u2, bre, bim, x0, are, aim, nseq, nchunks):
    no, r, kk = u2.shape
    sw = x0.shape[2]
    blk = lambda a: pl.BlockSpec((1,) + a.shape[1:], lambda o: (o,) + (0,) * (a.ndim - 1))
    return pl.pallas_call(
        functools.partial(_s5_state_kernel, nseq=nseq, nchunks=nchunks),
        grid=(no,),
        in_specs=[blk(u2), blk(bre), blk(bim), blk(x0), blk(are), blk(aim)],
        out_specs=[pl.BlockSpec((1, r, sw), lambda o: (o, 0, 0)), blk(x0)],
        out_shape=[jax.ShapeDtypeStruct((no, r, sw), BF16), jax.ShapeDtypeStruct(x0.shape, F32)],
        scratch_shapes=[pltpu.VMEM((r, sw), F32)],
        compiler_params=_cparams(("parallel",)),
        name="s5_state",
    )(u2, bre, bim, x0, are, aim)


def _dot_nt_f32(a, b):
    a_hi, a_lo = _split_bf16(a)
    b_hi, b_lo = _split_bf16(b)
    return _dot_nt(a_hi, b_hi) + _dot_nt(a_hi, b_lo) + _dot_nt(a_lo, b_hi)


def _s5_out_kernel(u_ref, xs_ref, bre_ref, bim_ref, cre_ref, cim_ref, zre_ref, zim_ref, d_ref, y_ref, yscr_ref):
    r, kk = u_ref.shape[1], u_ref.shape[2]
    t = kk // LANES
    sw = xs_ref.shape[2]
    ntile = kk // MXU_TILE
    taps = _dot_nt_f32(bre_ref[0], zre_ref[0]) - _dot_nt_f32(bim_ref[0], zim_ref[0])
    tmask = _group_mask((LANES, LANES), S5_GROUP, S5_GROUP)

    def tap(lag):
        if lag < 0:
            return jnp.zeros((LANES, LANES), F32)
        s = t - 1 - lag
        return jnp.where(tmask, taps[s * LANES:(s + 1) * LANES], 0.0)

    wts = [jnp.concatenate([jnp.concatenate([tap(2 * d), tap(2 * d + 1)], axis=1),
                            jnp.concatenate([tap(2 * d - 1), tap(2 * d)], axis=1)], axis=0).astype(BF16)
           for d in range(ntile)]
    cpt = jnp.concatenate([cre_ref[0]] * OCT + [cim_ref[0]] * OCT, axis=1)
    cpt = jnp.where(_group_mask((kk, sw), S5_GROUP, S5_STATE), cpt, 0.0).astype(BF16)
    xs = xs_ref[0]
    skip = jnp.concatenate([d_ref[0]] * (MXU_TILE // LANES), axis=1)
    for n in range(ntile):
        cols = slice(n * MXU_TILE, (n + 1) * MXU_TILE)
        acc = _dot_nt(xs, cpt[cols]) + skip * u_ref[0, :, cols].astype(F32)
        for k in range(n + 1):
            acc = acc + _dot(u_ref[0, :, k * MXU_TILE:(k + 1) * MXU_TILE], wts[n - k])
        for e in range(MXU_TILE // LANES):
            yscr_ref[pl.ds(2 * n + e, r, stride=t), :] = acc[:, e * LANES:(e + 1) * LANES]
    y_ref[0] = yscr_ref[...].astype(y_ref.dtype)


def _s5_out(u2, xs, bre, bim, cre, cim, zre, zim, d, out_dtype):
    no, r, kk = u2.shape
    m = r * (kk // LANES)
    blk = lambda a: pl.BlockSpec((1,) + a.shape[1:], lambda o: (o,) + (0,) * (a.ndim - 1))
    ops = (u2, xs, bre, bim, cre, cim, zre, zim, d)
    return pl.pallas_call(
        _s5_out_kernel,
        grid=(no,),
        in_specs=[blk(a) for a in ops],
        out_specs=pl.BlockSpec((1, m, LANES), lambda o: (o, 0, 0)),
        out_shape=jax.ShapeDtypeStruct((no, m, LANES), out_dtype),
        scratch_shapes=[pltpu.VMEM((m, LANES), F32)],
        compiler_params=_cparams(("parallel",)),
        name="s5_out",
    )(*ops)


def _s5_params(lam_re, lam_im, log_dt, b_re, b_im, c_re, c_im):
    t = S5_CHUNK
    ng = lam_re.shape[0]
    no = ng // OCT
    dt = jnp.exp(log_dt)[:, None]
    a = lam_re * dt
    b = lam_im * dt
    n = jnp.arange(t + 1, dtype=F32)[None, :, None]
    mag = jnp.exp(n * a[:, None, :])
    pw_re = mag * jnp.cos(n * b[:, None, :])
    pw_im = mag * jnp.sin(n * b[:, None, :])
    em1_re = jnp.expm1(a) * jnp.cos(b) - 2.0 * jnp.sin(0.5 * b) ** 2
    em1_im = jnp.exp(a) * jnp.sin(b)
    den = lam_re * lam_re + lam_im * lam_im
    z_re = (em1_re * lam_re + em1_im * lam_im) / den
    z_im = (em1_im * lam_re - em1_re * lam_im) / den
    bb_re = z_re[..., None] * b_re - z_im[..., None] * b_im
    bb_im = z_re[..., None] * b_im + z_im[..., None] * b_re
    def rows(w):
        return w.reshape(no, OCT, t, S5_GROUP, S5_STATE).transpose(0, 2, 1, 3, 4).reshape(no, t * LANES, S5_STATE)

    bt_re = bb_re.transpose(0, 2, 1)[:, None]
    bt_im = bb_im.transpose(0, 2, 1)[:, None]
    r_re, r_im = pw_re[:, t - 1::-1, None, :], pw_im[:, t - 1::-1, None, :]
    bre = rows(r_re * bt_re - r_im * bt_im)
    bim = rows(r_re * bt_im + r_im * bt_re)
    o_re, o_im = pw_re[:, 1:, None, :], pw_im[:, 1:, None, :]
    cre = rows(c_re[:, None] * o_re - c_im[:, None] * o_im)
    cim = rows(-(c_re[:, None] * o_im + c_im[:, None] * o_re))
    return bre, bim, cre, cim, c_re.reshape(no, LANES, S5_STATE), c_im.reshape(no, LANES, S5_STATE), pw_re, pw_im


def _s5_chunk(seqlen):
    return math.gcd(seqlen, S5_CHUNK)


def _s5_branch(u2, x0, ops, d, nseq, seqlen, out_dtype):
    bre, bim, cre, cim, zre, zim, pw_re, pw_im = ops
    no = u2.shape[0]
    t = _s5_chunk(seqlen)
    kk = t * LANES
    nchunks = seqlen // t
    are = pw_re[:, t].reshape(no, 1, OCT * S5_STATE)
    aim = pw_im[:, t].reshape(no, 1, OCT * S5_STATE)
    tail = S5_CHUNK * LANES - kk
    bre, bim = bre[:, tail:], bim[:, tail:]
    xs, xf = _s5_state(u2, bre, bim, x0, are, aim, nseq, nchunks)
    return _s5_out(u2, xs, bre, bim, cre[:, :kk], cim[:, :kk], zre, zim, d.reshape(no, 1, LANES), out_dtype), xf


def _s5_state_in(re, im, no):
    nseq = re.shape[0]
    f = lambda a: a.reshape(nseq, no, OCT * S5_STATE).transpose(1, 0, 2)
    return jnp.concatenate([f(re), f(im)], axis=-1)


def _s5_state_out(xf):
    no, nseq, _ = xf.shape
    hw = OCT * S5_STATE
    f = lambda a: a.transpose(1, 0, 2).reshape(1, nseq, no * OCT, S5_STATE)
    return f(xf[:, :, :hw]), f(xf[:, :, hw:])


def _gla_kernel(q_ref, k_ref, v_ref, la_ref, gb_ref, s0_ref, ng_ref, o_ref, sf_ref, st_ref,
                *, nseq, nchunks, c):
    i = pl.program_id(1)

    @pl.when(i == 0)
    def _():
        st_ref[...] = s0_ref[...]

    nh, hk, hv = GLA_HEADS, GLA_HEAD_K, GLA_HEAD_V
    nch = nseq * nchunks
    tm = nch * c
    iota = lambda shape, d: lax.broadcasted_iota(jnp.int32, shape, d)
    ng = ng_ref[...]

    tb = min(tm, MXU_TILE)
    rt, ct = iota((tb, tb), 0), iota((tb, tb), 1)
    tril = ((rt // c == ct // c) & (rt >= ct)).astype(BF16)
    parts = []
    for r0 in range(0, tm, tb):
        la_hi, la_lo = _split_bf16(la_ref[r0:r0 + tb, :])
        parts.append(_dot(tril, la_hi) + _dot(tril, la_lo))
    bcum = jnp.concatenate(parts, axis=0)
    e_hi, e_lo = _split_bf16(jnp.exp(jnp.concatenate([bcum[(ci + 1) * c - 1:(ci + 1) * c] for ci in range(nch)],
                                                     axis=0)))
    pick = (iota((nch, LANES), 0) == iota((nch, LANES), 1)).astype(BF16)
    dec_t = _dot_tn(e_hi, pick) + _dot_tn(e_lo, pick)
    q_all = q_ref[...].astype(F32)
    k_all = k_ref[...].astype(F32)
    q_dec_all = q_all * jnp.exp(bcum)
    k_dec_all = k_all * jnp.exp(-bcum)

    own_k = iota((nh * c, nh * hk), 0) // c == iota((nh * c, nh * hk), 1) // hk
    own_v = iota((nh * c, nh * hv), 0) // c == iota((nh * c, nh * hv), 1) // hv
    causal = iota((c, nh * c), 1) % c <= iota((c, nh * c), 0)
    zero_v = jnp.zeros((hk, hv), F32)

    def intra(ci):
        rows = slice(ci * c, (ci + 1) * c)
        q_dec = q_dec_all[rows].astype(BF16)
        k_dec = k_dec_all[rows]
        b_c = bcum[rows]
        k_tail = (k_all[rows] * jnp.exp(b_c[c - 1:c] - b_c)).astype(BF16)
        v = v_ref[rows, :].astype(F32)
        k_bd = jnp.where(own_k, jnp.concatenate([k_dec] * nh, axis=0), 0.0).astype(BF16)
        v_bd = jnp.where(own_v, jnp.concatenate([v] * nh, axis=0), 0.0).astype(BF16)
        att = jnp.where(causal, _dot_nt(q_dec, k_bd), 0.0)
        o_intra = _dot(att.astype(BF16), v_bd)
        vb = v.astype(BF16)
        kvs = []
        for h0 in range(0, nh, 2):
            kv2 = _dot_tn(k_tail[:, h0 * hk:(h0 + 2) * hk], vb[:, h0 * hv:(h0 + 2) * hv])
            kvs += [kv2[:hk, :hv], kv2[hk:, hv:]]
        return q_dec, o_intra, kvs

    def carry(ci, sts, q_dec, o_intra, kvs):
        st_bd = jnp.concatenate(
            [jnp.concatenate([zero_v] * h + [sts[h]] + [zero_v] * (nh - 1 - h), axis=1) for h in range(nh)],
            axis=0).astype(BF16)
        o = o_intra + _dot(q_dec, st_bd)
        new = [dec_t[h * hk:(h + 1) * hk, ci:ci + 1] * sts[h] + kvs[h] for h in range(nh)]
        return o, new

    def finish(ci, o):
        rows = slice(ci * c, (ci + 1) * c)
        for h in range(nh):
            vs = slice(h * hv, (h + 1) * hv)
            oh = _rms_rows(o[:, vs], ng) * gb_ref[rows, vs].astype(F32)
            o_ref[rows, vs] = oh.astype(o_ref.dtype)

    sts = None
    pending = intra(0)
    unfinished = None
    for ci in range(nch):
        s, first, last = ci // nchunks, ci % nchunks == 0, ci % nchunks == nchunks - 1
        current = pending
        if ci + 1 < nch:
            pending = intra(ci + 1)
        if first:
            sts = [st_ref[s, h] for h in range(nh)]
        o, sts = carry(ci, sts, *current)
        if last:
            for h in range(nh):
                st_ref[s, h] = sts[h]
        if unfinished is not None:
            finish(*unfinished)
        unfinished = (ci, o)
    finish(*unfinished)

    @pl.when(i == pl.num_programs(1) - 1)
    def _():
        sf_ref[...] = st_ref[...]


def _gla(q, k, v, la, gb, s0, ng, nseq_total, seqlen, out_dtype):
    c = math.gcd(seqlen, GLA_CHUNK)
    per_seq = seqlen // c
    if per_seq >= 8:
        nseq, nchunks = 1, next(n for n in (32, 16, 8, 1) if per_seq % n == 0)
    else:
        nseq, nchunks = 16, seqlen // c
    assert nseq_total % nseq == 0 and seqlen % (nchunks * c) == 0
    nblk = seqlen // (nchunks * c)
    tm = nseq * nchunks * c
    row = lambda n: pl.BlockSpec((tm, n), lambda b, i: (b * nblk + i, 0))
    st_spec = pl.BlockSpec((nseq, GLA_HEADS, GLA_HEAD_K, GLA_HEAD_V), lambda b, i: (b, 0, 0, 0))
    m = q.shape[0]
    return pl.pallas_call(
        functools.partial(_gla_kernel, nseq=nseq, nchunks=nchunks, c=c),
        grid=(nseq_total // nseq, nblk),
        in_specs=[row(GLA_K_WIDTH), row(GLA_K_WIDTH), row(GLA_V_WIDTH), row(GLA_K_WIDTH), row(GLA_V_WIDTH),
                  st_spec, _full(ng.shape)],
        out_specs=[row(GLA_V_WIDTH), st_spec],
        out_shape=[jax.ShapeDtypeStruct((m, GLA_V_WIDTH), out_dtype), jax.ShapeDtypeStruct(s0.shape, F32)],
        scratch_shapes=[pltpu.VMEM((nseq, GLA_HEADS, GLA_HEAD_K, GLA_HEAD_V), F32)],
        compiler_params=_cparams(("parallel", "arbitrary")),
        name="gla",
    )(q, k, v, la, gb, s0, ng)


def _gelu_tanh(x):
    return 0.5 * x * (1.0 + jnp.tanh(math.sqrt(2.0 / math.pi) * (x + 0.044715 * (x * x * x))))


def _head_ones(n):
    r = lax.broadcasted_iota(jnp.int32, (n, n), 0) // SWA_HEAD_DIM
    c = lax.broadcasted_iota(jnp.int32, (n, n), 1) // SWA_HEAD_DIM
    return (r == c).astype(BF16)


def _rope_block(x, cos_t, sin_t, upper):
    half = SWA_HEAD_DIM // 2
    swapped = jnp.where(upper, pltpu.roll(x, half, 1), pltpu.roll(x, LANES - half, 1))
    return x * cos_t + swapped * sin_t


def _mid_kernel(x_ref, y_ref, ga_ref, ob_ref, wglu_ref, bglu_ref, wo_ref,
                g_ref, w_ref, gq_ref, gk_ref, cos_ref, sin_ref, h_ref, q_ref, k_ref, v_ref, gate_ref):
    tm = x_ref.shape[0]
    lane = lax.broadcasted_iota(jnp.int32, (tm, LANES), 1)
    upper = (lane & (SWA_HEAD_DIM // 2)) != 0
    ones4 = _head_ones(MXU_TILE)
    inv_d = 1.0 / SWA_HEAD_DIM
    qw = SWA_HEADS * SWA_HEAD_DIM
    kw = SWA_KV_HEADS * SWA_HEAD_DIM
    nblk = qw // MXU_TILE
    no = y_ref.shape[0]

    na = ga_ref.shape[1]
    mix_b = _dot(ob_ref[...].astype(BF16), wo_ref[na:, :])
    z = _gelu_tanh(jnp.concatenate([y_ref[o].astype(F32) for o in range(no)], axis=1))
    z = z * _sigmoid(_dot(z.astype(BF16), wglu_ref[...]) + bglu_ref[...])
    out_a = z * ga_ref[...].astype(F32)
    h = x_ref[...] + (_dot(out_a.astype(BF16), wo_ref[:na, :]) + mix_b)
    h_ref[...] = h

    xb = _rms_rows(h, g_ref[...]).astype(BF16)
    cos_t = cos_ref[...]
    sin_t = sin_ref[...]

    def gate_finish(gate):
        gate_ref[...] = _silu(gate).astype(gate_ref.dtype)

    def q_finish(j, q):
        ss = _dot((q * q).astype(BF16), ones4)
        qn = q * lax.rsqrt(ss * inv_d + NORM_EPS) * gq_ref[...]
        for e in range(MXU_TILE // LANES):
            cols = slice(j * MXU_TILE + e * LANES, j * MXU_TILE + (e + 1) * LANES)
            qe = _rope_block(qn[:, e * LANES:(e + 1) * LANES], cos_t, sin_t, upper)
            q_ref[:, cols] = (qe * (SWA_HEAD_DIM ** -0.5 * LOG2E)).astype(q_ref.dtype)

    def kv_finish(kv):
        k = kv[:, :kw]
        ss = _dot((k * k).astype(BF16), ones4[:kw, :kw])
        kn = k * lax.rsqrt(ss * inv_d + NORM_EPS) * gk_ref[...]
        k_ref[...] = _rope_block(kn, cos_t, sin_t, upper)
        v_ref[...] = kv[:, kw:]

    work = [(slice(j * MXU_TILE, (j + 1) * MXU_TILE), functools.partial(q_finish, j)) for j in range(nblk)]
    work += [(slice(qw, qw + 2 * kw), kv_finish)]
    work += [(slice(qw + 2 * kw, None), gate_finish)]
    pending = _dot(xb, w_ref[:, work[0][0]])
    for n, (_, finish) in enumerate(work):
        current = pending
        if n + 1 < len(work):
            pending = _dot(xb, w_ref[:, work[n + 1][0]])
        finish(current)


def _mid(x, y, ga, ob, pe, po, cos_t, sin_t, act_dtype):
    m, dm = x.shape
    tm = _row_tile(m, 2 * ROW_TILE) if m > ROW_TILE else m
    row = lambda n: pl.BlockSpec((tm, n), lambda i: (i, 0))
    slab = pl.BlockSpec((y.shape[0], tm, LANES), lambda i: (0, i, 0))
    assert cos_t.shape[0] % tm == 0
    nper = cos_t.shape[0] // tm
    tab = pl.BlockSpec((tm, LANES), lambda i: (i % nper, 0))
    qw = SWA_HEADS * SWA_HEAD_DIM
    kw = SWA_KV_HEADS * SWA_HEAD_DIM
    weights = [pe['wglu'], pe['bglu'], pe['wo'], po['norm_g'], po['w_in'], po['gq'], po['gk']]
    resident = lambda w: pl.BlockSpec(w.shape, lambda i: (0,) * w.ndim, pipeline_mode=pl.Buffered(1))
    return pl.pallas_call(
        _mid_kernel,
        grid=(m // tm,),
        in_specs=[row(dm), slab, row(S5_WIDTH), row(GLA_V_WIDTH)] + [resident(w) for w in weights] + [tab, tab],
        out_specs=[row(dm), row(qw), row(kw), row(kw), row(qw)],
        out_shape=[jax.ShapeDtypeStruct((m, dm), F32),
                   jax.ShapeDtypeStruct((m, qw), act_dtype), jax.ShapeDtypeStruct((m, kw), F32),
                   jax.ShapeDtypeStruct((m, kw), F32), jax.ShapeDtypeStruct((m, qw), act_dtype)],
        compiler_params=_cparams(("parallel",)),
        name="mid",
    )(x, y, ga, ob, *weights, cos_t, sin_t)


def _rope_tables(pos):
    half = SWA_HEAD_DIM // 2
    inv_freq = ROPE_THETA ** (-np.arange(half, dtype=np.float64) / half)
    ang = np.asarray(pos, np.float64)[:, None] * inv_freq[None, :]
    cos, sin = np.cos(ang), np.sin(ang)
    cos_t = np.concatenate([cos, cos, cos, cos], axis=1).astype(np.float32)
    sin_t = np.concatenate([-sin, sin, -sin, sin], axis=1).astype(np.float32)
    return jnp.asarray(cos_t), jnp.asarray(sin_t)


def _attn_prompt_kernel(sink_ref, q_ref, kc_ref, kp_ref, vc_ref, vp_ref, gate_ref, h_ref, wout_ref,
                        y_ref, p_ref, o_ref, *, nqb):
    i = pl.program_id(1)
    w = SWA_WINDOW
    hd = SWA_HEAD_DIM
    npair = SWA_GROUP // 2
    lane2 = lax.broadcasted_iota(jnp.int32, (2 * w, LANES), 1)
    low = lane2 < hd
    rr = lax.broadcasted_iota(jnp.int32, (w, w), 0)
    cc = lax.broadcasted_iota(jnp.int32, (w, w), 1)
    tri = cc <= rr
    low_w = cc < hd
    r4 = lax.broadcasted_iota(jnp.int32, (4 * w, LANES), 0)
    c4 = lax.broadcasted_iota(jnp.int32, (4 * w, LANES), 1)
    den_cols = ((r4 < 2 * w) == (c4 < hd)).astype(BF16)
    units = [(jb, kv) for jb in range(nqb) for kv in range(SWA_KV_HEADS)]

    def scores(jb, kv):
        rows = slice(jb * w, (jb + 1) * w)
        if jb == 0:
            k_prev, v_prev = kp_ref[...], vp_ref[...]
        else:
            prev_rows = slice((jb - 1) * w, jb * w)
            k_prev, v_prev = kc_ref[prev_rows, :], vc_ref[prev_rows, :]
        kcat = jnp.concatenate([k_prev, kc_ref[rows, :]], axis=0)
        vcat = jnp.concatenate([v_prev, vc_ref[rows, :]], axis=0)
        own = low if kv == 0 else jnp.logical_not(low)
        k_own = jnp.where(own, kcat, 0.0)
        v_own = jnp.where(own, vcat, 0.0)
        k_oth = pltpu.roll(k_own, hd, 1)
        v_oth = pltpu.roll(v_own, hd, 1)
        k_lo, k_hi = (k_own, k_oth) if kv == 0 else (k_oth, k_own)
        v_lo, v_hi = (v_own, v_oth) if kv == 0 else (v_oth, v_own)
        k_rhs = jnp.concatenate([k_lo, k_hi], axis=0).astype(BF16)
        v_rhs = jnp.concatenate([jnp.concatenate([v_lo, v_hi], axis=0).astype(BF16), den_cols], axis=1)
        qs = jnp.concatenate([q_ref[rows, (kv * npair + pr) * LANES:(kv * npair + pr + 1) * LANES]
                              for pr in range(npair)], axis=0)
        return _dot_nt(qs, k_rhs), v_rhs

    def softmax_pv(kv, first, buf, s_all, v_rhs):
        sink_terms = []
        for pr in range(npair):
            prow = slice(pr * w, (pr + 1) * w)
            pair_terms = []
            for e in range(2):
                s_prev = s_all[prow, (2 * e) * w:(2 * e + 1) * w]
                s_cur = s_all[prow, (2 * e + 1) * w:(2 * e + 2) * w]
                if first:
                    s_prev = jnp.where(i > 0, s_prev, -jnp.inf)
                sc = jnp.where(tri, s_cur, s_prev)
                sink = sink_ref[2 * (kv * npair + pr) + e] * LOG2E
                mx = jnp.maximum(jnp.max(sc, axis=-1, keepdims=True), sink)
                pe = jnp.exp2(sc - mx)
                p_ref[buf, prow, (2 * e) * w:(2 * e + 1) * w] = jnp.where(tri, 0.0, pe).astype(BF16)
                p_ref[buf, prow, (2 * e + 1) * w:(2 * e + 2) * w] = jnp.where(tri, pe, 0.0).astype(BF16)
                pair_terms.append(jnp.exp2(sink - mx))
            sink_terms.append(pair_terms)
        return _dot(p_ref[buf], v_rhs), sink_terms

    def normalise(jb, kv, o_ext, sink_terms):
        rows = slice(jb * w, (jb + 1) * w)
        for pr in range(npair):
            prow = slice(pr * w, (pr + 1) * w)
            cols = slice((kv * npair + pr) * LANES, (kv * npair + pr + 1) * LANES)
            st = jnp.where(low_w, sink_terms[pr][0], sink_terms[pr][1])
            o = o_ext[prow, :LANES] / (o_ext[prow, LANES:] + st)
            o_ref[rows, cols] = o.astype(o_ref.dtype)

    def project(rows):
        og = o_ref[rows, :].astype(F32) * gate_ref[rows, :].astype(F32)
        y_ref[rows, :] = h_ref[rows, :] + _dot(og.astype(BF16), wout_ref[...])

    group = 2
    pending = scores(*units[0])
    unfinished = None
    for n, (jb, kv) in enumerate(units):
        current = pending
        if n + 1 < len(units):
            pending = scores(*units[n + 1])
        result = softmax_pv(kv, jb == 0, n % 2, *current)
        if unfinished is not None:
            normalise(*unfinished)
            done_jb, done_kv = unfinished[:2]
            if done_kv == SWA_KV_HEADS - 1 and (done_jb + 1) % group == 0:
                project(slice((done_jb + 1 - group) * w, (done_jb + 1) * w))
        unfinished = (jb, kv) + result
    normalise(*unfinished)
    project(slice((nqb - group) * w, nqb * w))


def _attn_prompt(sinks, q, k, v, gate, h, w_out, nseq, seqlen):
    w = SWA_WINDOW
    nqb = 8
    tm = nqb * w
    assert seqlen % tm == 0
    nblk = seqlen // tm
    qw = q.shape[1]
    dm = h.shape[1]
    row = lambda n: pl.BlockSpec((tm, n), lambda b, i: (b * nblk + i, 0))
    prev = lambda n: pl.BlockSpec((w, n), lambda b, i: (jnp.maximum((b * nblk + i) * nqb - 1, 0), 0))
    smem = pl.BlockSpec(memory_space=pltpu.SMEM)
    return pl.pallas_call(
        functools.partial(_attn_prompt_kernel, nqb=nqb),
        grid=(nseq, nblk),
        in_specs=[smem, row(qw), row(LANES), prev(LANES), row(LANES), prev(LANES), row(qw), row(dm),
                  _full(w_out.shape)],
        out_specs=row(dm),
        out_shape=jax.ShapeDtypeStruct(h.shape, F32),
        scratch_shapes=[pltpu.VMEM((2, 4 * w, 4 * w), BF16), pltpu.VMEM((tm, qw), BF16)],
        compiler_params=_cparams(("parallel", "arbitrary")),
        name="attn_prompt",
    )(sinks, q, k, k, v, v, gate, h, w_out)


def _attn_sample_kernel(sink_ref, q_ref, kn_ref, vn_ref, kc_ref, vc_ref, o_ref, ko_ref, vo_ref, q2_ref, *, seqlen):
    nq = SWA_HEADS * seqlen
    hd = SWA_HEAD_DIM
    heads = [(kv, g) for kv in range(SWA_KV_HEADS) for g in range(SWA_GROUP)]

    def stacked(kv, g):
        r0 = (kv * SWA_GROUP + g) * seqlen
        return slice(r0, r0 + seqlen), slice(kv * hd, (kv + 1) * hd)

    q2_ref[...] = jnp.zeros(q2_ref.shape, q2_ref.dtype)
    for kv, g in heads:
        rows, lanes = stacked(kv, g)
        h = kv * SWA_GROUP + g
        q2_ref[:, rows, lanes] = q_ref[:, :, h * hd:(h + 1) * hd]
    ncache = kc_ref.shape[1]
    t_row = lax.broadcasted_iota(jnp.int32, (nq, ncache), 0) % seqlen
    c_col = lax.broadcasted_iota(jnp.int32, (nq, ncache), 1)
    cache_ok = c_col > t_row - (SWA_WINDOW - ncache)
    nnew = kn_ref.shape[1]
    t_row_n = lax.broadcasted_iota(jnp.int32, (nq, nnew), 0) % seqlen
    n_col = lax.broadcasted_iota(jnp.int32, (nq, nnew), 1)
    new_ok = n_col <= t_row_n
    sink = (sink_ref[...] * LOG2E)[None, :, 0:1]
    bqk = lambda a, b: lax.dot_general(a, b, (((2,), (2,)), ((0,), (0,))), preferred_element_type=F32)
    bpv = lambda a, b: lax.dot_general(a, b, (((2,), (1,)), ((0,), (0,))), preferred_element_type=F32)
    q = q2_ref[...].astype(BF16)
    kc = kc_ref[...]
    vc = vc_ref[...]
    kn = kn_ref[...]
    vn = vn_ref[...]
    sc = jnp.where(cache_ok[None], bqk(q, kc.astype(BF16)), -jnp.inf)
    sn = jnp.where(new_ok[None], bqk(q, kn.astype(BF16)), -jnp.inf)
    mx = jnp.maximum(jnp.maximum(jnp.max(sc, axis=-1, keepdims=True), jnp.max(sn, axis=-1, keepdims=True)), sink)
    pc = jnp.exp2(sc - mx)
    pn = jnp.exp2(sn - mx)
    den = jnp.sum(pc, axis=-1, keepdims=True) + jnp.sum(pn, axis=-1, keepdims=True) + jnp.exp2(sink - mx)
    inv = 1.0 / den
    o2 = bpv((pc * inv).astype(BF16), vc.astype(BF16)) + bpv((pn * inv).astype(BF16), vn.astype(BF16))
    for kv, g in heads:
        rows, lanes = stacked(kv, g)
        h = kv * SWA_GROUP + g
        o_ref[:, :, h * hd:(h + 1) * hd] = o2[:, rows, lanes]
    keep = ncache - seqlen
    ko_ref[:, 0:keep, :] = kc[:, seqlen:ncache, :]
    ko_ref[:, keep:ncache, :] = kn[:, 0:seqlen, :]
    vo_ref[:, 0:keep, :] = vc[:, seqlen:ncache, :]
    vo_ref[:, keep:ncache, :] = vn[:, 0:seqlen, :]


def _attn_sample(sink_rows, q, kn, vn, kc, vc):
    n, seqlen, _ = q.shape
    nseq = 16
    assert n % nseq == 0
    blk = lambda a: pl.BlockSpec((nseq,) + a.shape[1:], lambda i: (i, 0, 0))
    return pl.pallas_call(
        functools.partial(_attn_sample_kernel, seqlen=seqlen),
        grid=(n // nseq,),
        in_specs=[_full(sink_rows.shape), blk(q), blk(kn), blk(vn), blk(kc), blk(vc)],
        out_specs=[blk(q), blk(kc), blk(vc)],
        out_shape=[jax.ShapeDtypeStruct(q.shape, F32), jax.ShapeDtypeStruct(kc.shape, F32),
                   jax.ShapeDtypeStruct(vc.shape, F32)],
        scratch_shapes=[pltpu.VMEM((nseq, SWA_HEADS * seqlen, SWA_KV_HEADS * SWA_HEAD_DIM), F32)],
        compiler_params=_cparams(("parallel",)),
        name="attn_sample",
    )(sink_rows, q, kn, vn, kc, vc)


def _odd_out_kernel(h_ref, o_ref, gate_ref, w_ref, y_ref):
    og = o_ref[...].astype(F32) * gate_ref[...].astype(F32)
    y_ref[...] = h_ref[...] + _dot(og.astype(BF16), w_ref[...])


def _odd_out(h, o, gate, w):
    m, dm = h.shape
    tm = _row_tile(m)
    row = lambda n: pl.BlockSpec((tm, n), lambda i: (i, 0))
    return pl.pallas_call(
        _odd_out_kernel,
        grid=(m // tm,),
        in_specs=[row(dm), row(o.shape[1]), row(gate.shape[1]), _full(w.shape)],
        out_specs=row(dm),
        out_shape=jax.ShapeDtypeStruct((m, dm), F32),
        compiler_params=_cparams(("parallel",)),
        name="odd_out",
    )(h, o, gate, w)


def _trunk(x, s5_x0, gla_s0, pe, po, pos, nseq, seqlen, act_dtype):
    u2, ga, q, k, v, la, gb = _even_in(x, pe['norm_g'], pe['w_in'], pe['wgate'], pe['bgate'], act_dtype,
                                       _s5_chunk(seqlen))
    y4, s5_fin = _s5_branch(u2, s5_x0, pe['s5_ops'], pe['d'], nseq, seqlen, act_dtype)
    ob, gla_fin = _gla(q, k, v, la, gb, gla_s0, pe['gla_norm_g'], nseq, seqlen, act_dtype)
    cos_t, sin_t = _rope_tables(pos)
    h, q1, k1, v1, gate = _mid(x, y4, ga, ob, pe, po, cos_t, sin_t, act_dtype)
    return h, q1, k1, v1, gate, s5_fin, gla_fin


def kernel(x_prompt, x_sample, state_s5_re, state_s5_im, state_gla, cache_swa_k, cache_swa_v,
           even_norm_g, even_w_in, s5_lambda_re, s5_lambda_im, s5_log_dt, s5_b_re, s5_b_im,
           s5_c_re, s5_c_im, s5_d, s5_w_glu, s5_b_glu, gla_w_gate, gla_b_gate, gla_norm_g,
           even_w_out, odd_norm_g, odd_w_in, swa_q_norm_g, swa_k_norm_g, swa_sinks, odd_w_out):
    nb, seq, dm = x_prompt.shape
    ns, dseq, _ = x_sample.shape
    ng = s5_lambda_re.shape[1]
    no = ng // OCT
    xp = x_prompt.reshape(nb * seq, dm)
    xs = x_sample.reshape(ns * dseq, dm)

    i = 0
    w_in = even_w_in[i]
    assert ng * S5_GROUP == S5_WIDTH
    col_alow = C_CODE
    col_gb = col_alow + GLA_GATE_RANK
    pad_rank = LANES - GLA_GATE_RANK
    pe = {
        'norm_g': even_norm_g[i][None, :],
        'w_in': jnp.concatenate([w_in[:, :col_alow], jnp.pad(w_in[:, col_alow:col_gb], ((0, 0), (0, pad_rank))),
                                 w_in[:, col_gb:]], axis=1).astype(BF16),
        'wgate': jnp.pad(gla_w_gate[i], ((0, pad_rank), (0, 0))).astype(BF16),
        'bgate': gla_b_gate[i][None, :],
        's5_ops': _s5_params(s5_lambda_re[i], s5_lambda_im[i], s5_log_dt[i], s5_b_re[i], s5_b_im[i],
                             s5_c_re[i], s5_c_im[i]),
        'gla_norm_g': gla_norm_g[i][None, :],
        'd': s5_d[i],
        'wglu': s5_w_glu[i].astype(BF16),
        'bglu': s5_b_glu[i][None, :],
        'wo': even_w_out[i].astype(BF16),
    }
    po = {
        'norm_g': odd_norm_g[i][None, :],
        'w_in': odd_w_in[i].astype(BF16),
        'gq': jnp.tile(swa_q_norm_g[i], MXU_TILE // SWA_HEAD_DIM)[None, :],
        'gk': jnp.tile(swa_k_norm_g[i], LANES // SWA_HEAD_DIM)[None, :],
    }
    w_out = odd_w_out[i].astype(BF16)
    sinks = swa_sinks[i]
    kvw = SWA_KV_HEADS * SWA_HEAD_DIM

    s5_zero = jnp.zeros((no, nb, 2 * OCT * S5_STATE), F32)
    gla_zero = jnp.zeros((nb, GLA_HEADS, GLA_HEAD_K, GLA_HEAD_V), F32)
    hp, q, k, v, gate, s5_p, gla_p = _trunk(xp, s5_zero, gla_zero, pe, po, np.arange(seq), nb, seq, BF16)
    s5r_p, s5i_p = _s5_state_out(s5_p)
    y_prompt = _attn_prompt(sinks, q, k, v, gate, hp, w_out, nb, seq).reshape(nb, seq, dm)
    cache_len = min(SWA_WINDOW, seq)
    tail = lambda a: (a.reshape(nb, seq, kvw)[:, seq - cache_len:]
                      .reshape(1, nb, cache_len, SWA_KV_HEADS, SWA_HEAD_DIM))
    swk_p, swv_p = tail(k), tail(v)

    pos_s = np.tile(PAST_LEN + np.arange(dseq), ns)
    s5_init = _s5_state_in(state_s5_re[i], state_s5_im[i], no)
    hs, q, k, v, gate, s5_s, gla_s = _trunk(xs, s5_init, state_gla[i], pe, po, pos_s, ns, dseq, F32)
    s5r_s, s5i_s = _s5_state_out(s5_s)
    ncache = cache_swa_k.shape[2]
    sink_rows = jnp.broadcast_to(jnp.repeat(sinks, dseq)[:, None], (SWA_HEADS * dseq, LANES))
    kn = k.reshape(ns, dseq, kvw)
    vn = v.reshape(ns, dseq, kvw)
    npad = BF16_ROWS - dseq
    kn_pad = jnp.pad(kn, ((0, 0), (0, npad), (0, 0)))
    vn_pad = jnp.pad(vn, ((0, 0), (0, npad), (0, 0)))
    kc = cache_swa_k[i].reshape(ns, ncache, kvw)
    vc = cache_swa_v[i].reshape(ns, ncache, kvw)
    o, kc_new, vc_new = _attn_sample(sink_rows, q.reshape(ns, dseq, -1), kn_pad, vn_pad, kc, vc)
    y_sample = _odd_out(hs, o.reshape(ns * dseq, -1), gate, w_out).reshape(ns, dseq, dm)
    swk_s = kc_new.reshape(1, ns, ncache, SWA_KV_HEADS, SWA_HEAD_DIM)
    swv_s = vc_new.reshape(1, ns, ncache, SWA_KV_HEADS, SWA_HEAD_DIM)

    return (y_prompt, y_sample,
            s5r_p, s5i_p, gla_p[None], swk_p, swv_p,
            s5r_s, s5i_s, gla_s[None], swk_s, swv_s)
```

```python
import functools
import math

import jax
import jax.numpy as jnp
import numpy as np
from jax import lax
from jax.experimental import pallas as pl
from jax.experimental.pallas import tpu as pltpu

F32 = jnp.float32
BF16 = jnp.bfloat16

PAST_LEN = 8192
NORM_EPS = 1e-6
S5_GROUP = 16
S5_STATE = 64
S5_CHUNK = 16
GLA_HEADS = 4
GLA_HEAD_K = 64
GLA_HEAD_V = 128
GLA_GATE_RANK = 16
GLA_GATE_TAU = 16.0
GLA_CHUNK = 64
S5_WIDTH = 512
GLA_K_WIDTH = GLA_HEADS * GLA_HEAD_K
GLA_V_WIDTH = GLA_HEADS * GLA_HEAD_V
SWA_HEADS = 16
SWA_KV_HEADS = 2
SWA_GROUP = SWA_HEADS // SWA_KV_HEADS
SWA_HEAD_DIM = 64
SWA_WINDOW = 128
ROPE_THETA = 10000.0
LOG2E = math.log2(math.e)
LANES = 128
BF16_ROWS = 16
MXU_TILE = 256
OCT = LANES // S5_GROUP
ROW_TILE = 512
VMEM_LIMIT = 48 * 1024 * 1024
C_U = 0
C_GATE_A = C_U + S5_WIDTH
C_Q = C_GATE_A + S5_WIDTH
C_K = C_Q + GLA_K_WIDTH
C_V = C_K + GLA_K_WIDTH
C_CODE = C_V + GLA_V_WIDTH
C_GATE_B = C_CODE + GLA_GATE_RANK
C_END = C_GATE_B + GLA_V_WIDTH


def _cparams(sem):
    return pltpu.CompilerParams(dimension_semantics=sem, vmem_limit_bytes=VMEM_LIMIT)


def _full(shape):
    n = len(shape)
    return pl.BlockSpec(shape, lambda *_: (0,) * n)


def _dot(a, b):
    return jnp.dot(a, b, preferred_element_type=F32)


def _dot_nt(a, b):
    return lax.dot_general(a, b, (((1,), (1,)), ((), ())), preferred_element_type=F32)


def _dot_tn(a, b):
    return lax.dot_general(a, b, (((0,), (0,)), ((), ())), preferred_element_type=F32)


def _split_bf16(x):
    hi = x.astype(BF16)
    lo = (x - hi.astype(F32)).astype(BF16)
    return hi, lo


def _rms_rows(x, g):
    return x * lax.rsqrt(jnp.mean(x * x, axis=-1, keepdims=True) + NORM_EPS) * g


def _sigmoid(x):
    return 1.0 / (1.0 + jnp.exp(-x))


def _silu(x):
    return x * _sigmoid(x)


def _row_tile(m, tile=ROW_TILE):
    return tile if m % tile == 0 else m


def _even_in_kernel(x_ref, g_ref, w_ref, wgate_ref, bgate_ref,
                    u2_ref, ga_ref, q_ref, k_ref, v_ref, la_ref, gb_ref, uscr_ref, wgb_ref, *, t):
    @pl.when(pl.program_id(0) == 0)
    def _():
        wgb_ref[...] = w_ref[:, C_GATE_B:C_END]

    xb = _rms_rows(x_ref[...], g_ref[...]).astype(BF16)

    def proj(lo, hi):
        return _dot(xb, w_ref[:, lo:hi])

    u = proj(C_U, C_GATE_A)
    nrow = u.shape[0] // t
    for o in range(u2_ref.shape[0]):
        uscr_ref[o] = u[:, o * LANES:(o + 1) * LANES]
        for tt in range(t):
            piece = uscr_ref[o, pl.ds(tt, nrow, stride=t), :]
            u2_ref[o, :, tt * LANES:(tt + 1) * LANES] = piece.astype(u2_ref.dtype)
    ga_ref[...] = _silu(proj(C_GATE_A, C_Q)).astype(ga_ref.dtype)
    q_ref[...] = (proj(C_Q, C_K) * (GLA_HEAD_K ** -0.5)).astype(q_ref.dtype)
    k_ref[...] = proj(C_K, C_V).astype(k_ref.dtype)
    v_ref[...] = proj(C_V, C_CODE).astype(v_ref.dtype)
    gb_ref[...] = _silu(_dot(xb, wgb_ref[...])).astype(gb_ref.dtype)
    a_low = proj(C_CODE, C_CODE + LANES)
    logit = _dot(a_low.astype(BF16), wgate_ref[...]) + bgate_ref[...]
    log_sig = jnp.minimum(logit, 0.0) - jnp.log1p(jnp.exp(-jnp.abs(logit)))
    la_ref[...] = log_sig * (1.0 / GLA_GATE_TAU)


def _even_in(x, g, w, wgate, bgate, act_dtype, t):
    m, d = x.shape
    tm = _row_tile(m, 2 * ROW_TILE) if m > ROW_TILE else m
    row = lambda n: pl.BlockSpec((tm, n), lambda i: (i, 0))
    assert w.shape[1] == C_END
    no = S5_WIDTH // LANES
    chunk = pl.BlockSpec((no, tm // t, t * LANES), lambda i: (0, i, 0))
    outs = [(S5_WIDTH, act_dtype), (GLA_K_WIDTH, act_dtype), (GLA_K_WIDTH, act_dtype), (GLA_V_WIDTH, act_dtype),
            (GLA_K_WIDTH, F32), (GLA_V_WIDTH, act_dtype)]
    return pl.pallas_call(
        functools.partial(_even_in_kernel, t=t),
        grid=(m // tm,),
        in_specs=[row(d), _full(g.shape), _full(w.shape), _full(wgate.shape), _full(bgate.shape)],
        out_specs=[chunk] + [row(n) for n, _ in outs],
        out_shape=[jax.ShapeDtypeStruct((no, m // t, t * LANES), act_dtype)]
        + [jax.ShapeDtypeStruct((m, n), dt) for n, dt in outs],
        scratch_shapes=[pltpu.VMEM((no, tm, LANES), F32), pltpu.VMEM((d, GLA_V_WIDTH), BF16)],
        compiler_params=_cparams(("arbitrary",)),
        name="even_in",
    )(x, g, w, wgate, bgate)


def _group_mask(shape, row_span, col_span):
    rg = (lax.broadcasted_iota(jnp.int32, shape, 0) // row_span) % OCT
    cg = (lax.broadcasted_iota(jnp.int32, shape, 1) // col_span) % OCT
    return rg == cg


def _s5_state_kernel(u_ref, bre_ref, bim_ref, x0_ref, are_ref, aim_ref, xs_ref, xf_ref, loc_ref, *, nseq, nchunks):
    hw = OCT * S5_STATE
    kk = u_ref.shape[2]
    bp = jnp.concatenate([bre_ref[0]] * OCT + [bim_ref[0]] * OCT, axis=1)
    bp = jnp.where(_group_mask((kk, 2 * hw), S5_GROUP, S5_STATE), bp, 0.0).astype(BF16)
    loc_ref[...] = _dot(u_ref[0], bp)
    a_re = are_ref[0]
    a_im = aim_ref[0]
    if nchunks == 1:
        x0 = x0_ref[0]
        xr, xi = x0[:, :hw], x0[:, hw:]
        loc = loc_ref[...]
        xf_ref[0, :, :hw] = a_re * xr - a_im * xi + loc[:, :hw]
        xf_ref[0, :, hw:] = a_re * xi + a_im * xr + loc[:, hw:]
        xs_ref[0] = x0.astype(xs_ref.dtype)
    else:
        def body(j, carry):
            new = []
            for b in range(nseq):
                xr, xi = carry[b]
                row = pl.ds(b * nchunks + j, 1)
                lr = loc_ref[row, :hw]
                li = loc_ref[row, hw:]
                loc_ref[row, :hw] = xr
                loc_ref[row, hw:] = xi
                new.append((a_re * xr - a_im * xi + lr, a_re * xi + a_im * xr + li))
            return tuple(new)

        init = tuple((x0_ref[0, b:b + 1, :hw], x0_ref[0, b:b + 1, hw:]) for b in range(nseq))
        fin = lax.fori_loop(0, nchunks, body, init, unroll=4)
        for b in range(nseq):
            xf_ref[0, b:b + 1, :hw] = fin[b][0]
            xf_ref[0, b:b + 1, hw:] = fin[b][1]
        xs_ref[0] = loc_ref[...].astype(xs_ref.dtype)


def _s5_state(u2, bre, bim, x0, are, aim, nseq, nchunks):
    no, r, kk = u2.shape
    sw = x0.shape[2]
    blk = lambda a: pl.BlockSpec((1,) + a.shape[1:], lambda o: (o,) + (0,) * (a.ndim - 1))
    return pl.pallas_call(
        functools.partial(_s5_state_kernel, nseq=nseq, nchunks=nchunks),
        grid=(no,),
        in_specs=[blk(u2), blk(bre), blk(bim), blk(x0), blk(are), blk(aim)],
        out_specs=[pl.BlockSpec((1, r, sw), lambda o: (o, 0, 0)), blk(x0)],
        out_shape=[jax.ShapeDtypeStruct((no, r, sw), BF16), jax.ShapeDtypeStruct(x0.shape, F32)],
        scratch_shapes=[pltpu.VMEM((r, sw), F32)],
        compiler_params=_cparams(("parallel",)),
        name="s5_state",
    )(u2, bre, bim, x0, are, aim)


def _dot_nt_f32(a, b):
    a_hi, a_lo = _split_bf16(a)
    b_hi, b_lo = _split_bf16(b)
    return _dot_nt(a_hi, b_hi) + _dot_nt(a_hi, b_lo) + _dot_nt(a_lo, b_hi)


def _s5_out_kernel(u_ref, xs_ref, bre_ref, bim_ref, cre_ref, cim_ref, zre_ref, zim_ref, d_ref, y_ref, yscr_ref):
    r, kk = u_ref.shape[1], u_ref.shape[2]
    t = kk // LANES
    sw = xs_ref.shape[2]
    ntile = kk // MXU_TILE
    taps = _dot_nt_f32(bre_ref[0], zre_ref[0]) - _dot_nt_f32(bim_ref[0], zim_ref[0])
    tmask = _group_mask((LANES, LANES), S5_GROUP, S5_GROUP)

    def tap(lag):
        if lag < 0:
            return jnp.zeros((LANES, LANES), F32)
        s = t - 1 - lag
        return jnp.where(tmask, taps[s * LANES:(s + 1) * LANES], 0.0)

    wts = [jnp.concatenate([jnp.concatenate([tap(2 * d), tap(2 * d + 1)], axis=1),
                            jnp.concatenate([tap(2 * d - 1), tap(2 * d)], axis=1)], axis=0).astype(BF16)
           for d in range(ntile)]
    cpt = jnp.concatenate([cre_ref[0]] * OCT + [cim_ref[0]] * OCT, axis=1)
    cpt = jnp.where(_group_mask((kk, sw), S5_GROUP, S5_STATE), cpt, 0.0).astype(BF16)
    xs = xs_ref[0]
    skip = jnp.concatenate([d_ref[0]] * (MXU_TILE // LANES), axis=1)
    for n in range(ntile):
        cols = slice(n * MXU_TILE, (n + 1) * MXU_TILE)
        acc = _dot_nt(xs, cpt[cols]) + skip * u_ref[0, :, cols].astype(F32)
        for k in range(n + 1):
            acc = acc + _dot(u_ref[0, :, k * MXU_TILE:(k + 1) * MXU_TILE], wts[n - k])
        for e in range(MXU_TILE // LANES):
            yscr_ref[pl.ds(2 * n + e, r, stride=t), :] = acc[:, e * LANES:(e + 1) * LANES]
    y_ref[0] = yscr_ref[...].astype(y_ref.dtype)


def _s5_out(u2, xs, bre, bim, cre, cim, zre, zim, d, out_dtype):
    no, r, kk = u2.shape
    m = r * (kk // LANES)
    blk = lambda a: pl.BlockSpec((1,) + a.shape[1:], lambda o: (o,) + (0,) * (a.ndim - 1))
    ops = (u2, xs, bre, bim, cre, cim, zre, zim, d)
    return pl.pallas_call(
        _s5_out_kernel,
        grid=(no,),
        in_specs=[blk(a) for a in ops],
        out_specs=pl.BlockSpec((1, m, LANES), lambda o: (o, 0, 0)),
        out_shape=jax.ShapeDtypeStruct((no, m, LANES), out_dtype),
        scratch_shapes=[pltpu.VMEM((m, LANES), F32)],
        compiler_params=_cparams(("parallel",)),
        name="s5_out",
    )(*ops)


def _s5_params(lam_re, lam_im, log_dt, b_re, b_im, c_re, c_im):
    t = S5_CHUNK
    ng = lam_re.shape[0]
    no = ng // OCT
    dt = jnp.exp(log_dt)[:, None]
    a = lam_re * dt
    b = lam_im * dt
    n = jnp.arange(t + 1, dtype=F32)[None, :, None]
    mag = jnp.exp(n * a[:, None, :])
    pw_re = mag * jnp.cos(n * b[:, None, :])
    pw_im = mag * jnp.sin(n * b[:, None, :])
    em1_re = jnp.expm1(a) * jnp.cos(b) - 2.0 * jnp.sin(0.5 * b) ** 2
    em1_im = jnp.exp(a) * jnp.sin(b)
    den = lam_re * lam_re + lam_im * lam_im
    z_re = (em1_re * lam_re + em1_im * lam_im) / den
    z_im = (em1_im * lam_re - em1_re * lam_im) / den
    bb_re = z_re[..., None] * b_re - z_im[..., None] * b_im
    bb_im = z_re[..., None] * b_im + z_im[..., None] * b_re
    def rows(w):
        return w.reshape(no, OCT, t, S5_GROUP, S5_STATE).transpose(0, 2, 1, 3, 4).reshape(no, t * LANES, S5_STATE)

    bt_re = bb_re.transpose(0, 2, 1)[:, None]
    bt_im = bb_im.transpose(0, 2, 1)[:, None]
    r_re, r_im = pw_re[:, t - 1::-1, None, :], pw_im[:, t - 1::-1, None, :]
    bre = rows(r_re * bt_re - r_im * bt_im)
    bim = rows(r_re * bt_im + r_im * bt_re)
    o_re, o_im = pw_re[:, 1:, None, :], pw_im[:, 1:, None, :]
    cre = rows(c_re[:, None] * o_re - c_im[:, None] * o_im)
    cim = rows(-(c_re[:, None] * o_im + c_im[:, None] * o_re))
    return bre, bim, cre, cim, c_re.reshape(no, LANES, S5_STATE), c_im.reshape(no, LANES, S5_STATE), pw_re, pw_im


def _s5_chunk(seqlen):
    return math.gcd(seqlen, S5_CHUNK)


def _s5_branch(u2, x0, ops, d, nseq, seqlen, out_dtype):
    bre, bim, cre, cim, zre, zim, pw_re, pw_im = ops
    no = u2.shape[0]
    t = _s5_chunk(seqlen)
    kk = t * LANES
    nchunks = seqlen // t
    are = pw_re[:, t].reshape(no, 1, OCT * S5_STATE)
    aim = pw_im[:, t].reshape(no, 1, OCT * S5_STATE)
    tail = S5_CHUNK * LANES - kk
    bre, bim = bre[:, tail:], bim[:, tail:]
    xs, xf = _s5_state(u2, bre, bim, x0, are, aim, nseq, nchunks)
    return _s5_out(u2, xs, bre, bim, cre[:, :kk], cim[:, :kk], zre, zim, d.reshape(no, 1, LANES), out_dtype), xf


def _s5_state_in(re, im, no):
    nseq = re.shape[0]
    f = lambda a: a.reshape(nseq, no, OCT * S5_STATE).transpose(1, 0, 2)
    return jnp.concatenate([f(re), f(im)], axis=-1)


def _s5_state_out(xf):
    no, nseq, _ = xf.shape
    hw = OCT * S5_STATE
    f = lambda a: a.transpose(1, 0, 2).reshape(1, nseq, no * OCT, S5_STATE)
    return f(xf[:, :, :hw]), f(xf[:, :, hw:])


def _gla_kernel(q_ref, k_ref, v_ref, la_ref, gb_ref, s0_ref, ng_ref, o_ref, sf_ref, st_ref,
                *, nseq, nchunks, c):
    i = pl.program_id(1)

    @pl.when(i == 0)
    def _():
        st_ref[...] = s0_ref[...]

    nh, hk, hv = GLA_HEADS, GLA_HEAD_K, GLA_HEAD_V
    nch = nseq * nchunks
    tm = nch * c
    iota = lambda shape, d: lax.broadcasted_iota(jnp.int32, shape, d)
    ng = ng_ref[...]

    tb = min(tm, MXU_TILE)
    rt, ct = iota((tb, tb), 0), iota((tb, tb), 1)
    tril = ((rt // c == ct // c) & (rt >= ct)).astype(BF16)
    parts = []
    for r0 in range(0, tm, tb):
        la_hi, la_lo = _split_bf16(la_ref[r0:r0 + tb, :])
        parts.append(_dot(tril, la_hi) + _dot(tril, la_lo))
    bcum = jnp.concatenate(parts, axis=0)
    e_hi, e_lo = _split_bf16(jnp.exp(jnp.concatenate([bcum[(ci + 1) * c - 1:(ci + 1) * c] for ci in range(nch)],
                                                     axis=0)))
    pick = (iota((nch, LANES), 0) == iota((nch, LANES), 1)).astype(BF16)
    dec_t = _dot_tn(e_hi, pick) + _dot_tn(e_lo, pick)
    q_all = q_ref[...].astype(F32)
    k_all = k_ref[...].astype(F32)
    q_dec_all = q_all * jnp.exp(bcum)
    k_dec_all = k_all * jnp.exp(-bcum)

    own_k = iota((nh * c, nh * hk), 0) // c == iota((nh * c, nh * hk), 1) // hk
    own_v = iota((nh * c, nh * hv), 0) // c == iota((nh * c, nh * hv), 1) // hv
    causal = iota((c, nh * c), 1) % c <= iota((c, nh * c), 0)
    zero_v = jnp.zeros((hk, hv), F32)

    def intra(ci):
        rows = slice(ci * c, (ci + 1) * c)
        q_dec = q_dec_all[rows].astype(BF16)
        k_dec = k_dec_all[rows]
        b_c = bcum[rows]
        k_tail = (k_all[rows] * jnp.exp(b_c[c - 1:c] - b_c)).astype(BF16)
        v = v_ref[rows, :].astype(F32)
        k_bd = jnp.where(own_k, jnp.concatenate([k_dec] * nh, axis=0), 0.0).astype(BF16)
        v_bd = jnp.where(own_v, jnp.concatenate([v] * nh, axis=0), 0.0).astype(BF16)
        att = jnp.where(causal, _dot_nt(q_dec, k_bd), 0.0)
        o_intra = _dot(att.astype(BF16), v_bd)
        vb = v.astype(BF16)
        kvs = []
        for h0 in range(0, nh, 2):
            kv2 = _dot_tn(k_tail[:, h0 * hk:(h0 + 2) * hk], vb[:, h0 * hv:(h0 + 2) * hv])
            kvs += [kv2[:hk, :hv], kv2[hk:, hv:]]
        return q_dec, o_intra, kvs

    def carry(ci, sts, q_dec, o_intra, kvs):
        st_bd = jnp.concatenate(
            [jnp.concatenate([zero_v] * h + [sts[h]] + [zero_v] * (nh - 1 - h), axis=1) for h in range(nh)],
            axis=0).astype(BF16)
        o = o_intra + _dot(q_dec, st_bd)
        new = [dec_t[h * hk:(h + 1) * hk, ci:ci + 1] * sts[h] + kvs[h] for h in range(nh)]
        return o, new

    def finish(ci, o):
        rows = slice(ci * c, (ci + 1) * c)
        for h in range(nh):
            vs = slice(h * hv, (h + 1) * hv)
            oh = _rms_rows(o[:, vs], ng) * gb_ref[rows, vs].astype(F32)
            o_ref[rows, vs] = oh.astype(o_ref.dtype)

    sts = None
    pending = intra(0)
    unfinished = None
    for ci in range(nch):
        s, first, last = ci // nchunks, ci % nchunks == 0, ci % nchunks == nchunks - 1
        current = pending
        if ci + 1 < nch:
            pending = intra(ci + 1)
        if first:
            sts = [st_ref[s, h] for h in range(nh)]
        o, sts = carry(ci, sts, *current)
        if last:
            for h in range(nh):
                st_ref[s, h] = sts[h]
        if unfinished is not None:
            finish(*unfinished)
        unfinished = (ci, o)
    finish(*unfinished)

    @pl.when(i == pl.num_programs(1) - 1)
    def _():
        sf_ref[...] = st_ref[...]


def _gla(q, k, v, la, gb, s0, ng, nseq_total, seqlen, out_dtype):
    c = math.gcd(seqlen, GLA_CHUNK)
    per_seq = seqlen // c
    if per_seq >= 8:
        nseq, nchunks = 1, next(n for n in (32, 16, 8, 1) if per_seq % n == 0)
    else:
        nseq, nchunks = 16, seqlen // c
    assert nseq_total % nseq == 0 and seqlen % (nchunks * c) == 0
    nblk = seqlen // (nchunks * c)
    tm = nseq * nchunks * c
    row = lambda n: pl.BlockSpec((tm, n), lambda b, i: (b * nblk + i, 0))
    st_spec = pl.BlockSpec((nseq, GLA_HEADS, GLA_HEAD_K, GLA_HEAD_V), lambda b, i: (b, 0, 0, 0))
    m = q.shape[0]
    return pl.pallas_call(
        functools.partial(_gla_kernel, nseq=nseq, nchunks=nchunks, c=c),
        grid=(nseq_total // nseq, nblk),
        in_specs=[row(GLA_K_WIDTH), row(GLA_K_WIDTH), row(GLA_V_WIDTH), row(GLA_K_WIDTH), row(GLA_V_WIDTH),
                  st_spec, _full(ng.shape)],
        out_specs=[row(GLA_V_WIDTH), st_spec],
        out_shape=[jax.ShapeDtypeStruct((m, GLA_V_WIDTH), out_dtype), jax.ShapeDtypeStruct(s0.shape, F32)],
        scratch_shapes=[pltpu.VMEM((nseq, GLA_HEADS, GLA_HEAD_K, GLA_HEAD_V), F32)],
        compiler_params=_cparams(("parallel", "arbitrary")),
        name="gla",
    )(q, k, v, la, gb, s0, ng)


def _gelu_tanh(x):
    return 0.5 * x * (1.0 + jnp.tanh(math.sqrt(2.0 / math.pi) * (x + 0.044715 * (x * x * x))))


def _head_ones(n):
    r = lax.broadcasted_iota(jnp.int32, (n, n), 0) // SWA_HEAD_DIM
    c = lax.broadcasted_iota(jnp.int32, (n, n), 1) // SWA_HEAD_DIM
    return (r == c).astype(BF16)


def _rope_block(x, cos_t, sin_t, upper):
    half = SWA_HEAD_DIM // 2
    swapped = jnp.where(upper, pltpu.roll(x, half, 1), pltpu.roll(x, LANES - half, 1))
    return x * cos_t + swapped * sin_t


def _mid_kernel(x_ref, y_ref, ga_ref, ob_ref, wglu_ref, bglu_ref, wo_ref,
                g_ref, w_ref, gq_ref, gk_ref, cos_ref, sin_ref, h_ref, q_ref, k_ref, v_ref, gate_ref):
    tm = x_ref.shape[0]
    lane = lax.broadcasted_iota(jnp.int32, (tm, LANES), 1)
    upper = (lane & (SWA_HEAD_DIM // 2)) != 0
    ones4 = _head_ones(MXU_TILE)
    inv_d = 1.0 / SWA_HEAD_DIM
    qw = SWA_HEADS * SWA_HEAD_DIM
    kw = SWA_KV_HEADS * SWA_HEAD_DIM
    nblk = qw // MXU_TILE
    no = y_ref.shape[0]

    na = ga_ref.shape[1]
    mix_b = _dot(ob_ref[...].astype(BF16), wo_ref[na:, :])
    z = _gelu_tanh(jnp.concatenate([y_ref[o].astype(F32) for o in range(no)], axis=1))
    z = z * _sigmoid(_dot(z.astype(BF16), wglu_ref[...]) + bglu_ref[...])
    out_a = z * ga_ref[...].astype(F32)
    h = x_ref[...] + (_dot(out_a.astype(BF16), wo_ref[:na, :]) + mix_b)
    h_ref[...] = h

    xb = _rms_rows(h, g_ref[...]).astype(BF16)
    cos_t = cos_ref[...]
    sin_t = sin_ref[...]

    def gate_finish(gate):
        gate_ref[...] = _silu(gate).astype(gate_ref.dtype)

    def q_finish(j, q):
        ss = _dot((q * q).astype(BF16), ones4)
        qn = q * lax.rsqrt(ss * inv_d + NORM_EPS) * gq_ref[...]
        for e in range(MXU_TILE // LANES):
            cols = slice(j * MXU_TILE + e * LANES, j * MXU_TILE + (e + 1) * LANES)
            qe = _rope_block(qn[:, e * LANES:(e + 1) * LANES], cos_t, sin_t, upper)
            q_ref[:, cols] = (qe * (SWA_HEAD_DIM ** -0.5 * LOG2E)).astype(q_ref.dtype)

    def kv_finish(kv):
        k = kv[:, :kw]
        ss = _dot((k * k).astype(BF16), ones4[:kw, :kw])
        kn = k * lax.rsqrt(ss * inv_d + NORM_EPS) * gk_ref[...]
        k_ref[...] = _rope_block(kn, cos_t, sin_t, upper)
        v_ref[...] = kv[:, kw:]

    work = [(slice(j * MXU_TILE, (j + 1) * MXU_TILE), functools.partial(q_finish, j)) for j in range(nblk)]
    work += [(slice(qw, qw + 2 * kw), kv_finish)]
    work += [(slice(qw + 2 * kw, None), gate_finish)]
    pending = _dot(xb, w_ref[:, work[0][0]])
    for n, (_, finish) in enumerate(work):
        current = pending
        if n + 1 < len(work):
            pending = _dot(xb, w_ref[:, work[n + 1][0]])
        finish(current)


def _mid(x, y, ga, ob, pe, po, cos_t, sin_t, act_dtype):
    m, dm = x.shape
    tm = _row_tile(m, 2 * ROW_TILE) if m > ROW_TILE else m
    row = lambda n: pl.BlockSpec((tm, n), lambda i: (i, 0))
    slab = pl.BlockSpec((y.shape[0], tm, LANES), lambda i: (0, i, 0))
    assert cos_t.shape[0] % tm == 0
    nper = cos_t.shape[0] // tm
    tab = pl.BlockSpec((tm, LANES), lambda i: (i % nper, 0))
    qw = SWA_HEADS * SWA_HEAD_DIM
    kw = SWA_KV_HEADS * SWA_HEAD_DIM
    weights = [pe['wglu'], pe['bglu'], pe['wo'], po['norm_g'], po['w_in'], po['gq'], po['gk']]
    resident = lambda w: pl.BlockSpec(w.shape, lambda i: (0,) * w.ndim, pipeline_mode=pl.Buffered(1))
    return pl.pallas_call(
        _mid_kernel,
        grid=(m // tm,),
        in_specs=[row(dm), slab, row(S5_WIDTH), row(GLA_V_WIDTH)] + [resident(w) for w in weights] + [tab, tab],
        out_specs=[row(dm), row(qw), row(kw), row(kw), row(qw)],
        out_shape=[jax.ShapeDtypeStruct((m, dm), F32),
                   jax.ShapeDtypeStruct((m, qw), act_dtype), jax.ShapeDtypeStruct((m, kw), F32),
                   jax.ShapeDtypeStruct((m, kw), F32), jax.ShapeDtypeStruct((m, qw), act_dtype)],
        compiler_params=_cparams(("parallel",)),
        name="mid",
    )(x, y, ga, ob, *weights, cos_t, sin_t)


def _rope_tables(pos):
    half = SWA_HEAD_DIM // 2
    inv_freq = ROPE_THETA ** (-np.arange(half, dtype=np.float64) / half)
    ang = np.asarray(pos, np.float64)[:, None] * inv_freq[None, :]
    cos, sin = np.cos(ang), np.sin(ang)
    cos_t = np.concatenate([cos, cos, cos, cos], axis=1).astype(np.float32)
    sin_t = np.concatenate([-sin, sin, -sin, sin], axis=1).astype(np.float32)
    return jnp.asarray(cos_t), jnp.asarray(sin_t)


def _attn_prompt_kernel(sink_ref, q_ref, kc_ref, kp_ref, vc_ref, vp_ref, gate_ref, h_ref, wout_ref,
                        y_ref, p_ref, o_ref, *, nqb):
    i = pl.program_id(1)
    w = SWA_WINDOW
    hd = SWA_HEAD_DIM
    npair = SWA_GROUP // 2
    lane2 = lax.broadcasted_iota(jnp.int32, (2 * w, LANES), 1)
    low = lane2 < hd
    rr = lax.broadcasted_iota(jnp.int32, (w, w), 0)
    cc = lax.broadcasted_iota(jnp.int32, (w, w), 1)
    tri = cc <= rr
    low_w = cc < hd
    r4 = lax.broadcasted_iota(jnp.int32, (4 * w, LANES), 0)
    c4 = lax.broadcasted_iota(jnp.int32, (4 * w, LANES), 1)
    den_cols = ((r4 < 2 * w) == (c4 < hd)).astype(BF16)
    units = [(jb, kv) for jb in range(nqb) for kv in range(SWA_KV_HEADS)]

    def scores(jb, kv):
        rows = slice(jb * w, (jb + 1) * w)
        if jb == 0:
            k_prev, v_prev = kp_ref[...], vp_ref[...]
        else:
            prev_rows = slice((jb - 1) * w, jb * w)
            k_prev, v_prev = kc_ref[prev_rows, :], vc_ref[prev_rows, :]
        kcat = jnp.concatenate([k_prev, kc_ref[rows, :]], axis=0)
        vcat = jnp.concatenate([v_prev, vc_ref[rows, :]], axis=0)
        own = low if kv == 0 else jnp.logical_not(low)
        k_own = jnp.where(own, kcat, 0.0)
        v_own = jnp.where(own, vcat, 0.0)
        k_oth = pltpu.roll(k_own, hd, 1)
        v_oth = pltpu.roll(v_own, hd, 1)
        k_lo, k_hi = (k_own, k_oth) if kv == 0 else (k_oth, k_own)
        v_lo, v_hi = (v_own, v_oth) if kv == 0 else (v_oth, v_own)
        k_rhs = jnp.concatenate([k_lo, k_hi], axis=0).astype(BF16)
        v_rhs = jnp.concatenate([jnp.concatenate([v_lo, v_hi], axis=0).astype(BF16), den_cols], axis=1)
        qs = jnp.concatenate([q_ref[rows, (kv * npair + pr) * LANES:(kv * npair + pr + 1) * LANES]
                              for pr in range(npair)], axis=0)
        return _dot_nt(qs, k_rhs), v_rhs

    def softmax_pv(kv, first, buf, s_all, v_rhs):
        sink_terms = []
        for pr in range(npair):
            prow = slice(pr * w, (pr + 1) * w)
            pair_terms = []
            for e in range(2):
                s_prev = s_all[prow, (2 * e) * w:(2 * e + 1) * w]
                s_cur = s_all[prow, (2 * e + 1) * w:(2 * e + 2) * w]
                if first:
                    s_prev = jnp.where(i > 0, s_prev, -jnp.inf)
                sc = jnp.where(tri, s_cur, s_prev)
                sink = sink_ref[2 * (kv * npair + pr) + e] * LOG2E
                mx = jnp.maximum(jnp.max(sc, axis=-1, keepdims=True), sink)
                pe = jnp.exp2(sc - mx)
                p_ref[buf, prow, (2 * e) * w:(2 * e + 1) * w] = jnp.where(tri, 0.0, pe).astype(BF16)
                p_ref[buf, prow, (2 * e + 1) * w:(2 * e + 2) * w] = jnp.where(tri, pe, 0.0).astype(BF16)
                pair_terms.append(jnp.exp2(sink - mx))
            sink_terms.append(pair_terms)
        return _dot(p_ref[buf], v_rhs), sink_terms

    def normalise(jb, kv, o_ext, sink_terms):
        rows = slice(jb * w, (jb + 1) * w)
        for pr in range(npair):
            prow = slice(pr * w, (pr + 1) * w)
            cols = slice((kv * npair + pr) * LANES, (kv * npair + pr + 1) * LANES)
            st = jnp.where(low_w, sink_terms[pr][0], sink_terms[pr][1])
            o = o_ext[prow, :LANES] / (o_ext[prow, LANES:] + st)
            o_ref[rows, cols] = o.astype(o_ref.dtype)

    def project(rows):
        og = o_ref[rows, :].astype(F32) * gate_ref[rows, :].astype(F32)
        y_ref[rows, :] = h_ref[rows, :] + _dot(og.astype(BF16), wout_ref[...])

    group = 2
    pending = scores(*units[0])
    unfinished = None
    for n, (jb, kv) in enumerate(units):
        current = pending
        if n + 1 < len(units):
            pending = scores(*units[n + 1])
        result = softmax_pv(kv, jb == 0, n % 2, *current)
        if unfinished is not None:
            normalise(*unfinished)
            done_jb, done_kv = unfinished[:2]
            if done_kv == SWA_KV_HEADS - 1 and (done_jb + 1) % group == 0:
                project(slice((done_jb + 1 - group) * w, (done_jb + 1) * w))
        unfinished = (jb, kv) + result
    normalise(*unfinished)
    project(slice((nqb - group) * w, nqb * w))


def _attn_prompt(sinks, q, k, v, gate, h, w_out, nseq, seqlen):
    w = SWA_WINDOW
    nqb = 8
    tm = nqb * w
    assert seqlen % tm == 0
    nblk = seqlen // tm
    qw = q.shape[1]
    dm = h.shape[1]
    row = lambda n: pl.BlockSpec((tm, n), lambda b, i: (b * nblk + i, 0))
    prev = lambda n: pl.BlockSpec((w, n), lambda b, i: (jnp.maximum((b * nblk + i) * nqb - 1, 0), 0))
    smem = pl.BlockSpec(memory_space=pltpu.SMEM)
    return pl.pallas_call(
        functools.partial(_attn_prompt_kernel, nqb=nqb),
        grid=(nseq, nblk),
        in_specs=[smem, row(qw), row(LANES), prev(LANES), row(LANES), prev(LANES), row(qw), row(dm),
                  _full(w_out.shape)],
        out_specs=row(dm),
        out_shape=jax.ShapeDtypeStruct(h.shape, F32),
        scratch_shapes=[pltpu.VMEM((2, 4 * w, 4 * w), BF16), pltpu.VMEM((tm, qw), BF16)],
        compiler_params=_cparams(("parallel", "arbitrary")),
        name="attn_prompt",
    )(sinks, q, k, k, v, v, gate, h, w_out)


def _attn_sample_kernel(sink_ref, q_ref, kn_ref, vn_ref, kc_ref, vc_ref, o_ref, ko_ref, vo_ref, q2_ref, *, seqlen):
    nq = SWA_HEADS * seqlen
    hd = SWA_HEAD_DIM
    heads = [(kv, g) for kv in range(SWA_KV_HEADS) for g in range(SWA_GROUP)]

    def stacked(kv, g):
        r0 = (kv * SWA_GROUP + g) * seqlen
        return slice(r0, r0 + seqlen), slice(kv * hd, (kv + 1) * hd)

    q2_ref[...] = jnp.zeros(q2_ref.shape, q2_ref.dtype)
    for kv, g in heads:
        rows, lanes = stacked(kv, g)
        h = kv * SWA_GROUP + g
        q2_ref[:, rows, lanes] = q_ref[:, :, h * hd:(h + 1) * hd]
    ncache = kc_ref.shape[1]
    t_row = lax.broadcasted_iota(jnp.int32, (nq, ncache), 0) % seqlen
    c_col = lax.broadcasted_iota(jnp.int32, (nq, ncache), 1)
    cache_ok = c_col > t_row - (SWA_WINDOW - ncache)
    nnew = kn_ref.shape[1]
    t_row_n = lax.broadcasted_iota(jnp.int32, (nq, nnew), 0) % seqlen
    n_col = lax.broadcasted_iota(jnp.int32, (nq, nnew), 1)
    new_ok = n_col <= t_row_n
    sink = (sink_ref[...] * LOG2E)[None, :, 0:1]
    bqk = lambda a, b: lax.dot_general(a, b, (((2,), (2,)), ((0,), (0,))), preferred_element_type=F32)
    bpv = lambda a, b: lax.dot_general(a, b, (((2,), (1,)), ((0,), (0,))), preferred_element_type=F32)
    q = q2_ref[...].astype(BF16)
    kc = kc_ref[...]
    vc = vc_ref[...]
    kn = kn_ref[...]
    vn = vn_ref[...]
    sc = jnp.where(cache_ok[None], bqk(q, kc.astype(BF16)), -jnp.inf)
    sn = jnp.where(new_ok[None], bqk(q, kn.astype(BF16)), -jnp.inf)
    mx = jnp.maximum(jnp.maximum(jnp.max(sc, axis=-1, keepdims=True), jnp.max(sn, axis=-1, keepdims=True)), sink)
    pc = jnp.exp2(sc - mx)
    pn = jnp.exp2(sn - mx)
    den = jnp.sum(pc, axis=-1, keepdims=True) + jnp.sum(pn, axis=-1, keepdims=True) + jnp.exp2(sink - mx)
    inv = 1.0 / den
    o2 = bpv((pc * inv).astype(BF16), vc.astype(BF16)) + bpv((pn * inv).astype(BF16), vn.astype(BF16))
    for kv, g in heads:
        rows, lanes = stacked(kv, g)
        h = kv * SWA_GROUP + g
        o_ref[:, :, h * hd:(h + 1) * hd] = o2[:, rows, lanes]
    keep = ncache - seqlen
    ko_ref[:, 0:keep, :] = kc[:, seqlen:ncache, :]
    ko_ref[:, keep:ncache, :] = kn[:, 0:seqlen, :]
    vo_ref[:, 0:keep, :] = vc[:, seqlen:ncache, :]
    vo_ref[:, keep:ncache, :] = vn[:, 0:seqlen, :]


def _attn_sample(sink_rows, q, kn, vn, kc, vc):
    n, seqlen, _ = q.shape
    nseq = 16
    assert n % nseq == 0
    blk = lambda a: pl.BlockSpec((nseq,) + a.shape[1:], lambda i: (i, 0, 0))
    return pl.pallas_call(
        functools.partial(_attn_sample_kernel, seqlen=seqlen),
        grid=(n // nseq,),
        in_specs=[_full(sink_rows.shape), blk(q), blk(kn), blk(vn), blk(kc), blk(vc)],
        out_specs=[blk(q), blk(kc), blk(vc)],
        out_shape=[jax.ShapeDtypeStruct(q.shape, F32), jax.ShapeDtypeStruct(kc.shape, F32),
                   jax.ShapeDtypeStruct(vc.shape, F32)],
        scratch_shapes=[pltpu.VMEM((nseq, SWA_HEADS * seqlen, SWA_KV_HEADS * SWA_HEAD_DIM), F32)],
        compiler_params=_cparams(("parallel",)),
        name="attn_sample",
    )(sink_rows, q, kn, vn, kc, vc)


def _odd_out_kernel(h_ref, o_ref, gate_ref, w_ref, y_ref):
    og = o_ref[...].astype(F32) * gate_ref[...].astype(F32)
    y_ref[...] = h_ref[...] + _dot(og.astype(BF16), w_ref[...])


def _odd_out(h, o, gate, w):
    m, dm = h.shape
    tm = _row_tile(m)
    row = lambda n: pl.BlockSpec((tm, n), lambda i: (i, 0))
    return pl.pallas_call(
        _odd_out_kernel,
        grid=(m // tm,),
        in_specs=[row(dm), row(o.shape[1]), row(gate.shape[1]), _full(w.shape)],
        out_specs=row(dm),
        out_shape=jax.ShapeDtypeStruct((m, dm), F32),
        compiler_params=_cparams(("parallel",)),
        name="odd_out",
    )(h, o, gate, w)


def _trunk(x, s5_x0, gla_s0, pe, po, pos, nseq, seqlen, act_dtype):
    u2, ga, q, k, v, la, gb = _even_in(x, pe['norm_g'], pe['w_in'], pe['wgate'], pe['bgate'], act_dtype,
                                       _s5_chunk(seqlen))
    y4, s5_fin = _s5_branch(u2, s5_x0, pe['s5_ops'], pe['d'], nseq, seqlen, act_dtype)
    ob, gla_fin = _gla(q, k, v, la, gb, gla_s0, pe['gla_norm_g'], nseq, seqlen, act_dtype)
    cos_t, sin_t = _rope_tables(pos)
    h, q1, k1, v1, gate = _mid(x, y4, ga, ob, pe, po, cos_t, sin_t, act_dtype)
    return h, q1, k1, v1, gate, s5_fin, gla_fin


def kernel(x_prompt, x_sample, state_s5_re, state_s5_im, state_gla, cache_swa_k, cache_swa_v,
           even_norm_g, even_w_in, s5_lambda_re, s5_lambda_im, s5_log_dt, s5_b_re, s5_b_im,
           s5_c_re, s5_c_im, s5_d, s5_w_glu, s5_b_glu, gla_w_gate, gla_b_gate, gla_norm_g,
           even_w_out, odd_norm_g, odd_w_in, swa_q_norm_g, swa_k_norm_g, swa_sinks, odd_w_out):
    nb, seq, dm = x_prompt.shape
    ns, dseq, _ = x_sample.shape
    ng = s5_lambda_re.shape[1]
    no = ng // OCT
    xp = x_prompt.reshape(nb * seq, dm)
    xs = x_sample.reshape(ns * dseq, dm)

    i = 0
    assert ng * S5_GROUP == S5_WIDTH
    pad_rank = LANES - GLA_GATE_RANK
    pe = {
        'norm_g': even_norm_g[i][None, :],
        'w_in': even_w_in[i].astype(BF16),
        'wgate': jnp.pad(gla_w_gate[i], ((0, pad_rank), (0, 0))).astype(BF16),
        'bgate': gla_b_gate[i][None, :],
        's5_ops': _s5_params(s5_lambda_re[i], s5_lambda_im[i], s5_log_dt[i], s5_b_re[i], s5_b_im[i],
                             s5_c_re[i], s5_c_im[i]),
        'gla_norm_g': gla_norm_g[i][None, :],
        'd': s5_d[i],
        'wglu': s5_w_glu[i].astype(BF16),
        'bglu': s5_b_glu[i][None, :],
        'wo': even_w_out[i].astype(BF16),
    }
    po = {
        'norm_g': odd_norm_g[i][None, :],
        'w_in': odd_w_in[i].astype(BF16),
        'gq': jnp.tile(swa_q_norm_g[i], MXU_TILE // SWA_HEAD_DIM)[None, :],
        'gk': jnp.tile(swa_k_norm_g[i], LANES // SWA_HEAD_DIM)[None, :],
    }
    w_out = odd_w_out[i].astype(BF16)
    sinks = swa_sinks[i]
    kvw = SWA_KV_HEADS * SWA_HEAD_DIM

    s5_zero = jnp.zeros((no, nb, 2 * OCT * S5_STATE), F32)
    gla_zero = jnp.zeros((nb, GLA_HEADS, GLA_HEAD_K, GLA_HEAD_V), F32)
    hp, q, k, v, gate, s5_p, gla_p = _trunk(xp, s5_zero, gla_zero, pe, po, np.arange(seq), nb, seq, BF16)
    s5r_p, s5i_p = _s5_state_out(s5_p)
    y_prompt = _attn_prompt(sinks, q, k, v, gate, hp, w_out, nb, seq).reshape(nb, seq, dm)
    cache_len = min(SWA_WINDOW, seq)
    tail = lambda a: (a.reshape(nb, seq, kvw)[:, seq - cache_len:]
                      .reshape(1, nb, cache_len, SWA_KV_HEADS, SWA_HEAD_DIM))
    swk_p, swv_p = tail(k), tail(v)

    pos_s = np.tile(PAST_LEN + np.arange(dseq), ns)
    s5_init = _s5_state_in(state_s5_re[i], state_s5_im[i], no)
    hs, q, k, v, gate, s5_s, gla_s = _trunk(xs, s5_init, state_gla[i], pe, po, pos_s, ns, dseq, F32)
    s5r_s, s5i_s = _s5_state_out(s5_s)
    ncache = cache_swa_k.shape[2]
    sink_rows = jnp.broadcast_to(jnp.repeat(sinks, dseq)[:, None], (SWA_HEADS * dseq, LANES))
    kn = k.reshape(ns, dseq, kvw)
    vn = v.reshape(ns, dseq, kvw)
    npad = BF16_ROWS - dseq
    kn_pad = jnp.pad(kn, ((0, 0), (0, npad), (0, 0)))
    vn_pad = jnp.pad(vn, ((0, 0), (0, npad), (0, 0)))
    kc = cache_swa_k[i].reshape(ns, ncache, kvw)
    vc = cache_swa_v[i].reshape(ns, ncache, kvw)
    o, kc_new, vc_new = _attn_sample(sink_rows, q.reshape(ns, dseq, -1), kn_pad, vn_pad, kc, vc)
    y_sample = _odd_out(hs, o.reshape(ns * dseq, -1), gate, w_out).reshape(ns, dseq, dm)
    swk_s = kc_new.reshape(1, ns, ncache, SWA_KV_HEADS, SWA_HEAD_DIM)
    swv_s = vc_new.reshape(1, ns, ncache, SWA_KV_HEADS, SWA_HEAD_DIM)

    return (y_prompt, y_sample,
            s5r_p, s5i_p, gla_p[None], swk_p, swv_p,
            s5r_s, s5i_s, gla_s[None], swk_s, swv_s)
```

```python
import functools
import math

import jax
import jax.numpy as jnp
import numpy as np
from jax import lax
from jax.experimental import pallas as pl
from jax.experimental.pallas import tpu as pltpu

F32 = jnp.float32
BF16 = jnp.bfloat16

PAST_LEN = 8192
NORM_EPS = 1e-6
S5_GROUP = 16
S5_STATE = 64
S5_CHUNK = 16
GLA_HEADS = 4
GLA_HEAD_K = 64
GLA_HEAD_V = 128
GLA_GATE_RANK = 16
GLA_GATE_TAU = 16.0
GLA_CHUNK = 64
S5_WIDTH = 512
GLA_K_WIDTH = GLA_HEADS * GLA_HEAD_K
GLA_V_WIDTH = GLA_HEADS * GLA_HEAD_V
SWA_HEADS = 16
SWA_KV_HEADS = 2
SWA_GROUP = SWA_HEADS // SWA_KV_HEADS
SWA_HEAD_DIM = 64
SWA_WINDOW = 128
ROPE_THETA = 10000.0
LOG2E = math.log2(math.e)
LANES = 128
BF16_ROWS = 16
MXU_TILE = 256
OCT = LANES // S5_GROUP
ROW_TILE = 512
VMEM_LIMIT = 48 * 1024 * 1024
C_U = 0
C_GATE_A = C_U + S5_WIDTH
C_Q = C_GATE_A + S5_WIDTH
C_K = C_Q + GLA_K_WIDTH
C_V = C_K + GLA_K_WIDTH
C_CODE = C_V + GLA_V_WIDTH
C_GATE_B = C_CODE + GLA_GATE_RANK
C_END = C_GATE_B + GLA_V_WIDTH


def _cparams(sem):
    return pltpu.CompilerParams(dimension_semantics=sem, vmem_limit_bytes=VMEM_LIMIT)


def _full(shape):
    n = len(shape)
    return pl.BlockSpec(shape, lambda *_: (0,) * n)


def _dot(a, b):
    return jnp.dot(a, b, preferred_element_type=F32)


def _dot_nt(a, b):
    return lax.dot_general(a, b, (((1,), (1,)), ((), ())), preferred_element_type=F32)


def _dot_tn(a, b):
    return lax.dot_general(a, b, (((0,), (0,)), ((), ())), preferred_element_type=F32)


def _split_bf16(x):
    hi = x.astype(BF16)
    lo = (x - hi.astype(F32)).astype(BF16)
    return hi, lo


def _rms_rows(x, g):
    return x * lax.rsqrt(jnp.mean(x * x, axis=-1, keepdims=True) + NORM_EPS) * g


def _sigmoid(x):
    return 1.0 / (1.0 + jnp.exp(-x))


def _silu(x):
    return x * _sigmoid(x)


def _row_tile(m, tile=ROW_TILE):
    return tile if m % tile == 0 else m


def _even_in_kernel(x_ref, g_ref, w_ref, wgate_ref, bgate_ref,
                    u2_ref, ga_ref, q_ref, k_ref, v_ref, la_ref, gb_ref, uscr_ref, wgb_ref, *, t):
    @pl.when(pl.program_id(0) == 0)
    def _():
        wgb_ref[...] = w_ref[:, C_GATE_B:C_END]

    xb = _rms_rows(x_ref[...], g_ref[...]).astype(BF16)

    def proj(lo, hi):
        return _dot(xb, w_ref[:, lo:hi])

    u = proj(C_U, C_GATE_A)
    nrow = u.shape[0] // t
    for o in range(u2_ref.shape[0]):
        uscr_ref[o] = u[:, o * LANES:(o + 1) * LANES]
        for tt in range(t):
            piece = uscr_ref[o, pl.ds(tt, nrow, stride=t), :]
            u2_ref[o, :, tt * LANES:(tt + 1) * LANES] = piece.astype(u2_ref.dtype)
    ga_ref[...] = _silu(proj(C_GATE_A, C_Q)).astype(ga_ref.dtype)
    q_ref[...] = (proj(C_Q, C_K) * (GLA_HEAD_K ** -0.5)).astype(q_ref.dtype)
    k_ref[...] = proj(C_K, C_V).astype(k_ref.dtype)
    v_ref[...] = proj(C_V, C_CODE).astype(v_ref.dtype)
    gb_ref[...] = _silu(_dot(xb, wgb_ref[...])).astype(gb_ref.dtype)
    a_low = proj(C_CODE, C_CODE + LANES)
    logit = _dot(a_low.astype(BF16), wgate_ref[...]) + bgate_ref[...]
    log_sig = jnp.minimum(logit, 0.0) - jnp.log1p(jnp.exp(-jnp.abs(logit)))
    la_ref[...] = log_sig * (1.0 / GLA_GATE_TAU)


def _even_in(x, g, w, wgate, bgate, act_dtype, t):
    m, d = x.shape
    tm = _row_tile(m, 2 * ROW_TILE) if m > ROW_TILE else m
    row = lambda n: pl.BlockSpec((tm, n), lambda i: (i, 0))
    assert w.shape[1] == C_END
    no = S5_WIDTH // LANES
    chunk = pl.BlockSpec((no, tm // t, t * LANES), lambda i: (0, i, 0))
    outs = [(S5_WIDTH, act_dtype), (GLA_K_WIDTH, act_dtype), (GLA_K_WIDTH, act_dtype), (GLA_V_WIDTH, act_dtype),
            (GLA_K_WIDTH, F32), (GLA_V_WIDTH, act_dtype)]
    return pl.pallas_call(
        functools.partial(_even_in_kernel, t=t),
        grid=(m // tm,),
        in_specs=[row(d), _full(g.shape), _full(w.shape), _full(wgate.shape), _full(bgate.shape)],
        out_specs=[chunk] + [row(n) for n, _ in outs],
        out_shape=[jax.ShapeDtypeStruct((no, m // t, t * LANES), act_dtype)]
        + [jax.ShapeDtypeStruct((m, n), dt) for n, dt in outs],
        scratch_shapes=[pltpu.VMEM((no, tm, LANES), F32), pltpu.VMEM((d, GLA_V_WIDTH), BF16)],
        compiler_params=_cparams(("arbitrary",)),
        name="even_in",
    )(x, g, w, wgate, bgate)


def _group_mask(shape, row_span, col_span):
    rg = (lax.broadcasted_iota(jnp.int32, shape, 0) // row_span) % OCT
    cg = (lax.broadcasted_iota(jnp.int32, shape, 1) // col_span) % OCT
    return rg == cg


def _s5_state_kernel(u_ref, bre_ref, bim_ref, x0_ref, are_ref, aim_ref, xs_ref, xf_ref, loc_ref, *, nseq, nchunks):
    hw = OCT * S5_STATE
    kk = u_ref.shape[2]
    bp = jnp.concatenate([bre_ref[0]] * OCT + [bim_ref[0]] * OCT, axis=1)
    bp = jnp.where(_group_mask((kk, 2 * hw), S5_GROUP, S5_STATE), bp, 0.0).astype(BF16)
    loc_ref[...] = _dot(u_ref[0], bp)
    a_re = are_ref[0]
    a_im = aim_ref[0]
    if nchunks == 1:
        x0 = x0_ref[0]
        xr, xi = x0[:, :hw], x0[:, hw:]
        loc = loc_ref[...]
        xf_ref[0, :, :hw] = a_re * xr - a_im * xi + loc[:, :hw]
        xf_ref[0, :, hw:] = a_re * xi + a_im * xr + loc[:, hw:]
        xs_ref[0] = x0.astype(xs_ref.dtype)
    else:
        def body(j, carry):
            new = []
            for b in range(nseq):
                xr, xi = carry[b]
                row = pl.ds(b * nchunks + j, 1)
                lr = loc_ref[row, :hw]
                li = loc_ref[row, hw:]
                loc_ref[row, :hw] = xr
                loc_ref[row, hw:] = xi
                new.append((a_re * xr - a_im * xi + lr, a_re * xi + a_im * xr + li))
            return tuple(new)

        init = tuple((x0_ref[0, b:b + 1, :hw], x0_ref[0, b:b + 1, hw:]) for b in range(nseq))
        fin = lax.fori_loop(0, nchunks, body, init, unroll=4)
        for b in range(nseq):
            xf_ref[0, b:b + 1, :hw] = fin[b][0]
            xf_ref[0, b:b + 1, hw:] = fin[b][1]
        xs_ref[0] = loc_ref[...].astype(xs_ref.dtype)


def _s5_state(u2, bre, bim, x0, are, aim, nseq, nchunks):
    no, r, kk = u2.shape
    sw = x0.shape[2]
    blk = lambda a: pl.BlockSpec((1,) + a.shape[1:], lambda o: (o,) + (0,) * (a.ndim - 1))
    return pl.pallas_call(
        functools.partial(_s5_state_kernel, nseq=nseq, nchunks=nchunks),
        grid=(no,),
        in_specs=[blk(u2), blk(bre), blk(bim), blk(x0), blk(are), blk(aim)],
        out_specs=[pl.BlockSpec((1, r, sw), lambda o: (o, 0, 0)), blk(x0)],
        out_shape=[jax.ShapeDtypeStruct((no, r, sw), BF16), jax.ShapeDtypeStruct(x0.shape, F32)],
        scratch_shapes=[pltpu.VMEM((r, sw), F32)],
        compiler_params=_cparams(("parallel",)),
        name="s5_state",
    )(u2, bre, bim, x0, are, aim)


def _dot_nt_f32(a, b):
    a_hi, a_lo = _split_bf16(a)
    b_hi, b_lo = _split_bf16(b)
    return _dot_nt(a_hi, b_hi) + _dot_nt(a_hi, b_lo) + _dot_nt(a_lo, b_hi)


def _s5_out_kernel(u_ref, xs_ref, bre_ref, bim_ref, cre_ref, cim_ref, zre_ref, zim_ref, d_ref, y_ref, yscr_ref):
    r, kk = u_ref.shape[1], u_ref.shape[2]
    t = kk // LANES
    sw = xs_ref.shape[2]
    ntile = kk // MXU_TILE
    taps = _dot_nt_f32(bre_ref[0], zre_ref[0]) - _dot_nt_f32(bim_ref[0], zim_ref[0])
    tmask = _group_mask((LANES, LANES), S5_GROUP, S5_GROUP)

    def tap(lag):
        if lag < 0:
            return jnp.zeros((LANES, LANES), F32)
        s = t - 1 - lag
        return jnp.where(tmask, taps[s * LANES:(s + 1) * LANES], 0.0)

    wts = [jnp.concatenate([jnp.concatenate([tap(2 * d), tap(2 * d + 1)], axis=1),
                            jnp.concatenate([tap(2 * d - 1), tap(2 * d)], axis=1)], axis=0).astype(BF16)
           for d in range(ntile)]
    cpt = jnp.concatenate([cre_ref[0]] * OCT + [cim_ref[0]] * OCT, axis=1)
    cpt = jnp.where(_group_mask((kk, sw), S5_GROUP, S5_STATE), cpt, 0.0).astype(BF16)
    xs = xs_ref[0]
    skip = jnp.concatenate([d_ref[0]] * (MXU_TILE // LANES), axis=1)
    for n in range(ntile):
        cols = slice(n * MXU_TILE, (n + 1) * MXU_TILE)
        acc = _dot_nt(xs, cpt[cols]) + skip * u_ref[0, :, cols].astype(F32)
        for k in range(n + 1):
            acc = acc + _dot(u_ref[0, :, k * MXU_TILE:(k + 1) * MXU_TILE], wts[n - k])
        for e in range(MXU_TILE // LANES):
            yscr_ref[pl.ds(2 * n + e, r, stride=t), :] = acc[:, e * LANES:(e + 1) * LANES]
    y_ref[0] = yscr_ref[...].astype(y_ref.dtype)


def _s5_out(u2, xs, bre, bim, cre, cim, zre, zim, d, out_dtype):
    no, r, kk = u2.shape
    m = r * (kk // LANES)
    blk = lambda a: pl.BlockSpec((1,) + a.shape[1:], lambda o: (o,) + (0,) * (a.ndim - 1))
    ops = (u2, xs, bre, bim, cre, cim, zre, zim, d)
    return pl.pallas_call(
        _s5_out_kernel,
        grid=(no,),
        in_specs=[blk(a) for a in ops],
        out_specs=pl.BlockSpec((1, m, LANES), lambda o: (o, 0, 0)),
        out_shape=jax.ShapeDtypeStruct((no, m, LANES), out_dtype),
        scratch_shapes=[pltpu.VMEM((m, LANES), F32)],
        compiler_params=_cparams(("parallel",)),
        name="s5_out",
    )(*ops)


def _s5_params(lam_re, lam_im, log_dt, b_re, b_im, c_re, c_im):
    t = S5_CHUNK
    ng = lam_re.shape[0]
    no = ng // OCT
    dt = jnp.exp(log_dt)[:, None]
    a = lam_re * dt
    b = lam_im * dt
    n = jnp.arange(t + 1, dtype=F32)[None, :, None]
    mag = jnp.exp(n * a[:, None, :])
    pw_re = mag * jnp.cos(n * b[:, None, :])
    pw_im = mag * jnp.sin(n * b[:, None, :])
    em1_re = jnp.expm1(a) * jnp.cos(b) - 2.0 * jnp.sin(0.5 * b) ** 2
    em1_im = jnp.exp(a) * jnp.sin(b)
    den = lam_re * lam_re + lam_im * lam_im
    z_re = (em1_re * lam_re + em1_im * lam_im) / den
    z_im = (em1_im * lam_re - em1_re * lam_im) / den
    bb_re = z_re[..., None] * b_re - z_im[..., None] * b_im
    bb_im = z_re[..., None] * b_im + z_im[..., None] * b_re

    def per_token(pw):
        return pw.reshape(no, OCT, t, S5_STATE).transpose(0, 2, 1, 3)[:, :, :, None, :]

    def per_group(w):
        return w.reshape(no, 1, OCT, S5_GROUP, S5_STATE)

    rows = lambda w: w.reshape(no, t * LANES, S5_STATE)
    bt_re, bt_im = per_group(bb_re.transpose(0, 2, 1)), per_group(bb_im.transpose(0, 2, 1))
    r_re, r_im = per_token(pw_re[:, t - 1::-1]), per_token(pw_im[:, t - 1::-1])
    bre = rows(r_re * bt_re - r_im * bt_im)
    bim = rows(r_re * bt_im + r_im * bt_re)
    o_re, o_im = per_token(pw_re[:, 1:]), per_token(pw_im[:, 1:])
    cg_re, cg_im = per_group(c_re), per_group(c_im)
    cre = rows(cg_re * o_re - cg_im * o_im)
    cim = rows(-(cg_re * o_im + cg_im * o_re))
    return bre, bim, cre, cim, c_re.reshape(no, LANES, S5_STATE), c_im.reshape(no, LANES, S5_STATE), pw_re, pw_im


def _s5_chunk(seqlen):
    return math.gcd(seqlen, S5_CHUNK)


def _s5_branch(u2, x0, ops, d, nseq, seqlen, out_dtype):
    bre, bim, cre, cim, zre, zim, pw_re, pw_im = ops
    no = u2.shape[0]
    t = _s5_chunk(seqlen)
    kk = t * LANES
    nchunks = seqlen // t
    are = pw_re[:, t].reshape(no, 1, OCT * S5_STATE)
    aim = pw_im[:, t].reshape(no, 1, OCT * S5_STATE)
    tail = S5_CHUNK * LANES - kk
    bre, bim = bre[:, tail:], bim[:, tail:]
    xs, xf = _s5_state(u2, bre, bim, x0, are, aim, nseq, nchunks)
    return _s5_out(u2, xs, bre, bim, cre[:, :kk], cim[:, :kk], zre, zim, d.reshape(no, 1, LANES), out_dtype), xf


def _s5_state_in(re, im, no):
    nseq = re.shape[0]
    f = lambda a: a.reshape(nseq, no, OCT * S5_STATE).transpose(1, 0, 2)
    return jnp.concatenate([f(re), f(im)], axis=-1)


def _s5_state_out(xf):
    no, nseq, _ = xf.shape
    hw = OCT * S5_STATE
    f = lambda a: a.transpose(1, 0, 2).reshape(1, nseq, no * OCT, S5_STATE)
    return f(xf[:, :, :hw]), f(xf[:, :, hw:])


def _gla_kernel(q_ref, k_ref, v_ref, la_ref, gb_ref, s0_ref, ng_ref, o_ref, sf_ref, st_ref,
                *, nseq, nchunks, c):
    i = pl.program_id(1)

    @pl.when(i == 0)
    def _():
        st_ref[...] = s0_ref[...]

    nh, hk, hv = GLA_HEADS, GLA_HEAD_K, GLA_HEAD_V
    nch = nseq * nchunks
    tm = nch * c
    iota = lambda shape, d: lax.broadcasted_iota(jnp.int32, shape, d)
    ng = ng_ref[...]

    tb = min(tm, MXU_TILE)
    rt, ct = iota((tb, tb), 0), iota((tb, tb), 1)
    tril = ((rt // c == ct // c) & (rt >= ct)).astype(BF16)
    parts = []
    for r0 in range(0, tm, tb):
        la_hi, la_lo = _split_bf16(la_ref[r0:r0 + tb, :])
        parts.append(_dot(tril, la_hi) + _dot(tril, la_lo))
    bcum = jnp.concatenate(parts, axis=0)
    e_hi, e_lo = _split_bf16(jnp.exp(jnp.concatenate([bcum[(ci + 1) * c - 1:(ci + 1) * c] for ci in range(nch)],
                                                     axis=0)))
    pick = (iota((nch, LANES), 0) == iota((nch, LANES), 1)).astype(BF16)
    dec_t = _dot_tn(e_hi, pick) + _dot_tn(e_lo, pick)
    q_all = q_ref[...].astype(F32)
    k_all = k_ref[...].astype(F32)
    q_dec_all = q_all * jnp.exp(bcum)
    k_dec_all = k_all * jnp.exp(-bcum)

    own_k = iota((nh * c, nh * hk), 0) // c == iota((nh * c, nh * hk), 1) // hk
    own_v = iota((nh * c, nh * hv), 0) // c == iota((nh * c, nh * hv), 1) // hv
    causal = iota((c, nh * c), 1) % c <= iota((c, nh * c), 0)
    zero_v = jnp.zeros((hk, hv), F32)

    def intra(ci):
        rows = slice(ci * c, (ci + 1) * c)
        q_dec = q_dec_all[rows].astype(BF16)
        k_dec = k_dec_all[rows]
        b_c = bcum[rows]
        k_tail = (k_all[rows] * jnp.exp(b_c[c - 1:c] - b_c)).astype(BF16)
        v = v_ref[rows, :].astype(F32)
        k_bd = jnp.where(own_k, jnp.concatenate([k_dec] * nh, axis=0), 0.0).astype(BF16)
        v_bd = jnp.where(own_v, jnp.concatenate([v] * nh, axis=0), 0.0).astype(BF16)
        att = jnp.where(causal, _dot_nt(q_dec, k_bd), 0.0)
        o_intra = _dot(att.astype(BF16), v_bd)
        vb = v.astype(BF16)
        kvs = []
        for h0 in range(0, nh, 2):
            kv2 = _dot_tn(k_tail[:, h0 * hk:(h0 + 2) * hk], vb[:, h0 * hv:(h0 + 2) * hv])
            kvs += [kv2[:hk, :hv], kv2[hk:, hv:]]
        return q_dec, o_intra, kvs

    def carry(ci, sts, q_dec, o_intra, kvs):
        st_bd = jnp.concatenate(
            [jnp.concatenate([zero_v] * h + [sts[h]] + [zero_v] * (nh - 1 - h), axis=1) for h in range(nh)],
            axis=0).astype(BF16)
        o = o_intra + _dot(q_dec, st_bd)
        new = [dec_t[h * hk:(h + 1) * hk, ci:ci + 1] * sts[h] + kvs[h] for h in range(nh)]
        return o, new

    def finish(ci, o):
        rows = slice(ci * c, (ci + 1) * c)
        for h in range(nh):
            vs = slice(h * hv, (h + 1) * hv)
            oh = _rms_rows(o[:, vs], ng) * gb_ref[rows, vs].astype(F32)
            o_ref[rows, vs] = oh.astype(o_ref.dtype)

    sts = None
    pending = intra(0)
    unfinished = None
    for ci in range(nch):
        s, first, last = ci // nchunks, ci % nchunks == 0, ci % nchunks == nchunks - 1
        current = pending
        if ci + 1 < nch:
            pending = intra(ci + 1)
        if first:
            sts = [st_ref[s, h] for h in range(nh)]
        o, sts = carry(ci, sts, *current)
        if last:
            for h in range(nh):
                st_ref[s, h] = sts[h]
        if unfinished is not None:
            finish(*unfinished)
        unfinished = (ci, o)
    finish(*unfinished)

    @pl.when(i == pl.num_programs(1) - 1)
    def _():
        sf_ref[...] = st_ref[...]


def _gla(q, k, v, la, gb, s0, ng, nseq_total, seqlen, out_dtype):
    c = math.gcd(seqlen, GLA_CHUNK)
    per_seq = seqlen // c
    if per_seq >= 8:
        nseq, nchunks = 1, next(n for n in (32, 16, 8, 1) if per_seq % n == 0)
    else:
        nseq, nchunks = 16, seqlen // c
    assert nseq_total % nseq == 0 and seqlen % (nchunks * c) == 0
    nblk = seqlen // (nchunks * c)
    tm = nseq * nchunks * c
    row = lambda n: pl.BlockSpec((tm, n), lambda b, i: (b * nblk + i, 0))
    st_spec = pl.BlockSpec((nseq, GLA_HEADS, GLA_HEAD_K, GLA_HEAD_V), lambda b, i: (b, 0, 0, 0))
    m = q.shape[0]
    return pl.pallas_call(
        functools.partial(_gla_kernel, nseq=nseq, nchunks=nchunks, c=c),
        grid=(nseq_total // nseq, nblk),
        in_specs=[row(GLA_K_WIDTH), row(GLA_K_WIDTH), row(GLA_V_WIDTH), row(GLA_K_WIDTH), row(GLA_V_WIDTH),
                  st_spec, _full(ng.shape)],
        out_specs=[row(GLA_V_WIDTH), st_spec],
        out_shape=[jax.ShapeDtypeStruct((m, GLA_V_WIDTH), out_dtype), jax.ShapeDtypeStruct(s0.shape, F32)],
        scratch_shapes=[pltpu.VMEM((nseq, GLA_HEADS, GLA_HEAD_K, GLA_HEAD_V), F32)],
        compiler_params=_cparams(("parallel", "arbitrary")),
        name="gla",
    )(q, k, v, la, gb, s0, ng)


def _gelu_tanh(x):
    return 0.5 * x * (1.0 + jnp.tanh(math.sqrt(2.0 / math.pi) * (x + 0.044715 * (x * x * x))))


def _head_ones(n):
    r = lax.broadcasted_iota(jnp.int32, (n, n), 0) // SWA_HEAD_DIM
    c = lax.broadcasted_iota(jnp.int32, (n, n), 1) // SWA_HEAD_DIM
    return (r == c).astype(BF16)


def _rope_block(x, cos_t, sin_t, upper):
    half = SWA_HEAD_DIM // 2
    swapped = jnp.where(upper, pltpu.roll(x, half, 1), pltpu.roll(x, LANES - half, 1))
    return x * cos_t + swapped * sin_t


def _mid_kernel(x_ref, y_ref, ga_ref, ob_ref, wglu_ref, bglu_ref, wo_ref,
                g_ref, w_ref, gq_ref, gk_ref, cos_ref, sin_ref, h_ref, q_ref, k_ref, v_ref, gate_ref):
    tm = x_ref.shape[0]
    lane = lax.broadcasted_iota(jnp.int32, (tm, LANES), 1)
    upper = (lane & (SWA_HEAD_DIM // 2)) != 0
    ones4 = _head_ones(MXU_TILE)
    inv_d = 1.0 / SWA_HEAD_DIM
    qw = SWA_HEADS * SWA_HEAD_DIM
    kw = SWA_KV_HEADS * SWA_HEAD_DIM
    nblk = qw // MXU_TILE
    no = y_ref.shape[0]

    na = ga_ref.shape[1]
    mix_b = _dot(ob_ref[...].astype(BF16), wo_ref[na:, :])
    z = _gelu_tanh(jnp.concatenate([y_ref[o].astype(F32) for o in range(no)], axis=1))
    z = z * _sigmoid(_dot(z.astype(BF16), wglu_ref[...]) + bglu_ref[...])
    out_a = z * ga_ref[...].astype(F32)
    h = x_ref[...] + (_dot(out_a.astype(BF16), wo_ref[:na, :]) + mix_b)
    h_ref[...] = h

    xb = _rms_rows(h, g_ref[...]).astype(BF16)
    cos_t = cos_ref[...]
    sin_t = sin_ref[...]

    def gate_finish(gate):
        gate_ref[...] = _silu(gate).astype(gate_ref.dtype)

    def q_finish(j, q):
        ss = _dot((q * q).astype(BF16), ones4)
        qn = q * lax.rsqrt(ss * inv_d + NORM_EPS) * gq_ref[...]
        for e in range(MXU_TILE // LANES):
            cols = slice(j * MXU_TILE + e * LANES, j * MXU_TILE + (e + 1) * LANES)
            qe = _rope_block(qn[:, e * LANES:(e + 1) * LANES], cos_t, sin_t, upper)
            q_ref[:, cols] = (qe * (SWA_HEAD_DIM ** -0.5 * LOG2E)).astype(q_ref.dtype)

    def kv_finish(kv):
        k = kv[:, :kw]
        ss = _dot((k * k).astype(BF16), ones4[:kw, :kw])
        kn = k * lax.rsqrt(ss * inv_d + NORM_EPS) * gk_ref[...]
        k_ref[...] = _rope_block(kn, cos_t, sin_t, upper)
        v_ref[...] = kv[:, kw:]

    work = [(slice(j * MXU_TILE, (j + 1) * MXU_TILE), functools.partial(q_finish, j)) for j in range(nblk)]
    work += [(slice(qw, qw + 2 * kw), kv_finish)]
    work += [(slice(qw + 2 * kw, None), gate_finish)]
    pending = _dot(xb, w_ref[:, work[0][0]])
    for n, (_, finish) in enumerate(work):
        current = pending
        if n + 1 < len(work):
            pending = _dot(xb, w_ref[:, work[n + 1][0]])
        finish(current)


def _mid(x, y, ga, ob, pe, po, cos_t, sin_t, act_dtype):
    m, dm = x.shape
    tm = _row_tile(m, 2 * ROW_TILE) if m > ROW_TILE else m
    row = lambda n: pl.BlockSpec((tm, n), lambda i: (i, 0))
    slab = pl.BlockSpec((y.shape[0], tm, LANES), lambda i: (0, i, 0))
    assert cos_t.shape[0] % tm == 0
    nper = cos_t.shape[0] // tm
    tab = pl.BlockSpec((tm, LANES), lambda i: (i % nper, 0))
    qw = SWA_HEADS * SWA_HEAD_DIM
    kw = SWA_KV_HEADS * SWA_HEAD_DIM
    weights = [pe['wglu'], pe['bglu'], pe['wo'], po['norm_g'], po['w_in'], po['gq'], po['gk']]
    resident = lambda w: pl.BlockSpec(w.shape, lambda i: (0,) * w.ndim, pipeline_mode=pl.Buffered(1))
    return pl.pallas_call(
        _mid_kernel,
        grid=(m // tm,),
        in_specs=[row(dm), slab, row(S5_WIDTH), row(GLA_V_WIDTH)] + [resident(w) for w in weights] + [tab, tab],
        out_specs=[row(dm), row(qw), row(kw), row(kw), row(qw)],
        out_shape=[jax.ShapeDtypeStruct((m, dm), F32),
                   jax.ShapeDtypeStruct((m, qw), act_dtype), jax.ShapeDtypeStruct((m, kw), F32),
                   jax.ShapeDtypeStruct((m, kw), F32), jax.ShapeDtypeStruct((m, qw), act_dtype)],
        compiler_params=_cparams(("parallel",)),
        name="mid",
    )(x, y, ga, ob, *weights, cos_t, sin_t)


def _rope_tables(pos):
    half = SWA_HEAD_DIM // 2
    inv_freq = ROPE_THETA ** (-np.arange(half, dtype=np.float64) / half)
    ang = np.asarray(pos, np.float64)[:, None] * inv_freq[None, :]
    cos, sin = np.cos(ang), np.sin(ang)
    cos_t = np.concatenate([cos, cos, cos, cos], axis=1).astype(np.float32)
    sin_t = np.concatenate([-sin, sin, -sin, sin], axis=1).astype(np.float32)
    return jnp.asarray(cos_t), jnp.asarray(sin_t)


def _attn_prompt_kernel(sink_ref, q_ref, kc_ref, kp_ref, vc_ref, vp_ref, gate_ref, h_ref, wout_ref,
                        y_ref, p_ref, o_ref, *, nqb):
    i = pl.program_id(1)
    w = SWA_WINDOW
    hd = SWA_HEAD_DIM
    npair = SWA_GROUP // 2
    lane2 = lax.broadcasted_iota(jnp.int32, (2 * w, LANES), 1)
    low = lane2 < hd
    rr = lax.broadcasted_iota(jnp.int32, (w, w), 0)
    cc = lax.broadcasted_iota(jnp.int32, (w, w), 1)
    tri = cc <= rr
    low_w = cc < hd
    r4 = lax.broadcasted_iota(jnp.int32, (4 * w, LANES), 0)
    c4 = lax.broadcasted_iota(jnp.int32, (4 * w, LANES), 1)
    den_cols = ((r4 < 2 * w) == (c4 < hd)).astype(BF16)
    units = [(jb, kv) for jb in range(nqb) for kv in range(SWA_KV_HEADS)]

    def scores(jb, kv):
        rows = slice(jb * w, (jb + 1) * w)
        if jb == 0:
            k_prev, v_prev = kp_ref[...], vp_ref[...]
        else:
            prev_rows = slice((jb - 1) * w, jb * w)
            k_prev, v_prev = kc_ref[prev_rows, :], vc_ref[prev_rows, :]
        kcat = jnp.concatenate([k_prev, kc_ref[rows, :]], axis=0)
        vcat = jnp.concatenate([v_prev, vc_ref[rows, :]], axis=0)
        own = low if kv == 0 else jnp.logical_not(low)
        k_own = jnp.where(own, kcat, 0.0)
        v_own = jnp.where(own, vcat, 0.0)
        k_oth = pltpu.roll(k_own, hd, 1)
        v_oth = pltpu.roll(v_own, hd, 1)
        k_lo, k_hi = (k_own, k_oth) if kv == 0 else (k_oth, k_own)
        v_lo, v_hi = (v_own, v_oth) if kv == 0 else (v_oth, v_own)
        k_rhs = jnp.concatenate([k_lo, k_hi], axis=0).astype(BF16)
        v_rhs = jnp.concatenate([jnp.concatenate([v_lo, v_hi], axis=0).astype(BF16), den_cols], axis=1)
        qs = jnp.concatenate([q_ref[rows, (kv * npair + pr) * LANES:(kv * npair + pr + 1) * LANES]
                              for pr in range(npair)], axis=0)
        return _dot_nt(qs, k_rhs), v_rhs

    def softmax_pv(kv, first, buf, s_all, v_rhs):
        sink_terms = []
        for pr in range(npair):
            prow = slice(pr * w, (pr + 1) * w)
            pair_terms = []
            for e in range(2):
                s_prev = s_all[prow, (2 * e) * w:(2 * e + 1) * w]
                s_cur = s_all[prow, (2 * e + 1) * w:(2 * e + 2) * w]
                if first:
                    s_prev = jnp.where(i > 0, s_prev, -jnp.inf)
                sc = jnp.where(tri, s_cur, s_prev)
                sink = sink_ref[2 * (kv * npair + pr) + e] * LOG2E
                mx = jnp.maximum(jnp.max(sc, axis=-1, keepdims=True), sink)
                pe = jnp.exp2(sc - mx)
                p_ref[buf, prow, (2 * e) * w:(2 * e + 1) * w] = jnp.where(tri, 0.0, pe).astype(BF16)
                p_ref[buf, prow, (2 * e + 1) * w:(2 * e + 2) * w] = jnp.where(tri, pe, 0.0).astype(BF16)
                pair_terms.append(jnp.exp2(sink - mx))
            sink_terms.append(pair_terms)
        return _dot(p_ref[buf], v_rhs), sink_terms

    def normalise(jb, kv, o_ext, sink_terms):
        rows = slice(jb * w, (jb + 1) * w)
        for pr in range(npair):
            prow = slice(pr * w, (pr + 1) * w)
            cols = slice((kv * npair + pr) * LANES, (kv * npair + pr + 1) * LANES)
            st = jnp.where(low_w, sink_terms[pr][0], sink_terms[pr][1])
            o = o_ext[prow, :LANES] / (o_ext[prow, LANES:] + st)
            o_ref[rows, cols] = o.astype(o_ref.dtype)

    def project(rows):
        og = o_ref[rows, :].astype(F32) * gate_ref[rows, :].astype(F32)
        y_ref[rows, :] = h_ref[rows, :] + _dot(og.astype(BF16), wout_ref[...])

    group = 2
    pending = scores(*units[0])
    unfinished = None
    for n, (jb, kv) in enumerate(units):
        current = pending
        if n + 1 < len(units):
            pending = scores(*units[n + 1])
        result = softmax_pv(kv, jb == 0, n % 2, *current)
        if unfinished is not None:
            normalise(*unfinished)
            done_jb, done_kv = unfinished[:2]
            if done_kv == SWA_KV_HEADS - 1 and (done_jb + 1) % group == 0:
                project(slice((done_jb + 1 - group) * w, (done_jb + 1) * w))
        unfinished = (jb, kv) + result
    normalise(*unfinished)
    project(slice((nqb - group) * w, nqb * w))


def _attn_prompt(sinks, q, k, v, gate, h, w_out, nseq, seqlen):
    w = SWA_WINDOW
    nqb = 8
    tm = nqb * w
    assert seqlen % tm == 0
    nblk = seqlen // tm
    qw = q.shape[1]
    dm = h.shape[1]
    row = lambda n: pl.BlockSpec((tm, n), lambda b, i: (b * nblk + i, 0))
    prev = lambda n: pl.BlockSpec((w, n), lambda b, i: (jnp.maximum((b * nblk + i) * nqb - 1, 0), 0))
    smem = pl.BlockSpec(memory_space=pltpu.SMEM)
    return pl.pallas_call(
        functools.partial(_attn_prompt_kernel, nqb=nqb),
        grid=(nseq, nblk),
        in_specs=[smem, row(qw), row(LANES), prev(LANES), row(LANES), prev(LANES), row(qw), row(dm),
                  _full(w_out.shape)],
        out_specs=row(dm),
        out_shape=jax.ShapeDtypeStruct(h.shape, F32),
        scratch_shapes=[pltpu.VMEM((2, 4 * w, 4 * w), BF16), pltpu.VMEM((tm, qw), BF16)],
        compiler_params=_cparams(("parallel", "arbitrary")),
        name="attn_prompt",
    )(sinks, q, k, k, v, v, gate, h, w_out)


def _attn_sample_kernel(sink_ref, q_ref, kn_ref, vn_ref, kc_ref, vc_ref, o_ref, ko_ref, vo_ref, q2_ref, *, seqlen):
    nq = SWA_HEADS * seqlen
    hd = SWA_HEAD_DIM
    heads = [(kv, g) for kv in range(SWA_KV_HEADS) for g in range(SWA_GROUP)]

    def stacked(kv, g):
        r0 = (kv * SWA_GROUP + g) * seqlen
        return slice(r0, r0 + seqlen), slice(kv * hd, (kv + 1) * hd)

    q2_ref[...] = jnp.zeros(q2_ref.shape, q2_ref.dtype)
    for kv, g in heads:
        rows, lanes = stacked(kv, g)
        h = kv * SWA_GROUP + g
        q2_ref[:, rows, lanes] = q_ref[:, :, h * hd:(h + 1) * hd]
    ncache = kc_ref.shape[1]
    t_row = lax.broadcasted_iota(jnp.int32, (nq, ncache), 0) % seqlen
    c_col = lax.broadcasted_iota(jnp.int32, (nq, ncache), 1)
    cache_ok = c_col > t_row - (SWA_WINDOW - ncache)
    nnew = kn_ref.shape[1]
    t_row_n = lax.broadcasted_iota(jnp.int32, (nq, nnew), 0) % seqlen
    n_col = lax.broadcasted_iota(jnp.int32, (nq, nnew), 1)
    new_ok = n_col <= t_row_n
    sink = (sink_ref[...] * LOG2E)[None, :, 0:1]
    bqk = lambda a, b: lax.dot_general(a, b, (((2,), (2,)), ((0,), (0,))), preferred_element_type=F32)
    bpv = lambda a, b: lax.dot_general(a, b, (((2,), (1,)), ((0,), (0,))), preferred_element_type=F32)
    q = q2_ref[...].astype(BF16)
    kc = kc_ref[...]
    vc = vc_ref[...]
    kn = kn_ref[...]
    vn = vn_ref[...]
    sc = jnp.where(cache_ok[None], bqk(q, kc.astype(BF16)), -jnp.inf)
    sn = jnp.where(new_ok[None], bqk(q, kn.astype(BF16)), -jnp.inf)
    mx = jnp.maximum(jnp.maximum(jnp.max(sc, axis=-1, keepdims=True), jnp.max(sn, axis=-1, keepdims=True)), sink)
    pc = jnp.exp2(sc - mx)
    pn = jnp.exp2(sn - mx)
    den = jnp.sum(pc, axis=-1, keepdims=True) + jnp.sum(pn, axis=-1, keepdims=True) + jnp.exp2(sink - mx)
    inv = 1.0 / den
    o2 = bpv((pc * inv).astype(BF16), vc.astype(BF16)) + bpv((pn * inv).astype(BF16), vn.astype(BF16))
    for kv, g in heads:
        rows, lanes = stacked(kv, g)
        h = kv * SWA_GROUP + g
        o_ref[:, :, h * hd:(h + 1) * hd] = o2[:, rows, lanes]
    keep = ncache - seqlen
    ko_ref[:, 0:keep, :] = kc[:, seqlen:ncache, :]
    ko_ref[:, keep:ncache, :] = kn[:, 0:seqlen, :]
    vo_ref[:, 0:keep, :] = vc[:, seqlen:ncache, :]
    vo_ref[:, keep:ncache, :] = vn[:, 0:seqlen, :]


def _attn_sample(sink_rows, q, kn, vn, kc, vc):
    n, seqlen, _ = q.shape
    nseq = 16
    assert n % nseq == 0
    blk = lambda a: pl.BlockSpec((nseq,) + a.shape[1:], lambda i: (i, 0, 0))
    return pl.pallas_call(
        functools.partial(_attn_sample_kernel, seqlen=seqlen),
        grid=(n // nseq,),
        in_specs=[_full(sink_rows.shape), blk(q), blk(kn), blk(vn), blk(kc), blk(vc)],
        out_specs=[blk(q), blk(kc), blk(vc)],
        out_shape=[jax.ShapeDtypeStruct(q.shape, F32), jax.ShapeDtypeStruct(kc.shape, F32),
                   jax.ShapeDtypeStruct(vc.shape, F32)],
        scratch_shapes=[pltpu.VMEM((nseq, SWA_HEADS * seqlen, SWA_KV_HEADS * SWA_HEAD_DIM), F32)],
        compiler_params=_cparams(("parallel",)),
        name="attn_sample",
    )(sink_rows, q, kn, vn, kc, vc)


def _odd_out_kernel(h_ref, o_ref, gate_ref, w_ref, y_ref):
    og = o_ref[...].astype(F32) * gate_ref[...].astype(F32)
    y_ref[...] = h_ref[...] + _dot(og.astype(BF16), w_ref[...])


def _odd_out(h, o, gate, w):
    m, dm = h.shape
    tm = _row_tile(m)
    row = lambda n: pl.BlockSpec((tm, n), lambda i: (i, 0))
    return pl.pallas_call(
        _odd_out_kernel,
        grid=(m // tm,),
        in_specs=[row(dm), row(o.shape[1]), row(gate.shape[1]), _full(w.shape)],
        out_specs=row(dm),
        out_shape=jax.ShapeDtypeStruct((m, dm), F32),
        compiler_params=_cparams(("parallel",)),
        name="odd_out",
    )(h, o, gate, w)


def _trunk(x, s5_x0, gla_s0, pe, po, pos, nseq, seqlen, act_dtype):
    u2, ga, q, k, v, la, gb = _even_in(x, pe['norm_g'], pe['w_in'], pe['wgate'], pe['bgate'], act_dtype,
                                       _s5_chunk(seqlen))
    y4, s5_fin = _s5_branch(u2, s5_x0, pe['s5_ops'], pe['d'], nseq, seqlen, act_dtype)
    ob, gla_fin = _gla(q, k, v, la, gb, gla_s0, pe['gla_norm_g'], nseq, seqlen, act_dtype)
    cos_t, sin_t = _rope_tables(pos)
    h, q1, k1, v1, gate = _mid(x, y4, ga, ob, pe, po, cos_t, sin_t, act_dtype)
    return h, q1, k1, v1, gate, s5_fin, gla_fin


def kernel(x_prompt, x_sample, state_s5_re, state_s5_im, state_gla, cache_swa_k, cache_swa_v,
           even_norm_g, even_w_in, s5_lambda_re, s5_lambda_im, s5_log_dt, s5_b_re, s5_b_im,
           s5_c_re, s5_c_im, s5_d, s5_w_glu, s5_b_glu, gla_w_gate, gla_b_gate, gla_norm_g,
           even_w_out, odd_norm_g, odd_w_in, swa_q_norm_g, swa_k_norm_g, swa_sinks, odd_w_out):
    nb, seq, dm = x_prompt.shape
    ns, dseq, _ = x_sample.shape
    ng = s5_lambda_re.shape[1]
    no = ng // OCT
    xp = x_prompt.reshape(nb * seq, dm)
    xs = x_sample.reshape(ns * dseq, dm)

    i = 0
    assert ng * S5_GROUP == S5_WIDTH
    pad_rank = LANES - GLA_GATE_RANK
    pe = {
        'norm_g': even_norm_g[i][None, :],
        'w_in': even_w_in[i].astype(BF16),
        'wgate': jnp.pad(gla_w_gate[i], ((0, pad_rank), (0, 0))).astype(BF16),
        'bgate': gla_b_gate[i][None, :],
        's5_ops': _s5_params(s5_lambda_re[i], s5_lambda_im[i], s5_log_dt[i], s5_b_re[i], s5_b_im[i],
                             s5_c_re[i], s5_c_im[i]),
        'gla_norm_g': gla_norm_g[i][None, :],
        'd': s5_d[i],
        'wglu': s5_w_glu[i].astype(BF16),
        'bglu': s5_b_glu[i][None, :],
        'wo': even_w_out[i].astype(BF16),
    }
    po = {
        'norm_g': odd_norm_g[i][None, :],
        'w_in': odd_w_in[i].astype(BF16),
        'gq': jnp.tile(swa_q_norm_g[i], MXU_TILE // SWA_HEAD_DIM)[None, :],
        'gk': jnp.tile(swa_k_norm_g[i], LANES // SWA_HEAD_DIM)[None, :],
    }
    w_out = odd_w_out[i].astype(BF16)
    sinks = swa_sinks[i]
    kvw = SWA_KV_HEADS * SWA_HEAD_DIM

    s5_zero = jnp.zeros((no, nb, 2 * OCT * S5_STATE), F32)
    gla_zero = jnp.zeros((nb, GLA_HEADS, GLA_HEAD_K, GLA_HEAD_V), F32)
    hp, q, k, v, gate, s5_p, gla_p = _trunk(xp, s5_zero, gla_zero, pe, po, np.arange(seq), nb, seq, BF16)
    s5r_p, s5i_p = _s5_state_out(s5_p)
    y_prompt = _attn_prompt(sinks, q, k, v, gate, hp, w_out, nb, seq).reshape(nb, seq, dm)
    cache_len = min(SWA_WINDOW, seq)
    tail = lambda a: (a.reshape(nb, seq, kvw)[:, seq - cache_len:]
                      .reshape(1, nb, cache_len, SWA_KV_HEADS, SWA_HEAD_DIM))
    swk_p, swv_p = tail(k), tail(v)

    pos_s = np.tile(PAST_LEN + np.arange(dseq), ns)
    s5_init = _s5_state_in(state_s5_re[i], state_s5_im[i], no)
    hs, q, k, v, gate, s5_s, gla_s = _trunk(xs, s5_init, state_gla[i], pe, po, pos_s, ns, dseq, F32)
    s5r_s, s5i_s = _s5_state_out(s5_s)
    ncache = cache_swa_k.shape[2]
    sink_rows = jnp.broadcast_to(jnp.repeat(sinks, dseq)[:, None], (SWA_HEADS * dseq, LANES))
    kn = k.reshape(ns, dseq, kvw)
    vn = v.reshape(ns, dseq, kvw)
    npad = BF16_ROWS - dseq
    kn_pad = jnp.pad(kn, ((0, 0), (0, npad), (0, 0)))
    vn_pad = jnp.pad(vn, ((0, 0), (0, npad), (0, 0)))
    kc = cache_swa_k[i].reshape(ns, ncache, kvw)
    vc = cache_swa_v[i].reshape(ns, ncache, kvw)
    o, kc_new, vc_new = _attn_sample(sink_rows, q.reshape(ns, dseq, -1), kn_pad, vn_pad, kc, vc)
    y_sample = _odd_out(hs, o.reshape(ns * dseq, -1), gate, w_out).reshape(ns, dseq, dm)
    swk_s = kc_new.reshape(1, ns, ncache, SWA_KV_HEADS, SWA_HEAD_DIM)
    swv_s = vc_new.reshape(1, ns, ncache, SWA_KV_HEADS, SWA_HEAD_DIM)

    return (y_prompt, y_sample,
            s5r_p, s5i_p, gla_p[None], swk_p, swv_p,
            s5r_s, s5i_s, gla_s[None], swk_s, swv_s)
```

```python
import functools
import math

import jax
import jax.numpy as jnp
import numpy as np
from jax import lax
from jax.experimental import pallas as pl
from jax.experimental.pallas import tpu as pltpu

F32 = jnp.float32
BF16 = jnp.bfloat16

PAST_LEN = 8192
NORM_EPS = 1e-6
S5_GROUP = 16
S5_STATE = 64
S5_CHUNK = 16
GLA_HEADS = 4
GLA_HEAD_K = 64
GLA_HEAD_V = 128
GLA_GATE_RANK = 16
GLA_GATE_TAU = 16.0
GLA_CHUNK = 64
S5_WIDTH = 512
GLA_K_WIDTH = GLA_HEADS * GLA_HEAD_K
GLA_V_WIDTH = GLA_HEADS * GLA_HEAD_V
SWA_HEADS = 16
SWA_KV_HEADS = 2
SWA_GROUP = SWA_HEADS // SWA_KV_HEADS
SWA_HEAD_DIM = 64
SWA_WINDOW = 128
ROPE_THETA = 10000.0
LOG2E = math.log2(math.e)
LANES = 128
BF16_ROWS = 16
MXU_TILE = 256
OCT = LANES // S5_GROUP
ROW_TILE = 512
VMEM_LIMIT = 48 * 1024 * 1024
C_U = 0
C_GATE_A = C_U + S5_WIDTH
C_Q = C_GATE_A + S5_WIDTH
C_K = C_Q + GLA_K_WIDTH
C_V = C_K + GLA_K_WIDTH
C_CODE = C_V + GLA_V_WIDTH
C_GATE_B = C_CODE + GLA_GATE_RANK
C_END = C_GATE_B + GLA_V_WIDTH


def _cparams(sem):
    return pltpu.CompilerParams(dimension_semantics=sem, vmem_limit_bytes=VMEM_LIMIT)


def _full(shape):
    n = len(shape)
    return pl.BlockSpec(shape, lambda *_: (0,) * n)


def _dot(a, b):
    return jnp.dot(a, b, preferred_element_type=F32)


def _dot_nt(a, b):
    return lax.dot_general(a, b, (((1,), (1,)), ((), ())), preferred_element_type=F32)


def _dot_tn(a, b):
    return lax.dot_general(a, b, (((0,), (0,)), ((), ())), preferred_element_type=F32)


def _split_bf16(x):
    hi = x.astype(BF16)
    lo = (x - hi.astype(F32)).astype(BF16)
    return hi, lo


def _rms_rows(x, g):
    return x * lax.rsqrt(jnp.mean(x * x, axis=-1, keepdims=True) + NORM_EPS) * g


def _sigmoid(x):
    return 1.0 / (1.0 + jnp.exp(-x))


def _silu(x):
    return x * _sigmoid(x)


def _row_tile(m, tile=ROW_TILE):
    return tile if m % tile == 0 else m


def _even_in_kernel(x_ref, g_ref, w_ref, wgate_ref, bgate_ref,
                    u2_ref, ga_ref, q_ref, k_ref, v_ref, la_ref, gb_ref, uscr_ref, wgb_ref, *, t):
    @pl.when(pl.program_id(0) == 0)
    def _():
        wgb_ref[...] = w_ref[:, C_GATE_B:C_END]

    xb = _rms_rows(x_ref[...], g_ref[...]).astype(BF16)

    def proj(lo, hi):
        return _dot(xb, w_ref[:, lo:hi])

    u = proj(C_U, C_GATE_A)
    nrow = u.shape[0] // t
    for o in range(u2_ref.shape[0]):
        uscr_ref[o] = u[:, o * LANES:(o + 1) * LANES]
        for tt in range(t):
            piece = uscr_ref[o, pl.ds(tt, nrow, stride=t), :]
            u2_ref[o, :, tt * LANES:(tt + 1) * LANES] = piece.astype(u2_ref.dtype)
    ga_ref[...] = _silu(proj(C_GATE_A, C_Q)).astype(ga_ref.dtype)
    q_ref[...] = (proj(C_Q, C_K) * (GLA_HEAD_K ** -0.5)).astype(q_ref.dtype)
    k_ref[...] = proj(C_K, C_V).astype(k_ref.dtype)
    v_ref[...] = proj(C_V, C_CODE).astype(v_ref.dtype)
    gb_ref[...] = _silu(_dot(xb, wgb_ref[...])).astype(gb_ref.dtype)
    a_low = proj(C_CODE, C_CODE + LANES)
    logit = _dot(a_low.astype(BF16), wgate_ref[...]) + bgate_ref[...]
    log_sig = jnp.minimum(logit, 0.0) - jnp.log1p(jnp.exp(-jnp.abs(logit)))
    la_ref[...] = log_sig * (1.0 / GLA_GATE_TAU)


def _even_in(x, g, w, wgate, bgate, act_dtype, t):
    m, d = x.shape
    tm = _row_tile(m, 2 * ROW_TILE) if m > ROW_TILE else m
    row = lambda n: pl.BlockSpec((tm, n), lambda i: (i, 0))
    assert w.shape[1] == C_END
    no = S5_WIDTH // LANES
    chunk = pl.BlockSpec((no, tm // t, t * LANES), lambda i: (0, i, 0))
    outs = [(S5_WIDTH, act_dtype), (GLA_K_WIDTH, act_dtype), (GLA_K_WIDTH, act_dtype), (GLA_V_WIDTH, act_dtype),
            (GLA_K_WIDTH, F32), (GLA_V_WIDTH, act_dtype)]
    return pl.pallas_call(
        functools.partial(_even_in_kernel, t=t),
        grid=(m // tm,),
        in_specs=[row(d), _full(g.shape), _full(w.shape), _full(wgate.shape), _full(bgate.shape)],
        out_specs=[chunk] + [row(n) for n, _ in outs],
        out_shape=[jax.ShapeDtypeStruct((no, m // t, t * LANES), act_dtype)]
        + [jax.ShapeDtypeStruct((m, n), dt) for n, dt in outs],
        scratch_shapes=[pltpu.VMEM((no, tm, LANES), F32), pltpu.VMEM((d, GLA_V_WIDTH), BF16)],
        compiler_params=_cparams(("arbitrary",)),
        name="even_in",
    )(x, g, w, wgate, bgate)


def _group_mask(shape, row_span, col_span):
    rg = (lax.broadcasted_iota(jnp.int32, shape, 0) // row_span) % OCT
    cg = (lax.broadcasted_iota(jnp.int32, shape, 1) // col_span) % OCT
    return rg == cg


def _s5_state_kernel(u_ref, bre_ref, bim_ref, x0r_ref, x0i_ref, are_ref, aim_ref, xs_ref, xfr_ref, xfi_ref, loc_ref,
                     *, nseq, nchunks):
    hw = OCT * S5_STATE
    kk = u_ref.shape[2]
    bp = jnp.concatenate([bre_ref[0]] * OCT + [bim_ref[0]] * OCT, axis=1)
    bp = jnp.where(_group_mask((kk, 2 * hw), S5_GROUP, S5_STATE), bp, 0.0).astype(BF16)
    loc_ref[...] = _dot(u_ref[0], bp)
    a_re = are_ref[0]
    a_im = aim_ref[0]
    if nchunks == 1:
        xr, xi = x0r_ref[...], x0i_ref[...]
        loc = loc_ref[...]
        xfr_ref[...] = a_re * xr - a_im * xi + loc[:, :hw]
        xfi_ref[...] = a_re * xi + a_im * xr + loc[:, hw:]
        xs_ref[0] = jnp.concatenate([xr, xi], axis=1).astype(xs_ref.dtype)
    else:
        def body(j, carry):
            new = []
            for b in range(nseq):
                xr, xi = carry[b]
                row = pl.ds(b * nchunks + j, 1)
                lr = loc_ref[row, :hw]
                li = loc_ref[row, hw:]
                loc_ref[row, :hw] = xr
                loc_ref[row, hw:] = xi
                new.append((a_re * xr - a_im * xi + lr, a_re * xi + a_im * xr + li))
            return tuple(new)

        init = tuple((x0r_ref[b:b + 1, :], x0i_ref[b:b + 1, :]) for b in range(nseq))
        fin = lax.fori_loop(0, nchunks, body, init, unroll=4)
        for b in range(nseq):
            xfr_ref[b:b + 1, :] = fin[b][0]
            xfi_ref[b:b + 1, :] = fin[b][1]
        xs_ref[0] = loc_ref[...].astype(xs_ref.dtype)


def _s5_state(u2, bre, bim, x0r, x0i, are, aim, nseq, nchunks):
    no, r, kk = u2.shape
    hw = OCT * S5_STATE
    blk = lambda a: pl.BlockSpec((1,) + a.shape[1:], lambda o: (o,) + (0,) * (a.ndim - 1))
    octet = pl.BlockSpec((nseq, hw), lambda o: (0, o))
    return pl.pallas_call(
        functools.partial(_s5_state_kernel, nseq=nseq, nchunks=nchunks),
        grid=(no,),
        in_specs=[blk(u2), blk(bre), blk(bim), octet, octet, blk(are), blk(aim)],
        out_specs=[pl.BlockSpec((1, r, 2 * hw), lambda o: (o, 0, 0)), octet, octet],
        out_shape=[jax.ShapeDtypeStruct((no, r, 2 * hw), BF16), jax.ShapeDtypeStruct(x0r.shape, F32),
                   jax.ShapeDtypeStruct(x0i.shape, F32)],
        scratch_shapes=[pltpu.VMEM((r, 2 * hw), F32)],
        compiler_params=_cparams(("parallel",)),
        name="s5_state",
    )(u2, bre, bim, x0r, x0i, are, aim)


def _dot_nt_f32(a, b):
    a_hi, a_lo = _split_bf16(a)
    b_hi, b_lo = _split_bf16(b)
    return _dot_nt(a_hi, b_hi) + _dot_nt(a_hi, b_lo) + _dot_nt(a_lo, b_hi)


def _s5_out_kernel(u_ref, xs_ref, bre_ref, bim_ref, cre_ref, cim_ref, zre_ref, zim_ref, d_ref, y_ref, yscr_ref):
    r, kk = u_ref.shape[1], u_ref.shape[2]
    t = kk // LANES
    sw = xs_ref.shape[2]
    ntile = kk // MXU_TILE
    taps = _dot_nt_f32(bre_ref[0], zre_ref[0]) - _dot_nt_f32(bim_ref[0], zim_ref[0])
    tmask = _group_mask((LANES, LANES), S5_GROUP, S5_GROUP)

    def tap(lag):
        if lag < 0:
            return jnp.zeros((LANES, LANES), F32)
        s = t - 1 - lag
        return jnp.where(tmask, taps[s * LANES:(s + 1) * LANES], 0.0)

    wts = [jnp.concatenate([jnp.concatenate([tap(2 * d), tap(2 * d + 1)], axis=1),
                            jnp.concatenate([tap(2 * d - 1), tap(2 * d)], axis=1)], axis=0).astype(BF16)
           for d in range(ntile)]
    cpt = jnp.concatenate([cre_ref[0]] * OCT + [cim_ref[0]] * OCT, axis=1)
    cpt = jnp.where(_group_mask((kk, sw), S5_GROUP, S5_STATE), cpt, 0.0).astype(BF16)
    xs = xs_ref[0]
    skip = jnp.concatenate([d_ref[0]] * (MXU_TILE // LANES), axis=1)
    for n in range(ntile):
        cols = slice(n * MXU_TILE, (n + 1) * MXU_TILE)
        acc = _dot_nt(xs, cpt[cols]) + skip * u_ref[0, :, cols].astype(F32)
        for k in range(n + 1):
            acc = acc + _dot(u_ref[0, :, k * MXU_TILE:(k + 1) * MXU_TILE], wts[n - k])
        for e in range(MXU_TILE // LANES):
            yscr_ref[pl.ds(2 * n + e, r, stride=t), :] = acc[:, e * LANES:(e + 1) * LANES]
    y_ref[0] = yscr_ref[...].astype(y_ref.dtype)


def _s5_out(u2, xs, bre, bim, cre, cim, zre, zim, d, out_dtype):
    no, r, kk = u2.shape
    m = r * (kk // LANES)
    blk = lambda a: pl.BlockSpec((1,) + a.shape[1:], lambda o: (o,) + (0,) * (a.ndim - 1))
    ops = (u2, xs, bre, bim, cre, cim, zre, zim, d)
    return pl.pallas_call(
        _s5_out_kernel,
        grid=(no,),
        in_specs=[blk(a) for a in ops],
        out_specs=pl.BlockSpec((1, m, LANES), lambda o: (o, 0, 0)),
        out_shape=jax.ShapeDtypeStruct((no, m, LANES), out_dtype),
        scratch_shapes=[pltpu.VMEM((m, LANES), F32)],
        compiler_params=_cparams(("parallel",)),
        name="s5_out",
    )(*ops)


def _s5_params(lam_re, lam_im, log_dt, b_re, b_im, c_re, c_im):
    t = S5_CHUNK
    ng = lam_re.shape[0]
    no = ng // OCT
    dt = jnp.exp(log_dt)[:, None]
    a = lam_re * dt
    b = lam_im * dt
    n = jnp.arange(t + 1, dtype=F32)[None, :, None]
    mag = jnp.exp(n * a[:, None, :])
    pw_re = mag * jnp.cos(n * b[:, None, :])
    pw_im = mag * jnp.sin(n * b[:, None, :])
    em1_re = jnp.expm1(a) * jnp.cos(b) - 2.0 * jnp.sin(0.5 * b) ** 2
    em1_im = jnp.exp(a) * jnp.sin(b)
    den = lam_re * lam_re + lam_im * lam_im
    z_re = (em1_re * lam_re + em1_im * lam_im) / den
    z_im = (em1_im * lam_re - em1_re * lam_im) / den
    bb_re = z_re[..., None] * b_re - z_im[..., None] * b_im
    bb_im = z_re[..., None] * b_im + z_im[..., None] * b_re

    def per_token(pw):
        return pw.reshape(no, OCT, t, S5_STATE).transpose(0, 2, 1, 3)[:, :, :, None, :]

    def per_group(w):
        return w.reshape(no, 1, OCT, S5_GROUP, S5_STATE)

    rows = lambda w: w.reshape(no, t * LANES, S5_STATE)
    bt_re, bt_im = per_group(bb_re.transpose(0, 2, 1)), per_group(bb_im.transpose(0, 2, 1))
    r_re, r_im = per_token(pw_re[:, t - 1::-1]), per_token(pw_im[:, t - 1::-1])
    bre = rows(r_re * bt_re - r_im * bt_im)
    bim = rows(r_re * bt_im + r_im * bt_re)
    o_re, o_im = per_token(pw_re[:, 1:]), per_token(pw_im[:, 1:])
    cg_re, cg_im = per_group(c_re), per_group(c_im)
    cre = rows(cg_re * o_re - cg_im * o_im)
    cim = rows(-(cg_re * o_im + cg_im * o_re))
    return bre, bim, cre, cim, c_re.reshape(no, LANES, S5_STATE), c_im.reshape(no, LANES, S5_STATE), pw_re, pw_im


def _s5_chunk(seqlen):
    return math.gcd(seqlen, S5_CHUNK)


def _s5_branch(u2, x0, ops, d, nseq, seqlen, out_dtype):
    bre, bim, cre, cim, zre, zim, pw_re, pw_im = ops
    no = u2.shape[0]
    t = _s5_chunk(seqlen)
    kk = t * LANES
    nchunks = seqlen // t
    are = pw_re[:, t].reshape(no, 1, OCT * S5_STATE)
    aim = pw_im[:, t].reshape(no, 1, OCT * S5_STATE)
    tail = S5_CHUNK * LANES - kk
    bre, bim = bre[:, tail:], bim[:, tail:]
    flat = lambda a: a.reshape(nseq, -1)
    xs, xf_re, xf_im = _s5_state(u2, bre, bim, flat(x0[0]), flat(x0[1]), are, aim, nseq, nchunks)
    y = _s5_out(u2, xs, bre, bim, cre[:, :kk], cim[:, :kk], zre, zim, d.reshape(no, 1, LANES), out_dtype)
    return y, (xf_re.reshape((1,) + x0[0].shape), xf_im.reshape((1,) + x0[1].shape))


def _gla_kernel(q_ref, k_ref, v_ref, la_ref, gb_ref, s0_ref, ng_ref, o_ref, sf_ref, st_ref,
                *, nseq, nchunks, c):
    i = pl.program_id(1)

    @pl.when(i == 0)
    def _():
        st_ref[...] = s0_ref[...]

    nh, hk, hv = GLA_HEADS, GLA_HEAD_K, GLA_HEAD_V
    nch = nseq * nchunks
    tm = nch * c
    iota = lambda shape, d: lax.broadcasted_iota(jnp.int32, shape, d)
    ng = ng_ref[...]

    tb = min(tm, MXU_TILE)
    rt, ct = iota((tb, tb), 0), iota((tb, tb), 1)
    tril = ((rt // c == ct // c) & (rt >= ct)).astype(BF16)
    parts = []
    for r0 in range(0, tm, tb):
        la_hi, la_lo = _split_bf16(la_ref[r0:r0 + tb, :])
        parts.append(_dot(tril, la_hi) + _dot(tril, la_lo))
    bcum = jnp.concatenate(parts, axis=0)
    e_hi, e_lo = _split_bf16(jnp.exp(jnp.concatenate([bcum[(ci + 1) * c - 1:(ci + 1) * c] for ci in range(nch)],
                                                     axis=0)))
    pick = (iota((nch, LANES), 0) == iota((nch, LANES), 1)).astype(BF16)
    dec_t = _dot_tn(e_hi, pick) + _dot_tn(e_lo, pick)
    q_all = q_ref[...].astype(F32)
    k_all = k_ref[...].astype(F32)
    q_dec_all = q_all * jnp.exp(bcum)
    k_dec_all = k_all * jnp.exp(-bcum)

    own_k = iota((nh * c, nh * hk), 0) // c == iota((nh * c, nh * hk), 1) // hk
    own_v = iota((nh * c, nh * hv), 0) // c == iota((nh * c, nh * hv), 1) // hv
    causal = iota((c, nh * c), 1) % c <= iota((c, nh * c), 0)
    zero_v = jnp.zeros((hk, hv), F32)

    def intra(ci):
        rows = slice(ci * c, (ci + 1) * c)
        q_dec = q_dec_all[rows].astype(BF16)
        k_dec = k_dec_all[rows]
        b_c = bcum[rows]
        k_tail = (k_all[rows] * jnp.exp(b_c[c - 1:c] - b_c)).astype(BF16)
        v = v_ref[rows, :].astype(F32)
        k_bd = jnp.where(own_k, jnp.concatenate([k_dec] * nh, axis=0), 0.0).astype(BF16)
        v_bd = jnp.where(own_v, jnp.concatenate([v] * nh, axis=0), 0.0).astype(BF16)
        att = jnp.where(causal, _dot_nt(q_dec, k_bd), 0.0)
        o_intra = _dot(att.astype(BF16), v_bd)
        vb = v.astype(BF16)
        kvs = []
        for h0 in range(0, nh, 2):
            kv2 = _dot_tn(k_tail[:, h0 * hk:(h0 + 2) * hk], vb[:, h0 * hv:(h0 + 2) * hv])
            kvs += [kv2[:hk, :hv], kv2[hk:, hv:]]
        return q_dec, o_intra, kvs

    def carry(ci, sts, q_dec, o_intra, kvs):
        st_bd = jnp.concatenate(
            [jnp.concatenate([zero_v] * h + [sts[h]] + [zero_v] * (nh - 1 - h), axis=1) for h in range(nh)],
            axis=0).astype(BF16)
        o = o_intra + _dot(q_dec, st_bd)
        new = [dec_t[h * hk:(h + 1) * hk, ci:ci + 1] * sts[h] + kvs[h] for h in range(nh)]
        return o, new

    def finish(ci, o):
        rows = slice(ci * c, (ci + 1) * c)
        for h in range(nh):
            vs = slice(h * hv, (h + 1) * hv)
            oh = _rms_rows(o[:, vs], ng) * gb_ref[rows, vs].astype(F32)
            o_ref[rows, vs] = oh.astype(o_ref.dtype)

    sts = None
    pending = intra(0)
    unfinished = None
    for ci in range(nch):
        s, first, last = ci // nchunks, ci % nchunks == 0, ci % nchunks == nchunks - 1
        current = pending
        if ci + 1 < nch:
            pending = intra(ci + 1)
        if first:
            sts = [st_ref[s, h] for h in range(nh)]
        o, sts = carry(ci, sts, *current)
        if last:
            for h in range(nh):
                st_ref[s, h] = sts[h]
        if unfinished is not None:
            finish(*unfinished)
        unfinished = (ci, o)
    finish(*unfinished)

    @pl.when(i == pl.num_programs(1) - 1)
    def _():
        sf_ref[...] = st_ref[...]


def _gla(q, k, v, la, gb, s0, ng, nseq_total, seqlen, out_dtype):
    c = math.gcd(seqlen, GLA_CHUNK)
    per_seq = seqlen // c
    if per_seq >= 8:
        nseq, nchunks = 1, next(n for n in (32, 16, 8, 1) if per_seq % n == 0)
    else:
        nseq, nchunks = 16, seqlen // c
    assert nseq_total % nseq == 0 and seqlen % (nchunks * c) == 0
    nblk = seqlen // (nchunks * c)
    tm = nseq * nchunks * c
    row = lambda n: pl.BlockSpec((tm, n), lambda b, i: (b * nblk + i, 0))
    st_spec = pl.BlockSpec((nseq, GLA_HEADS, GLA_HEAD_K, GLA_HEAD_V), lambda b, i: (b, 0, 0, 0))
    m = q.shape[0]
    return pl.pallas_call(
        functools.partial(_gla_kernel, nseq=nseq, nchunks=nchunks, c=c),
        grid=(nseq_total // nseq, nblk),
        in_specs=[row(GLA_K_WIDTH), row(GLA_K_WIDTH), row(GLA_V_WIDTH), row(GLA_K_WIDTH), row(GLA_V_WIDTH),
                  st_spec, _full(ng.shape)],
        out_specs=[row(GLA_V_WIDTH), st_spec],
        out_shape=[jax.ShapeDtypeStruct((m, GLA_V_WIDTH), out_dtype), jax.ShapeDtypeStruct(s0.shape, F32)],
        scratch_shapes=[pltpu.VMEM((nseq, GLA_HEADS, GLA_HEAD_K, GLA_HEAD_V), F32)],
        compiler_params=_cparams(("parallel", "arbitrary")),
        name="gla",
    )(q, k, v, la, gb, s0, ng)


def _gelu_tanh(x):
    return 0.5 * x * (1.0 + jnp.tanh(math.sqrt(2.0 / math.pi) * (x + 0.044715 * (x * x * x))))


def _head_ones(n):
    r = lax.broadcasted_iota(jnp.int32, (n, n), 0) // SWA_HEAD_DIM
    c = lax.broadcasted_iota(jnp.int32, (n, n), 1) // SWA_HEAD_DIM
    return (r == c).astype(BF16)


def _rope_block(x, cos_t, sin_t, upper):
    half = SWA_HEAD_DIM // 2
    swapped = jnp.where(upper, pltpu.roll(x, half, 1), pltpu.roll(x, LANES - half, 1))
    return x * cos_t + swapped * sin_t


def _mid_kernel(x_ref, y_ref, ga_ref, ob_ref, wglu_ref, bglu_ref, wo_ref,
                g_ref, w_ref, gq_ref, gk_ref, cos_ref, sin_ref, h_ref, q_ref, k_ref, v_ref, gate_ref):
    tm = x_ref.shape[0]
    lane = lax.broadcasted_iota(jnp.int32, (tm, LANES), 1)
    upper = (lane & (SWA_HEAD_DIM // 2)) != 0
    ones4 = _head_ones(MXU_TILE)
    inv_d = 1.0 / SWA_HEAD_DIM
    qw = SWA_HEADS * SWA_HEAD_DIM
    kw = SWA_KV_HEADS * SWA_HEAD_DIM
    nblk = qw // MXU_TILE
    no = y_ref.shape[0]

    na = ga_ref.shape[1]
    mix_b = _dot(ob_ref[...].astype(BF16), wo_ref[na:, :])
    z = _gelu_tanh(jnp.concatenate([y_ref[o].astype(F32) for o in range(no)], axis=1))
    z = z * _sigmoid(_dot(z.astype(BF16), wglu_ref[...]) + bglu_ref[...])
    out_a = z * ga_ref[...].astype(F32)
    h = x_ref[...] + (_dot(out_a.astype(BF16), wo_ref[:na, :]) + mix_b)
    h_ref[...] = h

    xb = _rms_rows(h, g_ref[...]).astype(BF16)
    cos_t = cos_ref[...]
    sin_t = sin_ref[...]

    def gate_finish(gate):
        gate_ref[...] = _silu(gate).astype(gate_ref.dtype)

    def q_finish(j, q):
        ss = _dot((q * q).astype(BF16), ones4)
        qn = q * lax.rsqrt(ss * inv_d + NORM_EPS) * gq_ref[...]
        for e in range(MXU_TILE // LANES):
            cols = slice(j * MXU_TILE + e * LANES, j * MXU_TILE + (e + 1) * LANES)
            qe = _rope_block(qn[:, e * LANES:(e + 1) * LANES], cos_t, sin_t, upper)
            q_ref[:, cols] = (qe * (SWA_HEAD_DIM ** -0.5 * LOG2E)).astype(q_ref.dtype)

    def kv_finish(kv):
        k = kv[:, :kw]
        ss = _dot((k * k).astype(BF16), ones4[:kw, :kw])
        kn = k * lax.rsqrt(ss * inv_d + NORM_EPS) * gk_ref[...]
        k_ref[...] = _rope_block(kn, cos_t, sin_t, upper)
        v_ref[...] = kv[:, kw:]

    work = [(slice(j * MXU_TILE, (j + 1) * MXU_TILE), functools.partial(q_finish, j)) for j in range(nblk)]
    work += [(slice(qw, qw + 2 * kw), kv_finish)]
    work += [(slice(qw + 2 * kw, None), gate_finish)]
    pending = _dot(xb, w_ref[:, work[0][0]])
    for n, (_, finish) in enumerate(work):
        current = pending
        if n + 1 < len(work):
            pending = _dot(xb, w_ref[:, work[n + 1][0]])
        finish(current)


def _mid(x, y, ga, ob, pe, po, cos_t, sin_t, act_dtype):
    m, dm = x.shape
    tm = _row_tile(m, 2 * ROW_TILE) if m > ROW_TILE else m
    row = lambda n: pl.BlockSpec((tm, n), lambda i: (i, 0))
    slab = pl.BlockSpec((y.shape[0], tm, LANES), lambda i: (0, i, 0))
    assert cos_t.shape[0] % tm == 0
    nper = cos_t.shape[0] // tm
    tab = pl.BlockSpec((tm, LANES), lambda i: (i % nper, 0))
    qw = SWA_HEADS * SWA_HEAD_DIM
    kw = SWA_KV_HEADS * SWA_HEAD_DIM
    weights = [pe['wglu'], pe['bglu'], pe['wo'], po['norm_g'], po['w_in'], po['gq'], po['gk']]
    resident = lambda w: pl.BlockSpec(w.shape, lambda i: (0,) * w.ndim, pipeline_mode=pl.Buffered(1))
    return pl.pallas_call(
        _mid_kernel,
        grid=(m // tm,),
        in_specs=[row(dm), slab, row(S5_WIDTH), row(GLA_V_WIDTH)] + [resident(w) for w in weights] + [tab, tab],
        out_specs=[row(dm), row(qw), row(kw), row(kw), row(qw)],
        out_shape=[jax.ShapeDtypeStruct((m, dm), F32),
                   jax.ShapeDtypeStruct((m, qw), act_dtype), jax.ShapeDtypeStruct((m, kw), F32),
                   jax.ShapeDtypeStruct((m, kw), F32), jax.ShapeDtypeStruct((m, qw), act_dtype)],
        compiler_params=_cparams(("parallel",)),
        name="mid",
    )(x, y, ga, ob, *weights, cos_t, sin_t)


def _rope_tables(pos):
    half = SWA_HEAD_DIM // 2
    inv_freq = ROPE_THETA ** (-np.arange(half, dtype=np.float64) / half)
    ang = np.asarray(pos, np.float64)[:, None] * inv_freq[None, :]
    cos, sin = np.cos(ang), np.sin(ang)
    cos_t = np.concatenate([cos, cos, cos, cos], axis=1).astype(np.float32)
    sin_t = np.concatenate([-sin, sin, -sin, sin], axis=1).astype(np.float32)
    return jnp.asarray(cos_t), jnp.asarray(sin_t)


def _attn_prompt_kernel(sink_ref, q_ref, kc_ref, kp_ref, vc_ref, vp_ref, gate_ref, h_ref, wout_ref,
                        y_ref, p_ref, o_ref, *, nqb):
    i = pl.program_id(1)
    w = SWA_WINDOW
    hd = SWA_HEAD_DIM
    npair = SWA_GROUP // 2
    lane2 = lax.broadcasted_iota(jnp.int32, (2 * w, LANES), 1)
    low = lane2 < hd
    rr = lax.broadcasted_iota(jnp.int32, (w, w), 0)
    cc = lax.broadcasted_iota(jnp.int32, (w, w), 1)
    tri = cc <= rr
    low_w = cc < hd
    r4 = lax.broadcasted_iota(jnp.int32, (4 * w, LANES), 0)
    c4 = lax.broadcasted_iota(jnp.int32, (4 * w, LANES), 1)
    den_cols = ((r4 < 2 * w) == (c4 < hd)).astype(BF16)
    units = [(jb, kv) for jb in range(nqb) for kv in range(SWA_KV_HEADS)]

    def scores(jb, kv):
        rows = slice(jb * w, (jb + 1) * w)
        if jb == 0:
            k_prev, v_prev = kp_ref[...], vp_ref[...]
        else:
            prev_rows = slice((jb - 1) * w, jb * w)
            k_prev, v_prev = kc_ref[prev_rows, :], vc_ref[prev_rows, :]
        kcat = jnp.concatenate([k_prev, kc_ref[rows, :]], axis=0)
        vcat = jnp.concatenate([v_prev, vc_ref[rows, :]], axis=0)
        own = low if kv == 0 else jnp.logical_not(low)
        k_own = jnp.where(own, kcat, 0.0)
        v_own = jnp.where(own, vcat, 0.0)
        k_oth = pltpu.roll(k_own, hd, 1)
        v_oth = pltpu.roll(v_own, hd, 1)
        k_lo, k_hi = (k_own, k_oth) if kv == 0 else (k_oth, k_own)
        v_lo, v_hi = (v_own, v_oth) if kv == 0 else (v_oth, v_own)
        k_rhs = jnp.concatenate([k_lo, k_hi], axis=0).astype(BF16)
        v_rhs = jnp.concatenate([jnp.concatenate([v_lo, v_hi], axis=0).astype(BF16), den_cols], axis=1)
        qs = jnp.concatenate([q_ref[rows, (kv * npair + pr) * LANES:(kv * npair + pr + 1) * LANES]
                              for pr in range(npair)], axis=0)
        return _dot_nt(qs, k_rhs), v_rhs

    def softmax_pv(kv, first, buf, s_all, v_rhs):
        sink_terms = []
        for pr in range(npair):
            prow = slice(pr * w, (pr + 1) * w)
            pair_terms = []
            for e in range(2):
                s_prev = s_all[prow, (2 * e) * w:(2 * e + 1) * w]
                s_cur = s_all[prow, (2 * e + 1) * w:(2 * e + 2) * w]
                if first:
                    s_prev = jnp.where(i > 0, s_prev, -jnp.inf)
                sc = jnp.where(tri, s_cur, s_prev)
                sink = sink_ref[2 * (kv * npair + pr) + e] * LOG2E
                mx = jnp.maximum(jnp.max(sc, axis=-1, keepdims=True), sink)
                pe = jnp.exp2(sc - mx)
                p_ref[buf, prow, (2 * e) * w:(2 * e + 1) * w] = jnp.where(tri, 0.0, pe).astype(BF16)
                p_ref[buf, prow, (2 * e + 1) * w:(2 * e + 2) * w] = jnp.where(tri, pe, 0.0).astype(BF16)
                pair_terms.append(jnp.exp2(sink - mx))
            sink_terms.append(pair_terms)
        return _dot(p_ref[buf], v_rhs), sink_terms

    def normalise(jb, kv, o_ext, sink_terms):
        rows = slice(jb * w, (jb + 1) * w)
        for pr in range(npair):
            prow = slice(pr * w, (pr + 1) * w)
            cols = slice((kv * npair + pr) * LANES, (kv * npair + pr + 1) * LANES)
            st = jnp.where(low_w, sink_terms[pr][0], sink_terms[pr][1])
            o = o_ext[prow, :LANES] / (o_ext[prow, LANES:] + st)
            o_ref[rows, cols] = o.astype(o_ref.dtype)

    def project(rows):
        og = o_ref[rows, :].astype(F32) * gate_ref[rows, :].astype(F32)
        y_ref[rows, :] = h_ref[rows, :] + _dot(og.astype(BF16), wout_ref[...])

    group = 2
    pending = scores(*units[0])
    unfinished = None
    for n, (jb, kv) in enumerate(units):
        current = pending
        if n + 1 < len(units):
            pending = scores(*units[n + 1])
        result = softmax_pv(kv, jb == 0, n % 2, *current)
        if unfinished is not None:
            normalise(*unfinished)
            done_jb, done_kv = unfinished[:2]
            if done_kv == SWA_KV_HEADS - 1 and (done_jb + 1) % group == 0:
                project(slice((done_jb + 1 - group) * w, (done_jb + 1) * w))
        unfinished = (jb, kv) + result
    normalise(*unfinished)
    project(slice((nqb - group) * w, nqb * w))


def _attn_prompt(sinks, q, k, v, gate, h, w_out, nseq, seqlen):
    w = SWA_WINDOW
    nqb = 8
    tm = nqb * w
    assert seqlen % tm == 0
    nblk = seqlen // tm
    qw = q.shape[1]
    dm = h.shape[1]
    row = lambda n: pl.BlockSpec((tm, n), lambda b, i: (b * nblk + i, 0))
    prev = lambda n: pl.BlockSpec((w, n), lambda b, i: (jnp.maximum((b * nblk + i) * nqb - 1, 0), 0))
    smem = pl.BlockSpec(memory_space=pltpu.SMEM)
    return pl.pallas_call(
        functools.partial(_attn_prompt_kernel, nqb=nqb),
        grid=(nseq, nblk),
        in_specs=[smem, row(qw), row(LANES), prev(LANES), row(LANES), prev(LANES), row(qw), row(dm),
                  _full(w_out.shape)],
        out_specs=row(dm),
        out_shape=jax.ShapeDtypeStruct(h.shape, F32),
        scratch_shapes=[pltpu.VMEM((2, 4 * w, 4 * w), BF16), pltpu.VMEM((tm, qw), BF16)],
        compiler_params=_cparams(("parallel", "arbitrary")),
        name="attn_prompt",
    )(sinks, q, k, k, v, v, gate, h, w_out)


def _attn_sample_kernel(sink_ref, q_ref, kn_ref, vn_ref, kc_ref, vc_ref, o_ref, ko_ref, vo_ref, q2_ref, *, seqlen):
    nq = SWA_HEADS * seqlen
    hd = SWA_HEAD_DIM
    heads = [(kv, g) for kv in range(SWA_KV_HEADS) for g in range(SWA_GROUP)]

    def stacked(kv, g):
        r0 = (kv * SWA_GROUP + g) * seqlen
        return slice(r0, r0 + seqlen), slice(kv * hd, (kv + 1) * hd)

    q2_ref[...] = jnp.zeros(q2_ref.shape, q2_ref.dtype)
    for kv, g in heads:
        rows, lanes = stacked(kv, g)
        h = kv * SWA_GROUP + g
        q2_ref[:, rows, lanes] = q_ref[:, :, h * hd:(h + 1) * hd]
    ncache = kc_ref.shape[1]
    t_row = lax.broadcasted_iota(jnp.int32, (nq, ncache), 0) % seqlen
    c_col = lax.broadcasted_iota(jnp.int32, (nq, ncache), 1)
    cache_ok = c_col > t_row - (SWA_WINDOW - ncache)
    nnew = kn_ref.shape[1]
    t_row_n = lax.broadcasted_iota(jnp.int32, (nq, nnew), 0) % seqlen
    n_col = lax.broadcasted_iota(jnp.int32, (nq, nnew), 1)
    new_ok = n_col <= t_row_n
    sink = (sink_ref[...] * LOG2E)[None, :, 0:1]
    bqk = lambda a, b: lax.dot_general(a, b, (((2,), (2,)), ((0,), (0,))), preferred_element_type=F32)
    bpv = lambda a, b: lax.dot_general(a, b, (((2,), (1,)), ((0,), (0,))), preferred_element_type=F32)
    q = q2_ref[...].astype(BF16)
    kc = kc_ref[...]
    vc = vc_ref[...]
    kn = kn_ref[...]
    vn = vn_ref[...]
    sc = jnp.where(cache_ok[None], bqk(q, kc.astype(BF16)), -jnp.inf)
    sn = jnp.where(new_ok[None], bqk(q, kn.astype(BF16)), -jnp.inf)
    mx = jnp.maximum(jnp.maximum(jnp.max(sc, axis=-1, keepdims=True), jnp.max(sn, axis=-1, keepdims=True)), sink)
    pc = jnp.exp2(sc - mx)
    pn = jnp.exp2(sn - mx)
    den = jnp.sum(pc, axis=-1, keepdims=True) + jnp.sum(pn, axis=-1, keepdims=True) + jnp.exp2(sink - mx)
    inv = 1.0 / den
    o2 = bpv((pc * inv).astype(BF16), vc.astype(BF16)) + bpv((pn * inv).astype(BF16), vn.astype(BF16))
    for kv, g in heads:
        rows, lanes = stacked(kv, g)
        h = kv * SWA_GROUP + g
        o_ref[:, :, h * hd:(h + 1) * hd] = o2[:, rows, lanes]
    keep = ncache - seqlen
    ko_ref[:, 0:keep, :] = kc[:, seqlen:ncache, :]
    ko_ref[:, keep:ncache, :] = kn[:, 0:seqlen, :]
    vo_ref[:, 0:keep, :] = vc[:, seqlen:ncache, :]
    vo_ref[:, keep:ncache, :] = vn[:, 0:seqlen, :]


def _attn_sample(sink_rows, q, kn, vn, kc, vc):
    n, seqlen, _ = q.shape
    nseq = 16
    assert n % nseq == 0
    blk = lambda a: pl.BlockSpec((nseq,) + a.shape[1:], lambda i: (i, 0, 0))
    return pl.pallas_call(
        functools.partial(_attn_sample_kernel, seqlen=seqlen),
        grid=(n // nseq,),
        in_specs=[_full(sink_rows.shape), blk(q), blk(kn), blk(vn), blk(kc), blk(vc)],
        out_specs=[blk(q), blk(kc), blk(vc)],
        out_shape=[jax.ShapeDtypeStruct(q.shape, F32), jax.ShapeDtypeStruct(kc.shape, F32),
                   jax.ShapeDtypeStruct(vc.shape, F32)],
        scratch_shapes=[pltpu.VMEM((nseq, SWA_HEADS * seqlen, SWA_KV_HEADS * SWA_HEAD_DIM), F32)],
        compiler_params=_cparams(("parallel",)),
        name="attn_sample",
    )(sink_rows, q, kn, vn, kc, vc)


def _odd_out_kernel(h_ref, o_ref, gate_ref, w_ref, y_ref):
    og = o_ref[...].astype(F32) * gate_ref[...].astype(F32)
    y_ref[...] = h_ref[...] + _dot(og.astype(BF16), w_ref[...])


def _odd_out(h, o, gate, w):
    m, dm = h.shape
    tm = _row_tile(m)
    row = lambda n: pl.BlockSpec((tm, n), lambda i: (i, 0))
    return pl.pallas_call(
        _odd_out_kernel,
        grid=(m // tm,),
        in_specs=[row(dm), row(o.shape[1]), row(gate.shape[1]), _full(w.shape)],
        out_specs=row(dm),
        out_shape=jax.ShapeDtypeStruct((m, dm), F32),
        compiler_params=_cparams(("parallel",)),
        name="odd_out",
    )(h, o, gate, w)


def _trunk(x, s5_x0, gla_s0, pe, po, pos, nseq, seqlen, act_dtype):
    u2, ga, q, k, v, la, gb = _even_in(x, pe['norm_g'], pe['w_in'], pe['wgate'], pe['bgate'], act_dtype,
                                       _s5_chunk(seqlen))
    y4, s5_fin = _s5_branch(u2, s5_x0, pe['s5_ops'], pe['d'], nseq, seqlen, act_dtype)
    ob, gla_fin = _gla(q, k, v, la, gb, gla_s0, pe['gla_norm_g'], nseq, seqlen, act_dtype)
    cos_t, sin_t = _rope_tables(pos)
    h, q1, k1, v1, gate = _mid(x, y4, ga, ob, pe, po, cos_t, sin_t, act_dtype)
    return h, q1, k1, v1, gate, s5_fin, gla_fin


def kernel(x_prompt, x_sample, state_s5_re, state_s5_im, state_gla, cache_swa_k, cache_swa_v,
           even_norm_g, even_w_in, s5_lambda_re, s5_lambda_im, s5_log_dt, s5_b_re, s5_b_im,
           s5_c_re, s5_c_im, s5_d, s5_w_glu, s5_b_glu, gla_w_gate, gla_b_gate, gla_norm_g,
           even_w_out, odd_norm_g, odd_w_in, swa_q_norm_g, swa_k_norm_g, swa_sinks, odd_w_out):
    nb, seq, dm = x_prompt.shape
    ns, dseq, _ = x_sample.shape
    ng = s5_lambda_re.shape[1]
    xp = x_prompt.reshape(nb * seq, dm)
    xs = x_sample.reshape(ns * dseq, dm)

    i = 0
    assert ng * S5_GROUP == S5_WIDTH
    pad_rank = LANES - GLA_GATE_RANK
    pe = {
        'norm_g': even_norm_g[i][None, :],
        'w_in': even_w_in[i].astype(BF16),
        'wgate': jnp.pad(gla_w_gate[i], ((0, pad_rank), (0, 0))).astype(BF16),
        'bgate': gla_b_gate[i][None, :],
        's5_ops': _s5_params(s5_lambda_re[i], s5_lambda_im[i], s5_log_dt[i], s5_b_re[i], s5_b_im[i],
                             s5_c_re[i], s5_c_im[i]),
        'gla_norm_g': gla_norm_g[i][None, :],
        'd': s5_d[i],
        'wglu': s5_w_glu[i].astype(BF16),
        'bglu': s5_b_glu[i][None, :],
        'wo': even_w_out[i].astype(BF16),
    }
    po = {
        'norm_g': odd_norm_g[i][None, :],
        'w_in': odd_w_in[i].astype(BF16),
        'gq': jnp.tile(swa_q_norm_g[i], MXU_TILE // SWA_HEAD_DIM)[None, :],
        'gk': jnp.tile(swa_k_norm_g[i], LANES // SWA_HEAD_DIM)[None, :],
    }
    w_out = odd_w_out[i].astype(BF16)
    sinks = swa_sinks[i]
    kvw = SWA_KV_HEADS * SWA_HEAD_DIM

    s5_zero = jnp.zeros((nb, ng, S5_STATE), F32)
    gla_zero = jnp.zeros((nb, GLA_HEADS, GLA_HEAD_K, GLA_HEAD_V), F32)
    hp, q, k, v, gate, (s5r_p, s5i_p), gla_p = _trunk(xp, (s5_zero, s5_zero), gla_zero, pe, po, np.arange(seq),
                                                      nb, seq, BF16)
    y_prompt = _attn_prompt(sinks, q, k, v, gate, hp, w_out, nb, seq).reshape(nb, seq, dm)
    cache_len = min(SWA_WINDOW, seq)
    tail = lambda a: (a.reshape(nb, seq, kvw)[:, seq - cache_len:]
                      .reshape(1, nb, cache_len, SWA_KV_HEADS, SWA_HEAD_DIM))
    swk_p, swv_p = tail(k), tail(v)

    pos_s = np.tile(PAST_LEN + np.arange(dseq), ns)
    hs, q, k, v, gate, (s5r_s, s5i_s), gla_s = _trunk(xs, (state_s5_re[i], state_s5_im[i]), state_gla[i], pe, po,
                                                      pos_s, ns, dseq, F32)
    ncache = cache_swa_k.shape[2]
    sink_rows = jnp.broadcast_to(jnp.repeat(sinks, dseq)[:, None], (SWA_HEADS * dseq, LANES))
    kn = k.reshape(ns, dseq, kvw)
    vn = v.reshape(ns, dseq, kvw)
    npad = BF16_ROWS - dseq
    kn_pad = jnp.pad(kn, ((0, 0), (0, npad), (0, 0)))
    vn_pad = jnp.pad(vn, ((0, 0), (0, npad), (0, 0)))
    kc = cache_swa_k[i].reshape(ns, ncache, kvw)
    vc = cache_swa_v[i].reshape(ns, ncache, kvw)
    o, kc_new, vc_new = _attn_sample(sink_rows, q.reshape(ns, dseq, -1), kn_pad, vn_pad, kc, vc)
    y_sample = _odd_out(hs, o.reshape(ns * dseq, -1), gate, w_out).reshape(ns, dseq, dm)
    swk_s = kc_new.reshape(1, ns, ncache, SWA_KV_HEADS, SWA_HEAD_DIM)
    swv_s = vc_new.reshape(1, ns, ncache, SWA_KV_HEADS, SWA_HEAD_DIM)

    return (y_prompt, y_sample,
            s5r_p, s5i_p, gla_p[None], swk_p, swv_p,
            s5r_s, s5i_s, gla_s[None], swk_s, swv_s)
```

```python
import functools
import math

import jax
import jax.numpy as jnp
import numpy as np
from jax import lax
from jax.experimental import pallas as pl
from jax.experimental.pallas import tpu as pltpu

F32 = jnp.float32
BF16 = jnp.bfloat16

PAST_LEN = 8192
NORM_EPS = 1e-6
S5_GROUP = 16
S5_STATE = 64
S5_CHUNK = 16
GLA_HEADS = 4
GLA_HEAD_K = 64
GLA_HEAD_V = 128
GLA_GATE_RANK = 16
GLA_GATE_TAU = 16.0
GLA_CHUNK = 64
S5_WIDTH = 512
GLA_K_WIDTH = GLA_HEADS * GLA_HEAD_K
GLA_V_WIDTH = GLA_HEADS * GLA_HEAD_V
SWA_HEADS = 16
SWA_KV_HEADS = 2
SWA_GROUP = SWA_HEADS // SWA_KV_HEADS
SWA_HEAD_DIM = 64
SWA_WINDOW = 128
ROPE_THETA = 10000.0
LOG2E = math.log2(math.e)
LANES = 128
BF16_ROWS = 16
MXU_TILE = 256
OCT = LANES // S5_GROUP
ROW_TILE = 512
VMEM_LIMIT = 48 * 1024 * 1024
C_U = 0
C_GATE_A = C_U + S5_WIDTH
C_Q = C_GATE_A + S5_WIDTH
C_K = C_Q + GLA_K_WIDTH
C_V = C_K + GLA_K_WIDTH
C_CODE = C_V + GLA_V_WIDTH
C_GATE_B = C_CODE + GLA_GATE_RANK
C_END = C_GATE_B + GLA_V_WIDTH


def _cparams(sem):
    return pltpu.CompilerParams(dimension_semantics=sem, vmem_limit_bytes=VMEM_LIMIT)


def _full(shape):
    n = len(shape)
    return pl.BlockSpec(shape, lambda *_: (0,) * n)


def _dot(a, b):
    return jnp.dot(a, b, preferred_element_type=F32)


def _dot_nt(a, b):
    return lax.dot_general(a, b, (((1,), (1,)), ((), ())), preferred_element_type=F32)


def _dot_tn(a, b):
    return lax.dot_general(a, b, (((0,), (0,)), ((), ())), preferred_element_type=F32)


def _split_bf16(x):
    hi = x.astype(BF16)
    lo = (x - hi.astype(F32)).astype(BF16)
    return hi, lo


def _rms_rows(x, g):
    return x * lax.rsqrt(jnp.mean(x * x, axis=-1, keepdims=True) + NORM_EPS) * g


def _sigmoid(x):
    return 1.0 / (1.0 + jnp.exp(-x))


def _silu(x):
    return x * _sigmoid(x)


def _row_tile(m, tile=ROW_TILE):
    return tile if m % tile == 0 else m


def _even_in_kernel(x_ref, g_ref, w_ref, wgate_ref, bgate_ref,
                    u2_ref, ga_ref, q_ref, k_ref, v_ref, la_ref, gb_ref, uscr_ref, wmain_ref, wgb_ref, *, t):
    @pl.when(pl.program_id(0) == 0)
    def _():
        wmain_ref[...] = w_ref[:, :C_CODE + LANES].astype(BF16)
        wgb_ref[...] = w_ref[:, C_GATE_B:C_END].astype(BF16)

    xb = _rms_rows(x_ref[...], g_ref[...]).astype(BF16)

    def proj(lo, hi):
        return _dot(xb, wmain_ref[:, lo:hi])

    u = proj(C_U, C_GATE_A)
    nrow = u.shape[0] // t
    for o in range(u2_ref.shape[0]):
        uscr_ref[o] = u[:, o * LANES:(o + 1) * LANES]
        for tt in range(t):
            piece = uscr_ref[o, pl.ds(tt, nrow, stride=t), :]
            u2_ref[o, :, tt * LANES:(tt + 1) * LANES] = piece.astype(u2_ref.dtype)
    ga_ref[...] = _silu(proj(C_GATE_A, C_Q)).astype(ga_ref.dtype)
    q_ref[...] = (proj(C_Q, C_K) * (GLA_HEAD_K ** -0.5)).astype(q_ref.dtype)
    k_ref[...] = proj(C_K, C_V).astype(k_ref.dtype)
    v_ref[...] = proj(C_V, C_CODE).astype(v_ref.dtype)
    gb_ref[...] = _silu(_dot(xb, wgb_ref[...])).astype(gb_ref.dtype)
    a_low = proj(C_CODE, C_CODE + LANES)
    logit = _dot(a_low.astype(BF16), wgate_ref[...]) + bgate_ref[...]
    log_sig = jnp.minimum(logit, 0.0) - jnp.log1p(jnp.exp(-jnp.abs(logit)))
    la_ref[...] = log_sig * (1.0 / GLA_GATE_TAU)


def _even_in(x, g, w, wgate, bgate, act_dtype, t):
    m, d = x.shape
    tm = _row_tile(m, 2 * ROW_TILE) if m > ROW_TILE else m
    row = lambda n: pl.BlockSpec((tm, n), lambda i: (i, 0))
    assert w.shape[1] == C_END
    no = S5_WIDTH // LANES
    chunk = pl.BlockSpec((no, tm // t, t * LANES), lambda i: (0, i, 0))
    outs = [(S5_WIDTH, act_dtype), (GLA_K_WIDTH, act_dtype), (GLA_K_WIDTH, act_dtype), (GLA_V_WIDTH, act_dtype),
            (GLA_K_WIDTH, F32), (GLA_V_WIDTH, act_dtype)]
    return pl.pallas_call(
        functools.partial(_even_in_kernel, t=t),
        grid=(m // tm,),
        in_specs=[row(d), _full(g.shape),
                  pl.BlockSpec(w.shape, lambda i: (0, 0), pipeline_mode=pl.Buffered(1)),
                  _full(wgate.shape), _full(bgate.shape)],
        out_specs=[chunk] + [row(n) for n, _ in outs],
        out_shape=[jax.ShapeDtypeStruct((no, m // t, t * LANES), act_dtype)]
        + [jax.ShapeDtypeStruct((m, n), dt) for n, dt in outs],
        scratch_shapes=[pltpu.VMEM((no, tm, LANES), F32), pltpu.VMEM((d, C_CODE + LANES), BF16),
                        pltpu.VMEM((d, GLA_V_WIDTH), BF16)],
        compiler_params=_cparams(("arbitrary",)),
        name="even_in",
    )(x, g, w, wgate, bgate)


def _group_mask(shape, row_span, col_span):
    rg = (lax.broadcasted_iota(jnp.int32, shape, 0) // row_span) % OCT
    cg = (lax.broadcasted_iota(jnp.int32, shape, 1) // col_span) % OCT
    return rg == cg


def _s5_state_kernel(u_ref, bre_ref, bim_ref, x0r_ref, x0i_ref, are_ref, aim_ref, xs_ref, xfr_ref, xfi_ref, loc_ref,
                     *, nseq, nchunks):
    hw = OCT * S5_STATE
    kk = u_ref.shape[2]
    bp = jnp.concatenate([bre_ref[0]] * OCT + [bim_ref[0]] * OCT, axis=1)
    bp = jnp.where(_group_mask((kk, 2 * hw), S5_GROUP, S5_STATE), bp, 0.0).astype(BF16)
    loc_ref[...] = _dot(u_ref[0], bp)
    a_re = are_ref[0]
    a_im = aim_ref[0]
    if nchunks == 1:
        xr, xi = x0r_ref[...], x0i_ref[...]
        loc = loc_ref[...]
        xfr_ref[...] = a_re * xr - a_im * xi + loc[:, :hw]
        xfi_ref[...] = a_re * xi + a_im * xr + loc[:, hw:]
        xs_ref[0] = jnp.concatenate([xr, xi], axis=1).astype(xs_ref.dtype)
    else:
        def body(j, carry):
            new = []
            for b in range(nseq):
                xr, xi = carry[b]
                row = pl.ds(b * nchunks + j, 1)
                lr = loc_ref[row, :hw]
                li = loc_ref[row, hw:]
                loc_ref[row, :hw] = xr
                loc_ref[row, hw:] = xi
                new.append((a_re * xr - a_im * xi + lr, a_re * xi + a_im * xr + li))
            return tuple(new)

        init = tuple((x0r_ref[b:b + 1, :], x0i_ref[b:b + 1, :]) for b in range(nseq))
        fin = lax.fori_loop(0, nchunks, body, init, unroll=4)
        for b in range(nseq):
            xfr_ref[b:b + 1, :] = fin[b][0]
            xfi_ref[b:b + 1, :] = fin[b][1]
        xs_ref[0] = loc_ref[...].astype(xs_ref.dtype)


def _s5_state(u2, bre, bim, x0r, x0i, are, aim, nseq, nchunks):
    no, r, kk = u2.shape
    hw = OCT * S5_STATE
    blk = lambda a: pl.BlockSpec((1,) + a.shape[1:], lambda o: (o,) + (0,) * (a.ndim - 1))
    octet = pl.BlockSpec((nseq, hw), lambda o: (0, o))
    return pl.pallas_call(
        functools.partial(_s5_state_kernel, nseq=nseq, nchunks=nchunks),
        grid=(no,),
        in_specs=[blk(u2), blk(bre), blk(bim), octet, octet, blk(are), blk(aim)],
        out_specs=[pl.BlockSpec((1, r, 2 * hw), lambda o: (o, 0, 0)), octet, octet],
        out_shape=[jax.ShapeDtypeStruct((no, r, 2 * hw), BF16), jax.ShapeDtypeStruct(x0r.shape, F32),
                   jax.ShapeDtypeStruct(x0i.shape, F32)],
        scratch_shapes=[pltpu.VMEM((r, 2 * hw), F32)],
        compiler_params=_cparams(("parallel",)),
        name="s5_state",
    )(u2, bre, bim, x0r, x0i, are, aim)


def _dot_nt_f32(a, b):
    a_hi, a_lo = _split_bf16(a)
    b_hi, b_lo = _split_bf16(b)
    return _dot_nt(a_hi, b_hi) + _dot_nt(a_hi, b_lo) + _dot_nt(a_lo, b_hi)


def _s5_out_kernel(u_ref, xs_ref, bre_ref, bim_ref, cre_ref, cim_ref, zre_ref, zim_ref, d_ref, y_ref, yscr_ref):
    r, kk = u_ref.shape[1], u_ref.shape[2]
    t = kk // LANES
    sw = xs_ref.shape[2]
    ntile = kk // MXU_TILE
    taps = _dot_nt_f32(bre_ref[0], zre_ref[0]) - _dot_nt_f32(bim_ref[0], zim_ref[0])
    tmask = _group_mask((LANES, LANES), S5_GROUP, S5_GROUP)

    def tap(lag):
        if lag < 0:
            return jnp.zeros((LANES, LANES), F32)
        s = t - 1 - lag
        return jnp.where(tmask, taps[s * LANES:(s + 1) * LANES], 0.0)

    wts = [jnp.concatenate([jnp.concatenate([tap(2 * d), tap(2 * d + 1)], axis=1),
                            jnp.concatenate([tap(2 * d - 1), tap(2 * d)], axis=1)], axis=0).astype(BF16)
           for d in range(ntile)]
    cpt = jnp.concatenate([cre_ref[0]] * OCT + [cim_ref[0]] * OCT, axis=1)
    cpt = jnp.where(_group_mask((kk, sw), S5_GROUP, S5_STATE), cpt, 0.0).astype(BF16)
    xs = xs_ref[0]
    skip = jnp.concatenate([d_ref[0]] * (MXU_TILE // LANES), axis=1)
    for n in range(ntile):
        cols = slice(n * MXU_TILE, (n + 1) * MXU_TILE)
        acc = _dot_nt(xs, cpt[cols]) + skip * u_ref[0, :, cols].astype(F32)
        for k in range(n + 1):
            acc = acc + _dot(u_ref[0, :, k * MXU_TILE:(k + 1) * MXU_TILE], wts[n - k])
        for e in range(MXU_TILE // LANES):
            yscr_ref[pl.ds(2 * n + e, r, stride=t), :] = acc[:, e * LANES:(e + 1) * LANES]
    y_ref[0] = yscr_ref[...].astype(y_ref.dtype)


def _s5_out(u2, xs, bre, bim, cre, cim, zre, zim, d, out_dtype):
    no, r, kk = u2.shape
    m = r * (kk // LANES)
    blk = lambda a: pl.BlockSpec((1,) + a.shape[1:], lambda o: (o,) + (0,) * (a.ndim - 1))
    ops = (u2, xs, bre, bim, cre, cim, zre, zim, d)
    return pl.pallas_call(
        _s5_out_kernel,
        grid=(no,),
        in_specs=[blk(a) for a in ops],
        out_specs=pl.BlockSpec((1, m, LANES), lambda o: (o, 0, 0)),
        out_shape=jax.ShapeDtypeStruct((no, m, LANES), out_dtype),
        scratch_shapes=[pltpu.VMEM((m, LANES), F32)],
        compiler_params=_cparams(("parallel",)),
        name="s5_out",
    )(*ops)


def _s5_params(lam_re, lam_im, log_dt, b_re, b_im, c_re, c_im):
    t = S5_CHUNK
    ng = lam_re.shape[0]
    no = ng // OCT
    dt = jnp.exp(log_dt)[:, None]
    a = lam_re * dt
    b = lam_im * dt
    n = jnp.arange(t + 1, dtype=F32)[None, :, None]
    mag = jnp.exp(n * a[:, None, :])
    pw_re = mag * jnp.cos(n * b[:, None, :])
    pw_im = mag * jnp.sin(n * b[:, None, :])
    em1_re = jnp.expm1(a) * jnp.cos(b) - 2.0 * jnp.sin(0.5 * b) ** 2
    em1_im = jnp.exp(a) * jnp.sin(b)
    den = lam_re * lam_re + lam_im * lam_im
    z_re = (em1_re * lam_re + em1_im * lam_im) / den
    z_im = (em1_im * lam_re - em1_re * lam_im) / den
    bb_re = z_re[..., None] * b_re - z_im[..., None] * b_im
    bb_im = z_re[..., None] * b_im + z_im[..., None] * b_re

    def per_token(pw):
        return pw.reshape(no, OCT, t, S5_STATE).transpose(0, 2, 1, 3)[:, :, :, None, :]

    def per_group(w):
        return w.reshape(no, 1, OCT, S5_GROUP, S5_STATE)

    rows = lambda w: w.reshape(no, t * LANES, S5_STATE)
    bt_re, bt_im = per_group(bb_re.transpose(0, 2, 1)), per_group(bb_im.transpose(0, 2, 1))
    r_re, r_im = per_token(pw_re[:, t - 1::-1]), per_token(pw_im[:, t - 1::-1])
    bre = rows(r_re * bt_re - r_im * bt_im)
    bim = rows(r_re * bt_im + r_im * bt_re)
    o_re, o_im = per_token(pw_re[:, 1:]), per_token(pw_im[:, 1:])
    cg_re, cg_im = per_group(c_re), per_group(c_im)
    cre = rows(cg_re * o_re - cg_im * o_im)
    cim = rows(-(cg_re * o_im + cg_im * o_re))
    return bre, bim, cre, cim, c_re.reshape(no, LANES, S5_STATE), c_im.reshape(no, LANES, S5_STATE), pw_re, pw_im


def _s5_chunk(seqlen):
    return math.gcd(seqlen, S5_CHUNK)


def _s5_branch(u2, x0, ops, d, nseq, seqlen, out_dtype):
    bre, bim, cre, cim, zre, zim, pw_re, pw_im = ops
    no = u2.shape[0]
    t = _s5_chunk(seqlen)
    kk = t * LANES
    nchunks = seqlen // t
    are = pw_re[:, t].reshape(no, 1, OCT * S5_STATE)
    aim = pw_im[:, t].reshape(no, 1, OCT * S5_STATE)
    tail = S5_CHUNK * LANES - kk
    bre, bim = bre[:, tail:], bim[:, tail:]
    flat = lambda a: a.reshape(nseq, -1)
    xs, xf_re, xf_im = _s5_state(u2, bre, bim, flat(x0[0]), flat(x0[1]), are, aim, nseq, nchunks)
    y = _s5_out(u2, xs, bre, bim, cre[:, :kk], cim[:, :kk], zre, zim, d.reshape(no, 1, LANES), out_dtype)
    return y, (xf_re.reshape((1,) + x0[0].shape), xf_im.reshape((1,) + x0[1].shape))


def _gla_kernel(q_ref, k_ref, v_ref, la_ref, gb_ref, s0_ref, ng_ref, o_ref, sf_ref, st_ref,
                *, nseq, nchunks, c):
    i = pl.program_id(1)

    @pl.when(i == 0)
    def _():
        st_ref[...] = s0_ref[...]

    nh, hk, hv = GLA_HEADS, GLA_HEAD_K, GLA_HEAD_V
    nch = nseq * nchunks
    tm = nch * c
    iota = lambda shape, d: lax.broadcasted_iota(jnp.int32, shape, d)
    ng = ng_ref[...]

    tb = min(tm, MXU_TILE)
    rt, ct = iota((tb, tb), 0), iota((tb, tb), 1)
    tril = ((rt // c == ct // c) & (rt >= ct)).astype(BF16)
    parts = []
    for r0 in range(0, tm, tb):
        la_hi, la_lo = _split_bf16(la_ref[r0:r0 + tb, :])
        parts.append(_dot(tril, la_hi) + _dot(tril, la_lo))
    bcum = jnp.concatenate(parts, axis=0)
    e_hi, e_lo = _split_bf16(jnp.exp(jnp.concatenate([bcum[(ci + 1) * c - 1:(ci + 1) * c] for ci in range(nch)],
                                                     axis=0)))
    pick = (iota((nch, LANES), 0) == iota((nch, LANES), 1)).astype(BF16)
    dec_t = _dot_tn(e_hi, pick) + _dot_tn(e_lo, pick)
    q_all = q_ref[...].astype(F32)
    k_all = k_ref[...].astype(F32)
    q_dec_all = q_all * jnp.exp(bcum)
    k_dec_all = k_all * jnp.exp(-bcum)

    own_k = iota((nh * c, nh * hk), 0) // c == iota((nh * c, nh * hk), 1) // hk
    own_v = iota((nh * c, nh * hv), 0) // c == iota((nh * c, nh * hv), 1) // hv
    causal = iota((c, nh * c), 1) % c <= iota((c, nh * c), 0)
    zero_v = jnp.zeros((hk, hv), F32)

    def intra(ci):
        rows = slice(ci * c, (ci + 1) * c)
        q_dec = q_dec_all[rows].astype(BF16)
        k_dec = k_dec_all[rows]
        b_c = bcum[rows]
        k_tail = (k_all[rows] * jnp.exp(b_c[c - 1:c] - b_c)).astype(BF16)
        v = v_ref[rows, :].astype(F32)
        k_bd = jnp.where(own_k, jnp.concatenate([k_dec] * nh, axis=0), 0.0).astype(BF16)
        v_bd = jnp.where(own_v, jnp.concatenate([v] * nh, axis=0), 0.0).astype(BF16)
        att = jnp.where(causal, _dot_nt(q_dec, k_bd), 0.0)
        o_intra = _dot(att.astype(BF16), v_bd)
        vb = v.astype(BF16)
        kvs = []
        for h0 in range(0, nh, 2):
            kv2 = _dot_tn(k_tail[:, h0 * hk:(h0 + 2) * hk], vb[:, h0 * hv:(h0 + 2) * hv])
            kvs += [kv2[:hk, :hv], kv2[hk:, hv:]]
        return q_dec, o_intra, kvs

    def carry(ci, sts, q_dec, o_intra, kvs):
        st_bd = jnp.concatenate(
            [jnp.concatenate([zero_v] * h + [sts[h]] + [zero_v] * (nh - 1 - h), axis=1) for h in range(nh)],
            axis=0).astype(BF16)
        o = o_intra + _dot(q_dec, st_bd)
        new = [dec_t[h * hk:(h + 1) * hk, ci:ci + 1] * sts[h] + kvs[h] for h in range(nh)]
        return o, new

    def finish(ci, o):
        rows = slice(ci * c, (ci + 1) * c)
        for h in range(nh):
            vs = slice(h * hv, (h + 1) * hv)
            oh = _rms_rows(o[:, vs], ng) * gb_ref[rows, vs].astype(F32)
            o_ref[rows, vs] = oh.astype(o_ref.dtype)

    sts = None
    pending = intra(0)
    unfinished = None
    for ci in range(nch):
        s, first, last = ci // nchunks, ci % nchunks == 0, ci % nchunks == nchunks - 1
        current = pending
        if ci + 1 < nch:
            pending = intra(ci + 1)
        if first:
            sts = [st_ref[s, h] for h in range(nh)]
        o, sts = carry(ci, sts, *current)
        if last:
            for h in range(nh):
                st_ref[s, h] = sts[h]
        if unfinished is not None:
            finish(*unfinished)
        unfinished = (ci, o)
    finish(*unfinished)

    @pl.when(i == pl.num_programs(1) - 1)
    def _():
        sf_ref[...] = st_ref[...]


def _gla(q, k, v, la, gb, s0, ng, nseq_total, seqlen, out_dtype):
    c = math.gcd(seqlen, GLA_CHUNK)
    per_seq = seqlen // c
    if per_seq >= 8:
        nseq, nchunks = 1, next(n for n in (32, 16, 8, 1) if per_seq % n == 0)
    else:
        nseq, nchunks = 16, seqlen // c
    assert nseq_total % nseq == 0 and seqlen % (nchunks * c) == 0
    nblk = seqlen // (nchunks * c)
    tm = nseq * nchunks * c
    row = lambda n: pl.BlockSpec((tm, n), lambda b, i: (b * nblk + i, 0))
    st_spec = pl.BlockSpec((nseq, GLA_HEADS, GLA_HEAD_K, GLA_HEAD_V), lambda b, i: (b, 0, 0, 0))
    m = q.shape[0]
    return pl.pallas_call(
        functools.partial(_gla_kernel, nseq=nseq, nchunks=nchunks, c=c),
        grid=(nseq_total // nseq, nblk),
        in_specs=[row(GLA_K_WIDTH), row(GLA_K_WIDTH), row(GLA_V_WIDTH), row(GLA_K_WIDTH), row(GLA_V_WIDTH),
                  st_spec, _full(ng.shape)],
        out_specs=[row(GLA_V_WIDTH), st_spec],
        out_shape=[jax.ShapeDtypeStruct((m, GLA_V_WIDTH), out_dtype), jax.ShapeDtypeStruct(s0.shape, F32)],
        scratch_shapes=[pltpu.VMEM((nseq, GLA_HEADS, GLA_HEAD_K, GLA_HEAD_V), F32)],
        compiler_params=_cparams(("parallel", "arbitrary")),
        name="gla",
    )(q, k, v, la, gb, s0, ng)


def _gelu_tanh(x):
    return 0.5 * x * (1.0 + jnp.tanh(math.sqrt(2.0 / math.pi) * (x + 0.044715 * (x * x * x))))


def _head_ones(n):
    r = lax.broadcasted_iota(jnp.int32, (n, n), 0) // SWA_HEAD_DIM
    c = lax.broadcasted_iota(jnp.int32, (n, n), 1) // SWA_HEAD_DIM
    return (r == c).astype(BF16)


def _rope_block(x, cos_t, sin_t, upper):
    half = SWA_HEAD_DIM // 2
    swapped = jnp.where(upper, pltpu.roll(x, half, 1), pltpu.roll(x, LANES - half, 1))
    return x * cos_t + swapped * sin_t


def _mid_kernel(x_ref, y_ref, ga_ref, ob_ref, wglu_ref, bglu_ref, wo_ref,
                g_ref, w_ref, gq_ref, gk_ref, cos_ref, sin_ref, h_ref, q_ref, k_ref, v_ref, gate_ref):
    tm = x_ref.shape[0]
    lane = lax.broadcasted_iota(jnp.int32, (tm, LANES), 1)
    upper = (lane & (SWA_HEAD_DIM // 2)) != 0
    ones4 = _head_ones(MXU_TILE)
    inv_d = 1.0 / SWA_HEAD_DIM
    qw = SWA_HEADS * SWA_HEAD_DIM
    kw = SWA_KV_HEADS * SWA_HEAD_DIM
    nblk = qw // MXU_TILE
    no = y_ref.shape[0]

    na = ga_ref.shape[1]
    mix_b = _dot(ob_ref[...].astype(BF16), wo_ref[na:, :])
    z = _gelu_tanh(jnp.concatenate([y_ref[o].astype(F32) for o in range(no)], axis=1))
    z = z * _sigmoid(_dot(z.astype(BF16), wglu_ref[...]) + bglu_ref[...])
    out_a = z * ga_ref[...].astype(F32)
    h = x_ref[...] + (_dot(out_a.astype(BF16), wo_ref[:na, :]) + mix_b)
    h_ref[...] = h

    xb = _rms_rows(h, g_ref[...]).astype(BF16)
    cos_t = cos_ref[...]
    sin_t = sin_ref[...]

    def gate_finish(gate):
        gate_ref[...] = _silu(gate).astype(gate_ref.dtype)

    def q_finish(j, q):
        ss = _dot((q * q).astype(BF16), ones4)
        qn = q * lax.rsqrt(ss * inv_d + NORM_EPS) * gq_ref[...]
        for e in range(MXU_TILE // LANES):
            cols = slice(j * MXU_TILE + e * LANES, j * MXU_TILE + (e + 1) * LANES)
            qe = _rope_block(qn[:, e * LANES:(e + 1) * LANES], cos_t, sin_t, upper)
            q_ref[:, cols] = (qe * (SWA_HEAD_DIM ** -0.5 * LOG2E)).astype(q_ref.dtype)

    def kv_finish(kv):
        k = kv[:, :kw]
        ss = _dot((k * k).astype(BF16), ones4[:kw, :kw])
        kn = k * lax.rsqrt(ss * inv_d + NORM_EPS) * gk_ref[...]
        k_ref[...] = _rope_block(kn, cos_t, sin_t, upper)
        v_ref[...] = kv[:, kw:]

    work = [(slice(j * MXU_TILE, (j + 1) * MXU_TILE), functools.partial(q_finish, j)) for j in range(nblk)]
    work += [(slice(qw, qw + 2 * kw), kv_finish)]
    work += [(slice(qw + 2 * kw, None), gate_finish)]
    pending = _dot(xb, w_ref[:, work[0][0]])
    for n, (_, finish) in enumerate(work):
        current = pending
        if n + 1 < len(work):
            pending = _dot(xb, w_ref[:, work[n + 1][0]])
        finish(current)


def _mid(x, y, ga, ob, pe, po, cos_t, sin_t, act_dtype):
    m, dm = x.shape
    tm = _row_tile(m, 2 * ROW_TILE) if m > ROW_TILE else m
    row = lambda n: pl.BlockSpec((tm, n), lambda i: (i, 0))
    slab = pl.BlockSpec((y.shape[0], tm, LANES), lambda i: (0, i, 0))
    assert cos_t.shape[0] % tm == 0
    nper = cos_t.shape[0] // tm
    tab = pl.BlockSpec((tm, LANES), lambda i: (i % nper, 0))
    qw = SWA_HEADS * SWA_HEAD_DIM
    kw = SWA_KV_HEADS * SWA_HEAD_DIM
    weights = [pe['wglu'], pe['bglu'], pe['wo'], po['norm_g'], po['w_in'], po['gq'], po['gk']]
    resident = lambda w: pl.BlockSpec(w.shape, lambda i: (0,) * w.ndim, pipeline_mode=pl.Buffered(1))
    return pl.pallas_call(
        _mid_kernel,
        grid=(m // tm,),
        in_specs=[row(dm), slab, row(S5_WIDTH), row(GLA_V_WIDTH)] + [resident(w) for w in weights] + [tab, tab],
        out_specs=[row(dm), row(qw), row(kw), row(kw), row(qw)],
        out_shape=[jax.ShapeDtypeStruct((m, dm), F32),
                   jax.ShapeDtypeStruct((m, qw), act_dtype), jax.ShapeDtypeStruct((m, kw), F32),
                   jax.ShapeDtypeStruct((m, kw), F32), jax.ShapeDtypeStruct((m, qw), act_dtype)],
        compiler_params=_cparams(("parallel",)),
        name="mid",
    )(x, y, ga, ob, *weights, cos_t, sin_t)


def _rope_tables(pos):
    half = SWA_HEAD_DIM // 2
    inv_freq = ROPE_THETA ** (-np.arange(half, dtype=np.float64) / half)
    ang = np.asarray(pos, np.float64)[:, None] * inv_freq[None, :]
    cos, sin = np.cos(ang), np.sin(ang)
    cos_t = np.concatenate([cos, cos, cos, cos], axis=1).astype(np.float32)
    sin_t = np.concatenate([-sin, sin, -sin, sin], axis=1).astype(np.float32)
    return jnp.asarray(cos_t), jnp.asarray(sin_t)


def _attn_prompt_kernel(sink_ref, q_ref, kc_ref, kp_ref, vc_ref, vp_ref, gate_ref, h_ref, wout_ref,
                        y_ref, p_ref, o_ref, *, nqb):
    i = pl.program_id(1)
    w = SWA_WINDOW
    hd = SWA_HEAD_DIM
    npair = SWA_GROUP // 2
    lane2 = lax.broadcasted_iota(jnp.int32, (2 * w, LANES), 1)
    low = lane2 < hd
    rr = lax.broadcasted_iota(jnp.int32, (w, w), 0)
    cc = lax.broadcasted_iota(jnp.int32, (w, w), 1)
    tri = cc <= rr
    low_w = cc < hd
    r4 = lax.broadcasted_iota(jnp.int32, (4 * w, LANES), 0)
    c4 = lax.broadcasted_iota(jnp.int32, (4 * w, LANES), 1)
    den_cols = ((r4 < 2 * w) == (c4 < hd)).astype(BF16)
    units = [(jb, kv) for jb in range(nqb) for kv in range(SWA_KV_HEADS)]

    def scores(jb, kv):
        rows = slice(jb * w, (jb + 1) * w)
        if jb == 0:
            k_prev, v_prev = kp_ref[...], vp_ref[...]
        else:
            prev_rows = slice((jb - 1) * w, jb * w)
            k_prev, v_prev = kc_ref[prev_rows, :], vc_ref[prev_rows, :]
        kcat = jnp.concatenate([k_prev, kc_ref[rows, :]], axis=0)
        vcat = jnp.concatenate([v_prev, vc_ref[rows, :]], axis=0)
        own = low if kv == 0 else jnp.logical_not(low)
        k_own = jnp.where(own, kcat, 0.0)
        v_own = jnp.where(own, vcat, 0.0)
        k_oth = pltpu.roll(k_own, hd, 1)
        v_oth = pltpu.roll(v_own, hd, 1)
        k_lo, k_hi = (k_own, k_oth) if kv == 0 else (k_oth, k_own)
        v_lo, v_hi = (v_own, v_oth) if kv == 0 else (v_oth, v_own)
        k_rhs = jnp.concatenate([k_lo, k_hi], axis=0).astype(BF16)
        v_rhs = jnp.concatenate([jnp.concatenate([v_lo, v_hi], axis=0).astype(BF16), den_cols], axis=1)
        qs = jnp.concatenate([q_ref[rows, (kv * npair + pr) * LANES:(kv * npair + pr + 1) * LANES]
                              for pr in range(npair)], axis=0)
        return _dot_nt(qs, k_rhs), v_rhs

    def softmax_pv(kv, first, buf, s_all, v_rhs):
        sink_terms = []
        for pr in range(npair):
            prow = slice(pr * w, (pr + 1) * w)
            pair_terms = []
            for e in range(2):
                s_prev = s_all[prow, (2 * e) * w:(2 * e + 1) * w]
                s_cur = s_all[prow, (2 * e + 1) * w:(2 * e + 2) * w]
                if first:
                    s_prev = jnp.where(i > 0, s_prev, -jnp.inf)
                sc = jnp.where(tri, s_cur, s_prev)
                sink = sink_ref[2 * (kv * npair + pr) + e] * LOG2E
                mx = jnp.maximum(jnp.max(sc, axis=-1, keepdims=True), sink)
                pe = jnp.exp2(sc - mx)
                p_ref[buf, prow, (2 * e) * w:(2 * e + 1) * w] = jnp.where(tri, 0.0, pe).astype(BF16)
                p_ref[buf, prow, (2 * e + 1) * w:(2 * e + 2) * w] = jnp.where(tri, pe, 0.0).astype(BF16)
                pair_terms.append(jnp.exp2(sink - mx))
            sink_terms.append(pair_terms)
        return _dot(p_ref[buf], v_rhs), sink_terms

    def normalise(jb, kv, o_ext, sink_terms):
        rows = slice(jb * w, (jb + 1) * w)
        for pr in range(npair):
            prow = slice(pr * w, (pr + 1) * w)
            cols = slice((kv * npair + pr) * LANES, (kv * npair + pr + 1) * LANES)
            st = jnp.where(low_w, sink_terms[pr][0], sink_terms[pr][1])
            o = o_ext[prow, :LANES] / (o_ext[prow, LANES:] + st)
            o_ref[rows, cols] = o.astype(o_ref.dtype)

    def project(rows):
        og = o_ref[rows, :].astype(F32) * gate_ref[rows, :].astype(F32)
        y_ref[rows, :] = h_ref[rows, :] + _dot(og.astype(BF16), wout_ref[...])

    group = 2
    pending = scores(*units[0])
    unfinished = None
    for n, (jb, kv) in enumerate(units):
        current = pending
        if n + 1 < len(units):
            pending = scores(*units[n + 1])
        result = softmax_pv(kv, jb == 0, n % 2, *current)
        if unfinished is not None:
            normalise(*unfinished)
            done_jb, done_kv = unfinished[:2]
            if done_kv == SWA_KV_HEADS - 1 and (done_jb + 1) % group == 0:
                project(slice((done_jb + 1 - group) * w, (done_jb + 1) * w))
        unfinished = (jb, kv) + result
    normalise(*unfinished)
    project(slice((nqb - group) * w, nqb * w))


def _attn_prompt(sinks, q, k, v, gate, h, w_out, nseq, seqlen):
    w = SWA_WINDOW
    nqb = 8
    tm = nqb * w
    assert seqlen % tm == 0
    nblk = seqlen // tm
    qw = q.shape[1]
    dm = h.shape[1]
    row = lambda n: pl.BlockSpec((tm, n), lambda b, i: (b * nblk + i, 0))
    prev = lambda n: pl.BlockSpec((w, n), lambda b, i: (jnp.maximum((b * nblk + i) * nqb - 1, 0), 0))
    smem = pl.BlockSpec(memory_space=pltpu.SMEM)
    return pl.pallas_call(
        functools.partial(_attn_prompt_kernel, nqb=nqb),
        grid=(nseq, nblk),
        in_specs=[smem, row(qw), row(LANES), prev(LANES), row(LANES), prev(LANES), row(qw), row(dm),
                  _full(w_out.shape)],
        out_specs=row(dm),
        out_shape=jax.ShapeDtypeStruct(h.shape, F32),
        scratch_shapes=[pltpu.VMEM((2, 4 * w, 4 * w), BF16), pltpu.VMEM((tm, qw), BF16)],
        compiler_params=_cparams(("parallel", "arbitrary")),
        name="attn_prompt",
    )(sinks, q, k, k, v, v, gate, h, w_out)


def _attn_sample_kernel(sink_ref, q_ref, kn_ref, vn_ref, kc_ref, vc_ref, o_ref, ko_ref, vo_ref, q2_ref, *, seqlen):
    nq = SWA_HEADS * seqlen
    hd = SWA_HEAD_DIM
    heads = [(kv, g) for kv in range(SWA_KV_HEADS) for g in range(SWA_GROUP)]

    def stacked(kv, g):
        r0 = (kv * SWA_GROUP + g) * seqlen
        return slice(r0, r0 + seqlen), slice(kv * hd, (kv + 1) * hd)

    q2_ref[...] = jnp.zeros(q2_ref.shape, q2_ref.dtype)
    for kv, g in heads:
        rows, lanes = stacked(kv, g)
        h = kv * SWA_GROUP + g
        q2_ref[:, rows, lanes] = q_ref[:, :, h * hd:(h + 1) * hd]
    ncache = kc_ref.shape[1]
    t_row = lax.broadcasted_iota(jnp.int32, (nq, ncache), 0) % seqlen
    c_col = lax.broadcasted_iota(jnp.int32, (nq, ncache), 1)
    cache_ok = c_col > t_row - (SWA_WINDOW - ncache)
    nnew = kn_ref.shape[1]
    t_row_n = lax.broadcasted_iota(jnp.int32, (nq, nnew), 0) % seqlen
    n_col = lax.broadcasted_iota(jnp.int32, (nq, nnew), 1)
    new_ok = n_col <= t_row_n
    sink = (sink_ref[...] * LOG2E)[None, :, 0:1]
    bqk = lambda a, b: lax.dot_general(a, b, (((2,), (2,)), ((0,), (0,))), preferred_element_type=F32)
    bpv = lambda a, b: lax.dot_general(a, b, (((2,), (1,)), ((0,), (0,))), preferred_element_type=F32)
    q = q2_ref[...].astype(BF16)
    kc = kc_ref[...]
    vc = vc_ref[...]
    kn = kn_ref[...]
    vn = vn_ref[...]
    sc = jnp.where(cache_ok[None], bqk(q, kc.astype(BF16)), -jnp.inf)
    sn = jnp.where(new_ok[None], bqk(q, kn.astype(BF16)), -jnp.inf)
    mx = jnp.maximum(jnp.maximum(jnp.max(sc, axis=-1, keepdims=True), jnp.max(sn, axis=-1, keepdims=True)), sink)
    pc = jnp.exp2(sc - mx)
    pn = jnp.exp2(sn - mx)
    den = jnp.sum(pc, axis=-1, keepdims=True) + jnp.sum(pn, axis=-1, keepdims=True) + jnp.exp2(sink - mx)
    inv = 1.0 / den
    o2 = bpv((pc * inv).astype(BF16), vc.astype(BF16)) + bpv((pn * inv).astype(BF16), vn.astype(BF16))
    for kv, g in heads:
        rows, lanes = stacked(kv, g)
        h = kv * SWA_GROUP + g
        o_ref[:, :, h * hd:(h + 1) * hd] = o2[:, rows, lanes]
    keep = ncache - seqlen
    ko_ref[:, 0:keep, :] = kc[:, seqlen:ncache, :]
    ko_ref[:, keep:ncache, :] = kn[:, 0:seqlen, :]
    vo_ref[:, 0:keep, :] = vc[:, seqlen:ncache, :]
    vo_ref[:, keep:ncache, :] = vn[:, 0:seqlen, :]


def _attn_sample(sink_rows, q, kn, vn, kc, vc):
    n, seqlen, _ = q.shape
    nseq = 16
    assert n % nseq == 0
    blk = lambda a: pl.BlockSpec((nseq,) + a.shape[1:], lambda i: (i, 0, 0))
    return pl.pallas_call(
        functools.partial(_attn_sample_kernel, seqlen=seqlen),
        grid=(n // nseq,),
        in_specs=[_full(sink_rows.shape), blk(q), blk(kn), blk(vn), blk(kc), blk(vc)],
        out_specs=[blk(q), blk(kc), blk(vc)],
        out_shape=[jax.ShapeDtypeStruct(q.shape, F32), jax.ShapeDtypeStruct(kc.shape, F32),
                   jax.ShapeDtypeStruct(vc.shape, F32)],
        scratch_shapes=[pltpu.VMEM((nseq, SWA_HEADS * seqlen, SWA_KV_HEADS * SWA_HEAD_DIM), F32)],
        compiler_params=_cparams(("parallel",)),
        name="attn_sample",
    )(sink_rows, q, kn, vn, kc, vc)


def _odd_out_kernel(h_ref, o_ref, gate_ref, w_ref, y_ref):
    og = o_ref[...].astype(F32) * gate_ref[...].astype(F32)
    y_ref[...] = h_ref[...] + _dot(og.astype(BF16), w_ref[...])


def _odd_out(h, o, gate, w):
    m, dm = h.shape
    tm = _row_tile(m)
    row = lambda n: pl.BlockSpec((tm, n), lambda i: (i, 0))
    return pl.pallas_call(
        _odd_out_kernel,
        grid=(m // tm,),
        in_specs=[row(dm), row(o.shape[1]), row(gate.shape[1]), _full(w.shape)],
        out_specs=row(dm),
        out_shape=jax.ShapeDtypeStruct((m, dm), F32),
        compiler_params=_cparams(("parallel",)),
        name="odd_out",
    )(h, o, gate, w)


def _trunk(x, s5_x0, gla_s0, pe, po, pos, nseq, seqlen, act_dtype):
    u2, ga, q, k, v, la, gb = _even_in(x, pe['norm_g'], pe['w_in'], pe['wgate'], pe['bgate'], act_dtype,
                                       _s5_chunk(seqlen))
    y4, s5_fin = _s5_branch(u2, s5_x0, pe['s5_ops'], pe['d'], nseq, seqlen, act_dtype)
    ob, gla_fin = _gla(q, k, v, la, gb, gla_s0, pe['gla_norm_g'], nseq, seqlen, act_dtype)
    cos_t, sin_t = _rope_tables(pos)
    h, q1, k1, v1, gate = _mid(x, y4, ga, ob, pe, po, cos_t, sin_t, act_dtype)
    return h, q1, k1, v1, gate, s5_fin, gla_fin


def kernel(x_prompt, x_sample, state_s5_re, state_s5_im, state_gla, cache_swa_k, cache_swa_v,
           even_norm_g, even_w_in, s5_lambda_re, s5_lambda_im, s5_log_dt, s5_b_re, s5_b_im,
           s5_c_re, s5_c_im, s5_d, s5_w_glu, s5_b_glu, gla_w_gate, gla_b_gate, gla_norm_g,
           even_w_out, odd_norm_g, odd_w_in, swa_q_norm_g, swa_k_norm_g, swa_sinks, odd_w_out):
    nb, seq, dm = x_prompt.shape
    ns, dseq, _ = x_sample.shape
    ng = s5_lambda_re.shape[1]
    xp = x_prompt.reshape(nb * seq, dm)
    xs = x_sample.reshape(ns * dseq, dm)

    i = 0
    assert ng * S5_GROUP == S5_WIDTH
    pad_rank = LANES - GLA_GATE_RANK
    pe = {
        'norm_g': even_norm_g[i][None, :],
        'w_in': even_w_in[i],
        'wgate': jnp.pad(gla_w_gate[i], ((0, pad_rank), (0, 0))).astype(BF16),
        'bgate': gla_b_gate[i][None, :],
        's5_ops': _s5_params(s5_lambda_re[i], s5_lambda_im[i], s5_log_dt[i], s5_b_re[i], s5_b_im[i],
                             s5_c_re[i], s5_c_im[i]),
        'gla_norm_g': gla_norm_g[i][None, :],
        'd': s5_d[i],
        'wglu': s5_w_glu[i].astype(BF16),
        'bglu': s5_b_glu[i][None, :],
        'wo': even_w_out[i].astype(BF16),
    }
    po = {
        'norm_g': odd_norm_g[i][None, :],
        'w_in': odd_w_in[i].astype(BF16),
        'gq': jnp.tile(swa_q_norm_g[i], MXU_TILE // SWA_HEAD_DIM)[None, :],
        'gk': jnp.tile(swa_k_norm_g[i], LANES // SWA_HEAD_DIM)[None, :],
    }
    w_out = odd_w_out[i].astype(BF16)
    sinks = swa_sinks[i]
    kvw = SWA_KV_HEADS * SWA_HEAD_DIM

    s5_zero = jnp.zeros((nb, ng, S5_STATE), F32)
    gla_zero = jnp.zeros((nb, GLA_HEADS, GLA_HEAD_K, GLA_HEAD_V), F32)
    hp, q, k, v, gate, (s5r_p, s5i_p), gla_p = _trunk(xp, (s5_zero, s5_zero), gla_zero, pe, po, np.arange(seq),
                                                      nb, seq, BF16)
    y_prompt = _attn_prompt(sinks, q, k, v, gate, hp, w_out, nb, seq).reshape(nb, seq, dm)
    cache_len = min(SWA_WINDOW, seq)
    tail = lambda a: (a.reshape(nb, seq, kvw)[:, seq - cache_len:]
                      .reshape(1, nb, cache_len, SWA_KV_HEADS, SWA_HEAD_DIM))
    swk_p, swv_p = tail(k), tail(v)

    pos_s = np.tile(PAST_LEN + np.arange(dseq), ns)
    hs, q, k, v, gate, (s5r_s, s5i_s), gla_s = _trunk(xs, (state_s5_re[i], state_s5_im[i]), state_gla[i], pe, po,
                                                      pos_s, ns, dseq, F32)
    ncache = cache_swa_k.shape[2]
    sink_rows = jnp.broadcast_to(jnp.repeat(sinks, dseq)[:, None], (SWA_HEADS * dseq, LANES))
    kn = k.reshape(ns, dseq, kvw)
    vn = v.reshape(ns, dseq, kvw)
    npad = BF16_ROWS - dseq
    kn_pad = jnp.pad(kn, ((0, 0), (0, npad), (0, 0)))
    vn_pad = jnp.pad(vn, ((0, 0), (0, npad), (0, 0)))
    kc = cache_swa_k[i].reshape(ns, ncache, kvw)
    vc = cache_swa_v[i].reshape(ns, ncache, kvw)
    o, kc_new, vc_new = _attn_sample(sink_rows, q.reshape(ns, dseq, -1), kn_pad, vn_pad, kc, vc)
    y_sample = _odd_out(hs, o.reshape(ns * dseq, -1), gate, w_out).reshape(ns, dseq, dm)
    swk_s = kc_new.reshape(1, ns, ncache, SWA_KV_HEADS, SWA_HEAD_DIM)
    swv_s = vc_new.reshape(1, ns, ncache, SWA_KV_HEADS, SWA_HEAD_DIM)

    return (y_prompt, y_sample,
            s5r_p, s5i_p, gla_p[None], swk_p, swv_p,
            s5r_s, s5i_s, gla_s[None], swk_s, swv_s)
```

```python
import functools
import math

import jax
import jax.numpy as jnp
import numpy as np
from jax import lax
from jax.experimental import pallas as pl
from jax.experimental.pallas import tpu as pltpu

F32 = jnp.float32
BF16 = jnp.bfloat16

PAST_LEN = 8192
NORM_EPS = 1e-6
S5_GROUP = 16
S5_STATE = 64
S5_CHUNK = 16
GLA_HEADS = 4
GLA_HEAD_K = 64
GLA_HEAD_V = 128
GLA_GATE_RANK = 16
GLA_GATE_TAU = 16.0
GLA_CHUNK = 64
S5_WIDTH = 512
GLA_K_WIDTH = GLA_HEADS * GLA_HEAD_K
GLA_V_WIDTH = GLA_HEADS * GLA_HEAD_V
SWA_HEADS = 16
SWA_KV_HEADS = 2
SWA_GROUP = SWA_HEADS // SWA_KV_HEADS
SWA_HEAD_DIM = 64
SWA_WINDOW = 128
ROPE_THETA = 10000.0
LOG2E = math.log2(math.e)
LANES = 128
BF16_ROWS = 16
MXU_TILE = 256
OCT = LANES // S5_GROUP
ROW_TILE = 512
VMEM_LIMIT = 48 * 1024 * 1024
C_U = 0
C_GATE_A = C_U + S5_WIDTH
C_Q = C_GATE_A + S5_WIDTH
C_K = C_Q + GLA_K_WIDTH
C_V = C_K + GLA_K_WIDTH
C_CODE = C_V + GLA_V_WIDTH
C_GATE_B = C_CODE + GLA_GATE_RANK
C_END = C_GATE_B + GLA_V_WIDTH


def _cparams(sem):
    return pltpu.CompilerParams(dimension_semantics=sem, vmem_limit_bytes=VMEM_LIMIT)


def _full(shape):
    n = len(shape)
    return pl.BlockSpec(shape, lambda *_: (0,) * n)


def _dot(a, b):
    return jnp.dot(a, b, preferred_element_type=F32)


def _dot_nt(a, b):
    return lax.dot_general(a, b, (((1,), (1,)), ((), ())), preferred_element_type=F32)


def _dot_tn(a, b):
    return lax.dot_general(a, b, (((0,), (0,)), ((), ())), preferred_element_type=F32)


def _split_bf16(x):
    hi = x.astype(BF16)
    lo = (x - hi.astype(F32)).astype(BF16)
    return hi, lo


def _rms_rows(x, g):
    return x * lax.rsqrt(jnp.mean(x * x, axis=-1, keepdims=True) + NORM_EPS) * g


def _sigmoid(x):
    return 1.0 / (1.0 + jnp.exp(-x))


def _silu(x):
    return x * _sigmoid(x)


def _row_tile(m, tile=ROW_TILE):
    return tile if m % tile == 0 else m


def _even_in_kernel(x_ref, g_ref, w_ref, wgate_ref, bgate_ref,
                    u2_ref, ga_ref, q_ref, k_ref, v_ref, la_ref, gb_ref, uscr_ref, wgb_ref, *, t):
    @pl.when(pl.program_id(0) == 0)
    def _():
        wgb_ref[...] = w_ref[:, C_GATE_B:C_END]

    xb = _rms_rows(x_ref[...], g_ref[...]).astype(BF16)

    def proj(lo, hi):
        return _dot(xb, w_ref[:, lo:hi])

    u = proj(C_U, C_GATE_A)
    nrow = u.shape[0] // t
    for o in range(u2_ref.shape[0]):
        uscr_ref[o] = u[:, o * LANES:(o + 1) * LANES]
        for tt in range(t):
            piece = uscr_ref[o, pl.ds(tt, nrow, stride=t), :]
            u2_ref[o, :, tt * LANES:(tt + 1) * LANES] = piece.astype(u2_ref.dtype)
    ga_ref[...] = _silu(proj(C_GATE_A, C_Q)).astype(ga_ref.dtype)
    q_ref[...] = (proj(C_Q, C_K) * (GLA_HEAD_K ** -0.5)).astype(q_ref.dtype)
    k_ref[...] = proj(C_K, C_V).astype(k_ref.dtype)
    v_ref[...] = proj(C_V, C_CODE).astype(v_ref.dtype)
    gb_ref[...] = _silu(_dot(xb, wgb_ref[...])).astype(gb_ref.dtype)
    a_low = proj(C_CODE, C_CODE + LANES)
    logit = _dot(a_low.astype(BF16), wgate_ref[...]) + bgate_ref[...]
    log_sig = jnp.minimum(logit, 0.0) - jnp.log1p(jnp.exp(-jnp.abs(logit)))
    la_ref[...] = log_sig * (1.0 / GLA_GATE_TAU)


def _even_in(x, g, w, wgate, bgate, act_dtype, t):
    m, d = x.shape
    tm = _row_tile(m, 2 * ROW_TILE) if m > ROW_TILE else m
    row = lambda n: pl.BlockSpec((tm, n), lambda i: (i, 0))
    assert w.shape[1] == C_END
    no = S5_WIDTH // LANES
    chunk = pl.BlockSpec((no, tm // t, t * LANES), lambda i: (0, i, 0))
    outs = [(S5_WIDTH, act_dtype), (GLA_K_WIDTH, act_dtype), (GLA_K_WIDTH, act_dtype), (GLA_V_WIDTH, act_dtype),
            (GLA_K_WIDTH, F32), (GLA_V_WIDTH, act_dtype)]
    return pl.pallas_call(
        functools.partial(_even_in_kernel, t=t),
        grid=(m // tm,),
        in_specs=[row(d), _full(g.shape), _full(w.shape), _full(wgate.shape), _full(bgate.shape)],
        out_specs=[chunk] + [row(n) for n, _ in outs],
        out_shape=[jax.ShapeDtypeStruct((no, m // t, t * LANES), act_dtype)]
        + [jax.ShapeDtypeStruct((m, n), dt) for n, dt in outs],
        scratch_shapes=[pltpu.VMEM((no, tm, LANES), F32), pltpu.VMEM((d, GLA_V_WIDTH), BF16)],
        compiler_params=_cparams(("arbitrary",)),
        name="even_in",
    )(x, g, w, wgate, bgate)


def _group_mask(shape, row_span, col_span):
    rg = (lax.broadcasted_iota(jnp.int32, shape, 0) // row_span) % OCT
    cg = (lax.broadcasted_iota(jnp.int32, shape, 1) // col_span) % OCT
    return rg == cg


def _s5_state_kernel(u_ref, bre_ref, bim_ref, x0r_ref, x0i_ref, are_ref, aim_ref, xs_ref, xfr_ref, xfi_ref, loc_ref,
                     *, nseq, nchunks):
    hw = OCT * S5_STATE
    kk = u_ref.shape[2]
    bp = jnp.concatenate([bre_ref[0]] * OCT + [bim_ref[0]] * OCT, axis=1)
    bp = jnp.where(_group_mask((kk, 2 * hw), S5_GROUP, S5_STATE), bp, 0.0).astype(BF16)
    loc_ref[...] = _dot(u_ref[0], bp)
    a_re = are_ref[0]
    a_im = aim_ref[0]
    if nchunks == 1:
        xr, xi = x0r_ref[...], x0i_ref[...]
        loc = loc_ref[...]
        xfr_ref[...] = a_re * xr - a_im * xi + loc[:, :hw]
        xfi_ref[...] = a_re * xi + a_im * xr + loc[:, hw:]
        xs_ref[0] = jnp.concatenate([xr, xi], axis=1).astype(xs_ref.dtype)
    else:
        def body(j, carry):
            new = []
            for b in range(nseq):
                xr, xi = carry[b]
                row = pl.ds(b * nchunks + j, 1)
                lr = loc_ref[row, :hw]
                li = loc_ref[row, hw:]
                loc_ref[row, :hw] = xr
                loc_ref[row, hw:] = xi
                new.append((a_re * xr - a_im * xi + lr, a_re * xi + a_im * xr + li))
            return tuple(new)

        init = tuple((x0r_ref[b:b + 1, :], x0i_ref[b:b + 1, :]) for b in range(nseq))
        fin = lax.fori_loop(0, nchunks, body, init, unroll=4)
        for b in range(nseq):
            xfr_ref[b:b + 1, :] = fin[b][0]
            xfi_ref[b:b + 1, :] = fin[b][1]
        xs_ref[0] = loc_ref[...].astype(xs_ref.dtype)


def _s5_state(u2, bre, bim, x0r, x0i, are, aim, nseq, nchunks):
    no, r, kk = u2.shape
    hw = OCT * S5_STATE
    blk = lambda a: pl.BlockSpec((1,) + a.shape[1:], lambda o: (o,) + (0,) * (a.ndim - 1))
    octet = pl.BlockSpec((nseq, hw), lambda o: (0, o))
    return pl.pallas_call(
        functools.partial(_s5_state_kernel, nseq=nseq, nchunks=nchunks),
        grid=(no,),
        in_specs=[blk(u2), blk(bre), blk(bim), octet, octet, blk(are), blk(aim)],
        out_specs=[pl.BlockSpec((1, r, 2 * hw), lambda o: (o, 0, 0)), octet, octet],
        out_shape=[jax.ShapeDtypeStruct((no, r, 2 * hw), BF16), jax.ShapeDtypeStruct(x0r.shape, F32),
                   jax.ShapeDtypeStruct(x0i.shape, F32)],
        scratch_shapes=[pltpu.VMEM((r, 2 * hw), F32)],
        compiler_params=_cparams(("parallel",)),
        name="s5_state",
    )(u2, bre, bim, x0r, x0i, are, aim)


def _dot_nt_f32(a, b):
    a_hi, a_lo = _split_bf16(a)
    b_hi, b_lo = _split_bf16(b)
    return _dot_nt(a_hi, b_hi) + _dot_nt(a_hi, b_lo) + _dot_nt(a_lo, b_hi)


def _s5_out_kernel(u_ref, xs_ref, bre_ref, bim_ref, cre_ref, cim_ref, zre_ref, zim_ref, d_ref, y_ref, yscr_ref):
    r, kk = u_ref.shape[1], u_ref.shape[2]
    t = kk // LANES
    sw = xs_ref.shape[2]
    ntile = kk // MXU_TILE
    taps = _dot_nt_f32(bre_ref[0], zre_ref[0]) - _dot_nt_f32(bim_ref[0], zim_ref[0])
    tmask = _group_mask((LANES, LANES), S5_GROUP, S5_GROUP)

    def tap(lag):
        if lag < 0:
            return jnp.zeros((LANES, LANES), F32)
        s = t - 1 - lag
        return jnp.where(tmask, taps[s * LANES:(s + 1) * LANES], 0.0)

    wts = [jnp.concatenate([jnp.concatenate([tap(2 * d), tap(2 * d + 1)], axis=1),
                            jnp.concatenate([tap(2 * d - 1), tap(2 * d)], axis=1)], axis=0).astype(BF16)
           for d in range(ntile)]
    cpt = jnp.concatenate([cre_ref[0]] * OCT + [cim_ref[0]] * OCT, axis=1)
    cpt = jnp.where(_group_mask((kk, sw), S5_GROUP, S5_STATE), cpt, 0.0).astype(BF16)
    xs = xs_ref[0]
    skip = jnp.concatenate([d_ref[0]] * (MXU_TILE // LANES), axis=1)
    for n in range(ntile):
        cols = slice(n * MXU_TILE, (n + 1) * MXU_TILE)
        acc = _dot_nt(xs, cpt[cols]) + skip * u_ref[0, :, cols].astype(F32)
        for k in range(n + 1):
            acc = acc + _dot(u_ref[0, :, k * MXU_TILE:(k + 1) * MXU_TILE], wts[n - k])
        for e in range(MXU_TILE // LANES):
            yscr_ref[pl.ds(2 * n + e, r, stride=t), :] = acc[:, e * LANES:(e + 1) * LANES]
    y_ref[0] = yscr_ref[...].astype(y_ref.dtype)


def _s5_out(u2, xs, bre, bim, cre, cim, zre, zim, d, out_dtype):
    no, r, kk = u2.shape
    m = r * (kk // LANES)
    blk = lambda a: pl.BlockSpec((1,) + a.shape[1:], lambda o: (o,) + (0,) * (a.ndim - 1))
    ops = (u2, xs, bre, bim, cre, cim, zre, zim, d)
    return pl.pallas_call(
        _s5_out_kernel,
        grid=(no,),
        in_specs=[blk(a) for a in ops],
        out_specs=pl.BlockSpec((1, m, LANES), lambda o: (o, 0, 0)),
        out_shape=jax.ShapeDtypeStruct((no, m, LANES), out_dtype),
        scratch_shapes=[pltpu.VMEM((m, LANES), F32)],
        compiler_params=_cparams(("parallel",)),
        name="s5_out",
    )(*ops)


def _s5_params(lam_re, lam_im, log_dt, b_re, b_im, c_re, c_im):
    t = S5_CHUNK
    ng = lam_re.shape[0]
    no = ng // OCT
    dt = jnp.exp(log_dt)[:, None]
    a = lam_re * dt
    b = lam_im * dt
    n = jnp.arange(t + 1, dtype=F32)[None, :, None]
    mag = jnp.exp(n * a[:, None, :])
    pw_re = mag * jnp.cos(n * b[:, None, :])
    pw_im = mag * jnp.sin(n * b[:, None, :])
    em1_re = jnp.expm1(a) * jnp.cos(b) - 2.0 * jnp.sin(0.5 * b) ** 2
    em1_im = jnp.exp(a) * jnp.sin(b)
    den = lam_re * lam_re + lam_im * lam_im
    z_re = (em1_re * lam_re + em1_im * lam_im) / den
    z_im = (em1_im * lam_re - em1_re * lam_im) / den
    bb_re = z_re[..., None] * b_re - z_im[..., None] * b_im
    bb_im = z_re[..., None] * b_im + z_im[..., None] * b_re

    def per_token(pw):
        return pw.reshape(no, OCT, t, S5_STATE).transpose(0, 2, 1, 3)[:, :, :, None, :]

    def per_group(w):
        return w.reshape(no, 1, OCT, S5_GROUP, S5_STATE)

    rows = lambda w: w.reshape(no, t * LANES, S5_STATE)
    bt_re, bt_im = per_group(bb_re.transpose(0, 2, 1)), per_group(bb_im.transpose(0, 2, 1))
    r_re, r_im = per_token(pw_re[:, t - 1::-1]), per_token(pw_im[:, t - 1::-1])
    bre = rows(r_re * bt_re - r_im * bt_im)
    bim = rows(r_re * bt_im + r_im * bt_re)
    o_re, o_im = per_token(pw_re[:, 1:]), per_token(pw_im[:, 1:])
    cg_re, cg_im = per_group(c_re), per_group(c_im)
    cre = rows(cg_re * o_re - cg_im * o_im)
    cim = rows(-(cg_re * o_im + cg_im * o_re))
    return bre, bim, cre, cim, c_re.reshape(no, LANES, S5_STATE), c_im.reshape(no, LANES, S5_STATE), pw_re, pw_im


def _s5_chunk(seqlen):
    return math.gcd(seqlen, S5_CHUNK)


def _s5_branch(u2, x0, ops, d, nseq, seqlen, out_dtype):
    bre, bim, cre, cim, zre, zim, pw_re, pw_im = ops
    no = u2.shape[0]
    t = _s5_chunk(seqlen)
    kk = t * LANES
    nchunks = seqlen // t
    are = pw_re[:, t].reshape(no, 1, OCT * S5_STATE)
    aim = pw_im[:, t].reshape(no, 1, OCT * S5_STATE)
    tail = S5_CHUNK * LANES - kk
    bre, bim = bre[:, tail:], bim[:, tail:]
    flat = lambda a: a.reshape(nseq, -1)
    xs, xf_re, xf_im = _s5_state(u2, bre, bim, flat(x0[0]), flat(x0[1]), are, aim, nseq, nchunks)
    y = _s5_out(u2, xs, bre, bim, cre[:, :kk], cim[:, :kk], zre, zim, d.reshape(no, 1, LANES), out_dtype)
    return y, (xf_re.reshape((1,) + x0[0].shape), xf_im.reshape((1,) + x0[1].shape))


def _gla_kernel(q_ref, k_ref, v_ref, la_ref, gb_ref, s0_ref, ng_ref, o_ref, sf_ref, st_ref,
                *, nseq, nchunks, c):
    i = pl.program_id(1)

    @pl.when(i == 0)
    def _():
        st_ref[...] = s0_ref[...]

    nh, hk, hv = GLA_HEADS, GLA_HEAD_K, GLA_HEAD_V
    nch = nseq * nchunks
    tm = nch * c
    iota = lambda shape, d: lax.broadcasted_iota(jnp.int32, shape, d)
    ng = ng_ref[...]

    tb = min(tm, MXU_TILE)
    rt, ct = iota((tb, tb), 0), iota((tb, tb), 1)
    tril = ((rt // c == ct // c) & (rt >= ct)).astype(BF16)
    parts = []
    for r0 in range(0, tm, tb):
        la_hi, la_lo = _split_bf16(la_ref[r0:r0 + tb, :])
        parts.append(_dot(tril, la_hi) + _dot(tril, la_lo))
    bcum = jnp.concatenate(parts, axis=0)
    e_hi, e_lo = _split_bf16(jnp.exp(jnp.concatenate([bcum[(ci + 1) * c - 1:(ci + 1) * c] for ci in range(nch)],
                                                     axis=0)))
    pick = (iota((nch, LANES), 0) == iota((nch, LANES), 1)).astype(BF16)
    dec_t = _dot_tn(e_hi, pick) + _dot_tn(e_lo, pick)
    q_all = q_ref[...].astype(F32)
    k_all = k_ref[...].astype(F32)
    q_dec_all = q_all * jnp.exp(bcum)
    k_dec_all = k_all * jnp.exp(-bcum)

    own_k = iota((nh * c, nh * hk), 0) // c == iota((nh * c, nh * hk), 1) // hk
    own_v = iota((nh * c, nh * hv), 0) // c == iota((nh * c, nh * hv), 1) // hv
    causal = iota((c, nh * c), 1) % c <= iota((c, nh * c), 0)
    zero_v = jnp.zeros((hk, hv), F32)

    def intra(ci):
        rows = slice(ci * c, (ci + 1) * c)
        q_dec = q_dec_all[rows].astype(BF16)
        k_dec = k_dec_all[rows]
        b_c = bcum[rows]
        k_tail = (k_all[rows] * jnp.exp(b_c[c - 1:c] - b_c)).astype(BF16)
        v = v_ref[rows, :].astype(F32)
        k_bd = jnp.where(own_k, jnp.concatenate([k_dec] * nh, axis=0), 0.0).astype(BF16)
        v_bd = jnp.where(own_v, jnp.concatenate([v] * nh, axis=0), 0.0).astype(BF16)
        att = jnp.where(causal, _dot_nt(q_dec, k_bd), 0.0)
        o_intra = _dot(att.astype(BF16), v_bd)
        vb = v.astype(BF16)
        kvs = []
        for h0 in range(0, nh, 2):
            kv2 = _dot_tn(k_tail[:, h0 * hk:(h0 + 2) * hk], vb[:, h0 * hv:(h0 + 2) * hv])
            kvs += [kv2[:hk, :hv], kv2[hk:, hv:]]
        return q_dec, o_intra, kvs

    def carry(ci, sts, q_dec, o_intra, kvs):
        st_bd = jnp.concatenate(
            [jnp.concatenate([zero_v] * h + [sts[h]] + [zero_v] * (nh - 1 - h), axis=1) for h in range(nh)],
            axis=0).astype(BF16)
        o = o_intra + _dot(q_dec, st_bd)
        new = [dec_t[h * hk:(h + 1) * hk, ci:ci + 1] * sts[h] + kvs[h] for h in range(nh)]
        return o, new

    def finish(ci, o):
        rows = slice(ci * c, (ci + 1) * c)
        for h in range(nh):
            vs = slice(h * hv, (h + 1) * hv)
            oh = _rms_rows(o[:, vs], ng) * gb_ref[rows, vs].astype(F32)
            o_ref[rows, vs] = oh.astype(o_ref.dtype)

    sts = None
    pending = intra(0)
    unfinished = None
    for ci in range(nch):
        s, first, last = ci // nchunks, ci % nchunks == 0, ci % nchunks == nchunks - 1
        current = pending
        if ci + 1 < nch:
            pending = intra(ci + 1)
        if first:
            sts = [st_ref[s, h] for h in range(nh)]
        o, sts = carry(ci, sts, *current)
        if last:
            for h in range(nh):
                st_ref[s, h] = sts[h]
        if unfinished is not None:
            finish(*unfinished)
        unfinished = (ci, o)
    finish(*unfinished)

    @pl.when(i == pl.num_programs(1) - 1)
    def _():
        sf_ref[...] = st_ref[...]


def _gla(q, k, v, la, gb, s0, ng, nseq_total, seqlen, out_dtype):
    c = math.gcd(seqlen, GLA_CHUNK)
    per_seq = seqlen // c
    if per_seq >= 8:
        nseq, nchunks = 1, next(n for n in (32, 16, 8, 1) if per_seq % n == 0)
    else:
        nseq, nchunks = 16, seqlen // c
    assert nseq_total % nseq == 0 and seqlen % (nchunks * c) == 0
    nblk = seqlen // (nchunks * c)
    tm = nseq * nchunks * c
    row = lambda n: pl.BlockSpec((tm, n), lambda b, i: (b * nblk + i, 0))
    st_spec = pl.BlockSpec((nseq, GLA_HEADS, GLA_HEAD_K, GLA_HEAD_V), lambda b, i: (b, 0, 0, 0))
    m = q.shape[0]
    return pl.pallas_call(
        functools.partial(_gla_kernel, nseq=nseq, nchunks=nchunks, c=c),
        grid=(nseq_total // nseq, nblk),
        in_specs=[row(GLA_K_WIDTH), row(GLA_K_WIDTH), row(GLA_V_WIDTH), row(GLA_K_WIDTH), row(GLA_V_WIDTH),
                  st_spec, _full(ng.shape)],
        out_specs=[row(GLA_V_WIDTH), st_spec],
        out_shape=[jax.ShapeDtypeStruct((m, GLA_V_WIDTH), out_dtype), jax.ShapeDtypeStruct(s0.shape, F32)],
        scratch_shapes=[pltpu.VMEM((nseq, GLA_HEADS, GLA_HEAD_K, GLA_HEAD_V), F32)],
        compiler_params=_cparams(("parallel", "arbitrary")),
        name="gla",
    )(q, k, v, la, gb, s0, ng)


def _gelu_tanh(x):
    return 0.5 * x * (1.0 + jnp.tanh(math.sqrt(2.0 / math.pi) * (x + 0.044715 * (x * x * x))))


def _head_ones(n):
    r = lax.broadcasted_iota(jnp.int32, (n, n), 0) // SWA_HEAD_DIM
    c = lax.broadcasted_iota(jnp.int32, (n, n), 1) // SWA_HEAD_DIM
    return (r == c).astype(BF16)


def _rope_block(x, cos_t, sin_t, upper):
    half = SWA_HEAD_DIM // 2
    swapped = jnp.where(upper, pltpu.roll(x, half, 1), pltpu.roll(x, LANES - half, 1))
    return x * cos_t + swapped * sin_t


def _mid_kernel(x_ref, y_ref, ga_ref, ob_ref, wglu_ref, bglu_ref, wo_ref,
                g_ref, w_ref, gq_ref, gk_ref, cos_ref, sin_ref, h_ref, q_ref, k_ref, v_ref, gate_ref):
    tm = x_ref.shape[0]
    lane = lax.broadcasted_iota(jnp.int32, (tm, LANES), 1)
    upper = (lane & (SWA_HEAD_DIM // 2)) != 0
    ones4 = _head_ones(MXU_TILE)
    inv_d = 1.0 / SWA_HEAD_DIM
    qw = SWA_HEADS * SWA_HEAD_DIM
    kw = SWA_KV_HEADS * SWA_HEAD_DIM
    nblk = qw // MXU_TILE
    no = y_ref.shape[0]

    na = ga_ref.shape[1]
    mix_b = _dot(ob_ref[...].astype(BF16), wo_ref[na:, :])
    z = _gelu_tanh(jnp.concatenate([y_ref[o].astype(F32) for o in range(no)], axis=1))
    z = z * _sigmoid(_dot(z.astype(BF16), wglu_ref[...]) + bglu_ref[...])
    out_a = z * ga_ref[...].astype(F32)
    h = x_ref[...] + (_dot(out_a.astype(BF16), wo_ref[:na, :]) + mix_b)
    h_ref[...] = h

    xb = _rms_rows(h, g_ref[...]).astype(BF16)
    cos_t = cos_ref[...]
    sin_t = sin_ref[...]

    def gate_finish(gate):
        gate_ref[...] = _silu(gate).astype(gate_ref.dtype)

    def q_finish(j, q):
        ss = _dot((q * q).astype(BF16), ones4)
        qn = q * lax.rsqrt(ss * inv_d + NORM_EPS) * gq_ref[...]
        for e in range(MXU_TILE // LANES):
            cols = slice(j * MXU_TILE + e * LANES, j * MXU_TILE + (e + 1) * LANES)
            qe = _rope_block(qn[:, e * LANES:(e + 1) * LANES], cos_t, sin_t, upper)
            q_ref[:, cols] = (qe * (SWA_HEAD_DIM ** -0.5 * LOG2E)).astype(q_ref.dtype)

    def kv_finish(kv):
        k = kv[:, :kw]
        ss = _dot((k * k).astype(BF16), ones4[:kw, :kw])
        kn = k * lax.rsqrt(ss * inv_d + NORM_EPS) * gk_ref[...]
        k_ref[...] = _rope_block(kn, cos_t, sin_t, upper)
        v_ref[...] = kv[:, kw:]

    work = [(slice(j * MXU_TILE, (j + 1) * MXU_TILE), functools.partial(q_finish, j)) for j in range(nblk)]
    work += [(slice(qw, qw + 2 * kw), kv_finish)]
    work += [(slice(qw + 2 * kw, None), gate_finish)]
    pending = _dot(xb, w_ref[:, work[0][0]])
    for n, (_, finish) in enumerate(work):
        current = pending
        if n + 1 < len(work):
            pending = _dot(xb, w_ref[:, work[n + 1][0]])
        finish(current)


def _mid(x, y, ga, ob, pe, po, cos_t, sin_t, act_dtype):
    m, dm = x.shape
    tm = _row_tile(m, 2 * ROW_TILE) if m > ROW_TILE else m
    row = lambda n: pl.BlockSpec((tm, n), lambda i: (i, 0))
    slab = pl.BlockSpec((y.shape[0], tm, LANES), lambda i: (0, i, 0))
    assert cos_t.shape[0] % tm == 0
    nper = cos_t.shape[0] // tm
    tab = pl.BlockSpec((tm, LANES), lambda i: (i % nper, 0))
    qw = SWA_HEADS * SWA_HEAD_DIM
    kw = SWA_KV_HEADS * SWA_HEAD_DIM
    weights = [pe['wglu'], pe['bglu'], pe['wo'], po['norm_g'], po['w_in'], po['gq'], po['gk']]
    resident = lambda w: pl.BlockSpec(w.shape, lambda i: (0,) * w.ndim, pipeline_mode=pl.Buffered(1))
    return pl.pallas_call(
        _mid_kernel,
        grid=(m // tm,),
        in_specs=[row(dm), slab, row(S5_WIDTH), row(GLA_V_WIDTH)] + [resident(w) for w in weights] + [tab, tab],
        out_specs=[row(dm), row(qw), row(kw), row(kw), row(qw)],
        out_shape=[jax.ShapeDtypeStruct((m, dm), F32),
                   jax.ShapeDtypeStruct((m, qw), act_dtype), jax.ShapeDtypeStruct((m, kw), F32),
                   jax.ShapeDtypeStruct((m, kw), F32), jax.ShapeDtypeStruct((m, qw), act_dtype)],
        compiler_params=_cparams(("parallel",)),
        name="mid",
    )(x, y, ga, ob, *weights, cos_t, sin_t)


def _rope_tables(pos):
    half = SWA_HEAD_DIM // 2
    inv_freq = ROPE_THETA ** (-np.arange(half, dtype=np.float64) / half)
    ang = np.asarray(pos, np.float64)[:, None] * inv_freq[None, :]
    cos, sin = np.cos(ang), np.sin(ang)
    cos_t = np.concatenate([cos, cos, cos, cos], axis=1).astype(np.float32)
    sin_t = np.concatenate([-sin, sin, -sin, sin], axis=1).astype(np.float32)
    return jnp.asarray(cos_t), jnp.asarray(sin_t)


def _attn_prompt_kernel(sink_ref, q_ref, kc_ref, kp_ref, vc_ref, vp_ref, gate_ref, h_ref, wout_ref,
                        y_ref, p_ref, o_ref, *, nqb):
    i = pl.program_id(1)
    w = SWA_WINDOW
    hd = SWA_HEAD_DIM
    npair = SWA_GROUP // 2
    lane2 = lax.broadcasted_iota(jnp.int32, (2 * w, LANES), 1)
    low = lane2 < hd
    rr = lax.broadcasted_iota(jnp.int32, (w, w), 0)
    cc = lax.broadcasted_iota(jnp.int32, (w, w), 1)
    tri = cc <= rr
    low_w = cc < hd
    r4 = lax.broadcasted_iota(jnp.int32, (4 * w, LANES), 0)
    c4 = lax.broadcasted_iota(jnp.int32, (4 * w, LANES), 1)
    den_cols = ((r4 < 2 * w) == (c4 < hd)).astype(BF16)
    units = [(jb, kv) for jb in range(nqb) for kv in range(SWA_KV_HEADS)]

    def scores(jb, kv):
        rows = slice(jb * w, (jb + 1) * w)
        if jb == 0:
            k_prev, v_prev = kp_ref[...], vp_ref[...]
        else:
            prev_rows = slice((jb - 1) * w, jb * w)
            k_prev, v_prev = kc_ref[prev_rows, :], vc_ref[prev_rows, :]
        kcat = jnp.concatenate([k_prev, kc_ref[rows, :]], axis=0)
        vcat = jnp.concatenate([v_prev, vc_ref[rows, :]], axis=0)
        own = low if kv == 0 else jnp.logical_not(low)
        k_own = jnp.where(own, kcat, 0.0)
        v_own = jnp.where(own, vcat, 0.0)
        k_oth = pltpu.roll(k_own, hd, 1)
        v_oth = pltpu.roll(v_own, hd, 1)
        k_lo, k_hi = (k_own, k_oth) if kv == 0 else (k_oth, k_own)
        v_lo, v_hi = (v_own, v_oth) if kv == 0 else (v_oth, v_own)
        k_rhs = jnp.concatenate([k_lo, k_hi], axis=0).astype(BF16)
        v_rhs = jnp.concatenate([jnp.concatenate([v_lo, v_hi], axis=0).astype(BF16), den_cols], axis=1)
        qs = jnp.concatenate([q_ref[rows, (kv * npair + pr) * LANES:(kv * npair + pr + 1) * LANES]
                              for pr in range(npair)], axis=0)
        return _dot_nt(qs, k_rhs), v_rhs

    def softmax_pv(kv, first, buf, s_all, v_rhs):
        sink_terms = []
        for pr in range(npair):
            prow = slice(pr * w, (pr + 1) * w)
            pair_terms = []
            for e in range(2):
                s_prev = s_all[prow, (2 * e) * w:(2 * e + 1) * w]
                s_cur = s_all[prow, (2 * e + 1) * w:(2 * e + 2) * w]
                if first:
                    s_prev = jnp.where(i > 0, s_prev, -jnp.inf)
                sc = jnp.where(tri, s_cur, s_prev)
                sink = sink_ref[2 * (kv * npair + pr) + e] * LOG2E
                mx = jnp.maximum(jnp.max(sc, axis=-1, keepdims=True), sink)
                pe = jnp.exp2(sc - mx)
                p_ref[buf, prow, (2 * e) * w:(2 * e + 1) * w] = jnp.where(tri, 0.0, pe).astype(BF16)
                p_ref[buf, prow, (2 * e + 1) * w:(2 * e + 2) * w] = jnp.where(tri, pe, 0.0).astype(BF16)
                pair_terms.append(jnp.exp2(sink - mx))
            sink_terms.append(pair_terms)
        return _dot(p_ref[buf], v_rhs), sink_terms

    def normalise(jb, kv, o_ext, sink_terms):
        rows = slice(jb * w, (jb + 1) * w)
        for pr in range(npair):
            prow = slice(pr * w, (pr + 1) * w)
            cols = slice((kv * npair + pr) * LANES, (kv * npair + pr + 1) * LANES)
            st = jnp.where(low_w, sink_terms[pr][0], sink_terms[pr][1])
            o = o_ext[prow, :LANES] / (o_ext[prow, LANES:] + st)
            o_ref[rows, cols] = o.astype(o_ref.dtype)

    def project(rows):
        og = o_ref[rows, :].astype(F32) * gate_ref[rows, :].astype(F32)
        y_ref[rows, :] = h_ref[rows, :] + _dot(og.astype(BF16), wout_ref[...])

    group = 2
    pending = scores(*units[0])
    unfinished = None
    for n, (jb, kv) in enumerate(units):
        current = pending
        if n + 1 < len(units):
            pending = scores(*units[n + 1])
        result = softmax_pv(kv, jb == 0, n % 2, *current)
        if unfinished is not None:
            normalise(*unfinished)
            done_jb, done_kv = unfinished[:2]
            if done_kv == SWA_KV_HEADS - 1 and (done_jb + 1) % group == 0:
                project(slice((done_jb + 1 - group) * w, (done_jb + 1) * w))
        unfinished = (jb, kv) + result
    normalise(*unfinished)
    project(slice((nqb - group) * w, nqb * w))


def _attn_prompt(sinks, q, k, v, gate, h, w_out, nseq, seqlen):
    w = SWA_WINDOW
    nqb = 8
    tm = nqb * w
    assert seqlen % tm == 0
    nblk = seqlen // tm
    qw = q.shape[1]
    dm = h.shape[1]
    row = lambda n: pl.BlockSpec((tm, n), lambda b, i: (b * nblk + i, 0))
    prev = lambda n: pl.BlockSpec((w, n), lambda b, i: (jnp.maximum((b * nblk + i) * nqb - 1, 0), 0))
    smem = pl.BlockSpec(memory_space=pltpu.SMEM)
    return pl.pallas_call(
        functools.partial(_attn_prompt_kernel, nqb=nqb),
        grid=(nseq, nblk),
        in_specs=[smem, row(qw), row(LANES), prev(LANES), row(LANES), prev(LANES), row(qw), row(dm),
                  _full(w_out.shape)],
        out_specs=row(dm),
        out_shape=jax.ShapeDtypeStruct(h.shape, F32),
        scratch_shapes=[pltpu.VMEM((2, 4 * w, 4 * w), BF16), pltpu.VMEM((tm, qw), BF16)],
        compiler_params=_cparams(("parallel", "arbitrary")),
        name="attn_prompt",
    )(sinks, q, k, k, v, v, gate, h, w_out)


def _attn_sample_kernel(sink_ref, q_ref, kn_ref, vn_ref, kc_ref, vc_ref, o_ref, ko_ref, vo_ref, q2_ref, *, seqlen):
    nq = SWA_HEADS * seqlen
    hd = SWA_HEAD_DIM
    heads = [(kv, g) for kv in range(SWA_KV_HEADS) for g in range(SWA_GROUP)]

    def stacked(kv, g):
        r0 = (kv * SWA_GROUP + g) * seqlen
        return slice(r0, r0 + seqlen), slice(kv * hd, (kv + 1) * hd)

    q2_ref[...] = jnp.zeros(q2_ref.shape, q2_ref.dtype)
    for kv, g in heads:
        rows, lanes = stacked(kv, g)
        h = kv * SWA_GROUP + g
        q2_ref[:, rows, lanes] = q_ref[:, :, h * hd:(h + 1) * hd]
    ncache = kc_ref.shape[1]
    t_row = lax.broadcasted_iota(jnp.int32, (nq, ncache), 0) % seqlen
    c_col = lax.broadcasted_iota(jnp.int32, (nq, ncache), 1)
    cache_ok = c_col > t_row - (SWA_WINDOW - ncache)
    nnew = BF16_ROWS
    t_row_n = lax.broadcasted_iota(jnp.int32, (nq, nnew), 0) % seqlen
    n_col = lax.broadcasted_iota(jnp.int32, (nq, nnew), 1)
    new_ok = n_col <= t_row_n
    sink = (sink_ref[...] * LOG2E)[None, :, 0:1]
    bqk = lambda a, b: lax.dot_general(a, b, (((2,), (2,)), ((0,), (0,))), preferred_element_type=F32)
    bpv = lambda a, b: lax.dot_general(a, b, (((2,), (1,)), ((0,), (0,))), preferred_element_type=F32)
    q = q2_ref[...].astype(BF16)
    kc = kc_ref[...]
    vc = vc_ref[...]
    pad = jnp.zeros((kn_ref.shape[0], nnew - seqlen, kn_ref.shape[2]), F32)
    kn = jnp.concatenate([kn_ref[...], pad], axis=1)
    vn = jnp.concatenate([vn_ref[...], pad], axis=1)
    sc = jnp.where(cache_ok[None], bqk(q, kc.astype(BF16)), -jnp.inf)
    sn = jnp.where(new_ok[None], bqk(q, kn.astype(BF16)), -jnp.inf)
    mx = jnp.maximum(jnp.maximum(jnp.max(sc, axis=-1, keepdims=True), jnp.max(sn, axis=-1, keepdims=True)), sink)
    pc = jnp.exp2(sc - mx)
    pn = jnp.exp2(sn - mx)
    den = jnp.sum(pc, axis=-1, keepdims=True) + jnp.sum(pn, axis=-1, keepdims=True) + jnp.exp2(sink - mx)
    inv = 1.0 / den
    o2 = bpv((pc * inv).astype(BF16), vc.astype(BF16)) + bpv((pn * inv).astype(BF16), vn.astype(BF16))
    for kv, g in heads:
        rows, lanes = stacked(kv, g)
        h = kv * SWA_GROUP + g
        o_ref[:, :, h * hd:(h + 1) * hd] = o2[:, rows, lanes]
    keep = ncache - seqlen
    ko_ref[:, 0:keep, :] = kc[:, seqlen:ncache, :]
    ko_ref[:, keep:ncache, :] = kn[:, 0:seqlen, :]
    vo_ref[:, 0:keep, :] = vc[:, seqlen:ncache, :]
    vo_ref[:, keep:ncache, :] = vn[:, 0:seqlen, :]


def _attn_sample(sink_rows, q, kn, vn, kc, vc):
    n, seqlen, _ = q.shape
    nseq = 16
    assert n % nseq == 0
    blk = lambda a: pl.BlockSpec((nseq,) + a.shape[1:], lambda i: (i, 0, 0))
    return pl.pallas_call(
        functools.partial(_attn_sample_kernel, seqlen=seqlen),
        grid=(n // nseq,),
        in_specs=[_full(sink_rows.shape), blk(q), blk(kn), blk(vn), blk(kc), blk(vc)],
        out_specs=[blk(q), blk(kc), blk(vc)],
        out_shape=[jax.ShapeDtypeStruct(q.shape, F32), jax.ShapeDtypeStruct(kc.shape, F32),
                   jax.ShapeDtypeStruct(vc.shape, F32)],
        scratch_shapes=[pltpu.VMEM((nseq, SWA_HEADS * seqlen, SWA_KV_HEADS * SWA_HEAD_DIM), F32)],
        compiler_params=_cparams(("parallel",)),
        name="attn_sample",
    )(sink_rows, q, kn, vn, kc, vc)


def _odd_out_kernel(h_ref, o_ref, gate_ref, w_ref, y_ref):
    og = o_ref[...].astype(F32) * gate_ref[...].astype(F32)
    y_ref[...] = h_ref[...] + _dot(og.astype(BF16), w_ref[...])


def _odd_out(h, o, gate, w):
    m, dm = h.shape
    tm = _row_tile(m)
    row = lambda n: pl.BlockSpec((tm, n), lambda i: (i, 0))
    return pl.pallas_call(
        _odd_out_kernel,
        grid=(m // tm,),
        in_specs=[row(dm), row(o.shape[1]), row(gate.shape[1]), _full(w.shape)],
        out_specs=row(dm),
        out_shape=jax.ShapeDtypeStruct((m, dm), F32),
        compiler_params=_cparams(("parallel",)),
        name="odd_out",
    )(h, o, gate, w)


def _trunk(x, s5_x0, gla_s0, pe, po, pos, nseq, seqlen, act_dtype):
    u2, ga, q, k, v, la, gb = _even_in(x, pe['norm_g'], pe['w_in'], pe['wgate'], pe['bgate'], act_dtype,
                                       _s5_chunk(seqlen))
    y4, s5_fin = _s5_branch(u2, s5_x0, pe['s5_ops'], pe['d'], nseq, seqlen, act_dtype)
    ob, gla_fin = _gla(q, k, v, la, gb, gla_s0, pe['gla_norm_g'], nseq, seqlen, act_dtype)
    cos_t, sin_t = _rope_tables(pos)
    h, q1, k1, v1, gate = _mid(x, y4, ga, ob, pe, po, cos_t, sin_t, act_dtype)
    return h, q1, k1, v1, gate, s5_fin, gla_fin


def kernel(x_prompt, x_sample, state_s5_re, state_s5_im, state_gla, cache_swa_k, cache_swa_v,
           even_norm_g, even_w_in, s5_lambda_re, s5_lambda_im, s5_log_dt, s5_b_re, s5_b_im,
           s5_c_re, s5_c_im, s5_d, s5_w_glu, s5_b_glu, gla_w_gate, gla_b_gate, gla_norm_g,
           even_w_out, odd_norm_g, odd_w_in, swa_q_norm_g, swa_k_norm_g, swa_sinks, odd_w_out):
    nb, seq, dm = x_prompt.shape
    ns, dseq, _ = x_sample.shape
    ng = s5_lambda_re.shape[1]
    xp = x_prompt.reshape(nb * seq, dm)
    xs = x_sample.reshape(ns * dseq, dm)

    i = 0
    assert ng * S5_GROUP == S5_WIDTH
    pad_rank = LANES - GLA_GATE_RANK
    pe = {
        'norm_g': even_norm_g[i][None, :],
        'w_in': even_w_in[i].astype(BF16),
        'wgate': jnp.pad(gla_w_gate[i], ((0, pad_rank), (0, 0))).astype(BF16),
        'bgate': gla_b_gate[i][None, :],
        's5_ops': _s5_params(s5_lambda_re[i], s5_lambda_im[i], s5_log_dt[i], s5_b_re[i], s5_b_im[i],
                             s5_c_re[i], s5_c_im[i]),
        'gla_norm_g': gla_norm_g[i][None, :],
        'd': s5_d[i],
        'wglu': s5_w_glu[i].astype(BF16),
        'bglu': s5_b_glu[i][None, :],
        'wo': even_w_out[i].astype(BF16),
    }
    po = {
        'norm_g': odd_norm_g[i][None, :],
        'w_in': odd_w_in[i].astype(BF16),
        'gq': jnp.tile(swa_q_norm_g[i], MXU_TILE // SWA_HEAD_DIM)[None, :],
        'gk': jnp.tile(swa_k_norm_g[i], LANES // SWA_HEAD_DIM)[None, :],
    }
    w_out = odd_w_out[i].astype(BF16)
    sinks = swa_sinks[i]
    kvw = SWA_KV_HEADS * SWA_HEAD_DIM

    s5_zero = jnp.zeros((nb, ng, S5_STATE), F32)
    gla_zero = jnp.zeros((nb, GLA_HEADS, GLA_HEAD_K, GLA_HEAD_V), F32)
    hp, q, k, v, gate, (s5r_p, s5i_p), gla_p = _trunk(xp, (s5_zero, s5_zero), gla_zero, pe, po, np.arange(seq),
                                                      nb, seq, BF16)
    y_prompt = _attn_prompt(sinks, q, k, v, gate, hp, w_out, nb, seq).reshape(nb, seq, dm)
    cache_len = min(SWA_WINDOW, seq)
    tail = lambda a: (a.reshape(nb, seq, kvw)[:, seq - cache_len:]
                      .reshape(1, nb, cache_len, SWA_KV_HEADS, SWA_HEAD_DIM))
    swk_p, swv_p = tail(k), tail(v)

    pos_s = np.tile(PAST_LEN + np.arange(dseq), ns)
    hs, q, k, v, gate, (s5r_s, s5i_s), gla_s = _trunk(xs, (state_s5_re[i], state_s5_im[i]), state_gla[i], pe, po,
                                                      pos_s, ns, dseq, F32)
    ncache = cache_swa_k.shape[2]
    sink_rows = jnp.broadcast_to(jnp.repeat(sinks, dseq)[:, None], (SWA_HEADS * dseq, LANES))
    kn = k.reshape(ns, dseq, kvw)
    vn = v.reshape(ns, dseq, kvw)
    kc = cache_swa_k[i].reshape(ns, ncache, kvw)
    vc = cache_swa_v[i].reshape(ns, ncache, kvw)
    o, kc_new, vc_new = _attn_sample(sink_rows, q.reshape(ns, dseq, -1), kn, vn, kc, vc)
    y_sample = _odd_out(hs, o.reshape(ns * dseq, -1), gate, w_out).reshape(ns, dseq, dm)
    swk_s = kc_new.reshape(1, ns, ncache, SWA_KV_HEADS, SWA_HEAD_DIM)
    swv_s = vc_new.reshape(1, ns, ncache, SWA_KV_HEADS, SWA_HEAD_DIM)

    return (y_prompt, y_sample,
            s5r_p, s5i_p, gla_p[None], swk_p, swv_p,
            s5r_s, s5i_s, gla_s[None], swk_s, swv_s)
```

```python
import functools
import math

import jax
import jax.numpy as jnp
import numpy as np
from jax import lax
from jax.experimental import pallas as pl
from jax.experimental.pallas import tpu as pltpu

F32 = jnp.float32
BF16 = jnp.bfloat16

PAST_LEN = 8192
NORM_EPS = 1e-6
S5_GROUP = 16
S5_STATE = 64
S5_CHUNK = 16
GLA_HEADS = 4
GLA_HEAD_K = 64
GLA_HEAD_V = 128
GLA_GATE_RANK = 16
GLA_GATE_TAU = 16.0
GLA_CHUNK = 64
S5_WIDTH = 512
GLA_K_WIDTH = GLA_HEADS * GLA_HEAD_K
GLA_V_WIDTH = GLA_HEADS * GLA_HEAD_V
SWA_HEADS = 16
SWA_KV_HEADS = 2
SWA_GROUP = SWA_HEADS // SWA_KV_HEADS
SWA_HEAD_DIM = 64
SWA_WINDOW = 128
ROPE_THETA = 10000.0
LOG2E = math.log2(math.e)
LANES = 128
BF16_ROWS = 16
MXU_TILE = 256
OCT = LANES // S5_GROUP
ROW_TILE = 512
VMEM_LIMIT = 48 * 1024 * 1024
C_U = 0
C_GATE_A = C_U + S5_WIDTH
C_Q = C_GATE_A + S5_WIDTH
C_K = C_Q + GLA_K_WIDTH
C_V = C_K + GLA_K_WIDTH
C_CODE = C_V + GLA_V_WIDTH
C_GATE_B = C_CODE + GLA_GATE_RANK
C_END = C_GATE_B + GLA_V_WIDTH


def _cparams(sem):
    return pltpu.CompilerParams(dimension_semantics=sem, vmem_limit_bytes=VMEM_LIMIT)


def _full(shape):
    n = len(shape)
    return pl.BlockSpec(shape, lambda *_: (0,) * n)


def _dot(a, b):
    return jnp.dot(a, b, preferred_element_type=F32)


def _dot_nt(a, b):
    return lax.dot_general(a, b, (((1,), (1,)), ((), ())), preferred_element_type=F32)


def _dot_tn(a, b):
    return lax.dot_general(a, b, (((0,), (0,)), ((), ())), preferred_element_type=F32)


def _split_bf16(x):
    hi = x.astype(BF16)
    lo = (x - hi.astype(F32)).astype(BF16)
    return hi, lo


def _rms_rows(x, g):
    return x * lax.rsqrt(jnp.mean(x * x, axis=-1, keepdims=True) + NORM_EPS) * g


def _sigmoid(x):
    return 1.0 / (1.0 + jnp.exp(-x))


def _silu(x):
    return x * _sigmoid(x)


def _row_tile(m, tile=ROW_TILE):
    return tile if m % tile == 0 else m


def _even_in_kernel(x_ref, g_ref, w_ref, wgate_ref, bgate_ref,
                    u2_ref, ga_ref, q_ref, k_ref, v_ref, la_ref, gb_ref, uscr_ref, wgb_ref, *, t):
    @pl.when(pl.program_id(0) == 0)
    def _():
        wgb_ref[...] = w_ref[:, C_GATE_B:C_END]

    xb = _rms_rows(x_ref[...], g_ref[...]).astype(BF16)

    def proj(lo, hi):
        return _dot(xb, w_ref[:, lo:hi])

    u = proj(C_U, C_GATE_A)
    nrow = u.shape[0] // t
    for o in range(u2_ref.shape[0]):
        uscr_ref[o] = u[:, o * LANES:(o + 1) * LANES]
        for tt in range(t):
            piece = uscr_ref[o, pl.ds(tt, nrow, stride=t), :]
            u2_ref[o, :, tt * LANES:(tt + 1) * LANES] = piece.astype(u2_ref.dtype)
    ga_ref[...] = _silu(proj(C_GATE_A, C_Q)).astype(ga_ref.dtype)
    q_ref[...] = (proj(C_Q, C_K) * (GLA_HEAD_K ** -0.5)).astype(q_ref.dtype)
    k_ref[...] = proj(C_K, C_V).astype(k_ref.dtype)
    v_ref[...] = proj(C_V, C_CODE).astype(v_ref.dtype)
    gb_ref[...] = _silu(_dot(xb, wgb_ref[...])).astype(gb_ref.dtype)
    a_low = proj(C_CODE, C_CODE + LANES)
    logit = _dot(a_low.astype(BF16), wgate_ref[...]) + bgate_ref[...]
    log_sig = jnp.minimum(logit, 0.0) - jnp.log1p(jnp.exp(-jnp.abs(logit)))
    la_ref[...] = log_sig * (1.0 / GLA_GATE_TAU)


def _even_in(x, g, w, wgate, bgate, act_dtype, t):
    m, d = x.shape
    tm = _row_tile(m, 2 * ROW_TILE) if m > ROW_TILE else m
    row = lambda n: pl.BlockSpec((tm, n), lambda i: (i, 0))
    assert w.shape[1] == C_END
    no = S5_WIDTH // LANES
    chunk = pl.BlockSpec((no, tm // t, t * LANES), lambda i: (0, i, 0))
    outs = [(S5_WIDTH, act_dtype), (GLA_K_WIDTH, act_dtype), (GLA_K_WIDTH, act_dtype), (GLA_V_WIDTH, act_dtype),
            (GLA_K_WIDTH, F32), (GLA_V_WIDTH, act_dtype)]
    return pl.pallas_call(
        functools.partial(_even_in_kernel, t=t),
        grid=(m // tm,),
        in_specs=[row(d), _full(g.shape), _full(w.shape), _full(wgate.shape), _full(bgate.shape)],
        out_specs=[chunk] + [row(n) for n, _ in outs],
        out_shape=[jax.ShapeDtypeStruct((no, m // t, t * LANES), act_dtype)]
        + [jax.ShapeDtypeStruct((m, n), dt) for n, dt in outs],
        scratch_shapes=[pltpu.VMEM((no, tm, LANES), F32), pltpu.VMEM((d, GLA_V_WIDTH), BF16)],
        compiler_params=_cparams(("arbitrary",)),
        name="even_in",
    )(x, g, w, wgate, bgate)


def _group_mask(shape, row_span, col_span):
    rg = (lax.broadcasted_iota(jnp.int32, shape, 0) // row_span) % OCT
    cg = (lax.broadcasted_iota(jnp.int32, shape, 1) // col_span) % OCT
    return rg == cg


def _s5_state_kernel(u_ref, bre_ref, bim_ref, x0r_ref, x0i_ref, are_ref, aim_ref, xs_ref, xfr_ref, xfi_ref, loc_ref,
                     *, nseq, nchunks):
    hw = OCT * S5_STATE
    kk = u_ref.shape[2]
    bp = jnp.concatenate([bre_ref[0]] * OCT + [bim_ref[0]] * OCT, axis=1)
    bp = jnp.where(_group_mask((kk, 2 * hw), S5_GROUP, S5_STATE), bp, 0.0).astype(BF16)
    loc_ref[...] = _dot(u_ref[0], bp)
    a_re = are_ref[0]
    a_im = aim_ref[0]
    if nchunks == 1:
        xr, xi = x0r_ref[...], x0i_ref[...]
        loc = loc_ref[...]
        xfr_ref[...] = a_re * xr - a_im * xi + loc[:, :hw]
        xfi_ref[...] = a_re * xi + a_im * xr + loc[:, hw:]
        xs_ref[0] = jnp.concatenate([xr, xi], axis=1).astype(xs_ref.dtype)
    else:
        def body(j, carry):
            new = []
            for b in range(nseq):
                xr, xi = carry[b]
                row = pl.ds(b * nchunks + j, 1)
                lr = loc_ref[row, :hw]
                li = loc_ref[row, hw:]
                loc_ref[row, :hw] = xr
                loc_ref[row, hw:] = xi
                new.append((a_re * xr - a_im * xi + lr, a_re * xi + a_im * xr + li))
            return tuple(new)

        init = tuple((x0r_ref[b:b + 1, :], x0i_ref[b:b + 1, :]) for b in range(nseq))
        fin = lax.fori_loop(0, nchunks, body, init, unroll=4)
        for b in range(nseq):
            xfr_ref[b:b + 1, :] = fin[b][0]
            xfi_ref[b:b + 1, :] = fin[b][1]
        xs_ref[0] = loc_ref[...].astype(xs_ref.dtype)


def _s5_state(u2, bre, bim, x0r, x0i, are, aim, nseq, nchunks):
    no, r, kk = u2.shape
    hw = OCT * S5_STATE
    blk = lambda a: pl.BlockSpec((1,) + a.shape[1:], lambda o: (o,) + (0,) * (a.ndim - 1))
    octet = pl.BlockSpec((nseq, hw), lambda o: (0, o))
    return pl.pallas_call(
        functools.partial(_s5_state_kernel, nseq=nseq, nchunks=nchunks),
        grid=(no,),
        in_specs=[blk(u2), blk(bre), blk(bim), octet, octet, blk(are), blk(aim)],
        out_specs=[pl.BlockSpec((1, r, 2 * hw), lambda o: (o, 0, 0)), octet, octet],
        out_shape=[jax.ShapeDtypeStruct((no, r, 2 * hw), BF16), jax.ShapeDtypeStruct(x0r.shape, F32),
                   jax.ShapeDtypeStruct(x0i.shape, F32)],
        scratch_shapes=[pltpu.VMEM((r, 2 * hw), F32)],
        compiler_params=_cparams(("parallel",)),
        name="s5_state",
    )(u2, bre, bim, x0r, x0i, are, aim)


def _dot_nt_f32(a, b):
    a_hi, a_lo = _split_bf16(a)
    b_hi, b_lo = _split_bf16(b)
    return _dot_nt(a_hi, b_hi) + _dot_nt(a_hi, b_lo) + _dot_nt(a_lo, b_hi)


def _s5_out_kernel(u_ref, xs_ref, bre_ref, bim_ref, cre_ref, cim_ref, zre_ref, zim_ref, d_ref, y_ref, yscr_ref):
    r, kk = u_ref.shape[1], u_ref.shape[2]
    t = kk // LANES
    sw = xs_ref.shape[2]
    ntile = kk // MXU_TILE
    taps = _dot_nt_f32(bre_ref[0], zre_ref[0]) - _dot_nt_f32(bim_ref[0], zim_ref[0])
    tmask = _group_mask((LANES, LANES), S5_GROUP, S5_GROUP)

    def tap(lag):
        if lag < 0:
            return jnp.zeros((LANES, LANES), F32)
        s = t - 1 - lag
        return jnp.where(tmask, taps[s * LANES:(s + 1) * LANES], 0.0)

    wts = [jnp.concatenate([jnp.concatenate([tap(2 * d), tap(2 * d + 1)], axis=1),
                            jnp.concatenate([tap(2 * d - 1), tap(2 * d)], axis=1)], axis=0).astype(BF16)
           for d in range(ntile)]
    cpt = jnp.concatenate([cre_ref[0]] * OCT + [cim_ref[0]] * OCT, axis=1)
    cpt = jnp.where(_group_mask((kk, sw), S5_GROUP, S5_STATE), cpt, 0.0).astype(BF16)
    xs = xs_ref[0]
    skip = jnp.concatenate([d_ref[0]] * (MXU_TILE // LANES), axis=1)
    for n in range(ntile):
        cols = slice(n * MXU_TILE, (n + 1) * MXU_TILE)
        acc = _dot_nt(xs, cpt[cols]) + skip * u_ref[0, :, cols].astype(F32)
        for k in range(n + 1):
            acc = acc + _dot(u_ref[0, :, k * MXU_TILE:(k + 1) * MXU_TILE], wts[n - k])
        for e in range(MXU_TILE // LANES):
            yscr_ref[pl.ds(2 * n + e, r, stride=t), :] = acc[:, e * LANES:(e + 1) * LANES]
    y_ref[0] = yscr_ref[...].astype(y_ref.dtype)


def _s5_out(u2, xs, bre, bim, cre, cim, zre, zim, d, out_dtype):
    no, r, kk = u2.shape
    m = r * (kk // LANES)
    blk = lambda a: pl.BlockSpec((1,) + a.shape[1:], lambda o: (o,) + (0,) * (a.ndim - 1))
    ops = (u2, xs, bre, bim, cre, cim, zre, zim, d)
    return pl.pallas_call(
        _s5_out_kernel,
        grid=(no,),
        in_specs=[blk(a) for a in ops],
        out_specs=pl.BlockSpec((1, m, LANES), lambda o: (o, 0, 0)),
        out_shape=jax.ShapeDtypeStruct((no, m, LANES), out_dtype),
        scratch_shapes=[pltpu.VMEM((m, LANES), F32)],
        compiler_params=_cparams(("parallel",)),
        name="s5_out",
    )(*ops)


def _s5_params(lam_re, lam_im, log_dt, b_re, b_im, c_re, c_im):
    t = S5_CHUNK
    ng = lam_re.shape[0]
    no = ng // OCT
    dt = jnp.exp(log_dt)[:, None]
    a = lam_re * dt
    b = lam_im * dt
    n = jnp.arange(t + 1, dtype=F32)[None, :, None]
    mag = jnp.exp(n * a[:, None, :])
    pw_re = mag * jnp.cos(n * b[:, None, :])
    pw_im = mag * jnp.sin(n * b[:, None, :])
    em1_re = jnp.expm1(a) * jnp.cos(b) - 2.0 * jnp.sin(0.5 * b) ** 2
    em1_im = jnp.exp(a) * jnp.sin(b)
    den = lam_re * lam_re + lam_im * lam_im
    z_re = (em1_re * lam_re + em1_im * lam_im) / den
    z_im = (em1_im * lam_re - em1_re * lam_im) / den
    bb_re = z_re[..., None] * b_re - z_im[..., None] * b_im
    bb_im = z_re[..., None] * b_im + z_im[..., None] * b_re

    def per_token(pw):
        return pw.reshape(no, OCT, t, S5_STATE).transpose(0, 2, 1, 3)[:, :, :, None, :]

    def per_group(w):
        return w.reshape(no, 1, OCT, S5_GROUP, S5_STATE)

    rows = lambda w: w.reshape(no, t * LANES, S5_STATE)
    bt_re, bt_im = per_group(bb_re.transpose(0, 2, 1)), per_group(bb_im.transpose(0, 2, 1))
    r_re, r_im = per_token(pw_re[:, t - 1::-1]), per_token(pw_im[:, t - 1::-1])
    bre = rows(r_re * bt_re - r_im * bt_im)
    bim = rows(r_re * bt_im + r_im * bt_re)
    o_re, o_im = per_token(pw_re[:, 1:]), per_token(pw_im[:, 1:])
    cg_re, cg_im = per_group(c_re), per_group(c_im)
    cre = rows(cg_re * o_re - cg_im * o_im)
    cim = rows(-(cg_re * o_im + cg_im * o_re))
    return bre, bim, cre, cim, c_re.reshape(no, LANES, S5_STATE), c_im.reshape(no, LANES, S5_STATE), pw_re, pw_im


def _s5_chunk(seqlen):
    return math.gcd(seqlen, S5_CHUNK)


def _s5_branch(u2, x0, ops, d, nseq, seqlen, out_dtype):
    bre, bim, cre, cim, zre, zim, pw_re, pw_im = ops
    no = u2.shape[0]
    t = _s5_chunk(seqlen)
    kk = t * LANES
    nchunks = seqlen // t
    are = pw_re[:, t].reshape(no, 1, OCT * S5_STATE)
    aim = pw_im[:, t].reshape(no, 1, OCT * S5_STATE)
    tail = S5_CHUNK * LANES - kk
    bre, bim = bre[:, tail:], bim[:, tail:]
    flat = lambda a: a.reshape(nseq, -1)
    xs, xf_re, xf_im = _s5_state(u2, bre, bim, flat(x0[0]), flat(x0[1]), are, aim, nseq, nchunks)
    y = _s5_out(u2, xs, bre, bim, cre[:, :kk], cim[:, :kk], zre, zim, d.reshape(no, 1, LANES), out_dtype)
    return y, (xf_re.reshape((1,) + x0[0].shape), xf_im.reshape((1,) + x0[1].shape))


def _gla_kernel(q_ref, k_ref, v_ref, la_ref, gb_ref, s0_ref, ng_ref, o_ref, sf_ref, st_ref,
                *, nseq, nchunks, c):
    i = pl.program_id(1)

    @pl.when(i == 0)
    def _():
        st_ref[...] = s0_ref[...]

    nh, hk, hv = GLA_HEADS, GLA_HEAD_K, GLA_HEAD_V
    nch = nseq * nchunks
    tm = nch * c
    iota = lambda shape, d: lax.broadcasted_iota(jnp.int32, shape, d)
    ng = ng_ref[...]

    tb = min(tm, MXU_TILE)
    rt, ct = iota((tb, tb), 0), iota((tb, tb), 1)
    tril = ((rt // c == ct // c) & (rt >= ct)).astype(BF16)
    parts = []
    for r0 in range(0, tm, tb):
        la_hi, la_lo = _split_bf16(la_ref[r0:r0 + tb, :])
        parts.append(_dot(tril, la_hi) + _dot(tril, la_lo))
    bcum = jnp.concatenate(parts, axis=0)
    e_hi, e_lo = _split_bf16(jnp.exp(jnp.concatenate([bcum[(ci + 1) * c - 1:(ci + 1) * c] for ci in range(nch)],
                                                     axis=0)))
    pick = (iota((nch, LANES), 0) == iota((nch, LANES), 1)).astype(BF16)
    dec_t = _dot_tn(e_hi, pick) + _dot_tn(e_lo, pick)
    q_all = q_ref[...].astype(F32)
    k_all = k_ref[...].astype(F32)
    q_dec_all = q_all * jnp.exp(bcum)
    k_dec_all = k_all * jnp.exp(-bcum)

    own_k = iota((nh * c, nh * hk), 0) // c == iota((nh * c, nh * hk), 1) // hk
    own_v = iota((nh * c, nh * hv), 0) // c == iota((nh * c, nh * hv), 1) // hv
    causal = iota((c, nh * c), 1) % c <= iota((c, nh * c), 0)
    zero_v = jnp.zeros((hk, hv), F32)

    def intra(ci):
        rows = slice(ci * c, (ci + 1) * c)
        q_dec = q_dec_all[rows].astype(BF16)
        k_dec = k_dec_all[rows]
        b_c = bcum[rows]
        k_tail = (k_all[rows] * jnp.exp(b_c[c - 1:c] - b_c)).astype(BF16)
        v = v_ref[rows, :].astype(F32)
        k_bd = jnp.where(own_k, jnp.concatenate([k_dec] * nh, axis=0), 0.0).astype(BF16)
        v_bd = jnp.where(own_v, jnp.concatenate([v] * nh, axis=0), 0.0).astype(BF16)
        att = jnp.where(causal, _dot_nt(q_dec, k_bd), 0.0)
        o_intra = _dot(att.astype(BF16), v_bd)
        vb = v.astype(BF16)
        kvs = []
        for h0 in range(0, nh, 2):
            kv2 = _dot_tn(k_tail[:, h0 * hk:(h0 + 2) * hk], vb[:, h0 * hv:(h0 + 2) * hv])
            kvs += [kv2[:hk, :hv], kv2[hk:, hv:]]
        return q_dec, o_intra, kvs

    def carry(ci, sts, q_dec, o_intra, kvs):
        st_bd = jnp.concatenate(
            [jnp.concatenate([zero_v] * h + [sts[h]] + [zero_v] * (nh - 1 - h), axis=1) for h in range(nh)],
            axis=0).astype(BF16)
        o = o_intra + _dot(q_dec, st_bd)
        new = [dec_t[h * hk:(h + 1) * hk, ci:ci + 1] * sts[h] + kvs[h] for h in range(nh)]
        return o, new

    def finish(ci, o):
        rows = slice(ci * c, (ci + 1) * c)
        for h in range(nh):
            vs = slice(h * hv, (h + 1) * hv)
            oh = _rms_rows(o[:, vs], ng) * gb_ref[rows, vs].astype(F32)
            o_ref[rows, vs] = oh.astype(o_ref.dtype)

    sts = None
    pending = intra(0)
    unfinished = None
    for ci in range(nch):
        s, first, last = ci // nchunks, ci % nchunks == 0, ci % nchunks == nchunks - 1
        current = pending
        if ci + 1 < nch:
            pending = intra(ci + 1)
        if first:
            sts = [st_ref[s, h] for h in range(nh)]
        o, sts = carry(ci, sts, *current)
        if last:
            for h in range(nh):
                st_ref[s, h] = sts[h]
        if unfinished is not None:
            finish(*unfinished)
        unfinished = (ci, o)
    finish(*unfinished)

    @pl.when(i == pl.num_programs(1) - 1)
    def _():
        sf_ref[...] = st_ref[...]


def _gla(q, k, v, la, gb, s0, ng, nseq_total, seqlen, out_dtype):
    c = math.gcd(seqlen, GLA_CHUNK)
    per_seq = seqlen // c
    if per_seq >= 8:
        nseq, nchunks = 1, next(n for n in (32, 16, 8, 1) if per_seq % n == 0)
    else:
        nseq, nchunks = 16, seqlen // c
    assert nseq_total % nseq == 0 and seqlen % (nchunks * c) == 0
    nblk = seqlen // (nchunks * c)
    tm = nseq * nchunks * c
    row = lambda n: pl.BlockSpec((tm, n), lambda b, i: (b * nblk + i, 0))
    st_spec = pl.BlockSpec((nseq, GLA_HEADS, GLA_HEAD_K, GLA_HEAD_V), lambda b, i: (b, 0, 0, 0))
    m = q.shape[0]
    return pl.pallas_call(
        functools.partial(_gla_kernel, nseq=nseq, nchunks=nchunks, c=c),
        grid=(nseq_total // nseq, nblk),
        in_specs=[row(GLA_K_WIDTH), row(GLA_K_WIDTH), row(GLA_V_WIDTH), row(GLA_K_WIDTH), row(GLA_V_WIDTH),
                  st_spec, _full(ng.shape)],
        out_specs=[row(GLA_V_WIDTH), st_spec],
        out_shape=[jax.ShapeDtypeStruct((m, GLA_V_WIDTH), out_dtype), jax.ShapeDtypeStruct(s0.shape, F32)],
        scratch_shapes=[pltpu.VMEM((nseq, GLA_HEADS, GLA_HEAD_K, GLA_HEAD_V), F32)],
        compiler_params=_cparams(("parallel", "arbitrary")),
        name="gla",
    )(q, k, v, la, gb, s0, ng)


def _gelu_tanh(x):
    return 0.5 * x * (1.0 + jnp.tanh(math.sqrt(2.0 / math.pi) * (x + 0.044715 * (x * x * x))))


def _head_ones(n):
    r = lax.broadcasted_iota(jnp.int32, (n, n), 0) // SWA_HEAD_DIM
    c = lax.broadcasted_iota(jnp.int32, (n, n), 1) // SWA_HEAD_DIM
    return (r == c).astype(BF16)


def _rope_block(x, cos_t, sin_t, upper):
    half = SWA_HEAD_DIM // 2
    swapped = jnp.where(upper, pltpu.roll(x, half, 1), pltpu.roll(x, LANES - half, 1))
    return x * cos_t + swapped * sin_t


def _mid_kernel(x_ref, y_ref, ga_ref, ob_ref, wglu_ref, bglu_ref, wo_ref,
                g_ref, w_ref, gq_ref, gk_ref, cos_ref, sin_ref, h_ref, q_ref, k_ref, v_ref, gate_ref):
    tm = x_ref.shape[0]
    lane = lax.broadcasted_iota(jnp.int32, (tm, LANES), 1)
    upper = (lane & (SWA_HEAD_DIM // 2)) != 0
    ones4 = _head_ones(MXU_TILE)
    inv_d = 1.0 / SWA_HEAD_DIM
    qw = SWA_HEADS * SWA_HEAD_DIM
    kw = SWA_KV_HEADS * SWA_HEAD_DIM
    nblk = qw // MXU_TILE
    no = y_ref.shape[0]

    na = ga_ref.shape[1]
    mix_b = _dot(ob_ref[...].astype(BF16), wo_ref[na:, :])
    z = _gelu_tanh(jnp.concatenate([y_ref[o].astype(F32) for o in range(no)], axis=1))
    z = z * _sigmoid(_dot(z.astype(BF16), wglu_ref[...]) + bglu_ref[...])
    out_a = z * ga_ref[...].astype(F32)
    h = x_ref[...] + (_dot(out_a.astype(BF16), wo_ref[:na, :]) + mix_b)
    h_ref[...] = h

    xb = _rms_rows(h, g_ref[...]).astype(BF16)
    cos_t = cos_ref[...]
    sin_t = sin_ref[...]

    def gate_finish(gate):
        gate_ref[...] = _silu(gate).astype(gate_ref.dtype)

    def q_finish(j, q):
        ss = _dot((q * q).astype(BF16), ones4)
        qn = q * lax.rsqrt(ss * inv_d + NORM_EPS) * gq_ref[...]
        for e in range(MXU_TILE // LANES):
            cols = slice(j * MXU_TILE + e * LANES, j * MXU_TILE + (e + 1) * LANES)
            qe = _rope_block(qn[:, e * LANES:(e + 1) * LANES], cos_t, sin_t, upper)
            q_ref[:, cols] = (qe * (SWA_HEAD_DIM ** -0.5 * LOG2E)).astype(q_ref.dtype)

    def kv_finish(kv):
        k = kv[:, :kw]
        ss = _dot((k * k).astype(BF16), ones4[:kw, :kw])
        kn = k * lax.rsqrt(ss * inv_d + NORM_EPS) * gk_ref[...]
        k_ref[...] = _rope_block(kn, cos_t, sin_t, upper)
        v_ref[...] = kv[:, kw:]

    work = [(slice(j * MXU_TILE, (j + 1) * MXU_TILE), functools.partial(q_finish, j)) for j in range(nblk)]
    work += [(slice(qw, qw + 2 * kw), kv_finish)]
    work += [(slice(qw + 2 * kw, None), gate_finish)]
    pending = _dot(xb, w_ref[:, work[0][0]])
    for n, (_, finish) in enumerate(work):
        current = pending
        if n + 1 < len(work):
            pending = _dot(xb, w_ref[:, work[n + 1][0]])
        finish(current)


def _mid(x, y, ga, ob, pe, po, cos_t, sin_t, act_dtype):
    m, dm = x.shape
    tm = _row_tile(m, 2 * ROW_TILE) if m > ROW_TILE else m
    row = lambda n: pl.BlockSpec((tm, n), lambda i: (i, 0))
    slab = pl.BlockSpec((y.shape[0], tm, LANES), lambda i: (0, i, 0))
    assert cos_t.shape[0] % tm == 0
    nper = cos_t.shape[0] // tm
    tab = pl.BlockSpec((tm, LANES), lambda i: (i % nper, 0))
    qw = SWA_HEADS * SWA_HEAD_DIM
    kw = SWA_KV_HEADS * SWA_HEAD_DIM
    weights = [pe['wglu'], pe['bglu'], pe['wo'], po['norm_g'], po['w_in'], po['gq'], po['gk']]
    resident = lambda w: pl.BlockSpec(w.shape, lambda i: (0,) * w.ndim, pipeline_mode=pl.Buffered(1))
    return pl.pallas_call(
        _mid_kernel,
        grid=(m // tm,),
        in_specs=[row(dm), slab, row(S5_WIDTH), row(GLA_V_WIDTH)] + [resident(w) for w in weights] + [tab, tab],
        out_specs=[row(dm), row(qw), row(kw), row(kw), row(qw)],
        out_shape=[jax.ShapeDtypeStruct((m, dm), F32),
                   jax.ShapeDtypeStruct((m, qw), act_dtype), jax.ShapeDtypeStruct((m, kw), F32),
                   jax.ShapeDtypeStruct((m, kw), F32), jax.ShapeDtypeStruct((m, qw), act_dtype)],
        compiler_params=pltpu.CompilerParams(
            dimension_semantics=("parallel",), vmem_limit_bytes=VMEM_LIMIT,
            allow_input_fusion=[False] * 4 + [True, False, True, False, True, False, False] + [False] * 2),
        name="mid",
    )(x, y, ga, ob, *weights, cos_t, sin_t)


def _rope_tables(pos):
    half = SWA_HEAD_DIM // 2
    inv_freq = ROPE_THETA ** (-np.arange(half, dtype=np.float64) / half)
    ang = np.asarray(pos, np.float64)[:, None] * inv_freq[None, :]
    cos, sin = np.cos(ang), np.sin(ang)
    cos_t = np.concatenate([cos, cos, cos, cos], axis=1).astype(np.float32)
    sin_t = np.concatenate([-sin, sin, -sin, sin], axis=1).astype(np.float32)
    return jnp.asarray(cos_t), jnp.asarray(sin_t)


def _attn_prompt_kernel(sink_ref, q_ref, kc_ref, kp_ref, vc_ref, vp_ref, gate_ref, h_ref, wout_ref,
                        y_ref, p_ref, o_ref, *, nqb):
    i = pl.program_id(1)
    w = SWA_WINDOW
    hd = SWA_HEAD_DIM
    npair = SWA_GROUP // 2
    lane2 = lax.broadcasted_iota(jnp.int32, (2 * w, LANES), 1)
    low = lane2 < hd
    rr = lax.broadcasted_iota(jnp.int32, (w, w), 0)
    cc = lax.broadcasted_iota(jnp.int32, (w, w), 1)
    tri = cc <= rr
    low_w = cc < hd
    r4 = lax.broadcasted_iota(jnp.int32, (4 * w, LANES), 0)
    c4 = lax.broadcasted_iota(jnp.int32, (4 * w, LANES), 1)
    den_cols = ((r4 < 2 * w) == (c4 < hd)).astype(BF16)
    units = [(jb, kv) for jb in range(nqb) for kv in range(SWA_KV_HEADS)]

    def scores(jb, kv):
        rows = slice(jb * w, (jb + 1) * w)
        if jb == 0:
            k_prev, v_prev = kp_ref[...], vp_ref[...]
        else:
            prev_rows = slice((jb - 1) * w, jb * w)
            k_prev, v_prev = kc_ref[prev_rows, :], vc_ref[prev_rows, :]
        kcat = jnp.concatenate([k_prev, kc_ref[rows, :]], axis=0)
        vcat = jnp.concatenate([v_prev, vc_ref[rows, :]], axis=0)
        own = low if kv == 0 else jnp.logical_not(low)
        k_own = jnp.where(own, kcat, 0.0)
        v_own = jnp.where(own, vcat, 0.0)
        k_oth = pltpu.roll(k_own, hd, 1)
        v_oth = pltpu.roll(v_own, hd, 1)
        k_lo, k_hi = (k_own, k_oth) if kv == 0 else (k_oth, k_own)
        v_lo, v_hi = (v_own, v_oth) if kv == 0 else (v_oth, v_own)
        k_rhs = jnp.concatenate([k_lo, k_hi], axis=0).astype(BF16)
        v_rhs = jnp.concatenate([jnp.concatenate([v_lo, v_hi], axis=0).astype(BF16), den_cols], axis=1)
        qs = jnp.concatenate([q_ref[rows, (kv * npair + pr) * LANES:(kv * npair + pr + 1) * LANES]
                              for pr in range(npair)], axis=0)
        return _dot_nt(qs, k_rhs), v_rhs

    def softmax_pv(kv, first, buf, s_all, v_rhs):
        sink_terms = []
        for pr in range(npair):
            prow = slice(pr * w, (pr + 1) * w)
            pair_terms = []
            for e in range(2):
                s_prev = s_all[prow, (2 * e) * w:(2 * e + 1) * w]
                s_cur = s_all[prow, (2 * e + 1) * w:(2 * e + 2) * w]
                if first:
                    s_prev = jnp.where(i > 0, s_prev, -jnp.inf)
                sc = jnp.where(tri, s_cur, s_prev)
                sink = sink_ref[2 * (kv * npair + pr) + e] * LOG2E
                mx = jnp.maximum(jnp.max(sc, axis=-1, keepdims=True), sink)
                pe = jnp.exp2(sc - mx)
                p_ref[buf, prow, (2 * e) * w:(2 * e + 1) * w] = jnp.where(tri, 0.0, pe).astype(BF16)
                p_ref[buf, prow, (2 * e + 1) * w:(2 * e + 2) * w] = jnp.where(tri, pe, 0.0).astype(BF16)
                pair_terms.append(jnp.exp2(sink - mx))
            sink_terms.append(pair_terms)
        return _dot(p_ref[buf], v_rhs), sink_terms

    def normalise(jb, kv, o_ext, sink_terms):
        rows = slice(jb * w, (jb + 1) * w)
        for pr in range(npair):
            prow = slice(pr * w, (pr + 1) * w)
            cols = slice((kv * npair + pr) * LANES, (kv * npair + pr + 1) * LANES)
            st = jnp.where(low_w, sink_terms[pr][0], sink_terms[pr][1])
            o = o_ext[prow, :LANES] / (o_ext[prow, LANES:] + st)
            o_ref[rows, cols] = o.astype(o_ref.dtype)

    def project(rows):
        og = o_ref[rows, :].astype(F32) * gate_ref[rows, :].astype(F32)
        y_ref[rows, :] = h_ref[rows, :] + _dot(og.astype(BF16), wout_ref[...])

    group = 2
    pending = scores(*units[0])
    unfinished = None
    for n, (jb, kv) in enumerate(units):
        current = pending
        if n + 1 < len(units):
            pending = scores(*units[n + 1])
        result = softmax_pv(kv, jb == 0, n % 2, *current)
        if unfinished is not None:
            normalise(*unfinished)
            done_jb, done_kv = unfinished[:2]
            if done_kv == SWA_KV_HEADS - 1 and (done_jb + 1) % group == 0:
                project(slice((done_jb + 1 - group) * w, (done_jb + 1) * w))
        unfinished = (jb, kv) + result
    normalise(*unfinished)
    project(slice((nqb - group) * w, nqb * w))


def _attn_prompt(sinks, q, k, v, gate, h, w_out, nseq, seqlen):
    w = SWA_WINDOW
    nqb = 8
    tm = nqb * w
    assert seqlen % tm == 0
    nblk = seqlen // tm
    qw = q.shape[1]
    dm = h.shape[1]
    row = lambda n: pl.BlockSpec((tm, n), lambda b, i: (b * nblk + i, 0))
    prev = lambda n: pl.BlockSpec((w, n), lambda b, i: (jnp.maximum((b * nblk + i) * nqb - 1, 0), 0))
    smem = pl.BlockSpec(memory_space=pltpu.SMEM)
    return pl.pallas_call(
        functools.partial(_attn_prompt_kernel, nqb=nqb),
        grid=(nseq, nblk),
        in_specs=[smem, row(qw), row(LANES), prev(LANES), row(LANES), prev(LANES), row(qw), row(dm),
                  _full(w_out.shape)],
        out_specs=row(dm),
        out_shape=jax.ShapeDtypeStruct(h.shape, F32),
        scratch_shapes=[pltpu.VMEM((2, 4 * w, 4 * w), BF16), pltpu.VMEM((tm, qw), BF16)],
        compiler_params=_cparams(("parallel", "arbitrary")),
        name="attn_prompt",
    )(sinks, q, k, k, v, v, gate, h, w_out)


def _attn_sample_kernel(sink_ref, q_ref, kn_ref, vn_ref, kc_ref, vc_ref, o_ref, ko_ref, vo_ref, q2_ref, *, seqlen):
    nq = SWA_HEADS * seqlen
    hd = SWA_HEAD_DIM
    heads = [(kv, g) for kv in range(SWA_KV_HEADS) for g in range(SWA_GROUP)]

    def stacked(kv, g):
        r0 = (kv * SWA_GROUP + g) * seqlen
        return slice(r0, r0 + seqlen), slice(kv * hd, (kv + 1) * hd)

    q2_ref[...] = jnp.zeros(q2_ref.shape, q2_ref.dtype)
    for kv, g in heads:
        rows, lanes = stacked(kv, g)
        h = kv * SWA_GROUP + g
        q2_ref[:, rows, lanes] = q_ref[:, :, h * hd:(h + 1) * hd]
    ncache = kc_ref.shape[1]
    t_row = lax.broadcasted_iota(jnp.int32, (nq, ncache), 0) % seqlen
    c_col = lax.broadcasted_iota(jnp.int32, (nq, ncache), 1)
    cache_ok = c_col > t_row - (SWA_WINDOW - ncache)
    nnew = BF16_ROWS
    t_row_n = lax.broadcasted_iota(jnp.int32, (nq, nnew), 0) % seqlen
    n_col = lax.broadcasted_iota(jnp.int32, (nq, nnew), 1)
    new_ok = n_col <= t_row_n
    sink = (sink_ref[...] * LOG2E)[None, :, 0:1]
    bqk = lambda a, b: lax.dot_general(a, b, (((2,), (2,)), ((0,), (0,))), preferred_element_type=F32)
    bpv = lambda a, b: lax.dot_general(a, b, (((2,), (1,)), ((0,), (0,))), preferred_element_type=F32)
    q = q2_ref[...].astype(BF16)
    kc = kc_ref[...]
    vc = vc_ref[...]
    pad = jnp.zeros((kn_ref.shape[0], nnew - seqlen, kn_ref.shape[2]), F32)
    kn = jnp.concatenate([kn_ref[...], pad], axis=1)
    vn = jnp.concatenate([vn_ref[...], pad], axis=1)
    sc = jnp.where(cache_ok[None], bqk(q, kc.astype(BF16)), -jnp.inf)
    sn = jnp.where(new_ok[None], bqk(q, kn.astype(BF16)), -jnp.inf)
    mx = jnp.maximum(jnp.maximum(jnp.max(sc, axis=-1, keepdims=True), jnp.max(sn, axis=-1, keepdims=True)), sink)
    pc = jnp.exp2(sc - mx)
    pn = jnp.exp2(sn - mx)
    den = jnp.sum(pc, axis=-1, keepdims=True) + jnp.sum(pn, axis=-1, keepdims=True) + jnp.exp2(sink - mx)
    inv = 1.0 / den
    o2 = bpv((pc * inv).astype(BF16), vc.astype(BF16)) + bpv((pn * inv).astype(BF16), vn.astype(BF16))
    for kv, g in heads:
        rows, lanes = stacked(kv, g)
        h = kv * SWA_GROUP + g
        o_ref[:, :, h * hd:(h + 1) * hd] = o2[:, rows, lanes]
    keep = ncache - seqlen
    ko_ref[:, 0:keep, :] = kc[:, seqlen:ncache, :]
    ko_ref[:, keep:ncache, :] = kn[:, 0:seqlen, :]
    vo_ref[:, 0:keep, :] = vc[:, seqlen:ncache, :]
    vo_ref[:, keep:ncache, :] = vn[:, 0:seqlen, :]


def _attn_sample(sink_rows, q, kn, vn, kc, vc):
    n, seqlen, _ = q.shape
    nseq = 16
    assert n % nseq == 0
    blk = lambda a: pl.BlockSpec((nseq,) + a.shape[1:], lambda i: (i, 0, 0))
    return pl.pallas_call(
        functools.partial(_attn_sample_kernel, seqlen=seqlen),
        grid=(n // nseq,),
        in_specs=[_full(sink_rows.shape), blk(q), blk(kn), blk(vn), blk(kc), blk(vc)],
        out_specs=[blk(q), blk(kc), blk(vc)],
        out_shape=[jax.ShapeDtypeStruct(q.shape, F32), jax.ShapeDtypeStruct(kc.shape, F32),
                   jax.ShapeDtypeStruct(vc.shape, F32)],
        scratch_shapes=[pltpu.VMEM((nseq, SWA_HEADS * seqlen, SWA_KV_HEADS * SWA_HEAD_DIM), F32)],
        compiler_params=_cparams(("parallel",)),
        name="attn_sample",
    )(sink_rows, q, kn, vn, kc, vc)


def _odd_out_kernel(h_ref, o_ref, gate_ref, w_ref, y_ref):
    og = o_ref[...].astype(F32) * gate_ref[...].astype(F32)
    y_ref[...] = h_ref[...] + _dot(og.astype(BF16), w_ref[...])


def _odd_out(h, o, gate, w):
    m, dm = h.shape
    tm = _row_tile(m)
    row = lambda n: pl.BlockSpec((tm, n), lambda i: (i, 0))
    return pl.pallas_call(
        _odd_out_kernel,
        grid=(m // tm,),
        in_specs=[row(dm), row(o.shape[1]), row(gate.shape[1]), _full(w.shape)],
        out_specs=row(dm),
        out_shape=jax.ShapeDtypeStruct((m, dm), F32),
        compiler_params=_cparams(("parallel",)),
        name="odd_out",
    )(h, o, gate, w)


def _trunk(x, s5_x0, gla_s0, pe, po, pos, nseq, seqlen, act_dtype):
    u2, ga, q, k, v, la, gb = _even_in(x, pe['norm_g'], pe['w_in'], pe['wgate'], pe['bgate'], act_dtype,
                                       _s5_chunk(seqlen))
    y4, s5_fin = _s5_branch(u2, s5_x0, pe['s5_ops'], pe['d'], nseq, seqlen, act_dtype)
    ob, gla_fin = _gla(q, k, v, la, gb, gla_s0, pe['gla_norm_g'], nseq, seqlen, act_dtype)
    cos_t, sin_t = _rope_tables(pos)
    h, q1, k1, v1, gate = _mid(x, y4, ga, ob, pe, po, cos_t, sin_t, act_dtype)
    return h, q1, k1, v1, gate, s5_fin, gla_fin


def kernel(x_prompt, x_sample, state_s5_re, state_s5_im, state_gla, cache_swa_k, cache_swa_v,
           even_norm_g, even_w_in, s5_lambda_re, s5_lambda_im, s5_log_dt, s5_b_re, s5_b_im,
           s5_c_re, s5_c_im, s5_d, s5_w_glu, s5_b_glu, gla_w_gate, gla_b_gate, gla_norm_g,
           even_w_out, odd_norm_g, odd_w_in, swa_q_norm_g, swa_k_norm_g, swa_sinks, odd_w_out):
    nb, seq, dm = x_prompt.shape
    ns, dseq, _ = x_sample.shape
    ng = s5_lambda_re.shape[1]
    xp = x_prompt.reshape(nb * seq, dm)
    xs = x_sample.reshape(ns * dseq, dm)

    i = 0
    assert ng * S5_GROUP == S5_WIDTH
    pad_rank = LANES - GLA_GATE_RANK
    pe = {
        'norm_g': even_norm_g[i][None, :],
        'w_in': even_w_in[i].astype(BF16),
        'wgate': jnp.pad(gla_w_gate[i], ((0, pad_rank), (0, 0))).astype(BF16),
        'bgate': gla_b_gate[i][None, :],
        's5_ops': _s5_params(s5_lambda_re[i], s5_lambda_im[i], s5_log_dt[i], s5_b_re[i], s5_b_im[i],
                             s5_c_re[i], s5_c_im[i]),
        'gla_norm_g': gla_norm_g[i][None, :],
        'd': s5_d[i],
        'wglu': s5_w_glu[i].astype(BF16),
        'bglu': s5_b_glu[i][None, :],
        'wo': even_w_out[i].astype(BF16),
    }
    po = {
        'norm_g': odd_norm_g[i][None, :],
        'w_in': odd_w_in[i].astype(BF16),
        'gq': jnp.tile(swa_q_norm_g[i], MXU_TILE // SWA_HEAD_DIM)[None, :],
        'gk': jnp.tile(swa_k_norm_g[i], LANES // SWA_HEAD_DIM)[None, :],
    }
    w_out = odd_w_out[i].astype(BF16)
    sinks = swa_sinks[i]
    kvw = SWA_KV_HEADS * SWA_HEAD_DIM

    s5_zero = jnp.zeros((nb, ng, S5_STATE), F32)
    gla_zero = jnp.zeros((nb, GLA_HEADS, GLA_HEAD_K, GLA_HEAD_V), F32)
    hp, q, k, v, gate, (s5r_p, s5i_p), gla_p = _trunk(xp, (s5_zero, s5_zero), gla_zero, pe, po, np.arange(seq),
                                                      nb, seq, BF16)
    y_prompt = _attn_prompt(sinks, q, k, v, gate, hp, w_out, nb, seq).reshape(nb, seq, dm)
    cache_len = min(SWA_WINDOW, seq)
    tail = lambda a: (a.reshape(nb, seq, kvw)[:, seq - cache_len:]
                      .reshape(1, nb, cache_len, SWA_KV_HEADS, SWA_HEAD_DIM))
    swk_p, swv_p = tail(k), tail(v)

    pos_s = np.tile(PAST_LEN + np.arange(dseq), ns)
    hs, q, k, v, gate, (s5r_s, s5i_s), gla_s = _trunk(xs, (state_s5_re[i], state_s5_im[i]), state_gla[i], pe, po,
                                                      pos_s, ns, dseq, F32)
    ncache = cache_swa_k.shape[2]
    sink_rows = jnp.broadcast_to(jnp.repeat(sinks, dseq)[:, None], (SWA_HEADS * dseq, LANES))
    kn = k.reshape(ns, dseq, kvw)
    vn = v.reshape(ns, dseq, kvw)
    kc = cache_swa_k[i].reshape(ns, ncache, kvw)
    vc = cache_swa_v[i].reshape(ns, ncache, kvw)
    o, kc_new, vc_new = _attn_sample(sink_rows, q.reshape(ns, dseq, -1), kn, vn, kc, vc)
    y_sample = _odd_out(hs, o.reshape(ns * dseq, -1), gate, w_out).reshape(ns, dseq, dm)
    swk_s = kc_new.reshape(1, ns, ncache, SWA_KV_HEADS, SWA_HEAD_DIM)
    swv_s = vc_new.reshape(1, ns, ncache, SWA_KV_HEADS, SWA_HEAD_DIM)

    return (y_prompt, y_sample,
            s5r_p, s5i_p, gla_p[None], swk_p, swv_p,
            s5r_s, s5i_s, gla_s[None], swk_s, swv_s)
```
